```python
import jax
import jax.numpy as jnp
from jax import lax
import numpy as np

D_MODEL = 1024
BATCH = 8
SEQ = 16384
DEPTH = 4

N_MIXERS = 3
ROPE_THETA = 500000.0
EPS = 1e-6
BLOCK = 128

A_HEADS = 16
A_KV_HEADS = 4
A_HEAD_DIM = D_MODEL // A_HEADS
A_ROT_DIM = A_HEAD_DIM // 4
A_WINDOW = 128

B_CONV_WIDTH = 3

C_HEADS = 16
C_NOPE_DIM = 64
C_ROPE_DIM = 32
C_V_DIM = 64
C_Q_RANK = 384
C_KV_RANK = 256

D_FF = ((8 * D_MODEL + 767) // 768) * 256

N_LAYERS_A = (DEPTH + N_MIXERS - 1) // N_MIXERS
N_LAYERS_B = (DEPTH + N_MIXERS - 2) // N_MIXERS
N_LAYERS_C = DEPTH // N_MIXERS

kernel_name = 'hybrid_swa_sink_shortconv_mla_swiglu'


def rms_norm(x, g):
    xf = x.astype(jnp.float32)
    y = xf * lax.rsqrt(jnp.mean(xf * xf, axis=-1, keepdims=True) + EPS)
    return (y * g.astype(jnp.float32)).astype(x.dtype)


def rotate(x, pos):
    r = x.shape[-1]
    inv_freq = ROPE_THETA ** (-jnp.arange(0, r, 2, dtype=jnp.float32) / r)
    ang = pos.astype(jnp.float32)[:, :, None] * inv_freq
    cos = jnp.cos(ang)[:, :, None, :]
    sin = jnp.sin(ang)[:, :, None, :]
    x1, x2 = jnp.split(x.astype(jnp.float32), 2, axis=-1)
    out = jnp.concatenate([x1 * cos - x2 * sin, x2 * cos + x1 * sin], axis=-1)
    return out.astype(x.dtype)


def swa_sink_attention(h, pos, w_qkv, q_norm, k_norm, sinks, w_o):
    b, s, _ = h.shape
    hq, hkv, hd = A_HEADS, A_KV_HEADS, A_HEAD_DIM
    grp = hq // hkv
    nblk = s // BLOCK
    q, k, v = jnp.split(h @ w_qkv, [hq * hd, (hq + hkv) * hd], axis=-1)
    q = rms_norm(q.reshape(b, s, hq, hd), q_norm)
    k = rms_norm(k.reshape(b, s, hkv, hd), k_norm)
    v = v.reshape(b, s, hkv, hd)
    q = jnp.concatenate([rotate(q[..., :A_ROT_DIM], pos), q[..., A_ROT_DIM:]], axis=-1)
    k = jnp.concatenate([rotate(k[..., :A_ROT_DIM], pos), k[..., A_ROT_DIM:]], axis=-1)
    qb = q.reshape(b, nblk, BLOCK, hkv, grp, hd)
    kb = k.reshape(b, nblk, BLOCK, hkv, hd)
    vb = v.reshape(b, nblk, BLOCK, hkv, hd)
    pad = ((0, 0), (1, 0), (0, 0), (0, 0), (0, 0))
    kw = jnp.concatenate([jnp.pad(kb, pad)[:, :-1], kb], axis=2)
    vw = jnp.concatenate([jnp.pad(vb, pad)[:, :-1], vb], axis=2)
    scores = jnp.einsum('bnqkgd,bnjkd->bnkgqj', qb, kw).astype(jnp.float32) * (hd ** -0.5)
    qi = jnp.arange(BLOCK)[:, None]
    kj = jnp.arange(2 * BLOCK)[None, :]
    delta = qi + BLOCK - kj
    band = (delta >= 0) & (delta < A_WINDOW)
    has_prev = (jnp.arange(nblk) > 0)[:, None, None] | (kj >= BLOCK)[None]
    valid = band[None] & has_prev
    scores = jnp.where(valid[None, :, None, None], scores, -jnp.inf)
    sink = sinks.astype(jnp.float32).reshape(hkv, grp)[None, None, :, :, None, None]
    m = jnp.maximum(jnp.max(scores, axis=-1, keepdims=True), sink)
    p = jnp.exp(scores - m)
    p = p / (jnp.sum(p, axis=-1, keepdims=True) + jnp.exp(sink - m))
    o = jnp.einsum('bnkgqj,bnjkd->bnqkgd', p.astype(vw.dtype), vw)
    return o.reshape(b, s, hq * hd) @ w_o


def short_conv_mixer(h, w_in, conv_w, w_out):
    s = h.shape[1]
    b_gate, c_gate, u = jnp.split(h @ w_in, 3, axis=-1)
    z = c_gate * u
    zp = jnp.pad(z, ((0, 0), (B_CONV_WIDTH - 1, 0), (0, 0)))
    y = sum(conv_w[i] * zp[:, i:i + s] for i in range(B_CONV_WIDTH))
    return (b_gate * y) @ w_out


def mla_attention(h, pos, w_down, q_a_norm, kv_a_norm, w_q_up, w_kv_up, q_norm, k_norm, w_o):
    b, s, _ = h.shape
    nh = C_HEADS
    dqk = C_NOPE_DIM + C_ROPE_DIM
    nblk = s // BLOCK
    cq, ckv, k_rope = jnp.split(h @ w_down, [C_Q_RANK, C_Q_RANK + C_KV_RANK], axis=-1)
    cq = rms_norm(cq, q_a_norm)
    ckv = rms_norm(ckv, kv_a_norm)
    q = (cq @ w_q_up).reshape(b, s, nh, dqk)
    k_nope, v = jnp.split((ckv @ w_kv_up).reshape(b, s, nh, C_NOPE_DIM + C_V_DIM), [C_NOPE_DIM], axis=-1)
    k = jnp.concatenate([k_nope, jnp.broadcast_to(k_rope[:, :, None, :], (b, s, nh, C_ROPE_DIM))], axis=-1)
    q = rms_norm(q, q_norm)
    k = rms_norm(k, k_norm)
    q = jnp.concatenate([q[..., :C_NOPE_DIM], rotate(q[..., C_NOPE_DIM:], pos)], axis=-1)
    k = jnp.concatenate([k[..., :C_NOPE_DIM], rotate(k[..., C_NOPE_DIM:], pos)], axis=-1)
    scale = dqk ** -0.5
    kpos = jnp.arange(s)
    qblocks = jnp.moveaxis(q.reshape(b, nblk, BLOCK, nh, dqk), 1, 0)

    def block_attn(args):
        qblk, j = args
        sc = jnp.einsum('bqhd,bkhd->bhqk', qblk, k).astype(jnp.float32) * scale
        qpos = j * BLOCK + jnp.arange(BLOCK)
        sc = jnp.where(kpos[None, :] <= qpos[:, None], sc, -jnp.inf)
        p = jax.nn.softmax(sc, axis=-1)
        return jnp.einsum('bhqk,bkhd->bqhd', p.astype(v.dtype), v)

    o = lax.map(block_attn, (qblocks, jnp.arange(nblk)))
    o = jnp.moveaxis(o, 0, 1).reshape(b, s, nh * C_V_DIM)
    return o @ w_o


def swiglu_ffn(h, w_gate_up, w_down):
    gate, up = jnp.split(h @ w_gate_up, 2, axis=-1)
    return (jax.nn.silu(gate) * up) @ w_down


def _normal(key, shape, scale):
    return jax.random.normal(key, shape, jnp.float32) * scale


def _fwd_setup_inputs(seed: int = 0) -> dict:
    key = jax.random.key(seed)
    ks = jax.random.split(key, 24)
    d = D_MODEL
    out_scale = (2 * DEPTH) ** -0.5
    qkv_width = (A_HEADS + 2 * A_KV_HEADS) * A_HEAD_DIM
    dqk = C_NOPE_DIM + C_ROPE_DIM
    return {
        'x': _normal(ks[0], (BATCH, SEQ, d), 1.0),
        'positions': jax.random.randint(ks[1], (BATCH, 1), 0, 4096, dtype=jnp.int32) + jnp.arange(SEQ, dtype=jnp.int32)[None, :],
        'mix_norm': 1.0 + _normal(ks[2], (DEPTH, d), 0.02),
        'ffn_norm': 1.0 + _normal(ks[3], (DEPTH, d), 0.02),
        'a_w_qkv': _normal(ks[4], (N_LAYERS_A, d, qkv_width), d ** -0.5),
        'a_q_norm': 1.0 + _normal(ks[5], (N_LAYERS_A, A_HEAD_DIM), 0.02),
        'a_k_norm': 1.0 + _normal(ks[6], (N_LAYERS_A, A_HEAD_DIM), 0.02),
        'a_sinks': _normal(ks[7], (N_LAYERS_A, A_HEADS), 0.5),
        'a_w_o': _normal(ks[8], (N_LAYERS_A, A_HEADS * A_HEAD_DIM, d), (A_HEADS * A_HEAD_DIM) ** -0.5 * out_scale),
        'b_w_in': _normal(ks[9], (N_LAYERS_B, d, 3 * d), d ** -0.5),
        'b_conv_w': _normal(ks[10], (N_LAYERS_B, B_CONV_WIDTH, d), B_CONV_WIDTH ** -0.5),
        'b_w_out': _normal(ks[11], (N_LAYERS_B, d, d), d ** -0.5 * out_scale),
        'c_w_down': _normal(ks[12], (N_LAYERS_C, d, C_Q_RANK + C_KV_RANK + C_ROPE_DIM), d ** -0.5),
        'c_q_a_norm': 1.0 + _normal(ks[13], (N_LAYERS_C, C_Q_RANK), 0.02),
        'c_kv_a_norm': 1.0 + _normal(ks[14], (N_LAYERS_C, C_KV_RANK), 0.02),
        'c_w_q_up': _normal(ks[15], (N_LAYERS_C, C_Q_RANK, C_HEADS * dqk), C_Q_RANK ** -0.5),
        'c_w_kv_up': _normal(ks[16], (N_LAYERS_C, C_KV_RANK, C_HEADS * (C_NOPE_DIM + C_V_DIM)), C_KV_RANK ** -0.5),
        'c_q_norm': 1.0 + _normal(ks[17], (N_LAYERS_C, dqk), 0.02),
        'c_k_norm': 1.0 + _normal(ks[18], (N_LAYERS_C, dqk), 0.02),
        'c_w_o': _normal(ks[19], (N_LAYERS_C, C_HEADS * C_V_DIM, d), (C_HEADS * C_V_DIM) ** -0.5 * out_scale),
        'f_w_gate_up': _normal(ks[20], (DEPTH, d, 2 * D_FF), d ** -0.5),
        'f_w_down': _normal(ks[21], (DEPTH, D_FF, d), D_FF ** -0.5 * out_scale),
    }


def _fwd_reference(x, positions, mix_norm, ffn_norm, a_w_qkv, a_q_norm, a_k_norm, a_sinks, a_w_o, b_w_in, b_conv_w, b_w_out, c_w_down, c_q_a_norm, c_kv_a_norm, c_w_q_up, c_w_kv_up, c_q_norm, c_k_norm, c_w_o, f_w_gate_up, f_w_down):
    for i in range(DEPTH):
        kind = i % N_MIXERS
        j = i // N_MIXERS
        h = rms_norm(x, mix_norm[i])
        if kind == 0:
            y = swa_sink_attention(h, positions, a_w_qkv[j], a_q_norm[j], a_k_norm[j], a_sinks[j], a_w_o[j])
        elif kind == 1:
            y = short_conv_mixer(h, b_w_in[j], b_conv_w[j], b_w_out[j])
        else:
            y = mla_attention(h, positions, c_w_down[j], c_q_a_norm[j], c_kv_a_norm[j], c_w_q_up[j], c_w_kv_up[j], c_q_norm[j], c_k_norm[j], c_w_o[j])
        x = x + y
        h = rms_norm(x, ffn_norm[i])
        x = x + swiglu_ffn(h, f_w_gate_up[i], f_w_down[i])
    return x


import jax as _jax
import jax.numpy as _jnp

TWIN_FORMAT = 'train_step'
FWD_PARAMS = ['x', 'positions', 'mix_norm', 'ffn_norm', 'a_w_qkv', 'a_q_norm', 'a_k_norm', 'a_sinks', 'a_w_o', 'b_w_in', 'b_conv_w', 'b_w_out', 'c_w_down', 'c_q_a_norm', 'c_kv_a_norm', 'c_w_q_up', 'c_w_kv_up', 'c_q_norm', 'c_k_norm', 'c_w_o', 'f_w_gate_up', 'f_w_down']
TWIN_WEIGHTS = ['mix_norm', 'ffn_norm', 'a_w_qkv', 'a_q_norm', 'a_k_norm', 'a_sinks', 'a_w_o', 'b_w_in', 'b_conv_w', 'b_w_out', 'c_w_down', 'c_q_a_norm', 'c_kv_a_norm', 'c_w_q_up', 'c_w_kv_up', 'c_q_norm', 'c_k_norm', 'c_w_o', 'f_w_gate_up', 'f_w_down']
TWIN_DIFF_INPUT = 'x'
TWIN_INPUTS = ['x', 'positions', 'mix_norm', 'ffn_norm', 'a_w_qkv', 'a_q_norm', 'a_k_norm', 'a_sinks', 'a_w_o', 'b_w_in', 'b_conv_w', 'b_w_out', 'c_w_down', 'c_q_a_norm', 'c_kv_a_norm', 'c_w_q_up', 'c_w_kv_up', 'c_q_norm', 'c_k_norm', 'c_w_o', 'f_w_gate_up', 'f_w_down', 'loss_target', 'm_mix_norm', 'm_ffn_norm', 'm_a_w_qkv', 'm_a_q_norm', 'm_a_k_norm', 'm_a_sinks', 'm_a_w_o', 'm_b_w_in', 'm_b_conv_w', 'm_b_w_out', 'm_c_w_down', 'm_c_q_a_norm', 'm_c_kv_a_norm', 'm_c_w_q_up', 'm_c_w_kv_up', 'm_c_q_norm', 'm_c_k_norm', 'm_c_w_o', 'm_f_w_gate_up', 'm_f_w_down', 'v_mix_norm', 'v_ffn_norm', 'v_a_w_qkv', 'v_a_q_norm', 'v_a_k_norm', 'v_a_sinks', 'v_a_w_o', 'v_b_w_in', 'v_b_conv_w', 'v_b_w_out', 'v_c_w_down', 'v_c_q_a_norm', 'v_c_kv_a_norm', 'v_c_w_q_up', 'v_c_w_kv_up', 'v_c_q_norm', 'v_c_k_norm', 'v_c_w_o', 'v_f_w_gate_up', 'v_f_w_down']
TWIN_OUTPUTS = ['loss', 'grad_x', 'grad_mix_norm', 'grad_ffn_norm', 'grad_a_w_qkv', 'grad_a_q_norm', 'grad_a_k_norm', 'grad_a_sinks', 'grad_a_w_o', 'grad_b_w_in', 'grad_b_conv_w', 'grad_b_w_out', 'grad_c_w_down', 'grad_c_q_a_norm', 'grad_c_kv_a_norm', 'grad_c_w_q_up', 'grad_c_w_kv_up', 'grad_c_q_norm', 'grad_c_k_norm', 'grad_c_w_o', 'grad_f_w_gate_up', 'grad_f_w_down', 'delta_mix_norm', 'delta_ffn_norm', 'delta_a_w_qkv', 'delta_a_q_norm', 'delta_a_k_norm', 'delta_a_sinks', 'delta_a_w_o', 'delta_b_w_in', 'delta_b_conv_w', 'delta_b_w_out', 'delta_c_w_down', 'delta_c_q_a_norm', 'delta_c_kv_a_norm', 'delta_c_w_q_up', 'delta_c_w_kv_up', 'delta_c_q_norm', 'delta_c_k_norm', 'delta_c_w_o', 'delta_f_w_gate_up', 'delta_f_w_down', 'new_m_mix_norm', 'new_m_ffn_norm', 'new_m_a_w_qkv', 'new_m_a_q_norm', 'new_m_a_k_norm', 'new_m_a_sinks', 'new_m_a_w_o', 'new_m_b_w_in', 'new_m_b_conv_w', 'new_m_b_w_out', 'new_m_c_w_down', 'new_m_c_q_a_norm', 'new_m_c_kv_a_norm', 'new_m_c_w_q_up', 'new_m_c_w_kv_up', 'new_m_c_q_norm', 'new_m_c_k_norm', 'new_m_c_w_o', 'new_m_f_w_gate_up', 'new_m_f_w_down', 'new_v_mix_norm', 'new_v_ffn_norm', 'new_v_a_w_qkv', 'new_v_a_q_norm', 'new_v_a_k_norm', 'new_v_a_sinks', 'new_v_a_w_o', 'new_v_b_w_in', 'new_v_b_conv_w', 'new_v_b_w_out', 'new_v_c_w_down', 'new_v_c_q_a_norm', 'new_v_c_kv_a_norm', 'new_v_c_w_q_up', 'new_v_c_w_kv_up', 'new_v_c_q_norm', 'new_v_c_k_norm', 'new_v_c_w_o', 'new_v_f_w_gate_up', 'new_v_f_w_down']
TWIN_LEAF_KINDS = {'loss': 'loss', 'grad_x': 'grad_x', 'grad_mix_norm': 'grad_w', 'grad_ffn_norm': 'grad_w', 'grad_a_w_qkv': 'grad_w', 'grad_a_q_norm': 'grad_w', 'grad_a_k_norm': 'grad_w', 'grad_a_sinks': 'grad_w', 'grad_a_w_o': 'grad_w', 'grad_b_w_in': 'grad_w', 'grad_b_conv_w': 'grad_w', 'grad_b_w_out': 'grad_w', 'grad_c_w_down': 'grad_w', 'grad_c_q_a_norm': 'grad_w', 'grad_c_kv_a_norm': 'grad_w', 'grad_c_w_q_up': 'grad_w', 'grad_c_w_kv_up': 'grad_w', 'grad_c_q_norm': 'grad_w', 'grad_c_k_norm': 'grad_w', 'grad_c_w_o': 'grad_w', 'grad_f_w_gate_up': 'grad_w', 'grad_f_w_down': 'grad_w', 'delta_mix_norm': 'delta_w', 'delta_ffn_norm': 'delta_w', 'delta_a_w_qkv': 'delta_w', 'delta_a_q_norm': 'delta_w', 'delta_a_k_norm': 'delta_w', 'delta_a_sinks': 'delta_w', 'delta_a_w_o': 'delta_w', 'delta_b_w_in': 'delta_w', 'delta_b_conv_w': 'delta_w', 'delta_b_w_out': 'delta_w', 'delta_c_w_down': 'delta_w', 'delta_c_q_a_norm': 'delta_w', 'delta_c_kv_a_norm': 'delta_w', 'delta_c_w_q_up': 'delta_w', 'delta_c_w_kv_up': 'delta_w', 'delta_c_q_norm': 'delta_w', 'delta_c_k_norm': 'delta_w', 'delta_c_w_o': 'delta_w', 'delta_f_w_gate_up': 'delta_w', 'delta_f_w_down': 'delta_w', 'new_m_mix_norm': 'new_m', 'new_m_ffn_norm': 'new_m', 'new_m_a_w_qkv': 'new_m', 'new_m_a_q_norm': 'new_m', 'new_m_a_k_norm': 'new_m', 'new_m_a_sinks': 'new_m', 'new_m_a_w_o': 'new_m', 'new_m_b_w_in': 'new_m', 'new_m_b_conv_w': 'new_m', 'new_m_b_w_out': 'new_m', 'new_m_c_w_down': 'new_m', 'new_m_c_q_a_norm': 'new_m', 'new_m_c_kv_a_norm': 'new_m', 'new_m_c_w_q_up': 'new_m', 'new_m_c_w_kv_up': 'new_m', 'new_m_c_q_norm': 'new_m', 'new_m_c_k_norm': 'new_m', 'new_m_c_w_o': 'new_m', 'new_m_f_w_gate_up': 'new_m', 'new_m_f_w_down': 'new_m', 'new_v_mix_norm': 'new_v', 'new_v_ffn_norm': 'new_v', 'new_v_a_w_qkv': 'new_v', 'new_v_a_q_norm': 'new_v', 'new_v_a_k_norm': 'new_v', 'new_v_a_sinks': 'new_v', 'new_v_a_w_o': 'new_v', 'new_v_b_w_in': 'new_v', 'new_v_b_conv_w': 'new_v', 'new_v_b_w_out': 'new_v', 'new_v_c_w_down': 'new_v', 'new_v_c_q_a_norm': 'new_v', 'new_v_c_kv_a_norm': 'new_v', 'new_v_c_w_q_up': 'new_v', 'new_v_c_w_kv_up': 'new_v', 'new_v_c_q_norm': 'new_v', 'new_v_c_k_norm': 'new_v', 'new_v_c_w_o': 'new_v', 'new_v_f_w_gate_up': 'new_v', 'new_v_f_w_down': 'new_v'}


def _forward(args):
    return _fwd_reference(*[args[k] for k in FWD_PARAMS])


def _output_shape():
    def fwd():
        inp = _fwd_setup_inputs(0)
        return _fwd_reference(*[inp[k] for k in FWD_PARAMS])
    out = _jax.eval_shape(fwd)
    return out.shape, out.dtype

N_MICROBATCH = 1
ADAM_LR = 0.001
ADAM_B1 = 0.9
ADAM_B2 = 0.999
ADAM_EPS = 1e-08
ADAM_WD = 0.01
ADAM_STEP = 10
PER_EXAMPLE_BATCH_AXIS = {'x': 0, 'positions': 0, 'loss_target': 0}
SHARED_INPUTS = []
_WEIGHT_DTYPES = {'mix_norm': _jnp.float32, 'ffn_norm': _jnp.float32, 'a_w_qkv': _jnp.float32, 'a_q_norm': _jnp.float32, 'a_k_norm': _jnp.float32, 'a_sinks': _jnp.float32, 'a_w_o': _jnp.float32, 'b_w_in': _jnp.float32, 'b_conv_w': _jnp.float32, 'b_w_out': _jnp.float32, 'c_w_down': _jnp.float32, 'c_q_a_norm': _jnp.float32, 'c_kv_a_norm': _jnp.float32, 'c_w_q_up': _jnp.float32, 'c_w_kv_up': _jnp.float32, 'c_q_norm': _jnp.float32, 'c_k_norm': _jnp.float32, 'c_w_o': _jnp.float32, 'f_w_gate_up': _jnp.float32, 'f_w_down': _jnp.float32}
MOMENT_SCALE = {'mix_norm': 2.358513e+01, 'ffn_norm': 1.226504e+01, 'a_w_qkv': 7.129027e-02, 'a_q_norm': 2.438737e+00, 'a_k_norm': 2.434180e+00, 'a_sinks': 2.985096e-01, 'a_w_o': 1.610341e-01, 'b_w_in': 3.965380e-01, 'b_conv_w': 9.124285e+00, 'b_w_out': 1.058413e+00, 'c_w_down': 9.811732e-02, 'c_q_a_norm': 6.481213e-02, 'c_kv_a_norm': 4.545815e-01, 'c_w_q_up': 3.266042e-02, 'c_w_kv_up': 4.732441e-02, 'c_q_norm': 5.967224e-01, 'c_k_norm': 5.953182e-01, 'c_w_o': 1.577787e-01, 'f_w_gate_up': 1.045385e-01, 'f_w_down': 5.667247e-01}


def _to_microbatches(a, axis):
    t = _jnp.moveaxis(a, axis, 0)
    t = t.reshape((N_MICROBATCH, t.shape[0] // N_MICROBATCH) + t.shape[1:])
    return _jnp.moveaxis(t, 1, axis + 1)


def setup_inputs(seed: int = 0) -> dict:
    inp = _fwd_setup_inputs(seed)
    key = _jax.random.fold_in(_jax.random.key(seed), 7919)
    shape, _ = _output_shape()
    out = dict(inp)
    out["loss_target"] = _jax.random.normal(_jax.random.fold_in(key, 0), shape, _jnp.float32)
    for i, name in enumerate(TWIN_WEIGHTS):
        w = inp[name].astype(_jnp.float32)
        if MOMENT_SCALE is None:
            s = _jnp.sqrt(_jnp.mean(_jnp.square(w)) + 1e-30)
        else:
            s = MOMENT_SCALE[name]
        km, kv = _jax.random.split(_jax.random.fold_in(key, i + 1))
        out[name] = w
        out["m_" + name] = s * _jax.random.normal(km, w.shape, _jnp.float32)
        out["v_" + name] = (s * s) * _jax.random.uniform(kv, w.shape, _jnp.float32, 0.5, 1.5)
    if N_MICROBATCH > 1:
        for name, axis in PER_EXAMPLE_BATCH_AXIS.items():
            out[name] = _to_microbatches(out[name], axis)
    return {'x': out['x'], 'positions': out['positions'], 'mix_norm': out['mix_norm'], 'ffn_norm': out['ffn_norm'], 'a_w_qkv': out['a_w_qkv'], 'a_q_norm': out['a_q_norm'], 'a_k_norm': out['a_k_norm'], 'a_sinks': out['a_sinks'], 'a_w_o': out['a_w_o'], 'b_w_in': out['b_w_in'], 'b_conv_w': out['b_conv_w'], 'b_w_out': out['b_w_out'], 'c_w_down': out['c_w_down'], 'c_q_a_norm': out['c_q_a_norm'], 'c_kv_a_norm': out['c_kv_a_norm'], 'c_w_q_up': out['c_w_q_up'], 'c_w_kv_up': out['c_w_kv_up'], 'c_q_norm': out['c_q_norm'], 'c_k_norm': out['c_k_norm'], 'c_w_o': out['c_w_o'], 'f_w_gate_up': out['f_w_gate_up'], 'f_w_down': out['f_w_down'], 'loss_target': out['loss_target'], 'm_mix_norm': out['m_mix_norm'], 'm_ffn_norm': out['m_ffn_norm'], 'm_a_w_qkv': out['m_a_w_qkv'], 'm_a_q_norm': out['m_a_q_norm'], 'm_a_k_norm': out['m_a_k_norm'], 'm_a_sinks': out['m_a_sinks'], 'm_a_w_o': out['m_a_w_o'], 'm_b_w_in': out['m_b_w_in'], 'm_b_conv_w': out['m_b_conv_w'], 'm_b_w_out': out['m_b_w_out'], 'm_c_w_down': out['m_c_w_down'], 'm_c_q_a_norm': out['m_c_q_a_norm'], 'm_c_kv_a_norm': out['m_c_kv_a_norm'], 'm_c_w_q_up': out['m_c_w_q_up'], 'm_c_w_kv_up': out['m_c_w_kv_up'], 'm_c_q_norm': out['m_c_q_norm'], 'm_c_k_norm': out['m_c_k_norm'], 'm_c_w_o': out['m_c_w_o'], 'm_f_w_gate_up': out['m_f_w_gate_up'], 'm_f_w_down': out['m_f_w_down'], 'v_mix_norm': out['v_mix_norm'], 'v_ffn_norm': out['v_ffn_norm'], 'v_a_w_qkv': out['v_a_w_qkv'], 'v_a_q_norm': out['v_a_q_norm'], 'v_a_k_norm': out['v_a_k_norm'], 'v_a_sinks': out['v_a_sinks'], 'v_a_w_o': out['v_a_w_o'], 'v_b_w_in': out['v_b_w_in'], 'v_b_conv_w': out['v_b_conv_w'], 'v_b_w_out': out['v_b_w_out'], 'v_c_w_down': out['v_c_w_down'], 'v_c_q_a_norm': out['v_c_q_a_norm'], 'v_c_kv_a_norm': out['v_c_kv_a_norm'], 'v_c_w_q_up': out['v_c_w_q_up'], 'v_c_w_kv_up': out['v_c_w_kv_up'], 'v_c_q_norm': out['v_c_q_norm'], 'v_c_k_norm': out['v_c_k_norm'], 'v_c_w_o': out['v_c_w_o'], 'v_f_w_gate_up': out['v_f_w_gate_up'], 'v_f_w_down': out['v_f_w_down']}


def _loss(weights, diff, rest, loss_target):
    with _jax.named_scope("forward"):
        args = {**rest, TWIN_DIFF_INPUT: diff, **{k: w.astype(_WEIGHT_DTYPES[k]) for k, w in weights.items()}}
        y = _forward(args)
    with _jax.named_scope("loss_head"):
        err = _jnp.square(y.astype(_jnp.float32) - loss_target)
        return 0.5 * _jnp.sum(_jnp.mean(err, axis=-1)) if err.ndim else 0.5 * err


def _adamw(w, g, m, v):
    m = ADAM_B1 * m + (1.0 - ADAM_B1) * g
    v = ADAM_B2 * v + (1.0 - ADAM_B2) * _jnp.square(g)
    m_hat = m / (1.0 - ADAM_B1 ** ADAM_STEP)
    v_hat = v / (1.0 - ADAM_B2 ** ADAM_STEP)
    delta = -ADAM_LR * (m_hat / (_jnp.sqrt(v_hat) + ADAM_EPS) + ADAM_WD * w)
    return delta, m, v


def reference(x, positions, mix_norm, ffn_norm, a_w_qkv, a_q_norm, a_k_norm, a_sinks, a_w_o, b_w_in, b_conv_w, b_w_out, c_w_down, c_q_a_norm, c_kv_a_norm, c_w_q_up, c_w_kv_up, c_q_norm, c_k_norm, c_w_o, f_w_gate_up, f_w_down, loss_target, m_mix_norm, m_ffn_norm, m_a_w_qkv, m_a_q_norm, m_a_k_norm, m_a_sinks, m_a_w_o, m_b_w_in, m_b_conv_w, m_b_w_out, m_c_w_down, m_c_q_a_norm, m_c_kv_a_norm, m_c_w_q_up, m_c_w_kv_up, m_c_q_norm, m_c_k_norm, m_c_w_o, m_f_w_gate_up, m_f_w_down, v_mix_norm, v_ffn_norm, v_a_w_qkv, v_a_q_norm, v_a_k_norm, v_a_sinks, v_a_w_o, v_b_w_in, v_b_conv_w, v_b_w_out, v_c_w_down, v_c_q_a_norm, v_c_kv_a_norm, v_c_w_q_up, v_c_w_kv_up, v_c_q_norm, v_c_k_norm, v_c_w_o, v_f_w_gate_up, v_f_w_down):
    given = dict(x=x, positions=positions, mix_norm=mix_norm, ffn_norm=ffn_norm, a_w_qkv=a_w_qkv, a_q_norm=a_q_norm, a_k_norm=a_k_norm, a_sinks=a_sinks, a_w_o=a_w_o, b_w_in=b_w_in, b_conv_w=b_conv_w, b_w_out=b_w_out, c_w_down=c_w_down, c_q_a_norm=c_q_a_norm, c_kv_a_norm=c_kv_a_norm, c_w_q_up=c_w_q_up, c_w_kv_up=c_w_kv_up, c_q_norm=c_q_norm, c_k_norm=c_k_norm, c_w_o=c_w_o, f_w_gate_up=f_w_gate_up, f_w_down=f_w_down, loss_target=loss_target, m_mix_norm=m_mix_norm, m_ffn_norm=m_ffn_norm, m_a_w_qkv=m_a_w_qkv, m_a_q_norm=m_a_q_norm, m_a_k_norm=m_a_k_norm, m_a_sinks=m_a_sinks, m_a_w_o=m_a_w_o, m_b_w_in=m_b_w_in, m_b_conv_w=m_b_conv_w, m_b_w_out=m_b_w_out, m_c_w_down=m_c_w_down, m_c_q_a_norm=m_c_q_a_norm, m_c_kv_a_norm=m_c_kv_a_norm, m_c_w_q_up=m_c_w_q_up, m_c_w_kv_up=m_c_w_kv_up, m_c_q_norm=m_c_q_norm, m_c_k_norm=m_c_k_norm, m_c_w_o=m_c_w_o, m_f_w_gate_up=m_f_w_gate_up, m_f_w_down=m_f_w_down, v_mix_norm=v_mix_norm, v_ffn_norm=v_ffn_norm, v_a_w_qkv=v_a_w_qkv, v_a_q_norm=v_a_q_norm, v_a_k_norm=v_a_k_norm, v_a_sinks=v_a_sinks, v_a_w_o=v_a_w_o, v_b_w_in=v_b_w_in, v_b_conv_w=v_b_conv_w, v_b_w_out=v_b_w_out, v_c_w_down=v_c_w_down, v_c_q_a_norm=v_c_q_a_norm, v_c_kv_a_norm=v_c_kv_a_norm, v_c_w_q_up=v_c_w_q_up, v_c_w_kv_up=v_c_w_kv_up, v_c_q_norm=v_c_q_norm, v_c_k_norm=v_c_k_norm, v_c_w_o=v_c_w_o, v_f_w_gate_up=v_f_w_gate_up, v_f_w_down=v_f_w_down)
    weights = {n: given[n] for n in TWIN_WEIGHTS}
    shared = {n: given[n] for n in SHARED_INPUTS}
    per_example = {n: given[n] for n in ['x', 'positions']}
    grad_fn = _jax.value_and_grad(_loss, argnums=(0, 1))

    def one_microbatch(ex, loss_target):
        ex = dict(ex)
        diff = ex.pop(TWIN_DIFF_INPUT)
        return grad_fn(weights, diff, {**shared, **ex}, loss_target)

    if N_MICROBATCH == 1:
        loss, (grad_w, grad_x) = one_microbatch(per_example, given["loss_target"])
    else:
        def body(carry, xs):
            loss_sum, grad_sum = carry
            l_k, (gw_k, gx_k) = one_microbatch(xs[0], xs[1])
            with _jax.named_scope("update"):
                return (loss_sum + l_k, _jax.tree.map(_jnp.add, grad_sum, gw_k)), gx_k

        init = (_jnp.zeros((), _jnp.float32), _jax.tree.map(_jnp.zeros_like, weights))
        (loss, grad_w), grad_x = _jax.lax.scan(body, init, (per_example, given["loss_target"]))
    with _jax.named_scope("update"):
        delta_w, new_m, new_v = {}, {}, {}
        for n in TWIN_WEIGHTS:
            delta_w[n], new_m[n], new_v[n] = _adamw(weights[n], grad_w[n], given["m_" + n], given["v_" + n])
    return (loss, grad_x, *[grad_w[n] for n in TWIN_WEIGHTS], *[delta_w[n] for n in TWIN_WEIGHTS],
            *[new_m[n] for n in TWIN_WEIGHTS], *[new_v[n] for n in TWIN_WEIGHTS])
```

```python
import functools

import jax
import jax.numpy as jnp
from jax import lax
from jax.experimental import pallas as pl
from jax.experimental.pallas import tpu as pltpu

F32 = jnp.float32
BF16 = jnp.bfloat16

N_DEV = 8
MESH_AXES = ("x", "y", "c")

DEPTH = 4
N_MIXERS = 3
ROPE_THETA = 500000.0
EPS = 1e-6
A_HEADS, A_KV_HEADS, A_HEAD_DIM, A_ROT_DIM, A_WINDOW = 16, 4, 64, 16, 128
A_GROUP = A_HEADS // A_KV_HEADS
C_HEADS, C_NOPE, C_ROPE, C_V, C_Q_RANK, C_KV_RANK = 16, 64, 32, 64, 384, 256
C_QK = C_NOPE + C_ROPE
ADAM_LR, ADAM_B1, ADAM_B2, ADAM_EPS, ADAM_WD, ADAM_STEP = 0.001, 0.9, 0.999, 1e-08, 0.01, 10

VMEM_LIMIT_BYTES = 48 * 1024 * 1024
LANES = 128
NEG = -1e30
MLA_BLOCK = 512
PACK_COLS = 1024
PACK_ROW_MULTIPLE = 128

WEIGHTS = ['mix_norm', 'ffn_norm', 'a_w_qkv', 'a_q_norm', 'a_k_norm', 'a_sinks', 'a_w_o', 'b_w_in', 'b_conv_w', 'b_w_out',
           'c_w_down', 'c_q_a_norm', 'c_kv_a_norm', 'c_w_q_up', 'c_w_kv_up', 'c_q_norm', 'c_k_norm', 'c_w_o', 'f_w_gate_up',
           'f_w_down']
SHARD_AXIS = {'a_w_qkv': 2, 'a_w_o': 1, 'b_w_in': 2, 'b_conv_w': 2, 'b_w_out': 1, 'c_w_down': 1, 'c_q_a_norm': 1,
              'c_kv_a_norm': 1, 'c_w_q_up': 2, 'c_w_kv_up': 2, 'c_w_o': 1, 'f_w_gate_up': 2, 'f_w_down': 1}
SHARDED = [n for n in WEIGHTS if n in SHARD_AXIS]
REPLICATED = [n for n in WEIGHTS if n not in SHARD_AXIS]
GATHER_F32 = ['b_conv_w', 'c_q_a_norm', 'c_kv_a_norm']
GATHER_BF16 = [n for n in SHARDED if n not in GATHER_F32]


def _params(semantics=None):
    return pltpu.CompilerParams(dimension_semantics=semantics, vmem_limit_bytes=VMEM_LIMIT_BYTES)


def _div_tile(n, cap, mult=LANES):
    best = None
    t = mult
    while t <= min(n, cap):
        if n % t == 0:
            best = t
        t += mult
    return n if best is None else best


def _exchange(src, scatter, name):
    block = src.shape[1:] if scatter else src.shape

    def body(src_ref, out_ref, send_sems, recv_sems, local_sem):
        x, y, c = lax.axis_index("x"), lax.axis_index("y"), lax.axis_index("c")
        me = 4 * x + 2 * y + c

        def piece(idx):
            return src_ref.at[idx] if scatter else src_ref

        local = pltpu.make_async_copy(piece(me), out_ref.at[me], local_sem)
        local.start()
        copies = []
        for r in range(1, N_DEV):
            px = 1 - x if (r >> 2) & 1 else x
            py = 1 - y if (r >> 1) & 1 else y
            pc = 1 - c if r & 1 else c
            cp = pltpu.make_async_remote_copy(
                src_ref=piece(4 * px + 2 * py + pc), dst_ref=out_ref.at[me],
                send_sem=send_sems.at[r - 1], recv_sem=recv_sems.at[r - 1],
                device_id=(px, py, pc), device_id_type=pl.DeviceIdType.MESH)
            cp.start()
            copies.append(cp)
        for cp in copies:
            cp.wait()
        local.wait()

    return pl.pallas_call(
        body, name=name,
        out_shape=jax.ShapeDtypeStruct((N_DEV,) + tuple(block), src.dtype),
        in_specs=[pl.BlockSpec(memory_space=pl.ANY)],
        out_specs=pl.BlockSpec(memory_space=pl.ANY),
        scratch_shapes=[pltpu.SemaphoreType.DMA((N_DEV - 1,)), pltpu.SemaphoreType.DMA((N_DEV - 1,)),
                        pltpu.SemaphoreType.DMA(())],
    )(src)


def _matmul(a, b, mode, name, out_dtype=F32, residual=None):
    if mode == 'nn':
        (m, k), (k2, n) = a.shape, b.shape
    elif mode == 'nt':
        (m, k), (n, k2) = a.shape, b.shape
    else:
        (k, m), (k2, n) = a.shape, b.shape
    assert k == k2, (name, a.shape, b.shape, mode)
    if mode == 'tn':
        tm, tk = _div_tile(m, 1408), _div_tile(k, 512, 16)
    else:
        tm, tk = _div_tile(m, 512, 16), _div_tile(k, 1536)
    tn = _div_tile(n, 1024)
    nk = k // tk
    dims = {'nn': (((1,), (0,)), ((), ())), 'nt': (((1,), (1,)), ((), ())), 'tn': (((0,), (0,)), ((), ()))}[mode]

    def body(a_ref, b_ref, *rest):
        if residual is None:
            o_ref, acc = rest
        else:
            r_ref, o_ref, acc = rest
        kk = pl.program_id(2)

        @pl.when(kk == 0)
        def _():
            acc[...] = jnp.zeros_like(acc)

        acc[...] += lax.dot_general(a_ref[...].astype(BF16), b_ref[...].astype(BF16), dims,
                                    preferred_element_type=F32)

        @pl.when(kk == nk - 1)
        def _():
            r = acc[...]
            if residual is not None:
                r = r + r_ref[...]
            o_ref[...] = r.astype(out_dtype)

    a_spec = pl.BlockSpec((tk, tm), lambda i, j, kk: (kk, i)) if mode == 'tn' else pl.BlockSpec((tm, tk), lambda i, j, kk: (i, kk))
    b_spec = pl.BlockSpec((tn, tk), lambda i, j, kk: (j, kk)) if mode == 'nt' else pl.BlockSpec((tk, tn), lambda i, j, kk: (kk, j))
    o_spec = pl.BlockSpec((tm, tn), lambda i, j, kk: (i, j))
    in_specs, operands = [a_spec, b_spec], [a, b]
    if residual is not None:
        in_specs.append(o_spec)
        operands.append(residual)
    return pl.pallas_call(
        body, name=name, out_shape=jax.ShapeDtypeStruct((m, n), out_dtype),
        grid=(m // tm, n // tn, nk), in_specs=in_specs, out_specs=o_spec,
        scratch_shapes=[pltpu.VMEM((tm, tn), F32)],
        compiler_params=_params(("parallel", "parallel", "arbitrary")),
    )(*operands)


def _row_spec(tm, cols):
    return pl.BlockSpec((tm, cols), lambda i: (i, 0))


def _const_spec(shape):
    return pl.BlockSpec(shape, lambda i: tuple(0 for _ in shape))


def _accumulate(ref, value, step):
    @pl.when(step == 0)
    def _():
        ref[...] = value

    @pl.when(step > 0)
    def _():
        ref[...] += value


def _rstd(x):
    return lax.rsqrt(jnp.mean(x * x, axis=-1, keepdims=True) + EPS)


def _norm_bwd(x, g, dout):
    xn = x * _rstd(x)
    dg = jnp.sum(dout * xn, axis=0, keepdims=True)
    dxn = dout * g
    dx = _rstd(x) * (dxn - xn * jnp.mean(dxn * xn, axis=-1, keepdims=True))
    return dx, dg


def _rmsnorm_fwd(x, g, name):
    t, d = x.shape
    tm = _div_tile(t, 512, 16)

    def body(x_ref, g_ref, o_ref):
        xv = x_ref[...]
        o_ref[...] = (xv * _rstd(xv) * g_ref[...]).astype(BF16)

    return pl.pallas_call(
        body, name=name, out_shape=jax.ShapeDtypeStruct((t, d), BF16), grid=(t // tm,),
        in_specs=[_row_spec(tm, d), _const_spec((1, d))], out_specs=_row_spec(tm, d),
        compiler_params=_params(("parallel",)),
    )(x, g)


def _rmsnorm_bwd(x, g, dh, dres, name):
    t, d = x.shape
    tm = _div_tile(t, 512, 8)

    def body(x_ref, g_ref, dh_ref, dres_ref, dx_ref, dg_ref):
        dx, dg = _norm_bwd(x_ref[...], g_ref[...], dh_ref[...])
        dx_ref[...] = dres_ref[...] + dx
        _accumulate(dg_ref, dg, pl.program_id(0))

    return pl.pallas_call(
        body, name=name,
        out_shape=(jax.ShapeDtypeStruct((t, d), F32), jax.ShapeDtypeStruct((1, d), F32)), grid=(t // tm,),
        in_specs=[_row_spec(tm, d), _const_spec((1, d)), _row_spec(tm, d), _row_spec(tm, d)],
        out_specs=(_row_spec(tm, d), _const_spec((1, d))),
        compiler_params=_params(("arbitrary",)),
    )(x, g, dh, dres)


def _sigmoid(x):
    return 1.0 / (1.0 + jnp.exp(-x))


def _swiglu_fwd(gu, name):
    t, f2 = gu.shape
    f = f2 // 2
    tm = _div_tile(t, 256, 16)

    def body(gu_ref, o_ref):
        gate, up = gu_ref[:, :f], gu_ref[:, f:]
        o_ref[...] = (gate * _sigmoid(gate) * up).astype(BF16)

    return pl.pallas_call(
        body, name=name, out_shape=jax.ShapeDtypeStruct((t, f), BF16), grid=(t // tm,),
        in_specs=[_row_spec(tm, f2)], out_specs=_row_spec(tm, f),
        compiler_params=_params(("parallel",)),
    )(gu)


def _swiglu_bwd(gu, da, name):
    t, f2 = gu.shape
    f = f2 // 2
    tm = _div_tile(t, 256, 16)

    def body(gu_ref, da_ref, o_ref):
        gate, up, dav = gu_ref[:, :f], gu_ref[:, f:], da_ref[...]
        sig = _sigmoid(gate)
        o_ref[:, :f] = (dav * up * (sig * (1.0 + gate * (1.0 - sig)))).astype(BF16)
        o_ref[:, f:] = (dav * (gate * sig)).astype(BF16)

    return pl.pallas_call(
        body, name=name, out_shape=jax.ShapeDtypeStruct((t, f2), BF16), grid=(t // tm,),
        in_specs=[_row_spec(tm, f2), _row_spec(tm, f)], out_specs=_row_spec(tm, f2),
        compiler_params=_params(("parallel",)),
    )(gu, da)


def _loss_head(y, target, name):
    t, d = y.shape
    tm = _div_tile(t, 512, 8)

    def body(y_ref, t_ref, loss_ref, dy_ref):
        diff = y_ref[...] - t_ref[...]
        dy_ref[...] = diff * (1.0 / d)
        part = jnp.sum(jnp.sum(diff * diff, axis=1, keepdims=True), axis=0, keepdims=True) * (0.5 / d)
        _accumulate(loss_ref, part, pl.program_id(0))

    return pl.pallas_call(
        body, name=name,
        out_shape=(jax.ShapeDtypeStruct((1, 1), F32), jax.ShapeDtypeStruct((t, d), F32)), grid=(t // tm,),
        in_specs=[_row_spec(tm, d), _row_spec(tm, d)], out_specs=(_const_spec((1, 1)), _row_spec(tm, d)),
        compiler_params=_params(("arbitrary",)),
    )(y, target)


HALO = 8


def _shift_down(z, k, halo_rows):
    tm = z.shape[0]
    row = lax.broadcasted_iota(jnp.int32, z.shape, 0)
    out = pltpu.roll(z, k, 0)
    for j in range(k):
        out = jnp.where(row == j, halo_rows[HALO - k + j:HALO - k + j + 1, :], out)
    return out


def _shift_up(z, k, halo_rows):
    tm = z.shape[0]
    row = lax.broadcasted_iota(jnp.int32, z.shape, 0)
    out = pltpu.roll(z, tm - k, 0)
    for j in range(k):
        out = jnp.where(row == tm - k + j, halo_rows[j:j + 1, :], out)
    return out


def _sconv_specs(t, tm, cols):
    per = tm // HALO
    last = t // HALO - 1
    cur = pl.BlockSpec((tm, cols), lambda i: (i, 0))
    prev = pl.BlockSpec((HALO, cols), lambda i: (jnp.maximum(i * per - 1, 0), 0))
    nxt = pl.BlockSpec((HALO, cols), lambda i: (jnp.minimum((i + 1) * per, last), 0))
    return cur, prev, nxt


def _sconv_fwd(bcu, conv_w, name):
    t, d3 = bcu.shape
    d = d3 // 3
    tm = _div_tile(t, 256, 16)
    cur, prev, _ = _sconv_specs(t, tm, d3)

    def body(cur_ref, prev_ref, w_ref, o_ref):
        i = pl.program_id(0)
        z = cur_ref[:, d:2 * d] * cur_ref[:, 2 * d:]
        zp = prev_ref[:, d:2 * d] * prev_ref[:, 2 * d:] * (i > 0).astype(F32)
        y = w_ref[0:1, :] * _shift_down(z, 2, zp) + w_ref[1:2, :] * _shift_down(z, 1, zp) + w_ref[2:3, :] * z
        o_ref[...] = (cur_ref[:, :d] * y).astype(BF16)

    return pl.pallas_call(
        body, name=name, out_shape=jax.ShapeDtypeStruct((t, d), BF16), grid=(t // tm,),
        in_specs=[cur, prev, _const_spec((3, d))], out_specs=_row_spec(tm, d),
        compiler_params=_params(("parallel",)),
    )(bcu, bcu, conv_w)


def _sconv_bwd(bcu, dout, conv_w, name):
    t, d3 = bcu.shape
    d = d3 // 3
    tm = _div_tile(t, 256, 16)
    cur, prev, nxt = _sconv_specs(t, tm, d3)
    dcur, _, dnxt = _sconv_specs(t, tm, d)
    n_tiles = t // tm

    def body(cur_ref, prev_ref, nxt_ref, do_ref, don_ref, w_ref, o_ref, dw_ref):
        i = pl.program_id(0)
        b, cg, u = cur_ref[:, :d], cur_ref[:, d:2 * d], cur_ref[:, 2 * d:]
        z = cg * u
        zp = prev_ref[:, d:2 * d] * prev_ref[:, 2 * d:] * (i > 0).astype(F32)
        z1, z2 = _shift_down(z, 1, zp), _shift_down(z, 2, zp)
        w0, w1, w2 = w_ref[0:1, :], w_ref[1:2, :], w_ref[2:3, :]
        y = w0 * z2 + w1 * z1 + w2 * z
        dov = do_ref[...]
        dy = dov * b
        dyn = don_ref[...] * nxt_ref[:, :d] * (i < n_tiles - 1).astype(F32)
        dz = w2 * dy + w1 * _shift_up(dy, 1, dyn) + w0 * _shift_up(dy, 2, dyn)
        o_ref[:, :d] = (dov * y).astype(BF16)
        o_ref[:, d:2 * d] = (dz * u).astype(BF16)
        o_ref[:, 2 * d:] = (dz * cg).astype(BF16)
        dw = jnp.concatenate([jnp.sum(dy * z2, axis=0, keepdims=True), jnp.sum(dy * z1, axis=0, keepdims=True),
                              jnp.sum(dy * z, axis=0, keepdims=True)], axis=0)
        _accumulate(dw_ref, dw, i)

    return pl.pallas_call(
        body, name=name,
        out_shape=(jax.ShapeDtypeStruct((t, d3), BF16), jax.ShapeDtypeStruct((3, d), F32)), grid=(n_tiles,),
        in_specs=[cur, prev, nxt, dcur, dnxt, _const_spec((3, d))],
        out_specs=(_row_spec(tm, d3), _const_spec((3, d))),
        compiler_params=_params(("arbitrary",)),
    )(bcu, bcu, bcu, dout, dout, conv_w)


def _rope_tables(positions, rot, lead, trail):
    inv_freq = ROPE_THETA ** (-jnp.arange(0, rot, 2, dtype=F32) / rot)
    ang = positions.astype(F32)[:, None] * inv_freq
    cos, sin = jnp.cos(ang), jnp.sin(ang)
    t = positions.shape[0]
    cos_full = jnp.concatenate([jnp.ones((t, lead), F32), cos, cos, jnp.ones((t, trail), F32)], axis=1)
    sin_full = jnp.concatenate([jnp.zeros((t, lead), F32), -sin, sin, jnp.zeros((t, trail), F32)], axis=1)
    return cos_full, sin_full


def _swap_halves(x, lead, rot):
    half = rot // 2
    rows, d = x.shape
    parts = []
    if lead:
        parts.append(jnp.zeros((rows, lead), x.dtype))
    parts += [x[:, lead + half:lead + rot], x[:, lead:lead + half]]
    if d - lead - rot:
        parts.append(jnp.zeros((rows, d - lead - rot), x.dtype))
    return jnp.concatenate(parts, axis=1)


def _head_fwd(x, g, cos, sin, lead, rot):
    n = x * _rstd(x) * g
    return n * cos + _swap_halves(n, lead, rot) * sin


def _head_bwd(x, g, cos, sin, dout, lead, rot):
    dn = dout * cos + _swap_halves(dout * sin, lead, rot)
    return _norm_bwd(x, g, dn)


A_Q_COLS = A_HEADS * A_HEAD_DIM
A_KV_COLS = A_KV_HEADS * A_HEAD_DIM
A_COLS = A_Q_COLS + 2 * A_KV_COLS
A_SCALE = A_HEAD_DIM ** -0.5


def _swa_prep_fwd(qkv, q_norm, k_norm, cos, sin, name):
    t = qkv.shape[0]
    tm = _div_tile(t, 256, 16)
    hd = A_HEAD_DIM

    def body(x_ref, gq_ref, gk_ref, cos_ref, sin_ref, o_ref):
        cosv, sinv = cos_ref[...], sin_ref[...]
        for h in range(A_HEADS + A_KV_HEADS):
            g = gq_ref[...] if h < A_HEADS else gk_ref[...]
            o_ref[:, h * hd:(h + 1) * hd] = _head_fwd(x_ref[:, h * hd:(h + 1) * hd], g, cosv, sinv, 0, A_ROT_DIM).astype(BF16)
        o_ref[:, A_Q_COLS + A_KV_COLS:] = x_ref[:, A_Q_COLS + A_KV_COLS:].astype(BF16)

    return pl.pallas_call(
        body, name=name, out_shape=jax.ShapeDtypeStruct((t, A_COLS), BF16), grid=(t // tm,),
        in_specs=[_row_spec(tm, A_COLS), _const_spec((1, hd)), _const_spec((1, hd)), _row_spec(tm, hd), _row_spec(tm, hd)],
        out_specs=_row_spec(tm, A_COLS), compiler_params=_params(("parallel",)),
    )(qkv, q_norm, k_norm, cos, sin)


def _swa_prep_bwd(qkv, dqkv_r, q_norm, k_norm, cos, sin, name):
    t = qkv.shape[0]
    tm = _div_tile(t, 256, 16)
    hd = A_HEAD_DIM

    def body(x_ref, d_ref, gq_ref, gk_ref, cos_ref, sin_ref, o_ref, dgq_ref, dgk_ref):
        cosv, sinv = cos_ref[...], sin_ref[...]
        dgq = jnp.zeros((1, hd), F32)
        dgk = jnp.zeros((1, hd), F32)
        for h in range(A_HEADS + A_KV_HEADS):
            sl = slice(h * hd, (h + 1) * hd)
            g = gq_ref[...] if h < A_HEADS else gk_ref[...]
            dx, dg = _head_bwd(x_ref[:, sl], g, cosv, sinv, d_ref[:, sl], 0, A_ROT_DIM)
            o_ref[:, sl] = dx.astype(BF16)
            if h < A_HEADS:
                dgq = dgq + dg
            else:
                dgk = dgk + dg
        o_ref[:, A_Q_COLS + A_KV_COLS:] = d_ref[:, A_Q_COLS + A_KV_COLS:].astype(BF16)
        _accumulate(dgq_ref, dgq, pl.program_id(0))
        _accumulate(dgk_ref, dgk, pl.program_id(0))

    return pl.pallas_call(
        body, name=name,
        out_shape=(jax.ShapeDtypeStruct((t, A_COLS), BF16), jax.ShapeDtypeStruct((1, hd), F32),
                   jax.ShapeDtypeStruct((1, hd), F32)),
        grid=(t // tm,),
        in_specs=[_row_spec(tm, A_COLS), _row_spec(tm, A_COLS), _const_spec((1, hd)), _const_spec((1, hd)),
                  _row_spec(tm, hd), _row_spec(tm, hd)],
        out_specs=(_row_spec(tm, A_COLS), _const_spec((1, hd)), _const_spec((1, hd))),
        compiler_params=_params(("arbitrary",)),
    )(qkv, dqkv_r, q_norm, k_norm, cos, sin)


def _group_rows(ref, k, width=A_HEAD_DIM, base=0):
    return jnp.concatenate([ref[:, base + (A_GROUP * k + g) * width:base + (A_GROUP * k + g + 1) * width]
                            for g in range(A_GROUP)], axis=0)


def _group_column(ref, k, rows):
    cols = []
    for g in range(A_GROUP):
        h = A_GROUP * k + g
        col = ref[:, h:h + 1]
        cols.append(jnp.broadcast_to(col, (rows, 1)) if col.shape[0] == 1 else col)
    return jnp.concatenate(cols, axis=0)


def _swa_fwd(qkv_r, sinks, name):
    t = qkv_r.shape[0]
    blk = A_WINDOW
    nb = t // blk
    hd = A_HEAD_DIM
    kv_block = A_Q_COLS // (2 * A_KV_COLS)

    def body(q_ref, kvc_ref, kvp_ref, s_ref, o_ref, lse_ref):
        n = pl.program_id(0)
        shape = (A_GROUP * blk, 2 * blk)
        qpos = lax.broadcasted_iota(jnp.int32, shape, 0) & (blk - 1)
        col = lax.broadcasted_iota(jnp.int32, shape, 1)
        delta = qpos + blk - col
        valid = (delta >= 0) & (delta < A_WINDOW) & ((col >= blk) | (n > 0))
        for k in range(A_KV_HEADS):
            qg = _group_rows(q_ref, k)
            kw = jnp.concatenate([kvp_ref[:, k * hd:(k + 1) * hd], kvc_ref[:, k * hd:(k + 1) * hd]], axis=0)
            vw = jnp.concatenate([kvp_ref[:, A_KV_COLS + k * hd:A_KV_COLS + (k + 1) * hd],
                                  kvc_ref[:, A_KV_COLS + k * hd:A_KV_COLS + (k + 1) * hd]], axis=0)
            s = lax.dot_general(qg, kw, (((1,), (1,)), ((), ())), preferred_element_type=F32) * A_SCALE
            s = jnp.where(valid, s, NEG)
            sink = _group_column(s_ref, k, blk)
            m = jnp.maximum(jnp.max(s, axis=-1, keepdims=True), sink)
            p = jnp.exp(s - m)
            denom = jnp.sum(p, axis=-1, keepdims=True) + jnp.exp(sink - m)
            p = p / denom
            o = jnp.dot(p.astype(BF16), vw, preferred_element_type=F32)
            lse = m + jnp.log(denom)
            for g in range(A_GROUP):
                h = A_GROUP * k + g
                o_ref[:, h * hd:(h + 1) * hd] = o[g * blk:(g + 1) * blk].astype(BF16)
                lse_ref[:, h:h + 1] = lse[g * blk:(g + 1) * blk]

    return pl.pallas_call(
        body, name=name,
        out_shape=(jax.ShapeDtypeStruct((t, A_Q_COLS), BF16), jax.ShapeDtypeStruct((t, A_HEADS), F32)), grid=(nb,),
        in_specs=[pl.BlockSpec((blk, A_Q_COLS), lambda n: (n, 0)),
                  pl.BlockSpec((blk, 2 * A_KV_COLS), lambda n: (n, kv_block)),
                  pl.BlockSpec((blk, 2 * A_KV_COLS), lambda n: (jnp.maximum(n - 1, 0), kv_block)),
                  _const_spec((1, A_HEADS))],
        out_specs=(pl.BlockSpec((blk, A_Q_COLS), lambda n: (n, 0)), pl.BlockSpec((blk, A_HEADS), lambda n: (n, 0))),
        compiler_params=_params(("parallel",)),
    )(qkv_r, qkv_r, qkv_r, sinks)


def _swa_bwd(qkv_r, o, lse, do, sinks, name):
    t = qkv_r.shape[0]
    blk = A_WINDOW
    nb = t // blk
    hd = A_HEAD_DIM
    kv_block = A_Q_COLS // (2 * A_KV_COLS)
    rows = A_GROUP * blk

    def nxt(n):
        return jnp.minimum(n + 1, nb - 1)

    def body(qc_ref, qn_ref, kvc_ref, kvp_ref, doc_ref, don_ref, oc_ref, on_ref, lc_ref, ln_ref, s_ref, dx_ref, ds_ref):
        n = pl.program_id(0)
        shape = (2 * rows, 2 * blk)
        row = lax.broadcasted_iota(jnp.int32, shape, 0)
        col = lax.broadcasted_iota(jnp.int32, shape, 1)
        is_next = row >= rows
        delta = jnp.where(is_next, blk, 0) + blk + (row & (blk - 1)) - col
        valid = ((delta >= 0) & (delta < A_WINDOW) & ((col >= blk) | (n > 0)) & (jnp.logical_not(is_next) | (n < nb - 1)))
        dsink_cols = []
        for k in range(A_KV_HEADS):
            qs = jnp.concatenate([_group_rows(qc_ref, k), _group_rows(qn_ref, k)], axis=0)
            dos = jnp.concatenate([_group_rows(doc_ref, k), _group_rows(don_ref, k)], axis=0)
            os_ = jnp.concatenate([_group_rows(oc_ref, k), _group_rows(on_ref, k)], axis=0).astype(F32)
            lses = jnp.concatenate([_group_column(lc_ref, k, blk), _group_column(ln_ref, k, blk)], axis=0)
            kw = jnp.concatenate([kvp_ref[:, k * hd:(k + 1) * hd], kvc_ref[:, k * hd:(k + 1) * hd]], axis=0)
            vw = jnp.concatenate([kvp_ref[:, A_KV_COLS + k * hd:A_KV_COLS + (k + 1) * hd],
                                  kvc_ref[:, A_KV_COLS + k * hd:A_KV_COLS + (k + 1) * hd]], axis=0)
            s = lax.dot_general(qs, kw, (((1,), (1,)), ((), ())), preferred_element_type=F32) * A_SCALE
            p = jnp.exp(jnp.where(valid, s - lses, NEG))
            dos_b = dos.astype(BF16)
            dp = lax.dot_general(dos_b, vw, (((1,), (1,)), ((), ())), preferred_element_type=F32)
            dlt = jnp.sum(dos * os_, axis=-1, keepdims=True)
            ds = p * (dp - dlt)
            dq = jnp.dot(ds[:rows].astype(BF16), kw, preferred_element_type=F32) * A_SCALE
            dk = lax.dot_general(ds[:, blk:].astype(BF16), qs, (((0,), (0,)), ((), ())), preferred_element_type=F32) * A_SCALE
            dv = lax.dot_general(p[:, blk:].astype(BF16), dos_b, (((0,), (0,)), ((), ())), preferred_element_type=F32)
            for g in range(A_GROUP):
                h = A_GROUP * k + g
                dx_ref[:, h * hd:(h + 1) * hd] = dq[g * blk:(g + 1) * blk]
            dx_ref[:, A_Q_COLS + k * hd:A_Q_COLS + (k + 1) * hd] = dk
            dx_ref[:, A_Q_COLS + A_KV_COLS + k * hd:A_Q_COLS + A_KV_COLS + (k + 1) * hd] = dv
            sink = _group_column(s_ref, k, blk)
            contrib = -jnp.exp(sink - lses[:rows]) * dlt[:rows]
            for g in range(A_GROUP):
                dsink_cols.append(jnp.sum(contrib[g * blk:(g + 1) * blk], axis=0, keepdims=True))
        _accumulate(ds_ref, jnp.concatenate(dsink_cols, axis=1), n)

    q_spec = lambda f: pl.BlockSpec((blk, A_Q_COLS), lambda n: (f(n), 0))
    l_spec = lambda f: pl.BlockSpec((blk, A_HEADS), lambda n: (f(n), 0))
    same = lambda n: n
    return pl.pallas_call(
        body, name=name,
        out_shape=(jax.ShapeDtypeStruct((t, A_COLS), F32), jax.ShapeDtypeStruct((1, A_HEADS), F32)), grid=(nb,),
        in_specs=[q_spec(same), q_spec(nxt),
                  pl.BlockSpec((blk, 2 * A_KV_COLS), lambda n: (n, kv_block)),
                  pl.BlockSpec((blk, 2 * A_KV_COLS), lambda n: (jnp.maximum(n - 1, 0), kv_block)),
                  q_spec(same), q_spec(nxt), q_spec(same), q_spec(nxt), l_spec(same), l_spec(nxt),
                  _const_spec((1, A_HEADS))],
        out_specs=(pl.BlockSpec((blk, A_COLS), lambda n: (n, 0)), _const_spec((1, A_HEADS))),
        compiler_params=_params(("arbitrary",)),
    )(qkv_r, qkv_r, qkv_r, qkv_r, do, do, o, o, lse, lse, sinks)


C_DOWN_COLS = C_Q_RANK + C_KV_RANK + C_ROPE
C_Q_COLS = C_HEADS * C_QK
C_KV_COLS = C_HEADS * (C_NOPE + C_V)
C_O_COLS = C_HEADS * C_V
C_PAD = LANES
C_SCALE = C_QK ** -0.5
C_PAIR = 2


def _mla_latent_fwd(down, q_a_norm, kv_a_norm, name):
    t = down.shape[0]
    tm = _div_tile(t, 512, 16)

    def body(x_ref, gq_ref, gk_ref, cq_ref, ckv_ref):
        cq, ckv = x_ref[:, :C_Q_RANK], x_ref[:, C_Q_RANK:C_Q_RANK + C_KV_RANK]
        cq_ref[...] = (cq * _rstd(cq) * gq_ref[...]).astype(BF16)
        ckv_ref[...] = (ckv * _rstd(ckv) * gk_ref[...]).astype(BF16)

    return pl.pallas_call(
        body, name=name,
        out_shape=(jax.ShapeDtypeStruct((t, C_Q_RANK), BF16), jax.ShapeDtypeStruct((t, C_KV_RANK), BF16)), grid=(t // tm,),
        in_specs=[_row_spec(tm, C_DOWN_COLS), _const_spec((1, C_Q_RANK)), _const_spec((1, C_KV_RANK))],
        out_specs=(_row_spec(tm, C_Q_RANK), _row_spec(tm, C_KV_RANK)), compiler_params=_params(("parallel",)),
    )(down, q_a_norm, kv_a_norm)


def _mla_latent_bwd(down, dcq, dckv, dkrope, q_a_norm, kv_a_norm, name):
    t = down.shape[0]
    tm = _div_tile(t, 512, 16)

    def body(x_ref, dcq_ref, dckv_ref, dkr_ref, gq_ref, gk_ref, o_ref, dgq_ref, dgk_ref):
        dq, dgq = _norm_bwd(x_ref[:, :C_Q_RANK], gq_ref[...], dcq_ref[...])
        dkv, dgk = _norm_bwd(x_ref[:, C_Q_RANK:C_Q_RANK + C_KV_RANK], gk_ref[...], dckv_ref[...])
        o_ref[...] = jnp.concatenate([dq, dkv, dkr_ref[...]], axis=1).astype(BF16)
        _accumulate(dgq_ref, dgq, pl.program_id(0))
        _accumulate(dgk_ref, dgk, pl.program_id(0))

    return pl.pallas_call(
        body, name=name,
        out_shape=(jax.ShapeDtypeStruct((t, C_DOWN_COLS), BF16), jax.ShapeDtypeStruct((1, C_Q_RANK), F32),
                   jax.ShapeDtypeStruct((1, C_KV_RANK), F32)),
        grid=(t // tm,),
        in_specs=[_row_spec(tm, C_DOWN_COLS), _row_spec(tm, C_Q_RANK), _row_spec(tm, C_KV_RANK), _row_spec(tm, C_ROPE),
                  _const_spec((1, C_Q_RANK)), _const_spec((1, C_KV_RANK))],
        out_specs=(_row_spec(tm, C_DOWN_COLS), _const_spec((1, C_Q_RANK)), _const_spec((1, C_KV_RANK))),
        compiler_params=_params(("arbitrary",)),
    )(down, dcq, dckv, dkrope, q_a_norm, kv_a_norm)


def _head_major_spec(tm, width):
    return pl.BlockSpec((C_HEADS, tm, width), lambda i: (0, i, 0))


def _mla_qk_fwd(qw, kvw, down, q_norm, k_norm, cos, sin, name):
    t = qw.shape[0]
    tm = _div_tile(t, 256, 16)
    kvd = C_NOPE + C_V

    def body(q_ref, kv_ref, dn_ref, gq_ref, gk_ref, cos_ref, sin_ref, qo_ref, ko_ref, vo_ref):
        cosv, sinv = cos_ref[...], sin_ref[...]
        k_rope = dn_ref[:, C_Q_RANK + C_KV_RANK:]
        pad = jnp.zeros((tm, C_PAD - C_QK), F32)
        for h in range(C_HEADS):
            qh = _head_fwd(q_ref[:, h * C_QK:(h + 1) * C_QK], gq_ref[...], cosv, sinv, C_NOPE, C_ROPE)
            kx = jnp.concatenate([kv_ref[:, h * kvd:h * kvd + C_NOPE], k_rope], axis=1)
            kh = _head_fwd(kx, gk_ref[...], cosv, sinv, C_NOPE, C_ROPE)
            qo_ref[h] = jnp.concatenate([qh, pad], axis=1).astype(BF16)
            ko_ref[h] = jnp.concatenate([kh, pad], axis=1).astype(BF16)
            vo_ref[h] = kv_ref[:, h * kvd + C_NOPE:(h + 1) * kvd].astype(BF16)

    return pl.pallas_call(
        body, name=name,
        out_shape=(jax.ShapeDtypeStruct((C_HEADS, t, C_PAD), BF16), jax.ShapeDtypeStruct((C_HEADS, t, C_PAD), BF16),
                   jax.ShapeDtypeStruct((C_HEADS, t, C_V), BF16)),
        grid=(t // tm,),
        in_specs=[_row_spec(tm, C_Q_COLS), _row_spec(tm, C_KV_COLS), _row_spec(tm, C_DOWN_COLS), _const_spec((1, C_QK)),
                  _const_spec((1, C_QK)), _row_spec(tm, C_QK), _row_spec(tm, C_QK)],
        out_specs=(_head_major_spec(tm, C_PAD), _head_major_spec(tm, C_PAD), _head_major_spec(tm, C_V)),
        compiler_params=_params(("parallel",)),
    )(qw, kvw, down, q_norm, k_norm, cos, sin)


def _mla_qk_bwd(qw, kvw, down, dq, dk, dv, q_norm, k_norm, cos, sin, name):
    t = qw.shape[0]
    tm = _div_tile(t, 256, 16)
    kvd = C_NOPE + C_V

    def body(q_ref, kv_ref, dn_ref, dq_ref, dk_ref, dv_ref, gq_ref, gk_ref, cos_ref, sin_ref,
             dqw_ref, dkvw_ref, dkr_ref, dgq_ref, dgk_ref):
        cosv, sinv = cos_ref[...], sin_ref[...]
        k_rope = dn_ref[:, C_Q_RANK + C_KV_RANK:]
        dgq = jnp.zeros((1, C_QK), F32)
        dgk = jnp.zeros((1, C_QK), F32)
        dkr = jnp.zeros((tm, C_ROPE), F32)
        for h in range(C_HEADS):
            dxq, dg = _head_bwd(q_ref[:, h * C_QK:(h + 1) * C_QK], gq_ref[...], cosv, sinv, dq_ref[h][:, :C_QK], C_NOPE, C_ROPE)
            dgq = dgq + dg
            dqw_ref[:, h * C_QK:(h + 1) * C_QK] = dxq.astype(BF16)
            kx = jnp.concatenate([kv_ref[:, h * kvd:h * kvd + C_NOPE], k_rope], axis=1)
            dxk, dg = _head_bwd(kx, gk_ref[...], cosv, sinv, dk_ref[h][:, :C_QK], C_NOPE, C_ROPE)
            dgk = dgk + dg
            dkr = dkr + dxk[:, C_NOPE:]
            dkvw_ref[:, h * kvd:(h + 1) * kvd] = jnp.concatenate([dxk[:, :C_NOPE], dv_ref[h]], axis=1).astype(BF16)
        dkr_ref[...] = dkr
        _accumulate(dgq_ref, dgq, pl.program_id(0))
        _accumulate(dgk_ref, dgk, pl.program_id(0))

    return pl.pallas_call(
        body, name=name,
        out_shape=(jax.ShapeDtypeStruct((t, C_Q_COLS), BF16), jax.ShapeDtypeStruct((t, C_KV_COLS), BF16),
                   jax.ShapeDtypeStruct((t, C_ROPE), F32), jax.ShapeDtypeStruct((1, C_QK), F32),
                   jax.ShapeDtypeStruct((1, C_QK), F32)),
        grid=(t // tm,),
        in_specs=[_row_spec(tm, C_Q_COLS), _row_spec(tm, C_KV_COLS), _row_spec(tm, C_DOWN_COLS),
                  _head_major_spec(tm, C_PAD), _head_major_spec(tm, C_PAD), _head_major_spec(tm, C_V),
                  _const_spec((1, C_QK)), _const_spec((1, C_QK)), _row_spec(tm, C_QK), _row_spec(tm, C_QK)],
        out_specs=(_row_spec(tm, C_Q_COLS), _row_spec(tm, C_KV_COLS), _row_spec(tm, C_ROPE), _const_spec((1, C_QK)),
                   _const_spec((1, C_QK))),
        compiler_params=_params(("arbitrary",)),
    )(qw, kvw, down, dq, dk, dv, q_norm, k_norm, cos, sin)


def _causal_keep(blk, qi, ki):
    row = lax.broadcasted_iota(jnp.int32, (blk, blk), 0)
    col = lax.broadcasted_iota(jnp.int32, (blk, blk), 1)
    return (col <= row) | (ki < qi)


def _mla_fwd(q, k, v, name):
    _, t, _ = q.shape
    blk = min(MLA_BLOCK, t)
    nq = t // blk

    def body(q_ref, k_ref, v_ref, o_ref, lse_ref, m_sc, l_sc, acc_sc):
        qi, ki = pl.program_id(1), pl.program_id(2)

        @pl.when(ki == 0)
        def _():
            m_sc[...] = jnp.full_like(m_sc, NEG)
            l_sc[...] = jnp.zeros_like(l_sc)
            acc_sc[...] = jnp.zeros_like(acc_sc)

        @pl.when(ki <= qi)
        def _():
            keep = _causal_keep(blk, qi, ki)
            for hh in range(C_PAIR):
                s = lax.dot_general(q_ref[hh], k_ref[hh], (((1,), (1,)), ((), ())), preferred_element_type=F32) * C_SCALE
                s = jnp.where(keep, s, NEG)
                m_prev = m_sc[hh]
                m_new = jnp.maximum(m_prev, jnp.max(s, axis=-1, keepdims=True))
                alpha = jnp.exp(m_prev - m_new)
                p = jnp.exp(s - m_new)
                l_sc[hh] = alpha * l_sc[hh] + jnp.sum(p, axis=-1, keepdims=True)
                acc_sc[hh] = alpha * acc_sc[hh] + jnp.dot(p.astype(BF16), v_ref[hh], preferred_element_type=F32)
                m_sc[hh] = m_new

        @pl.when(ki == qi)
        def _():
            outs = []
            for hh in range(C_PAIR):
                outs.append(acc_sc[hh] / l_sc[hh])
                lse_ref[hh] = m_sc[hh] + jnp.log(l_sc[hh])
            o_ref[...] = jnp.concatenate(outs, axis=1).astype(BF16)

    kmap = lambda hp, qi, ki: (hp, jnp.minimum(ki, qi), 0)
    return pl.pallas_call(
        body, name=name,
        out_shape=(jax.ShapeDtypeStruct((t, C_O_COLS), BF16), jax.ShapeDtypeStruct((C_HEADS, t, 1), F32)),
        grid=(C_HEADS // C_PAIR, nq, nq),
        in_specs=[pl.BlockSpec((C_PAIR, blk, C_PAD), lambda hp, qi, ki: (hp, qi, 0)),
                  pl.BlockSpec((C_PAIR, blk, C_PAD), kmap), pl.BlockSpec((C_PAIR, blk, C_V), kmap)],
        out_specs=(pl.BlockSpec((blk, C_PAIR * C_V), lambda hp, qi, ki: (qi, hp)),
                   pl.BlockSpec((C_PAIR, blk, 1), lambda hp, qi, ki: (hp, qi, 0))),
        scratch_shapes=[pltpu.VMEM((C_PAIR, blk, 1), F32), pltpu.VMEM((C_PAIR, blk, 1), F32),
                        pltpu.VMEM((C_PAIR, blk, C_V), F32)],
        compiler_params=_params(("parallel", "parallel", "arbitrary")),
    )(q, k, v)


def _mla_probs(q_ref, k_ref, v_ref, do_ref, o_ref, lse_ref, hh, keep):
    do_h = do_ref[:, hh * C_V:(hh + 1) * C_V]
    o_h = o_ref[:, hh * C_V:(hh + 1) * C_V].astype(F32)
    dlt = jnp.sum(do_h * o_h, axis=-1, keepdims=True)
    s = lax.dot_general(q_ref[hh], k_ref[hh], (((1,), (1,)), ((), ())), preferred_element_type=F32) * C_SCALE
    p = jnp.exp(jnp.where(keep, s - lse_ref[hh], NEG))
    do_b = do_h.astype(BF16)
    dp = lax.dot_general(do_b, v_ref[hh], (((1,), (1,)), ((), ())), preferred_element_type=F32)
    return p, p * (dp - dlt), do_b


def _mla_bwd_dq(q, k, v, do, o, lse, name):
    _, t, _ = q.shape
    blk = min(MLA_BLOCK, t)
    nq = t // blk

    def body(q_ref, k_ref, v_ref, do_ref, o_ref, lse_ref, dq_ref, acc_sc):
        qi, ki = pl.program_id(1), pl.program_id(2)

        @pl.when(ki == 0)
        def _():
            acc_sc[...] = jnp.zeros_like(acc_sc)

        @pl.when(ki <= qi)
        def _():
            keep = _causal_keep(blk, qi, ki)
            for hh in range(C_PAIR):
                _, ds, _ = _mla_probs(q_ref, k_ref, v_ref, do_ref, o_ref, lse_ref, hh, keep)
                acc_sc[hh] += jnp.dot(ds.astype(BF16), k_ref[hh], preferred_element_type=F32)

        @pl.when(ki == qi)
        def _():
            dq_ref[...] = acc_sc[...] * C_SCALE

    kmap = lambda hp, qi, ki: (hp, jnp.minimum(ki, qi), 0)
    qmap = lambda hp, qi, ki: (hp, qi, 0)
    wide = lambda hp, qi, ki: (qi, hp)
    return pl.pallas_call(
        body, name=name, out_shape=jax.ShapeDtypeStruct((C_HEADS, t, C_PAD), F32),
        grid=(C_HEADS // C_PAIR, nq, nq),
        in_specs=[pl.BlockSpec((C_PAIR, blk, C_PAD), qmap), pl.BlockSpec((C_PAIR, blk, C_PAD), kmap),
                  pl.BlockSpec((C_PAIR, blk, C_V), kmap), pl.BlockSpec((blk, C_PAIR * C_V), wide),
                  pl.BlockSpec((blk, C_PAIR * C_V), wide), pl.BlockSpec((C_PAIR, blk, 1), qmap)],
        out_specs=pl.BlockSpec((C_PAIR, blk, C_PAD), qmap),
        scratch_shapes=[pltpu.VMEM((C_PAIR, blk, C_PAD), F32)],
        compiler_params=_params(("parallel", "parallel", "arbitrary")),
    )(q, k, v, do, o, lse)


def _mla_bwd_dkv(q, k, v, do, o, lse, name):
    _, t, _ = q.shape
    blk = min(MLA_BLOCK, t)
    nq = t // blk

    def body(q_ref, k_ref, v_ref, do_ref, o_ref, lse_ref, dk_ref, dv_ref, dk_sc, dv_sc):
        ki, qi = pl.program_id(1), pl.program_id(2)

        @pl.when(qi == 0)
        def _():
            dk_sc[...] = jnp.zeros_like(dk_sc)
            dv_sc[...] = jnp.zeros_like(dv_sc)

        @pl.when(qi >= ki)
        def _():
            keep = _causal_keep(blk, qi, ki)
            for hh in range(C_PAIR):
                p, ds, do_b = _mla_probs(q_ref, k_ref, v_ref, do_ref, o_ref, lse_ref, hh, keep)
                dv_sc[hh] += lax.dot_general(p.astype(BF16), do_b, (((0,), (0,)), ((), ())), preferred_element_type=F32)
                dk_sc[hh] += lax.dot_general(ds.astype(BF16), q_ref[hh], (((0,), (0,)), ((), ())),
                                             preferred_element_type=F32)

        @pl.when(qi == nq - 1)
        def _():
            dk_ref[...] = dk_sc[...] * C_SCALE
            dv_ref[...] = dv_sc[...]

    qmap = lambda hp, ki, qi: (hp, jnp.maximum(qi, ki), 0)
    kmap = lambda hp, ki, qi: (hp, ki, 0)
    wide = lambda hp, ki, qi: (jnp.maximum(qi, ki), hp)
    return pl.pallas_call(
        body, name=name,
        out_shape=(jax.ShapeDtypeStruct((C_HEADS, t, C_PAD), F32), jax.ShapeDtypeStruct((C_HEADS, t, C_V), F32)),
        grid=(C_HEADS // C_PAIR, nq, nq),
        in_specs=[pl.BlockSpec((C_PAIR, blk, C_PAD), qmap), pl.BlockSpec((C_PAIR, blk, C_PAD), kmap),
                  pl.BlockSpec((C_PAIR, blk, C_V), kmap), pl.BlockSpec((blk, C_PAIR * C_V), wide),
                  pl.BlockSpec((blk, C_PAIR * C_V), wide), pl.BlockSpec((C_PAIR, blk, 1), qmap)],
        out_specs=(pl.BlockSpec((C_PAIR, blk, C_PAD), kmap), pl.BlockSpec((C_PAIR, blk, C_V), kmap)),
        scratch_shapes=[pltpu.VMEM((C_PAIR, blk, C_PAD), F32), pltpu.VMEM((C_PAIR, blk, C_V), F32)],
        compiler_params=_params(("parallel", "parallel", "arbitrary")),
    )(q, k, v, do, o, lse)


def _adamw(parts, w, m, v, name):
    rows, cols = w.shape
    tm = _div_tile(rows, 128, 8)

    def body(p_ref, w_ref, m_ref, v_ref, g_ref, d_ref, nm_ref, nv_ref):
        g = p_ref[0]
        for j in range(1, N_DEV):
            g = g + p_ref[j]
        nm = ADAM_B1 * m_ref[...] + (1.0 - ADAM_B1) * g
        nv = ADAM_B2 * v_ref[...] + (1.0 - ADAM_B2) * jnp.square(g)
        m_hat = nm / (1.0 - ADAM_B1 ** ADAM_STEP)
        v_hat = nv / (1.0 - ADAM_B2 ** ADAM_STEP)
        g_ref[...] = g
        d_ref[...] = -ADAM_LR * (m_hat / (jnp.sqrt(v_hat) + ADAM_EPS) + ADAM_WD * w_ref[...])
        nm_ref[...] = nm
        nv_ref[...] = nv

    spec = _row_spec(tm, cols)
    return pl.pallas_call(
        body, name=name, out_shape=tuple(jax.ShapeDtypeStruct((rows, cols), F32) for _ in range(4)), grid=(rows // tm,),
        in_specs=[pl.BlockSpec((N_DEV, tm, cols), lambda i: (0, i, 0)), spec, spec, spec],
        out_specs=(spec, spec, spec, spec), compiler_params=_params(("parallel",)),
    )(parts, w, m, v)


def _pack(flat_pieces, dtype):
    flat = jnp.concatenate([p.reshape(-1).astype(dtype) for p in flat_pieces])
    unit = PACK_COLS * PACK_ROW_MULTIPLE
    padded = -(-flat.shape[0] // unit) * unit
    return jnp.pad(flat, (0, padded - flat.shape[0])).reshape(padded // PACK_COLS, PACK_COLS)


def _unpack(buf, shapes):
    flat = buf.reshape(-1)
    out, off = [], 0
    for shape in shapes:
        size = 1
        for s in shape:
            size *= s
        out.append(flat[off:off + size].reshape(shape))
        off += size
    return out


def _join_shards(stacked, axis):
    moved = jnp.moveaxis(stacked, 0, axis)
    shape = list(moved.shape)
    shape[axis:axis + 2] = [shape[axis] * shape[axis + 1]]
    return moved.reshape(shape)


def _split_shards(full, axis):
    shape = list(full.shape)
    shape[axis:axis + 1] = [N_DEV, shape[axis] // N_DEV]
    return jnp.moveaxis(full.reshape(shape), axis, 0)


def _gather_weights(local, names, dtype, name):
    shapes = [local[n].shape for n in names]
    got = _exchange(_pack([local[n] for n in names], dtype), False, name)
    per_dev = [_unpack(got[d], shapes) for d in range(N_DEV)]
    return {n: _join_shards(jnp.stack([per_dev[d][i] for d in range(N_DEV)]), SHARD_AXIS[n]) for i, n in enumerate(names)}


def _forward_backward(x, positions, target, w, rep):
    cos_a, sin_a = _rope_tables(positions, A_ROT_DIM, 0, A_HEAD_DIM - A_ROT_DIM)
    cos_c, sin_c = _rope_tables(positions, C_ROPE, C_NOPE, 0)
    saved = []
    for i in range(DEPTH):
        kind, j = i % N_MIXERS, i // N_MIXERS
        s = {'x': x}
        h1 = _rmsnorm_fwd(x, rep['mix_norm'][i:i + 1], f"mix_norm_fwd_{i}")
        s['h1'] = h1
        if kind == 0:
            s['qkv'] = _matmul(h1, w['a_w_qkv'][j], 'nn', f"a_qkv_{i}")
            s['qkv_r'] = _swa_prep_fwd(s['qkv'], rep['a_q_norm'][j:j + 1], rep['a_k_norm'][j:j + 1], cos_a, sin_a,
                                       f"a_prep_fwd_{i}")
            s['o'], s['lse'] = _swa_fwd(s['qkv_r'], rep['a_sinks'][j:j + 1], f"a_attn_fwd_{i}")
            x1 = _matmul(s['o'], w['a_w_o'][j], 'nn', f"a_out_{i}", residual=x)
        elif kind == 1:
            s['bcu'] = _matmul(h1, w['b_w_in'][j], 'nn', f"b_in_{i}")
            s['by'] = _sconv_fwd(s['bcu'], w['b_conv_w'][j], f"b_conv_fwd_{i}")
            x1 = _matmul(s['by'], w['b_w_out'][j], 'nn', f"b_out_{i}", residual=x)
        else:
            s['down'] = _matmul(h1, w['c_w_down'][j], 'nn', f"c_down_{i}")
            s['cq'], s['ckv'] = _mla_latent_fwd(s['down'], w['c_q_a_norm'][j:j + 1], w['c_kv_a_norm'][j:j + 1],
                                                f"c_latent_fwd_{i}")
            s['qw'] = _matmul(s['cq'], w['c_w_q_up'][j], 'nn', f"c_q_up_{i}")
            s['kvw'] = _matmul(s['ckv'], w['c_w_kv_up'][j], 'nn', f"c_kv_up_{i}")
            s['q'], s['k'], s['v'] = _mla_qk_fwd(s['qw'], s['kvw'], s['down'], rep['c_q_norm'][j:j + 1],
                                                 rep['c_k_norm'][j:j + 1], cos_c, sin_c, f"c_prep_fwd_{i}")
            s['o'], s['lse'] = _mla_fwd(s['q'], s['k'], s['v'], f"c_attn_fwd_{i}")
            x1 = _matmul(s['o'], w['c_w_o'][j], 'nn', f"c_out_{i}", residual=x)
        s['x1'] = x1
        s['h2'] = _rmsnorm_fwd(x1, rep['ffn_norm'][i:i + 1], f"ffn_norm_fwd_{i}")
        s['gu'] = _matmul(s['h2'], w['f_w_gate_up'][i], 'nn', f"f_gate_up_{i}")
        s['act'] = _swiglu_fwd(s['gu'], f"f_act_fwd_{i}")
        x = _matmul(s['act'], w['f_w_down'][i], 'nn', f"f_down_{i}", residual=x1)
        saved.append(s)

    loss, dx = _loss_head(x, target, "loss_head")

    per_layer = {n: {} for n in WEIGHTS}
    for i in reversed(range(DEPTH)):
        kind, j = i % N_MIXERS, i // N_MIXERS
        s = saved[i]
        per_layer['f_w_down'][i] = _matmul(s['act'], dx, 'tn', f"f_down_dw_{i}")
        dact = _matmul(dx, w['f_w_down'][i], 'nt', f"f_down_dx_{i}")
        dgu = _swiglu_bwd(s['gu'], dact, f"f_act_bwd_{i}")
        per_layer['f_w_gate_up'][i] = _matmul(s['h2'], dgu, 'tn', f"f_gate_up_dw_{i}")
        dh2 = _matmul(dgu, w['f_w_gate_up'][i], 'nt', f"f_gate_up_dx_{i}")
        dx1, per_layer['ffn_norm'][i] = _rmsnorm_bwd(s['x1'], rep['ffn_norm'][i:i + 1], dh2, dx, f"ffn_norm_bwd_{i}")
        if kind == 0:
            per_layer['a_w_o'][j] = _matmul(s['o'], dx1, 'tn', f"a_out_dw_{i}")
            do = _matmul(dx1, w['a_w_o'][j], 'nt', f"a_out_dx_{i}")
            dqkv_r, per_layer['a_sinks'][j] = _swa_bwd(s['qkv_r'], s['o'], s['lse'], do, rep['a_sinks'][j:j + 1],
                                                       f"a_attn_bwd_{i}")
            dqkv, per_layer['a_q_norm'][j], per_layer['a_k_norm'][j] = _swa_prep_bwd(
                s['qkv'], dqkv_r, rep['a_q_norm'][j:j + 1], rep['a_k_norm'][j:j + 1], cos_a, sin_a, f"a_prep_bwd_{i}")
            per_layer['a_w_qkv'][j] = _matmul(s['h1'], dqkv, 'tn', f"a_qkv_dw_{i}")
            dh1 = _matmul(dqkv, w['a_w_qkv'][j], 'nt', f"a_qkv_dx_{i}")
        elif kind == 1:
            per_layer['b_w_out'][j] = _matmul(s['by'], dx1, 'tn', f"b_out_dw_{i}")
            dby = _matmul(dx1, w['b_w_out'][j], 'nt', f"b_out_dx_{i}")
            dbcu, per_layer['b_conv_w'][j] = _sconv_bwd(s['bcu'], dby, w['b_conv_w'][j], f"b_conv_bwd_{i}")
            per_layer['b_w_in'][j] = _matmul(s['h1'], dbcu, 'tn', f"b_in_dw_{i}")
            dh1 = _matmul(dbcu, w['b_w_in'][j], 'nt', f"b_in_dx_{i}")
        else:
            per_layer['c_w_o'][j] = _matmul(s['o'], dx1, 'tn', f"c_out_dw_{i}")
            do = _matmul(dx1, w['c_w_o'][j], 'nt', f"c_out_dx_{i}")
            dq = _mla_bwd_dq(s['q'], s['k'], s['v'], do, s['o'], s['lse'], f"c_attn_bwd_dq_{i}")
            dk, dv = _mla_bwd_dkv(s['q'], s['k'], s['v'], do, s['o'], s['lse'], f"c_attn_bwd_dkv_{i}")
            dqw, dkvw, dkrope, per_layer['c_q_norm'][j], per_layer['c_k_norm'][j] = _mla_qk_bwd(
                s['qw'], s['kvw'], s['down'], dq, dk, dv, rep['c_q_norm'][j:j + 1], rep['c_k_norm'][j:j + 1], cos_c, sin_c,
                f"c_prep_bwd_{i}")
            per_layer['c_w_q_up'][j] = _matmul(s['cq'], dqw, 'tn', f"c_q_up_dw_{i}")
            dcq = _matmul(dqw, w['c_w_q_up'][j], 'nt', f"c_q_up_dx_{i}")
            per_layer['c_w_kv_up'][j] = _matmul(s['ckv'], dkvw, 'tn', f"c_kv_up_dw_{i}")
            dckv = _matmul(dkvw, w['c_w_kv_up'][j], 'nt', f"c_kv_up_dx_{i}")
            ddown, per_layer['c_q_a_norm'][j], per_layer['c_kv_a_norm'][j] = _mla_latent_bwd(
                s['down'], dcq, dckv, dkrope, w['c_q_a_norm'][j:j + 1], w['c_kv_a_norm'][j:j + 1], f"c_latent_bwd_{i}")
            per_layer['c_w_down'][j] = _matmul(s['h1'], ddown, 'tn', f"c_down_dw_{i}")
            dh1 = _matmul(ddown, w['c_w_down'][j], 'nt', f"c_down_dx_{i}")
        dx, per_layer['mix_norm'][i] = _rmsnorm_bwd(s['x'], rep['mix_norm'][i:i + 1], dh1, dx1, f"mix_norm_bwd_{i}")

    grads = {}
    for n in WEIGHTS:
        stacked = jnp.stack([per_layer[n][j] for j in sorted(per_layer[n])])
        if n in ('mix_norm', 'ffn_norm', 'a_q_norm', 'a_k_norm', 'a_sinks', 'c_q_a_norm', 'c_kv_a_norm', 'c_q_norm', 'c_k_norm'):
            stacked = stacked.reshape(stacked.shape[0], stacked.shape[-1])
        grads[n] = stacked
    return loss, dx, grads


def kernel(x, positions, mix_norm, ffn_norm, a_w_qkv, a_q_norm, a_k_norm, a_sinks, a_w_o, b_w_in, b_conv_w, b_w_out, c_w_down, c_q_a_norm, c_kv_a_norm, c_w_q_up, c_w_kv_up, c_q_norm, c_k_norm, c_w_o, f_w_gate_up, f_w_down, loss_target, m_mix_norm, m_ffn_norm, m_a_w_qkv, m_a_q_norm, m_a_k_norm, m_a_sinks, m_a_w_o, m_b_w_in, m_b_conv_w, m_b_w_out, m_c_w_down, m_c_q_a_norm, m_c_kv_a_norm, m_c_w_q_up, m_c_w_kv_up, m_c_q_norm, m_c_k_norm, m_c_w_o, m_f_w_gate_up, m_f_w_down, v_mix_norm, v_ffn_norm, v_a_w_qkv, v_a_q_norm, v_a_k_norm, v_a_sinks, v_a_w_o, v_b_w_in, v_b_conv_w, v_b_w_out, v_c_w_down, v_c_q_a_norm, v_c_kv_a_norm, v_c_w_q_up, v_c_w_kv_up, v_c_q_norm, v_c_k_norm, v_c_w_o, v_f_w_gate_up, v_f_w_down):
    local = dict(mix_norm=mix_norm, ffn_norm=ffn_norm, a_w_qkv=a_w_qkv, a_q_norm=a_q_norm, a_k_norm=a_k_norm, a_sinks=a_sinks, a_w_o=a_w_o, b_w_in=b_w_in, b_conv_w=b_conv_w, b_w_out=b_w_out, c_w_down=c_w_down, c_q_a_norm=c_q_a_norm, c_kv_a_norm=c_kv_a_norm, c_w_q_up=c_w_q_up, c_w_kv_up=c_w_kv_up, c_q_norm=c_q_norm, c_k_norm=c_k_norm, c_w_o=c_w_o, f_w_gate_up=f_w_gate_up, f_w_down=f_w_down)
    mom1 = dict(mix_norm=m_mix_norm, ffn_norm=m_ffn_norm, a_w_qkv=m_a_w_qkv, a_q_norm=m_a_q_norm, a_k_norm=m_a_k_norm, a_sinks=m_a_sinks, a_w_o=m_a_w_o, b_w_in=m_b_w_in, b_conv_w=m_b_conv_w, b_w_out=m_b_w_out, c_w_down=m_c_w_down, c_q_a_norm=m_c_q_a_norm, c_kv_a_norm=m_c_kv_a_norm, c_w_q_up=m_c_w_q_up, c_w_kv_up=m_c_w_kv_up, c_q_norm=m_c_q_norm, c_k_norm=m_c_k_norm, c_w_o=m_c_w_o, f_w_gate_up=m_f_w_gate_up, f_w_down=m_f_w_down)
    mom2 = dict(mix_norm=v_mix_norm, ffn_norm=v_ffn_norm, a_w_qkv=v_a_w_qkv, a_q_norm=v_a_q_norm, a_k_norm=v_a_k_norm, a_sinks=v_a_sinks, a_w_o=v_a_w_o, b_w_in=v_b_w_in, b_conv_w=v_b_conv_w, b_w_out=v_b_w_out, c_w_down=v_c_w_down, c_q_a_norm=v_c_q_a_norm, c_kv_a_norm=v_c_kv_a_norm, c_w_q_up=v_c_w_q_up, c_w_kv_up=v_c_w_kv_up, c_q_norm=v_c_q_norm, c_k_norm=v_c_k_norm, c_w_o=v_c_w_o, f_w_gate_up=v_f_w_gate_up, f_w_down=v_f_w_down)
    t, d = x.shape[1], x.shape[2]

    full = _gather_weights(local, GATHER_BF16, BF16, "gather_weights_bf16")
    full.update(_gather_weights(local, GATHER_F32, F32, "gather_weights_f32"))
    rep = {n: local[n] for n in REPLICATED}

    loss, grad_x, grads = _forward_backward(x.reshape(t, d), positions.reshape(t), loss_target.reshape(t, d), full, rep)

    out_g, out_d, out_m, out_v = {}, {}, {}, {}

    def update(names, parts):
        shapes = [local[n].shape for n in names]
        packed = [_pack([src[n] for n in names], F32) for src in (local, mom1, mom2)]
        results = _adamw(parts, *packed, name="adamw_" + names[0])
        for dst, buf in zip((out_g, out_d, out_m, out_v), results):
            dst.update(dict(zip(names, _unpack(buf, shapes))))

    to_send = jnp.stack([_pack([_split_shards(grads[n], SHARD_AXIS[n])[dev] for n in SHARDED], F32) for dev in range(N_DEV)])
    update(SHARDED, _exchange(to_send, True, "scatter_gradients"))
    update(REPLICATED, _exchange(_pack([grads[n] for n in REPLICATED], F32), False, "gather_small_gradients"))

    loss = lax.psum(loss.reshape(()), MESH_AXES)
    outs = [loss, grad_x.reshape(1, t, d)]
    for res in (out_g, out_d, out_m, out_v):
        outs += [res[n] for n in WEIGHTS]
    return tuple(outs)
```

```python
import functools

import jax
import jax.numpy as jnp
from jax import lax
from jax.experimental import pallas as pl
from jax.experimental.pallas import tpu as pltpu

F32 = jnp.float32
BF16 = jnp.bfloat16

N_DEV = 8
MESH_AXES = ("x", "y", "c")

DEPTH = 4
N_MIXERS = 3
ROPE_THETA = 500000.0
EPS = 1e-6
A_HEADS, A_KV_HEADS, A_HEAD_DIM, A_ROT_DIM, A_WINDOW = 16, 4, 64, 16, 128
A_GROUP = A_HEADS // A_KV_HEADS
C_HEADS, C_NOPE, C_ROPE, C_V, C_Q_RANK, C_KV_RANK = 16, 64, 32, 64, 384, 256
C_QK = C_NOPE + C_ROPE
ADAM_LR, ADAM_B1, ADAM_B2, ADAM_EPS, ADAM_WD, ADAM_STEP = 0.001, 0.9, 0.999, 1e-08, 0.01, 10

VMEM_LIMIT_BYTES = 48 * 1024 * 1024
LANES = 128
NEG = -1e30
MLA_BLOCK = 512
MLA_FWD_BLOCK = 1024
PACK_COLS = 1024
PACK_ROW_MULTIPLE = 128

WEIGHTS = ['mix_norm', 'ffn_norm', 'a_w_qkv', 'a_q_norm', 'a_k_norm', 'a_sinks', 'a_w_o', 'b_w_in', 'b_conv_w', 'b_w_out',
           'c_w_down', 'c_q_a_norm', 'c_kv_a_norm', 'c_w_q_up', 'c_w_kv_up', 'c_q_norm', 'c_k_norm', 'c_w_o', 'f_w_gate_up',
           'f_w_down']
SHARD_AXIS = {'a_w_qkv': 2, 'a_w_o': 1, 'b_w_in': 2, 'b_conv_w': 2, 'b_w_out': 1, 'c_w_down': 1, 'c_q_a_norm': 1,
              'c_kv_a_norm': 1, 'c_w_q_up': 2, 'c_w_kv_up': 2, 'c_w_o': 1, 'f_w_gate_up': 2, 'f_w_down': 1}
SHARDED = [n for n in WEIGHTS if n in SHARD_AXIS]
REPLICATED = [n for n in WEIGHTS if n not in SHARD_AXIS]
GATHER_F32 = ['b_conv_w', 'c_q_a_norm', 'c_kv_a_norm']
GATHER_BF16 = [n for n in SHARDED if n not in GATHER_F32]


def _params(semantics=None):
    return pltpu.CompilerParams(dimension_semantics=semantics, vmem_limit_bytes=VMEM_LIMIT_BYTES)


def _div_tile(n, cap, mult=LANES):
    best = None
    t = mult
    while t <= min(n, cap):
        if n % t == 0:
            best = t
        t += mult
    return n if best is None else best


def _exchange(src, scatter, name):
    block = src.shape[1:] if scatter else src.shape

    def body(src_ref, out_ref, send_sems, recv_sems, local_sem):
        x, y, c = lax.axis_index("x"), lax.axis_index("y"), lax.axis_index("c")
        me = 4 * x + 2 * y + c

        def piece(idx):
            return src_ref.at[idx] if scatter else src_ref

        local = pltpu.make_async_copy(piece(me), out_ref.at[me], local_sem)
        local.start()
        copies = []
        for r in range(1, N_DEV):
            px = 1 - x if (r >> 2) & 1 else x
            py = 1 - y if (r >> 1) & 1 else y
            pc = 1 - c if r & 1 else c
            cp = pltpu.make_async_remote_copy(
                src_ref=piece(4 * px + 2 * py + pc), dst_ref=out_ref.at[me],
                send_sem=send_sems.at[r - 1], recv_sem=recv_sems.at[r - 1],
                device_id=(px, py, pc), device_id_type=pl.DeviceIdType.MESH)
            cp.start()
            copies.append(cp)
        for cp in copies:
            cp.wait()
        local.wait()

    return pl.pallas_call(
        body, name=name,
        out_shape=jax.ShapeDtypeStruct((N_DEV,) + tuple(block), src.dtype),
        in_specs=[pl.BlockSpec(memory_space=pl.ANY)],
        out_specs=pl.BlockSpec(memory_space=pl.ANY),
        scratch_shapes=[pltpu.SemaphoreType.DMA((N_DEV - 1,)), pltpu.SemaphoreType.DMA((N_DEV - 1,)),
                        pltpu.SemaphoreType.DMA(())],
    )(src)


def _matmul(a, b, mode, name, out_dtype=F32, residual=None):
    if mode == 'nn':
        (m, k), (k2, n) = a.shape, b.shape
    elif mode == 'nt':
        (m, k), (n, k2) = a.shape, b.shape
    else:
        (k, m), (k2, n) = a.shape, b.shape
    assert k == k2, (name, a.shape, b.shape, mode)
    if mode == 'tn':
        tm, tk = _div_tile(m, 1408), _div_tile(k, 512, 16)
    else:
        tm, tk = _div_tile(m, 512, 16), _div_tile(k, 1536)
    tn = _div_tile(n, 1408)
    nk = k // tk
    dims = {'nn': (((1,), (0,)), ((), ())), 'nt': (((1,), (1,)), ((), ())), 'tn': (((0,), (0,)), ((), ()))}[mode]

    def body(a_ref, b_ref, *rest):
        if residual is None:
            o_ref, acc = rest
        else:
            r_ref, o_ref, acc = rest
        kk = pl.program_id(2)

        @pl.when(kk == 0)
        def _():
            acc[...] = jnp.zeros_like(acc)

        acc[...] += lax.dot_general(a_ref[...].astype(BF16), b_ref[...].astype(BF16), dims,
                                    preferred_element_type=F32)

        @pl.when(kk == nk - 1)
        def _():
            r = acc[...]
            if residual is not None:
                r = r + r_ref[...]
            o_ref[...] = r.astype(out_dtype)

    a_spec = pl.BlockSpec((tk, tm), lambda i, j, kk: (kk, i)) if mode == 'tn' else pl.BlockSpec((tm, tk), lambda i, j, kk: (i, kk))
    b_spec = pl.BlockSpec((tn, tk), lambda i, j, kk: (j, kk)) if mode == 'nt' else pl.BlockSpec((tk, tn), lambda i, j, kk: (kk, j))
    o_spec = pl.BlockSpec((tm, tn), lambda i, j, kk: (i, j))
    in_specs, operands = [a_spec, b_spec], [a, b]
    if residual is not None:
        in_specs.append(o_spec)
        operands.append(residual)
    return pl.pallas_call(
        body, name=name, out_shape=jax.ShapeDtypeStruct((m, n), out_dtype),
        grid=(m // tm, n // tn, nk), in_specs=in_specs, out_specs=o_spec,
        scratch_shapes=[pltpu.VMEM((tm, tn), F32)],
        compiler_params=_params(("parallel", "parallel", "arbitrary")),
    )(*operands)


def _row_spec(tm, cols):
    return pl.BlockSpec((tm, cols), lambda i: (i, 0))


def _const_spec(shape):
    return pl.BlockSpec(shape, lambda i: tuple(0 for _ in shape))


def _accumulate(ref, value, step):
    @pl.when(step == 0)
    def _():
        ref[...] = value

    @pl.when(step > 0)
    def _():
        ref[...] += value


def _rstd(x):
    return lax.rsqrt(jnp.mean(x * x, axis=-1, keepdims=True) + EPS)


def _norm_bwd(x, g, dout):
    xn = x * _rstd(x)
    dg = jnp.sum(dout * xn, axis=0, keepdims=True)
    dxn = dout * g
    dx = _rstd(x) * (dxn - xn * jnp.mean(dxn * xn, axis=-1, keepdims=True))
    return dx, dg


def _rmsnorm_fwd(x, g, name):
    t, d = x.shape
    tm = _div_tile(t, 512, 16)

    def body(x_ref, g_ref, o_ref):
        xv = x_ref[...]
        o_ref[...] = (xv * _rstd(xv) * g_ref[...]).astype(BF16)

    return pl.pallas_call(
        body, name=name, out_shape=jax.ShapeDtypeStruct((t, d), BF16), grid=(t // tm,),
        in_specs=[_row_spec(tm, d), _const_spec((1, d))], out_specs=_row_spec(tm, d),
        compiler_params=_params(("parallel",)),
    )(x, g)


def _rmsnorm_bwd(x, g, dh, dres, name):
    t, d = x.shape
    tm = _div_tile(t, 512, 8)

    def body(x_ref, g_ref, dh_ref, dres_ref, dx_ref, dg_ref):
        dx, dg = _norm_bwd(x_ref[...], g_ref[...], dh_ref[...])
        dx_ref[...] = dres_ref[...] + dx
        _accumulate(dg_ref, dg, pl.program_id(0))

    return pl.pallas_call(
        body, name=name,
        out_shape=(jax.ShapeDtypeStruct((t, d), F32), jax.ShapeDtypeStruct((1, d), F32)), grid=(t // tm,),
        in_specs=[_row_spec(tm, d), _const_spec((1, d)), _row_spec(tm, d), _row_spec(tm, d)],
        out_specs=(_row_spec(tm, d), _const_spec((1, d))),
        compiler_params=_params(("arbitrary",)),
    )(x, g, dh, dres)


def _sigmoid(x):
    return 1.0 / (1.0 + jnp.exp(-x))


def _swiglu_fwd(gu, name):
    t, f2 = gu.shape
    f = f2 // 2
    tm = _div_tile(t, 256, 16)

    def body(gu_ref, o_ref):
        gate, up = gu_ref[:, :f], gu_ref[:, f:]
        o_ref[...] = (gate * _sigmoid(gate) * up).astype(BF16)

    return pl.pallas_call(
        body, name=name, out_shape=jax.ShapeDtypeStruct((t, f), BF16), grid=(t // tm,),
        in_specs=[_row_spec(tm, f2)], out_specs=_row_spec(tm, f),
        compiler_params=_params(("parallel",)),
    )(gu)


def _swiglu_bwd(gu, da, name):
    t, f2 = gu.shape
    f = f2 // 2
    tm = _div_tile(t, 256, 16)

    def body(gu_ref, da_ref, o_ref):
        gate, up, dav = gu_ref[:, :f], gu_ref[:, f:], da_ref[...]
        sig = _sigmoid(gate)
        o_ref[:, :f] = (dav * up * (sig * (1.0 + gate * (1.0 - sig)))).astype(BF16)
        o_ref[:, f:] = (dav * (gate * sig)).astype(BF16)

    return pl.pallas_call(
        body, name=name, out_shape=jax.ShapeDtypeStruct((t, f2), BF16), grid=(t // tm,),
        in_specs=[_row_spec(tm, f2), _row_spec(tm, f)], out_specs=_row_spec(tm, f2),
        compiler_params=_params(("parallel",)),
    )(gu, da)


def _loss_head(y, target, name):
    t, d = y.shape
    tm = _div_tile(t, 512, 8)

    def body(y_ref, t_ref, loss_ref, dy_ref):
        diff = y_ref[...] - t_ref[...]
        dy_ref[...] = diff * (1.0 / d)
        part = jnp.sum(jnp.sum(diff * diff, axis=1, keepdims=True), axis=0, keepdims=True) * (0.5 / d)
        _accumulate(loss_ref, part, pl.program_id(0))

    return pl.pallas_call(
        body, name=name,
        out_shape=(jax.ShapeDtypeStruct((1, 1), F32), jax.ShapeDtypeStruct((t, d), F32)), grid=(t // tm,),
        in_specs=[_row_spec(tm, d), _row_spec(tm, d)], out_specs=(_const_spec((1, 1)), _row_spec(tm, d)),
        compiler_params=_params(("arbitrary",)),
    )(y, target)


HALO = 8


def _shift_down(z, k, halo_rows):
    tm = z.shape[0]
    row = lax.broadcasted_iota(jnp.int32, z.shape, 0)
    out = pltpu.roll(z, k, 0)
    for j in range(k):
        out = jnp.where(row == j, halo_rows[HALO - k + j:HALO - k + j + 1, :], out)
    return out


def _shift_up(z, k, halo_rows):
    tm = z.shape[0]
    row = lax.broadcasted_iota(jnp.int32, z.shape, 0)
    out = pltpu.roll(z, tm - k, 0)
    for j in range(k):
        out = jnp.where(row == tm - k + j, halo_rows[j:j + 1, :], out)
    return out


def _sconv_specs(t, tm, cols):
    per = tm // HALO
    last = t // HALO - 1
    cur = pl.BlockSpec((tm, cols), lambda i: (i, 0))
    prev = pl.BlockSpec((HALO, cols), lambda i: (jnp.maximum(i * per - 1, 0), 0))
    nxt = pl.BlockSpec((HALO, cols), lambda i: (jnp.minimum((i + 1) * per, last), 0))
    return cur, prev, nxt


def _sconv_fwd(bcu, conv_w, name):
    t, d3 = bcu.shape
    d = d3 // 3
    tm = _div_tile(t, 256, 16)
    cur, prev, _ = _sconv_specs(t, tm, d3)

    def body(cur_ref, prev_ref, w_ref, o_ref):
        i = pl.program_id(0)
        z = cur_ref[:, d:2 * d] * cur_ref[:, 2 * d:]
        zp = prev_ref[:, d:2 * d] * prev_ref[:, 2 * d:] * (i > 0).astype(F32)
        y = w_ref[0:1, :] * _shift_down(z, 2, zp) + w_ref[1:2, :] * _shift_down(z, 1, zp) + w_ref[2:3, :] * z
        o_ref[...] = (cur_ref[:, :d] * y).astype(BF16)

    return pl.pallas_call(
        body, name=name, out_shape=jax.ShapeDtypeStruct((t, d), BF16), grid=(t // tm,),
        in_specs=[cur, prev, _const_spec((3, d))], out_specs=_row_spec(tm, d),
        compiler_params=_params(("parallel",)),
    )(bcu, bcu, conv_w)


def _sconv_bwd(bcu, dout, conv_w, name):
    t, d3 = bcu.shape
    d = d3 // 3
    tm = _div_tile(t, 256, 16)
    cur, prev, nxt = _sconv_specs(t, tm, d3)
    dcur, _, dnxt = _sconv_specs(t, tm, d)
    n_tiles = t // tm

    def body(cur_ref, prev_ref, nxt_ref, do_ref, don_ref, w_ref, o_ref, dw_ref):
        i = pl.program_id(0)
        b, cg, u = cur_ref[:, :d], cur_ref[:, d:2 * d], cur_ref[:, 2 * d:]
        z = cg * u
        zp = prev_ref[:, d:2 * d] * prev_ref[:, 2 * d:] * (i > 0).astype(F32)
        z1, z2 = _shift_down(z, 1, zp), _shift_down(z, 2, zp)
        w0, w1, w2 = w_ref[0:1, :], w_ref[1:2, :], w_ref[2:3, :]
        y = w0 * z2 + w1 * z1 + w2 * z
        dov = do_ref[...]
        dy = dov * b
        dyn = don_ref[...] * nxt_ref[:, :d] * (i < n_tiles - 1).astype(F32)
        dz = w2 * dy + w1 * _shift_up(dy, 1, dyn) + w0 * _shift_up(dy, 2, dyn)
        o_ref[:, :d] = (dov * y).astype(BF16)
        o_ref[:, d:2 * d] = (dz * u).astype(BF16)
        o_ref[:, 2 * d:] = (dz * cg).astype(BF16)
        dw = jnp.concatenate([jnp.sum(dy * z2, axis=0, keepdims=True), jnp.sum(dy * z1, axis=0, keepdims=True),
                              jnp.sum(dy * z, axis=0, keepdims=True)], axis=0)
        _accumulate(dw_ref, dw, i)

    return pl.pallas_call(
        body, name=name,
        out_shape=(jax.ShapeDtypeStruct((t, d3), BF16), jax.ShapeDtypeStruct((3, d), F32)), grid=(n_tiles,),
        in_specs=[cur, prev, nxt, dcur, dnxt, _const_spec((3, d))],
        out_specs=(_row_spec(tm, d3), _const_spec((3, d))),
        compiler_params=_params(("arbitrary",)),
    )(bcu, bcu, bcu, dout, dout, conv_w)


def _rope_tables(positions, rot, lead, trail):
    inv_freq = ROPE_THETA ** (-jnp.arange(0, rot, 2, dtype=F32) / rot)
    ang = positions.astype(F32)[:, None] * inv_freq
    cos, sin = jnp.cos(ang), jnp.sin(ang)
    t = positions.shape[0]
    cos_full = jnp.concatenate([jnp.ones((t, lead), F32), cos, cos, jnp.ones((t, trail), F32)], axis=1)
    sin_full = jnp.concatenate([jnp.zeros((t, lead), F32), -sin, sin, jnp.zeros((t, trail), F32)], axis=1)
    return cos_full, sin_full


def _swap_halves(x, lead, rot):
    half = rot // 2
    rows, d = x.shape
    parts = []
    if lead:
        parts.append(jnp.zeros((rows, lead), x.dtype))
    parts += [x[:, lead + half:lead + rot], x[:, lead:lead + half]]
    if d - lead - rot:
        parts.append(jnp.zeros((rows, d - lead - rot), x.dtype))
    return jnp.concatenate(parts, axis=1)


def _head_fwd(x, g, cos, sin, lead, rot):
    n = x * _rstd(x) * g
    return n * cos + _swap_halves(n, lead, rot) * sin


def _head_bwd(x, g, cos, sin, dout, lead, rot):
    dn = dout * cos + _swap_halves(dout * sin, lead, rot)
    return _norm_bwd(x, g, dn)


A_Q_COLS = A_HEADS * A_HEAD_DIM
A_KV_COLS = A_KV_HEADS * A_HEAD_DIM
A_COLS = A_Q_COLS + 2 * A_KV_COLS
A_SCALE = A_HEAD_DIM ** -0.5


def _swa_prep_fwd(qkv, q_norm, k_norm, cos, sin, name):
    t = qkv.shape[0]
    tm = _div_tile(t, 256, 16)
    hd = A_HEAD_DIM

    def body(x_ref, gq_ref, gk_ref, cos_ref, sin_ref, o_ref):
        cosv, sinv = cos_ref[...], sin_ref[...]
        for h in range(A_HEADS + A_KV_HEADS):
            g = gq_ref[...] if h < A_HEADS else gk_ref[...]
            o_ref[:, h * hd:(h + 1) * hd] = _head_fwd(x_ref[:, h * hd:(h + 1) * hd], g, cosv, sinv, 0, A_ROT_DIM).astype(BF16)
        o_ref[:, A_Q_COLS + A_KV_COLS:] = x_ref[:, A_Q_COLS + A_KV_COLS:].astype(BF16)

    return pl.pallas_call(
        body, name=name, out_shape=jax.ShapeDtypeStruct((t, A_COLS), BF16), grid=(t // tm,),
        in_specs=[_row_spec(tm, A_COLS), _const_spec((1, hd)), _const_spec((1, hd)), _row_spec(tm, hd), _row_spec(tm, hd)],
        out_specs=_row_spec(tm, A_COLS), compiler_params=_params(("parallel",)),
    )(qkv, q_norm, k_norm, cos, sin)


def _swa_prep_bwd(qkv, dqkv_r, q_norm, k_norm, cos, sin, name):
    t = qkv.shape[0]
    tm = _div_tile(t, 256, 16)
    hd = A_HEAD_DIM

    def body(x_ref, d_ref, gq_ref, gk_ref, cos_ref, sin_ref, o_ref, dgq_ref, dgk_ref):
        cosv, sinv = cos_ref[...], sin_ref[...]
        dgq = jnp.zeros((1, hd), F32)
        dgk = jnp.zeros((1, hd), F32)
        for h in range(A_HEADS + A_KV_HEADS):
            sl = slice(h * hd, (h + 1) * hd)
            g = gq_ref[...] if h < A_HEADS else gk_ref[...]
            dx, dg = _head_bwd(x_ref[:, sl], g, cosv, sinv, d_ref[:, sl], 0, A_ROT_DIM)
            o_ref[:, sl] = dx.astype(BF16)
            if h < A_HEADS:
                dgq = dgq + dg
            else:
                dgk = dgk + dg
        o_ref[:, A_Q_COLS + A_KV_COLS:] = d_ref[:, A_Q_COLS + A_KV_COLS:].astype(BF16)
        _accumulate(dgq_ref, dgq, pl.program_id(0))
        _accumulate(dgk_ref, dgk, pl.program_id(0))

    return pl.pallas_call(
        body, name=name,
        out_shape=(jax.ShapeDtypeStruct((t, A_COLS), BF16), jax.ShapeDtypeStruct((1, hd), F32),
                   jax.ShapeDtypeStruct((1, hd), F32)),
        grid=(t // tm,),
        in_specs=[_row_spec(tm, A_COLS), _row_spec(tm, A_COLS), _const_spec((1, hd)), _const_spec((1, hd)),
                  _row_spec(tm, hd), _row_spec(tm, hd)],
        out_specs=(_row_spec(tm, A_COLS), _const_spec((1, hd)), _const_spec((1, hd))),
        compiler_params=_params(("arbitrary",)),
    )(qkv, dqkv_r, q_norm, k_norm, cos, sin)


def _group_rows(ref, k, width=A_HEAD_DIM, base=0):
    return jnp.concatenate([ref[:, base + (A_GROUP * k + g) * width:base + (A_GROUP * k + g + 1) * width]
                            for g in range(A_GROUP)], axis=0)


def _group_column(ref, k, rows):
    cols = []
    for g in range(A_GROUP):
        h = A_GROUP * k + g
        col = ref[:, h:h + 1]
        cols.append(jnp.broadcast_to(col, (rows, 1)) if col.shape[0] == 1 else col)
    return jnp.concatenate(cols, axis=0)


def _swa_fwd(qkv_r, sinks, name):
    t = qkv_r.shape[0]
    blk = A_WINDOW
    nb = t // blk
    hd = A_HEAD_DIM
    kv_block = A_Q_COLS // (2 * A_KV_COLS)

    def body(q_ref, kvc_ref, kvp_ref, s_ref, o_ref, lse_ref):
        n = pl.program_id(0)
        shape = (A_GROUP * blk, 2 * blk)
        qpos = lax.broadcasted_iota(jnp.int32, shape, 0) & (blk - 1)
        col = lax.broadcasted_iota(jnp.int32, shape, 1)
        delta = qpos + blk - col
        valid = (delta >= 0) & (delta < A_WINDOW) & ((col >= blk) | (n > 0))
        for k in range(A_KV_HEADS):
            qg = _group_rows(q_ref, k)
            kw = jnp.concatenate([kvp_ref[:, k * hd:(k + 1) * hd], kvc_ref[:, k * hd:(k + 1) * hd]], axis=0)
            vw = jnp.concatenate([kvp_ref[:, A_KV_COLS + k * hd:A_KV_COLS + (k + 1) * hd],
                                  kvc_ref[:, A_KV_COLS + k * hd:A_KV_COLS + (k + 1) * hd]], axis=0)
            s = lax.dot_general(qg, kw, (((1,), (1,)), ((), ())), preferred_element_type=F32) * A_SCALE
            s = jnp.where(valid, s, NEG)
            sink = _group_column(s_ref, k, blk)
            m = jnp.maximum(jnp.max(s, axis=-1, keepdims=True), sink)
            p = jnp.exp(s - m)
            denom = jnp.sum(p, axis=-1, keepdims=True) + jnp.exp(sink - m)
            p = p / denom
            o = jnp.dot(p.astype(BF16), vw, preferred_element_type=F32)
            lse = m + jnp.log(denom)
            for g in range(A_GROUP):
                h = A_GROUP * k + g
                o_ref[:, h * hd:(h + 1) * hd] = o[g * blk:(g + 1) * blk].astype(BF16)
                lse_ref[:, h:h + 1] = lse[g * blk:(g + 1) * blk]

    return pl.pallas_call(
        body, name=name,
        out_shape=(jax.ShapeDtypeStruct((t, A_Q_COLS), BF16), jax.ShapeDtypeStruct((t, A_HEADS), F32)), grid=(nb,),
        in_specs=[pl.BlockSpec((blk, A_Q_COLS), lambda n: (n, 0)),
                  pl.BlockSpec((blk, 2 * A_KV_COLS), lambda n: (n, kv_block)),
                  pl.BlockSpec((blk, 2 * A_KV_COLS), lambda n: (jnp.maximum(n - 1, 0), kv_block)),
                  _const_spec((1, A_HEADS))],
        out_specs=(pl.BlockSpec((blk, A_Q_COLS), lambda n: (n, 0)), pl.BlockSpec((blk, A_HEADS), lambda n: (n, 0))),
        compiler_params=_params(("parallel",)),
    )(qkv_r, qkv_r, qkv_r, sinks)


def _swa_bwd(qkv_r, o, lse, do, sinks, name):
    t = qkv_r.shape[0]
    blk = A_WINDOW
    nb = t // blk
    hd = A_HEAD_DIM
    kv_block = A_Q_COLS // (2 * A_KV_COLS)
    rows = A_GROUP * blk

    def nxt(n):
        return jnp.minimum(n + 1, nb - 1)

    def body(qc_ref, qn_ref, kvc_ref, kvp_ref, doc_ref, don_ref, oc_ref, on_ref, lc_ref, ln_ref, s_ref, dx_ref, ds_ref):
        n = pl.program_id(0)
        shape = (2 * rows, 2 * blk)
        row = lax.broadcasted_iota(jnp.int32, shape, 0)
        col = lax.broadcasted_iota(jnp.int32, shape, 1)
        is_next = row >= rows
        delta = jnp.where(is_next, blk, 0) + blk + (row & (blk - 1)) - col
        valid = ((delta >= 0) & (delta < A_WINDOW) & ((col >= blk) | (n > 0)) & (jnp.logical_not(is_next) | (n < nb - 1)))
        dsink_cols = []
        for k in range(A_KV_HEADS):
            qs = jnp.concatenate([_group_rows(qc_ref, k), _group_rows(qn_ref, k)], axis=0)
            dos = jnp.concatenate([_group_rows(doc_ref, k), _group_rows(don_ref, k)], axis=0)
            os_ = jnp.concatenate([_group_rows(oc_ref, k), _group_rows(on_ref, k)], axis=0).astype(F32)
            lses = jnp.concatenate([_group_column(lc_ref, k, blk), _group_column(ln_ref, k, blk)], axis=0)
            kw = jnp.concatenate([kvp_ref[:, k * hd:(k + 1) * hd], kvc_ref[:, k * hd:(k + 1) * hd]], axis=0)
            vw = jnp.concatenate([kvp_ref[:, A_KV_COLS + k * hd:A_KV_COLS + (k + 1) * hd],
                                  kvc_ref[:, A_KV_COLS + k * hd:A_KV_COLS + (k + 1) * hd]], axis=0)
            s = lax.dot_general(qs, kw, (((1,), (1,)), ((), ())), preferred_element_type=F32) * A_SCALE
            p = jnp.exp(jnp.where(valid, s - lses, NEG))
            dos_b = dos.astype(BF16)
            dp = lax.dot_general(dos_b, vw, (((1,), (1,)), ((), ())), preferred_element_type=F32)
            dlt = jnp.sum(dos * os_, axis=-1, keepdims=True)
            ds = p * (dp - dlt)
            dq = jnp.dot(ds[:rows].astype(BF16), kw, preferred_element_type=F32) * A_SCALE
            dk = lax.dot_general(ds[:, blk:].astype(BF16), qs, (((0,), (0,)), ((), ())), preferred_element_type=F32) * A_SCALE
            dv = lax.dot_general(p[:, blk:].astype(BF16), dos_b, (((0,), (0,)), ((), ())), preferred_element_type=F32)
            for g in range(A_GROUP):
                h = A_GROUP * k + g
                dx_ref[:, h * hd:(h + 1) * hd] = dq[g * blk:(g + 1) * blk]
            dx_ref[:, A_Q_COLS + k * hd:A_Q_COLS + (k + 1) * hd] = dk
            dx_ref[:, A_Q_COLS + A_KV_COLS + k * hd:A_Q_COLS + A_KV_COLS + (k + 1) * hd] = dv
            sink = _group_column(s_ref, k, blk)
            contrib = -jnp.exp(sink - lses[:rows]) * dlt[:rows]
            for g in range(A_GROUP):
                dsink_cols.append(jnp.sum(contrib[g * blk:(g + 1) * blk], axis=0, keepdims=True))
        _accumulate(ds_ref, jnp.concatenate(dsink_cols, axis=1), n)

    q_spec = lambda f: pl.BlockSpec((blk, A_Q_COLS), lambda n: (f(n), 0))
    l_spec = lambda f: pl.BlockSpec((blk, A_HEADS), lambda n: (f(n), 0))
    same = lambda n: n
    return pl.pallas_call(
        body, name=name,
        out_shape=(jax.ShapeDtypeStruct((t, A_COLS), F32), jax.ShapeDtypeStruct((1, A_HEADS), F32)), grid=(nb,),
        in_specs=[q_spec(same), q_spec(nxt),
                  pl.BlockSpec((blk, 2 * A_KV_COLS), lambda n: (n, kv_block)),
                  pl.BlockSpec((blk, 2 * A_KV_COLS), lambda n: (jnp.maximum(n - 1, 0), kv_block)),
                  q_spec(same), q_spec(nxt), q_spec(same), q_spec(nxt), l_spec(same), l_spec(nxt),
                  _const_spec((1, A_HEADS))],
        out_specs=(pl.BlockSpec((blk, A_COLS), lambda n: (n, 0)), _const_spec((1, A_HEADS))),
        compiler_params=_params(("arbitrary",)),
    )(qkv_r, qkv_r, qkv_r, qkv_r, do, do, o, o, lse, lse, sinks)


C_DOWN_COLS = C_Q_RANK + C_KV_RANK + C_ROPE
C_Q_COLS = C_HEADS * C_QK
C_KV_COLS = C_HEADS * (C_NOPE + C_V)
C_O_COLS = C_HEADS * C_V
C_PAD = LANES
C_SCALE = C_QK ** -0.5
LOG2E = 1.4426950408889634
LN2 = 0.6931471805599453
C_Q_SCALE = C_SCALE * LOG2E
C_PAIR = 2


def _mla_latent_fwd(down, q_a_norm, kv_a_norm, name):
    t = down.shape[0]
    tm = _div_tile(t, 512, 16)

    def body(x_ref, gq_ref, gk_ref, cq_ref, ckv_ref):
        cq, ckv = x_ref[:, :C_Q_RANK], x_ref[:, C_Q_RANK:C_Q_RANK + C_KV_RANK]
        cq_ref[...] = (cq * _rstd(cq) * gq_ref[...]).astype(BF16)
        ckv_ref[...] = (ckv * _rstd(ckv) * gk_ref[...]).astype(BF16)

    return pl.pallas_call(
        body, name=name,
        out_shape=(jax.ShapeDtypeStruct((t, C_Q_RANK), BF16), jax.ShapeDtypeStruct((t, C_KV_RANK), BF16)), grid=(t // tm,),
        in_specs=[_row_spec(tm, C_DOWN_COLS), _const_spec((1, C_Q_RANK)), _const_spec((1, C_KV_RANK))],
        out_specs=(_row_spec(tm, C_Q_RANK), _row_spec(tm, C_KV_RANK)), compiler_params=_params(("parallel",)),
    )(down, q_a_norm, kv_a_norm)


def _mla_latent_bwd(down, dcq, dckv, dkrope, q_a_norm, kv_a_norm, name):
    t = down.shape[0]
    tm = _div_tile(t, 512, 16)

    def body(x_ref, dcq_ref, dckv_ref, dkr_ref, gq_ref, gk_ref, o_ref, dgq_ref, dgk_ref):
        dq, dgq = _norm_bwd(x_ref[:, :C_Q_RANK], gq_ref[...], dcq_ref[...])
        dkv, dgk = _norm_bwd(x_ref[:, C_Q_RANK:C_Q_RANK + C_KV_RANK], gk_ref[...], dckv_ref[...])
        o_ref[...] = jnp.concatenate([dq, dkv, dkr_ref[...]], axis=1).astype(BF16)
        _accumulate(dgq_ref, dgq, pl.program_id(0))
        _accumulate(dgk_ref, dgk, pl.program_id(0))

    return pl.pallas_call(
        body, name=name,
        out_shape=(jax.ShapeDtypeStruct((t, C_DOWN_COLS), BF16), jax.ShapeDtypeStruct((1, C_Q_RANK), F32),
                   jax.ShapeDtypeStruct((1, C_KV_RANK), F32)),
        grid=(t // tm,),
        in_specs=[_row_spec(tm, C_DOWN_COLS), _row_spec(tm, C_Q_RANK), _row_spec(tm, C_KV_RANK), _row_spec(tm, C_ROPE),
                  _const_spec((1, C_Q_RANK)), _const_spec((1, C_KV_RANK))],
        out_specs=(_row_spec(tm, C_DOWN_COLS), _const_spec((1, C_Q_RANK)), _const_spec((1, C_KV_RANK))),
        compiler_params=_params(("arbitrary",)),
    )(down, dcq, dckv, dkrope, q_a_norm, kv_a_norm)


def _head_major_spec(tm, width):
    return pl.BlockSpec((C_HEADS, tm, width), lambda i: (0, i, 0))


def _mla_qk_fwd(qw, kvw, down, q_norm, k_norm, cos, sin, name):
    t = qw.shape[0]
    tm = _div_tile(t, 256, 16)
    kvd = C_NOPE + C_V

    def body(q_ref, kv_ref, dn_ref, gq_ref, gk_ref, cos_ref, sin_ref, qo_ref, ko_ref, vo_ref):
        cosv, sinv = cos_ref[...], sin_ref[...]
        k_rope = dn_ref[:, C_Q_RANK + C_KV_RANK:]
        pad = jnp.zeros((tm, C_PAD - C_QK), F32)
        one_then_zeros = (lax.broadcasted_iota(jnp.int32, (tm, C_PAD - C_V), 1) == 0).astype(F32)
        for h in range(C_HEADS):
            qh = _head_fwd(q_ref[:, h * C_QK:(h + 1) * C_QK], gq_ref[...], cosv, sinv, C_NOPE, C_ROPE)
            kx = jnp.concatenate([kv_ref[:, h * kvd:h * kvd + C_NOPE], k_rope], axis=1)
            kh = _head_fwd(kx, gk_ref[...], cosv, sinv, C_NOPE, C_ROPE)
            qo_ref[h] = jnp.concatenate([qh * C_Q_SCALE, pad], axis=1).astype(BF16)
            ko_ref[h] = jnp.concatenate([kh, pad], axis=1).astype(BF16)
            vo_ref[h] = jnp.concatenate([kv_ref[:, h * kvd + C_NOPE:(h + 1) * kvd], one_then_zeros], axis=1).astype(BF16)

    return pl.pallas_call(
        body, name=name,
        out_shape=(jax.ShapeDtypeStruct((C_HEADS, t, C_PAD), BF16), jax.ShapeDtypeStruct((C_HEADS, t, C_PAD), BF16),
                   jax.ShapeDtypeStruct((C_HEADS, t, C_PAD), BF16)),
        grid=(t // tm,),
        in_specs=[_row_spec(tm, C_Q_COLS), _row_spec(tm, C_KV_COLS), _row_spec(tm, C_DOWN_COLS), _const_spec((1, C_QK)),
                  _const_spec((1, C_QK)), _row_spec(tm, C_QK), _row_spec(tm, C_QK)],
        out_specs=(_head_major_spec(tm, C_PAD), _head_major_spec(tm, C_PAD), _head_major_spec(tm, C_PAD)),
        compiler_params=_params(("parallel",)),
    )(qw, kvw, down, q_norm, k_norm, cos, sin)


def _mla_qk_bwd(qw, kvw, down, dq, dk, dv, q_norm, k_norm, cos, sin, name):
    t = qw.shape[0]
    tm = _div_tile(t, 256, 16)
    kvd = C_NOPE + C_V

    def body(q_ref, kv_ref, dn_ref, dq_ref, dk_ref, dv_ref, gq_ref, gk_ref, cos_ref, sin_ref,
             dqw_ref, dkvw_ref, dkr_ref, dgq_ref, dgk_ref):
        cosv, sinv = cos_ref[...], sin_ref[...]
        k_rope = dn_ref[:, C_Q_RANK + C_KV_RANK:]
        dgq = jnp.zeros((1, C_QK), F32)
        dgk = jnp.zeros((1, C_QK), F32)
        dkr = jnp.zeros((tm, C_ROPE), F32)
        for h in range(C_HEADS):
            dxq, dg = _head_bwd(q_ref[:, h * C_QK:(h + 1) * C_QK], gq_ref[...], cosv, sinv, dq_ref[h][:, :C_QK], C_NOPE, C_ROPE)
            dgq = dgq + dg
            dqw_ref[:, h * C_QK:(h + 1) * C_QK] = dxq.astype(BF16)
            kx = jnp.concatenate([kv_ref[:, h * kvd:h * kvd + C_NOPE], k_rope], axis=1)
            dxk, dg = _head_bwd(kx, gk_ref[...], cosv, sinv, dk_ref[h][:, :C_QK], C_NOPE, C_ROPE)
            dgk = dgk + dg
            dkr = dkr + dxk[:, C_NOPE:]
            dkvw_ref[:, h * kvd:(h + 1) * kvd] = jnp.concatenate([dxk[:, :C_NOPE], dv_ref[h]], axis=1).astype(BF16)
        dkr_ref[...] = dkr
        _accumulate(dgq_ref, dgq, pl.program_id(0))
        _accumulate(dgk_ref, dgk, pl.program_id(0))

    return pl.pallas_call(
        body, name=name,
        out_shape=(jax.ShapeDtypeStruct((t, C_Q_COLS), BF16), jax.ShapeDtypeStruct((t, C_KV_COLS), BF16),
                   jax.ShapeDtypeStruct((t, C_ROPE), F32), jax.ShapeDtypeStruct((1, C_QK), F32),
                   jax.ShapeDtypeStruct((1, C_QK), F32)),
        grid=(t // tm,),
        in_specs=[_row_spec(tm, C_Q_COLS), _row_spec(tm, C_KV_COLS), _row_spec(tm, C_DOWN_COLS),
                  _head_major_spec(tm, C_PAD), _head_major_spec(tm, C_PAD), _head_major_spec(tm, C_V),
                  _const_spec((1, C_QK)), _const_spec((1, C_QK)), _row_spec(tm, C_QK), _row_spec(tm, C_QK)],
        out_specs=(_row_spec(tm, C_Q_COLS), _row_spec(tm, C_KV_COLS), _row_spec(tm, C_ROPE), _const_spec((1, C_QK)),
                   _const_spec((1, C_QK))),
        compiler_params=_params(("arbitrary",)),
    )(qw, kvw, down, dq, dk, dv, q_norm, k_norm, cos, sin)


def _causal_keep(rows, cols, row_offset=0, transposed=False):
    row = lax.broadcasted_iota(jnp.int32, (rows, cols), 0) + row_offset
    col = lax.broadcasted_iota(jnp.int32, (rows, cols), 1)
    return (row <= col) if transposed else (col <= row)


def _mla_fwd(q, k, v, name):
    _, t, _ = q.shape
    blk = min(MLA_FWD_BLOCK, t)
    nq = t // blk

    def body(q_ref, k_ref, v_ref, o_ref, lse_ref, m_sc, acc_sc):
        qi = pl.program_id(1)
        m_sc[...] = jnp.full_like(m_sc, NEG)
        acc_sc[...] = jnp.zeros_like(acc_sc)

        def step(ki, masked):
            rows = pl.ds(pl.multiple_of(ki * blk, blk), blk)
            for hh in range(C_PAIR):
                s = lax.dot_general(q_ref[hh], k_ref[hh, rows, :], (((1,), (1,)), ((), ())), preferred_element_type=F32)
                if masked:
                    s = jnp.where(_causal_keep(blk, blk), s, NEG)
                m_prev = m_sc[hh]
                m_new = jnp.maximum(m_prev, jnp.max(s, axis=-1, keepdims=True))
                p = jnp.exp2(s - m_new)
                acc_sc[hh] = jnp.exp2(m_prev - m_new) * acc_sc[hh] + jnp.dot(p.astype(BF16), v_ref[hh, rows, :],
                                                                                preferred_element_type=F32)
                m_sc[hh] = m_new

        def below_diagonal(ki, carry):
            step(ki, False)
            return carry

        lax.fori_loop(0, qi, below_diagonal, 0)
        step(qi, True)
        outs = []
        for hh in range(C_PAIR):
            denom = acc_sc[hh, :, C_V:C_V + 1]
            outs.append(acc_sc[hh, :, :C_V] / denom)
            lse_ref[hh] = m_sc[hh] + jnp.log(denom) * LOG2E
        o_ref[...] = jnp.concatenate(outs, axis=1).astype(BF16)

    whole = lambda hp, qi: (hp, 0, 0)
    return pl.pallas_call(
        body, name=name,
        out_shape=(jax.ShapeDtypeStruct((t, C_O_COLS), BF16), jax.ShapeDtypeStruct((C_HEADS, t, 1), F32)),
        grid=(C_HEADS // C_PAIR, nq),
        in_specs=[pl.BlockSpec((C_PAIR, blk, C_PAD), lambda hp, qi: (hp, qi, 0)),
                  pl.BlockSpec((C_PAIR, t, C_PAD), whole), pl.BlockSpec((C_PAIR, t, C_PAD), whole)],
        out_specs=(pl.BlockSpec((blk, C_PAIR * C_V), lambda hp, qi: (qi, hp)),
                   pl.BlockSpec((C_PAIR, blk, 1), lambda hp, qi: (hp, qi, 0))),
        scratch_shapes=[pltpu.VMEM((C_PAIR, blk, 1), F32), pltpu.VMEM((C_PAIR, blk, C_PAD), F32)],
        compiler_params=_params(("parallel", "arbitrary")),
    )(q, k, v)


def _mla_bwd_dq(q, k, v, do, o, lse, name):
    _, t, _ = q.shape
    blk = min(MLA_BLOCK, t)
    nq = t // blk

    def body(q_ref, k_ref, v_ref, do_ref, o_ref, lse_ref, dq_ref, dlt_ref, acc_sc, dob_sc):
        qi = pl.program_id(1)
        for hh in range(C_PAIR):
            do_h = do_ref[:, hh * C_V:(hh + 1) * C_V]
            dlt_ref[hh] = jnp.sum(do_h * o_ref[:, hh * C_V:(hh + 1) * C_V].astype(F32), axis=-1, keepdims=True)
            dob_sc[hh] = do_h.astype(BF16)
        acc_sc[...] = jnp.zeros_like(acc_sc)

        def step(ki, masked):
            rows = pl.ds(pl.multiple_of(ki * blk, blk), blk)
            for hh in range(C_PAIR):
                kb = k_ref[hh, rows, :]
                s = lax.dot_general(q_ref[hh], kb, (((1,), (1,)), ((), ())), preferred_element_type=F32)
                if masked:
                    s = jnp.where(_causal_keep(blk, blk), s, NEG)
                p = jnp.exp2(s - lse_ref[hh])
                dp = lax.dot_general(dob_sc[hh], v_ref[hh, rows, :C_V], (((1,), (1,)), ((), ())),
                                     preferred_element_type=F32)
                ds = p * (dp - dlt_ref[hh])
                acc_sc[hh] += jnp.dot(ds.astype(BF16), kb, preferred_element_type=F32)

        def below_diagonal(ki, carry):
            step(ki, False)
            return carry

        lax.fori_loop(0, qi, below_diagonal, 0)
        step(qi, True)
        dq_ref[...] = acc_sc[...] * C_SCALE

    whole = lambda hp, qi: (hp, 0, 0)
    qmap = lambda hp, qi: (hp, qi, 0)
    wide = lambda hp, qi: (qi, hp)
    return pl.pallas_call(
        body, name=name,
        out_shape=(jax.ShapeDtypeStruct((C_HEADS, t, C_PAD), F32), jax.ShapeDtypeStruct((C_HEADS, t, 1), F32)),
        grid=(C_HEADS // C_PAIR, nq),
        in_specs=[pl.BlockSpec((C_PAIR, blk, C_PAD), qmap), pl.BlockSpec((C_PAIR, t, C_PAD), whole),
                  pl.BlockSpec((C_PAIR, t, C_PAD), whole), pl.BlockSpec((blk, C_PAIR * C_V), wide),
                  pl.BlockSpec((blk, C_PAIR * C_V), wide), pl.BlockSpec((C_PAIR, blk, 1), qmap)],
        out_specs=(pl.BlockSpec((C_PAIR, blk, C_PAD), qmap), pl.BlockSpec((C_PAIR, blk, 1), qmap)),
        scratch_shapes=[pltpu.VMEM((C_PAIR, blk, C_PAD), F32), pltpu.VMEM((C_PAIR, blk, C_V), BF16)],
        compiler_params=_params(("parallel", "arbitrary")),
    )(q, k, v, do, o, lse)


def _mla_bwd_dkv(q, k, v, do_b, lse_rows, dlt_rows, name):
    _, t, _ = q.shape
    blk = min(MLA_BLOCK, t)
    nq = t // blk

    def body(q_ref, k_ref, v_ref, do_ref, lse_ref, dlt_ref, dk_ref, dv_ref, dk_sc, dv_sc):
        ki = pl.program_id(1)
        dk_sc[...] = jnp.zeros_like(dk_sc)
        dv_sc[...] = jnp.zeros_like(dv_sc)

        def step(qi, masked):
            rows = pl.ds(pl.multiple_of(qi * blk, blk), blk)
            for hh in range(C_PAIR):
                qb = q_ref[hh, rows, :]
                dob = do_ref[rows, hh * C_V:(hh + 1) * C_V]
                s = lax.dot_general(k_ref[hh], qb, (((1,), (1,)), ((), ())), preferred_element_type=F32)
                if masked:
                    s = jnp.where(_causal_keep(blk, blk, transposed=True), s, NEG)
                p = jnp.exp2(s - lse_ref[hh, qi])
                dp = lax.dot_general(v_ref[hh, :, :C_V], dob, (((1,), (1,)), ((), ())), preferred_element_type=F32)
                ds = p * (dp - dlt_ref[hh, qi])
                dv_sc[hh] += jnp.dot(p.astype(BF16), dob, preferred_element_type=F32)
                dk_sc[hh] += jnp.dot(ds.astype(BF16), qb, preferred_element_type=F32)

        def above_diagonal(qi, carry):
            step(qi, False)
            return carry

        step(ki, True)
        lax.fori_loop(ki + 1, nq, above_diagonal, 0)
        dk_ref[...] = dk_sc[...] * LN2
        dv_ref[...] = dv_sc[...]

    whole = lambda hp, ki: (hp, 0, 0)
    whole4 = lambda hp, ki: (hp, 0, 0, 0)
    kmap = lambda hp, ki: (hp, ki, 0)
    return pl.pallas_call(
        body, name=name,
        out_shape=(jax.ShapeDtypeStruct((C_HEADS, t, C_PAD), F32), jax.ShapeDtypeStruct((C_HEADS, t, C_V), F32)),
        grid=(C_HEADS // C_PAIR, nq),
        in_specs=[pl.BlockSpec((C_PAIR, t, C_PAD), whole), pl.BlockSpec((C_PAIR, blk, C_PAD), kmap),
                  pl.BlockSpec((C_PAIR, blk, C_PAD), kmap), pl.BlockSpec((t, C_PAIR * C_V), lambda hp, ki: (0, hp)),
                  pl.BlockSpec((C_PAIR, nq, 1, blk), whole4), pl.BlockSpec((C_PAIR, nq, 1, blk), whole4)],
        out_specs=(pl.BlockSpec((C_PAIR, blk, C_PAD), kmap), pl.BlockSpec((C_PAIR, blk, C_V), kmap)),
        scratch_shapes=[pltpu.VMEM((C_PAIR, blk, C_PAD), F32), pltpu.VMEM((C_PAIR, blk, C_V), F32)],
        compiler_params=_params(("parallel", "arbitrary")),
    )(q, k, v, do_b, lse_rows, dlt_rows)


def _adamw(parts, w, m, v, name):
    rows, cols = w.shape
    tm = _div_tile(rows, 128, 8)

    def body(p_ref, w_ref, m_ref, v_ref, g_ref, d_ref, nm_ref, nv_ref):
        g = p_ref[0]
        for j in range(1, N_DEV):
            g = g + p_ref[j]
        nm = ADAM_B1 * m_ref[...] + (1.0 - ADAM_B1) * g
        nv = ADAM_B2 * v_ref[...] + (1.0 - ADAM_B2) * jnp.square(g)
        m_hat = nm / (1.0 - ADAM_B1 ** ADAM_STEP)
        v_hat = nv / (1.0 - ADAM_B2 ** ADAM_STEP)
        g_ref[...] = g
        d_ref[...] = -ADAM_LR * (m_hat / (jnp.sqrt(v_hat) + ADAM_EPS) + ADAM_WD * w_ref[...])
        nm_ref[...] = nm
        nv_ref[...] = nv

    spec = _row_spec(tm, cols)
    return pl.pallas_call(
        body, name=name, out_shape=tuple(jax.ShapeDtypeStruct((rows, cols), F32) for _ in range(4)), grid=(rows // tm,),
        in_specs=[pl.BlockSpec((N_DEV, tm, cols), lambda i: (0, i, 0)), spec, spec, spec],
        out_specs=(spec, spec, spec, spec), compiler_params=_params(("parallel",)),
    )(parts, w, m, v)


def _pack(flat_pieces, dtype):
    flat = jnp.concatenate([p.reshape(-1).astype(dtype) for p in flat_pieces])
    unit = PACK_COLS * PACK_ROW_MULTIPLE
    padded = -(-flat.shape[0] // unit) * unit
    return jnp.pad(flat, (0, padded - flat.shape[0])).reshape(padded // PACK_COLS, PACK_COLS)


def _unpack(buf, shapes):
    flat = buf.reshape(-1)
    out, off = [], 0
    for shape in shapes:
        size = 1
        for s in shape:
            size *= s
        out.append(flat[off:off + size].reshape(shape))
        off += size
    return out


def _join_shards(stacked, axis):
    moved = jnp.moveaxis(stacked, 0, axis)
    shape = list(moved.shape)
    shape[axis:axis + 2] = [shape[axis] * shape[axis + 1]]
    return moved.reshape(shape)


def _split_shards(full, axis):
    shape = list(full.shape)
    shape[axis:axis + 1] = [N_DEV, shape[axis] // N_DEV]
    return jnp.moveaxis(full.reshape(shape), axis, 0)


def _gather_weights(local, names, dtype, name):
    shapes = [local[n].shape for n in names]
    got = _exchange(_pack([local[n] for n in names], dtype), False, name)
    per_dev = [_unpack(got[d], shapes) for d in range(N_DEV)]
    return {n: _join_shards(jnp.stack([per_dev[d][i] for d in range(N_DEV)]), SHARD_AXIS[n]) for i, n in enumerate(names)}


def _forward_backward(x, positions, target, w, rep):
    cos_a, sin_a = _rope_tables(positions, A_ROT_DIM, 0, A_HEAD_DIM - A_ROT_DIM)
    cos_c, sin_c = _rope_tables(positions, C_ROPE, C_NOPE, 0)
    saved = []
    for i in range(DEPTH):
        kind, j = i % N_MIXERS, i // N_MIXERS
        s = {'x': x}
        h1 = _rmsnorm_fwd(x, rep['mix_norm'][i:i + 1], f"mix_norm_fwd_{i}")
        s['h1'] = h1
        if kind == 0:
            s['qkv'] = _matmul(h1, w['a_w_qkv'][j], 'nn', f"a_qkv_{i}")
            s['qkv_r'] = _swa_prep_fwd(s['qkv'], rep['a_q_norm'][j:j + 1], rep['a_k_norm'][j:j + 1], cos_a, sin_a,
                                       f"a_prep_fwd_{i}")
            s['o'], s['lse'] = _swa_fwd(s['qkv_r'], rep['a_sinks'][j:j + 1], f"a_attn_fwd_{i}")
            x1 = _matmul(s['o'], w['a_w_o'][j], 'nn', f"a_out_{i}", residual=x)
        elif kind == 1:
            s['bcu'] = _matmul(h1, w['b_w_in'][j], 'nn', f"b_in_{i}")
            s['by'] = _sconv_fwd(s['bcu'], w['b_conv_w'][j], f"b_conv_fwd_{i}")
            x1 = _matmul(s['by'], w['b_w_out'][j], 'nn', f"b_out_{i}", residual=x)
        else:
            s['down'] = _matmul(h1, w['c_w_down'][j], 'nn', f"c_down_{i}")
            s['cq'], s['ckv'] = _mla_latent_fwd(s['down'], w['c_q_a_norm'][j:j + 1], w['c_kv_a_norm'][j:j + 1],
                                                f"c_latent_fwd_{i}")
            s['qw'] = _matmul(s['cq'], w['c_w_q_up'][j], 'nn', f"c_q_up_{i}")
            s['kvw'] = _matmul(s['ckv'], w['c_w_kv_up'][j], 'nn', f"c_kv_up_{i}")
            s['q'], s['k'], s['v'] = _mla_qk_fwd(s['qw'], s['kvw'], s['down'], rep['c_q_norm'][j:j + 1],
                                                 rep['c_k_norm'][j:j + 1], cos_c, sin_c, f"c_prep_fwd_{i}")
            s['o'], s['lse'] = _mla_fwd(s['q'], s['k'], s['v'], f"c_attn_fwd_{i}")
            x1 = _matmul(s['o'], w['c_w_o'][j], 'nn', f"c_out_{i}", residual=x)
        s['x1'] = x1
        s['h2'] = _rmsnorm_fwd(x1, rep['ffn_norm'][i:i + 1], f"ffn_norm_fwd_{i}")
        s['gu'] = _matmul(s['h2'], w['f_w_gate_up'][i], 'nn', f"f_gate_up_{i}")
        s['act'] = _swiglu_fwd(s['gu'], f"f_act_fwd_{i}")
        x = _matmul(s['act'], w['f_w_down'][i], 'nn', f"f_down_{i}", residual=x1)
        saved.append(s)

    loss, dx = _loss_head(x, target, "loss_head")

    per_layer = {n: {} for n in WEIGHTS}
    for i in reversed(range(DEPTH)):
        kind, j = i % N_MIXERS, i // N_MIXERS
        s = saved[i]
        per_layer['f_w_down'][i] = _matmul(s['act'], dx, 'tn', f"f_down_dw_{i}")
        dact = _matmul(dx, w['f_w_down'][i], 'nt', f"f_down_dx_{i}")
        dgu = _swiglu_bwd(s['gu'], dact, f"f_act_bwd_{i}")
        per_layer['f_w_gate_up'][i] = _matmul(s['h2'], dgu, 'tn', f"f_gate_up_dw_{i}")
        dh2 = _matmul(dgu, w['f_w_gate_up'][i], 'nt', f"f_gate_up_dx_{i}")
        dx1, per_layer['ffn_norm'][i] = _rmsnorm_bwd(s['x1'], rep['ffn_norm'][i:i + 1], dh2, dx, f"ffn_norm_bwd_{i}")
        if kind == 0:
            per_layer['a_w_o'][j] = _matmul(s['o'], dx1, 'tn', f"a_out_dw_{i}")
            do = _matmul(dx1, w['a_w_o'][j], 'nt', f"a_out_dx_{i}")
            dqkv_r, per_layer['a_sinks'][j] = _swa_bwd(s['qkv_r'], s['o'], s['lse'], do, rep['a_sinks'][j:j + 1],
                                                       f"a_attn_bwd_{i}")
            dqkv, per_layer['a_q_norm'][j], per_layer['a_k_norm'][j] = _swa_prep_bwd(
                s['qkv'], dqkv_r, rep['a_q_norm'][j:j + 1], rep['a_k_norm'][j:j + 1], cos_a, sin_a, f"a_prep_bwd_{i}")
            per_layer['a_w_qkv'][j] = _matmul(s['h1'], dqkv, 'tn', f"a_qkv_dw_{i}")
            dh1 = _matmul(dqkv, w['a_w_qkv'][j], 'nt', f"a_qkv_dx_{i}")
        elif kind == 1:
            per_layer['b_w_out'][j] = _matmul(s['by'], dx1, 'tn', f"b_out_dw_{i}")
            dby = _matmul(dx1, w['b_w_out'][j], 'nt', f"b_out_dx_{i}")
            dbcu, per_layer['b_conv_w'][j] = _sconv_bwd(s['bcu'], dby, w['b_conv_w'][j], f"b_conv_bwd_{i}")
            per_layer['b_w_in'][j] = _matmul(s['h1'], dbcu, 'tn', f"b_in_dw_{i}")
            dh1 = _matmul(dbcu, w['b_w_in'][j], 'nt', f"b_in_dx_{i}")
        else:
            per_layer['c_w_o'][j] = _matmul(s['o'], dx1, 'tn', f"c_out_dw_{i}")
            do = _matmul(dx1, w['c_w_o'][j], 'nt', f"c_out_dx_{i}")
            dq, dlt = _mla_bwd_dq(s['q'], s['k'], s['v'], do, s['o'], s['lse'], f"c_attn_bwd_dq_{i}")
            blk = min(MLA_BLOCK, do.shape[0])
            as_rows = lambda col: col.reshape(C_HEADS, do.shape[0] // blk, 1, blk)
            dk, dv = _mla_bwd_dkv(s['q'], s['k'], s['v'], do.astype(BF16), as_rows(s['lse']), as_rows(dlt),
                                  f"c_attn_bwd_dkv_{i}")
            dqw, dkvw, dkrope, per_layer['c_q_norm'][j], per_layer['c_k_norm'][j] = _mla_qk_bwd(
                s['qw'], s['kvw'], s['down'], dq, dk, dv, rep['c_q_norm'][j:j + 1], rep['c_k_norm'][j:j + 1], cos_c, sin_c,
                f"c_prep_bwd_{i}")
            per_layer['c_w_q_up'][j] = _matmul(s['cq'], dqw, 'tn', f"c_q_up_dw_{i}")
            dcq = _matmul(dqw, w['c_w_q_up'][j], 'nt', f"c_q_up_dx_{i}")
            per_layer['c_w_kv_up'][j] = _matmul(s['ckv'], dkvw, 'tn', f"c_kv_up_dw_{i}")
            dckv = _matmul(dkvw, w['c_w_kv_up'][j], 'nt', f"c_kv_up_dx_{i}")
            ddown, per_layer['c_q_a_norm'][j], per_layer['c_kv_a_norm'][j] = _mla_latent_bwd(
                s['down'], dcq, dckv, dkrope, w['c_q_a_norm'][j:j + 1], w['c_kv_a_norm'][j:j + 1], f"c_latent_bwd_{i}")
            per_layer['c_w_down'][j] = _matmul(s['h1'], ddown, 'tn', f"c_down_dw_{i}")
            dh1 = _matmul(ddown, w['c_w_down'][j], 'nt', f"c_down_dx_{i}")
        dx, per_layer['mix_norm'][i] = _rmsnorm_bwd(s['x'], rep['mix_norm'][i:i + 1], dh1, dx1, f"mix_norm_bwd_{i}")

    grads = {}
    for n in WEIGHTS:
        stacked = jnp.stack([per_layer[n][j] for j in sorted(per_layer[n])])
        if n in ('mix_norm', 'ffn_norm', 'a_q_norm', 'a_k_norm', 'a_sinks', 'c_q_a_norm', 'c_kv_a_norm', 'c_q_norm', 'c_k_norm'):
            stacked = stacked.reshape(stacked.shape[0], stacked.shape[-1])
        grads[n] = stacked
    return loss, dx, grads


def kernel(x, positions, mix_norm, ffn_norm, a_w_qkv, a_q_norm, a_k_norm, a_sinks, a_w_o, b_w_in, b_conv_w, b_w_out, c_w_down, c_q_a_norm, c_kv_a_norm, c_w_q_up, c_w_kv_up, c_q_norm, c_k_norm, c_w_o, f_w_gate_up, f_w_down, loss_target, m_mix_norm, m_ffn_norm, m_a_w_qkv, m_a_q_norm, m_a_k_norm, m_a_sinks, m_a_w_o, m_b_w_in, m_b_conv_w, m_b_w_out, m_c_w_down, m_c_q_a_norm, m_c_kv_a_norm, m_c_w_q_up, m_c_w_kv_up, m_c_q_norm, m_c_k_norm, m_c_w_o, m_f_w_gate_up, m_f_w_down, v_mix_norm, v_ffn_norm, v_a_w_qkv, v_a_q_norm, v_a_k_norm, v_a_sinks, v_a_w_o, v_b_w_in, v_b_conv_w, v_b_w_out, v_c_w_down, v_c_q_a_norm, v_c_kv_a_norm, v_c_w_q_up, v_c_w_kv_up, v_c_q_norm, v_c_k_norm, v_c_w_o, v_f_w_gate_up, v_f_w_down):
    local = dict(mix_norm=mix_norm, ffn_norm=ffn_norm, a_w_qkv=a_w_qkv, a_q_norm=a_q_norm, a_k_norm=a_k_norm, a_sinks=a_sinks, a_w_o=a_w_o, b_w_in=b_w_in, b_conv_w=b_conv_w, b_w_out=b_w_out, c_w_down=c_w_down, c_q_a_norm=c_q_a_norm, c_kv_a_norm=c_kv_a_norm, c_w_q_up=c_w_q_up, c_w_kv_up=c_w_kv_up, c_q_norm=c_q_norm, c_k_norm=c_k_norm, c_w_o=c_w_o, f_w_gate_up=f_w_gate_up, f_w_down=f_w_down)
    mom1 = dict(mix_norm=m_mix_norm, ffn_norm=m_ffn_norm, a_w_qkv=m_a_w_qkv, a_q_norm=m_a_q_norm, a_k_norm=m_a_k_norm, a_sinks=m_a_sinks, a_w_o=m_a_w_o, b_w_in=m_b_w_in, b_conv_w=m_b_conv_w, b_w_out=m_b_w_out, c_w_down=m_c_w_down, c_q_a_norm=m_c_q_a_norm, c_kv_a_norm=m_c_kv_a_norm, c_w_q_up=m_c_w_q_up, c_w_kv_up=m_c_w_kv_up, c_q_norm=m_c_q_norm, c_k_norm=m_c_k_norm, c_w_o=m_c_w_o, f_w_gate_up=m_f_w_gate_up, f_w_down=m_f_w_down)
    mom2 = dict(mix_norm=v_mix_norm, ffn_norm=v_ffn_norm, a_w_qkv=v_a_w_qkv, a_q_norm=v_a_q_norm, a_k_norm=v_a_k_norm, a_sinks=v_a_sinks, a_w_o=v_a_w_o, b_w_in=v_b_w_in, b_conv_w=v_b_conv_w, b_w_out=v_b_w_out, c_w_down=v_c_w_down, c_q_a_norm=v_c_q_a_norm, c_kv_a_norm=v_c_kv_a_norm, c_w_q_up=v_c_w_q_up, c_w_kv_up=v_c_w_kv_up, c_q_norm=v_c_q_norm, c_k_norm=v_c_k_norm, c_w_o=v_c_w_o, f_w_gate_up=v_f_w_gate_up, f_w_down=v_f_w_down)
    t, d = x.shape[1], x.shape[2]

    full = _gather_weights(local, GATHER_BF16, BF16, "gather_weights_bf16")
    full.update(_gather_weights(local, GATHER_F32, F32, "gather_weights_f32"))
    rep = {n: local[n] for n in REPLICATED}

    loss, grad_x, grads = _forward_backward(x.reshape(t, d), positions.reshape(t), loss_target.reshape(t, d), full, rep)

    out_g, out_d, out_m, out_v = {}, {}, {}, {}

    def update(names, parts):
        shapes = [local[n].shape for n in names]
        packed = [_pack([src[n] for n in names], F32) for src in (local, mom1, mom2)]
        results = _adamw(parts, *packed, name="adamw_" + names[0])
        for dst, buf in zip((out_g, out_d, out_m, out_v), results):
            dst.update(dict(zip(names, _unpack(buf, shapes))))

    to_send = jnp.stack([_pack([_split_shards(grads[n], SHARD_AXIS[n])[dev] for n in SHARDED], F32) for dev in range(N_DEV)])
    update(SHARDED, _exchange(to_send, True, "scatter_gradients"))
    update(REPLICATED, _exchange(_pack([grads[n] for n in REPLICATED], F32), False, "gather_small_gradients"))

    loss = lax.psum(loss.reshape(()), MESH_AXES)
    outs = [loss, grad_x.reshape(1, t, d)]
    for res in (out_g, out_d, out_m, out_v):
        outs += [res[n] for n in WEIGHTS]
    return tuple(outs)
```

```python
import functools

import jax
import jax.numpy as jnp
from jax import lax
from jax.experimental import pallas as pl
from jax.experimental.pallas import tpu as pltpu

F32 = jnp.float32
BF16 = jnp.bfloat16

N_DEV = 8
MESH_AXES = ("x", "y", "c")

DEPTH = 4
N_MIXERS = 3
ROPE_THETA = 500000.0
EPS = 1e-6
A_HEADS, A_KV_HEADS, A_HEAD_DIM, A_ROT_DIM, A_WINDOW = 16, 4, 64, 16, 128
A_GROUP = A_HEADS // A_KV_HEADS
C_HEADS, C_NOPE, C_ROPE, C_V, C_Q_RANK, C_KV_RANK = 16, 64, 32, 64, 384, 256
C_QK = C_NOPE + C_ROPE
ADAM_LR, ADAM_B1, ADAM_B2, ADAM_EPS, ADAM_WD, ADAM_STEP = 0.001, 0.9, 0.999, 1e-08, 0.01, 10

VMEM_LIMIT_BYTES = 48 * 1024 * 1024
LANES = 128
NEG = -1e30
MLA_BLOCK = 512
MLA_FWD_BLOCK = 1024

WEIGHTS = ['mix_norm', 'ffn_norm', 'a_w_qkv', 'a_q_norm', 'a_k_norm', 'a_sinks', 'a_w_o', 'b_w_in', 'b_conv_w', 'b_w_out',
           'c_w_down', 'c_q_a_norm', 'c_kv_a_norm', 'c_w_q_up', 'c_w_kv_up', 'c_q_norm', 'c_k_norm', 'c_w_o', 'f_w_gate_up',
           'f_w_down']
SHARD_AXIS = {'a_w_qkv': 2, 'a_w_o': 1, 'b_w_in': 2, 'b_conv_w': 2, 'b_w_out': 1, 'c_w_down': 1, 'c_q_a_norm': 1,
              'c_kv_a_norm': 1, 'c_w_q_up': 2, 'c_w_kv_up': 2, 'c_w_o': 1, 'f_w_gate_up': 2, 'f_w_down': 1}
SHARDED = [n for n in WEIGHTS if n in SHARD_AXIS]
REPLICATED = [n for n in WEIGHTS if n not in SHARD_AXIS]
GATHER_F32 = ['b_conv_w', 'c_q_a_norm', 'c_kv_a_norm']
GATHER_BF16 = [n for n in SHARDED if n not in GATHER_F32]
PANEL_WEIGHTS = ['f_w_gate_up', 'f_w_down']


def _params(semantics=None):
    return pltpu.CompilerParams(dimension_semantics=semantics, vmem_limit_bytes=VMEM_LIMIT_BYTES)


def _div_tile(n, cap, mult=LANES):
    best = None
    t = mult
    while t <= min(n, cap):
        if n % t == 0:
            best = t
        t += mult
    return n if best is None else best


def _exchange(arrays, scatter, name):
    n = len(arrays)

    def body(*refs):
        src_refs, out_refs = refs[:n], refs[n:2 * n]
        send_sems, recv_sems, local_sems = refs[2 * n:]
        x, y, c = lax.axis_index("x"), lax.axis_index("y"), lax.axis_index("c")
        me = 4 * x + 2 * y + c
        copies = []
        for a in range(n):
            def piece(idx, a=a):
                return src_refs[a].at[:, idx] if scatter else src_refs[a]

            local = pltpu.make_async_copy(piece(me), out_refs[a].at[:, me], local_sems.at[a])
            local.start()
            copies.append(local)
            for r in range(1, N_DEV):
                px = 1 - x if (r >> 2) & 1 else x
                py = 1 - y if (r >> 1) & 1 else y
                pc = 1 - c if r & 1 else c
                cp = pltpu.make_async_remote_copy(
                    src_ref=piece(4 * px + 2 * py + pc), dst_ref=out_refs[a].at[:, me],
                    send_sem=send_sems.at[a, r - 1], recv_sem=recv_sems.at[a, r - 1],
                    device_id=(px, py, pc), device_id_type=pl.DeviceIdType.MESH)
                cp.start()
                copies.append(cp)
        for cp in copies:
            cp.wait()

    def out_shape(arr):
        rest = arr.shape[2:] if scatter else arr.shape[1:]
        return jax.ShapeDtypeStruct((arr.shape[0], N_DEV) + tuple(rest), arr.dtype)

    return pl.pallas_call(
        body, name=name,
        out_shape=tuple(out_shape(arr) for arr in arrays),
        in_specs=[pl.BlockSpec(memory_space=pl.ANY)] * n,
        out_specs=tuple(pl.BlockSpec(memory_space=pl.ANY) for _ in range(n)),
        scratch_shapes=[pltpu.SemaphoreType.DMA((n, N_DEV - 1)), pltpu.SemaphoreType.DMA((n, N_DEV - 1)),
                        pltpu.SemaphoreType.DMA((n,))],
    )(*arrays)


def _matmul(a, b, mode, name, out_dtype=F32, residual=None, out_panels=False):
    a_pan, b_pan = a.ndim == 3, b.ndim == 3
    panels = a.shape[0] if a_pan else (b.shape[0] if b_pan else 1)
    assert not (a_pan and b_pan) or a.shape[0] == b.shape[0], (name, a.shape, b.shape)
    assert (a_pan or b_pan) or not out_panels, name
    a2, b2 = a.shape[-2:], b.shape[-2:]
    if mode == 'nn':
        (m, k), (k2, n) = a2, b2
    elif mode == 'nt':
        (m, k), (n, k2) = a2, b2
    else:
        (k, m), (k2, n) = a2, b2
    assert k == k2, (name, a.shape, b.shape, mode)
    if mode == 'tn':
        tm, tk = _div_tile(m, 1408), _div_tile(k, 512, 16)
    else:
        tm, tk = _div_tile(m, 512, 16), _div_tile(k, 1536)
    tn = _div_tile(n, 1408)
    nk = k // tk
    p_out = panels if out_panels else 1
    p_red = 1 if out_panels else panels
    dims = {'nn': (((1,), (0,)), ((), ())), 'nt': (((1,), (1,)), ((), ())), 'tn': (((0,), (0,)), ((), ()))}[mode]

    def body(a_ref, b_ref, *rest):
        if residual is None:
            o_ref, acc = rest
        else:
            r_ref, o_ref, acc = rest
        pr, kk = pl.program_id(3), pl.program_id(4)

        @pl.when((pr == 0) & (kk == 0))
        def _():
            acc[...] = jnp.zeros_like(acc)

        acc[...] += lax.dot_general(a_ref[...].astype(BF16), b_ref[...].astype(BF16), dims,
                                    preferred_element_type=F32)

        @pl.when((pr == p_red - 1) & (kk == nk - 1))
        def _():
            r = acc[...]
            if residual is not None:
                r = r + r_ref[...]
            o_ref[...] = r.astype(out_dtype)

    def spec(block, index, paneled):
        if not paneled:
            return pl.BlockSpec(block, lambda po, i, j, pr, kk: index(i, j, kk))
        return pl.BlockSpec((None,) + block, lambda po, i, j, pr, kk: (po if out_panels else pr,) + index(i, j, kk))

    a_spec = (spec((tk, tm), lambda i, j, kk: (kk, i), a_pan) if mode == 'tn' else
              spec((tm, tk), lambda i, j, kk: (i, kk), a_pan))
    b_spec = (spec((tn, tk), lambda i, j, kk: (j, kk), b_pan) if mode == 'nt' else
              spec((tk, tn), lambda i, j, kk: (kk, j), b_pan))
    o_spec = spec((tm, tn), lambda i, j, kk: (i, j), out_panels)
    in_specs, operands = [a_spec, b_spec], [a, b]
    if residual is not None:
        assert not out_panels, name
        in_specs.append(o_spec)
        operands.append(residual)
    out_shape = (panels, m, n) if out_panels else (m, n)
    return pl.pallas_call(
        body, name=name, out_shape=jax.ShapeDtypeStruct(out_shape, out_dtype),
        grid=(p_out, m // tm, n // tn, p_red, nk), in_specs=in_specs, out_specs=o_spec,
        scratch_shapes=[pltpu.VMEM((tm, tn), F32)],
        compiler_params=_params(("parallel", "parallel", "parallel", "arbitrary", "arbitrary")),
    )(*operands)


def _row_spec(tm, cols):
    return pl.BlockSpec((tm, cols), lambda i: (i, 0))


def _const_spec(shape):
    return pl.BlockSpec(shape, lambda i: tuple(0 for _ in shape))


def _accumulate(ref, value, step):
    @pl.when(step == 0)
    def _():
        ref[...] = value

    @pl.when(step > 0)
    def _():
        ref[...] += value


def _rstd(x):
    return lax.rsqrt(jnp.mean(x * x, axis=-1, keepdims=True) + EPS)


def _norm_bwd(x, g, dout):
    xn = x * _rstd(x)
    dg = jnp.sum(dout * xn, axis=0, keepdims=True)
    dxn = dout * g
    dx = _rstd(x) * (dxn - xn * jnp.mean(dxn * xn, axis=-1, keepdims=True))
    return dx, dg


def _rmsnorm_fwd(x, g, name):
    t, d = x.shape
    tm = _div_tile(t, 512, 16)

    def body(x_ref, g_ref, o_ref):
        xv = x_ref[...]
        o_ref[...] = (xv * _rstd(xv) * g_ref[...]).astype(BF16)

    return pl.pallas_call(
        body, name=name, out_shape=jax.ShapeDtypeStruct((t, d), BF16), grid=(t // tm,),
        in_specs=[_row_spec(tm, d), _const_spec((1, d))], out_specs=_row_spec(tm, d),
        compiler_params=_params(("parallel",)),
    )(x, g)


def _rmsnorm_bwd(x, g, dh, dres, name):
    t, d = x.shape
    tm = _div_tile(t, 512, 8)

    def body(x_ref, g_ref, dh_ref, dres_ref, dx_ref, dg_ref):
        dx, dg = _norm_bwd(x_ref[...], g_ref[...], dh_ref[...])
        dx_ref[...] = dres_ref[...] + dx
        _accumulate(dg_ref, dg, pl.program_id(0))

    return pl.pallas_call(
        body, name=name,
        out_shape=(jax.ShapeDtypeStruct((t, d), F32), jax.ShapeDtypeStruct((1, d), F32)), grid=(t // tm,),
        in_specs=[_row_spec(tm, d), _const_spec((1, d)), _row_spec(tm, d), _row_spec(tm, d)],
        out_specs=(_row_spec(tm, d), _const_spec((1, d))),
        compiler_params=_params(("arbitrary",)),
    )(x, g, dh, dres)


def _sigmoid(x):
    return 1.0 / (1.0 + jnp.exp(-x))


def _swiglu_specs(gu):
    _, _, t, c = gu.shape
    tm = _div_tile(t, 512, 16)
    pair = pl.BlockSpec((2, None, tm, c), lambda p, i: (0, p, i, 0))
    single = pl.BlockSpec((None, tm, c), lambda p, i: (p, i, 0))
    return tm, pair, single


def _swiglu_fwd(gu, name):
    _, p, t, c = gu.shape
    tm, pair, single = _swiglu_specs(gu)

    def body(gu_ref, o_ref):
        gate, up = gu_ref[0], gu_ref[1]
        o_ref[...] = (gate * _sigmoid(gate) * up).astype(BF16)

    return pl.pallas_call(
        body, name=name, out_shape=jax.ShapeDtypeStruct((p, t, c), BF16), grid=(p, t // tm),
        in_specs=[pair], out_specs=single, compiler_params=_params(("parallel", "parallel")),
    )(gu)


def _swiglu_bwd(gu, da, name):
    _, p, t, c = gu.shape
    tm, pair, single = _swiglu_specs(gu)

    def body(gu_ref, da_ref, o_ref):
        gate, up, dav = gu_ref[0], gu_ref[1], da_ref[...]
        sig = _sigmoid(gate)
        o_ref[0] = (dav * up * (sig * (1.0 + gate * (1.0 - sig)))).astype(BF16)
        o_ref[1] = (dav * (gate * sig)).astype(BF16)

    return pl.pallas_call(
        body, name=name, out_shape=jax.ShapeDtypeStruct(gu.shape, BF16), grid=(p, t // tm),
        in_specs=[pair, single], out_specs=pair, compiler_params=_params(("parallel", "parallel")),
    )(gu, da)


def _loss_head(y, target, name):
    t, d = y.shape
    tm = _div_tile(t, 512, 8)

    def body(y_ref, t_ref, loss_ref, dy_ref):
        diff = y_ref[...] - t_ref[...]
        dy_ref[...] = diff * (1.0 / d)
        part = jnp.sum(jnp.sum(diff * diff, axis=1, keepdims=True), axis=0, keepdims=True) * (0.5 / d)
        _accumulate(loss_ref, part, pl.program_id(0))

    return pl.pallas_call(
        body, name=name,
        out_shape=(jax.ShapeDtypeStruct((1, 1), F32), jax.ShapeDtypeStruct((t, d), F32)), grid=(t // tm,),
        in_specs=[_row_spec(tm, d), _row_spec(tm, d)], out_specs=(_const_spec((1, 1)), _row_spec(tm, d)),
        compiler_params=_params(("arbitrary",)),
    )(y, target)


HALO = 8


def _shift_down(z, k, halo_rows):
    tm = z.shape[0]
    row = lax.broadcasted_iota(jnp.int32, z.shape, 0)
    out = pltpu.roll(z, k, 0)
    for j in range(k):
        out = jnp.where(row == j, halo_rows[HALO - k + j:HALO - k + j + 1, :], out)
    return out


def _shift_up(z, k, halo_rows):
    tm = z.shape[0]
    row = lax.broadcasted_iota(jnp.int32, z.shape, 0)
    out = pltpu.roll(z, tm - k, 0)
    for j in range(k):
        out = jnp.where(row == tm - k + j, halo_rows[j:j + 1, :], out)
    return out


def _sconv_specs(t, tm, cols):
    per = tm // HALO
    last = t // HALO - 1
    cur = pl.BlockSpec((tm, cols), lambda i: (i, 0))
    prev = pl.BlockSpec((HALO, cols), lambda i: (jnp.maximum(i * per - 1, 0), 0))
    nxt = pl.BlockSpec((HALO, cols), lambda i: (jnp.minimum((i + 1) * per, last), 0))
    return cur, prev, nxt


def _sconv_fwd(bcu, conv_w, name):
    t, d3 = bcu.shape
    d = d3 // 3
    tm = _div_tile(t, 256, 16)
    cur, prev, _ = _sconv_specs(t, tm, d3)

    def body(cur_ref, prev_ref, w_ref, o_ref):
        i = pl.program_id(0)
        z = cur_ref[:, d:2 * d] * cur_ref[:, 2 * d:]
        zp = prev_ref[:, d:2 * d] * prev_ref[:, 2 * d:] * (i > 0).astype(F32)
        y = w_ref[0:1, :] * _shift_down(z, 2, zp) + w_ref[1:2, :] * _shift_down(z, 1, zp) + w_ref[2:3, :] * z
        o_ref[...] = (cur_ref[:, :d] * y).astype(BF16)

    return pl.pallas_call(
        body, name=name, out_shape=jax.ShapeDtypeStruct((t, d), BF16), grid=(t // tm,),
        in_specs=[cur, prev, _const_spec((3, d))], out_specs=_row_spec(tm, d),
        compiler_params=_params(("parallel",)),
    )(bcu, bcu, conv_w)


def _sconv_bwd(bcu, dout, conv_w, name):
    t, d3 = bcu.shape
    d = d3 // 3
    tm = _div_tile(t, 256, 16)
    cur, prev, nxt = _sconv_specs(t, tm, d3)
    dcur, _, dnxt = _sconv_specs(t, tm, d)
    n_tiles = t // tm

    def body(cur_ref, prev_ref, nxt_ref, do_ref, don_ref, w_ref, o_ref, dw_ref):
        i = pl.program_id(0)
        b, cg, u = cur_ref[:, :d], cur_ref[:, d:2 * d], cur_ref[:, 2 * d:]
        z = cg * u
        zp = prev_ref[:, d:2 * d] * prev_ref[:, 2 * d:] * (i > 0).astype(F32)
        z1, z2 = _shift_down(z, 1, zp), _shift_down(z, 2, zp)
        w0, w1, w2 = w_ref[0:1, :], w_ref[1:2, :], w_ref[2:3, :]
        y = w0 * z2 + w1 * z1 + w2 * z
        dov = do_ref[...]
        dy = dov * b
        dyn = don_ref[...] * nxt_ref[:, :d] * (i < n_tiles - 1).astype(F32)
        dz = w2 * dy + w1 * _shift_up(dy, 1, dyn) + w0 * _shift_up(dy, 2, dyn)
        o_ref[:, :d] = (dov * y).astype(BF16)
        o_ref[:, d:2 * d] = (dz * u).astype(BF16)
        o_ref[:, 2 * d:] = (dz * cg).astype(BF16)
        dw = jnp.concatenate([jnp.sum(dy * z2, axis=0, keepdims=True), jnp.sum(dy * z1, axis=0, keepdims=True),
                              jnp.sum(dy * z, axis=0, keepdims=True)], axis=0)
        _accumulate(dw_ref, dw, i)

    return pl.pallas_call(
        body, name=name,
        out_shape=(jax.ShapeDtypeStruct((t, d3), BF16), jax.ShapeDtypeStruct((3, d), F32)), grid=(n_tiles,),
        in_specs=[cur, prev, nxt, dcur, dnxt, _const_spec((3, d))],
        out_specs=(_row_spec(tm, d3), _const_spec((3, d))),
        compiler_params=_params(("arbitrary",)),
    )(bcu, bcu, bcu, dout, dout, conv_w)


def _rope_tables(positions, rot, lead, trail):
    inv_freq = ROPE_THETA ** (-jnp.arange(0, rot, 2, dtype=F32) / rot)
    ang = positions.astype(F32)[:, None] * inv_freq
    cos, sin = jnp.cos(ang), jnp.sin(ang)
    t = positions.shape[0]
    cos_full = jnp.concatenate([jnp.ones((t, lead), F32), cos, cos, jnp.ones((t, trail), F32)], axis=1)
    sin_full = jnp.concatenate([jnp.zeros((t, lead), F32), -sin, sin, jnp.zeros((t, trail), F32)], axis=1)
    return cos_full, sin_full


def _swap_halves(x, lead, rot):
    half = rot // 2
    rows, d = x.shape
    parts = []
    if lead:
        parts.append(jnp.zeros((rows, lead), x.dtype))
    parts += [x[:, lead + half:lead + rot], x[:, lead:lead + half]]
    if d - lead - rot:
        parts.append(jnp.zeros((rows, d - lead - rot), x.dtype))
    return jnp.concatenate(parts, axis=1)


def _head_fwd(x, g, cos, sin, lead, rot):
    n = x * _rstd(x) * g
    return n * cos + _swap_halves(n, lead, rot) * sin


def _head_bwd(x, g, cos, sin, dout, lead, rot):
    dn = dout * cos + _swap_halves(dout * sin, lead, rot)
    return _norm_bwd(x, g, dn)


A_Q_COLS = A_HEADS * A_HEAD_DIM
A_KV_COLS = A_KV_HEADS * A_HEAD_DIM
A_COLS = A_Q_COLS + 2 * A_KV_COLS
A_SCALE = A_HEAD_DIM ** -0.5


def _swa_prep_fwd(qkv, q_norm, k_norm, cos, sin, name):
    t = qkv.shape[0]
    tm = _div_tile(t, 256, 16)
    hd = A_HEAD_DIM

    def body(x_ref, gq_ref, gk_ref, cos_ref, sin_ref, o_ref):
        cosv, sinv = cos_ref[...], sin_ref[...]
        for h in range(A_HEADS + A_KV_HEADS):
            g = gq_ref[...] if h < A_HEADS else gk_ref[...]
            o_ref[:, h * hd:(h + 1) * hd] = _head_fwd(x_ref[:, h * hd:(h + 1) * hd], g, cosv, sinv, 0, A_ROT_DIM).astype(BF16)
        o_ref[:, A_Q_COLS + A_KV_COLS:] = x_ref[:, A_Q_COLS + A_KV_COLS:].astype(BF16)

    return pl.pallas_call(
        body, name=name, out_shape=jax.ShapeDtypeStruct((t, A_COLS), BF16), grid=(t // tm,),
        in_specs=[_row_spec(tm, A_COLS), _const_spec((1, hd)), _const_spec((1, hd)), _row_spec(tm, hd), _row_spec(tm, hd)],
        out_specs=_row_spec(tm, A_COLS), compiler_params=_params(("parallel",)),
    )(qkv, q_norm, k_norm, cos, sin)


def _swa_prep_bwd(qkv, dqkv_r, q_norm, k_norm, cos, sin, name):
    t = qkv.shape[0]
    tm = _div_tile(t, 256, 16)
    hd = A_HEAD_DIM

    def body(x_ref, d_ref, gq_ref, gk_ref, cos_ref, sin_ref, o_ref, dgq_ref, dgk_ref):
        cosv, sinv = cos_ref[...], sin_ref[...]
        dgq = jnp.zeros((1, hd), F32)
        dgk = jnp.zeros((1, hd), F32)
        for h in range(A_HEADS + A_KV_HEADS):
            sl = slice(h * hd, (h + 1) * hd)
            g = gq_ref[...] if h < A_HEADS else gk_ref[...]
            dx, dg = _head_bwd(x_ref[:, sl], g, cosv, sinv, d_ref[:, sl], 0, A_ROT_DIM)
            o_ref[:, sl] = dx.astype(BF16)
            if h < A_HEADS:
                dgq = dgq + dg
            else:
                dgk = dgk + dg
        o_ref[:, A_Q_COLS + A_KV_COLS:] = d_ref[:, A_Q_COLS + A_KV_COLS:].astype(BF16)
        _accumulate(dgq_ref, dgq, pl.program_id(0))
        _accumulate(dgk_ref, dgk, pl.program_id(0))

    return pl.pallas_call(
        body, name=name,
        out_shape=(jax.ShapeDtypeStruct((t, A_COLS), BF16), jax.ShapeDtypeStruct((1, hd), F32),
                   jax.ShapeDtypeStruct((1, hd), F32)),
        grid=(t // tm,),
        in_specs=[_row_spec(tm, A_COLS), _row_spec(tm, A_COLS), _const_spec((1, hd)), _const_spec((1, hd)),
                  _row_spec(tm, hd), _row_spec(tm, hd)],
        out_specs=(_row_spec(tm, A_COLS), _const_spec((1, hd)), _const_spec((1, hd))),
        compiler_params=_params(("arbitrary",)),
    )(qkv, dqkv_r, q_norm, k_norm, cos, sin)


def _group_rows(ref, k, width=A_HEAD_DIM, base=0):
    return jnp.concatenate([ref[:, base + (A_GROUP * k + g) * width:base + (A_GROUP * k + g + 1) * width]
                            for g in range(A_GROUP)], axis=0)


def _group_column(ref, k, rows):
    cols = []
    for g in range(A_GROUP):
        h = A_GROUP * k + g
        col = ref[:, h:h + 1]
        cols.append(jnp.broadcast_to(col, (rows, 1)) if col.shape[0] == 1 else col)
    return jnp.concatenate(cols, axis=0)


def _swa_fwd(qkv_r, sinks, name):
    t = qkv_r.shape[0]
    blk = A_WINDOW
    nb = t // blk
    hd = A_HEAD_DIM
    kv_block = A_Q_COLS // (2 * A_KV_COLS)

    def body(q_ref, kvc_ref, kvp_ref, s_ref, o_ref, lse_ref):
        n = pl.program_id(0)
        shape = (A_GROUP * blk, 2 * blk)
        qpos = lax.broadcasted_iota(jnp.int32, shape, 0) & (blk - 1)
        col = lax.broadcasted_iota(jnp.int32, shape, 1)
        delta = qpos + blk - col
        valid = (delta >= 0) & (delta < A_WINDOW) & ((col >= blk) | (n > 0))
        for k in range(A_KV_HEADS):
            qg = _group_rows(q_ref, k)
            kw = jnp.concatenate([kvp_ref[:, k * hd:(k + 1) * hd], kvc_ref[:, k * hd:(k + 1) * hd]], axis=0)
            vw = jnp.concatenate([kvp_ref[:, A_KV_COLS + k * hd:A_KV_COLS + (k + 1) * hd],
                                  kvc_ref[:, A_KV_COLS + k * hd:A_KV_COLS + (k + 1) * hd]], axis=0)
            s = lax.dot_general(qg, kw, (((1,), (1,)), ((), ())), preferred_element_type=F32) * A_SCALE
            s = jnp.where(valid, s, NEG)
            sink = _group_column(s_ref, k, blk)
            m = jnp.maximum(jnp.max(s, axis=-1, keepdims=True), sink)
            p = jnp.exp(s - m)
            denom = jnp.sum(p, axis=-1, keepdims=True) + jnp.exp(sink - m)
            p = p / denom
            o = jnp.dot(p.astype(BF16), vw, preferred_element_type=F32)
            lse = m + jnp.log(denom)
            for g in range(A_GROUP):
                h = A_GROUP * k + g
                o_ref[:, h * hd:(h + 1) * hd] = o[g * blk:(g + 1) * blk].astype(BF16)
                lse_ref[:, h:h + 1] = lse[g * blk:(g + 1) * blk]

    return pl.pallas_call(
        body, name=name,
        out_shape=(jax.ShapeDtypeStruct((t, A_Q_COLS), BF16), jax.ShapeDtypeStruct((t, A_HEADS), F32)), grid=(nb,),
        in_specs=[pl.BlockSpec((blk, A_Q_COLS), lambda n: (n, 0)),
                  pl.BlockSpec((blk, 2 * A_KV_COLS), lambda n: (n, kv_block)),
                  pl.BlockSpec((blk, 2 * A_KV_COLS), lambda n: (jnp.maximum(n - 1, 0), kv_block)),
                  _const_spec((1, A_HEADS))],
        out_specs=(pl.BlockSpec((blk, A_Q_COLS), lambda n: (n, 0)), pl.BlockSpec((blk, A_HEADS), lambda n: (n, 0))),
        compiler_params=_params(("parallel",)),
    )(qkv_r, qkv_r, qkv_r, sinks)


def _swa_bwd(qkv_r, o, lse, do, sinks, name):
    t = qkv_r.shape[0]
    blk = A_WINDOW
    nb = t // blk
    hd = A_HEAD_DIM
    kv_block = A_Q_COLS // (2 * A_KV_COLS)
    rows = A_GROUP * blk

    def nxt(n):
        return jnp.minimum(n + 1, nb - 1)

    def body(qc_ref, qn_ref, kvc_ref, kvp_ref, doc_ref, don_ref, oc_ref, on_ref, lc_ref, ln_ref, s_ref, dx_ref, ds_ref):
        n = pl.program_id(0)
        shape = (2 * rows, 2 * blk)
        row = lax.broadcasted_iota(jnp.int32, shape, 0)
        col = lax.broadcasted_iota(jnp.int32, shape, 1)
        is_next = row >= rows
        delta = jnp.where(is_next, blk, 0) + blk + (row & (blk - 1)) - col
        valid = ((delta >= 0) & (delta < A_WINDOW) & ((col >= blk) | (n > 0)) & (jnp.logical_not(is_next) | (n < nb - 1)))
        dsink_cols = []
        for k in range(A_KV_HEADS):
            qs = jnp.concatenate([_group_rows(qc_ref, k), _group_rows(qn_ref, k)], axis=0)
            dos = jnp.concatenate([_group_rows(doc_ref, k), _group_rows(don_ref, k)], axis=0)
            os_ = jnp.concatenate([_group_rows(oc_ref, k), _group_rows(on_ref, k)], axis=0).astype(F32)
            lses = jnp.concatenate([_group_column(lc_ref, k, blk), _group_column(ln_ref, k, blk)], axis=0)
            kw = jnp.concatenate([kvp_ref[:, k * hd:(k + 1) * hd], kvc_ref[:, k * hd:(k + 1) * hd]], axis=0)
            vw = jnp.concatenate([kvp_ref[:, A_KV_COLS + k * hd:A_KV_COLS + (k + 1) * hd],
                                  kvc_ref[:, A_KV_COLS + k * hd:A_KV_COLS + (k + 1) * hd]], axis=0)
            s = lax.dot_general(qs, kw, (((1,), (1,)), ((), ())), preferred_element_type=F32) * A_SCALE
            p = jnp.exp(jnp.where(valid, s - lses, NEG))
            dos_b = dos.astype(BF16)
            dp = lax.dot_general(dos_b, vw, (((1,), (1,)), ((), ())), preferred_element_type=F32)
            dlt = jnp.sum(dos * os_, axis=-1, keepdims=True)
            ds = p * (dp - dlt)
            dq = jnp.dot(ds[:rows].astype(BF16), kw, preferred_element_type=F32) * A_SCALE
            dk = lax.dot_general(ds[:, blk:].astype(BF16), qs, (((0,), (0,)), ((), ())), preferred_element_type=F32) * A_SCALE
            dv = lax.dot_general(p[:, blk:].astype(BF16), dos_b, (((0,), (0,)), ((), ())), preferred_element_type=F32)
            for g in range(A_GROUP):
                h = A_GROUP * k + g
                dx_ref[:, h * hd:(h + 1) * hd] = dq[g * blk:(g + 1) * blk]
            dx_ref[:, A_Q_COLS + k * hd:A_Q_COLS + (k + 1) * hd] = dk
            dx_ref[:, A_Q_COLS + A_KV_COLS + k * hd:A_Q_COLS + A_KV_COLS + (k + 1) * hd] = dv
            sink = _group_column(s_ref, k, blk)
            contrib = -jnp.exp(sink - lses[:rows]) * dlt[:rows]
            for g in range(A_GROUP):
                dsink_cols.append(jnp.sum(contrib[g * blk:(g + 1) * blk], axis=0, keepdims=True))
        _accumulate(ds_ref, jnp.concatenate(dsink_cols, axis=1), n)

    q_spec = lambda f: pl.BlockSpec((blk, A_Q_COLS), lambda n: (f(n), 0))
    l_spec = lambda f: pl.BlockSpec((blk, A_HEADS), lambda n: (f(n), 0))
    same = lambda n: n
    return pl.pallas_call(
        body, name=name,
        out_shape=(jax.ShapeDtypeStruct((t, A_COLS), F32), jax.ShapeDtypeStruct((1, A_HEADS), F32)), grid=(nb,),
        in_specs=[q_spec(same), q_spec(nxt),
                  pl.BlockSpec((blk, 2 * A_KV_COLS), lambda n: (n, kv_block)),
                  pl.BlockSpec((blk, 2 * A_KV_COLS), lambda n: (jnp.maximum(n - 1, 0), kv_block)),
                  q_spec(same), q_spec(nxt), q_spec(same), q_spec(nxt), l_spec(same), l_spec(nxt),
                  _const_spec((1, A_HEADS))],
        out_specs=(pl.BlockSpec((blk, A_COLS), lambda n: (n, 0)), _const_spec((1, A_HEADS))),
        compiler_params=_params(("arbitrary",)),
    )(qkv_r, qkv_r, qkv_r, qkv_r, do, do, o, o, lse, lse, sinks)


C_DOWN_COLS = C_Q_RANK + C_KV_RANK + C_ROPE
C_Q_COLS = C_HEADS * C_QK
C_KV_COLS = C_HEADS * (C_NOPE + C_V)
C_O_COLS = C_HEADS * C_V
C_PAD = LANES
C_SCALE = C_QK ** -0.5
LOG2E = 1.4426950408889634
LN2 = 0.6931471805599453
C_Q_SCALE = C_SCALE * LOG2E
C_PAIR = 2


def _mla_latent_fwd(down, q_a_norm, kv_a_norm, name):
    t = down.shape[0]
    tm = _div_tile(t, 512, 16)

    def body(x_ref, gq_ref, gk_ref, cq_ref, ckv_ref):
        cq, ckv = x_ref[:, :C_Q_RANK], x_ref[:, C_Q_RANK:C_Q_RANK + C_KV_RANK]
        cq_ref[...] = (cq * _rstd(cq) * gq_ref[...]).astype(BF16)
        ckv_ref[...] = (ckv * _rstd(ckv) * gk_ref[...]).astype(BF16)

    return pl.pallas_call(
        body, name=name,
        out_shape=(jax.ShapeDtypeStruct((t, C_Q_RANK), BF16), jax.ShapeDtypeStruct((t, C_KV_RANK), BF16)), grid=(t // tm,),
        in_specs=[_row_spec(tm, C_DOWN_COLS), _const_spec((1, C_Q_RANK)), _const_spec((1, C_KV_RANK))],
        out_specs=(_row_spec(tm, C_Q_RANK), _row_spec(tm, C_KV_RANK)), compiler_params=_params(("parallel",)),
    )(down, q_a_norm, kv_a_norm)


def _mla_latent_bwd(down, dcq, dckv, dkrope, q_a_norm, kv_a_norm, name):
    t = down.shape[0]
    tm = _div_tile(t, 512, 16)

    def body(x_ref, dcq_ref, dckv_ref, dkr_ref, gq_ref, gk_ref, o_ref, dgq_ref, dgk_ref):
        dq, dgq = _norm_bwd(x_ref[:, :C_Q_RANK], gq_ref[...], dcq_ref[...])
        dkv, dgk = _norm_bwd(x_ref[:, C_Q_RANK:C_Q_RANK + C_KV_RANK], gk_ref[...], dckv_ref[...])
        o_ref[...] = jnp.concatenate([dq, dkv, dkr_ref[...]], axis=1).astype(BF16)
        _accumulate(dgq_ref, dgq, pl.program_id(0))
        _accumulate(dgk_ref, dgk, pl.program_id(0))

    return pl.pallas_call(
        body, name=name,
        out_shape=(jax.ShapeDtypeStruct((t, C_DOWN_COLS), BF16), jax.ShapeDtypeStruct((1, C_Q_RANK), F32),
                   jax.ShapeDtypeStruct((1, C_KV_RANK), F32)),
        grid=(t // tm,),
        in_specs=[_row_spec(tm, C_DOWN_COLS), _row_spec(tm, C_Q_RANK), _row_spec(tm, C_KV_RANK), _row_spec(tm, C_ROPE),
                  _const_spec((1, C_Q_RANK)), _const_spec((1, C_KV_RANK))],
        out_specs=(_row_spec(tm, C_DOWN_COLS), _const_spec((1, C_Q_RANK)), _const_spec((1, C_KV_RANK))),
        compiler_params=_params(("arbitrary",)),
    )(down, dcq, dckv, dkrope, q_a_norm, kv_a_norm)


def _head_major_spec(tm, width):
    return pl.BlockSpec((C_HEADS, tm, width), lambda i: (0, i, 0))


def _mla_qk_fwd(qw, kvw, down, q_norm, k_norm, cos, sin, name):
    t = qw.shape[0]
    tm = _div_tile(t, 256, 16)
    kvd = C_NOPE + C_V

    def body(q_ref, kv_ref, dn_ref, gq_ref, gk_ref, cos_ref, sin_ref, qo_ref, ko_ref, vo_ref):
        cosv, sinv = cos_ref[...], sin_ref[...]
        k_rope = dn_ref[:, C_Q_RANK + C_KV_RANK:]
        pad = jnp.zeros((tm, C_PAD - C_QK), F32)
        one_then_zeros = (lax.broadcasted_iota(jnp.int32, (tm, C_PAD - C_V), 1) == 0).astype(F32)
        for h in range(C_HEADS):
            qh = _head_fwd(q_ref[:, h * C_QK:(h + 1) * C_QK], gq_ref[...], cosv, sinv, C_NOPE, C_ROPE)
            kx = jnp.concatenate([kv_ref[:, h * kvd:h * kvd + C_NOPE], k_rope], axis=1)
            kh = _head_fwd(kx, gk_ref[...], cosv, sinv, C_NOPE, C_ROPE)
            qo_ref[h] = jnp.concatenate([qh * C_Q_SCALE, pad], axis=1).astype(BF16)
            ko_ref[h] = jnp.concatenate([kh, pad], axis=1).astype(BF16)
            vo_ref[h] = jnp.concatenate([kv_ref[:, h * kvd + C_NOPE:(h + 1) * kvd], one_then_zeros], axis=1).astype(BF16)

    return pl.pallas_call(
        body, name=name,
        out_shape=(jax.ShapeDtypeStruct((C_HEADS, t, C_PAD), BF16), jax.ShapeDtypeStruct((C_HEADS, t, C_PAD), BF16),
                   jax.ShapeDtypeStruct((C_HEADS, t, C_PAD), BF16)),
        grid=(t // tm,),
        in_specs=[_row_spec(tm, C_Q_COLS), _row_spec(tm, C_KV_COLS), _row_spec(tm, C_DOWN_COLS), _const_spec((1, C_QK)),
                  _const_spec((1, C_QK)), _row_spec(tm, C_QK), _row_spec(tm, C_QK)],
        out_specs=(_head_major_spec(tm, C_PAD), _head_major_spec(tm, C_PAD), _head_major_spec(tm, C_PAD)),
        compiler_params=_params(("parallel",)),
    )(qw, kvw, down, q_norm, k_norm, cos, sin)


def _mla_qk_bwd(qw, kvw, down, dq, dk, dv, q_norm, k_norm, cos, sin, name):
    t = qw.shape[0]
    tm = _div_tile(t, 256, 16)
    kvd = C_NOPE + C_V

    def body(q_ref, kv_ref, dn_ref, dq_ref, dk_ref, dv_ref, gq_ref, gk_ref, cos_ref, sin_ref,
             dqw_ref, dkvw_ref, dkr_ref, dgq_ref, dgk_ref):
        cosv, sinv = cos_ref[...], sin_ref[...]
        k_rope = dn_ref[:, C_Q_RANK + C_KV_RANK:]
        dgq = jnp.zeros((1, C_QK), F32)
        dgk = jnp.zeros((1, C_QK), F32)
        dkr = jnp.zeros((tm, C_ROPE), F32)
        for h in range(C_HEADS):
            dxq, dg = _head_bwd(q_ref[:, h * C_QK:(h + 1) * C_QK], gq_ref[...], cosv, sinv, dq_ref[h][:, :C_QK], C_NOPE, C_ROPE)
            dgq = dgq + dg
            dqw_ref[:, h * C_QK:(h + 1) * C_QK] = dxq.astype(BF16)
            kx = jnp.concatenate([kv_ref[:, h * kvd:h * kvd + C_NOPE], k_rope], axis=1)
            dxk, dg = _head_bwd(kx, gk_ref[...], cosv, sinv, dk_ref[h][:, :C_QK], C_NOPE, C_ROPE)
            dgk = dgk + dg
            dkr = dkr + dxk[:, C_NOPE:]
            dkvw_ref[:, h * kvd:(h + 1) * kvd] = jnp.concatenate([dxk[:, :C_NOPE], dv_ref[h]], axis=1).astype(BF16)
        dkr_ref[...] = dkr
        _accumulate(dgq_ref, dgq, pl.program_id(0))
        _accumulate(dgk_ref, dgk, pl.program_id(0))

    return pl.pallas_call(
        body, name=name,
        out_shape=(jax.ShapeDtypeStruct((t, C_Q_COLS), BF16), jax.ShapeDtypeStruct((t, C_KV_COLS), BF16),
                   jax.ShapeDtypeStruct((t, C_ROPE), F32), jax.ShapeDtypeStruct((1, C_QK), F32),
                   jax.ShapeDtypeStruct((1, C_QK), F32)),
        grid=(t // tm,),
        in_specs=[_row_spec(tm, C_Q_COLS), _row_spec(tm, C_KV_COLS), _row_spec(tm, C_DOWN_COLS),
                  _head_major_spec(tm, C_PAD), _head_major_spec(tm, C_PAD), _head_major_spec(tm, C_V),
                  _const_spec((1, C_QK)), _const_spec((1, C_QK)), _row_spec(tm, C_QK), _row_spec(tm, C_QK)],
        out_specs=(_row_spec(tm, C_Q_COLS), _row_spec(tm, C_KV_COLS), _row_spec(tm, C_ROPE), _const_spec((1, C_QK)),
                   _const_spec((1, C_QK))),
        compiler_params=_params(("arbitrary",)),
    )(qw, kvw, down, dq, dk, dv, q_norm, k_norm, cos, sin)


def _causal_keep(rows, cols, row_offset=0, transposed=False):
    row = lax.broadcasted_iota(jnp.int32, (rows, cols), 0) + row_offset
    col = lax.broadcasted_iota(jnp.int32, (rows, cols), 1)
    return (row <= col) if transposed else (col <= row)


def _mla_fwd(q, k, v, name):
    _, t, _ = q.shape
    blk = min(MLA_FWD_BLOCK, t)
    nq = t // blk

    def body(q_ref, k_ref, v_ref, o_ref, lse_ref, m_sc, acc_sc):
        qi = pl.program_id(1)
        m_sc[...] = jnp.full_like(m_sc, NEG)
        acc_sc[...] = jnp.zeros_like(acc_sc)

        def step(ki, masked):
            rows = pl.ds(pl.multiple_of(ki * blk, blk), blk)
            for hh in range(C_PAIR):
                s = lax.dot_general(q_ref[hh], k_ref[hh, rows, :], (((1,), (1,)), ((), ())), preferred_element_type=F32)
                if masked:
                    s = jnp.where(_causal_keep(blk, blk), s, NEG)
                m_prev = m_sc[hh]
                m_new = jnp.maximum(m_prev, jnp.max(s, axis=-1, keepdims=True))
                p = jnp.exp2(s - m_new)
                acc_sc[hh] = jnp.exp2(m_prev - m_new) * acc_sc[hh] + jnp.dot(p.astype(BF16), v_ref[hh, rows, :],
                                                                                preferred_element_type=F32)
                m_sc[hh] = m_new

        def below_diagonal(ki, carry):
            step(ki, False)
            return carry

        lax.fori_loop(0, qi, below_diagonal, 0)
        step(qi, True)
        outs = []
        for hh in range(C_PAIR):
            denom = acc_sc[hh, :, C_V:C_V + 1]
            outs.append(acc_sc[hh, :, :C_V] / denom)
            lse_ref[hh] = m_sc[hh] + jnp.log(denom) * LOG2E
        o_ref[...] = jnp.concatenate(outs, axis=1).astype(BF16)

    whole = lambda hp, qi: (hp, 0, 0)
    return pl.pallas_call(
        body, name=name,
        out_shape=(jax.ShapeDtypeStruct((t, C_O_COLS), BF16), jax.ShapeDtypeStruct((C_HEADS, t, 1), F32)),
        grid=(C_HEADS // C_PAIR, nq),
        in_specs=[pl.BlockSpec((C_PAIR, blk, C_PAD), lambda hp, qi: (hp, qi, 0)),
                  pl.BlockSpec((C_PAIR, t, C_PAD), whole), pl.BlockSpec((C_PAIR, t, C_PAD), whole)],
        out_specs=(pl.BlockSpec((blk, C_PAIR * C_V), lambda hp, qi: (qi, hp)),
                   pl.BlockSpec((C_PAIR, blk, 1), lambda hp, qi: (hp, qi, 0))),
        scratch_shapes=[pltpu.VMEM((C_PAIR, blk, 1), F32), pltpu.VMEM((C_PAIR, blk, C_PAD), F32)],
        compiler_params=_params(("parallel", "arbitrary")),
    )(q, k, v)


def _mla_bwd_dq(q, k, v, do, o, lse, name):
    _, t, _ = q.shape
    blk = min(MLA_BLOCK, t)
    nq = t // blk

    def body(q_ref, k_ref, v_ref, do_ref, o_ref, lse_ref, dq_ref, dlt_ref, acc_sc, dob_sc):
        qi = pl.program_id(1)
        for hh in range(C_PAIR):
            do_h = do_ref[:, hh * C_V:(hh + 1) * C_V]
            dlt_ref[hh] = jnp.sum(do_h * o_ref[:, hh * C_V:(hh + 1) * C_V].astype(F32), axis=-1, keepdims=True)
            dob_sc[hh] = do_h.astype(BF16)
        acc_sc[...] = jnp.zeros_like(acc_sc)

        def step(ki, masked):
            rows = pl.ds(pl.multiple_of(ki * blk, blk), blk)
            for hh in range(C_PAIR):
                kb = k_ref[hh, rows, :]
                s = lax.dot_general(q_ref[hh], kb, (((1,), (1,)), ((), ())), preferred_element_type=F32)
                if masked:
                    s = jnp.where(_causal_keep(blk, blk), s, NEG)
                p = jnp.exp2(s - lse_ref[hh])
                dp = lax.dot_general(dob_sc[hh], v_ref[hh, rows, :C_V], (((1,), (1,)), ((), ())),
                                     preferred_element_type=F32)
                ds = p * (dp - dlt_ref[hh])
                acc_sc[hh] += jnp.dot(ds.astype(BF16), kb, preferred_element_type=F32)

        def below_diagonal(ki, carry):
            step(ki, False)
            return carry

        lax.fori_loop(0, qi, below_diagonal, 0)
        step(qi, True)
        dq_ref[...] = acc_sc[...] * C_SCALE

    whole = lambda hp, qi: (hp, 0, 0)
    qmap = lambda hp, qi: (hp, qi, 0)
    wide = lambda hp, qi: (qi, hp)
    return pl.pallas_call(
        body, name=name,
        out_shape=(jax.ShapeDtypeStruct((C_HEADS, t, C_PAD), F32), jax.ShapeDtypeStruct((C_HEADS, t, 1), F32)),
        grid=(C_HEADS // C_PAIR, nq),
        in_specs=[pl.BlockSpec((C_PAIR, blk, C_PAD), qmap), pl.BlockSpec((C_PAIR, t, C_PAD), whole),
                  pl.BlockSpec((C_PAIR, t, C_PAD), whole), pl.BlockSpec((blk, C_PAIR * C_V), wide),
                  pl.BlockSpec((blk, C_PAIR * C_V), wide), pl.BlockSpec((C_PAIR, blk, 1), qmap)],
        out_specs=(pl.BlockSpec((C_PAIR, blk, C_PAD), qmap), pl.BlockSpec((C_PAIR, blk, 1), qmap)),
        scratch_shapes=[pltpu.VMEM((C_PAIR, blk, C_PAD), F32), pltpu.VMEM((C_PAIR, blk, C_V), BF16)],
        compiler_params=_params(("parallel", "arbitrary")),
    )(q, k, v, do, o, lse)


def _mla_bwd_dkv(q, k, v, do_b, lse_rows, dlt_rows, name):
    _, t, _ = q.shape
    blk = min(MLA_BLOCK, t)
    nq = t // blk

    def body(q_ref, k_ref, v_ref, do_ref, lse_ref, dlt_ref, dk_ref, dv_ref, dk_sc, dv_sc):
        ki = pl.program_id(1)
        dk_sc[...] = jnp.zeros_like(dk_sc)
        dv_sc[...] = jnp.zeros_like(dv_sc)

        def step(qi, masked):
            rows = pl.ds(pl.multiple_of(qi * blk, blk), blk)
            for hh in range(C_PAIR):
                qb = q_ref[hh, rows, :]
                dob = do_ref[rows, hh * C_V:(hh + 1) * C_V]
                s = lax.dot_general(k_ref[hh], qb, (((1,), (1,)), ((), ())), preferred_element_type=F32)
                if masked:
                    s = jnp.where(_causal_keep(blk, blk, transposed=True), s, NEG)
                p = jnp.exp2(s - lse_ref[hh, qi])
                dp = lax.dot_general(v_ref[hh, :, :C_V], dob, (((1,), (1,)), ((), ())), preferred_element_type=F32)
                ds = p * (dp - dlt_ref[hh, qi])
                dv_sc[hh] += jnp.dot(p.astype(BF16), dob, preferred_element_type=F32)
                dk_sc[hh] += jnp.dot(ds.astype(BF16), qb, preferred_element_type=F32)

        def above_diagonal(qi, carry):
            step(qi, False)
            return carry

        step(ki, True)
        lax.fori_loop(ki + 1, nq, above_diagonal, 0)
        dk_ref[...] = dk_sc[...] * LN2
        dv_ref[...] = dv_sc[...]

    whole = lambda hp, ki: (hp, 0, 0)
    whole4 = lambda hp, ki: (hp, 0, 0, 0)
    kmap = lambda hp, ki: (hp, ki, 0)
    return pl.pallas_call(
        body, name=name,
        out_shape=(jax.ShapeDtypeStruct((C_HEADS, t, C_PAD), F32), jax.ShapeDtypeStruct((C_HEADS, t, C_V), F32)),
        grid=(C_HEADS // C_PAIR, nq),
        in_specs=[pl.BlockSpec((C_PAIR, t, C_PAD), whole), pl.BlockSpec((C_PAIR, blk, C_PAD), kmap),
                  pl.BlockSpec((C_PAIR, blk, C_PAD), kmap), pl.BlockSpec((t, C_PAIR * C_V), lambda hp, ki: (0, hp)),
                  pl.BlockSpec((C_PAIR, nq, 1, blk), whole4), pl.BlockSpec((C_PAIR, nq, 1, blk), whole4)],
        out_specs=(pl.BlockSpec((C_PAIR, blk, C_PAD), kmap), pl.BlockSpec((C_PAIR, blk, C_V), kmap)),
        scratch_shapes=[pltpu.VMEM((C_PAIR, blk, C_PAD), F32), pltpu.VMEM((C_PAIR, blk, C_V), F32)],
        compiler_params=_params(("parallel", "arbitrary")),
    )(q, k, v, do_b, lse_rows, dlt_rows)


def _adamw(parts, w, m, v, name):
    layers, rows, cols = w.shape
    tm = _div_tile(rows, 256, 16)

    def body(p_ref, w_ref, m_ref, v_ref, g_ref, d_ref, nm_ref, nv_ref):
        g = p_ref[0].astype(F32)
        for j in range(1, N_DEV):
            g = g + p_ref[j].astype(F32)
        nm = ADAM_B1 * m_ref[...] + (1.0 - ADAM_B1) * g
        nv = ADAM_B2 * v_ref[...] + (1.0 - ADAM_B2) * jnp.square(g)
        m_hat = nm / (1.0 - ADAM_B1 ** ADAM_STEP)
        v_hat = nv / (1.0 - ADAM_B2 ** ADAM_STEP)
        g_ref[...] = g
        d_ref[...] = -ADAM_LR * (m_hat / (jnp.sqrt(v_hat) + ADAM_EPS) + ADAM_WD * w_ref[...])
        nm_ref[...] = nm
        nv_ref[...] = nv

    spec = pl.BlockSpec((None, tm, cols), lambda l, i: (l, i, 0))
    return pl.pallas_call(
        body, name=name, out_shape=tuple(jax.ShapeDtypeStruct(w.shape, F32) for _ in range(4)),
        grid=(layers, rows // tm),
        in_specs=[pl.BlockSpec((None, N_DEV, tm, cols), lambda l, i: (l, 0, i, 0)), spec, spec, spec],
        out_specs=(spec, spec, spec, spec), compiler_params=_params(("parallel", "parallel")),
    )(parts, w, m, v)


def _join_shards(gathered, axis):
    moved = jnp.moveaxis(gathered, 1, axis)
    shape = list(moved.shape)
    shape[axis:axis + 2] = [shape[axis] * shape[axis + 1]]
    return moved.reshape(shape)


def _split_shards(full, axis):
    shape = list(full.shape)
    shape[axis:axis + 1] = [N_DEV, shape[axis] // N_DEV]
    return jnp.moveaxis(full.reshape(shape), axis, 1)


def _as_rows(shape):
    rest = tuple(shape[1:])
    return (shape[0], 1, rest[0]) if len(rest) == 1 else (shape[0],) + rest


def _down_panels(w_down):
    p, r, d = w_down.shape
    return w_down.reshape(p // 2, 2 * r, d)


def _forward_backward(x, positions, target, w, rep):
    t = x.shape[0]
    cos_a, sin_a = _rope_tables(positions, A_ROT_DIM, 0, A_HEAD_DIM - A_ROT_DIM)
    cos_c, sin_c = _rope_tables(positions, C_ROPE, C_NOPE, 0)
    saved = []
    for i in range(DEPTH):
        kind, j = i % N_MIXERS, i // N_MIXERS
        s = {'x': x}
        h1 = _rmsnorm_fwd(x, rep['mix_norm'][i:i + 1], f"mix_norm_fwd_{i}")
        s['h1'] = h1
        if kind == 0:
            s['qkv'] = _matmul(h1, w['a_w_qkv'][j], 'nn', f"a_qkv_{i}")
            s['qkv_r'] = _swa_prep_fwd(s['qkv'], rep['a_q_norm'][j:j + 1], rep['a_k_norm'][j:j + 1], cos_a, sin_a,
                                       f"a_prep_fwd_{i}")
            s['o'], s['lse'] = _swa_fwd(s['qkv_r'], rep['a_sinks'][j:j + 1], f"a_attn_fwd_{i}")
            x1 = _matmul(s['o'], w['a_w_o'][j], 'nn', f"a_out_{i}", residual=x)
        elif kind == 1:
            s['bcu'] = _matmul(h1, w['b_w_in'][j], 'nn', f"b_in_{i}")
            s['by'] = _sconv_fwd(s['bcu'], w['b_conv_w'][j], f"b_conv_fwd_{i}")
            x1 = _matmul(s['by'], w['b_w_out'][j], 'nn', f"b_out_{i}", residual=x)
        else:
            s['down'] = _matmul(h1, w['c_w_down'][j], 'nn', f"c_down_{i}")
            s['cq'], s['ckv'] = _mla_latent_fwd(s['down'], w['c_q_a_norm'][j:j + 1], w['c_kv_a_norm'][j:j + 1],
                                                f"c_latent_fwd_{i}")
            s['qw'] = _matmul(s['cq'], w['c_w_q_up'][j], 'nn', f"c_q_up_{i}")
            s['kvw'] = _matmul(s['ckv'], w['c_w_kv_up'][j], 'nn', f"c_kv_up_{i}")
            s['q'], s['k'], s['v'] = _mla_qk_fwd(s['qw'], s['kvw'], s['down'], rep['c_q_norm'][j:j + 1],
                                                 rep['c_k_norm'][j:j + 1], cos_c, sin_c, f"c_prep_fwd_{i}")
            s['o'], s['lse'] = _mla_fwd(s['q'], s['k'], s['v'], f"c_attn_fwd_{i}")
            x1 = _matmul(s['o'], w['c_w_o'][j], 'nn', f"c_out_{i}", residual=x)
        s['x1'] = x1
        s['h2'] = _rmsnorm_fwd(x1, rep['ffn_norm'][i:i + 1], f"ffn_norm_fwd_{i}")
        s['gu'] = _matmul(s['h2'], w['f_w_gate_up'][i], 'nn', f"f_gate_up_{i}", out_panels=True).reshape(
            2, N_DEV // 2, t, -1)
        s['act'] = _swiglu_fwd(s['gu'], f"f_act_fwd_{i}")
        x = _matmul(s['act'], _down_panels(w['f_w_down'][i]), 'nn', f"f_down_{i}", residual=x1)
        saved.append(s)

    loss, dx = _loss_head(x, target, "loss_head")

    per_layer = {n: {} for n in WEIGHTS}
    for i in reversed(range(DEPTH)):
        kind, j = i % N_MIXERS, i // N_MIXERS
        s = saved[i]
        per_layer['f_w_down'][i] = _matmul(s['act'], dx, 'tn', f"f_down_dw_{i}", out_dtype=BF16, out_panels=True).reshape(
            w['f_w_down'][i].shape)
        dact = _matmul(dx, _down_panels(w['f_w_down'][i]), 'nt', f"f_down_dx_{i}", out_panels=True)
        dgu = _swiglu_bwd(s['gu'], dact, f"f_act_bwd_{i}").reshape(N_DEV, t, -1)
        per_layer['f_w_gate_up'][i] = _matmul(s['h2'], dgu, 'tn', f"f_gate_up_dw_{i}", out_dtype=BF16, out_panels=True)
        dh2 = _matmul(dgu, w['f_w_gate_up'][i], 'nt', f"f_gate_up_dx_{i}")
        dx1, per_layer['ffn_norm'][i] = _rmsnorm_bwd(s['x1'], rep['ffn_norm'][i:i + 1], dh2, dx, f"ffn_norm_bwd_{i}")
        if kind == 0:
            per_layer['a_w_o'][j] = _matmul(s['o'], dx1, 'tn', f"a_out_dw_{i}", out_dtype=BF16)
            do = _matmul(dx1, w['a_w_o'][j], 'nt', f"a_out_dx_{i}")
            dqkv_r, per_layer['a_sinks'][j] = _swa_bwd(s['qkv_r'], s['o'], s['lse'], do, rep['a_sinks'][j:j + 1],
                                                       f"a_attn_bwd_{i}")
            dqkv, per_layer['a_q_norm'][j], per_layer['a_k_norm'][j] = _swa_prep_bwd(
                s['qkv'], dqkv_r, rep['a_q_norm'][j:j + 1], rep['a_k_norm'][j:j + 1], cos_a, sin_a, f"a_prep_bwd_{i}")
            per_layer['a_w_qkv'][j] = _matmul(s['h1'], dqkv, 'tn', f"a_qkv_dw_{i}", out_dtype=BF16)
            dh1 = _matmul(dqkv, w['a_w_qkv'][j], 'nt', f"a_qkv_dx_{i}")
        elif kind == 1:
            per_layer['b_w_out'][j] = _matmul(s['by'], dx1, 'tn', f"b_out_dw_{i}", out_dtype=BF16)
            dby = _matmul(dx1, w['b_w_out'][j], 'nt', f"b_out_dx_{i}")
            dbcu, per_layer['b_conv_w'][j] = _sconv_bwd(s['bcu'], dby, w['b_conv_w'][j], f"b_conv_bwd_{i}")
            per_layer['b_w_in'][j] = _matmul(s['h1'], dbcu, 'tn', f"b_in_dw_{i}", out_dtype=BF16)
            dh1 = _matmul(dbcu, w['b_w_in'][j], 'nt', f"b_in_dx_{i}")
        else:
            per_layer['c_w_o'][j] = _matmul(s['o'], dx1, 'tn', f"c_out_dw_{i}", out_dtype=BF16)
            do = _matmul(dx1, w['c_w_o'][j], 'nt', f"c_out_dx_{i}")
            dq, dlt = _mla_bwd_dq(s['q'], s['k'], s['v'], do, s['o'], s['lse'], f"c_attn_bwd_dq_{i}")
            blk = min(MLA_BLOCK, do.shape[0])
            as_rows = lambda col: col.reshape(C_HEADS, do.shape[0] // blk, 1, blk)
            dk, dv = _mla_bwd_dkv(s['q'], s['k'], s['v'], do.astype(BF16), as_rows(s['lse']), as_rows(dlt),
                                  f"c_attn_bwd_dkv_{i}")
            dqw, dkvw, dkrope, per_layer['c_q_norm'][j], per_layer['c_k_norm'][j] = _mla_qk_bwd(
                s['qw'], s['kvw'], s['down'], dq, dk, dv, rep['c_q_norm'][j:j + 1], rep['c_k_norm'][j:j + 1], cos_c, sin_c,
                f"c_prep_bwd_{i}")
            per_layer['c_w_q_up'][j] = _matmul(s['cq'], dqw, 'tn', f"c_q_up_dw_{i}", out_dtype=BF16)
            dcq = _matmul(dqw, w['c_w_q_up'][j], 'nt', f"c_q_up_dx_{i}")
            per_layer['c_w_kv_up'][j] = _matmul(s['ckv'], dkvw, 'tn', f"c_kv_up_dw_{i}", out_dtype=BF16)
            dckv = _matmul(dkvw, w['c_w_kv_up'][j], 'nt', f"c_kv_up_dx_{i}")
            ddown, per_layer['c_q_a_norm'][j], per_layer['c_kv_a_norm'][j] = _mla_latent_bwd(
                s['down'], dcq, dckv, dkrope, w['c_q_a_norm'][j:j + 1], w['c_kv_a_norm'][j:j + 1], f"c_latent_bwd_{i}")
            per_layer['c_w_down'][j] = _matmul(s['h1'], ddown, 'tn', f"c_down_dw_{i}", out_dtype=BF16)
            dh1 = _matmul(ddown, w['c_w_down'][j], 'nt', f"c_down_dx_{i}")
        dx, per_layer['mix_norm'][i] = _rmsnorm_bwd(s['x'], rep['mix_norm'][i:i + 1], dh1, dx1, f"mix_norm_bwd_{i}")

    grads = {}
    for n in WEIGHTS:
        stacked = jnp.stack([per_layer[n][j] for j in sorted(per_layer[n])])
        if n in ('mix_norm', 'ffn_norm', 'a_q_norm', 'a_k_norm', 'a_sinks', 'c_q_a_norm', 'c_kv_a_norm', 'c_q_norm', 'c_k_norm'):
            stacked = stacked.reshape(stacked.shape[0], stacked.shape[-1])
        if n in SHARD_AXIS and n not in PANEL_WEIGHTS:
            stacked = _split_shards(stacked, SHARD_AXIS[n])
        grads[n] = stacked
    return loss, dx, grads


def kernel(x, positions, mix_norm, ffn_norm, a_w_qkv, a_q_norm, a_k_norm, a_sinks, a_w_o, b_w_in, b_conv_w, b_w_out, c_w_down, c_q_a_norm, c_kv_a_norm, c_w_q_up, c_w_kv_up, c_q_norm, c_k_norm, c_w_o, f_w_gate_up, f_w_down, loss_target, m_mix_norm, m_ffn_norm, m_a_w_qkv, m_a_q_norm, m_a_k_norm, m_a_sinks, m_a_w_o, m_b_w_in, m_b_conv_w, m_b_w_out, m_c_w_down, m_c_q_a_norm, m_c_kv_a_norm, m_c_w_q_up, m_c_w_kv_up, m_c_q_norm, m_c_k_norm, m_c_w_o, m_f_w_gate_up, m_f_w_down, v_mix_norm, v_ffn_norm, v_a_w_qkv, v_a_q_norm, v_a_k_norm, v_a_sinks, v_a_w_o, v_b_w_in, v_b_conv_w, v_b_w_out, v_c_w_down, v_c_q_a_norm, v_c_kv_a_norm, v_c_w_q_up, v_c_w_kv_up, v_c_q_norm, v_c_k_norm, v_c_w_o, v_f_w_gate_up, v_f_w_down):
    local = dict(mix_norm=mix_norm, ffn_norm=ffn_norm, a_w_qkv=a_w_qkv, a_q_norm=a_q_norm, a_k_norm=a_k_norm, a_sinks=a_sinks, a_w_o=a_w_o, b_w_in=b_w_in, b_conv_w=b_conv_w, b_w_out=b_w_out, c_w_down=c_w_down, c_q_a_norm=c_q_a_norm, c_kv_a_norm=c_kv_a_norm, c_w_q_up=c_w_q_up, c_w_kv_up=c_w_kv_up, c_q_norm=c_q_norm, c_k_norm=c_k_norm, c_w_o=c_w_o, f_w_gate_up=f_w_gate_up, f_w_down=f_w_down)
    mom1 = dict(mix_norm=m_mix_norm, ffn_norm=m_ffn_norm, a_w_qkv=m_a_w_qkv, a_q_norm=m_a_q_norm, a_k_norm=m_a_k_norm, a_sinks=m_a_sinks, a_w_o=m_a_w_o, b_w_in=m_b_w_in, b_conv_w=m_b_conv_w, b_w_out=m_b_w_out, c_w_down=m_c_w_down, c_q_a_norm=m_c_q_a_norm, c_kv_a_norm=m_c_kv_a_norm, c_w_q_up=m_c_w_q_up, c_w_kv_up=m_c_w_kv_up, c_q_norm=m_c_q_norm, c_k_norm=m_c_k_norm, c_w_o=m_c_w_o, f_w_gate_up=m_f_w_gate_up, f_w_down=m_f_w_down)
    mom2 = dict(mix_norm=v_mix_norm, ffn_norm=v_ffn_norm, a_w_qkv=v_a_w_qkv, a_q_norm=v_a_q_norm, a_k_norm=v_a_k_norm, a_sinks=v_a_sinks, a_w_o=v_a_w_o, b_w_in=v_b_w_in, b_conv_w=v_b_conv_w, b_w_out=v_b_w_out, c_w_down=v_c_w_down, c_q_a_norm=v_c_q_a_norm, c_kv_a_norm=v_c_kv_a_norm, c_w_q_up=v_c_w_q_up, c_w_kv_up=v_c_w_kv_up, c_q_norm=v_c_q_norm, c_k_norm=v_c_k_norm, c_w_o=v_c_w_o, f_w_gate_up=v_f_w_gate_up, f_w_down=v_f_w_down)
    t, d = x.shape[1], x.shape[2]

    gathered = _exchange([local[n].astype(BF16) if n in GATHER_BF16 else local[n] for n in SHARDED], False,
                         "gather_weights")
    full = {n: g if n in PANEL_WEIGHTS else _join_shards(g, SHARD_AXIS[n]) for n, g in zip(SHARDED, gathered)}
    rep = {n: local[n] for n in REPLICATED}

    loss, grad_x, grads = _forward_backward(x.reshape(t, d), positions.reshape(t), loss_target.reshape(t, d), full, rep)

    out_g, out_d, out_m, out_v = {}, {}, {}, {}

    def update(names, parts):
        for n, part in zip(names, parts):
            shape = local[n].shape if n in SHARD_AXIS else (1,) + local[n].shape
            view = _as_rows(shape)
            results = _adamw(part.reshape(view[0], N_DEV, view[1], view[2]),
                             *[src[n].reshape(view) for src in (local, mom1, mom2)], name="adamw_" + n)
            for dst, res in zip((out_g, out_d, out_m, out_v), results):
                dst[n] = res.reshape(local[n].shape)

    update(SHARDED, _exchange([grads[n] for n in SHARDED], True, "scatter_gradients"))
    update(REPLICATED, _exchange([grads[n].reshape((1,) + grads[n].shape) for n in REPLICATED], False,
                                 "gather_small_gradients"))

    loss = lax.psum(loss.reshape(()), MESH_AXES)
    outs = [loss, grad_x.reshape(1, t, d)]
    for res in (out_g, out_d, out_m, out_v):
        outs += [res[n] for n in WEIGHTS]
    return tuple(outs)
```

```python
import functools

import jax
import jax.numpy as jnp
from jax import lax
from jax.experimental import pallas as pl
from jax.experimental.pallas import tpu as pltpu

F32 = jnp.float32
BF16 = jnp.bfloat16

N_DEV = 8
MESH_AXES = ("x", "y", "c")

DEPTH = 4
N_MIXERS = 3
ROPE_THETA = 500000.0
EPS = 1e-6
A_HEADS, A_KV_HEADS, A_HEAD_DIM, A_ROT_DIM, A_WINDOW = 16, 4, 64, 16, 128
A_GROUP = A_HEADS // A_KV_HEADS
C_HEADS, C_NOPE, C_ROPE, C_V, C_Q_RANK, C_KV_RANK = 16, 64, 32, 64, 384, 256
C_QK = C_NOPE + C_ROPE
ADAM_LR, ADAM_B1, ADAM_B2, ADAM_EPS, ADAM_WD, ADAM_STEP = 0.001, 0.9, 0.999, 1e-08, 0.01, 10

VMEM_LIMIT_BYTES = 48 * 1024 * 1024
LANES = 128
NEG = -1e30
MLA_BLOCK = 512
MLA_FWD_BLOCK = 1024

WEIGHTS = ['mix_norm', 'ffn_norm', 'a_w_qkv', 'a_q_norm', 'a_k_norm', 'a_sinks', 'a_w_o', 'b_w_in', 'b_conv_w', 'b_w_out',
           'c_w_down', 'c_q_a_norm', 'c_kv_a_norm', 'c_w_q_up', 'c_w_kv_up', 'c_q_norm', 'c_k_norm', 'c_w_o', 'f_w_gate_up',
           'f_w_down']
SHARD_AXIS = {'a_w_qkv': 2, 'a_w_o': 1, 'b_w_in': 2, 'b_conv_w': 2, 'b_w_out': 1, 'c_w_down': 1, 'c_q_a_norm': 1,
              'c_kv_a_norm': 1, 'c_w_q_up': 2, 'c_w_kv_up': 2, 'c_w_o': 1, 'f_w_gate_up': 2, 'f_w_down': 1}
SHARDED = [n for n in WEIGHTS if n in SHARD_AXIS]
REPLICATED = [n for n in WEIGHTS if n not in SHARD_AXIS]
GATHER_F32 = ['b_conv_w', 'c_q_a_norm', 'c_kv_a_norm']
GATHER_BF16 = [n for n in SHARDED if n not in GATHER_F32]

def _params(semantics=None):
    return pltpu.CompilerParams(dimension_semantics=semantics, vmem_limit_bytes=VMEM_LIMIT_BYTES)


def _div_tile(n, cap, mult=LANES):
    best = None
    t = mult
    while t <= min(n, cap):
        if n % t == 0:
            best = t
        t += mult
    return n if best is None else best


def _exchange(arrays, scatter, name):
    n = len(arrays)

    def body(*refs):
        src_refs, out_refs = refs[:n], refs[n:2 * n]
        send_sems, recv_sems, local_sems = refs[2 * n:]
        x, y, c = lax.axis_index("x"), lax.axis_index("y"), lax.axis_index("c")
        me = 4 * x + 2 * y + c
        copies = []
        for a in range(n):
            def piece(idx, a=a):
                return src_refs[a].at[:, idx] if scatter else src_refs[a]

            local = pltpu.make_async_copy(piece(me), out_refs[a].at[:, me], local_sems.at[a])
            local.start()
            copies.append(local)
            for r in range(1, N_DEV):
                px = 1 - x if (r >> 2) & 1 else x
                py = 1 - y if (r >> 1) & 1 else y
                pc = 1 - c if r & 1 else c
                cp = pltpu.make_async_remote_copy(
                    src_ref=piece(4 * px + 2 * py + pc), dst_ref=out_refs[a].at[:, me],
                    send_sem=send_sems.at[a, r - 1], recv_sem=recv_sems.at[a, r - 1],
                    device_id=(px, py, pc), device_id_type=pl.DeviceIdType.MESH)
                cp.start()
                copies.append(cp)
        for cp in copies:
            cp.wait()

    def out_shape(arr):
        rest = arr.shape[2:] if scatter else arr.shape[1:]
        return jax.ShapeDtypeStruct((arr.shape[0], N_DEV) + tuple(rest), arr.dtype)

    return pl.pallas_call(
        body, name=name,
        out_shape=tuple(out_shape(arr) for arr in arrays),
        in_specs=[pl.BlockSpec(memory_space=pl.ANY)] * n,
        out_specs=tuple(pl.BlockSpec(memory_space=pl.ANY) for _ in range(n)),
        scratch_shapes=[pltpu.SemaphoreType.DMA((n, N_DEV - 1)), pltpu.SemaphoreType.DMA((n, N_DEV - 1)),
                        pltpu.SemaphoreType.DMA((n,))],
    )(*arrays)


def _matmul(a, b, mode, name, out_dtype=F32, residual=None):
    if mode == 'nn':
        (m, k), (k2, n) = a.shape, b.shape
    elif mode == 'nt':
        (m, k), (n, k2) = a.shape, b.shape
    else:
        (k, m), (k2, n) = a.shape, b.shape
    assert k == k2, (name, a.shape, b.shape, mode)
    if mode == 'tn':
        tm, tk = _div_tile(m, 1408), _div_tile(k, 512, 16)
    else:
        tk = _div_tile(k, 1536)
        tm = _div_tile(m, 1024 if tk == k else 512, 16)
    tn = _div_tile(n, 1408)
    nk = k // tk
    dims = {'nn': (((1,), (0,)), ((), ())), 'nt': (((1,), (1,)), ((), ())), 'tn': (((0,), (0,)), ((), ()))}[mode]

    def product(a_ref, b_ref):
        return lax.dot_general(a_ref[...].astype(BF16), b_ref[...].astype(BF16), dims, preferred_element_type=F32)

    def finish(r, rest):
        if residual is not None:
            r = r + rest[0][...]
        rest[-1 if nk == 1 else -2][...] = r.astype(out_dtype)

    def body_single(a_ref, b_ref, *rest):
        finish(product(a_ref, b_ref), rest)

    def body_accumulate(a_ref, b_ref, *rest):
        acc = rest[-1]
        kk = pl.program_id(2)

        @pl.when(kk == 0)
        def _():
            acc[...] = jnp.zeros_like(acc)

        acc[...] += product(a_ref, b_ref)

        @pl.when(kk == nk - 1)
        def _():
            finish(acc[...], rest)

    a_spec = pl.BlockSpec((tk, tm), lambda i, j, kk: (kk, i)) if mode == 'tn' else pl.BlockSpec((tm, tk), lambda i, j, kk: (i, kk))
    b_spec = pl.BlockSpec((tn, tk), lambda i, j, kk: (j, kk)) if mode == 'nt' else pl.BlockSpec((tk, tn), lambda i, j, kk: (kk, j))
    o_spec = pl.BlockSpec((tm, tn), lambda i, j, kk: (i, j))
    in_specs, operands = [a_spec, b_spec], [a, b]
    if residual is not None:
        in_specs.append(o_spec)
        operands.append(residual)
    return pl.pallas_call(
        body_single if nk == 1 else body_accumulate, name=name, out_shape=jax.ShapeDtypeStruct((m, n), out_dtype),
        grid=(m // tm, n // tn, nk), in_specs=in_specs, out_specs=o_spec,
        scratch_shapes=[] if nk == 1 else [pltpu.VMEM((tm, tn), F32)],
        compiler_params=_params(("parallel", "parallel", "arbitrary")),
    )(*operands)


def _row_spec(tm, cols):
    return pl.BlockSpec((tm, cols), lambda i: (i, 0))


def _const_spec(shape):
    return pl.BlockSpec(shape, lambda i: tuple(0 for _ in shape))


def _accumulate(ref, value, step):
    @pl.when(step == 0)
    def _():
        ref[...] = value

    @pl.when(step > 0)
    def _():
        ref[...] += value


def _rstd(x):
    return lax.rsqrt(jnp.mean(x * x, axis=-1, keepdims=True) + EPS)


def _norm_bwd(x, g, dout):
    xn = x * _rstd(x)
    dg = jnp.sum(dout * xn, axis=0, keepdims=True)
    dxn = dout * g
    dx = _rstd(x) * (dxn - xn * jnp.mean(dxn * xn, axis=-1, keepdims=True))
    return dx, dg


def _rmsnorm_fwd(x, g, name):
    t, d = x.shape
    tm = _div_tile(t, 512, 16)

    def body(x_ref, g_ref, o_ref):
        xv = x_ref[...]
        o_ref[...] = (xv * _rstd(xv) * g_ref[...]).astype(BF16)

    return pl.pallas_call(
        body, name=name, out_shape=jax.ShapeDtypeStruct((t, d), BF16), grid=(t // tm,),
        in_specs=[_row_spec(tm, d), _const_spec((1, d))], out_specs=_row_spec(tm, d),
        compiler_params=_params(("parallel",)),
    )(x, g)


def _rmsnorm_bwd(x, g, dh, dres, name):
    t, d = x.shape
    tm = _div_tile(t, 512, 8)

    def body(x_ref, g_ref, dh_ref, dres_ref, dx_ref, dg_ref):
        dx, dg = _norm_bwd(x_ref[...], g_ref[...], dh_ref[...])
        dx_ref[...] = dres_ref[...] + dx
        _accumulate(dg_ref, dg, pl.program_id(0))

    return pl.pallas_call(
        body, name=name,
        out_shape=(jax.ShapeDtypeStruct((t, d), F32), jax.ShapeDtypeStruct((1, d), F32)), grid=(t // tm,),
        in_specs=[_row_spec(tm, d), _const_spec((1, d)), _row_spec(tm, d), _row_spec(tm, d)],
        out_specs=(_row_spec(tm, d), _const_spec((1, d))),
        compiler_params=_params(("arbitrary",)),
    )(x, g, dh, dres)


def _sigmoid(x):
    return 1.0 / (1.0 + jnp.exp(-x))


def _swiglu_fwd(gu, name):
    t, f2 = gu.shape
    f = f2 // 2
    tm = _div_tile(t, 512, 16)

    def body(gu_ref, o_ref):
        gate, up = gu_ref[:, :f].astype(F32), gu_ref[:, f:].astype(F32)
        o_ref[...] = (gate * _sigmoid(gate) * up).astype(BF16)

    return pl.pallas_call(
        body, name=name, out_shape=jax.ShapeDtypeStruct((t, f), BF16), grid=(t // tm,),
        in_specs=[_row_spec(tm, f2)], out_specs=_row_spec(tm, f),
        compiler_params=_params(("parallel",)),
    )(gu)


def _swiglu_bwd(gu, da, name):
    t, f2 = gu.shape
    f = f2 // 2
    tm = _div_tile(t, 512, 16)

    def body(gu_ref, da_ref, o_ref):
        gate, up, dav = gu_ref[:, :f].astype(F32), gu_ref[:, f:].astype(F32), da_ref[...].astype(F32)
        sig = _sigmoid(gate)
        o_ref[:, :f] = (dav * up * (sig * (1.0 + gate * (1.0 - sig)))).astype(BF16)
        o_ref[:, f:] = (dav * (gate * sig)).astype(BF16)

    return pl.pallas_call(
        body, name=name, out_shape=jax.ShapeDtypeStruct((t, f2), BF16), grid=(t // tm,),
        in_specs=[_row_spec(tm, f2), _row_spec(tm, f)], out_specs=_row_spec(tm, f2),
        compiler_params=_params(("parallel",)),
    )(gu, da)


def _loss_head(y, target, name):
    t, d = y.shape
    tm = _div_tile(t, 512, 8)

    def body(y_ref, t_ref, loss_ref, dy_ref):
        diff = y_ref[...] - t_ref[...]
        dy_ref[...] = diff * (1.0 / d)
        part = jnp.sum(jnp.sum(diff * diff, axis=1, keepdims=True), axis=0, keepdims=True) * (0.5 / d)
        _accumulate(loss_ref, part, pl.program_id(0))

    return pl.pallas_call(
        body, name=name,
        out_shape=(jax.ShapeDtypeStruct((1, 1), F32), jax.ShapeDtypeStruct((t, d), F32)), grid=(t // tm,),
        in_specs=[_row_spec(tm, d), _row_spec(tm, d)], out_specs=(_const_spec((1, 1)), _row_spec(tm, d)),
        compiler_params=_params(("arbitrary",)),
    )(y, target)


HALO = 8


def _shift_down(z, k, halo_rows):
    tm = z.shape[0]
    row = lax.broadcasted_iota(jnp.int32, z.shape, 0)
    out = pltpu.roll(z, k, 0)
    for j in range(k):
        out = jnp.where(row == j, halo_rows[HALO - k + j:HALO - k + j + 1, :], out)
    return out


def _shift_up(z, k, halo_rows):
    tm = z.shape[0]
    row = lax.broadcasted_iota(jnp.int32, z.shape, 0)
    out = pltpu.roll(z, tm - k, 0)
    for j in range(k):
        out = jnp.where(row == tm - k + j, halo_rows[j:j + 1, :], out)
    return out


def _sconv_specs(t, tm, cols):
    per = tm // HALO
    last = t // HALO - 1
    cur = pl.BlockSpec((tm, cols), lambda i: (i, 0))
    prev = pl.BlockSpec((HALO, cols), lambda i: (jnp.maximum(i * per - 1, 0), 0))
    nxt = pl.BlockSpec((HALO, cols), lambda i: (jnp.minimum((i + 1) * per, last), 0))
    return cur, prev, nxt


def _sconv_fwd(bcu, conv_w, name):
    t, d3 = bcu.shape
    d = d3 // 3
    tm = _div_tile(t, 256, 16)
    cur, prev, _ = _sconv_specs(t, tm, d3)

    def body(cur_ref, prev_ref, w_ref, o_ref):
        i = pl.program_id(0)
        z = cur_ref[:, d:2 * d] * cur_ref[:, 2 * d:]
        zp = prev_ref[:, d:2 * d] * prev_ref[:, 2 * d:] * (i > 0).astype(F32)
        y = w_ref[0:1, :] * _shift_down(z, 2, zp) + w_ref[1:2, :] * _shift_down(z, 1, zp) + w_ref[2:3, :] * z
        o_ref[...] = (cur_ref[:, :d] * y).astype(BF16)

    return pl.pallas_call(
        body, name=name, out_shape=jax.ShapeDtypeStruct((t, d), BF16), grid=(t // tm,),
        in_specs=[cur, prev, _const_spec((3, d))], out_specs=_row_spec(tm, d),
        compiler_params=_params(("parallel",)),
    )(bcu, bcu, conv_w)


def _sconv_bwd(bcu, dout, conv_w, name):
    t, d3 = bcu.shape
    d = d3 // 3
    tm = _div_tile(t, 256, 16)
    cur, prev, nxt = _sconv_specs(t, tm, d3)
    dcur, _, dnxt = _sconv_specs(t, tm, d)
    n_tiles = t // tm

    def body(cur_ref, prev_ref, nxt_ref, do_ref, don_ref, w_ref, o_ref, dw_ref):
        i = pl.program_id(0)
        b, cg, u = cur_ref[:, :d], cur_ref[:, d:2 * d], cur_ref[:, 2 * d:]
        z = cg * u
        zp = prev_ref[:, d:2 * d] * prev_ref[:, 2 * d:] * (i > 0).astype(F32)
        z1, z2 = _shift_down(z, 1, zp), _shift_down(z, 2, zp)
        w0, w1, w2 = w_ref[0:1, :], w_ref[1:2, :], w_ref[2:3, :]
        y = w0 * z2 + w1 * z1 + w2 * z
        dov = do_ref[...]
        dy = dov * b
        dyn = don_ref[...] * nxt_ref[:, :d] * (i < n_tiles - 1).astype(F32)
        dz = w2 * dy + w1 * _shift_up(dy, 1, dyn) + w0 * _shift_up(dy, 2, dyn)
        o_ref[:, :d] = (dov * y).astype(BF16)
        o_ref[:, d:2 * d] = (dz * u).astype(BF16)
        o_ref[:, 2 * d:] = (dz * cg).astype(BF16)
        dw = jnp.concatenate([jnp.sum(dy * z2, axis=0, keepdims=True), jnp.sum(dy * z1, axis=0, keepdims=True),
                              jnp.sum(dy * z, axis=0, keepdims=True)], axis=0)
        _accumulate(dw_ref, dw, i)

    return pl.pallas_call(
        body, name=name,
        out_shape=(jax.ShapeDtypeStruct((t, d3), BF16), jax.ShapeDtypeStruct((3, d), F32)), grid=(n_tiles,),
        in_specs=[cur, prev, nxt, dcur, dnxt, _const_spec((3, d))],
        out_specs=(_row_spec(tm, d3), _const_spec((3, d))),
        compiler_params=_params(("arbitrary",)),
    )(bcu, bcu, bcu, dout, dout, conv_w)


def _rope_tables(positions, rot, lead, trail):
    inv_freq = ROPE_THETA ** (-jnp.arange(0, rot, 2, dtype=F32) / rot)
    ang = positions.astype(F32)[:, None] * inv_freq
    cos, sin = jnp.cos(ang), jnp.sin(ang)
    t = positions.shape[0]
    cos_full = jnp.concatenate([jnp.ones((t, lead), F32), cos, cos, jnp.ones((t, trail), F32)], axis=1)
    sin_full = jnp.concatenate([jnp.zeros((t, lead), F32), -sin, sin, jnp.zeros((t, trail), F32)], axis=1)
    return cos_full, sin_full


def _swap_halves(x, lead, rot):
    half = rot // 2
    rows, d = x.shape
    parts = []
    if lead:
        parts.append(jnp.zeros((rows, lead), x.dtype))
    parts += [x[:, lead + half:lead + rot], x[:, lead:lead + half]]
    if d - lead - rot:
        parts.append(jnp.zeros((rows, d - lead - rot), x.dtype))
    return jnp.concatenate(parts, axis=1)


def _head_fwd(x, g, cos, sin, lead, rot):
    n = x * _rstd(x) * g
    return n * cos + _swap_halves(n, lead, rot) * sin


def _head_bwd(x, g, cos, sin, dout, lead, rot):
    dn = dout * cos + _swap_halves(dout * sin, lead, rot)
    return _norm_bwd(x, g, dn)


A_Q_COLS = A_HEADS * A_HEAD_DIM
A_KV_COLS = A_KV_HEADS * A_HEAD_DIM
A_COLS = A_Q_COLS + 2 * A_KV_COLS
A_SCALE = A_HEAD_DIM ** -0.5


def _swa_prep_fwd(qkv, q_norm, k_norm, cos, sin, name):
    t = qkv.shape[0]
    tm = _div_tile(t, 256, 16)
    hd = A_HEAD_DIM

    def body(x_ref, gq_ref, gk_ref, cos_ref, sin_ref, o_ref):
        cosv, sinv = cos_ref[...], sin_ref[...]
        for h in range(A_HEADS + A_KV_HEADS):
            g = gq_ref[...] if h < A_HEADS else gk_ref[...]
            o_ref[:, h * hd:(h + 1) * hd] = _head_fwd(x_ref[:, h * hd:(h + 1) * hd], g, cosv, sinv, 0, A_ROT_DIM).astype(BF16)
        o_ref[:, A_Q_COLS + A_KV_COLS:] = x_ref[:, A_Q_COLS + A_KV_COLS:].astype(BF16)

    return pl.pallas_call(
        body, name=name, out_shape=jax.ShapeDtypeStruct((t, A_COLS), BF16), grid=(t // tm,),
        in_specs=[_row_spec(tm, A_COLS), _const_spec((1, hd)), _const_spec((1, hd)), _row_spec(tm, hd), _row_spec(tm, hd)],
        out_specs=_row_spec(tm, A_COLS), compiler_params=_params(("parallel",)),
    )(qkv, q_norm, k_norm, cos, sin)


def _swa_prep_bwd(qkv, dqkv_r, q_norm, k_norm, cos, sin, name):
    t = qkv.shape[0]
    tm = _div_tile(t, 256, 16)
    hd = A_HEAD_DIM

    def body(x_ref, d_ref, gq_ref, gk_ref, cos_ref, sin_ref, o_ref, dgq_ref, dgk_ref):
        cosv, sinv = cos_ref[...], sin_ref[...]
        dgq = jnp.zeros((1, hd), F32)
        dgk = jnp.zeros((1, hd), F32)
        for h in range(A_HEADS + A_KV_HEADS):
            sl = slice(h * hd, (h + 1) * hd)
            g = gq_ref[...] if h < A_HEADS else gk_ref[...]
            dx, dg = _head_bwd(x_ref[:, sl], g, cosv, sinv, d_ref[:, sl], 0, A_ROT_DIM)
            o_ref[:, sl] = dx.astype(BF16)
            if h < A_HEADS:
                dgq = dgq + dg
            else:
                dgk = dgk + dg
        o_ref[:, A_Q_COLS + A_KV_COLS:] = d_ref[:, A_Q_COLS + A_KV_COLS:].astype(BF16)
        _accumulate(dgq_ref, dgq, pl.program_id(0))
        _accumulate(dgk_ref, dgk, pl.program_id(0))

    return pl.pallas_call(
        body, name=name,
        out_shape=(jax.ShapeDtypeStruct((t, A_COLS), BF16), jax.ShapeDtypeStruct((1, hd), F32),
                   jax.ShapeDtypeStruct((1, hd), F32)),
        grid=(t // tm,),
        in_specs=[_row_spec(tm, A_COLS), _row_spec(tm, A_COLS), _const_spec((1, hd)), _const_spec((1, hd)),
                  _row_spec(tm, hd), _row_spec(tm, hd)],
        out_specs=(_row_spec(tm, A_COLS), _const_spec((1, hd)), _const_spec((1, hd))),
        compiler_params=_params(("arbitrary",)),
    )(qkv, dqkv_r, q_norm, k_norm, cos, sin)


def _group_rows(ref, k, width=A_HEAD_DIM, base=0):
    return jnp.concatenate([ref[:, base + (A_GROUP * k + g) * width:base + (A_GROUP * k + g + 1) * width]
                            for g in range(A_GROUP)], axis=0)


def _group_column(ref, k, rows):
    cols = []
    for g in range(A_GROUP):
        h = A_GROUP * k + g
        col = ref[:, h:h + 1]
        cols.append(jnp.broadcast_to(col, (rows, 1)) if col.shape[0] == 1 else col)
    return jnp.concatenate(cols, axis=0)


def _swa_fwd(qkv_r, sinks, name):
    t = qkv_r.shape[0]
    blk = A_WINDOW
    nb = t // blk
    hd = A_HEAD_DIM
    kv_block = A_Q_COLS // (2 * A_KV_COLS)

    def body(q_ref, kvc_ref, kvp_ref, s_ref, o_ref, lse_ref):
        n = pl.program_id(0)
        shape = (A_GROUP * blk, 2 * blk)
        qpos = lax.broadcasted_iota(jnp.int32, shape, 0) & (blk - 1)
        col = lax.broadcasted_iota(jnp.int32, shape, 1)
        delta = qpos + blk - col
        valid = (delta >= 0) & (delta < A_WINDOW) & ((col >= blk) | (n > 0))
        for k in range(A_KV_HEADS):
            qg = _group_rows(q_ref, k)
            kw = jnp.concatenate([kvp_ref[:, k * hd:(k + 1) * hd], kvc_ref[:, k * hd:(k + 1) * hd]], axis=0)
            vw = jnp.concatenate([kvp_ref[:, A_KV_COLS + k * hd:A_KV_COLS + (k + 1) * hd],
                                  kvc_ref[:, A_KV_COLS + k * hd:A_KV_COLS + (k + 1) * hd]], axis=0)
            s = lax.dot_general(qg, kw, (((1,), (1,)), ((), ())), preferred_element_type=F32) * A_SCALE
            s = jnp.where(valid, s, NEG)
            sink = _group_column(s_ref, k, blk)
            m = jnp.maximum(jnp.max(s, axis=-1, keepdims=True), sink)
            p = jnp.exp(s - m)
            denom = jnp.sum(p, axis=-1, keepdims=True) + jnp.exp(sink - m)
            p = p / denom
            o = jnp.dot(p.astype(BF16), vw, preferred_element_type=F32)
            lse = m + jnp.log(denom)
            for g in range(A_GROUP):
                h = A_GROUP * k + g
                o_ref[:, h * hd:(h + 1) * hd] = o[g * blk:(g + 1) * blk].astype(BF16)
                lse_ref[:, h:h + 1] = lse[g * blk:(g + 1) * blk]

    return pl.pallas_call(
        body, name=name,
        out_shape=(jax.ShapeDtypeStruct((t, A_Q_COLS), BF16), jax.ShapeDtypeStruct((t, A_HEADS), F32)), grid=(nb,),
        in_specs=[pl.BlockSpec((blk, A_Q_COLS), lambda n: (n, 0)),
                  pl.BlockSpec((blk, 2 * A_KV_COLS), lambda n: (n, kv_block)),
                  pl.BlockSpec((blk, 2 * A_KV_COLS), lambda n: (jnp.maximum(n - 1, 0), kv_block)),
                  _const_spec((1, A_HEADS))],
        out_specs=(pl.BlockSpec((blk, A_Q_COLS), lambda n: (n, 0)), pl.BlockSpec((blk, A_HEADS), lambda n: (n, 0))),
        compiler_params=_params(("parallel",)),
    )(qkv_r, qkv_r, qkv_r, sinks)


def _swa_bwd(qkv_r, o, lse, do, sinks, name):
    t = qkv_r.shape[0]
    blk = A_WINDOW
    nb = t // blk
    hd = A_HEAD_DIM
    kv_block = A_Q_COLS // (2 * A_KV_COLS)
    rows = A_GROUP * blk

    def nxt(n):
        return jnp.minimum(n + 1, nb - 1)

    def body(qc_ref, qn_ref, kvc_ref, kvp_ref, doc_ref, don_ref, oc_ref, on_ref, lc_ref, ln_ref, s_ref, dx_ref, ds_ref):
        n = pl.program_id(0)
        shape = (2 * rows, 2 * blk)
        row = lax.broadcasted_iota(jnp.int32, shape, 0)
        col = lax.broadcasted_iota(jnp.int32, shape, 1)
        is_next = row >= rows
        delta = jnp.where(is_next, blk, 0) + blk + (row & (blk - 1)) - col
        valid = ((delta >= 0) & (delta < A_WINDOW) & ((col >= blk) | (n > 0)) & (jnp.logical_not(is_next) | (n < nb - 1)))
        dsink_cols = []
        for k in range(A_KV_HEADS):
            qs = jnp.concatenate([_group_rows(qc_ref, k), _group_rows(qn_ref, k)], axis=0)
            dos = jnp.concatenate([_group_rows(doc_ref, k), _group_rows(don_ref, k)], axis=0)
            os_ = jnp.concatenate([_group_rows(oc_ref, k), _group_rows(on_ref, k)], axis=0).astype(F32)
            lses = jnp.concatenate([_group_column(lc_ref, k, blk), _group_column(ln_ref, k, blk)], axis=0)
            kw = jnp.concatenate([kvp_ref[:, k * hd:(k + 1) * hd], kvc_ref[:, k * hd:(k + 1) * hd]], axis=0)
            vw = jnp.concatenate([kvp_ref[:, A_KV_COLS + k * hd:A_KV_COLS + (k + 1) * hd],
                                  kvc_ref[:, A_KV_COLS + k * hd:A_KV_COLS + (k + 1) * hd]], axis=0)
            s = lax.dot_general(qs, kw, (((1,), (1,)), ((), ())), preferred_element_type=F32) * A_SCALE
            p = jnp.exp(jnp.where(valid, s - lses, NEG))
            dos_b = dos.astype(BF16)
            dp = lax.dot_general(dos_b, vw, (((1,), (1,)), ((), ())), preferred_element_type=F32)
            dlt = jnp.sum(dos * os_, axis=-1, keepdims=True)
            ds = p * (dp - dlt)
            dq = jnp.dot(ds[:rows].astype(BF16), kw, preferred_element_type=F32) * A_SCALE
            dk = lax.dot_general(ds[:, blk:].astype(BF16), qs, (((0,), (0,)), ((), ())), preferred_element_type=F32) * A_SCALE
            dv = lax.dot_general(p[:, blk:].astype(BF16), dos_b, (((0,), (0,)), ((), ())), preferred_element_type=F32)
            for g in range(A_GROUP):
                h = A_GROUP * k + g
                dx_ref[:, h * hd:(h + 1) * hd] = dq[g * blk:(g + 1) * blk]
            dx_ref[:, A_Q_COLS + k * hd:A_Q_COLS + (k + 1) * hd] = dk
            dx_ref[:, A_Q_COLS + A_KV_COLS + k * hd:A_Q_COLS + A_KV_COLS + (k + 1) * hd] = dv
            sink = _group_column(s_ref, k, blk)
            contrib = -jnp.exp(sink - lses[:rows]) * dlt[:rows]
            for g in range(A_GROUP):
                dsink_cols.append(jnp.sum(contrib[g * blk:(g + 1) * blk], axis=0, keepdims=True))
        _accumulate(ds_ref, jnp.concatenate(dsink_cols, axis=1), n)

    q_spec = lambda f: pl.BlockSpec((blk, A_Q_COLS), lambda n: (f(n), 0))
    l_spec = lambda f: pl.BlockSpec((blk, A_HEADS), lambda n: (f(n), 0))
    same = lambda n: n
    return pl.pallas_call(
        body, name=name,
        out_shape=(jax.ShapeDtypeStruct((t, A_COLS), F32), jax.ShapeDtypeStruct((1, A_HEADS), F32)), grid=(nb,),
        in_specs=[q_spec(same), q_spec(nxt),
                  pl.BlockSpec((blk, 2 * A_KV_COLS), lambda n: (n, kv_block)),
                  pl.BlockSpec((blk, 2 * A_KV_COLS), lambda n: (jnp.maximum(n - 1, 0), kv_block)),
                  q_spec(same), q_spec(nxt), q_spec(same), q_spec(nxt), l_spec(same), l_spec(nxt),
                  _const_spec((1, A_HEADS))],
        out_specs=(pl.BlockSpec((blk, A_COLS), lambda n: (n, 0)), _const_spec((1, A_HEADS))),
        compiler_params=_params(("arbitrary",)),
    )(qkv_r, qkv_r, qkv_r, qkv_r, do, do, o, o, lse, lse, sinks)


C_DOWN_COLS = C_Q_RANK + C_KV_RANK + C_ROPE
C_Q_COLS = C_HEADS * C_QK
C_KV_COLS = C_HEADS * (C_NOPE + C_V)
C_O_COLS = C_HEADS * C_V
C_PAD = LANES
C_SCALE = C_QK ** -0.5
LOG2E = 1.4426950408889634
LN2 = 0.6931471805599453
C_Q_SCALE = C_SCALE * LOG2E
C_PAIR = 2


def _mla_latent_fwd(down, q_a_norm, kv_a_norm, name):
    t = down.shape[0]
    tm = _div_tile(t, 512, 16)

    def body(x_ref, gq_ref, gk_ref, cq_ref, ckv_ref):
        cq, ckv = x_ref[:, :C_Q_RANK], x_ref[:, C_Q_RANK:C_Q_RANK + C_KV_RANK]
        cq_ref[...] = (cq * _rstd(cq) * gq_ref[...]).astype(BF16)
        ckv_ref[...] = (ckv * _rstd(ckv) * gk_ref[...]).astype(BF16)

    return pl.pallas_call(
        body, name=name,
        out_shape=(jax.ShapeDtypeStruct((t, C_Q_RANK), BF16), jax.ShapeDtypeStruct((t, C_KV_RANK), BF16)), grid=(t // tm,),
        in_specs=[_row_spec(tm, C_DOWN_COLS), _const_spec((1, C_Q_RANK)), _const_spec((1, C_KV_RANK))],
        out_specs=(_row_spec(tm, C_Q_RANK), _row_spec(tm, C_KV_RANK)), compiler_params=_params(("parallel",)),
    )(down, q_a_norm, kv_a_norm)


def _mla_latent_bwd(down, dcq, dckv, dkrope, q_a_norm, kv_a_norm, name):
    t = down.shape[0]
    tm = _div_tile(t, 512, 16)

    def body(x_ref, dcq_ref, dckv_ref, dkr_ref, gq_ref, gk_ref, o_ref, dgq_ref, dgk_ref):
        dq, dgq = _norm_bwd(x_ref[:, :C_Q_RANK], gq_ref[...], dcq_ref[...])
        dkv, dgk = _norm_bwd(x_ref[:, C_Q_RANK:C_Q_RANK + C_KV_RANK], gk_ref[...], dckv_ref[...])
        o_ref[...] = jnp.concatenate([dq, dkv, dkr_ref[...]], axis=1).astype(BF16)
        _accumulate(dgq_ref, dgq, pl.program_id(0))
        _accumulate(dgk_ref, dgk, pl.program_id(0))

    return pl.pallas_call(
        body, name=name,
        out_shape=(jax.ShapeDtypeStruct((t, C_DOWN_COLS), BF16), jax.ShapeDtypeStruct((1, C_Q_RANK), F32),
                   jax.ShapeDtypeStruct((1, C_KV_RANK), F32)),
        grid=(t // tm,),
        in_specs=[_row_spec(tm, C_DOWN_COLS), _row_spec(tm, C_Q_RANK), _row_spec(tm, C_KV_RANK), _row_spec(tm, C_ROPE),
                  _const_spec((1, C_Q_RANK)), _const_spec((1, C_KV_RANK))],
        out_specs=(_row_spec(tm, C_DOWN_COLS), _const_spec((1, C_Q_RANK)), _const_spec((1, C_KV_RANK))),
        compiler_params=_params(("arbitrary",)),
    )(down, dcq, dckv, dkrope, q_a_norm, kv_a_norm)


def _head_major_spec(tm, width):
    return pl.BlockSpec((C_HEADS, tm, width), lambda i: (0, i, 0))


def _mla_qk_fwd(qw, kvw, down, q_norm, k_norm, cos, sin, name):
    t = qw.shape[0]
    tm = _div_tile(t, 256, 16)
    kvd = C_NOPE + C_V

    def body(q_ref, kv_ref, dn_ref, gq_ref, gk_ref, cos_ref, sin_ref, qo_ref, ko_ref, vo_ref):
        cosv, sinv = cos_ref[...], sin_ref[...]
        k_rope = dn_ref[:, C_Q_RANK + C_KV_RANK:]
        pad = jnp.zeros((tm, C_PAD - C_QK), F32)
        one_then_zeros = (lax.broadcasted_iota(jnp.int32, (tm, C_PAD - C_V), 1) == 0).astype(F32)
        for h in range(C_HEADS):
            qh = _head_fwd(q_ref[:, h * C_QK:(h + 1) * C_QK], gq_ref[...], cosv, sinv, C_NOPE, C_ROPE)
            kx = jnp.concatenate([kv_ref[:, h * kvd:h * kvd + C_NOPE], k_rope], axis=1)
            kh = _head_fwd(kx, gk_ref[...], cosv, sinv, C_NOPE, C_ROPE)
            qo_ref[h] = jnp.concatenate([qh * C_Q_SCALE, pad], axis=1).astype(BF16)
            ko_ref[h] = jnp.concatenate([kh, pad], axis=1).astype(BF16)
            vo_ref[h] = jnp.concatenate([kv_ref[:, h * kvd + C_NOPE:(h + 1) * kvd], one_then_zeros], axis=1).astype(BF16)

    return pl.pallas_call(
        body, name=name,
        out_shape=(jax.ShapeDtypeStruct((C_HEADS, t, C_PAD), BF16), jax.ShapeDtypeStruct((C_HEADS, t, C_PAD), BF16),
                   jax.ShapeDtypeStruct((C_HEADS, t, C_PAD), BF16)),
        grid=(t // tm,),
        in_specs=[_row_spec(tm, C_Q_COLS), _row_spec(tm, C_KV_COLS), _row_spec(tm, C_DOWN_COLS), _const_spec((1, C_QK)),
                  _const_spec((1, C_QK)), _row_spec(tm, C_QK), _row_spec(tm, C_QK)],
        out_specs=(_head_major_spec(tm, C_PAD), _head_major_spec(tm, C_PAD), _head_major_spec(tm, C_PAD)),
        compiler_params=_params(("parallel",)),
    )(qw, kvw, down, q_norm, k_norm, cos, sin)


def _mla_qk_bwd(qw, kvw, down, dq, dk, dv, q_norm, k_norm, cos, sin, name):
    t = qw.shape[0]
    tm = _div_tile(t, 256, 16)
    kvd = C_NOPE + C_V

    def body(q_ref, kv_ref, dn_ref, dq_ref, dk_ref, dv_ref, gq_ref, gk_ref, cos_ref, sin_ref,
             dqw_ref, dkvw_ref, dkr_ref, dgq_ref, dgk_ref):
        cosv, sinv = cos_ref[...], sin_ref[...]
        k_rope = dn_ref[:, C_Q_RANK + C_KV_RANK:]
        dgq = jnp.zeros((1, C_QK), F32)
        dgk = jnp.zeros((1, C_QK), F32)
        dkr = jnp.zeros((tm, C_ROPE), F32)
        for h in range(C_HEADS):
            dxq, dg = _head_bwd(q_ref[:, h * C_QK:(h + 1) * C_QK], gq_ref[...], cosv, sinv, dq_ref[h][:, :C_QK], C_NOPE, C_ROPE)
            dgq = dgq + dg
            dqw_ref[:, h * C_QK:(h + 1) * C_QK] = dxq.astype(BF16)
            kx = jnp.concatenate([kv_ref[:, h * kvd:h * kvd + C_NOPE], k_rope], axis=1)
            dxk, dg = _head_bwd(kx, gk_ref[...], cosv, sinv, dk_ref[h][:, :C_QK], C_NOPE, C_ROPE)
            dgk = dgk + dg
            dkr = dkr + dxk[:, C_NOPE:]
            dkvw_ref[:, h * kvd:(h + 1) * kvd] = jnp.concatenate([dxk[:, :C_NOPE], dv_ref[h]], axis=1).astype(BF16)
        dkr_ref[...] = dkr
        _accumulate(dgq_ref, dgq, pl.program_id(0))
        _accumulate(dgk_ref, dgk, pl.program_id(0))

    return pl.pallas_call(
        body, name=name,
        out_shape=(jax.ShapeDtypeStruct((t, C_Q_COLS), BF16), jax.ShapeDtypeStruct((t, C_KV_COLS), BF16),
                   jax.ShapeDtypeStruct((t, C_ROPE), F32), jax.ShapeDtypeStruct((1, C_QK), F32),
                   jax.ShapeDtypeStruct((1, C_QK), F32)),
        grid=(t // tm,),
        in_specs=[_row_spec(tm, C_Q_COLS), _row_spec(tm, C_KV_COLS), _row_spec(tm, C_DOWN_COLS),
                  _head_major_spec(tm, C_PAD), _head_major_spec(tm, C_PAD), _head_major_spec(tm, C_V),
                  _const_spec((1, C_QK)), _const_spec((1, C_QK)), _row_spec(tm, C_QK), _row_spec(tm, C_QK)],
        out_specs=(_row_spec(tm, C_Q_COLS), _row_spec(tm, C_KV_COLS), _row_spec(tm, C_ROPE), _const_spec((1, C_QK)),
                   _const_spec((1, C_QK))),
        compiler_params=_params(("arbitrary",)),
    )(qw, kvw, down, dq, dk, dv, q_norm, k_norm, cos, sin)


def _causal_keep(rows, cols, row_offset=0, transposed=False):
    row = lax.broadcasted_iota(jnp.int32, (rows, cols), 0) + row_offset
    col = lax.broadcasted_iota(jnp.int32, (rows, cols), 1)
    return (row <= col) if transposed else (col <= row)


def _mla_fwd(q, k, v, name):
    _, t, _ = q.shape
    blk = min(MLA_FWD_BLOCK, t)
    nq = t // blk

    def body(q_ref, k_ref, v_ref, o_ref, lse_ref, m_sc, acc_sc):
        qi = pl.program_id(1)
        m_sc[...] = jnp.full_like(m_sc, NEG)
        acc_sc[...] = jnp.zeros_like(acc_sc)

        def step(ki, masked):
            rows = pl.ds(pl.multiple_of(ki * blk, blk), blk)
            for hh in range(C_PAIR):
                s = lax.dot_general(q_ref[hh], k_ref[hh, rows, :], (((1,), (1,)), ((), ())), preferred_element_type=F32)
                if masked:
                    s = jnp.where(_causal_keep(blk, blk), s, NEG)
                m_prev = m_sc[hh]
                m_new = jnp.maximum(m_prev, jnp.max(s, axis=-1, keepdims=True))
                p = jnp.exp2(s - m_new)
                acc_sc[hh] = jnp.exp2(m_prev - m_new) * acc_sc[hh] + jnp.dot(p.astype(BF16), v_ref[hh, rows, :],
                                                                                preferred_element_type=F32)
                m_sc[hh] = m_new

        def below_diagonal(ki, carry):
            step(ki, False)
            return carry

        lax.fori_loop(0, qi, below_diagonal, 0)
        step(qi, True)
        outs = []
        for hh in range(C_PAIR):
            denom = acc_sc[hh, :, C_V:C_V + 1]
            outs.append(acc_sc[hh, :, :C_V] / denom)
            lse_ref[hh] = m_sc[hh] + jnp.log(denom) * LOG2E
        o_ref[...] = jnp.concatenate(outs, axis=1).astype(BF16)

    whole = lambda hp, qi: (hp, 0, 0)
    return pl.pallas_call(
        body, name=name,
        out_shape=(jax.ShapeDtypeStruct((t, C_O_COLS), BF16), jax.ShapeDtypeStruct((C_HEADS, t, 1), F32)),
        grid=(C_HEADS // C_PAIR, nq),
        in_specs=[pl.BlockSpec((C_PAIR, blk, C_PAD), lambda hp, qi: (hp, qi, 0)),
                  pl.BlockSpec((C_PAIR, t, C_PAD), whole), pl.BlockSpec((C_PAIR, t, C_PAD), whole)],
        out_specs=(pl.BlockSpec((blk, C_PAIR * C_V), lambda hp, qi: (qi, hp)),
                   pl.BlockSpec((C_PAIR, blk, 1), lambda hp, qi: (hp, qi, 0))),
        scratch_shapes=[pltpu.VMEM((C_PAIR, blk, 1), F32), pltpu.VMEM((C_PAIR, blk, C_PAD), F32)],
        compiler_params=_params(("parallel", "arbitrary")),
    )(q, k, v)


def _mla_bwd_dq(q, k, v, do, o, lse, name):
    _, t, _ = q.shape
    blk = min(MLA_BLOCK, t)
    nq = t // blk

    def body(q_ref, k_ref, v_ref, do_ref, o_ref, lse_ref, dq_ref, dlt_ref, acc_sc, dob_sc):
        qi = pl.program_id(1)
        for hh in range(C_PAIR):
            do_h = do_ref[:, hh * C_V:(hh + 1) * C_V]
            dlt_ref[hh] = jnp.sum(do_h * o_ref[:, hh * C_V:(hh + 1) * C_V].astype(F32), axis=-1, keepdims=True)
            dob_sc[hh] = do_h.astype(BF16)
        acc_sc[...] = jnp.zeros_like(acc_sc)

        def step(ki, masked):
            rows = pl.ds(pl.multiple_of(ki * blk, blk), blk)
            for hh in range(C_PAIR):
                kb = k_ref[hh, rows, :]
                s = lax.dot_general(q_ref[hh], kb, (((1,), (1,)), ((), ())), preferred_element_type=F32)
                if masked:
                    s = jnp.where(_causal_keep(blk, blk), s, NEG)
                p = jnp.exp2(s - lse_ref[hh])
                dp = lax.dot_general(dob_sc[hh], v_ref[hh, rows, :C_V], (((1,), (1,)), ((), ())),
                                     preferred_element_type=F32)
                ds = p * (dp - dlt_ref[hh])
                acc_sc[hh] += jnp.dot(ds.astype(BF16), kb, preferred_element_type=F32)

        def below_diagonal(ki, carry):
            step(ki, False)
            return carry

        lax.fori_loop(0, qi, below_diagonal, 0)
        step(qi, True)
        dq_ref[...] = acc_sc[...] * C_SCALE

    whole = lambda hp, qi: (hp, 0, 0)
    qmap = lambda hp, qi: (hp, qi, 0)
    wide = lambda hp, qi: (qi, hp)
    return pl.pallas_call(
        body, name=name,
        out_shape=(jax.ShapeDtypeStruct((C_HEADS, t, C_PAD), F32), jax.ShapeDtypeStruct((C_HEADS, t, 1), F32)),
        grid=(C_HEADS // C_PAIR, nq),
        in_specs=[pl.BlockSpec((C_PAIR, blk, C_PAD), qmap), pl.BlockSpec((C_PAIR, t, C_PAD), whole),
                  pl.BlockSpec((C_PAIR, t, C_PAD), whole), pl.BlockSpec((blk, C_PAIR * C_V), wide),
                  pl.BlockSpec((blk, C_PAIR * C_V), wide), pl.BlockSpec((C_PAIR, blk, 1), qmap)],
        out_specs=(pl.BlockSpec((C_PAIR, blk, C_PAD), qmap), pl.BlockSpec((C_PAIR, blk, 1), qmap)),
        scratch_shapes=[pltpu.VMEM((C_PAIR, blk, C_PAD), F32), pltpu.VMEM((C_PAIR, blk, C_V), BF16)],
        compiler_params=_params(("parallel", "arbitrary")),
    )(q, k, v, do, o, lse)


def _mla_bwd_dkv(q, k, v, do_b, lse_rows, dlt_rows, name):
    _, t, _ = q.shape
    blk = min(MLA_BLOCK, t)
    nq = t // blk

    def body(q_ref, k_ref, v_ref, do_ref, lse_ref, dlt_ref, dk_ref, dv_ref, dk_sc, dv_sc):
        ki = pl.program_id(1)
        dk_sc[...] = jnp.zeros_like(dk_sc)
        dv_sc[...] = jnp.zeros_like(dv_sc)

        def step(qi, masked):
            rows = pl.ds(pl.multiple_of(qi * blk, blk), blk)
            for hh in range(C_PAIR):
                qb = q_ref[hh, rows, :]
                dob = do_ref[rows, hh * C_V:(hh + 1) * C_V]
                s = lax.dot_general(k_ref[hh], qb, (((1,), (1,)), ((), ())), preferred_element_type=F32)
                if masked:
                    s = jnp.where(_causal_keep(blk, blk, transposed=True), s, NEG)
                p = jnp.exp2(s - lse_ref[hh, qi])
                dp = lax.dot_general(v_ref[hh, :, :C_V], dob, (((1,), (1,)), ((), ())), preferred_element_type=F32)
                ds = p * (dp - dlt_ref[hh, qi])
                dv_sc[hh] += jnp.dot(p.astype(BF16), dob, preferred_element_type=F32)
                dk_sc[hh] += jnp.dot(ds.astype(BF16), qb, preferred_element_type=F32)

        def above_diagonal(qi, carry):
            step(qi, False)
            return carry

        step(ki, True)
        lax.fori_loop(ki + 1, nq, above_diagonal, 0)
        dk_ref[...] = dk_sc[...] * LN2
        dv_ref[...] = dv_sc[...]

    whole = lambda hp, ki: (hp, 0, 0)
    whole4 = lambda hp, ki: (hp, 0, 0, 0)
    kmap = lambda hp, ki: (hp, ki, 0)
    return pl.pallas_call(
        body, name=name,
        out_shape=(jax.ShapeDtypeStruct((C_HEADS, t, C_PAD), F32), jax.ShapeDtypeStruct((C_HEADS, t, C_V), F32)),
        grid=(C_HEADS // C_PAIR, nq),
        in_specs=[pl.BlockSpec((C_PAIR, t, C_PAD), whole), pl.BlockSpec((C_PAIR, blk, C_PAD), kmap),
                  pl.BlockSpec((C_PAIR, blk, C_PAD), kmap), pl.BlockSpec((t, C_PAIR * C_V), lambda hp, ki: (0, hp)),
                  pl.BlockSpec((C_PAIR, nq, 1, blk), whole4), pl.BlockSpec((C_PAIR, nq, 1, blk), whole4)],
        out_specs=(pl.BlockSpec((C_PAIR, blk, C_PAD), kmap), pl.BlockSpec((C_PAIR, blk, C_V), kmap)),
        scratch_shapes=[pltpu.VMEM((C_PAIR, blk, C_PAD), F32), pltpu.VMEM((C_PAIR, blk, C_V), F32)],
        compiler_params=_params(("parallel", "arbitrary")),
    )(q, k, v, do_b, lse_rows, dlt_rows)


def _adamw(parts, w, m, v, name):
    layers, rows, cols = w.shape
    tm = _div_tile(rows, 256, 16)

    def body(p_ref, w_ref, m_ref, v_ref, g_ref, d_ref, nm_ref, nv_ref):
        g = p_ref[0].astype(F32)
        for j in range(1, N_DEV):
            g = g + p_ref[j].astype(F32)
        nm = ADAM_B1 * m_ref[...] + (1.0 - ADAM_B1) * g
        nv = ADAM_B2 * v_ref[...] + (1.0 - ADAM_B2) * jnp.square(g)
        m_hat = nm / (1.0 - ADAM_B1 ** ADAM_STEP)
        v_hat = nv / (1.0 - ADAM_B2 ** ADAM_STEP)
        g_ref[...] = g
        d_ref[...] = -ADAM_LR * (m_hat / (jnp.sqrt(v_hat) + ADAM_EPS) + ADAM_WD * w_ref[...])
        nm_ref[...] = nm
        nv_ref[...] = nv

    spec = pl.BlockSpec((None, tm, cols), lambda l, i: (l, i, 0))
    return pl.pallas_call(
        body, name=name, out_shape=tuple(jax.ShapeDtypeStruct(w.shape, F32) for _ in range(4)),
        grid=(layers, rows // tm),
        in_specs=[pl.BlockSpec((None, N_DEV, tm, cols), lambda l, i: (l, 0, i, 0)), spec, spec, spec],
        out_specs=(spec, spec, spec, spec), compiler_params=_params(("parallel", "parallel")),
    )(parts, w, m, v)


def _join_shards(gathered, axis):
    moved = jnp.moveaxis(gathered, 1, axis)
    shape = list(moved.shape)
    shape[axis:axis + 2] = [shape[axis] * shape[axis + 1]]
    return moved.reshape(shape)


def _split_shards(full, axis):
    shape = list(full.shape)
    shape[axis:axis + 1] = [N_DEV, shape[axis] // N_DEV]
    return jnp.moveaxis(full.reshape(shape), axis, 1)


def _as_rows(shape):
    rest = tuple(shape[1:])
    return (shape[0], 1, rest[0]) if len(rest) == 1 else (shape[0],) + rest


def _forward_backward(x, positions, target, w, rep):
    cos_a, sin_a = _rope_tables(positions, A_ROT_DIM, 0, A_HEAD_DIM - A_ROT_DIM)
    cos_c, sin_c = _rope_tables(positions, C_ROPE, C_NOPE, 0)
    saved = []
    for i in range(DEPTH):
        kind, j = i % N_MIXERS, i // N_MIXERS
        s = {'x': x}
        h1 = _rmsnorm_fwd(x, rep['mix_norm'][i:i + 1], f"mix_norm_fwd_{i}")
        s['h1'] = h1
        if kind == 0:
            s['qkv'] = _matmul(h1, w['a_w_qkv'][j], 'nn', f"a_qkv_{i}")
            s['qkv_r'] = _swa_prep_fwd(s['qkv'], rep['a_q_norm'][j:j + 1], rep['a_k_norm'][j:j + 1], cos_a, sin_a,
                                       f"a_prep_fwd_{i}")
            s['o'], s['lse'] = _swa_fwd(s['qkv_r'], rep['a_sinks'][j:j + 1], f"a_attn_fwd_{i}")
            x1 = _matmul(s['o'], w['a_w_o'][j], 'nn', f"a_out_{i}", residual=x)
        elif kind == 1:
            s['bcu'] = _matmul(h1, w['b_w_in'][j], 'nn', f"b_in_{i}")
            s['by'] = _sconv_fwd(s['bcu'], w['b_conv_w'][j], f"b_conv_fwd_{i}")
            x1 = _matmul(s['by'], w['b_w_out'][j], 'nn', f"b_out_{i}", residual=x)
        else:
            s['down'] = _matmul(h1, w['c_w_down'][j], 'nn', f"c_down_{i}")
            s['cq'], s['ckv'] = _mla_latent_fwd(s['down'], w['c_q_a_norm'][j:j + 1], w['c_kv_a_norm'][j:j + 1],
                                                f"c_latent_fwd_{i}")
            s['qw'] = _matmul(s['cq'], w['c_w_q_up'][j], 'nn', f"c_q_up_{i}")
            s['kvw'] = _matmul(s['ckv'], w['c_w_kv_up'][j], 'nn', f"c_kv_up_{i}")
            s['q'], s['k'], s['v'] = _mla_qk_fwd(s['qw'], s['kvw'], s['down'], rep['c_q_norm'][j:j + 1],
                                                 rep['c_k_norm'][j:j + 1], cos_c, sin_c, f"c_prep_fwd_{i}")
            s['o'], s['lse'] = _mla_fwd(s['q'], s['k'], s['v'], f"c_attn_fwd_{i}")
            x1 = _matmul(s['o'], w['c_w_o'][j], 'nn', f"c_out_{i}", residual=x)
        s['x1'] = x1
        s['h2'] = _rmsnorm_fwd(x1, rep['ffn_norm'][i:i + 1], f"ffn_norm_fwd_{i}")
        s['gu'] = _matmul(s['h2'], w['f_w_gate_up'][i], 'nn', f"f_gate_up_{i}", out_dtype=BF16)
        s['act'] = _swiglu_fwd(s['gu'], f"f_act_fwd_{i}")
        x = _matmul(s['act'], w['f_w_down'][i], 'nn', f"f_down_{i}", residual=x1)
        saved.append(s)

    loss, dx = _loss_head(x, target, "loss_head")

    per_layer = {n: {} for n in WEIGHTS}
    for i in reversed(range(DEPTH)):
        kind, j = i % N_MIXERS, i // N_MIXERS
        s = saved[i]
        per_layer['f_w_down'][i] = _matmul(s['act'], dx, 'tn', f"f_down_dw_{i}", out_dtype=BF16)
        dact = _matmul(dx, w['f_w_down'][i], 'nt', f"f_down_dx_{i}", out_dtype=BF16)
        dgu = _swiglu_bwd(s['gu'], dact, f"f_act_bwd_{i}")
        per_layer['f_w_gate_up'][i] = _matmul(s['h2'], dgu, 'tn', f"f_gate_up_dw_{i}", out_dtype=BF16)
        dh2 = _matmul(dgu, w['f_w_gate_up'][i], 'nt', f"f_gate_up_dx_{i}")
        dx1, per_layer['ffn_norm'][i] = _rmsnorm_bwd(s['x1'], rep['ffn_norm'][i:i + 1], dh2, dx, f"ffn_norm_bwd_{i}")
        if kind == 0:
            per_layer['a_w_o'][j] = _matmul(s['o'], dx1, 'tn', f"a_out_dw_{i}", out_dtype=BF16)
            do = _matmul(dx1, w['a_w_o'][j], 'nt', f"a_out_dx_{i}")
            dqkv_r, per_layer['a_sinks'][j] = _swa_bwd(s['qkv_r'], s['o'], s['lse'], do, rep['a_sinks'][j:j + 1],
                                                       f"a_attn_bwd_{i}")
            dqkv, per_layer['a_q_norm'][j], per_layer['a_k_norm'][j] = _swa_prep_bwd(
                s['qkv'], dqkv_r, rep['a_q_norm'][j:j + 1], rep['a_k_norm'][j:j + 1], cos_a, sin_a, f"a_prep_bwd_{i}")
            per_layer['a_w_qkv'][j] = _matmul(s['h1'], dqkv, 'tn', f"a_qkv_dw_{i}", out_dtype=BF16)
            dh1 = _matmul(dqkv, w['a_w_qkv'][j], 'nt', f"a_qkv_dx_{i}")
        elif kind == 1:
            per_layer['b_w_out'][j] = _matmul(s['by'], dx1, 'tn', f"b_out_dw_{i}", out_dtype=BF16)
            dby = _matmul(dx1, w['b_w_out'][j], 'nt', f"b_out_dx_{i}")
            dbcu, per_layer['b_conv_w'][j] = _sconv_bwd(s['bcu'], dby, w['b_conv_w'][j], f"b_conv_bwd_{i}")
            per_layer['b_w_in'][j] = _matmul(s['h1'], dbcu, 'tn', f"b_in_dw_{i}", out_dtype=BF16)
            dh1 = _matmul(dbcu, w['b_w_in'][j], 'nt', f"b_in_dx_{i}")
        else:
            per_layer['c_w_o'][j] = _matmul(s['o'], dx1, 'tn', f"c_out_dw_{i}", out_dtype=BF16)
            do = _matmul(dx1, w['c_w_o'][j], 'nt', f"c_out_dx_{i}")
            dq, dlt = _mla_bwd_dq(s['q'], s['k'], s['v'], do, s['o'], s['lse'], f"c_attn_bwd_dq_{i}")
            blk = min(MLA_BLOCK, do.shape[0])
            as_rows = lambda col: col.reshape(C_HEADS, do.shape[0] // blk, 1, blk)
            dk, dv = _mla_bwd_dkv(s['q'], s['k'], s['v'], do.astype(BF16), as_rows(s['lse']), as_rows(dlt),
                                  f"c_attn_bwd_dkv_{i}")
            dqw, dkvw, dkrope, per_layer['c_q_norm'][j], per_layer['c_k_norm'][j] = _mla_qk_bwd(
                s['qw'], s['kvw'], s['down'], dq, dk, dv, rep['c_q_norm'][j:j + 1], rep['c_k_norm'][j:j + 1], cos_c, sin_c,
                f"c_prep_bwd_{i}")
            per_layer['c_w_q_up'][j] = _matmul(s['cq'], dqw, 'tn', f"c_q_up_dw_{i}", out_dtype=BF16)
            dcq = _matmul(dqw, w['c_w_q_up'][j], 'nt', f"c_q_up_dx_{i}")
            per_layer['c_w_kv_up'][j] = _matmul(s['ckv'], dkvw, 'tn', f"c_kv_up_dw_{i}", out_dtype=BF16)
            dckv = _matmul(dkvw, w['c_w_kv_up'][j], 'nt', f"c_kv_up_dx_{i}")
            ddown, per_layer['c_q_a_norm'][j], per_layer['c_kv_a_norm'][j] = _mla_latent_bwd(
                s['down'], dcq, dckv, dkrope, w['c_q_a_norm'][j:j + 1], w['c_kv_a_norm'][j:j + 1], f"c_latent_bwd_{i}")
            per_layer['c_w_down'][j] = _matmul(s['h1'], ddown, 'tn', f"c_down_dw_{i}", out_dtype=BF16)
            dh1 = _matmul(ddown, w['c_w_down'][j], 'nt', f"c_down_dx_{i}")
        dx, per_layer['mix_norm'][i] = _rmsnorm_bwd(s['x'], rep['mix_norm'][i:i + 1], dh1, dx1, f"mix_norm_bwd_{i}")

    grads = {}
    for n in WEIGHTS:
        stacked = jnp.stack([per_layer[n][j] for j in sorted(per_layer[n])])
        if n in ('mix_norm', 'ffn_norm', 'a_q_norm', 'a_k_norm', 'a_sinks', 'c_q_a_norm', 'c_kv_a_norm', 'c_q_norm', 'c_k_norm'):
            stacked = stacked.reshape(stacked.shape[0], stacked.shape[-1])
        if n in SHARD_AXIS:
            stacked = _split_shards(stacked, SHARD_AXIS[n])
        grads[n] = stacked
    return loss, dx, grads


def kernel(x, positions, mix_norm, ffn_norm, a_w_qkv, a_q_norm, a_k_norm, a_sinks, a_w_o, b_w_in, b_conv_w, b_w_out, c_w_down, c_q_a_norm, c_kv_a_norm, c_w_q_up, c_w_kv_up, c_q_norm, c_k_norm, c_w_o, f_w_gate_up, f_w_down, loss_target, m_mix_norm, m_ffn_norm, m_a_w_qkv, m_a_q_norm, m_a_k_norm, m_a_sinks, m_a_w_o, m_b_w_in, m_b_conv_w, m_b_w_out, m_c_w_down, m_c_q_a_norm, m_c_kv_a_norm, m_c_w_q_up, m_c_w_kv_up, m_c_q_norm, m_c_k_norm, m_c_w_o, m_f_w_gate_up, m_f_w_down, v_mix_norm, v_ffn_norm, v_a_w_qkv, v_a_q_norm, v_a_k_norm, v_a_sinks, v_a_w_o, v_b_w_in, v_b_conv_w, v_b_w_out, v_c_w_down, v_c_q_a_norm, v_c_kv_a_norm, v_c_w_q_up, v_c_w_kv_up, v_c_q_norm, v_c_k_norm, v_c_w_o, v_f_w_gate_up, v_f_w_down):
    local = dict(mix_norm=mix_norm, ffn_norm=ffn_norm, a_w_qkv=a_w_qkv, a_q_norm=a_q_norm, a_k_norm=a_k_norm, a_sinks=a_sinks, a_w_o=a_w_o, b_w_in=b_w_in, b_conv_w=b_conv_w, b_w_out=b_w_out, c_w_down=c_w_down, c_q_a_norm=c_q_a_norm, c_kv_a_norm=c_kv_a_norm, c_w_q_up=c_w_q_up, c_w_kv_up=c_w_kv_up, c_q_norm=c_q_norm, c_k_norm=c_k_norm, c_w_o=c_w_o, f_w_gate_up=f_w_gate_up, f_w_down=f_w_down)
    mom1 = dict(mix_norm=m_mix_norm, ffn_norm=m_ffn_norm, a_w_qkv=m_a_w_qkv, a_q_norm=m_a_q_norm, a_k_norm=m_a_k_norm, a_sinks=m_a_sinks, a_w_o=m_a_w_o, b_w_in=m_b_w_in, b_conv_w=m_b_conv_w, b_w_out=m_b_w_out, c_w_down=m_c_w_down, c_q_a_norm=m_c_q_a_norm, c_kv_a_norm=m_c_kv_a_norm, c_w_q_up=m_c_w_q_up, c_w_kv_up=m_c_w_kv_up, c_q_norm=m_c_q_norm, c_k_norm=m_c_k_norm, c_w_o=m_c_w_o, f_w_gate_up=m_f_w_gate_up, f_w_down=m_f_w_down)
    mom2 = dict(mix_norm=v_mix_norm, ffn_norm=v_ffn_norm, a_w_qkv=v_a_w_qkv, a_q_norm=v_a_q_norm, a_k_norm=v_a_k_norm, a_sinks=v_a_sinks, a_w_o=v_a_w_o, b_w_in=v_b_w_in, b_conv_w=v_b_conv_w, b_w_out=v_b_w_out, c_w_down=v_c_w_down, c_q_a_norm=v_c_q_a_norm, c_kv_a_norm=v_c_kv_a_norm, c_w_q_up=v_c_w_q_up, c_w_kv_up=v_c_w_kv_up, c_q_norm=v_c_q_norm, c_k_norm=v_c_k_norm, c_w_o=v_c_w_o, f_w_gate_up=v_f_w_gate_up, f_w_down=v_f_w_down)
    t, d = x.shape[1], x.shape[2]

    gathered = _exchange([local[n].astype(BF16) if n in GATHER_BF16 else local[n] for n in SHARDED], False,
                         "gather_weights")
    full = {n: _join_shards(g, SHARD_AXIS[n]) for n, g in zip(SHARDED, gathered)}
    rep = {n: local[n] for n in REPLICATED}

    loss, grad_x, grads = _forward_backward(x.reshape(t, d), positions.reshape(t), loss_target.reshape(t, d), full, rep)

    out_g, out_d, out_m, out_v = {}, {}, {}, {}

    def update(names, parts):
        for n, part in zip(names, parts):
            shape = local[n].shape if n in SHARD_AXIS else (1,) + local[n].shape
            view = _as_rows(shape)
            results = _adamw(part.reshape(view[0], N_DEV, view[1], view[2]),
                             *[src[n].reshape(view) for src in (local, mom1, mom2)], name="adamw_" + n)
            for dst, res in zip((out_g, out_d, out_m, out_v), results):
                dst[n] = res.reshape(local[n].shape)

    update(SHARDED, _exchange([grads[n] for n in SHARDED], True, "scatter_gradients"))
    update(REPLICATED, _exchange([grads[n].reshape((1,) + grads[n].shape) for n in REPLICATED], False,
                                 "gather_small_gradients"))

    loss = lax.psum(loss.reshape(()), MESH_AXES)
    outs = [loss, grad_x.reshape(1, t, d)]
    for res in (out_g, out_d, out_m, out_v):
        outs += [res[n] for n in WEIGHTS]
    return tuple(outs)
```

```python
import functools

import jax
import jax.numpy as jnp
from jax import lax
from jax.experimental import pallas as pl
from jax.experimental.pallas import tpu as pltpu

F32 = jnp.float32
BF16 = jnp.bfloat16

N_DEV = 8
MESH_AXES = ("x", "y", "c")

DEPTH = 4
N_MIXERS = 3
ROPE_THETA = 500000.0
EPS = 1e-6
A_HEADS, A_KV_HEADS, A_HEAD_DIM, A_ROT_DIM, A_WINDOW = 16, 4, 64, 16, 128
A_GROUP = A_HEADS // A_KV_HEADS
C_HEADS, C_NOPE, C_ROPE, C_V, C_Q_RANK, C_KV_RANK = 16, 64, 32, 64, 384, 256
C_QK = C_NOPE + C_ROPE
ADAM_LR, ADAM_B1, ADAM_B2, ADAM_EPS, ADAM_WD, ADAM_STEP = 0.001, 0.9, 0.999, 1e-08, 0.01, 10

VMEM_LIMIT_BYTES = 48 * 1024 * 1024
LANES = 128
NEG = -1e30
MLA_BLOCK = 512
MLA_FWD_BLOCK = 1024

WEIGHTS = ['mix_norm', 'ffn_norm', 'a_w_qkv', 'a_q_norm', 'a_k_norm', 'a_sinks', 'a_w_o', 'b_w_in', 'b_conv_w', 'b_w_out',
           'c_w_down', 'c_q_a_norm', 'c_kv_a_norm', 'c_w_q_up', 'c_w_kv_up', 'c_q_norm', 'c_k_norm', 'c_w_o', 'f_w_gate_up',
           'f_w_down']
SHARD_AXIS = {'a_w_qkv': 2, 'a_w_o': 1, 'b_w_in': 2, 'b_conv_w': 2, 'b_w_out': 1, 'c_w_down': 1, 'c_q_a_norm': 1,
              'c_kv_a_norm': 1, 'c_w_q_up': 2, 'c_w_kv_up': 2, 'c_w_o': 1, 'f_w_gate_up': 2, 'f_w_down': 1}
SHARDED = [n for n in WEIGHTS if n in SHARD_AXIS]
REPLICATED = [n for n in WEIGHTS if n not in SHARD_AXIS]
GATHER_F32 = ['b_conv_w', 'c_q_a_norm', 'c_kv_a_norm']
GATHER_BF16 = [n for n in SHARDED if n not in GATHER_F32]

def _params(semantics=None):
    return pltpu.CompilerParams(dimension_semantics=semantics, vmem_limit_bytes=VMEM_LIMIT_BYTES)


def _div_tile(n, cap, mult=LANES):
    best = None
    t = mult
    while t <= min(n, cap):
        if n % t == 0:
            best = t
        t += mult
    return n if best is None else best


def _exchange_call(body, arrays, out_shapes, name):
    n = len(arrays)
    return pl.pallas_call(
        body, name=name, out_shape=tuple(out_shapes),
        in_specs=[pl.BlockSpec(memory_space=pl.ANY)] * n,
        out_specs=tuple(pl.BlockSpec(memory_space=pl.ANY) for _ in range(n)),
        scratch_shapes=[pltpu.SemaphoreType.DMA((n, N_DEV - 1)), pltpu.SemaphoreType.DMA((n, N_DEV - 1)),
                        pltpu.SemaphoreType.DMA((n,))],
    )(*arrays)


def _gather_blocks(arrays, name):
    n = len(arrays)

    def body(*refs):
        src_refs, out_refs = refs[:n], refs[n:2 * n]
        send_sems, recv_sems, local_sems = refs[2 * n:]
        x, y, c = lax.axis_index("x"), lax.axis_index("y"), lax.axis_index("c")
        me, sibling = (x, y, c), (x, y, 1 - c)
        chips = [(1 - x, y), (x, 1 - y), (1 - x, 1 - y)]

        def copy(a, k, block, to, own=False):
            place = out_refs[a].at[:, 4 * block[0] + 2 * block[1] + block[2]]
            return pltpu.make_async_remote_copy(
                src_ref=src_refs[a] if own else place, dst_ref=place, send_sem=send_sems.at[a, k],
                recv_sem=recv_sems.at[a, k], device_id=to, device_id_type=pl.DeviceIdType.MESH)

        local, sent = [], []
        for a in range(n):
            mine = pltpu.make_async_copy(src_refs[a], out_refs[a].at[:, 4 * x + 2 * y + c], local_sems.at[a])
            mine.start()
            local.append(mine)
            first = [copy(a, 0, me, sibling, own=True)]
            first += [copy(a, 1 + j, me, (*chip, c), own=True) for j, chip in enumerate(chips)]
            for cp in first:
                cp.start()
            sent += first
        for a in range(n):
            for j, chip in enumerate(chips):
                copy(a, 1 + j, (*chip, c), me).wait_recv()
                forward = copy(a, 4 + j, (*chip, c), sibling)
                forward.start()
                sent.append(forward)
        for a in range(n):
            copy(a, 0, sibling, me).wait_recv()
            for j, chip in enumerate(chips):
                copy(a, 4 + j, (*chip, 1 - c), me).wait_recv()
        for cp in sent:
            cp.wait_send()
        for cp in local:
            cp.wait()

    shapes = [jax.ShapeDtypeStruct((arr.shape[0], N_DEV) + tuple(arr.shape[1:]), arr.dtype) for arr in arrays]
    return _exchange_call(body, arrays, shapes, name)


def _scatter_blocks(arrays, name):
    n = len(arrays)

    def body(*refs):
        src_refs, out_refs = refs[:n], refs[n:2 * n]
        send_sems, recv_sems, local_sems = refs[2 * n:]
        x, y, c = lax.axis_index("x"), lax.axis_index("y"), lax.axis_index("c")
        me = 4 * x + 2 * y + c
        copies = []
        for a in range(n):
            local = pltpu.make_async_copy(src_refs[a].at[:, me], out_refs[a].at[:, me], local_sems.at[a])
            local.start()
            copies.append(local)
            for r in range(1, N_DEV):
                px = 1 - x if (r >> 2) & 1 else x
                py = 1 - y if (r >> 1) & 1 else y
                pc = 1 - c if r & 1 else c
                cp = pltpu.make_async_remote_copy(
                    src_ref=src_refs[a].at[:, 4 * px + 2 * py + pc], dst_ref=out_refs[a].at[:, me],
                    send_sem=send_sems.at[a, r - 1], recv_sem=recv_sems.at[a, r - 1],
                    device_id=(px, py, pc), device_id_type=pl.DeviceIdType.MESH)
                cp.start()
                copies.append(cp)
        for cp in copies:
            cp.wait()

    shapes = [jax.ShapeDtypeStruct(arr.shape, arr.dtype) for arr in arrays]
    return _exchange_call(body, arrays, shapes, name)


def _matmul(a, b, mode, name, out_dtype=F32, residual=None):
    if mode == 'nn':
        (m, k), (k2, n) = a.shape, b.shape
    elif mode == 'nt':
        (m, k), (n, k2) = a.shape, b.shape
    else:
        (k, m), (k2, n) = a.shape, b.shape
    assert k == k2, (name, a.shape, b.shape, mode)
    if mode == 'tn':
        tm, tk = _div_tile(m, 1408), _div_tile(k, 512, 16)
    else:
        tk = _div_tile(k, 1536)
        tm = _div_tile(m, 1024 if tk == k else 512, 16)
    tn = _div_tile(n, 1408)
    nk = k // tk
    dims = {'nn': (((1,), (0,)), ((), ())), 'nt': (((1,), (1,)), ((), ())), 'tn': (((0,), (0,)), ((), ()))}[mode]

    def product(a_ref, b_ref):
        return lax.dot_general(a_ref[...].astype(BF16), b_ref[...].astype(BF16), dims, preferred_element_type=F32)

    def finish(r, rest):
        if residual is not None:
            r = r + rest[0][...]
        rest[-1 if nk == 1 else -2][...] = r.astype(out_dtype)

    def body_single(a_ref, b_ref, *rest):
        finish(product(a_ref, b_ref), rest)

    def body_accumulate(a_ref, b_ref, *rest):
        acc = rest[-1]
        kk = pl.program_id(2)

        @pl.when(kk == 0)
        def _():
            acc[...] = jnp.zeros_like(acc)

        acc[...] += product(a_ref, b_ref)

        @pl.when(kk == nk - 1)
        def _():
            finish(acc[...], rest)

    a_spec = pl.BlockSpec((tk, tm), lambda i, j, kk: (kk, i)) if mode == 'tn' else pl.BlockSpec((tm, tk), lambda i, j, kk: (i, kk))
    b_spec = pl.BlockSpec((tn, tk), lambda i, j, kk: (j, kk)) if mode == 'nt' else pl.BlockSpec((tk, tn), lambda i, j, kk: (kk, j))
    o_spec = pl.BlockSpec((tm, tn), lambda i, j, kk: (i, j))
    in_specs, operands = [a_spec, b_spec], [a, b]
    if residual is not None:
        in_specs.append(o_spec)
        operands.append(residual)
    return pl.pallas_call(
        body_single if nk == 1 else body_accumulate, name=name, out_shape=jax.ShapeDtypeStruct((m, n), out_dtype),
        grid=(m // tm, n // tn, nk), in_specs=in_specs, out_specs=o_spec,
        scratch_shapes=[] if nk == 1 else [pltpu.VMEM((tm, tn), F32)],
        compiler_params=_params(("parallel", "parallel", "arbitrary")),
    )(*operands)


def _row_spec(tm, cols):
    return pl.BlockSpec((tm, cols), lambda i: (i, 0))


def _const_spec(shape):
    return pl.BlockSpec(shape, lambda i: tuple(0 for _ in shape))


def _accumulate(ref, value, step):
    @pl.when(step == 0)
    def _():
        ref[...] = value

    @pl.when(step > 0)
    def _():
        ref[...] += value


def _rstd(x):
    return lax.rsqrt(jnp.mean(x * x, axis=-1, keepdims=True) + EPS)


def _norm_bwd(x, g, dout):
    xn = x * _rstd(x)
    dg = jnp.sum(dout * xn, axis=0, keepdims=True)
    dxn = dout * g
    dx = _rstd(x) * (dxn - xn * jnp.mean(dxn * xn, axis=-1, keepdims=True))
    return dx, dg


def _rmsnorm_fwd(x, g, name):
    t, d = x.shape
    tm = _div_tile(t, 512, 16)

    def body(x_ref, g_ref, o_ref):
        xv = x_ref[...]
        o_ref[...] = (xv * _rstd(xv) * g_ref[...]).astype(BF16)

    return pl.pallas_call(
        body, name=name, out_shape=jax.ShapeDtypeStruct((t, d), BF16), grid=(t // tm,),
        in_specs=[_row_spec(tm, d), _const_spec((1, d))], out_specs=_row_spec(tm, d),
        compiler_params=_params(("parallel",)),
    )(x, g)


def _rmsnorm_bwd(x, g, dh, dres, name):
    t, d = x.shape
    tm = _div_tile(t, 512, 8)

    def body(x_ref, g_ref, dh_ref, dres_ref, dx_ref, dg_ref):
        dx, dg = _norm_bwd(x_ref[...], g_ref[...], dh_ref[...])
        dx_ref[...] = dres_ref[...] + dx
        _accumulate(dg_ref, dg, pl.program_id(0))

    return pl.pallas_call(
        body, name=name,
        out_shape=(jax.ShapeDtypeStruct((t, d), F32), jax.ShapeDtypeStruct((1, d), F32)), grid=(t // tm,),
        in_specs=[_row_spec(tm, d), _const_spec((1, d)), _row_spec(tm, d), _row_spec(tm, d)],
        out_specs=(_row_spec(tm, d), _const_spec((1, d))),
        compiler_params=_params(("arbitrary",)),
    )(x, g, dh, dres)


def _sigmoid(x):
    return 1.0 / (1.0 + jnp.exp(-x))


def _swiglu_fwd(gu, name):
    t, f2 = gu.shape
    f = f2 // 2
    tm = _div_tile(t, 512, 16)

    def body(gu_ref, o_ref):
        gate, up = gu_ref[:, :f].astype(F32), gu_ref[:, f:].astype(F32)
        o_ref[...] = (gate * _sigmoid(gate) * up).astype(BF16)

    return pl.pallas_call(
        body, name=name, out_shape=jax.ShapeDtypeStruct((t, f), BF16), grid=(t // tm,),
        in_specs=[_row_spec(tm, f2)], out_specs=_row_spec(tm, f),
        compiler_params=_params(("parallel",)),
    )(gu)


def _swiglu_bwd(gu, da, name):
    t, f2 = gu.shape
    f = f2 // 2
    tm = _div_tile(t, 512, 16)

    def body(gu_ref, da_ref, o_ref):
        gate, up, dav = gu_ref[:, :f].astype(F32), gu_ref[:, f:].astype(F32), da_ref[...].astype(F32)
        sig = _sigmoid(gate)
        o_ref[:, :f] = (dav * up * (sig * (1.0 + gate * (1.0 - sig)))).astype(BF16)
        o_ref[:, f:] = (dav * (gate * sig)).astype(BF16)

    return pl.pallas_call(
        body, name=name, out_shape=jax.ShapeDtypeStruct((t, f2), BF16), grid=(t // tm,),
        in_specs=[_row_spec(tm, f2), _row_spec(tm, f)], out_specs=_row_spec(tm, f2),
        compiler_params=_params(("parallel",)),
    )(gu, da)


def _loss_head(y, target, name):
    t, d = y.shape
    tm = _div_tile(t, 512, 8)

    def body(y_ref, t_ref, loss_ref, dy_ref):
        diff = y_ref[...] - t_ref[...]
        dy_ref[...] = diff * (1.0 / d)
        part = jnp.sum(jnp.sum(diff * diff, axis=1, keepdims=True), axis=0, keepdims=True) * (0.5 / d)
        _accumulate(loss_ref, part, pl.program_id(0))

    return pl.pallas_call(
        body, name=name,
        out_shape=(jax.ShapeDtypeStruct((1, 1), F32), jax.ShapeDtypeStruct((t, d), F32)), grid=(t // tm,),
        in_specs=[_row_spec(tm, d), _row_spec(tm, d)], out_specs=(_const_spec((1, 1)), _row_spec(tm, d)),
        compiler_params=_params(("arbitrary",)),
    )(y, target)


HALO = 8


def _shift_down(z, k, halo_rows):
    tm = z.shape[0]
    row = lax.broadcasted_iota(jnp.int32, z.shape, 0)
    out = pltpu.roll(z, k, 0)
    for j in range(k):
        out = jnp.where(row == j, halo_rows[HALO - k + j:HALO - k + j + 1, :], out)
    return out


def _shift_up(z, k, halo_rows):
    tm = z.shape[0]
    row = lax.broadcasted_iota(jnp.int32, z.shape, 0)
    out = pltpu.roll(z, tm - k, 0)
    for j in range(k):
        out = jnp.where(row == tm - k + j, halo_rows[j:j + 1, :], out)
    return out


def _sconv_specs(t, tm, cols):
    per = tm // HALO
    last = t // HALO - 1
    cur = pl.BlockSpec((tm, cols), lambda i: (i, 0))
    prev = pl.BlockSpec((HALO, cols), lambda i: (jnp.maximum(i * per - 1, 0), 0))
    nxt = pl.BlockSpec((HALO, cols), lambda i: (jnp.minimum((i + 1) * per, last), 0))
    return cur, prev, nxt


def _sconv_fwd(bcu, conv_w, name):
    t, d3 = bcu.shape
    d = d3 // 3
    tm = _div_tile(t, 256, 16)
    cur, prev, _ = _sconv_specs(t, tm, d3)

    def body(cur_ref, prev_ref, w_ref, o_ref):
        i = pl.program_id(0)
        z = cur_ref[:, d:2 * d] * cur_ref[:, 2 * d:]
        zp = prev_ref[:, d:2 * d] * prev_ref[:, 2 * d:] * (i > 0).astype(F32)
        y = w_ref[0:1, :] * _shift_down(z, 2, zp) + w_ref[1:2, :] * _shift_down(z, 1, zp) + w_ref[2:3, :] * z
        o_ref[...] = (cur_ref[:, :d] * y).astype(BF16)

    return pl.pallas_call(
        body, name=name, out_shape=jax.ShapeDtypeStruct((t, d), BF16), grid=(t // tm,),
        in_specs=[cur, prev, _const_spec((3, d))], out_specs=_row_spec(tm, d),
        compiler_params=_params(("parallel",)),
    )(bcu, bcu, conv_w)


def _sconv_bwd(bcu, dout, conv_w, name):
    t, d3 = bcu.shape
    d = d3 // 3
    tm = _div_tile(t, 256, 16)
    cur, prev, nxt = _sconv_specs(t, tm, d3)
    dcur, _, dnxt = _sconv_specs(t, tm, d)
    n_tiles = t // tm

    def body(cur_ref, prev_ref, nxt_ref, do_ref, don_ref, w_ref, o_ref, dw_ref):
        i = pl.program_id(0)
        b, cg, u = cur_ref[:, :d], cur_ref[:, d:2 * d], cur_ref[:, 2 * d:]
        z = cg * u
        zp = prev_ref[:, d:2 * d] * prev_ref[:, 2 * d:] * (i > 0).astype(F32)
        z1, z2 = _shift_down(z, 1, zp), _shift_down(z, 2, zp)
        w0, w1, w2 = w_ref[0:1, :], w_ref[1:2, :], w_ref[2:3, :]
        y = w0 * z2 + w1 * z1 + w2 * z
        dov = do_ref[...]
        dy = dov * b
        dyn = don_ref[...] * nxt_ref[:, :d] * (i < n_tiles - 1).astype(F32)
        dz = w2 * dy + w1 * _shift_up(dy, 1, dyn) + w0 * _shift_up(dy, 2, dyn)
        o_ref[:, :d] = (dov * y).astype(BF16)
        o_ref[:, d:2 * d] = (dz * u).astype(BF16)
        o_ref[:, 2 * d:] = (dz * cg).astype(BF16)
        dw = jnp.concatenate([jnp.sum(dy * z2, axis=0, keepdims=True), jnp.sum(dy * z1, axis=0, keepdims=True),
                              jnp.sum(dy * z, axis=0, keepdims=True)], axis=0)
        _accumulate(dw_ref, dw, i)

    return pl.pallas_call(
        body, name=name,
        out_shape=(jax.ShapeDtypeStruct((t, d3), BF16), jax.ShapeDtypeStruct((3, d), F32)), grid=(n_tiles,),
        in_specs=[cur, prev, nxt, dcur, dnxt, _const_spec((3, d))],
        out_specs=(_row_spec(tm, d3), _const_spec((3, d))),
        compiler_params=_params(("arbitrary",)),
    )(bcu, bcu, bcu, dout, dout, conv_w)


def _rope_tables(positions, rot, lead, trail):
    inv_freq = ROPE_THETA ** (-jnp.arange(0, rot, 2, dtype=F32) / rot)
    ang = positions.astype(F32)[:, None] * inv_freq
    cos, sin = jnp.cos(ang), jnp.sin(ang)
    t = positions.shape[0]
    cos_full = jnp.concatenate([jnp.ones((t, lead), F32), cos, cos, jnp.ones((t, trail), F32)], axis=1)
    sin_full = jnp.concatenate([jnp.zeros((t, lead), F32), -sin, sin, jnp.zeros((t, trail), F32)], axis=1)
    return cos_full, sin_full


def _swap_halves(x, lead, rot):
    half = rot // 2
    rows, d = x.shape
    parts = []
    if lead:
        parts.append(jnp.zeros((rows, lead), x.dtype))
    parts += [x[:, lead + half:lead + rot], x[:, lead:lead + half]]
    if d - lead - rot:
        parts.append(jnp.zeros((rows, d - lead - rot), x.dtype))
    return jnp.concatenate(parts, axis=1)


def _head_fwd(x, g, cos, sin, lead, rot):
    n = x * _rstd(x) * g
    return n * cos + _swap_halves(n, lead, rot) * sin


def _head_bwd(x, g, cos, sin, dout, lead, rot):
    dn = dout * cos + _swap_halves(dout * sin, lead, rot)
    return _norm_bwd(x, g, dn)


A_Q_COLS = A_HEADS * A_HEAD_DIM
A_KV_COLS = A_KV_HEADS * A_HEAD_DIM
A_COLS = A_Q_COLS + 2 * A_KV_COLS
A_SCALE = A_HEAD_DIM ** -0.5


def _swa_prep_fwd(qkv, q_norm, k_norm, cos, sin, name):
    t = qkv.shape[0]
    tm = _div_tile(t, 256, 16)
    hd = A_HEAD_DIM

    def body(x_ref, gq_ref, gk_ref, cos_ref, sin_ref, o_ref):
        cosv, sinv = cos_ref[...], sin_ref[...]
        for h in range(A_HEADS + A_KV_HEADS):
            g = gq_ref[...] if h < A_HEADS else gk_ref[...]
            o_ref[:, h * hd:(h + 1) * hd] = _head_fwd(x_ref[:, h * hd:(h + 1) * hd], g, cosv, sinv, 0, A_ROT_DIM).astype(BF16)
        o_ref[:, A_Q_COLS + A_KV_COLS:] = x_ref[:, A_Q_COLS + A_KV_COLS:].astype(BF16)

    return pl.pallas_call(
        body, name=name, out_shape=jax.ShapeDtypeStruct((t, A_COLS), BF16), grid=(t // tm,),
        in_specs=[_row_spec(tm, A_COLS), _const_spec((1, hd)), _const_spec((1, hd)), _row_spec(tm, hd), _row_spec(tm, hd)],
        out_specs=_row_spec(tm, A_COLS), compiler_params=_params(("parallel",)),
    )(qkv, q_norm, k_norm, cos, sin)


def _swa_prep_bwd(qkv, dqkv_r, q_norm, k_norm, cos, sin, name):
    t = qkv.shape[0]
    tm = _div_tile(t, 256, 16)
    hd = A_HEAD_DIM

    def body(x_ref, d_ref, gq_ref, gk_ref, cos_ref, sin_ref, o_ref, dgq_ref, dgk_ref):
        cosv, sinv = cos_ref[...], sin_ref[...]
        dgq = jnp.zeros((1, hd), F32)
        dgk = jnp.zeros((1, hd), F32)
        for h in range(A_HEADS + A_KV_HEADS):
            sl = slice(h * hd, (h + 1) * hd)
            g = gq_ref[...] if h < A_HEADS else gk_ref[...]
            dx, dg = _head_bwd(x_ref[:, sl], g, cosv, sinv, d_ref[:, sl], 0, A_ROT_DIM)
            o_ref[:, sl] = dx.astype(BF16)
            if h < A_HEADS:
                dgq = dgq + dg
            else:
                dgk = dgk + dg
        o_ref[:, A_Q_COLS + A_KV_COLS:] = d_ref[:, A_Q_COLS + A_KV_COLS:].astype(BF16)
        _accumulate(dgq_ref, dgq, pl.program_id(0))
        _accumulate(dgk_ref, dgk, pl.program_id(0))

    return pl.pallas_call(
        body, name=name,
        out_shape=(jax.ShapeDtypeStruct((t, A_COLS), BF16), jax.ShapeDtypeStruct((1, hd), F32),
                   jax.ShapeDtypeStruct((1, hd), F32)),
        grid=(t // tm,),
        in_specs=[_row_spec(tm, A_COLS), _row_spec(tm, A_COLS), _const_spec((1, hd)), _const_spec((1, hd)),
                  _row_spec(tm, hd), _row_spec(tm, hd)],
        out_specs=(_row_spec(tm, A_COLS), _const_spec((1, hd)), _const_spec((1, hd))),
        compiler_params=_params(("arbitrary",)),
    )(qkv, dqkv_r, q_norm, k_norm, cos, sin)


def _group_rows(ref, k, width=A_HEAD_DIM, base=0):
    return jnp.concatenate([ref[:, base + (A_GROUP * k + g) * width:base + (A_GROUP * k + g + 1) * width]
                            for g in range(A_GROUP)], axis=0)


def _group_column(ref, k, rows):
    cols = []
    for g in range(A_GROUP):
        h = A_GROUP * k + g
        col = ref[:, h:h + 1]
        cols.append(jnp.broadcast_to(col, (rows, 1)) if col.shape[0] == 1 else col)
    return jnp.concatenate(cols, axis=0)


def _swa_fwd(qkv_r, sinks, name):
    t = qkv_r.shape[0]
    blk = A_WINDOW
    nb = t // blk
    hd = A_HEAD_DIM
    kv_block = A_Q_COLS // (2 * A_KV_COLS)

    def body(q_ref, kvc_ref, kvp_ref, s_ref, o_ref, lse_ref):
        n = pl.program_id(0)
        shape = (A_GROUP * blk, 2 * blk)
        qpos = lax.broadcasted_iota(jnp.int32, shape, 0) & (blk - 1)
        col = lax.broadcasted_iota(jnp.int32, shape, 1)
        delta = qpos + blk - col
        valid = (delta >= 0) & (delta < A_WINDOW) & ((col >= blk) | (n > 0))
        for k in range(A_KV_HEADS):
            qg = _group_rows(q_ref, k)
            kw = jnp.concatenate([kvp_ref[:, k * hd:(k + 1) * hd], kvc_ref[:, k * hd:(k + 1) * hd]], axis=0)
            vw = jnp.concatenate([kvp_ref[:, A_KV_COLS + k * hd:A_KV_COLS + (k + 1) * hd],
                                  kvc_ref[:, A_KV_COLS + k * hd:A_KV_COLS + (k + 1) * hd]], axis=0)
            s = lax.dot_general(qg, kw, (((1,), (1,)), ((), ())), preferred_element_type=F32) * A_SCALE
            s = jnp.where(valid, s, NEG)
            sink = _group_column(s_ref, k, blk)
            m = jnp.maximum(jnp.max(s, axis=-1, keepdims=True), sink)
            p = jnp.exp(s - m)
            denom = jnp.sum(p, axis=-1, keepdims=True) + jnp.exp(sink - m)
            p = p / denom
            o = jnp.dot(p.astype(BF16), vw, preferred_element_type=F32)
            lse = m + jnp.log(denom)
            for g in range(A_GROUP):
                h = A_GROUP * k + g
                o_ref[:, h * hd:(h + 1) * hd] = o[g * blk:(g + 1) * blk].astype(BF16)
                lse_ref[:, h:h + 1] = lse[g * blk:(g + 1) * blk]

    return pl.pallas_call(
        body, name=name,
        out_shape=(jax.ShapeDtypeStruct((t, A_Q_COLS), BF16), jax.ShapeDtypeStruct((t, A_HEADS), F32)), grid=(nb,),
        in_specs=[pl.BlockSpec((blk, A_Q_COLS), lambda n: (n, 0)),
                  pl.BlockSpec((blk, 2 * A_KV_COLS), lambda n: (n, kv_block)),
                  pl.BlockSpec((blk, 2 * A_KV_COLS), lambda n: (jnp.maximum(n - 1, 0), kv_block)),
                  _const_spec((1, A_HEADS))],
        out_specs=(pl.BlockSpec((blk, A_Q_COLS), lambda n: (n, 0)), pl.BlockSpec((blk, A_HEADS), lambda n: (n, 0))),
        compiler_params=_params(("parallel",)),
    )(qkv_r, qkv_r, qkv_r, sinks)


def _swa_bwd(qkv_r, o, lse, do, sinks, name):
    t = qkv_r.shape[0]
    blk = A_WINDOW
    nb = t // blk
    hd = A_HEAD_DIM
    kv_block = A_Q_COLS // (2 * A_KV_COLS)
    rows = A_GROUP * blk

    def nxt(n):
        return jnp.minimum(n + 1, nb - 1)

    def body(qc_ref, qn_ref, kvc_ref, kvp_ref, doc_ref, don_ref, oc_ref, on_ref, lc_ref, ln_ref, s_ref, dx_ref, ds_ref):
        n = pl.program_id(0)
        shape = (2 * rows, 2 * blk)
        row = lax.broadcasted_iota(jnp.int32, shape, 0)
        col = lax.broadcasted_iota(jnp.int32, shape, 1)
        is_next = row >= rows
        delta = jnp.where(is_next, blk, 0) + blk + (row & (blk - 1)) - col
        valid = ((delta >= 0) & (delta < A_WINDOW) & ((col >= blk) | (n > 0)) & (jnp.logical_not(is_next) | (n < nb - 1)))
        dsink_cols = []
        for k in range(A_KV_HEADS):
            qs = jnp.concatenate([_group_rows(qc_ref, k), _group_rows(qn_ref, k)], axis=0)
            dos = jnp.concatenate([_group_rows(doc_ref, k), _group_rows(don_ref, k)], axis=0)
            os_ = jnp.concatenate([_group_rows(oc_ref, k), _group_rows(on_ref, k)], axis=0).astype(F32)
            lses = jnp.concatenate([_group_column(lc_ref, k, blk), _group_column(ln_ref, k, blk)], axis=0)
            kw = jnp.concatenate([kvp_ref[:, k * hd:(k + 1) * hd], kvc_ref[:, k * hd:(k + 1) * hd]], axis=0)
            vw = jnp.concatenate([kvp_ref[:, A_KV_COLS + k * hd:A_KV_COLS + (k + 1) * hd],
                                  kvc_ref[:, A_KV_COLS + k * hd:A_KV_COLS + (k + 1) * hd]], axis=0)
            s = lax.dot_general(qs, kw, (((1,), (1,)), ((), ())), preferred_element_type=F32) * A_SCALE
            p = jnp.exp(jnp.where(valid, s - lses, NEG))
            dos_b = dos.astype(BF16)
            dp = lax.dot_general(dos_b, vw, (((1,), (1,)), ((), ())), preferred_element_type=F32)
            dlt = jnp.sum(dos * os_, axis=-1, keepdims=True)
            ds = p * (dp - dlt)
            dq = jnp.dot(ds[:rows].astype(BF16), kw, preferred_element_type=F32) * A_SCALE
            dk = lax.dot_general(ds[:, blk:].astype(BF16), qs, (((0,), (0,)), ((), ())), preferred_element_type=F32) * A_SCALE
            dv = lax.dot_general(p[:, blk:].astype(BF16), dos_b, (((0,), (0,)), ((), ())), preferred_element_type=F32)
            for g in range(A_GROUP):
                h = A_GROUP * k + g
                dx_ref[:, h * hd:(h + 1) * hd] = dq[g * blk:(g + 1) * blk]
            dx_ref[:, A_Q_COLS + k * hd:A_Q_COLS + (k + 1) * hd] = dk
            dx_ref[:, A_Q_COLS + A_KV_COLS + k * hd:A_Q_COLS + A_KV_COLS + (k + 1) * hd] = dv
            sink = _group_column(s_ref, k, blk)
            contrib = -jnp.exp(sink - lses[:rows]) * dlt[:rows]
            for g in range(A_GROUP):
                dsink_cols.append(jnp.sum(contrib[g * blk:(g + 1) * blk], axis=0, keepdims=True))
        _accumulate(ds_ref, jnp.concatenate(dsink_cols, axis=1), n)

    q_spec = lambda f: pl.BlockSpec((blk, A_Q_COLS), lambda n: (f(n), 0))
    l_spec = lambda f: pl.BlockSpec((blk, A_HEADS), lambda n: (f(n), 0))
    same = lambda n: n
    return pl.pallas_call(
        body, name=name,
        out_shape=(jax.ShapeDtypeStruct((t, A_COLS), F32), jax.ShapeDtypeStruct((1, A_HEADS), F32)), grid=(nb,),
        in_specs=[q_spec(same), q_spec(nxt),
                  pl.BlockSpec((blk, 2 * A_KV_COLS), lambda n: (n, kv_block)),
                  pl.BlockSpec((blk, 2 * A_KV_COLS), lambda n: (jnp.maximum(n - 1, 0), kv_block)),
                  q_spec(same), q_spec(nxt), q_spec(same), q_spec(nxt), l_spec(same), l_spec(nxt),
                  _const_spec((1, A_HEADS))],
        out_specs=(pl.BlockSpec((blk, A_COLS), lambda n: (n, 0)), _const_spec((1, A_HEADS))),
        compiler_params=_params(("arbitrary",)),
    )(qkv_r, qkv_r, qkv_r, qkv_r, do, do, o, o, lse, lse, sinks)


C_DOWN_COLS = C_Q_RANK + C_KV_RANK + C_ROPE
C_Q_COLS = C_HEADS * C_QK
C_KV_COLS = C_HEADS * (C_NOPE + C_V)
C_O_COLS = C_HEADS * C_V
C_PAD = LANES
C_SCALE = C_QK ** -0.5
LOG2E = 1.4426950408889634
LN2 = 0.6931471805599453
C_Q_SCALE = C_SCALE * LOG2E
C_PAIR = 2


def _mla_latent_fwd(down, q_a_norm, kv_a_norm, name):
    t = down.shape[0]
    tm = _div_tile(t, 512, 16)

    def body(x_ref, gq_ref, gk_ref, cq_ref, ckv_ref):
        cq, ckv = x_ref[:, :C_Q_RANK], x_ref[:, C_Q_RANK:C_Q_RANK + C_KV_RANK]
        cq_ref[...] = (cq * _rstd(cq) * gq_ref[...]).astype(BF16)
        ckv_ref[...] = (ckv * _rstd(ckv) * gk_ref[...]).astype(BF16)

    return pl.pallas_call(
        body, name=name,
        out_shape=(jax.ShapeDtypeStruct((t, C_Q_RANK), BF16), jax.ShapeDtypeStruct((t, C_KV_RANK), BF16)), grid=(t // tm,),
        in_specs=[_row_spec(tm, C_DOWN_COLS), _const_spec((1, C_Q_RANK)), _const_spec((1, C_KV_RANK))],
        out_specs=(_row_spec(tm, C_Q_RANK), _row_spec(tm, C_KV_RANK)), compiler_params=_params(("parallel",)),
    )(down, q_a_norm, kv_a_norm)


def _mla_latent_bwd(down, dcq, dckv, dkrope, q_a_norm, kv_a_norm, name):
    t = down.shape[0]
    tm = _div_tile(t, 512, 16)

    def body(x_ref, dcq_ref, dckv_ref, dkr_ref, gq_ref, gk_ref, o_ref, dgq_ref, dgk_ref):
        dq, dgq = _norm_bwd(x_ref[:, :C_Q_RANK], gq_ref[...], dcq_ref[...])
        dkv, dgk = _norm_bwd(x_ref[:, C_Q_RANK:C_Q_RANK + C_KV_RANK], gk_ref[...], dckv_ref[...])
        o_ref[...] = jnp.concatenate([dq, dkv, dkr_ref[...]], axis=1).astype(BF16)
        _accumulate(dgq_ref, dgq, pl.program_id(0))
        _accumulate(dgk_ref, dgk, pl.program_id(0))

    return pl.pallas_call(
        body, name=name,
        out_shape=(jax.ShapeDtypeStruct((t, C_DOWN_COLS), BF16), jax.ShapeDtypeStruct((1, C_Q_RANK), F32),
                   jax.ShapeDtypeStruct((1, C_KV_RANK), F32)),
        grid=(t // tm,),
        in_specs=[_row_spec(tm, C_DOWN_COLS), _row_spec(tm, C_Q_RANK), _row_spec(tm, C_KV_RANK), _row_spec(tm, C_ROPE),
                  _const_spec((1, C_Q_RANK)), _const_spec((1, C_KV_RANK))],
        out_specs=(_row_spec(tm, C_DOWN_COLS), _const_spec((1, C_Q_RANK)), _const_spec((1, C_KV_RANK))),
        compiler_params=_params(("arbitrary",)),
    )(down, dcq, dckv, dkrope, q_a_norm, kv_a_norm)


def _head_major_spec(tm, width):
    return pl.BlockSpec((C_HEADS, tm, width), lambda i: (0, i, 0))


def _mla_qk_fwd(qw, kvw, down, q_norm, k_norm, cos, sin, name):
    t = qw.shape[0]
    tm = _div_tile(t, 256, 16)
    kvd = C_NOPE + C_V

    def body(q_ref, kv_ref, dn_ref, gq_ref, gk_ref, cos_ref, sin_ref, qo_ref, ko_ref, vo_ref):
        cosv, sinv = cos_ref[...], sin_ref[...]
        k_rope = dn_ref[:, C_Q_RANK + C_KV_RANK:]
        pad = jnp.zeros((tm, C_PAD - C_QK), F32)
        one_then_zeros = (lax.broadcasted_iota(jnp.int32, (tm, C_PAD - C_V), 1) == 0).astype(F32)
        for h in range(C_HEADS):
            qh = _head_fwd(q_ref[:, h * C_QK:(h + 1) * C_QK], gq_ref[...], cosv, sinv, C_NOPE, C_ROPE)
            kx = jnp.concatenate([kv_ref[:, h * kvd:h * kvd + C_NOPE], k_rope], axis=1)
            kh = _head_fwd(kx, gk_ref[...], cosv, sinv, C_NOPE, C_ROPE)
            qo_ref[h] = jnp.concatenate([qh * C_Q_SCALE, pad], axis=1).astype(BF16)
            ko_ref[h] = jnp.concatenate([kh, pad], axis=1).astype(BF16)
            vo_ref[h] = jnp.concatenate([kv_ref[:, h * kvd + C_NOPE:(h + 1) * kvd], one_then_zeros], axis=1).astype(BF16)

    return pl.pallas_call(
        body, name=name,
        out_shape=(jax.ShapeDtypeStruct((C_HEADS, t, C_PAD), BF16), jax.ShapeDtypeStruct((C_HEADS, t, C_PAD), BF16),
                   jax.ShapeDtypeStruct((C_HEADS, t, C_PAD), BF16)),
        grid=(t // tm,),
        in_specs=[_row_spec(tm, C_Q_COLS), _row_spec(tm, C_KV_COLS), _row_spec(tm, C_DOWN_COLS), _const_spec((1, C_QK)),
                  _const_spec((1, C_QK)), _row_spec(tm, C_QK), _row_spec(tm, C_QK)],
        out_specs=(_head_major_spec(tm, C_PAD), _head_major_spec(tm, C_PAD), _head_major_spec(tm, C_PAD)),
        compiler_params=_params(("parallel",)),
    )(qw, kvw, down, q_norm, k_norm, cos, sin)


def _mla_qk_bwd(qw, kvw, down, dq, dk, dv, q_norm, k_norm, cos, sin, name):
    t = qw.shape[0]
    tm = _div_tile(t, 256, 16)
    kvd = C_NOPE + C_V

    def body(q_ref, kv_ref, dn_ref, dq_ref, dk_ref, dv_ref, gq_ref, gk_ref, cos_ref, sin_ref,
             dqw_ref, dkvw_ref, dkr_ref, dgq_ref, dgk_ref):
        cosv, sinv = cos_ref[...], sin_ref[...]
        k_rope = dn_ref[:, C_Q_RANK + C_KV_RANK:]
        dgq = jnp.zeros((1, C_QK), F32)
        dgk = jnp.zeros((1, C_QK), F32)
        dkr = jnp.zeros((tm, C_ROPE), F32)
        for h in range(C_HEADS):
            dxq, dg = _head_bwd(q_ref[:, h * C_QK:(h + 1) * C_QK], gq_ref[...], cosv, sinv, dq_ref[h][:, :C_QK], C_NOPE, C_ROPE)
            dgq = dgq + dg
            dqw_ref[:, h * C_QK:(h + 1) * C_QK] = dxq.astype(BF16)
            kx = jnp.concatenate([kv_ref[:, h * kvd:h * kvd + C_NOPE], k_rope], axis=1)
            dxk, dg = _head_bwd(kx, gk_ref[...], cosv, sinv, dk_ref[h][:, :C_QK], C_NOPE, C_ROPE)
            dgk = dgk + dg
            dkr = dkr + dxk[:, C_NOPE:]
            dkvw_ref[:, h * kvd:(h + 1) * kvd] = jnp.concatenate([dxk[:, :C_NOPE], dv_ref[h]], axis=1).astype(BF16)
        dkr_ref[...] = dkr
        _accumulate(dgq_ref, dgq, pl.program_id(0))
        _accumulate(dgk_ref, dgk, pl.program_id(0))

    return pl.pallas_call(
        body, name=name,
        out_shape=(jax.ShapeDtypeStruct((t, C_Q_COLS), BF16), jax.ShapeDtypeStruct((t, C_KV_COLS), BF16),
                   jax.ShapeDtypeStruct((t, C_ROPE), F32), jax.ShapeDtypeStruct((1, C_QK), F32),
                   jax.ShapeDtypeStruct((1, C_QK), F32)),
        grid=(t // tm,),
        in_specs=[_row_spec(tm, C_Q_COLS), _row_spec(tm, C_KV_COLS), _row_spec(tm, C_DOWN_COLS),
                  _head_major_spec(tm, C_PAD), _head_major_spec(tm, C_PAD), _head_major_spec(tm, C_V),
                  _const_spec((1, C_QK)), _const_spec((1, C_QK)), _row_spec(tm, C_QK), _row_spec(tm, C_QK)],
        out_specs=(_row_spec(tm, C_Q_COLS), _row_spec(tm, C_KV_COLS), _row_spec(tm, C_ROPE), _const_spec((1, C_QK)),
                   _const_spec((1, C_QK))),
        compiler_params=_params(("arbitrary",)),
    )(qw, kvw, down, dq, dk, dv, q_norm, k_norm, cos, sin)


def _causal_keep(rows, cols, row_offset=0, transposed=False):
    row = lax.broadcasted_iota(jnp.int32, (rows, cols), 0) + row_offset
    col = lax.broadcasted_iota(jnp.int32, (rows, cols), 1)
    return (row <= col) if transposed else (col <= row)


def _mla_fwd(q, k, v, name):
    _, t, _ = q.shape
    blk = min(MLA_FWD_BLOCK, t)
    nq = t // blk

    def body(q_ref, k_ref, v_ref, o_ref, lse_ref, m_sc, acc_sc):
        qi = pl.program_id(1)
        m_sc[...] = jnp.full_like(m_sc, NEG)
        acc_sc[...] = jnp.zeros_like(acc_sc)

        def step(ki, masked):
            rows = pl.ds(pl.multiple_of(ki * blk, blk), blk)
            for hh in range(C_PAIR):
                s = lax.dot_general(q_ref[hh], k_ref[hh, rows, :], (((1,), (1,)), ((), ())), preferred_element_type=F32)
                if masked:
                    s = jnp.where(_causal_keep(blk, blk), s, NEG)
                m_prev = m_sc[hh]
                m_new = jnp.maximum(m_prev, jnp.max(s, axis=-1, keepdims=True))
                p = jnp.exp2(s - m_new)
                acc_sc[hh] = jnp.exp2(m_prev - m_new) * acc_sc[hh] + jnp.dot(p.astype(BF16), v_ref[hh, rows, :],
                                                                                preferred_element_type=F32)
                m_sc[hh] = m_new

        def below_diagonal(ki, carry):
            step(ki, False)
            return carry

        lax.fori_loop(0, qi, below_diagonal, 0)
        step(qi, True)
        outs = []
        for hh in range(C_PAIR):
            denom = acc_sc[hh, :, C_V:C_V + 1]
            outs.append(acc_sc[hh, :, :C_V] / denom)
            lse_ref[hh] = m_sc[hh] + jnp.log(denom) * LOG2E
        o_ref[...] = jnp.concatenate(outs, axis=1).astype(BF16)

    whole = lambda hp, qi: (hp, 0, 0)
    return pl.pallas_call(
        body, name=name,
        out_shape=(jax.ShapeDtypeStruct((t, C_O_COLS), BF16), jax.ShapeDtypeStruct((C_HEADS, t, 1), F32)),
        grid=(C_HEADS // C_PAIR, nq),
        in_specs=[pl.BlockSpec((C_PAIR, blk, C_PAD), lambda hp, qi: (hp, qi, 0)),
                  pl.BlockSpec((C_PAIR, t, C_PAD), whole), pl.BlockSpec((C_PAIR, t, C_PAD), whole)],
        out_specs=(pl.BlockSpec((blk, C_PAIR * C_V), lambda hp, qi: (qi, hp)),
                   pl.BlockSpec((C_PAIR, blk, 1), lambda hp, qi: (hp, qi, 0))),
        scratch_shapes=[pltpu.VMEM((C_PAIR, blk, 1), F32), pltpu.VMEM((C_PAIR, blk, C_PAD), F32)],
        compiler_params=_params(("parallel", "arbitrary")),
    )(q, k, v)


def _mla_delta(do, o, name):
    t = do.shape[0]
    blk = min(MLA_BLOCK, t)

    def body(do_ref, o_ref, dlt_ref, dob_ref):
        for hh in range(C_PAIR):
            do_h = do_ref[:, hh * C_V:(hh + 1) * C_V]
            dlt_ref[hh] = jnp.sum(do_h * o_ref[:, hh * C_V:(hh + 1) * C_V].astype(F32), axis=-1, keepdims=True)
        dob_ref[...] = do_ref[...].astype(BF16)

    wide = pl.BlockSpec((blk, C_PAIR * C_V), lambda hp, i: (i, hp))
    return pl.pallas_call(
        body, name=name,
        out_shape=(jax.ShapeDtypeStruct((C_HEADS, t, 1), F32), jax.ShapeDtypeStruct(do.shape, BF16)),
        grid=(C_HEADS // C_PAIR, t // blk), in_specs=[wide, wide],
        out_specs=(pl.BlockSpec((C_PAIR, blk, 1), lambda hp, i: (hp, i, 0)), wide),
        compiler_params=_params(("parallel", "parallel")),
    )(do, o)


def _mla_bwd(q, k, v, do_b, lse_rows, dlt_rows, name):
    _, t, _ = q.shape
    blk = min(MLA_BLOCK, t)
    nq = t // blk

    def body(q_ref, k_ref, v_ref, do_ref, lse_ref, dlt_ref, dq_hbm, dk_ref, dv_ref, dq_sc, dk_sc, dv_sc, sem):
        hp, ki = pl.program_id(0), pl.program_id(1)

        @pl.when(ki == 0)
        def _():
            dq_sc[...] = jnp.zeros_like(dq_sc)

        dk_sc[...] = jnp.zeros_like(dk_sc)
        dv_sc[...] = jnp.zeros_like(dv_sc)

        def step(qi, masked):
            rows = pl.ds(pl.multiple_of(qi * blk, blk), blk)
            for hh in range(C_PAIR):
                qb = q_ref[hh, rows, :]
                dob = do_ref[rows, hh * C_V:(hh + 1) * C_V]
                s = lax.dot_general(k_ref[hh], qb, (((1,), (1,)), ((), ())), preferred_element_type=F32)
                if masked:
                    s = jnp.where(_causal_keep(blk, blk, transposed=True), s, NEG)
                p = jnp.exp2(s - lse_ref[hh, qi])
                dp = lax.dot_general(v_ref[hh, :, :C_V], dob, (((1,), (1,)), ((), ())), preferred_element_type=F32)
                ds = (p * (dp - dlt_ref[hh, qi])).astype(BF16)
                dv_sc[hh] += jnp.dot(p.astype(BF16), dob, preferred_element_type=F32)
                dk_sc[hh] += jnp.dot(ds, qb, preferred_element_type=F32)
                dq_sc[hh, rows, :] += lax.dot_general(ds, k_ref[hh], (((0,), (0,)), ((), ())), preferred_element_type=F32)

        def above_diagonal(qi, carry):
            step(qi, False)
            return carry

        step(ki, True)
        lax.fori_loop(ki + 1, nq, above_diagonal, 0)
        dk_ref[...] = dk_sc[...] * LN2
        dv_ref[...] = dv_sc[...]

        @pl.when(ki == nq - 1)
        def _():
            dq_sc[...] = dq_sc[...] * C_SCALE
            out = pltpu.make_async_copy(dq_sc, dq_hbm.at[pl.ds(hp * C_PAIR, C_PAIR)], sem)
            out.start()
            out.wait()

    once = pl.Buffered(1)
    whole = lambda hp, ki: (hp, 0, 0)
    whole4 = lambda hp, ki: (hp, 0, 0, 0)
    kmap = lambda hp, ki: (hp, ki, 0)
    return pl.pallas_call(
        body, name=name,
        out_shape=(jax.ShapeDtypeStruct((C_HEADS, t, C_PAD), F32), jax.ShapeDtypeStruct((C_HEADS, t, C_PAD), F32),
                   jax.ShapeDtypeStruct((C_HEADS, t, C_V), F32)),
        grid=(C_HEADS // C_PAIR, nq),
        in_specs=[pl.BlockSpec((C_PAIR, t, C_PAD), whole, pipeline_mode=once), pl.BlockSpec((C_PAIR, blk, C_PAD), kmap),
                  pl.BlockSpec((C_PAIR, blk, C_PAD), kmap),
                  pl.BlockSpec((t, C_PAIR * C_V), lambda hp, ki: (0, hp), pipeline_mode=once),
                  pl.BlockSpec((C_PAIR, nq, 1, blk), whole4, pipeline_mode=once),
                  pl.BlockSpec((C_PAIR, nq, 1, blk), whole4, pipeline_mode=once)],
        out_specs=(pl.BlockSpec(memory_space=pl.ANY), pl.BlockSpec((C_PAIR, blk, C_PAD), kmap),
                   pl.BlockSpec((C_PAIR, blk, C_V), kmap)),
        scratch_shapes=[pltpu.VMEM((C_PAIR, t, C_PAD), F32), pltpu.VMEM((C_PAIR, blk, C_PAD), F32),
                        pltpu.VMEM((C_PAIR, blk, C_V), F32), pltpu.SemaphoreType.DMA(())],
        compiler_params=_params(("arbitrary", "arbitrary")),
    )(q, k, v, do_b, lse_rows, dlt_rows)


def _adamw(parts, w, m, v, name):
    layers, rows, cols = w.shape
    tm = _div_tile(rows, 256, 16)

    def body(p_ref, w_ref, m_ref, v_ref, g_ref, d_ref, nm_ref, nv_ref):
        g = p_ref[0].astype(F32)
        for j in range(1, N_DEV):
            g = g + p_ref[j].astype(F32)
        nm = ADAM_B1 * m_ref[...] + (1.0 - ADAM_B1) * g
        nv = ADAM_B2 * v_ref[...] + (1.0 - ADAM_B2) * jnp.square(g)
        m_hat = nm / (1.0 - ADAM_B1 ** ADAM_STEP)
        v_hat = nv / (1.0 - ADAM_B2 ** ADAM_STEP)
        g_ref[...] = g
        d_ref[...] = -ADAM_LR * (m_hat / (jnp.sqrt(v_hat) + ADAM_EPS) + ADAM_WD * w_ref[...])
        nm_ref[...] = nm
        nv_ref[...] = nv

    spec = pl.BlockSpec((None, tm, cols), lambda l, i: (l, i, 0))
    return pl.pallas_call(
        body, name=name, out_shape=tuple(jax.ShapeDtypeStruct(w.shape, F32) for _ in range(4)),
        grid=(layers, rows // tm),
        in_specs=[pl.BlockSpec((None, N_DEV, tm, cols), lambda l, i: (l, 0, i, 0)), spec, spec, spec],
        out_specs=(spec, spec, spec, spec), compiler_params=_params(("parallel", "parallel")),
    )(parts, w, m, v)


def _join_shards(gathered, axis):
    moved = jnp.moveaxis(gathered, 1, axis)
    shape = list(moved.shape)
    shape[axis:axis + 2] = [shape[axis] * shape[axis + 1]]
    return moved.reshape(shape)


def _split_shards(full, axis):
    shape = list(full.shape)
    shape[axis:axis + 1] = [N_DEV, shape[axis] // N_DEV]
    return jnp.moveaxis(full.reshape(shape), axis, 1)


def _as_rows(shape):
    rest = tuple(shape[1:])
    return (shape[0], 1, rest[0]) if len(rest) == 1 else (shape[0],) + rest


def _forward_backward(x, positions, target, w, rep):
    cos_a, sin_a = _rope_tables(positions, A_ROT_DIM, 0, A_HEAD_DIM - A_ROT_DIM)
    cos_c, sin_c = _rope_tables(positions, C_ROPE, C_NOPE, 0)
    saved = []
    for i in range(DEPTH):
        kind, j = i % N_MIXERS, i // N_MIXERS
        s = {'x': x}
        h1 = _rmsnorm_fwd(x, rep['mix_norm'][i:i + 1], f"mix_norm_fwd_{i}")
        s['h1'] = h1
        if kind == 0:
            s['qkv'] = _matmul(h1, w['a_w_qkv'][j], 'nn', f"a_qkv_{i}")
            s['qkv_r'] = _swa_prep_fwd(s['qkv'], rep['a_q_norm'][j:j + 1], rep['a_k_norm'][j:j + 1], cos_a, sin_a,
                                       f"a_prep_fwd_{i}")
            s['o'], s['lse'] = _swa_fwd(s['qkv_r'], rep['a_sinks'][j:j + 1], f"a_attn_fwd_{i}")
            x1 = _matmul(s['o'], w['a_w_o'][j], 'nn', f"a_out_{i}", residual=x)
        elif kind == 1:
            s['bcu'] = _matmul(h1, w['b_w_in'][j], 'nn', f"b_in_{i}")
            s['by'] = _sconv_fwd(s['bcu'], w['b_conv_w'][j], f"b_conv_fwd_{i}")
            x1 = _matmul(s['by'], w['b_w_out'][j], 'nn', f"b_out_{i}", residual=x)
        else:
            s['down'] = _matmul(h1, w['c_w_down'][j], 'nn', f"c_down_{i}")
            s['cq'], s['ckv'] = _mla_latent_fwd(s['down'], w['c_q_a_norm'][j:j + 1], w['c_kv_a_norm'][j:j + 1],
                                                f"c_latent_fwd_{i}")
            s['qw'] = _matmul(s['cq'], w['c_w_q_up'][j], 'nn', f"c_q_up_{i}")
            s['kvw'] = _matmul(s['ckv'], w['c_w_kv_up'][j], 'nn', f"c_kv_up_{i}")
            s['q'], s['k'], s['v'] = _mla_qk_fwd(s['qw'], s['kvw'], s['down'], rep['c_q_norm'][j:j + 1],
                                                 rep['c_k_norm'][j:j + 1], cos_c, sin_c, f"c_prep_fwd_{i}")
            s['o'], s['lse'] = _mla_fwd(s['q'], s['k'], s['v'], f"c_attn_fwd_{i}")
            x1 = _matmul(s['o'], w['c_w_o'][j], 'nn', f"c_out_{i}", residual=x)
        s['x1'] = x1
        s['h2'] = _rmsnorm_fwd(x1, rep['ffn_norm'][i:i + 1], f"ffn_norm_fwd_{i}")
        s['gu'] = _matmul(s['h2'], w['f_w_gate_up'][i], 'nn', f"f_gate_up_{i}", out_dtype=BF16)
        s['act'] = _swiglu_fwd(s['gu'], f"f_act_fwd_{i}")
        x = _matmul(s['act'], w['f_w_down'][i], 'nn', f"f_down_{i}", residual=x1)
        saved.append(s)

    loss, dx = _loss_head(x, target, "loss_head")

    per_layer = {n: {} for n in WEIGHTS}
    for i in reversed(range(DEPTH)):
        kind, j = i % N_MIXERS, i // N_MIXERS
        s = saved[i]
        per_layer['f_w_down'][i] = _matmul(s['act'], dx, 'tn', f"f_down_dw_{i}", out_dtype=BF16)
        dact = _matmul(dx, w['f_w_down'][i], 'nt', f"f_down_dx_{i}", out_dtype=BF16)
        dgu = _swiglu_bwd(s['gu'], dact, f"f_act_bwd_{i}")
        per_layer['f_w_gate_up'][i] = _matmul(s['h2'], dgu, 'tn', f"f_gate_up_dw_{i}", out_dtype=BF16)
        dh2 = _matmul(dgu, w['f_w_gate_up'][i], 'nt', f"f_gate_up_dx_{i}")
        dx1, per_layer['ffn_norm'][i] = _rmsnorm_bwd(s['x1'], rep['ffn_norm'][i:i + 1], dh2, dx, f"ffn_norm_bwd_{i}")
        if kind == 0:
            per_layer['a_w_o'][j] = _matmul(s['o'], dx1, 'tn', f"a_out_dw_{i}", out_dtype=BF16)
            do = _matmul(dx1, w['a_w_o'][j], 'nt', f"a_out_dx_{i}")
            dqkv_r, per_layer['a_sinks'][j] = _swa_bwd(s['qkv_r'], s['o'], s['lse'], do, rep['a_sinks'][j:j + 1],
                                                       f"a_attn_bwd_{i}")
            dqkv, per_layer['a_q_norm'][j], per_layer['a_k_norm'][j] = _swa_prep_bwd(
                s['qkv'], dqkv_r, rep['a_q_norm'][j:j + 1], rep['a_k_norm'][j:j + 1], cos_a, sin_a, f"a_prep_bwd_{i}")
            per_layer['a_w_qkv'][j] = _matmul(s['h1'], dqkv, 'tn', f"a_qkv_dw_{i}", out_dtype=BF16)
            dh1 = _matmul(dqkv, w['a_w_qkv'][j], 'nt', f"a_qkv_dx_{i}")
        elif kind == 1:
            per_layer['b_w_out'][j] = _matmul(s['by'], dx1, 'tn', f"b_out_dw_{i}", out_dtype=BF16)
            dby = _matmul(dx1, w['b_w_out'][j], 'nt', f"b_out_dx_{i}")
            dbcu, per_layer['b_conv_w'][j] = _sconv_bwd(s['bcu'], dby, w['b_conv_w'][j], f"b_conv_bwd_{i}")
            per_layer['b_w_in'][j] = _matmul(s['h1'], dbcu, 'tn', f"b_in_dw_{i}", out_dtype=BF16)
            dh1 = _matmul(dbcu, w['b_w_in'][j], 'nt', f"b_in_dx_{i}")
        else:
            per_layer['c_w_o'][j] = _matmul(s['o'], dx1, 'tn', f"c_out_dw_{i}", out_dtype=BF16)
            do = _matmul(dx1, w['c_w_o'][j], 'nt', f"c_out_dx_{i}")
            dlt, do_b = _mla_delta(do, s['o'], f"c_attn_delta_{i}")
            blk = min(MLA_BLOCK, do.shape[0])
            as_rows = lambda col: col.reshape(C_HEADS, do.shape[0] // blk, 1, blk)
            dq, dk, dv = _mla_bwd(s['q'], s['k'], s['v'], do_b, as_rows(s['lse']), as_rows(dlt), f"c_attn_bwd_{i}")
            dqw, dkvw, dkrope, per_layer['c_q_norm'][j], per_layer['c_k_norm'][j] = _mla_qk_bwd(
                s['qw'], s['kvw'], s['down'], dq, dk, dv, rep['c_q_norm'][j:j + 1], rep['c_k_norm'][j:j + 1], cos_c, sin_c,
                f"c_prep_bwd_{i}")
            per_layer['c_w_q_up'][j] = _matmul(s['cq'], dqw, 'tn', f"c_q_up_dw_{i}", out_dtype=BF16)
            dcq = _matmul(dqw, w['c_w_q_up'][j], 'nt', f"c_q_up_dx_{i}")
            per_layer['c_w_kv_up'][j] = _matmul(s['ckv'], dkvw, 'tn', f"c_kv_up_dw_{i}", out_dtype=BF16)
            dckv = _matmul(dkvw, w['c_w_kv_up'][j], 'nt', f"c_kv_up_dx_{i}")
            ddown, per_layer['c_q_a_norm'][j], per_layer['c_kv_a_norm'][j] = _mla_latent_bwd(
                s['down'], dcq, dckv, dkrope, w['c_q_a_norm'][j:j + 1], w['c_kv_a_norm'][j:j + 1], f"c_latent_bwd_{i}")
            per_layer['c_w_down'][j] = _matmul(s['h1'], ddown, 'tn', f"c_down_dw_{i}", out_dtype=BF16)
            dh1 = _matmul(ddown, w['c_w_down'][j], 'nt', f"c_down_dx_{i}")
        dx, per_layer['mix_norm'][i] = _rmsnorm_bwd(s['x'], rep['mix_norm'][i:i + 1], dh1, dx1, f"mix_norm_bwd_{i}")

    grads = {}
    for n in WEIGHTS:
        stacked = jnp.stack([per_layer[n][j] for j in sorted(per_layer[n])])
        if n in ('mix_norm', 'ffn_norm', 'a_q_norm', 'a_k_norm', 'a_sinks', 'c_q_a_norm', 'c_kv_a_norm', 'c_q_norm', 'c_k_norm'):
            stacked = stacked.reshape(stacked.shape[0], stacked.shape[-1])
        if n in SHARD_AXIS:
            stacked = _split_shards(stacked, SHARD_AXIS[n])
        grads[n] = stacked
    return loss, dx, grads


def kernel(x, positions, mix_norm, ffn_norm, a_w_qkv, a_q_norm, a_k_norm, a_sinks, a_w_o, b_w_in, b_conv_w, b_w_out, c_w_down, c_q_a_norm, c_kv_a_norm, c_w_q_up, c_w_kv_up, c_q_norm, c_k_norm, c_w_o, f_w_gate_up, f_w_down, loss_target, m_mix_norm, m_ffn_norm, m_a_w_qkv, m_a_q_norm, m_a_k_norm, m_a_sinks, m_a_w_o, m_b_w_in, m_b_conv_w, m_b_w_out, m_c_w_down, m_c_q_a_norm, m_c_kv_a_norm, m_c_w_q_up, m_c_w_kv_up, m_c_q_norm, m_c_k_norm, m_c_w_o, m_f_w_gate_up, m_f_w_down, v_mix_norm, v_ffn_norm, v_a_w_qkv, v_a_q_norm, v_a_k_norm, v_a_sinks, v_a_w_o, v_b_w_in, v_b_conv_w, v_b_w_out, v_c_w_down, v_c_q_a_norm, v_c_kv_a_norm, v_c_w_q_up, v_c_w_kv_up, v_c_q_norm, v_c_k_norm, v_c_w_o, v_f_w_gate_up, v_f_w_down):
    local = dict(mix_norm=mix_norm, ffn_norm=ffn_norm, a_w_qkv=a_w_qkv, a_q_norm=a_q_norm, a_k_norm=a_k_norm, a_sinks=a_sinks, a_w_o=a_w_o, b_w_in=b_w_in, b_conv_w=b_conv_w, b_w_out=b_w_out, c_w_down=c_w_down, c_q_a_norm=c_q_a_norm, c_kv_a_norm=c_kv_a_norm, c_w_q_up=c_w_q_up, c_w_kv_up=c_w_kv_up, c_q_norm=c_q_norm, c_k_norm=c_k_norm, c_w_o=c_w_o, f_w_gate_up=f_w_gate_up, f_w_down=f_w_down)
    mom1 = dict(mix_norm=m_mix_norm, ffn_norm=m_ffn_norm, a_w_qkv=m_a_w_qkv, a_q_norm=m_a_q_norm, a_k_norm=m_a_k_norm, a_sinks=m_a_sinks, a_w_o=m_a_w_o, b_w_in=m_b_w_in, b_conv_w=m_b_conv_w, b_w_out=m_b_w_out, c_w_down=m_c_w_down, c_q_a_norm=m_c_q_a_norm, c_kv_a_norm=m_c_kv_a_norm, c_w_q_up=m_c_w_q_up, c_w_kv_up=m_c_w_kv_up, c_q_norm=m_c_q_norm, c_k_norm=m_c_k_norm, c_w_o=m_c_w_o, f_w_gate_up=m_f_w_gate_up, f_w_down=m_f_w_down)
    mom2 = dict(mix_norm=v_mix_norm, ffn_norm=v_ffn_norm, a_w_qkv=v_a_w_qkv, a_q_norm=v_a_q_norm, a_k_norm=v_a_k_norm, a_sinks=v_a_sinks, a_w_o=v_a_w_o, b_w_in=v_b_w_in, b_conv_w=v_b_conv_w, b_w_out=v_b_w_out, c_w_down=v_c_w_down, c_q_a_norm=v_c_q_a_norm, c_kv_a_norm=v_c_kv_a_norm, c_w_q_up=v_c_w_q_up, c_w_kv_up=v_c_w_kv_up, c_q_norm=v_c_q_norm, c_k_norm=v_c_k_norm, c_w_o=v_c_w_o, f_w_gate_up=v_f_w_gate_up, f_w_down=v_f_w_down)
    t, d = x.shape[1], x.shape[2]

    gathered = _gather_blocks([local[n].astype(BF16) if n in GATHER_BF16 else local[n] for n in SHARDED],
                              "gather_weights")
    full = {n: _join_shards(g, SHARD_AXIS[n]) for n, g in zip(SHARDED, gathered)}
    rep = {n: local[n] for n in REPLICATED}

    loss, grad_x, grads = _forward_backward(x.reshape(t, d), positions.reshape(t), loss_target.reshape(t, d), full, rep)

    out_g, out_d, out_m, out_v = {}, {}, {}, {}

    def update(names, parts):
        for n, part in zip(names, parts):
            shape = local[n].shape if n in SHARD_AXIS else (1,) + local[n].shape
            view = _as_rows(shape)
            results = _adamw(part.reshape(view[0], N_DEV, view[1], view[2]),
                             *[src[n].reshape(view) for src in (local, mom1, mom2)], name="adamw_" + n)
            for dst, res in zip((out_g, out_d, out_m, out_v), results):
                dst[n] = res.reshape(local[n].shape)

    update(SHARDED, _scatter_blocks([grads[n] for n in SHARDED], "scatter_gradients"))
    update(REPLICATED, _gather_blocks([grads[n].reshape((1,) + grads[n].shape) for n in REPLICATED],
                                      "gather_small_gradients"))

    loss = lax.psum(loss.reshape(()), MESH_AXES)
    outs = [loss, grad_x.reshape(1, t, d)]
    for res in (out_g, out_d, out_m, out_v):
        outs += [res[n] for n in WEIGHTS]
    return tuple(outs)
```

```python
import functools

import jax
import jax.numpy as jnp
from jax import lax
from jax.experimental import pallas as pl
from jax.experimental.pallas import tpu as pltpu

F32 = jnp.float32
BF16 = jnp.bfloat16

N_DEV = 8
MESH_AXES = ("x", "y", "c")

DEPTH = 4
N_MIXERS = 3
ROPE_THETA = 500000.0
EPS = 1e-6
A_HEADS, A_KV_HEADS, A_HEAD_DIM, A_ROT_DIM, A_WINDOW = 16, 4, 64, 16, 128
A_GROUP = A_HEADS // A_KV_HEADS
C_HEADS, C_NOPE, C_ROPE, C_V, C_Q_RANK, C_KV_RANK = 16, 64, 32, 64, 384, 256
C_QK = C_NOPE + C_ROPE
ADAM_LR, ADAM_B1, ADAM_B2, ADAM_EPS, ADAM_WD, ADAM_STEP = 0.001, 0.9, 0.999, 1e-08, 0.01, 10

VMEM_LIMIT_BYTES = 48 * 1024 * 1024
LANES = 128
NEG = -1e30
MLA_BLOCK = 512
MLA_FWD_BLOCK = 1024

WEIGHTS = ['mix_norm', 'ffn_norm', 'a_w_qkv', 'a_q_norm', 'a_k_norm', 'a_sinks', 'a_w_o', 'b_w_in', 'b_conv_w', 'b_w_out',
           'c_w_down', 'c_q_a_norm', 'c_kv_a_norm', 'c_w_q_up', 'c_w_kv_up', 'c_q_norm', 'c_k_norm', 'c_w_o', 'f_w_gate_up',
           'f_w_down']
SHARD_AXIS = {'a_w_qkv': 2, 'a_w_o': 1, 'b_w_in': 2, 'b_conv_w': 2, 'b_w_out': 1, 'c_w_down': 1, 'c_q_a_norm': 1,
              'c_kv_a_norm': 1, 'c_w_q_up': 2, 'c_w_kv_up': 2, 'c_w_o': 1, 'f_w_gate_up': 2, 'f_w_down': 1}
SHARDED = [n for n in WEIGHTS if n in SHARD_AXIS]
REPLICATED = [n for n in WEIGHTS if n not in SHARD_AXIS]
GATHER_F32 = ['b_conv_w', 'c_q_a_norm', 'c_kv_a_norm']
GATHER_BF16 = [n for n in SHARDED if n not in GATHER_F32]

def _params(semantics=None):
    return pltpu.CompilerParams(dimension_semantics=semantics, vmem_limit_bytes=VMEM_LIMIT_BYTES)


def _div_tile(n, cap, mult=LANES):
    best = None
    t = mult
    while t <= min(n, cap):
        if n % t == 0:
            best = t
        t += mult
    return n if best is None else best


def _exchange_call(body, arrays, out_shapes, name):
    n = len(arrays)
    return pl.pallas_call(
        body, name=name, out_shape=tuple(out_shapes),
        in_specs=[pl.BlockSpec(memory_space=pl.ANY)] * n,
        out_specs=tuple(pl.BlockSpec(memory_space=pl.ANY) for _ in range(n)),
        scratch_shapes=[pltpu.SemaphoreType.DMA((n, N_DEV - 1)), pltpu.SemaphoreType.DMA((n, N_DEV - 1)),
                        pltpu.SemaphoreType.DMA((n,))],
    )(*arrays)


def _gather_blocks(arrays, name):
    n = len(arrays)

    def body(*refs):
        src_refs, out_refs = refs[:n], refs[n:2 * n]
        send_sems, recv_sems, local_sems = refs[2 * n:]
        x, y, c = lax.axis_index("x"), lax.axis_index("y"), lax.axis_index("c")
        me, sibling = (x, y, c), (x, y, 1 - c)
        chips = [(1 - x, y), (x, 1 - y), (1 - x, 1 - y)]

        def copy(a, k, block, to, own=False):
            place = out_refs[a].at[:, 4 * block[0] + 2 * block[1] + block[2]]
            return pltpu.make_async_remote_copy(
                src_ref=src_refs[a] if own else place, dst_ref=place, send_sem=send_sems.at[a, k],
                recv_sem=recv_sems.at[a, k], device_id=to, device_id_type=pl.DeviceIdType.MESH)

        local, sent = [], []
        for a in range(n):
            mine = pltpu.make_async_copy(src_refs[a], out_refs[a].at[:, 4 * x + 2 * y + c], local_sems.at[a])
            mine.start()
            local.append(mine)
            first = [copy(a, 0, me, sibling, own=True)]
            first += [copy(a, 1 + j, me, (*chip, c), own=True) for j, chip in enumerate(chips)]
            for cp in first:
                cp.start()
            sent += first
        for a in range(n):
            for j, chip in enumerate(chips):
                copy(a, 1 + j, (*chip, c), me).wait_recv()
                forward = copy(a, 4 + j, (*chip, c), sibling)
                forward.start()
                sent.append(forward)
        for a in range(n):
            copy(a, 0, sibling, me).wait_recv()
            for j, chip in enumerate(chips):
                copy(a, 4 + j, (*chip, 1 - c), me).wait_recv()
        for cp in sent:
            cp.wait_send()
        for cp in local:
            cp.wait()

    shapes = [jax.ShapeDtypeStruct((arr.shape[0], N_DEV) + tuple(arr.shape[1:]), arr.dtype) for arr in arrays]
    return _exchange_call(body, arrays, shapes, name)


def _scatter_blocks(arrays, name):
    n = len(arrays)

    def body(*refs):
        src_refs, out_refs = refs[:n], refs[n:2 * n]
        send_sems, recv_sems, local_sems = refs[2 * n:]
        x, y, c = lax.axis_index("x"), lax.axis_index("y"), lax.axis_index("c")
        me = 4 * x + 2 * y + c
        copies = []
        for a in range(n):
            local = pltpu.make_async_copy(src_refs[a].at[:, me], out_refs[a].at[:, me], local_sems.at[a])
            local.start()
            copies.append(local)
            for r in range(1, N_DEV):
                px = 1 - x if (r >> 2) & 1 else x
                py = 1 - y if (r >> 1) & 1 else y
                pc = 1 - c if r & 1 else c
                cp = pltpu.make_async_remote_copy(
                    src_ref=src_refs[a].at[:, 4 * px + 2 * py + pc], dst_ref=out_refs[a].at[:, me],
                    send_sem=send_sems.at[a, r - 1], recv_sem=recv_sems.at[a, r - 1],
                    device_id=(px, py, pc), device_id_type=pl.DeviceIdType.MESH)
                cp.start()
                copies.append(cp)
        for cp in copies:
            cp.wait()

    shapes = [jax.ShapeDtypeStruct(arr.shape, arr.dtype) for arr in arrays]
    return _exchange_call(body, arrays, shapes, name)


def _matmul(a, b, mode, name, out_dtype=F32, residual=None):
    if mode == 'nn':
        (m, k), (k2, n) = a.shape, b.shape
    elif mode == 'nt':
        (m, k), (n, k2) = a.shape, b.shape
    else:
        (k, m), (k2, n) = a.shape, b.shape
    assert k == k2, (name, a.shape, b.shape, mode)
    if mode == 'tn':
        tm, tk = _div_tile(m, 1408), _div_tile(k, 512, 16)
    else:
        tm, tk = _div_tile(m, 1024, 16), _div_tile(k, 1536)
    tn = _div_tile(n, 1408)
    nk = k // tk
    dims = {'nn': (((1,), (0,)), ((), ())), 'nt': (((1,), (1,)), ((), ())), 'tn': (((0,), (0,)), ((), ()))}[mode]

    def product(a_ref, b_ref):
        return lax.dot_general(a_ref[...].astype(BF16), b_ref[...].astype(BF16), dims, preferred_element_type=F32)

    def finish(r, rest):
        if residual is not None:
            r = r + rest[0][...]
        rest[-1 if nk == 1 else -2][...] = r.astype(out_dtype)

    def body_single(a_ref, b_ref, *rest):
        finish(product(a_ref, b_ref), rest)

    def body_accumulate(a_ref, b_ref, *rest):
        acc = rest[-1]
        kk = pl.program_id(2)

        @pl.when(kk == 0)
        def _():
            acc[...] = jnp.zeros_like(acc)

        acc[...] += product(a_ref, b_ref)

        @pl.when(kk == nk - 1)
        def _():
            finish(acc[...], rest)

    a_spec = pl.BlockSpec((tk, tm), lambda i, j, kk: (kk, i)) if mode == 'tn' else pl.BlockSpec((tm, tk), lambda i, j, kk: (i, kk))
    b_spec = pl.BlockSpec((tn, tk), lambda i, j, kk: (j, kk)) if mode == 'nt' else pl.BlockSpec((tk, tn), lambda i, j, kk: (kk, j))
    o_spec = pl.BlockSpec((tm, tn), lambda i, j, kk: (i, j))
    in_specs, operands = [a_spec, b_spec], [a, b]
    if residual is not None:
        in_specs.append(o_spec)
        operands.append(residual)
    return pl.pallas_call(
        body_single if nk == 1 else body_accumulate, name=name, out_shape=jax.ShapeDtypeStruct((m, n), out_dtype),
        grid=(m // tm, n // tn, nk), in_specs=in_specs, out_specs=o_spec,
        scratch_shapes=[] if nk == 1 else [pltpu.VMEM((tm, tn), F32)],
        compiler_params=_params(("parallel", "parallel", "arbitrary")),
    )(*operands)


def _row_spec(tm, cols):
    return pl.BlockSpec((tm, cols), lambda i: (i, 0))


def _const_spec(shape):
    return pl.BlockSpec(shape, lambda i: tuple(0 for _ in shape))


def _accumulate(ref, value, step):
    @pl.when(step == 0)
    def _():
        ref[...] = value

    @pl.when(step > 0)
    def _():
        ref[...] += value


def _rstd(x):
    return lax.rsqrt(jnp.mean(x * x, axis=-1, keepdims=True) + EPS)


def _norm_bwd(x, g, dout):
    xn = x * _rstd(x)
    dg = jnp.sum(dout * xn, axis=0, keepdims=True)
    dxn = dout * g
    dx = _rstd(x) * (dxn - xn * jnp.mean(dxn * xn, axis=-1, keepdims=True))
    return dx, dg


def _rmsnorm_fwd(x, g, name):
    t, d = x.shape
    tm = _div_tile(t, 512, 16)

    def body(x_ref, g_ref, o_ref):
        xv = x_ref[...]
        o_ref[...] = (xv * _rstd(xv) * g_ref[...]).astype(BF16)

    return pl.pallas_call(
        body, name=name, out_shape=jax.ShapeDtypeStruct((t, d), BF16), grid=(t // tm,),
        in_specs=[_row_spec(tm, d), _const_spec((1, d))], out_specs=_row_spec(tm, d),
        compiler_params=_params(("parallel",)),
    )(x, g)


def _rmsnorm_bwd(x, g, dh, dres, name):
    t, d = x.shape
    tm = _div_tile(t, 512, 8)

    def body(x_ref, g_ref, dh_ref, dres_ref, dx_ref, dg_ref):
        dx, dg = _norm_bwd(x_ref[...], g_ref[...], dh_ref[...])
        dx_ref[...] = dres_ref[...] + dx
        _accumulate(dg_ref, dg, pl.program_id(0))

    return pl.pallas_call(
        body, name=name,
        out_shape=(jax.ShapeDtypeStruct((t, d), F32), jax.ShapeDtypeStruct((1, d), F32)), grid=(t // tm,),
        in_specs=[_row_spec(tm, d), _const_spec((1, d)), _row_spec(tm, d), _row_spec(tm, d)],
        out_specs=(_row_spec(tm, d), _const_spec((1, d))),
        compiler_params=_params(("arbitrary",)),
    )(x, g, dh, dres)


def _sigmoid(x):
    return 1.0 / (1.0 + jnp.exp(-x))


def _swiglu_fwd(gu, name):
    t, f2 = gu.shape
    f = f2 // 2
    tm = _div_tile(t, 512, 16)

    def body(gu_ref, o_ref):
        gate, up = gu_ref[:, :f].astype(F32), gu_ref[:, f:].astype(F32)
        o_ref[...] = (gate * _sigmoid(gate) * up).astype(BF16)

    return pl.pallas_call(
        body, name=name, out_shape=jax.ShapeDtypeStruct((t, f), BF16), grid=(t // tm,),
        in_specs=[_row_spec(tm, f2)], out_specs=_row_spec(tm, f),
        compiler_params=_params(("parallel",)),
    )(gu)


def _swiglu_bwd(gu, da, name):
    t, f2 = gu.shape
    f = f2 // 2
    tm = _div_tile(t, 512, 16)

    def body(gu_ref, da_ref, o_ref):
        gate, up, dav = gu_ref[:, :f].astype(F32), gu_ref[:, f:].astype(F32), da_ref[...].astype(F32)
        sig = _sigmoid(gate)
        o_ref[:, :f] = (dav * up * (sig * (1.0 + gate * (1.0 - sig)))).astype(BF16)
        o_ref[:, f:] = (dav * (gate * sig)).astype(BF16)

    return pl.pallas_call(
        body, name=name, out_shape=jax.ShapeDtypeStruct((t, f2), BF16), grid=(t // tm,),
        in_specs=[_row_spec(tm, f2), _row_spec(tm, f)], out_specs=_row_spec(tm, f2),
        compiler_params=_params(("parallel",)),
    )(gu, da)


def _loss_head(y, target, name):
    t, d = y.shape
    tm = _div_tile(t, 512, 8)

    def body(y_ref, t_ref, loss_ref, dy_ref):
        diff = y_ref[...] - t_ref[...]
        dy_ref[...] = diff * (1.0 / d)
        part = jnp.sum(jnp.sum(diff * diff, axis=1, keepdims=True), axis=0, keepdims=True) * (0.5 / d)
        _accumulate(loss_ref, part, pl.program_id(0))

    return pl.pallas_call(
        body, name=name,
        out_shape=(jax.ShapeDtypeStruct((1, 1), F32), jax.ShapeDtypeStruct((t, d), F32)), grid=(t // tm,),
        in_specs=[_row_spec(tm, d), _row_spec(tm, d)], out_specs=(_const_spec((1, 1)), _row_spec(tm, d)),
        compiler_params=_params(("arbitrary",)),
    )(y, target)


HALO = 8


def _shift_down(z, k, halo_rows):
    tm = z.shape[0]
    row = lax.broadcasted_iota(jnp.int32, z.shape, 0)
    out = pltpu.roll(z, k, 0)
    for j in range(k):
        out = jnp.where(row == j, halo_rows[HALO - k + j:HALO - k + j + 1, :], out)
    return out


def _shift_up(z, k, halo_rows):
    tm = z.shape[0]
    row = lax.broadcasted_iota(jnp.int32, z.shape, 0)
    out = pltpu.roll(z, tm - k, 0)
    for j in range(k):
        out = jnp.where(row == tm - k + j, halo_rows[j:j + 1, :], out)
    return out


def _sconv_specs(t, tm, cols):
    per = tm // HALO
    last = t // HALO - 1
    cur = pl.BlockSpec((tm, cols), lambda i: (i, 0))
    prev = pl.BlockSpec((HALO, cols), lambda i: (jnp.maximum(i * per - 1, 0), 0))
    nxt = pl.BlockSpec((HALO, cols), lambda i: (jnp.minimum((i + 1) * per, last), 0))
    return cur, prev, nxt


def _sconv_fwd(bcu, conv_w, name):
    t, d3 = bcu.shape
    d = d3 // 3
    tm = _div_tile(t, 256, 16)
    cur, prev, _ = _sconv_specs(t, tm, d3)

    def body(cur_ref, prev_ref, w_ref, o_ref):
        i = pl.program_id(0)
        z = cur_ref[:, d:2 * d] * cur_ref[:, 2 * d:]
        zp = prev_ref[:, d:2 * d] * prev_ref[:, 2 * d:] * (i > 0).astype(F32)
        y = w_ref[0:1, :] * _shift_down(z, 2, zp) + w_ref[1:2, :] * _shift_down(z, 1, zp) + w_ref[2:3, :] * z
        o_ref[...] = (cur_ref[:, :d] * y).astype(BF16)

    return pl.pallas_call(
        body, name=name, out_shape=jax.ShapeDtypeStruct((t, d), BF16), grid=(t // tm,),
        in_specs=[cur, prev, _const_spec((3, d))], out_specs=_row_spec(tm, d),
        compiler_params=_params(("parallel",)),
    )(bcu, bcu, conv_w)


def _sconv_bwd(bcu, dout, conv_w, name):
    t, d3 = bcu.shape
    d = d3 // 3
    tm = _div_tile(t, 256, 16)
    cur, prev, nxt = _sconv_specs(t, tm, d3)
    dcur, _, dnxt = _sconv_specs(t, tm, d)
    n_tiles = t // tm

    def body(cur_ref, prev_ref, nxt_ref, do_ref, don_ref, w_ref, o_ref, dw_ref):
        i = pl.program_id(0)
        b, cg, u = cur_ref[:, :d], cur_ref[:, d:2 * d], cur_ref[:, 2 * d:]
        z = cg * u
        zp = prev_ref[:, d:2 * d] * prev_ref[:, 2 * d:] * (i > 0).astype(F32)
        z1, z2 = _shift_down(z, 1, zp), _shift_down(z, 2, zp)
        w0, w1, w2 = w_ref[0:1, :], w_ref[1:2, :], w_ref[2:3, :]
        y = w0 * z2 + w1 * z1 + w2 * z
        dov = do_ref[...]
        dy = dov * b
        dyn = don_ref[...] * nxt_ref[:, :d] * (i < n_tiles - 1).astype(F32)
        dz = w2 * dy + w1 * _shift_up(dy, 1, dyn) + w0 * _shift_up(dy, 2, dyn)
        o_ref[:, :d] = (dov * y).astype(BF16)
        o_ref[:, d:2 * d] = (dz * u).astype(BF16)
        o_ref[:, 2 * d:] = (dz * cg).astype(BF16)
        dw = jnp.concatenate([jnp.sum(dy * z2, axis=0, keepdims=True), jnp.sum(dy * z1, axis=0, keepdims=True),
                              jnp.sum(dy * z, axis=0, keepdims=True)], axis=0)
        _accumulate(dw_ref, dw, i)

    return pl.pallas_call(
        body, name=name,
        out_shape=(jax.ShapeDtypeStruct((t, d3), BF16), jax.ShapeDtypeStruct((3, d), F32)), grid=(n_tiles,),
        in_specs=[cur, prev, nxt, dcur, dnxt, _const_spec((3, d))],
        out_specs=(_row_spec(tm, d3), _const_spec((3, d))),
        compiler_params=_params(("arbitrary",)),
    )(bcu, bcu, bcu, dout, dout, conv_w)


def _rope_tables(positions, rot, lead, trail):
    inv_freq = ROPE_THETA ** (-jnp.arange(0, rot, 2, dtype=F32) / rot)
    ang = positions.astype(F32)[:, None] * inv_freq
    cos, sin = jnp.cos(ang), jnp.sin(ang)
    t = positions.shape[0]
    cos_full = jnp.concatenate([jnp.ones((t, lead), F32), cos, cos, jnp.ones((t, trail), F32)], axis=1)
    sin_full = jnp.concatenate([jnp.zeros((t, lead), F32), -sin, sin, jnp.zeros((t, trail), F32)], axis=1)
    return cos_full, sin_full


def _swap_halves(x, lead, rot):
    half = rot // 2
    rows, d = x.shape
    parts = []
    if lead:
        parts.append(jnp.zeros((rows, lead), x.dtype))
    parts += [x[:, lead + half:lead + rot], x[:, lead:lead + half]]
    if d - lead - rot:
        parts.append(jnp.zeros((rows, d - lead - rot), x.dtype))
    return jnp.concatenate(parts, axis=1)


def _head_fwd(x, g, cos, sin, lead, rot):
    n = x * _rstd(x) * g
    return n * cos + _swap_halves(n, lead, rot) * sin


def _head_bwd(x, g, cos, sin, dout, lead, rot):
    dn = dout * cos + _swap_halves(dout * sin, lead, rot)
    return _norm_bwd(x, g, dn)


A_Q_COLS = A_HEADS * A_HEAD_DIM
A_KV_COLS = A_KV_HEADS * A_HEAD_DIM
A_COLS = A_Q_COLS + 2 * A_KV_COLS
A_SCALE = A_HEAD_DIM ** -0.5


def _swa_prep_fwd(qkv, q_norm, k_norm, cos, sin, name):
    t = qkv.shape[0]
    tm = _div_tile(t, 256, 16)
    hd = A_HEAD_DIM

    def body(x_ref, gq_ref, gk_ref, cos_ref, sin_ref, o_ref):
        cosv, sinv = cos_ref[...], sin_ref[...]
        for h in range(A_HEADS + A_KV_HEADS):
            g = gq_ref[...] if h < A_HEADS else gk_ref[...]
            o_ref[:, h * hd:(h + 1) * hd] = _head_fwd(x_ref[:, h * hd:(h + 1) * hd], g, cosv, sinv, 0, A_ROT_DIM).astype(BF16)
        o_ref[:, A_Q_COLS + A_KV_COLS:] = x_ref[:, A_Q_COLS + A_KV_COLS:].astype(BF16)

    return pl.pallas_call(
        body, name=name, out_shape=jax.ShapeDtypeStruct((t, A_COLS), BF16), grid=(t // tm,),
        in_specs=[_row_spec(tm, A_COLS), _const_spec((1, hd)), _const_spec((1, hd)), _row_spec(tm, hd), _row_spec(tm, hd)],
        out_specs=_row_spec(tm, A_COLS), compiler_params=_params(("parallel",)),
    )(qkv, q_norm, k_norm, cos, sin)


def _swa_prep_bwd(qkv, dqkv_r, q_norm, k_norm, cos, sin, name):
    t = qkv.shape[0]
    tm = _div_tile(t, 256, 16)
    hd = A_HEAD_DIM

    def body(x_ref, d_ref, gq_ref, gk_ref, cos_ref, sin_ref, o_ref, dgq_ref, dgk_ref):
        cosv, sinv = cos_ref[...], sin_ref[...]
        dgq = jnp.zeros((1, hd), F32)
        dgk = jnp.zeros((1, hd), F32)
        for h in range(A_HEADS + A_KV_HEADS):
            sl = slice(h * hd, (h + 1) * hd)
            g = gq_ref[...] if h < A_HEADS else gk_ref[...]
            dx, dg = _head_bwd(x_ref[:, sl], g, cosv, sinv, d_ref[:, sl], 0, A_ROT_DIM)
            o_ref[:, sl] = dx.astype(BF16)
            if h < A_HEADS:
                dgq = dgq + dg
            else:
                dgk = dgk + dg
        o_ref[:, A_Q_COLS + A_KV_COLS:] = d_ref[:, A_Q_COLS + A_KV_COLS:].astype(BF16)
        _accumulate(dgq_ref, dgq, pl.program_id(0))
        _accumulate(dgk_ref, dgk, pl.program_id(0))

    return pl.pallas_call(
        body, name=name,
        out_shape=(jax.ShapeDtypeStruct((t, A_COLS), BF16), jax.ShapeDtypeStruct((1, hd), F32),
                   jax.ShapeDtypeStruct((1, hd), F32)),
        grid=(t // tm,),
        in_specs=[_row_spec(tm, A_COLS), _row_spec(tm, A_COLS), _const_spec((1, hd)), _const_spec((1, hd)),
                  _row_spec(tm, hd), _row_spec(tm, hd)],
        out_specs=(_row_spec(tm, A_COLS), _const_spec((1, hd)), _const_spec((1, hd))),
        compiler_params=_params(("arbitrary",)),
    )(qkv, dqkv_r, q_norm, k_norm, cos, sin)


def _group_rows(ref, k, width=A_HEAD_DIM, base=0):
    return jnp.concatenate([ref[:, base + (A_GROUP * k + g) * width:base + (A_GROUP * k + g + 1) * width]
                            for g in range(A_GROUP)], axis=0)


def _group_column(ref, k, rows):
    cols = []
    for g in range(A_GROUP):
        h = A_GROUP * k + g
        col = ref[:, h:h + 1]
        cols.append(jnp.broadcast_to(col, (rows, 1)) if col.shape[0] == 1 else col)
    return jnp.concatenate(cols, axis=0)


def _swa_fwd(qkv_r, sinks, name):
    t = qkv_r.shape[0]
    blk = A_WINDOW
    nb = t // blk
    hd = A_HEAD_DIM
    kv_block = A_Q_COLS // (2 * A_KV_COLS)

    def body(q_ref, kvc_ref, kvp_ref, s_ref, o_ref, lse_ref):
        n = pl.program_id(0)
        shape = (A_GROUP * blk, 2 * blk)
        qpos = lax.broadcasted_iota(jnp.int32, shape, 0) & (blk - 1)
        col = lax.broadcasted_iota(jnp.int32, shape, 1)
        delta = qpos + blk - col
        valid = (delta >= 0) & (delta < A_WINDOW) & ((col >= blk) | (n > 0))
        for k in range(A_KV_HEADS):
            qg = _group_rows(q_ref, k)
            kw = jnp.concatenate([kvp_ref[:, k * hd:(k + 1) * hd], kvc_ref[:, k * hd:(k + 1) * hd]], axis=0)
            vw = jnp.concatenate([kvp_ref[:, A_KV_COLS + k * hd:A_KV_COLS + (k + 1) * hd],
                                  kvc_ref[:, A_KV_COLS + k * hd:A_KV_COLS + (k + 1) * hd]], axis=0)
            s = lax.dot_general(qg, kw, (((1,), (1,)), ((), ())), preferred_element_type=F32) * A_SCALE
            s = jnp.where(valid, s, NEG)
            sink = _group_column(s_ref, k, blk)
            m = jnp.maximum(jnp.max(s, axis=-1, keepdims=True), sink)
            p = jnp.exp(s - m)
            denom = jnp.sum(p, axis=-1, keepdims=True) + jnp.exp(sink - m)
            o = jnp.dot(p.astype(BF16), vw, preferred_element_type=F32) / denom
            lse = m + jnp.log(denom)
            for g in range(A_GROUP):
                h = A_GROUP * k + g
                o_ref[:, h * hd:(h + 1) * hd] = o[g * blk:(g + 1) * blk].astype(BF16)
                lse_ref[:, h:h + 1] = lse[g * blk:(g + 1) * blk]

    return pl.pallas_call(
        body, name=name,
        out_shape=(jax.ShapeDtypeStruct((t, A_Q_COLS), BF16), jax.ShapeDtypeStruct((t, A_HEADS), F32)), grid=(nb,),
        in_specs=[pl.BlockSpec((blk, A_Q_COLS), lambda n: (n, 0)),
                  pl.BlockSpec((blk, 2 * A_KV_COLS), lambda n: (n, kv_block)),
                  pl.BlockSpec((blk, 2 * A_KV_COLS), lambda n: (jnp.maximum(n - 1, 0), kv_block)),
                  _const_spec((1, A_HEADS))],
        out_specs=(pl.BlockSpec((blk, A_Q_COLS), lambda n: (n, 0)), pl.BlockSpec((blk, A_HEADS), lambda n: (n, 0))),
        compiler_params=_params(("parallel",)),
    )(qkv_r, qkv_r, qkv_r, sinks)


def _swa_bwd(qkv_r, o, lse, do, sinks, name):
    t = qkv_r.shape[0]
    blk = A_WINDOW
    nb = t // blk
    hd = A_HEAD_DIM
    kv_block = A_Q_COLS // (2 * A_KV_COLS)
    rows = A_GROUP * blk

    def nxt(n):
        return jnp.minimum(n + 1, nb - 1)

    def body(qc_ref, qn_ref, kvc_ref, kvp_ref, doc_ref, don_ref, oc_ref, on_ref, lc_ref, ln_ref, s_ref, dx_ref, ds_ref):
        n = pl.program_id(0)
        shape = (2 * rows, 2 * blk)
        row = lax.broadcasted_iota(jnp.int32, shape, 0)
        col = lax.broadcasted_iota(jnp.int32, shape, 1)
        is_next = row >= rows
        delta = jnp.where(is_next, blk, 0) + blk + (row & (blk - 1)) - col
        valid = ((delta >= 0) & (delta < A_WINDOW) & ((col >= blk) | (n > 0)) & (jnp.logical_not(is_next) | (n < nb - 1)))
        dsink_cols = []
        for k in range(A_KV_HEADS):
            qs = jnp.concatenate([_group_rows(qc_ref, k), _group_rows(qn_ref, k)], axis=0)
            dos = jnp.concatenate([_group_rows(doc_ref, k), _group_rows(don_ref, k)], axis=0)
            os_ = jnp.concatenate([_group_rows(oc_ref, k), _group_rows(on_ref, k)], axis=0).astype(F32)
            lses = jnp.concatenate([_group_column(lc_ref, k, blk), _group_column(ln_ref, k, blk)], axis=0)
            kw = jnp.concatenate([kvp_ref[:, k * hd:(k + 1) * hd], kvc_ref[:, k * hd:(k + 1) * hd]], axis=0)
            vw = jnp.concatenate([kvp_ref[:, A_KV_COLS + k * hd:A_KV_COLS + (k + 1) * hd],
                                  kvc_ref[:, A_KV_COLS + k * hd:A_KV_COLS + (k + 1) * hd]], axis=0)
            s = lax.dot_general(qs, kw, (((1,), (1,)), ((), ())), preferred_element_type=F32) * A_SCALE
            p = jnp.exp(jnp.where(valid, s - lses, NEG))
            dos_b = dos.astype(BF16)
            dp = lax.dot_general(dos_b, vw, (((1,), (1,)), ((), ())), preferred_element_type=F32)
            dlt = jnp.sum(dos * os_, axis=-1, keepdims=True)
            ds = p * (dp - dlt)
            dq = jnp.dot(ds[:rows].astype(BF16), kw, preferred_element_type=F32) * A_SCALE
            dk = lax.dot_general(ds[:, blk:].astype(BF16), qs, (((0,), (0,)), ((), ())), preferred_element_type=F32) * A_SCALE
            dv = lax.dot_general(p[:, blk:].astype(BF16), dos_b, (((0,), (0,)), ((), ())), preferred_element_type=F32)
            for g in range(A_GROUP):
                h = A_GROUP * k + g
                dx_ref[:, h * hd:(h + 1) * hd] = dq[g * blk:(g + 1) * blk]
            dx_ref[:, A_Q_COLS + k * hd:A_Q_COLS + (k + 1) * hd] = dk
            dx_ref[:, A_Q_COLS + A_KV_COLS + k * hd:A_Q_COLS + A_KV_COLS + (k + 1) * hd] = dv
            sink = _group_column(s_ref, k, blk)
            contrib = -jnp.exp(sink - lses[:rows]) * dlt[:rows]
            for g in range(A_GROUP):
                dsink_cols.append(jnp.sum(contrib[g * blk:(g + 1) * blk], axis=0, keepdims=True))
        _accumulate(ds_ref, jnp.concatenate(dsink_cols, axis=1), n)

    q_spec = lambda f: pl.BlockSpec((blk, A_Q_COLS), lambda n: (f(n), 0))
    l_spec = lambda f: pl.BlockSpec((blk, A_HEADS), lambda n: (f(n), 0))
    same = lambda n: n
    return pl.pallas_call(
        body, name=name,
        out_shape=(jax.ShapeDtypeStruct((t, A_COLS), F32), jax.ShapeDtypeStruct((1, A_HEADS), F32)), grid=(nb,),
        in_specs=[q_spec(same), q_spec(nxt),
                  pl.BlockSpec((blk, 2 * A_KV_COLS), lambda n: (n, kv_block)),
                  pl.BlockSpec((blk, 2 * A_KV_COLS), lambda n: (jnp.maximum(n - 1, 0), kv_block)),
                  q_spec(same), q_spec(nxt), q_spec(same), q_spec(nxt), l_spec(same), l_spec(nxt),
                  _const_spec((1, A_HEADS))],
        out_specs=(pl.BlockSpec((blk, A_COLS), lambda n: (n, 0)), _const_spec((1, A_HEADS))),
        compiler_params=_params(("arbitrary",)),
    )(qkv_r, qkv_r, qkv_r, qkv_r, do, do, o, o, lse, lse, sinks)


C_DOWN_COLS = C_Q_RANK + C_KV_RANK + C_ROPE
C_Q_COLS = C_HEADS * C_QK
C_KV_COLS = C_HEADS * (C_NOPE + C_V)
C_O_COLS = C_HEADS * C_V
C_PAD = LANES
C_SCALE = C_QK ** -0.5
LOG2E = 1.4426950408889634
LN2 = 0.6931471805599453
C_Q_SCALE = C_SCALE * LOG2E
C_PAIR = 2


def _mla_latent_fwd(down, q_a_norm, kv_a_norm, name):
    t = down.shape[0]
    tm = _div_tile(t, 512, 16)

    def body(x_ref, gq_ref, gk_ref, cq_ref, ckv_ref):
        cq, ckv = x_ref[:, :C_Q_RANK], x_ref[:, C_Q_RANK:C_Q_RANK + C_KV_RANK]
        cq_ref[...] = (cq * _rstd(cq) * gq_ref[...]).astype(BF16)
        ckv_ref[...] = (ckv * _rstd(ckv) * gk_ref[...]).astype(BF16)

    return pl.pallas_call(
        body, name=name,
        out_shape=(jax.ShapeDtypeStruct((t, C_Q_RANK), BF16), jax.ShapeDtypeStruct((t, C_KV_RANK), BF16)), grid=(t // tm,),
        in_specs=[_row_spec(tm, C_DOWN_COLS), _const_spec((1, C_Q_RANK)), _const_spec((1, C_KV_RANK))],
        out_specs=(_row_spec(tm, C_Q_RANK), _row_spec(tm, C_KV_RANK)), compiler_params=_params(("parallel",)),
    )(down, q_a_norm, kv_a_norm)


def _mla_latent_bwd(down, dcq, dckv, dkrope, q_a_norm, kv_a_norm, name):
    t = down.shape[0]
    tm = _div_tile(t, 512, 16)

    def body(x_ref, dcq_ref, dckv_ref, dkr_ref, gq_ref, gk_ref, o_ref, dgq_ref, dgk_ref):
        dq, dgq = _norm_bwd(x_ref[:, :C_Q_RANK], gq_ref[...], dcq_ref[...])
        dkv, dgk = _norm_bwd(x_ref[:, C_Q_RANK:C_Q_RANK + C_KV_RANK], gk_ref[...], dckv_ref[...])
        o_ref[...] = jnp.concatenate([dq, dkv, dkr_ref[...]], axis=1).astype(BF16)
        _accumulate(dgq_ref, dgq, pl.program_id(0))
        _accumulate(dgk_ref, dgk, pl.program_id(0))

    return pl.pallas_call(
        body, name=name,
        out_shape=(jax.ShapeDtypeStruct((t, C_DOWN_COLS), BF16), jax.ShapeDtypeStruct((1, C_Q_RANK), F32),
                   jax.ShapeDtypeStruct((1, C_KV_RANK), F32)),
        grid=(t // tm,),
        in_specs=[_row_spec(tm, C_DOWN_COLS), _row_spec(tm, C_Q_RANK), _row_spec(tm, C_KV_RANK), _row_spec(tm, C_ROPE),
                  _const_spec((1, C_Q_RANK)), _const_spec((1, C_KV_RANK))],
        out_specs=(_row_spec(tm, C_DOWN_COLS), _const_spec((1, C_Q_RANK)), _const_spec((1, C_KV_RANK))),
        compiler_params=_params(("arbitrary",)),
    )(down, dcq, dckv, dkrope, q_a_norm, kv_a_norm)


def _head_major_spec(tm, width):
    return pl.BlockSpec((C_HEADS, tm, width), lambda i: (0, i, 0))


def _mla_qk_fwd(qw, kvw, down, q_norm, k_norm, cos, sin, name):
    t = qw.shape[0]
    tm = _div_tile(t, 256, 16)
    kvd = C_NOPE + C_V

    def body(q_ref, kv_ref, dn_ref, gq_ref, gk_ref, cos_ref, sin_ref, qo_ref, ko_ref, vo_ref):
        cosv, sinv = cos_ref[...], sin_ref[...]
        k_rope = dn_ref[:, C_Q_RANK + C_KV_RANK:]
        pad = jnp.zeros((tm, C_PAD - C_QK), F32)
        one_then_zeros = (lax.broadcasted_iota(jnp.int32, (tm, C_PAD - C_V), 1) == 0).astype(F32)
        for h in range(C_HEADS):
            qh = _head_fwd(q_ref[:, h * C_QK:(h + 1) * C_QK], gq_ref[...], cosv, sinv, C_NOPE, C_ROPE)
            kx = jnp.concatenate([kv_ref[:, h * kvd:h * kvd + C_NOPE], k_rope], axis=1)
            kh = _head_fwd(kx, gk_ref[...], cosv, sinv, C_NOPE, C_ROPE)
            qo_ref[h] = jnp.concatenate([qh * C_Q_SCALE, pad], axis=1).astype(BF16)
            ko_ref[h] = jnp.concatenate([kh, pad], axis=1).astype(BF16)
            vo_ref[h] = jnp.concatenate([kv_ref[:, h * kvd + C_NOPE:(h + 1) * kvd], one_then_zeros], axis=1).astype(BF16)

    return pl.pallas_call(
        body, name=name,
        out_shape=(jax.ShapeDtypeStruct((C_HEADS, t, C_PAD), BF16), jax.ShapeDtypeStruct((C_HEADS, t, C_PAD), BF16),
                   jax.ShapeDtypeStruct((C_HEADS, t, C_PAD), BF16)),
        grid=(t // tm,),
        in_specs=[_row_spec(tm, C_Q_COLS), _row_spec(tm, C_KV_COLS), _row_spec(tm, C_DOWN_COLS), _const_spec((1, C_QK)),
                  _const_spec((1, C_QK)), _row_spec(tm, C_QK), _row_spec(tm, C_QK)],
        out_specs=(_head_major_spec(tm, C_PAD), _head_major_spec(tm, C_PAD), _head_major_spec(tm, C_PAD)),
        compiler_params=_params(("parallel",)),
    )(qw, kvw, down, q_norm, k_norm, cos, sin)


def _mla_qk_bwd(qw, kvw, down, dq, dk, dv, q_norm, k_norm, cos, sin, name):
    t = qw.shape[0]
    tm = _div_tile(t, 256, 16)
    kvd = C_NOPE + C_V

    def body(q_ref, kv_ref, dn_ref, dq_ref, dk_ref, dv_ref, gq_ref, gk_ref, cos_ref, sin_ref,
             dqw_ref, dkvw_ref, dkr_ref, dgq_ref, dgk_ref):
        cosv, sinv = cos_ref[...], sin_ref[...]
        k_rope = dn_ref[:, C_Q_RANK + C_KV_RANK:]
        dgq = jnp.zeros((1, C_QK), F32)
        dgk = jnp.zeros((1, C_QK), F32)
        dkr = jnp.zeros((tm, C_ROPE), F32)
        for h in range(C_HEADS):
            dxq, dg = _head_bwd(q_ref[:, h * C_QK:(h + 1) * C_QK], gq_ref[...], cosv, sinv, dq_ref[h][:, :C_QK], C_NOPE, C_ROPE)
            dgq = dgq + dg
            dqw_ref[:, h * C_QK:(h + 1) * C_QK] = dxq.astype(BF16)
            kx = jnp.concatenate([kv_ref[:, h * kvd:h * kvd + C_NOPE], k_rope], axis=1)
            dxk, dg = _head_bwd(kx, gk_ref[...], cosv, sinv, dk_ref[h][:, :C_QK], C_NOPE, C_ROPE)
            dgk = dgk + dg
            dkr = dkr + dxk[:, C_NOPE:]
            dkvw_ref[:, h * kvd:(h + 1) * kvd] = jnp.concatenate([dxk[:, :C_NOPE], dv_ref[h]], axis=1).astype(BF16)
        dkr_ref[...] = dkr
        _accumulate(dgq_ref, dgq, pl.program_id(0))
        _accumulate(dgk_ref, dgk, pl.program_id(0))

    return pl.pallas_call(
        body, name=name,
        out_shape=(jax.ShapeDtypeStruct((t, C_Q_COLS), BF16), jax.ShapeDtypeStruct((t, C_KV_COLS), BF16),
                   jax.ShapeDtypeStruct((t, C_ROPE), F32), jax.ShapeDtypeStruct((1, C_QK), F32),
                   jax.ShapeDtypeStruct((1, C_QK), F32)),
        grid=(t // tm,),
        in_specs=[_row_spec(tm, C_Q_COLS), _row_spec(tm, C_KV_COLS), _row_spec(tm, C_DOWN_COLS),
                  _head_major_spec(tm, C_PAD), _head_major_spec(tm, C_PAD), _head_major_spec(tm, C_V),
                  _const_spec((1, C_QK)), _const_spec((1, C_QK)), _row_spec(tm, C_QK), _row_spec(tm, C_QK)],
        out_specs=(_row_spec(tm, C_Q_COLS), _row_spec(tm, C_KV_COLS), _row_spec(tm, C_ROPE), _const_spec((1, C_QK)),
                   _const_spec((1, C_QK))),
        compiler_params=_params(("arbitrary",)),
    )(qw, kvw, down, dq, dk, dv, q_norm, k_norm, cos, sin)


def _causal_keep(rows, cols, row_offset=0, transposed=False):
    row = lax.broadcasted_iota(jnp.int32, (rows, cols), 0) + row_offset
    col = lax.broadcasted_iota(jnp.int32, (rows, cols), 1)
    return (row <= col) if transposed else (col <= row)


def _mla_fwd(q, k, v, name):
    _, t, _ = q.shape
    blk = min(MLA_FWD_BLOCK, t)
    nq = t // blk

    def body(q_ref, k_ref, v_ref, o_ref, lse_ref, m_sc, acc_sc):
        qi = pl.program_id(1)
        m_sc[...] = jnp.full_like(m_sc, NEG)
        acc_sc[...] = jnp.zeros_like(acc_sc)

        def step(ki, masked):
            rows = pl.ds(pl.multiple_of(ki * blk, blk), blk)
            for hh in range(C_PAIR):
                s = lax.dot_general(q_ref[hh], k_ref[hh, rows, :], (((1,), (1,)), ((), ())), preferred_element_type=F32)
                if masked:
                    s = jnp.where(_causal_keep(blk, blk), s, NEG)
                m_prev = m_sc[hh]
                m_new = jnp.maximum(m_prev, jnp.max(s, axis=-1, keepdims=True))
                p = jnp.exp2(s - m_new)
                acc_sc[hh] = jnp.exp2(m_prev - m_new) * acc_sc[hh] + jnp.dot(p.astype(BF16), v_ref[hh, rows, :],
                                                                                preferred_element_type=F32)
                m_sc[hh] = m_new

        def below_diagonal(ki, carry):
            step(ki, False)
            return carry

        lax.fori_loop(0, qi, below_diagonal, 0)
        step(qi, True)
        outs = []
        for hh in range(C_PAIR):
            denom = acc_sc[hh, :, C_V:C_V + 1]
            outs.append(acc_sc[hh, :, :C_V] / denom)
            lse_ref[hh] = m_sc[hh] + jnp.log(denom) * LOG2E
        o_ref[...] = jnp.concatenate(outs, axis=1).astype(BF16)

    whole = lambda hp, qi: (hp, 0, 0)
    return pl.pallas_call(
        body, name=name,
        out_shape=(jax.ShapeDtypeStruct((t, C_O_COLS), BF16), jax.ShapeDtypeStruct((C_HEADS, t, 1), F32)),
        grid=(C_HEADS // C_PAIR, nq),
        in_specs=[pl.BlockSpec((C_PAIR, blk, C_PAD), lambda hp, qi: (hp, qi, 0)),
                  pl.BlockSpec((C_PAIR, t, C_PAD), whole), pl.BlockSpec((C_PAIR, t, C_PAD), whole)],
        out_specs=(pl.BlockSpec((blk, C_PAIR * C_V), lambda hp, qi: (qi, hp)),
                   pl.BlockSpec((C_PAIR, blk, 1), lambda hp, qi: (hp, qi, 0))),
        scratch_shapes=[pltpu.VMEM((C_PAIR, blk, 1), F32), pltpu.VMEM((C_PAIR, blk, C_PAD), F32)],
        compiler_params=_params(("parallel", "arbitrary")),
    )(q, k, v)


def _mla_delta(do, o, name):
    t = do.shape[0]
    blk = min(MLA_BLOCK, t)

    def body(do_ref, o_ref, dlt_ref, dob_ref):
        for hh in range(C_PAIR):
            do_h = do_ref[:, hh * C_V:(hh + 1) * C_V]
            dlt_ref[hh] = jnp.sum(do_h * o_ref[:, hh * C_V:(hh + 1) * C_V].astype(F32), axis=-1, keepdims=True)
        dob_ref[...] = do_ref[...].astype(BF16)

    wide = pl.BlockSpec((blk, C_PAIR * C_V), lambda hp, i: (i, hp))
    return pl.pallas_call(
        body, name=name,
        out_shape=(jax.ShapeDtypeStruct((C_HEADS, t, 1), F32), jax.ShapeDtypeStruct(do.shape, BF16)),
        grid=(C_HEADS // C_PAIR, t // blk), in_specs=[wide, wide],
        out_specs=(pl.BlockSpec((C_PAIR, blk, 1), lambda hp, i: (hp, i, 0)), wide),
        compiler_params=_params(("parallel", "parallel")),
    )(do, o)


def _mla_bwd(q, k, v, do_b, lse_rows, dlt_rows, name):
    _, t, _ = q.shape
    blk = min(MLA_BLOCK, t)
    nq = t // blk

    def body(q_ref, k_ref, v_ref, do_ref, lse_ref, dlt_ref, dq_hbm, dk_ref, dv_ref, dq_sc, dk_sc, dv_sc, sem):
        hp, ki = pl.program_id(0), pl.program_id(1)

        @pl.when(ki == 0)
        def _():
            dq_sc[...] = jnp.zeros_like(dq_sc)

        dk_sc[...] = jnp.zeros_like(dk_sc)
        dv_sc[...] = jnp.zeros_like(dv_sc)

        def step(qi, masked):
            rows = pl.ds(pl.multiple_of(qi * blk, blk), blk)
            for hh in range(C_PAIR):
                qb = q_ref[hh, rows, :]
                dob = do_ref[rows, hh * C_V:(hh + 1) * C_V]
                s = lax.dot_general(k_ref[hh], qb, (((1,), (1,)), ((), ())), preferred_element_type=F32)
                if masked:
                    s = jnp.where(_causal_keep(blk, blk, transposed=True), s, NEG)
                p = jnp.exp2(s - lse_ref[hh, qi])
                dp = lax.dot_general(v_ref[hh, :, :C_V], dob, (((1,), (1,)), ((), ())), preferred_element_type=F32)
                ds = (p * (dp - dlt_ref[hh, qi])).astype(BF16)
                dv_sc[hh] += jnp.dot(p.astype(BF16), dob, preferred_element_type=F32)
                dk_sc[hh] += jnp.dot(ds, qb, preferred_element_type=F32)
                dq_sc[hh, rows, :] += lax.dot_general(ds, k_ref[hh], (((0,), (0,)), ((), ())), preferred_element_type=F32)

        def above_diagonal(qi, carry):
            step(qi, False)
            return carry

        step(ki, True)
        lax.fori_loop(ki + 1, nq, above_diagonal, 0)
        dk_ref[...] = dk_sc[...] * LN2
        dv_ref[...] = dv_sc[...]

        @pl.when(ki == nq - 1)
        def _():
            dq_sc[...] = dq_sc[...] * C_SCALE
            out = pltpu.make_async_copy(dq_sc, dq_hbm.at[pl.ds(hp * C_PAIR, C_PAIR)], sem)
            out.start()
            out.wait()

    once = pl.Buffered(1)
    whole = lambda hp, ki: (hp, 0, 0)
    whole4 = lambda hp, ki: (hp, 0, 0, 0)
    kmap = lambda hp, ki: (hp, ki, 0)
    return pl.pallas_call(
        body, name=name,
        out_shape=(jax.ShapeDtypeStruct((C_HEADS, t, C_PAD), F32), jax.ShapeDtypeStruct((C_HEADS, t, C_PAD), F32),
                   jax.ShapeDtypeStruct((C_HEADS, t, C_V), F32)),
        grid=(C_HEADS // C_PAIR, nq),
        in_specs=[pl.BlockSpec((C_PAIR, t, C_PAD), whole, pipeline_mode=once), pl.BlockSpec((C_PAIR, blk, C_PAD), kmap),
                  pl.BlockSpec((C_PAIR, blk, C_PAD), kmap),
                  pl.BlockSpec((t, C_PAIR * C_V), lambda hp, ki: (0, hp), pipeline_mode=once),
                  pl.BlockSpec((C_PAIR, nq, 1, blk), whole4, pipeline_mode=once),
                  pl.BlockSpec((C_PAIR, nq, 1, blk), whole4, pipeline_mode=once)],
        out_specs=(pl.BlockSpec(memory_space=pl.ANY), pl.BlockSpec((C_PAIR, blk, C_PAD), kmap),
                   pl.BlockSpec((C_PAIR, blk, C_V), kmap)),
        scratch_shapes=[pltpu.VMEM((C_PAIR, t, C_PAD), F32), pltpu.VMEM((C_PAIR, blk, C_PAD), F32),
                        pltpu.VMEM((C_PAIR, blk, C_V), F32), pltpu.SemaphoreType.DMA(())],
        compiler_params=_params(("arbitrary", "arbitrary")),
    )(q, k, v, do_b, lse_rows, dlt_rows)


def _adamw(parts, w, m, v, name):
    layers, rows, cols = w.shape
    tm = _div_tile(rows, 256, 16)

    def body(p_ref, w_ref, m_ref, v_ref, g_ref, d_ref, nm_ref, nv_ref):
        g = p_ref[0].astype(F32)
        for j in range(1, N_DEV):
            g = g + p_ref[j].astype(F32)
        nm = ADAM_B1 * m_ref[...] + (1.0 - ADAM_B1) * g
        nv = ADAM_B2 * v_ref[...] + (1.0 - ADAM_B2) * jnp.square(g)
        m_hat = nm / (1.0 - ADAM_B1 ** ADAM_STEP)
        v_hat = nv / (1.0 - ADAM_B2 ** ADAM_STEP)
        g_ref[...] = g
        d_ref[...] = -ADAM_LR * (m_hat / (jnp.sqrt(v_hat) + ADAM_EPS) + ADAM_WD * w_ref[...])
        nm_ref[...] = nm
        nv_ref[...] = nv

    spec = pl.BlockSpec((None, tm, cols), lambda l, i: (l, i, 0))
    return pl.pallas_call(
        body, name=name, out_shape=tuple(jax.ShapeDtypeStruct(w.shape, F32) for _ in range(4)),
        grid=(layers, rows // tm),
        in_specs=[pl.BlockSpec((None, N_DEV, tm, cols), lambda l, i: (l, 0, i, 0)), spec, spec, spec],
        out_specs=(spec, spec, spec, spec), compiler_params=_params(("parallel", "parallel")),
    )(parts, w, m, v)


def _join_shards(gathered, axis):
    moved = jnp.moveaxis(gathered, 1, axis)
    shape = list(moved.shape)
    shape[axis:axis + 2] = [shape[axis] * shape[axis + 1]]
    return moved.reshape(shape)


def _split_shards(full, axis):
    shape = list(full.shape)
    shape[axis:axis + 1] = [N_DEV, shape[axis] // N_DEV]
    return jnp.moveaxis(full.reshape(shape), axis, 1)


def _as_rows(shape):
    rest = tuple(shape[1:])
    return (shape[0], 1, rest[0]) if len(rest) == 1 else (shape[0],) + rest


def _forward_backward(x, positions, target, w, rep):
    cos_a, sin_a = _rope_tables(positions, A_ROT_DIM, 0, A_HEAD_DIM - A_ROT_DIM)
    cos_c, sin_c = _rope_tables(positions, C_ROPE, C_NOPE, 0)
    saved = []
    for i in range(DEPTH):
        kind, j = i % N_MIXERS, i // N_MIXERS
        s = {'x': x}
        h1 = _rmsnorm_fwd(x, rep['mix_norm'][i:i + 1], f"mix_norm_fwd_{i}")
        s['h1'] = h1
        if kind == 0:
            s['qkv'] = _matmul(h1, w['a_w_qkv'][j], 'nn', f"a_qkv_{i}")
            s['qkv_r'] = _swa_prep_fwd(s['qkv'], rep['a_q_norm'][j:j + 1], rep['a_k_norm'][j:j + 1], cos_a, sin_a,
                                       f"a_prep_fwd_{i}")
            s['o'], s['lse'] = _swa_fwd(s['qkv_r'], rep['a_sinks'][j:j + 1], f"a_attn_fwd_{i}")
            x1 = _matmul(s['o'], w['a_w_o'][j], 'nn', f"a_out_{i}", residual=x)
        elif kind == 1:
            s['bcu'] = _matmul(h1, w['b_w_in'][j], 'nn', f"b_in_{i}")
            s['by'] = _sconv_fwd(s['bcu'], w['b_conv_w'][j], f"b_conv_fwd_{i}")
            x1 = _matmul(s['by'], w['b_w_out'][j], 'nn', f"b_out_{i}", residual=x)
        else:
            s['down'] = _matmul(h1, w['c_w_down'][j], 'nn', f"c_down_{i}")
            s['cq'], s['ckv'] = _mla_latent_fwd(s['down'], w['c_q_a_norm'][j:j + 1], w['c_kv_a_norm'][j:j + 1],
                                                f"c_latent_fwd_{i}")
            s['qw'] = _matmul(s['cq'], w['c_w_q_up'][j], 'nn', f"c_q_up_{i}")
            s['kvw'] = _matmul(s['ckv'], w['c_w_kv_up'][j], 'nn', f"c_kv_up_{i}")
            s['q'], s['k'], s['v'] = _mla_qk_fwd(s['qw'], s['kvw'], s['down'], rep['c_q_norm'][j:j + 1],
                                                 rep['c_k_norm'][j:j + 1], cos_c, sin_c, f"c_prep_fwd_{i}")
            s['o'], s['lse'] = _mla_fwd(s['q'], s['k'], s['v'], f"c_attn_fwd_{i}")
            x1 = _matmul(s['o'], w['c_w_o'][j], 'nn', f"c_out_{i}", residual=x)
        s['x1'] = x1
        s['h2'] = _rmsnorm_fwd(x1, rep['ffn_norm'][i:i + 1], f"ffn_norm_fwd_{i}")
        s['gu'] = _matmul(s['h2'], w['f_w_gate_up'][i], 'nn', f"f_gate_up_{i}", out_dtype=BF16)
        s['act'] = _swiglu_fwd(s['gu'], f"f_act_fwd_{i}")
        x = _matmul(s['act'], w['f_w_down'][i], 'nn', f"f_down_{i}", residual=x1)
        saved.append(s)

    loss, dx = _loss_head(x, target, "loss_head")

    per_layer = {n: {} for n in WEIGHTS}
    for i in reversed(range(DEPTH)):
        kind, j = i % N_MIXERS, i // N_MIXERS
        s = saved[i]
        per_layer['f_w_down'][i] = _matmul(s['act'], dx, 'tn', f"f_down_dw_{i}", out_dtype=BF16)
        dact = _matmul(dx, w['f_w_down'][i], 'nt', f"f_down_dx_{i}", out_dtype=BF16)
        dgu = _swiglu_bwd(s['gu'], dact, f"f_act_bwd_{i}")
        per_layer['f_w_gate_up'][i] = _matmul(s['h2'], dgu, 'tn', f"f_gate_up_dw_{i}", out_dtype=BF16)
        dh2 = _matmul(dgu, w['f_w_gate_up'][i], 'nt', f"f_gate_up_dx_{i}")
        dx1, per_layer['ffn_norm'][i] = _rmsnorm_bwd(s['x1'], rep['ffn_norm'][i:i + 1], dh2, dx, f"ffn_norm_bwd_{i}")
        if kind == 0:
            per_layer['a_w_o'][j] = _matmul(s['o'], dx1, 'tn', f"a_out_dw_{i}", out_dtype=BF16)
            do = _matmul(dx1, w['a_w_o'][j], 'nt', f"a_out_dx_{i}")
            dqkv_r, per_layer['a_sinks'][j] = _swa_bwd(s['qkv_r'], s['o'], s['lse'], do, rep['a_sinks'][j:j + 1],
                                                       f"a_attn_bwd_{i}")
            dqkv, per_layer['a_q_norm'][j], per_layer['a_k_norm'][j] = _swa_prep_bwd(
                s['qkv'], dqkv_r, rep['a_q_norm'][j:j + 1], rep['a_k_norm'][j:j + 1], cos_a, sin_a, f"a_prep_bwd_{i}")
            per_layer['a_w_qkv'][j] = _matmul(s['h1'], dqkv, 'tn', f"a_qkv_dw_{i}", out_dtype=BF16)
            dh1 = _matmul(dqkv, w['a_w_qkv'][j], 'nt', f"a_qkv_dx_{i}")
        elif kind == 1:
            per_layer['b_w_out'][j] = _matmul(s['by'], dx1, 'tn', f"b_out_dw_{i}", out_dtype=BF16)
            dby = _matmul(dx1, w['b_w_out'][j], 'nt', f"b_out_dx_{i}")
            dbcu, per_layer['b_conv_w'][j] = _sconv_bwd(s['bcu'], dby, w['b_conv_w'][j], f"b_conv_bwd_{i}")
            per_layer['b_w_in'][j] = _matmul(s['h1'], dbcu, 'tn', f"b_in_dw_{i}", out_dtype=BF16)
            dh1 = _matmul(dbcu, w['b_w_in'][j], 'nt', f"b_in_dx_{i}")
        else:
            per_layer['c_w_o'][j] = _matmul(s['o'], dx1, 'tn', f"c_out_dw_{i}", out_dtype=BF16)
            do = _matmul(dx1, w['c_w_o'][j], 'nt', f"c_out_dx_{i}")
            dlt, do_b = _mla_delta(do, s['o'], f"c_attn_delta_{i}")
            blk = min(MLA_BLOCK, do.shape[0])
            as_rows = lambda col: col.reshape(C_HEADS, do.shape[0] // blk, 1, blk)
            dq, dk, dv = _mla_bwd(s['q'], s['k'], s['v'], do_b, as_rows(s['lse']), as_rows(dlt), f"c_attn_bwd_{i}")
            dqw, dkvw, dkrope, per_layer['c_q_norm'][j], per_layer['c_k_norm'][j] = _mla_qk_bwd(
                s['qw'], s['kvw'], s['down'], dq, dk, dv, rep['c_q_norm'][j:j + 1], rep['c_k_norm'][j:j + 1], cos_c, sin_c,
                f"c_prep_bwd_{i}")
            per_layer['c_w_q_up'][j] = _matmul(s['cq'], dqw, 'tn', f"c_q_up_dw_{i}", out_dtype=BF16)
            dcq = _matmul(dqw, w['c_w_q_up'][j], 'nt', f"c_q_up_dx_{i}")
            per_layer['c_w_kv_up'][j] = _matmul(s['ckv'], dkvw, 'tn', f"c_kv_up_dw_{i}", out_dtype=BF16)
            dckv = _matmul(dkvw, w['c_w_kv_up'][j], 'nt', f"c_kv_up_dx_{i}")
            ddown, per_layer['c_q_a_norm'][j], per_layer['c_kv_a_norm'][j] = _mla_latent_bwd(
                s['down'], dcq, dckv, dkrope, w['c_q_a_norm'][j:j + 1], w['c_kv_a_norm'][j:j + 1], f"c_latent_bwd_{i}")
            per_layer['c_w_down'][j] = _matmul(s['h1'], ddown, 'tn', f"c_down_dw_{i}", out_dtype=BF16)
            dh1 = _matmul(ddown, w['c_w_down'][j], 'nt', f"c_down_dx_{i}")
        dx, per_layer['mix_norm'][i] = _rmsnorm_bwd(s['x'], rep['mix_norm'][i:i + 1], dh1, dx1, f"mix_norm_bwd_{i}")

    grads = {}
    for n in WEIGHTS:
        stacked = jnp.stack([per_layer[n][j] for j in sorted(per_layer[n])])
        if n in ('mix_norm', 'ffn_norm', 'a_q_norm', 'a_k_norm', 'a_sinks', 'c_q_a_norm', 'c_kv_a_norm', 'c_q_norm', 'c_k_norm'):
            stacked = stacked.reshape(stacked.shape[0], stacked.shape[-1])
        if n in SHARD_AXIS:
            stacked = _split_shards(stacked, SHARD_AXIS[n])
        grads[n] = stacked
    return loss, dx, grads


def kernel(x, positions, mix_norm, ffn_norm, a_w_qkv, a_q_norm, a_k_norm, a_sinks, a_w_o, b_w_in, b_conv_w, b_w_out, c_w_down, c_q_a_norm, c_kv_a_norm, c_w_q_up, c_w_kv_up, c_q_norm, c_k_norm, c_w_o, f_w_gate_up, f_w_down, loss_target, m_mix_norm, m_ffn_norm, m_a_w_qkv, m_a_q_norm, m_a_k_norm, m_a_sinks, m_a_w_o, m_b_w_in, m_b_conv_w, m_b_w_out, m_c_w_down, m_c_q_a_norm, m_c_kv_a_norm, m_c_w_q_up, m_c_w_kv_up, m_c_q_norm, m_c_k_norm, m_c_w_o, m_f_w_gate_up, m_f_w_down, v_mix_norm, v_ffn_norm, v_a_w_qkv, v_a_q_norm, v_a_k_norm, v_a_sinks, v_a_w_o, v_b_w_in, v_b_conv_w, v_b_w_out, v_c_w_down, v_c_q_a_norm, v_c_kv_a_norm, v_c_w_q_up, v_c_w_kv_up, v_c_q_norm, v_c_k_norm, v_c_w_o, v_f_w_gate_up, v_f_w_down):
    local = dict(mix_norm=mix_norm, ffn_norm=ffn_norm, a_w_qkv=a_w_qkv, a_q_norm=a_q_norm, a_k_norm=a_k_norm, a_sinks=a_sinks, a_w_o=a_w_o, b_w_in=b_w_in, b_conv_w=b_conv_w, b_w_out=b_w_out, c_w_down=c_w_down, c_q_a_norm=c_q_a_norm, c_kv_a_norm=c_kv_a_norm, c_w_q_up=c_w_q_up, c_w_kv_up=c_w_kv_up, c_q_norm=c_q_norm, c_k_norm=c_k_norm, c_w_o=c_w_o, f_w_gate_up=f_w_gate_up, f_w_down=f_w_down)
    mom1 = dict(mix_norm=m_mix_norm, ffn_norm=m_ffn_norm, a_w_qkv=m_a_w_qkv, a_q_norm=m_a_q_norm, a_k_norm=m_a_k_norm, a_sinks=m_a_sinks, a_w_o=m_a_w_o, b_w_in=m_b_w_in, b_conv_w=m_b_conv_w, b_w_out=m_b_w_out, c_w_down=m_c_w_down, c_q_a_norm=m_c_q_a_norm, c_kv_a_norm=m_c_kv_a_norm, c_w_q_up=m_c_w_q_up, c_w_kv_up=m_c_w_kv_up, c_q_norm=m_c_q_norm, c_k_norm=m_c_k_norm, c_w_o=m_c_w_o, f_w_gate_up=m_f_w_gate_up, f_w_down=m_f_w_down)
    mom2 = dict(mix_norm=v_mix_norm, ffn_norm=v_ffn_norm, a_w_qkv=v_a_w_qkv, a_q_norm=v_a_q_norm, a_k_norm=v_a_k_norm, a_sinks=v_a_sinks, a_w_o=v_a_w_o, b_w_in=v_b_w_in, b_conv_w=v_b_conv_w, b_w_out=v_b_w_out, c_w_down=v_c_w_down, c_q_a_norm=v_c_q_a_norm, c_kv_a_norm=v_c_kv_a_norm, c_w_q_up=v_c_w_q_up, c_w_kv_up=v_c_w_kv_up, c_q_norm=v_c_q_norm, c_k_norm=v_c_k_norm, c_w_o=v_c_w_o, f_w_gate_up=v_f_w_gate_up, f_w_down=v_f_w_down)
    t, d = x.shape[1], x.shape[2]

    gathered = _gather_blocks([local[n].astype(BF16) if n in GATHER_BF16 else local[n] for n in SHARDED],
                              "gather_weights")
    full = {n: _join_shards(g, SHARD_AXIS[n]) for n, g in zip(SHARDED, gathered)}
    rep = {n: local[n] for n in REPLICATED}

    loss, grad_x, grads = _forward_backward(x.reshape(t, d), positions.reshape(t), loss_target.reshape(t, d), full, rep)

    out_g, out_d, out_m, out_v = {}, {}, {}, {}

    def update(names, parts):
        for n, part in zip(names, parts):
            shape = local[n].shape if n in SHARD_AXIS else (1,) + local[n].shape
            view = _as_rows(shape)
            results = _adamw(part.reshape(view[0], N_DEV, view[1], view[2]),
                             *[src[n].reshape(view) for src in (local, mom1, mom2)], name="adamw_" + n)
            for dst, res in zip((out_g, out_d, out_m, out_v), results):
                dst[n] = res.reshape(local[n].shape)

    update(SHARDED, _scatter_blocks([grads[n] for n in SHARDED], "scatter_gradients"))
    update(REPLICATED, _gather_blocks([grads[n].reshape((1,) + grads[n].shape) for n in REPLICATED],
                                      "gather_small_gradients"))

    loss = lax.psum(loss.reshape(()), MESH_AXES)
    outs = [loss, grad_x.reshape(1, t, d)]
    for res in (out_g, out_d, out_m, out_v):
        outs += [res[n] for n in WEIGHTS]
    return tuple(outs)
```

```python
import functools

import jax
import jax.numpy as jnp
from jax import lax
from jax.experimental import pallas as pl
from jax.experimental.pallas import tpu as pltpu

F32 = jnp.float32
BF16 = jnp.bfloat16

N_DEV = 8
MESH_AXES = ("x", "y", "c")

DEPTH = 4
N_MIXERS = 3
ROPE_THETA = 500000.0
EPS = 1e-6
A_HEADS, A_KV_HEADS, A_HEAD_DIM, A_ROT_DIM, A_WINDOW = 16, 4, 64, 16, 128
A_GROUP = A_HEADS // A_KV_HEADS
C_HEADS, C_NOPE, C_ROPE, C_V, C_Q_RANK, C_KV_RANK = 16, 64, 32, 64, 384, 256
C_QK = C_NOPE + C_ROPE
ADAM_LR, ADAM_B1, ADAM_B2, ADAM_EPS, ADAM_WD, ADAM_STEP = 0.001, 0.9, 0.999, 1e-08, 0.01, 10

VMEM_LIMIT_BYTES = 48 * 1024 * 1024
LANES = 128
NEG = -1e30
MLA_BLOCK = 512
MLA_FWD_BLOCK = 1024

WEIGHTS = ['mix_norm', 'ffn_norm', 'a_w_qkv', 'a_q_norm', 'a_k_norm', 'a_sinks', 'a_w_o', 'b_w_in', 'b_conv_w', 'b_w_out',
           'c_w_down', 'c_q_a_norm', 'c_kv_a_norm', 'c_w_q_up', 'c_w_kv_up', 'c_q_norm', 'c_k_norm', 'c_w_o', 'f_w_gate_up',
           'f_w_down']
SHARD_AXIS = {'a_w_qkv': 2, 'a_w_o': 1, 'b_w_in': 2, 'b_conv_w': 2, 'b_w_out': 1, 'c_w_down': 1, 'c_q_a_norm': 1,
              'c_kv_a_norm': 1, 'c_w_q_up': 2, 'c_w_kv_up': 2, 'c_w_o': 1, 'f_w_gate_up': 2, 'f_w_down': 1}
SHARDED = [n for n in WEIGHTS if n in SHARD_AXIS]
REPLICATED = [n for n in WEIGHTS if n not in SHARD_AXIS]
GATHER_F32 = ['b_conv_w', 'c_q_a_norm', 'c_kv_a_norm']
GATHER_BF16 = [n for n in SHARDED if n not in GATHER_F32]

def _params(semantics=None):
    return pltpu.CompilerParams(dimension_semantics=semantics, vmem_limit_bytes=VMEM_LIMIT_BYTES)


def _div_tile(n, cap, mult=LANES):
    best = None
    t = mult
    while t <= min(n, cap):
        if n % t == 0:
            best = t
        t += mult
    return n if best is None else best


ANY_SPEC = pl.BlockSpec(memory_space=pl.ANY)


class _Exchange:
    def __init__(self, arrays, scatter):
        self.arrays, self.scatter = list(arrays), scatter
        n = len(self.arrays)
        self.out_shapes = [jax.ShapeDtypeStruct(a.shape if scatter else (a.shape[0], N_DEV) + tuple(a.shape[1:]), a.dtype)
                           for a in self.arrays]
        self.scratch = [pltpu.SemaphoreType.DMA((n, N_DEV - 1)), pltpu.SemaphoreType.DMA((n, N_DEV - 1)),
                        pltpu.SemaphoreType.DMA((n,))]

    def _copies(self, src_refs, out_refs, sems):
        send_sems, recv_sems, local_sems = sems
        x, y, c = lax.axis_index("x"), lax.axis_index("y"), lax.axis_index("c")
        me_idx = 4 * x + 2 * y + c
        n = len(self.arrays)

        def remote(a, k, src, dst, to):
            return pltpu.make_async_remote_copy(src_ref=src, dst_ref=dst, send_sem=send_sems.at[a, k],
                                                recv_sem=recv_sems.at[a, k], device_id=to,
                                                device_id_type=pl.DeviceIdType.MESH)

        local, first, forwards, last = [], [], [], []
        if self.scatter:
            for a in range(n):
                local.append(pltpu.make_async_copy(src_refs[a].at[:, me_idx], out_refs[a].at[:, me_idx], local_sems.at[a]))
                for r in range(1, N_DEV):
                    px = 1 - x if (r >> 2) & 1 else x
                    py = 1 - y if (r >> 1) & 1 else y
                    pc = 1 - c if r & 1 else c
                    cp = remote(a, r - 1, src_refs[a].at[:, 4 * px + 2 * py + pc], out_refs[a].at[:, me_idx], (px, py, pc))
                    first.append(cp)
                    last.append(cp)
            return local, first, forwards, last
        me, sibling = (x, y, c), (x, y, 1 - c)
        chips = [(1 - x, y), (x, 1 - y), (1 - x, 1 - y)]

        def place(a, block):
            return out_refs[a].at[:, 4 * block[0] + 2 * block[1] + block[2]]

        for a in range(n):
            local.append(pltpu.make_async_copy(src_refs[a], place(a, me), local_sems.at[a]))
            first.append(remote(a, 0, src_refs[a], place(a, me), sibling))
            last.append(remote(a, 0, place(a, sibling), place(a, sibling), me))
            for j, chip in enumerate(chips):
                first.append(remote(a, 1 + j, src_refs[a], place(a, me), (*chip, c)))
                forwards.append((remote(a, 1 + j, place(a, (*chip, c)), place(a, (*chip, c)), me),
                                 remote(a, 4 + j, place(a, (*chip, c)), place(a, (*chip, c)), sibling)))
                last.append(remote(a, 4 + j, place(a, (*chip, 1 - c)), place(a, (*chip, 1 - c)), me))
        return local, first, forwards, last

    def start(self, src_refs, out_refs, sems):
        local, first, _, _ = self._copies(src_refs, out_refs, sems)
        for cp in local + first:
            cp.start()

    def finish(self, src_refs, out_refs, sems):
        local, first, forwards, last = self._copies(src_refs, out_refs, sems)
        for arrival, forward in forwards:
            arrival.wait_recv()
            forward.start()
        for cp in last:
            cp.wait_recv()
        for cp in first + [forward for _, forward in forwards]:
            cp.wait_send()
        for cp in local:
            cp.wait()


def _exchange_now(exchange, name):
    n = len(exchange.arrays)

    def body(*refs):
        exchange.start(refs[:n], refs[n:2 * n], refs[2 * n:])
        exchange.finish(refs[:n], refs[n:2 * n], refs[2 * n:])

    return pl.pallas_call(
        body, name=name, out_shape=tuple(exchange.out_shapes), in_specs=[ANY_SPEC] * n, out_specs=(ANY_SPEC,) * n,
        scratch_shapes=exchange.scratch,
    )(*exchange.arrays)


def _host_call(body, rider, name, out_shape, grid, in_specs, out_specs, operands, semantics):
    if rider is None:
        return pl.pallas_call(body, name=name, out_shape=tuple(out_shape), grid=grid, in_specs=list(in_specs),
                              out_specs=tuple(out_specs), compiler_params=_params(semantics))(*operands), None
    n_in, n_out, r = len(in_specs), len(out_shape), len(rider.arrays)

    def riding(*refs):
        ins, rider_in = refs[:n_in], refs[n_in:n_in + r]
        outs, rider_out = refs[n_in + r:n_in + r + n_out], refs[n_in + r + n_out:n_in + 2 * r + n_out]
        sems = refs[n_in + 2 * r + n_out:]
        step = pl.program_id(0)

        @pl.when(step == 0)
        def _():
            rider.start(rider_in, rider_out, sems)

        body(*ins, *outs)

        @pl.when(step == grid[0] - 1)
        def _():
            rider.finish(rider_in, rider_out, sems)

    results = pl.pallas_call(
        riding, name=name, out_shape=tuple(out_shape) + tuple(rider.out_shapes), grid=grid,
        in_specs=list(in_specs) + [ANY_SPEC] * r, out_specs=tuple(out_specs) + (ANY_SPEC,) * r,
        scratch_shapes=rider.scratch, compiler_params=_params(("arbitrary",)),
    )(*operands, *rider.arrays)
    return results[:n_out], results[n_out:]


def _matmul(a, b, mode, name, out_dtype=F32, residual=None):
    if mode == 'nn':
        (m, k), (k2, n) = a.shape, b.shape
    elif mode == 'nt':
        (m, k), (n, k2) = a.shape, b.shape
    else:
        (k, m), (k2, n) = a.shape, b.shape
    assert k == k2, (name, a.shape, b.shape, mode)
    if mode == 'tn':
        tm, tk = _div_tile(m, 1408), _div_tile(k, 512, 16)
    else:
        tm, tk = _div_tile(m, 1024, 16), _div_tile(k, 1536)
    tn = _div_tile(n, 1408)
    nk = k // tk
    dims = {'nn': (((1,), (0,)), ((), ())), 'nt': (((1,), (1,)), ((), ())), 'tn': (((0,), (0,)), ((), ()))}[mode]

    def product(a_ref, b_ref):
        return lax.dot_general(a_ref[...].astype(BF16), b_ref[...].astype(BF16), dims, preferred_element_type=F32)

    def finish(r, rest):
        if residual is not None:
            r = r + rest[0][...]
        rest[-1 if nk == 1 else -2][...] = r.astype(out_dtype)

    def body_single(a_ref, b_ref, *rest):
        finish(product(a_ref, b_ref), rest)

    def body_accumulate(a_ref, b_ref, *rest):
        acc = rest[-1]
        kk = pl.program_id(2)

        @pl.when(kk == 0)
        def _():
            acc[...] = jnp.zeros_like(acc)

        acc[...] += product(a_ref, b_ref)

        @pl.when(kk == nk - 1)
        def _():
            finish(acc[...], rest)

    a_spec = pl.BlockSpec((tk, tm), lambda i, j, kk: (kk, i)) if mode == 'tn' else pl.BlockSpec((tm, tk), lambda i, j, kk: (i, kk))
    b_spec = pl.BlockSpec((tn, tk), lambda i, j, kk: (j, kk)) if mode == 'nt' else pl.BlockSpec((tk, tn), lambda i, j, kk: (kk, j))
    o_spec = pl.BlockSpec((tm, tn), lambda i, j, kk: (i, j))
    in_specs, operands = [a_spec, b_spec], [a, b]
    if residual is not None:
        in_specs.append(o_spec)
        operands.append(residual)
    return pl.pallas_call(
        body_single if nk == 1 else body_accumulate, name=name, out_shape=jax.ShapeDtypeStruct((m, n), out_dtype),
        grid=(m // tm, n // tn, nk), in_specs=in_specs, out_specs=o_spec,
        scratch_shapes=[] if nk == 1 else [pltpu.VMEM((tm, tn), F32)],
        compiler_params=_params(("parallel", "parallel", "arbitrary")),
    )(*operands)


def _row_spec(tm, cols):
    return pl.BlockSpec((tm, cols), lambda i: (i, 0))


def _const_spec(shape):
    return pl.BlockSpec(shape, lambda i: tuple(0 for _ in shape))


def _accumulate(ref, value, step):
    @pl.when(step == 0)
    def _():
        ref[...] = value

    @pl.when(step > 0)
    def _():
        ref[...] += value


def _rstd(x):
    return lax.rsqrt(jnp.mean(x * x, axis=-1, keepdims=True) + EPS)


def _norm_bwd(x, g, dout):
    xn = x * _rstd(x)
    dg = jnp.sum(dout * xn, axis=0, keepdims=True)
    dxn = dout * g
    dx = _rstd(x) * (dxn - xn * jnp.mean(dxn * xn, axis=-1, keepdims=True))
    return dx, dg


def _rmsnorm_fwd(x, g, name):
    t, d = x.shape
    tm = _div_tile(t, 512, 16)

    def body(x_ref, g_ref, o_ref):
        xv = x_ref[...]
        o_ref[...] = (xv * _rstd(xv) * g_ref[...]).astype(BF16)

    return pl.pallas_call(
        body, name=name, out_shape=jax.ShapeDtypeStruct((t, d), BF16), grid=(t // tm,),
        in_specs=[_row_spec(tm, d), _const_spec((1, d))], out_specs=_row_spec(tm, d),
        compiler_params=_params(("parallel",)),
    )(x, g)


def _rmsnorm_bwd(x, g, dh, dres, name):
    t, d = x.shape
    tm = _div_tile(t, 512, 8)

    def body(x_ref, g_ref, dh_ref, dres_ref, dx_ref, dg_ref):
        dx, dg = _norm_bwd(x_ref[...], g_ref[...], dh_ref[...])
        dx_ref[...] = dres_ref[...] + dx
        _accumulate(dg_ref, dg, pl.program_id(0))

    return pl.pallas_call(
        body, name=name,
        out_shape=(jax.ShapeDtypeStruct((t, d), F32), jax.ShapeDtypeStruct((1, d), F32)), grid=(t // tm,),
        in_specs=[_row_spec(tm, d), _const_spec((1, d)), _row_spec(tm, d), _row_spec(tm, d)],
        out_specs=(_row_spec(tm, d), _const_spec((1, d))),
        compiler_params=_params(("arbitrary",)),
    )(x, g, dh, dres)


def _sigmoid(x):
    return 0.5 * jnp.tanh(0.5 * x) + 0.5


def _swiglu_fwd(gu, name):
    t, f2 = gu.shape
    f = f2 // 2
    tm = _div_tile(t, 512, 16)

    def body(gu_ref, o_ref):
        gate, up = gu_ref[:, :f].astype(F32), gu_ref[:, f:].astype(F32)
        o_ref[...] = (gate * _sigmoid(gate) * up).astype(BF16)

    return pl.pallas_call(
        body, name=name, out_shape=jax.ShapeDtypeStruct((t, f), BF16), grid=(t // tm,),
        in_specs=[_row_spec(tm, f2)], out_specs=_row_spec(tm, f),
        compiler_params=_params(("parallel",)),
    )(gu)


def _swiglu_bwd(gu, da, name):
    t, f2 = gu.shape
    f = f2 // 2
    tm = _div_tile(t, 512, 16)

    def body(gu_ref, da_ref, o_ref):
        gate, up, dav = gu_ref[:, :f].astype(F32), gu_ref[:, f:].astype(F32), da_ref[...].astype(F32)
        sig = _sigmoid(gate)
        o_ref[:, :f] = (dav * up * (sig * (1.0 + gate * (1.0 - sig)))).astype(BF16)
        o_ref[:, f:] = (dav * (gate * sig)).astype(BF16)

    return pl.pallas_call(
        body, name=name, out_shape=jax.ShapeDtypeStruct((t, f2), BF16), grid=(t // tm,),
        in_specs=[_row_spec(tm, f2), _row_spec(tm, f)], out_specs=_row_spec(tm, f2),
        compiler_params=_params(("parallel",)),
    )(gu, da)


def _loss_head(y, target, name):
    t, d = y.shape
    tm = _div_tile(t, 512, 8)

    def body(y_ref, t_ref, loss_ref, dy_ref):
        diff = y_ref[...] - t_ref[...]
        dy_ref[...] = diff * (1.0 / d)
        part = jnp.sum(jnp.sum(diff * diff, axis=1, keepdims=True), axis=0, keepdims=True) * (0.5 / d)
        _accumulate(loss_ref, part, pl.program_id(0))

    return pl.pallas_call(
        body, name=name,
        out_shape=(jax.ShapeDtypeStruct((1, 1), F32), jax.ShapeDtypeStruct((t, d), F32)), grid=(t // tm,),
        in_specs=[_row_spec(tm, d), _row_spec(tm, d)], out_specs=(_const_spec((1, 1)), _row_spec(tm, d)),
        compiler_params=_params(("arbitrary",)),
    )(y, target)


HALO = 8


def _shift_down(z, k, halo_rows):
    tm = z.shape[0]
    row = lax.broadcasted_iota(jnp.int32, z.shape, 0)
    out = pltpu.roll(z, k, 0)
    for j in range(k):
        out = jnp.where(row == j, halo_rows[HALO - k + j:HALO - k + j + 1, :], out)
    return out


def _shift_up(z, k, halo_rows):
    tm = z.shape[0]
    row = lax.broadcasted_iota(jnp.int32, z.shape, 0)
    out = pltpu.roll(z, tm - k, 0)
    for j in range(k):
        out = jnp.where(row == tm - k + j, halo_rows[j:j + 1, :], out)
    return out


def _sconv_specs(t, tm, cols):
    per = tm // HALO
    last = t // HALO - 1
    cur = pl.BlockSpec((tm, cols), lambda i: (i, 0))
    prev = pl.BlockSpec((HALO, cols), lambda i: (jnp.maximum(i * per - 1, 0), 0))
    nxt = pl.BlockSpec((HALO, cols), lambda i: (jnp.minimum((i + 1) * per, last), 0))
    return cur, prev, nxt


def _sconv_fwd(bcu, conv_w, name):
    t, d3 = bcu.shape
    d = d3 // 3
    tm = _div_tile(t, 256, 16)
    cur, prev, _ = _sconv_specs(t, tm, d3)

    def body(cur_ref, prev_ref, w_ref, o_ref):
        i = pl.program_id(0)
        z = cur_ref[:, d:2 * d] * cur_ref[:, 2 * d:]
        zp = prev_ref[:, d:2 * d] * prev_ref[:, 2 * d:] * (i > 0).astype(F32)
        y = w_ref[0:1, :] * _shift_down(z, 2, zp) + w_ref[1:2, :] * _shift_down(z, 1, zp) + w_ref[2:3, :] * z
        o_ref[...] = (cur_ref[:, :d] * y).astype(BF16)

    return pl.pallas_call(
        body, name=name, out_shape=jax.ShapeDtypeStruct((t, d), BF16), grid=(t // tm,),
        in_specs=[cur, prev, _const_spec((3, d))], out_specs=_row_spec(tm, d),
        compiler_params=_params(("parallel",)),
    )(bcu, bcu, conv_w)


def _sconv_bwd(bcu, dout, conv_w, name):
    t, d3 = bcu.shape
    d = d3 // 3
    tm = _div_tile(t, 256, 16)
    cur, prev, nxt = _sconv_specs(t, tm, d3)
    dcur, _, dnxt = _sconv_specs(t, tm, d)
    n_tiles = t // tm

    def body(cur_ref, prev_ref, nxt_ref, do_ref, don_ref, w_ref, o_ref, dw_ref):
        i = pl.program_id(0)
        b, cg, u = cur_ref[:, :d], cur_ref[:, d:2 * d], cur_ref[:, 2 * d:]
        z = cg * u
        zp = prev_ref[:, d:2 * d] * prev_ref[:, 2 * d:] * (i > 0).astype(F32)
        z1, z2 = _shift_down(z, 1, zp), _shift_down(z, 2, zp)
        w0, w1, w2 = w_ref[0:1, :], w_ref[1:2, :], w_ref[2:3, :]
        y = w0 * z2 + w1 * z1 + w2 * z
        dov = do_ref[...]
        dy = dov * b
        dyn = don_ref[...] * nxt_ref[:, :d] * (i < n_tiles - 1).astype(F32)
        dz = w2 * dy + w1 * _shift_up(dy, 1, dyn) + w0 * _shift_up(dy, 2, dyn)
        o_ref[:, :d] = (dov * y).astype(BF16)
        o_ref[:, d:2 * d] = (dz * u).astype(BF16)
        o_ref[:, 2 * d:] = (dz * cg).astype(BF16)
        dw = jnp.concatenate([jnp.sum(dy * z2, axis=0, keepdims=True), jnp.sum(dy * z1, axis=0, keepdims=True),
                              jnp.sum(dy * z, axis=0, keepdims=True)], axis=0)
        _accumulate(dw_ref, dw, i)

    return pl.pallas_call(
        body, name=name,
        out_shape=(jax.ShapeDtypeStruct((t, d3), BF16), jax.ShapeDtypeStruct((3, d), F32)), grid=(n_tiles,),
        in_specs=[cur, prev, nxt, dcur, dnxt, _const_spec((3, d))],
        out_specs=(_row_spec(tm, d3), _const_spec((3, d))),
        compiler_params=_params(("arbitrary",)),
    )(bcu, bcu, bcu, dout, dout, conv_w)


def _rope_tables(positions, rot, lead, trail):
    inv_freq = ROPE_THETA ** (-jnp.arange(0, rot, 2, dtype=F32) / rot)
    ang = positions.astype(F32)[:, None] * inv_freq
    cos, sin = jnp.cos(ang), jnp.sin(ang)
    t = positions.shape[0]
    cos_full = jnp.concatenate([jnp.ones((t, lead), F32), cos, cos, jnp.ones((t, trail), F32)], axis=1)
    sin_full = jnp.concatenate([jnp.zeros((t, lead), F32), -sin, sin, jnp.zeros((t, trail), F32)], axis=1)
    return cos_full, sin_full


def _swap_halves(x, lead, rot):
    half = rot // 2
    rows, d = x.shape
    parts = []
    if lead:
        parts.append(jnp.zeros((rows, lead), x.dtype))
    parts += [x[:, lead + half:lead + rot], x[:, lead:lead + half]]
    if d - lead - rot:
        parts.append(jnp.zeros((rows, d - lead - rot), x.dtype))
    return jnp.concatenate(parts, axis=1)


def _head_fwd(x, g, cos, sin, lead, rot):
    n = x * _rstd(x) * g
    return n * cos + _swap_halves(n, lead, rot) * sin


def _head_bwd(x, g, cos, sin, dout, lead, rot):
    dn = dout * cos + _swap_halves(dout * sin, lead, rot)
    return _norm_bwd(x, g, dn)


A_Q_COLS = A_HEADS * A_HEAD_DIM
A_KV_COLS = A_KV_HEADS * A_HEAD_DIM
A_COLS = A_Q_COLS + 2 * A_KV_COLS
A_SCALE = A_HEAD_DIM ** -0.5


def _swa_prep_fwd(qkv, q_norm, k_norm, cos, sin, name):
    t = qkv.shape[0]
    tm = _div_tile(t, 256, 16)
    hd = A_HEAD_DIM

    def body(x_ref, gq_ref, gk_ref, cos_ref, sin_ref, o_ref):
        cosv, sinv = cos_ref[...], sin_ref[...]
        for h in range(A_HEADS + A_KV_HEADS):
            g = gq_ref[...] if h < A_HEADS else gk_ref[...]
            o_ref[:, h * hd:(h + 1) * hd] = _head_fwd(x_ref[:, h * hd:(h + 1) * hd], g, cosv, sinv, 0, A_ROT_DIM).astype(BF16)
        o_ref[:, A_Q_COLS + A_KV_COLS:] = x_ref[:, A_Q_COLS + A_KV_COLS:].astype(BF16)

    return pl.pallas_call(
        body, name=name, out_shape=jax.ShapeDtypeStruct((t, A_COLS), BF16), grid=(t // tm,),
        in_specs=[_row_spec(tm, A_COLS), _const_spec((1, hd)), _const_spec((1, hd)), _row_spec(tm, hd), _row_spec(tm, hd)],
        out_specs=_row_spec(tm, A_COLS), compiler_params=_params(("parallel",)),
    )(qkv, q_norm, k_norm, cos, sin)


def _swa_prep_bwd(qkv, dqkv_r, q_norm, k_norm, cos, sin, name):
    t = qkv.shape[0]
    tm = _div_tile(t, 256, 16)
    hd = A_HEAD_DIM

    def body(x_ref, d_ref, gq_ref, gk_ref, cos_ref, sin_ref, o_ref, dgq_ref, dgk_ref):
        cosv, sinv = cos_ref[...], sin_ref[...]
        dgq = jnp.zeros((1, hd), F32)
        dgk = jnp.zeros((1, hd), F32)
        for h in range(A_HEADS + A_KV_HEADS):
            sl = slice(h * hd, (h + 1) * hd)
            g = gq_ref[...] if h < A_HEADS else gk_ref[...]
            dx, dg = _head_bwd(x_ref[:, sl], g, cosv, sinv, d_ref[:, sl], 0, A_ROT_DIM)
            o_ref[:, sl] = dx.astype(BF16)
            if h < A_HEADS:
                dgq = dgq + dg
            else:
                dgk = dgk + dg
        o_ref[:, A_Q_COLS + A_KV_COLS:] = d_ref[:, A_Q_COLS + A_KV_COLS:].astype(BF16)
        _accumulate(dgq_ref, dgq, pl.program_id(0))
        _accumulate(dgk_ref, dgk, pl.program_id(0))

    return pl.pallas_call(
        body, name=name,
        out_shape=(jax.ShapeDtypeStruct((t, A_COLS), BF16), jax.ShapeDtypeStruct((1, hd), F32),
                   jax.ShapeDtypeStruct((1, hd), F32)),
        grid=(t // tm,),
        in_specs=[_row_spec(tm, A_COLS), _row_spec(tm, A_COLS), _const_spec((1, hd)), _const_spec((1, hd)),
                  _row_spec(tm, hd), _row_spec(tm, hd)],
        out_specs=(_row_spec(tm, A_COLS), _const_spec((1, hd)), _const_spec((1, hd))),
        compiler_params=_params(("arbitrary",)),
    )(qkv, dqkv_r, q_norm, k_norm, cos, sin)


def _group_rows(ref, k, width=A_HEAD_DIM, base=0):
    return jnp.concatenate([ref[:, base + (A_GROUP * k + g) * width:base + (A_GROUP * k + g + 1) * width]
                            for g in range(A_GROUP)], axis=0)


def _group_column(ref, k, rows):
    cols = []
    for g in range(A_GROUP):
        h = A_GROUP * k + g
        col = ref[:, h:h + 1]
        cols.append(jnp.broadcast_to(col, (rows, 1)) if col.shape[0] == 1 else col)
    return jnp.concatenate(cols, axis=0)


def _swa_fwd(qkv_r, sinks, name, rider=None):
    t = qkv_r.shape[0]
    blk = A_WINDOW
    nb = t // blk
    hd = A_HEAD_DIM
    kv_block = A_Q_COLS // (2 * A_KV_COLS)

    def body(q_ref, kvc_ref, kvp_ref, s_ref, o_ref, lse_ref):
        n = pl.program_id(0)
        shape = (A_GROUP * blk, 2 * blk)
        qpos = lax.broadcasted_iota(jnp.int32, shape, 0) & (blk - 1)
        col = lax.broadcasted_iota(jnp.int32, shape, 1)
        delta = qpos + blk - col
        valid = (delta >= 0) & (delta < A_WINDOW) & ((col >= blk) | (n > 0))
        for k in range(A_KV_HEADS):
            qg = _group_rows(q_ref, k)
            kw = jnp.concatenate([kvp_ref[:, k * hd:(k + 1) * hd], kvc_ref[:, k * hd:(k + 1) * hd]], axis=0)
            vw = jnp.concatenate([kvp_ref[:, A_KV_COLS + k * hd:A_KV_COLS + (k + 1) * hd],
                                  kvc_ref[:, A_KV_COLS + k * hd:A_KV_COLS + (k + 1) * hd]], axis=0)
            s = lax.dot_general(qg, kw, (((1,), (1,)), ((), ())), preferred_element_type=F32) * A_SCALE
            s = jnp.where(valid, s, NEG)
            sink = _group_column(s_ref, k, blk)
            m = jnp.maximum(jnp.max(s, axis=-1, keepdims=True), sink)
            p = jnp.exp(s - m)
            denom = jnp.sum(p, axis=-1, keepdims=True) + jnp.exp(sink - m)
            o = jnp.dot(p.astype(BF16), vw, preferred_element_type=F32) / denom
            lse = m + jnp.log(denom)
            for g in range(A_GROUP):
                h = A_GROUP * k + g
                o_ref[:, h * hd:(h + 1) * hd] = o[g * blk:(g + 1) * blk].astype(BF16)
                lse_ref[:, h:h + 1] = lse[g * blk:(g + 1) * blk]

    return _host_call(
        body, rider, name,
        out_shape=(jax.ShapeDtypeStruct((t, A_Q_COLS), BF16), jax.ShapeDtypeStruct((t, A_HEADS), F32)), grid=(nb,),
        in_specs=[pl.BlockSpec((blk, A_Q_COLS), lambda n: (n, 0)),
                  pl.BlockSpec((blk, 2 * A_KV_COLS), lambda n: (n, kv_block)),
                  pl.BlockSpec((blk, 2 * A_KV_COLS), lambda n: (jnp.maximum(n - 1, 0), kv_block)),
                  _const_spec((1, A_HEADS))],
        out_specs=(pl.BlockSpec((blk, A_Q_COLS), lambda n: (n, 0)), pl.BlockSpec((blk, A_HEADS), lambda n: (n, 0))),
        operands=(qkv_r, qkv_r, qkv_r, sinks), semantics=("parallel",))


def _swa_bwd(qkv_r, o, lse, do, sinks, name, rider=None):
    t = qkv_r.shape[0]
    blk = A_WINDOW
    nb = t // blk
    hd = A_HEAD_DIM
    kv_block = A_Q_COLS // (2 * A_KV_COLS)
    rows = A_GROUP * blk

    def nxt(n):
        return jnp.minimum(n + 1, nb - 1)

    def body(qc_ref, qn_ref, kvc_ref, kvp_ref, doc_ref, don_ref, oc_ref, on_ref, lc_ref, ln_ref, s_ref, dx_ref, ds_ref):
        n = pl.program_id(0)
        shape = (2 * rows, 2 * blk)
        row = lax.broadcasted_iota(jnp.int32, shape, 0)
        col = lax.broadcasted_iota(jnp.int32, shape, 1)
        is_next = row >= rows
        delta = jnp.where(is_next, blk, 0) + blk + (row & (blk - 1)) - col
        valid = ((delta >= 0) & (delta < A_WINDOW) & ((col >= blk) | (n > 0)) & (jnp.logical_not(is_next) | (n < nb - 1)))
        dsink_cols = []
        for k in range(A_KV_HEADS):
            qs = jnp.concatenate([_group_rows(qc_ref, k), _group_rows(qn_ref, k)], axis=0)
            dos = jnp.concatenate([_group_rows(doc_ref, k), _group_rows(don_ref, k)], axis=0)
            os_ = jnp.concatenate([_group_rows(oc_ref, k), _group_rows(on_ref, k)], axis=0).astype(F32)
            lses = jnp.concatenate([_group_column(lc_ref, k, blk), _group_column(ln_ref, k, blk)], axis=0)
            kw = jnp.concatenate([kvp_ref[:, k * hd:(k + 1) * hd], kvc_ref[:, k * hd:(k + 1) * hd]], axis=0)
            vw = jnp.concatenate([kvp_ref[:, A_KV_COLS + k * hd:A_KV_COLS + (k + 1) * hd],
                                  kvc_ref[:, A_KV_COLS + k * hd:A_KV_COLS + (k + 1) * hd]], axis=0)
            s = lax.dot_general(qs, kw, (((1,), (1,)), ((), ())), preferred_element_type=F32) * A_SCALE
            p = jnp.exp(jnp.where(valid, s - lses, NEG))
            dos_b = dos.astype(BF16)
            dp = lax.dot_general(dos_b, vw, (((1,), (1,)), ((), ())), preferred_element_type=F32)
            dlt = jnp.sum(dos * os_, axis=-1, keepdims=True)
            ds = p * (dp - dlt)
            dq = jnp.dot(ds[:rows].astype(BF16), kw, preferred_element_type=F32) * A_SCALE
            dk = lax.dot_general(ds[:, blk:].astype(BF16), qs, (((0,), (0,)), ((), ())), preferred_element_type=F32) * A_SCALE
            dv = lax.dot_general(p[:, blk:].astype(BF16), dos_b, (((0,), (0,)), ((), ())), preferred_element_type=F32)
            for g in range(A_GROUP):
                h = A_GROUP * k + g
                dx_ref[:, h * hd:(h + 1) * hd] = dq[g * blk:(g + 1) * blk]
            dx_ref[:, A_Q_COLS + k * hd:A_Q_COLS + (k + 1) * hd] = dk
            dx_ref[:, A_Q_COLS + A_KV_COLS + k * hd:A_Q_COLS + A_KV_COLS + (k + 1) * hd] = dv
            sink = _group_column(s_ref, k, blk)
            contrib = -jnp.exp(sink - lses[:rows]) * dlt[:rows]
            for g in range(A_GROUP):
                dsink_cols.append(jnp.sum(contrib[g * blk:(g + 1) * blk], axis=0, keepdims=True))
        _accumulate(ds_ref, jnp.concatenate(dsink_cols, axis=1), n)

    q_spec = lambda f: pl.BlockSpec((blk, A_Q_COLS), lambda n: (f(n), 0))
    l_spec = lambda f: pl.BlockSpec((blk, A_HEADS), lambda n: (f(n), 0))
    same = lambda n: n
    return _host_call(
        body, rider, name,
        out_shape=(jax.ShapeDtypeStruct((t, A_COLS), F32), jax.ShapeDtypeStruct((1, A_HEADS), F32)), grid=(nb,),
        in_specs=[q_spec(same), q_spec(nxt),
                  pl.BlockSpec((blk, 2 * A_KV_COLS), lambda n: (n, kv_block)),
                  pl.BlockSpec((blk, 2 * A_KV_COLS), lambda n: (jnp.maximum(n - 1, 0), kv_block)),
                  q_spec(same), q_spec(nxt), q_spec(same), q_spec(nxt), l_spec(same), l_spec(nxt),
                  _const_spec((1, A_HEADS))],
        out_specs=(pl.BlockSpec((blk, A_COLS), lambda n: (n, 0)), _const_spec((1, A_HEADS))),
        operands=(qkv_r, qkv_r, qkv_r, qkv_r, do, do, o, o, lse, lse, sinks), semantics=("arbitrary",))


C_DOWN_COLS = C_Q_RANK + C_KV_RANK + C_ROPE
C_Q_COLS = C_HEADS * C_QK
C_KV_COLS = C_HEADS * (C_NOPE + C_V)
C_O_COLS = C_HEADS * C_V
C_PAD = LANES
C_SCALE = C_QK ** -0.5
LOG2E = 1.4426950408889634
LN2 = 0.6931471805599453
C_Q_SCALE = C_SCALE * LOG2E
C_PAIR = 2


def _mla_latent_fwd(down, q_a_norm, kv_a_norm, name):
    t = down.shape[0]
    tm = _div_tile(t, 512, 16)

    def body(x_ref, gq_ref, gk_ref, cq_ref, ckv_ref):
        cq, ckv = x_ref[:, :C_Q_RANK], x_ref[:, C_Q_RANK:C_Q_RANK + C_KV_RANK]
        cq_ref[...] = (cq * _rstd(cq) * gq_ref[...]).astype(BF16)
        ckv_ref[...] = (ckv * _rstd(ckv) * gk_ref[...]).astype(BF16)

    return pl.pallas_call(
        body, name=name,
        out_shape=(jax.ShapeDtypeStruct((t, C_Q_RANK), BF16), jax.ShapeDtypeStruct((t, C_KV_RANK), BF16)), grid=(t // tm,),
        in_specs=[_row_spec(tm, C_DOWN_COLS), _const_spec((1, C_Q_RANK)), _const_spec((1, C_KV_RANK))],
        out_specs=(_row_spec(tm, C_Q_RANK), _row_spec(tm, C_KV_RANK)), compiler_params=_params(("parallel",)),
    )(down, q_a_norm, kv_a_norm)


def _mla_latent_bwd(down, dcq, dckv, dkrope, q_a_norm, kv_a_norm, name):
    t = down.shape[0]
    tm = _div_tile(t, 512, 16)

    def body(x_ref, dcq_ref, dckv_ref, dkr_ref, gq_ref, gk_ref, o_ref, dgq_ref, dgk_ref):
        dq, dgq = _norm_bwd(x_ref[:, :C_Q_RANK], gq_ref[...], dcq_ref[...])
        dkv, dgk = _norm_bwd(x_ref[:, C_Q_RANK:C_Q_RANK + C_KV_RANK], gk_ref[...], dckv_ref[...])
        o_ref[...] = jnp.concatenate([dq, dkv, dkr_ref[...]], axis=1).astype(BF16)
        _accumulate(dgq_ref, dgq, pl.program_id(0))
        _accumulate(dgk_ref, dgk, pl.program_id(0))

    return pl.pallas_call(
        body, name=name,
        out_shape=(jax.ShapeDtypeStruct((t, C_DOWN_COLS), BF16), jax.ShapeDtypeStruct((1, C_Q_RANK), F32),
                   jax.ShapeDtypeStruct((1, C_KV_RANK), F32)),
        grid=(t // tm,),
        in_specs=[_row_spec(tm, C_DOWN_COLS), _row_spec(tm, C_Q_RANK), _row_spec(tm, C_KV_RANK), _row_spec(tm, C_ROPE),
                  _const_spec((1, C_Q_RANK)), _const_spec((1, C_KV_RANK))],
        out_specs=(_row_spec(tm, C_DOWN_COLS), _const_spec((1, C_Q_RANK)), _const_spec((1, C_KV_RANK))),
        compiler_params=_params(("arbitrary",)),
    )(down, dcq, dckv, dkrope, q_a_norm, kv_a_norm)


def _head_major_spec(tm, width):
    return pl.BlockSpec((C_HEADS, tm, width), lambda i: (0, i, 0))


def _mla_qk_fwd(qw, kvw, down, q_norm, k_norm, cos, sin, name):
    t = qw.shape[0]
    tm = _div_tile(t, 256, 16)
    kvd = C_NOPE + C_V

    def body(q_ref, kv_ref, dn_ref, gq_ref, gk_ref, cos_ref, sin_ref, qo_ref, ko_ref, vo_ref):
        cosv, sinv = cos_ref[...], sin_ref[...]
        k_rope = dn_ref[:, C_Q_RANK + C_KV_RANK:]
        pad = jnp.zeros((tm, C_PAD - C_QK), F32)
        one_then_zeros = (lax.broadcasted_iota(jnp.int32, (tm, C_PAD - C_V), 1) == 0).astype(F32)
        for h in range(C_HEADS):
            qh = _head_fwd(q_ref[:, h * C_QK:(h + 1) * C_QK], gq_ref[...], cosv, sinv, C_NOPE, C_ROPE)
            kx = jnp.concatenate([kv_ref[:, h * kvd:h * kvd + C_NOPE], k_rope], axis=1)
            kh = _head_fwd(kx, gk_ref[...], cosv, sinv, C_NOPE, C_ROPE)
            qo_ref[h] = jnp.concatenate([qh * C_Q_SCALE, pad], axis=1).astype(BF16)
            ko_ref[h] = jnp.concatenate([kh, pad], axis=1).astype(BF16)
            vo_ref[h] = jnp.concatenate([kv_ref[:, h * kvd + C_NOPE:(h + 1) * kvd], one_then_zeros], axis=1).astype(BF16)

    return pl.pallas_call(
        body, name=name,
        out_shape=(jax.ShapeDtypeStruct((C_HEADS, t, C_PAD), BF16), jax.ShapeDtypeStruct((C_HEADS, t, C_PAD), BF16),
                   jax.ShapeDtypeStruct((C_HEADS, t, C_PAD), BF16)),
        grid=(t // tm,),
        in_specs=[_row_spec(tm, C_Q_COLS), _row_spec(tm, C_KV_COLS), _row_spec(tm, C_DOWN_COLS), _const_spec((1, C_QK)),
                  _const_spec((1, C_QK)), _row_spec(tm, C_QK), _row_spec(tm, C_QK)],
        out_specs=(_head_major_spec(tm, C_PAD), _head_major_spec(tm, C_PAD), _head_major_spec(tm, C_PAD)),
        compiler_params=_params(("parallel",)),
    )(qw, kvw, down, q_norm, k_norm, cos, sin)


def _mla_qk_bwd(qw, kvw, down, dq, dk, dv, q_norm, k_norm, cos, sin, name, rider=None):
    t = qw.shape[0]
    tm = _div_tile(t, 256, 16)
    kvd = C_NOPE + C_V

    def body(q_ref, kv_ref, dn_ref, dq_ref, dk_ref, dv_ref, gq_ref, gk_ref, cos_ref, sin_ref,
             dqw_ref, dkvw_ref, dkr_ref, dgq_ref, dgk_ref):
        cosv, sinv = cos_ref[...], sin_ref[...]
        k_rope = dn_ref[:, C_Q_RANK + C_KV_RANK:]
        dgq = jnp.zeros((1, C_QK), F32)
        dgk = jnp.zeros((1, C_QK), F32)
        dkr = jnp.zeros((tm, C_ROPE), F32)
        for h in range(C_HEADS):
            dxq, dg = _head_bwd(q_ref[:, h * C_QK:(h + 1) * C_QK], gq_ref[...], cosv, sinv, dq_ref[h][:, :C_QK], C_NOPE, C_ROPE)
            dgq = dgq + dg
            dqw_ref[:, h * C_QK:(h + 1) * C_QK] = dxq.astype(BF16)
            kx = jnp.concatenate([kv_ref[:, h * kvd:h * kvd + C_NOPE], k_rope], axis=1)
            dxk, dg = _head_bwd(kx, gk_ref[...], cosv, sinv, dk_ref[h][:, :C_QK], C_NOPE, C_ROPE)
            dgk = dgk + dg
            dkr = dkr + dxk[:, C_NOPE:]
            dkvw_ref[:, h * kvd:(h + 1) * kvd] = jnp.concatenate([dxk[:, :C_NOPE], dv_ref[h]], axis=1).astype(BF16)
        dkr_ref[...] = dkr
        _accumulate(dgq_ref, dgq, pl.program_id(0))
        _accumulate(dgk_ref, dgk, pl.program_id(0))

    return _host_call(
        body, rider, name,
        out_shape=(jax.ShapeDtypeStruct((t, C_Q_COLS), BF16), jax.ShapeDtypeStruct((t, C_KV_COLS), BF16),
                   jax.ShapeDtypeStruct((t, C_ROPE), F32), jax.ShapeDtypeStruct((1, C_QK), F32),
                   jax.ShapeDtypeStruct((1, C_QK), F32)),
        grid=(t // tm,),
        in_specs=[_row_spec(tm, C_Q_COLS), _row_spec(tm, C_KV_COLS), _row_spec(tm, C_DOWN_COLS),
                  _head_major_spec(tm, C_PAD), _head_major_spec(tm, C_PAD), _head_major_spec(tm, C_V),
                  _const_spec((1, C_QK)), _const_spec((1, C_QK)), _row_spec(tm, C_QK), _row_spec(tm, C_QK)],
        out_specs=(_row_spec(tm, C_Q_COLS), _row_spec(tm, C_KV_COLS), _row_spec(tm, C_ROPE), _const_spec((1, C_QK)),
                   _const_spec((1, C_QK))),
        operands=(qw, kvw, down, dq, dk, dv, q_norm, k_norm, cos, sin), semantics=("arbitrary",))


def _causal_keep(rows, cols, row_offset=0, transposed=False):
    row = lax.broadcasted_iota(jnp.int32, (rows, cols), 0) + row_offset
    col = lax.broadcasted_iota(jnp.int32, (rows, cols), 1)
    return (row <= col) if transposed else (col <= row)


def _mla_fwd(q, k, v, name):
    _, t, _ = q.shape
    blk = min(MLA_FWD_BLOCK, t)
    nq = t // blk

    def body(q_ref, k_ref, v_ref, o_ref, lse_ref, m_sc, acc_sc):
        qi = pl.program_id(1)
        m_sc[...] = jnp.full_like(m_sc, NEG)
        acc_sc[...] = jnp.zeros_like(acc_sc)

        def step(ki, masked):
            rows = pl.ds(pl.multiple_of(ki * blk, blk), blk)
            for hh in range(C_PAIR):
                s = lax.dot_general(q_ref[hh], k_ref[hh, rows, :], (((1,), (1,)), ((), ())), preferred_element_type=F32)
                if masked:
                    s = jnp.where(_causal_keep(blk, blk), s, NEG)
                m_prev = m_sc[hh]
                m_new = jnp.maximum(m_prev, jnp.max(s, axis=-1, keepdims=True))
                p = jnp.exp2(s - m_new)
                acc_sc[hh] = jnp.exp2(m_prev - m_new) * acc_sc[hh] + jnp.dot(p.astype(BF16), v_ref[hh, rows, :],
                                                                                preferred_element_type=F32)
                m_sc[hh] = m_new

        def below_diagonal(ki, carry):
            step(ki, False)
            return carry

        lax.fori_loop(0, qi, below_diagonal, 0)
        step(qi, True)
        outs = []
        for hh in range(C_PAIR):
            denom = acc_sc[hh, :, C_V:C_V + 1]
            outs.append(acc_sc[hh, :, :C_V] / denom)
            lse_ref[hh] = m_sc[hh] + jnp.log(denom) * LOG2E
        o_ref[...] = jnp.concatenate(outs, axis=1).astype(BF16)

    whole = lambda hp, qi: (hp, 0, 0)
    return pl.pallas_call(
        body, name=name,
        out_shape=(jax.ShapeDtypeStruct((t, C_O_COLS), BF16), jax.ShapeDtypeStruct((C_HEADS, t, 1), F32)),
        grid=(C_HEADS // C_PAIR, nq),
        in_specs=[pl.BlockSpec((C_PAIR, blk, C_PAD), lambda hp, qi: (hp, qi, 0)),
                  pl.BlockSpec((C_PAIR, t, C_PAD), whole), pl.BlockSpec((C_PAIR, t, C_PAD), whole)],
        out_specs=(pl.BlockSpec((blk, C_PAIR * C_V), lambda hp, qi: (qi, hp)),
                   pl.BlockSpec((C_PAIR, blk, 1), lambda hp, qi: (hp, qi, 0))),
        scratch_shapes=[pltpu.VMEM((C_PAIR, blk, 1), F32), pltpu.VMEM((C_PAIR, blk, C_PAD), F32)],
        compiler_params=_params(("parallel", "arbitrary")),
    )(q, k, v)


def _mla_delta(do, o, name):
    t = do.shape[0]
    blk = min(MLA_BLOCK, t)

    def body(do_ref, o_ref, dlt_ref, dob_ref):
        for hh in range(C_PAIR):
            do_h = do_ref[:, hh * C_V:(hh + 1) * C_V]
            dlt_ref[hh] = jnp.sum(do_h * o_ref[:, hh * C_V:(hh + 1) * C_V].astype(F32), axis=-1, keepdims=True)
        dob_ref[...] = do_ref[...].astype(BF16)

    wide = pl.BlockSpec((blk, C_PAIR * C_V), lambda hp, i: (i, hp))
    return pl.pallas_call(
        body, name=name,
        out_shape=(jax.ShapeDtypeStruct((C_HEADS, t, 1), F32), jax.ShapeDtypeStruct(do.shape, BF16)),
        grid=(C_HEADS // C_PAIR, t // blk), in_specs=[wide, wide],
        out_specs=(pl.BlockSpec((C_PAIR, blk, 1), lambda hp, i: (hp, i, 0)), wide),
        compiler_params=_params(("parallel", "parallel")),
    )(do, o)


def _mla_bwd(q, k, v, do_b, lse_rows, dlt_rows, name):
    _, t, _ = q.shape
    blk = min(MLA_BLOCK, t)
    nq = t // blk

    def body(q_ref, k_ref, v_ref, do_ref, lse_ref, dlt_ref, dq_hbm, dk_ref, dv_ref, dq_sc, dk_sc, dv_sc, sem):
        hp, ki = pl.program_id(0), pl.program_id(1)

        @pl.when(ki == 0)
        def _():
            dq_sc[...] = jnp.zeros_like(dq_sc)

        dk_sc[...] = jnp.zeros_like(dk_sc)
        dv_sc[...] = jnp.zeros_like(dv_sc)

        def step(qi, masked):
            rows = pl.ds(pl.multiple_of(qi * blk, blk), blk)
            for hh in range(C_PAIR):
                qb = q_ref[hh, rows, :]
                dob = do_ref[rows, hh * C_V:(hh + 1) * C_V]
                s = lax.dot_general(k_ref[hh], qb, (((1,), (1,)), ((), ())), preferred_element_type=F32)
                if masked:
                    s = jnp.where(_causal_keep(blk, blk, transposed=True), s, NEG)
                p = jnp.exp2(s - lse_ref[hh, qi])
                dp = lax.dot_general(v_ref[hh, :, :C_V], dob, (((1,), (1,)), ((), ())), preferred_element_type=F32)
                ds = (p * (dp - dlt_ref[hh, qi])).astype(BF16)
                dv_sc[hh] += jnp.dot(p.astype(BF16), dob, preferred_element_type=F32)
                dk_sc[hh] += jnp.dot(ds, qb, preferred_element_type=F32)
                dq_sc[hh, rows, :] += lax.dot_general(ds, k_ref[hh], (((0,), (0,)), ((), ())), preferred_element_type=F32)

        def above_diagonal(qi, carry):
            step(qi, False)
            return carry

        step(ki, True)
        lax.fori_loop(ki + 1, nq, above_diagonal, 0)
        dk_ref[...] = dk_sc[...] * LN2
        dv_ref[...] = dv_sc[...]

        @pl.when(ki == nq - 1)
        def _():
            dq_sc[...] = dq_sc[...] * C_SCALE
            out = pltpu.make_async_copy(dq_sc, dq_hbm.at[pl.ds(hp * C_PAIR, C_PAIR)], sem)
            out.start()
            out.wait()

    once = pl.Buffered(1)
    whole = lambda hp, ki: (hp, 0, 0)
    whole4 = lambda hp, ki: (hp, 0, 0, 0)
    kmap = lambda hp, ki: (hp, ki, 0)
    return pl.pallas_call(
        body, name=name,
        out_shape=(jax.ShapeDtypeStruct((C_HEADS, t, C_PAD), F32), jax.ShapeDtypeStruct((C_HEADS, t, C_PAD), F32),
                   jax.ShapeDtypeStruct((C_HEADS, t, C_V), F32)),
        grid=(C_HEADS // C_PAIR, nq),
        in_specs=[pl.BlockSpec((C_PAIR, t, C_PAD), whole, pipeline_mode=once), pl.BlockSpec((C_PAIR, blk, C_PAD), kmap),
                  pl.BlockSpec((C_PAIR, blk, C_PAD), kmap),
                  pl.BlockSpec((t, C_PAIR * C_V), lambda hp, ki: (0, hp), pipeline_mode=once),
                  pl.BlockSpec((C_PAIR, nq, 1, blk), whole4, pipeline_mode=once),
                  pl.BlockSpec((C_PAIR, nq, 1, blk), whole4, pipeline_mode=once)],
        out_specs=(pl.BlockSpec(memory_space=pl.ANY), pl.BlockSpec((C_PAIR, blk, C_PAD), kmap),
                   pl.BlockSpec((C_PAIR, blk, C_V), kmap)),
        scratch_shapes=[pltpu.VMEM((C_PAIR, t, C_PAD), F32), pltpu.VMEM((C_PAIR, blk, C_PAD), F32),
                        pltpu.VMEM((C_PAIR, blk, C_V), F32), pltpu.SemaphoreType.DMA(())],
        compiler_params=_params(("arbitrary", "arbitrary")),
    )(q, k, v, do_b, lse_rows, dlt_rows)


def _adamw(parts, w, m, v, name):
    layers, rows, cols = w.shape
    tm = _div_tile(rows, 256, 16)

    def body(p_ref, w_ref, m_ref, v_ref, g_ref, d_ref, nm_ref, nv_ref):
        g = p_ref[0].astype(F32)
        for j in range(1, N_DEV):
            g = g + p_ref[j].astype(F32)
        nm = ADAM_B1 * m_ref[...] + (1.0 - ADAM_B1) * g
        nv = ADAM_B2 * v_ref[...] + (1.0 - ADAM_B2) * jnp.square(g)
        m_hat = nm / (1.0 - ADAM_B1 ** ADAM_STEP)
        v_hat = nv / (1.0 - ADAM_B2 ** ADAM_STEP)
        g_ref[...] = g
        d_ref[...] = -ADAM_LR * (m_hat / (jnp.sqrt(v_hat) + ADAM_EPS) + ADAM_WD * w_ref[...])
        nm_ref[...] = nm
        nv_ref[...] = nv

    spec = pl.BlockSpec((None, tm, cols), lambda l, i: (l, i, 0))
    return pl.pallas_call(
        body, name=name, out_shape=tuple(jax.ShapeDtypeStruct(w.shape, F32) for _ in range(4)),
        grid=(layers, rows // tm),
        in_specs=[pl.BlockSpec((None, N_DEV, tm, cols), lambda l, i: (l, 0, i, 0)), spec, spec, spec],
        out_specs=(spec, spec, spec, spec), compiler_params=_params(("parallel", "parallel")),
    )(parts, w, m, v)


def _join_shards(gathered, axis):
    moved = jnp.moveaxis(gathered, 1, axis)
    shape = list(moved.shape)
    shape[axis:axis + 2] = [shape[axis] * shape[axis + 1]]
    return moved.reshape(shape)


def _split_shards(full, axis):
    shape = list(full.shape)
    shape[axis:axis + 1] = [N_DEV, shape[axis] // N_DEV]
    return jnp.moveaxis(full.reshape(shape), axis, 1)


def _as_rows(shape):
    rest = tuple(shape[1:])
    return (shape[0], 1, rest[0]) if len(rest) == 1 else (shape[0],) + rest


MIXER_WEIGHTS = {0: ['a_w_qkv', 'a_w_o'], 1: ['b_w_in', 'b_conv_w', 'b_w_out'],
                 2: ['c_w_down', 'c_q_a_norm', 'c_kv_a_norm', 'c_w_q_up', 'c_w_kv_up', 'c_w_o']}


def _layer_units(i):
    return [(n, i // N_MIXERS) for n in MIXER_WEIGHTS[i % N_MIXERS]] + [('f_w_gate_up', i), ('f_w_down', i)]


def _forward_backward(x, positions, target, local, rep):
    def gather(units):
        return _Exchange([local[n][i:i + 1].astype(BF16) if n in GATHER_BF16 else local[n][i:i + 1] for n, i in units],
                         scatter=False)

    w = {n: {} for n in SHARDED}

    def arrived(units, gathered):
        for (n, i), g in zip(units, gathered):
            full = _join_shards(g, SHARD_AXIS[n])
            w[n][i] = full if full.ndim == 2 else full[0]

    all_units = [u for i in range(DEPTH) for u in _layer_units(i)]
    first_units = _layer_units(0) + [u for u in all_units if u[0] in GATHER_F32]
    later_units = [u for u in all_units if u not in first_units]
    arrived(first_units, _exchange_now(gather(first_units), "gather_first_weights"))

    cos_a, sin_a = _rope_tables(positions, A_ROT_DIM, 0, A_HEAD_DIM - A_ROT_DIM)
    cos_c, sin_c = _rope_tables(positions, C_ROPE, C_NOPE, 0)
    saved = []
    for i in range(DEPTH):
        kind, j = i % N_MIXERS, i // N_MIXERS
        s = {'x': x}
        h1 = _rmsnorm_fwd(x, rep['mix_norm'][i:i + 1], f"mix_norm_fwd_{i}")
        s['h1'] = h1
        if kind == 0:
            s['qkv'] = _matmul(h1, w['a_w_qkv'][j], 'nn', f"a_qkv_{i}")
            s['qkv_r'] = _swa_prep_fwd(s['qkv'], rep['a_q_norm'][j:j + 1], rep['a_k_norm'][j:j + 1], cos_a, sin_a,
                                       f"a_prep_fwd_{i}")
            (s['o'], s['lse']), gathered = _swa_fwd(s['qkv_r'], rep['a_sinks'][j:j + 1], f"a_attn_fwd_{i}",
                                                    rider=gather(later_units) if i == 0 else None)
            if i == 0:
                arrived(later_units, gathered)
            x1 = _matmul(s['o'], w['a_w_o'][j], 'nn', f"a_out_{i}", residual=x)
        elif kind == 1:
            s['bcu'] = _matmul(h1, w['b_w_in'][j], 'nn', f"b_in_{i}")
            s['by'] = _sconv_fwd(s['bcu'], w['b_conv_w'][j], f"b_conv_fwd_{i}")
            x1 = _matmul(s['by'], w['b_w_out'][j], 'nn', f"b_out_{i}", residual=x)
        else:
            s['down'] = _matmul(h1, w['c_w_down'][j], 'nn', f"c_down_{i}")
            s['cq'], s['ckv'] = _mla_latent_fwd(s['down'], w['c_q_a_norm'][j], w['c_kv_a_norm'][j],
                                                f"c_latent_fwd_{i}")
            s['qw'] = _matmul(s['cq'], w['c_w_q_up'][j], 'nn', f"c_q_up_{i}")
            s['kvw'] = _matmul(s['ckv'], w['c_w_kv_up'][j], 'nn', f"c_kv_up_{i}")
            s['q'], s['k'], s['v'] = _mla_qk_fwd(s['qw'], s['kvw'], s['down'], rep['c_q_norm'][j:j + 1],
                                                 rep['c_k_norm'][j:j + 1], cos_c, sin_c, f"c_prep_fwd_{i}")
            s['o'], s['lse'] = _mla_fwd(s['q'], s['k'], s['v'], f"c_attn_fwd_{i}")
            x1 = _matmul(s['o'], w['c_w_o'][j], 'nn', f"c_out_{i}", residual=x)
        s['x1'] = x1
        s['h2'] = _rmsnorm_fwd(x1, rep['ffn_norm'][i:i + 1], f"ffn_norm_fwd_{i}")
        s['gu'] = _matmul(s['h2'], w['f_w_gate_up'][i], 'nn', f"f_gate_up_{i}", out_dtype=BF16)
        s['act'] = _swiglu_fwd(s['gu'], f"f_act_fwd_{i}")
        x = _matmul(s['act'], w['f_w_down'][i], 'nn', f"f_down_{i}", residual=x1)
        saved.append(s)

    loss, dx = _loss_head(x, target, "loss_head")

    per_layer = {n: {} for n in WEIGHTS}
    received = {}
    sent = set()

    def ready():
        units = [(n, j) for n in SHARDED for j in sorted(per_layer[n]) if (n, j) not in sent]
        if not units:
            return None, units
        sent.update(units)
        blocks = []
        for n, j in units:
            g = per_layer[n][j]
            blocks.append(_split_shards(g if n in ('c_q_a_norm', 'c_kv_a_norm') else g[None], SHARD_AXIS[n]))
        return _Exchange(blocks, scatter=True), units

    for i in reversed(range(DEPTH)):
        kind, j = i % N_MIXERS, i // N_MIXERS
        s = saved[i]
        per_layer['f_w_down'][i] = _matmul(s['act'], dx, 'tn', f"f_down_dw_{i}", out_dtype=BF16)
        dact = _matmul(dx, w['f_w_down'][i], 'nt', f"f_down_dx_{i}", out_dtype=BF16)
        dgu = _swiglu_bwd(s['gu'], dact, f"f_act_bwd_{i}")
        per_layer['f_w_gate_up'][i] = _matmul(s['h2'], dgu, 'tn', f"f_gate_up_dw_{i}", out_dtype=BF16)
        dh2 = _matmul(dgu, w['f_w_gate_up'][i], 'nt', f"f_gate_up_dx_{i}")
        dx1, per_layer['ffn_norm'][i] = _rmsnorm_bwd(s['x1'], rep['ffn_norm'][i:i + 1], dh2, dx, f"ffn_norm_bwd_{i}")
        if kind == 0:
            per_layer['a_w_o'][j] = _matmul(s['o'], dx1, 'tn', f"a_out_dw_{i}", out_dtype=BF16)
            do = _matmul(dx1, w['a_w_o'][j], 'nt', f"a_out_dx_{i}")
            rider, units = ready()
            (dqkv_r, per_layer['a_sinks'][j]), parts = _swa_bwd(s['qkv_r'], s['o'], s['lse'], do, rep['a_sinks'][j:j + 1],
                                                                f"a_attn_bwd_{i}", rider=rider)
            received.update(zip(units, parts or ()))
            dqkv, per_layer['a_q_norm'][j], per_layer['a_k_norm'][j] = _swa_prep_bwd(
                s['qkv'], dqkv_r, rep['a_q_norm'][j:j + 1], rep['a_k_norm'][j:j + 1], cos_a, sin_a, f"a_prep_bwd_{i}")
            per_layer['a_w_qkv'][j] = _matmul(s['h1'], dqkv, 'tn', f"a_qkv_dw_{i}", out_dtype=BF16)
            dh1 = _matmul(dqkv, w['a_w_qkv'][j], 'nt', f"a_qkv_dx_{i}")
        elif kind == 1:
            per_layer['b_w_out'][j] = _matmul(s['by'], dx1, 'tn', f"b_out_dw_{i}", out_dtype=BF16)
            dby = _matmul(dx1, w['b_w_out'][j], 'nt', f"b_out_dx_{i}")
            dbcu, per_layer['b_conv_w'][j] = _sconv_bwd(s['bcu'], dby, w['b_conv_w'][j], f"b_conv_bwd_{i}")
            per_layer['b_w_in'][j] = _matmul(s['h1'], dbcu, 'tn', f"b_in_dw_{i}", out_dtype=BF16)
            dh1 = _matmul(dbcu, w['b_w_in'][j], 'nt', f"b_in_dx_{i}")
        else:
            per_layer['c_w_o'][j] = _matmul(s['o'], dx1, 'tn', f"c_out_dw_{i}", out_dtype=BF16)
            do = _matmul(dx1, w['c_w_o'][j], 'nt', f"c_out_dx_{i}")
            dlt, do_b = _mla_delta(do, s['o'], f"c_attn_delta_{i}")
            blk = min(MLA_BLOCK, do.shape[0])
            as_rows = lambda col: col.reshape(C_HEADS, do.shape[0] // blk, 1, blk)
            dq, dk, dv = _mla_bwd(s['q'], s['k'], s['v'], do_b, as_rows(s['lse']), as_rows(dlt), f"c_attn_bwd_{i}")
            rider, units = ready()
            (dqw, dkvw, dkrope, per_layer['c_q_norm'][j], per_layer['c_k_norm'][j]), parts = _mla_qk_bwd(
                s['qw'], s['kvw'], s['down'], dq, dk, dv, rep['c_q_norm'][j:j + 1], rep['c_k_norm'][j:j + 1], cos_c, sin_c,
                f"c_prep_bwd_{i}", rider=rider)
            received.update(zip(units, parts or ()))
            per_layer['c_w_q_up'][j] = _matmul(s['cq'], dqw, 'tn', f"c_q_up_dw_{i}", out_dtype=BF16)
            dcq = _matmul(dqw, w['c_w_q_up'][j], 'nt', f"c_q_up_dx_{i}")
            per_layer['c_w_kv_up'][j] = _matmul(s['ckv'], dkvw, 'tn', f"c_kv_up_dw_{i}", out_dtype=BF16)
            dckv = _matmul(dkvw, w['c_w_kv_up'][j], 'nt', f"c_kv_up_dx_{i}")
            ddown, per_layer['c_q_a_norm'][j], per_layer['c_kv_a_norm'][j] = _mla_latent_bwd(
                s['down'], dcq, dckv, dkrope, w['c_q_a_norm'][j], w['c_kv_a_norm'][j], f"c_latent_bwd_{i}")
            per_layer['c_w_down'][j] = _matmul(s['h1'], ddown, 'tn', f"c_down_dw_{i}", out_dtype=BF16)
            dh1 = _matmul(ddown, w['c_w_down'][j], 'nt', f"c_down_dx_{i}")
        dx, per_layer['mix_norm'][i] = _rmsnorm_bwd(s['x'], rep['mix_norm'][i:i + 1], dh1, dx1, f"mix_norm_bwd_{i}")

    last, units = ready()
    received.update(zip(units, _exchange_now(last, "scatter_last_gradients")))
    parts = {n: jnp.concatenate([received[(n, j)] for j in sorted(per_layer[n])], axis=0) for n in SHARDED}
    small = {}
    for n in REPLICATED:
        stacked = jnp.stack([per_layer[n][j] for j in sorted(per_layer[n])])
        small[n] = stacked.reshape(stacked.shape[0], stacked.shape[-1])
    return loss, dx, parts, small


def kernel(x, positions, mix_norm, ffn_norm, a_w_qkv, a_q_norm, a_k_norm, a_sinks, a_w_o, b_w_in, b_conv_w, b_w_out, c_w_down, c_q_a_norm, c_kv_a_norm, c_w_q_up, c_w_kv_up, c_q_norm, c_k_norm, c_w_o, f_w_gate_up, f_w_down, loss_target, m_mix_norm, m_ffn_norm, m_a_w_qkv, m_a_q_norm, m_a_k_norm, m_a_sinks, m_a_w_o, m_b_w_in, m_b_conv_w, m_b_w_out, m_c_w_down, m_c_q_a_norm, m_c_kv_a_norm, m_c_w_q_up, m_c_w_kv_up, m_c_q_norm, m_c_k_norm, m_c_w_o, m_f_w_gate_up, m_f_w_down, v_mix_norm, v_ffn_norm, v_a_w_qkv, v_a_q_norm, v_a_k_norm, v_a_sinks, v_a_w_o, v_b_w_in, v_b_conv_w, v_b_w_out, v_c_w_down, v_c_q_a_norm, v_c_kv_a_norm, v_c_w_q_up, v_c_w_kv_up, v_c_q_norm, v_c_k_norm, v_c_w_o, v_f_w_gate_up, v_f_w_down):
    local = dict(mix_norm=mix_norm, ffn_norm=ffn_norm, a_w_qkv=a_w_qkv, a_q_norm=a_q_norm, a_k_norm=a_k_norm, a_sinks=a_sinks, a_w_o=a_w_o, b_w_in=b_w_in, b_conv_w=b_conv_w, b_w_out=b_w_out, c_w_down=c_w_down, c_q_a_norm=c_q_a_norm, c_kv_a_norm=c_kv_a_norm, c_w_q_up=c_w_q_up, c_w_kv_up=c_w_kv_up, c_q_norm=c_q_norm, c_k_norm=c_k_norm, c_w_o=c_w_o, f_w_gate_up=f_w_gate_up, f_w_down=f_w_down)
    mom1 = dict(mix_norm=m_mix_norm, ffn_norm=m_ffn_norm, a_w_qkv=m_a_w_qkv, a_q_norm=m_a_q_norm, a_k_norm=m_a_k_norm, a_sinks=m_a_sinks, a_w_o=m_a_w_o, b_w_in=m_b_w_in, b_conv_w=m_b_conv_w, b_w_out=m_b_w_out, c_w_down=m_c_w_down, c_q_a_norm=m_c_q_a_norm, c_kv_a_norm=m_c_kv_a_norm, c_w_q_up=m_c_w_q_up, c_w_kv_up=m_c_w_kv_up, c_q_norm=m_c_q_norm, c_k_norm=m_c_k_norm, c_w_o=m_c_w_o, f_w_gate_up=m_f_w_gate_up, f_w_down=m_f_w_down)
    mom2 = dict(mix_norm=v_mix_norm, ffn_norm=v_ffn_norm, a_w_qkv=v_a_w_qkv, a_q_norm=v_a_q_norm, a_k_norm=v_a_k_norm, a_sinks=v_a_sinks, a_w_o=v_a_w_o, b_w_in=v_b_w_in, b_conv_w=v_b_conv_w, b_w_out=v_b_w_out, c_w_down=v_c_w_down, c_q_a_norm=v_c_q_a_norm, c_kv_a_norm=v_c_kv_a_norm, c_w_q_up=v_c_w_q_up, c_w_kv_up=v_c_w_kv_up, c_q_norm=v_c_q_norm, c_k_norm=v_c_k_norm, c_w_o=v_c_w_o, f_w_gate_up=v_f_w_gate_up, f_w_down=v_f_w_down)
    t, d = x.shape[1], x.shape[2]

    rep = {n: local[n] for n in REPLICATED}
    loss, grad_x, parts, small = _forward_backward(x.reshape(t, d), positions.reshape(t), loss_target.reshape(t, d),
                                                   {n: local[n] for n in SHARDED}, rep)

    out_g, out_d, out_m, out_v = {}, {}, {}, {}

    def update(names, parts):
        for n, part in zip(names, parts):
            shape = local[n].shape if n in SHARD_AXIS else (1,) + local[n].shape
            view = _as_rows(shape)
            results = _adamw(part.reshape(view[0], N_DEV, view[1], view[2]),
                             *[src[n].reshape(view) for src in (local, mom1, mom2)], name="adamw_" + n)
            for dst, res in zip((out_g, out_d, out_m, out_v), results):
                dst[n] = res.reshape(local[n].shape)

    update(SHARDED, [parts[n] for n in SHARDED])
    update(REPLICATED, _exchange_now(_Exchange([small[n].reshape((1,) + small[n].shape) for n in REPLICATED],
                                               scatter=False), "gather_small_gradients"))

    loss = lax.psum(loss.reshape(()), MESH_AXES)
    outs = [loss, grad_x.reshape(1, t, d)]
    for res in (out_g, out_d, out_m, out_v):
        outs += [res[n] for n in WEIGHTS]
    return tuple(outs)
```

```python
import jax
import jax.numpy as jnp
import numpy as np
from jax import lax
from jax.experimental import pallas as pl
from jax.experimental.pallas import tpu as pltpu

F32 = jnp.float32
BF16 = jnp.bfloat16

N_DEV = 8
MESH_AXES = ("x", "y", "c")

DEPTH = 4
N_MIXERS = 3
ROPE_THETA = 500000.0
EPS = 1e-6
A_HEADS, A_KV_HEADS, A_HEAD_DIM, A_ROT_DIM, A_WINDOW = 16, 4, 64, 16, 128
A_GROUP = A_HEADS // A_KV_HEADS
C_HEADS, C_NOPE, C_ROPE, C_V, C_Q_RANK, C_KV_RANK = 16, 64, 32, 64, 384, 256
C_QK = C_NOPE + C_ROPE
ADAM_LR, ADAM_B1, ADAM_B2, ADAM_EPS, ADAM_WD, ADAM_STEP = 0.001, 0.9, 0.999, 1e-08, 0.01, 10

VMEM_LIMIT_BYTES = 48 * 1024 * 1024
LANES = 128
NEG = -1e30
MLA_BLOCK = 512
MLA_FWD_BLOCK = 1024

WEIGHTS = ['mix_norm', 'ffn_norm', 'a_w_qkv', 'a_q_norm', 'a_k_norm', 'a_sinks', 'a_w_o', 'b_w_in', 'b_conv_w', 'b_w_out',
           'c_w_down', 'c_q_a_norm', 'c_kv_a_norm', 'c_w_q_up', 'c_w_kv_up', 'c_q_norm', 'c_k_norm', 'c_w_o', 'f_w_gate_up',
           'f_w_down']
SHARD_AXIS = {'a_w_qkv': 2, 'a_w_o': 1, 'b_w_in': 2, 'b_conv_w': 2, 'b_w_out': 1, 'c_w_down': 1, 'c_q_a_norm': 1,
              'c_kv_a_norm': 1, 'c_w_q_up': 2, 'c_w_kv_up': 2, 'c_w_o': 1, 'f_w_gate_up': 2, 'f_w_down': 1}
SHARDED = [n for n in WEIGHTS if n in SHARD_AXIS]
REPLICATED = [n for n in WEIGHTS if n not in SHARD_AXIS]
GATHER_F32 = ['b_conv_w', 'c_q_a_norm', 'c_kv_a_norm']
GATHER_BF16 = [n for n in SHARDED if n not in GATHER_F32]

def _params(semantics=None):
    return pltpu.CompilerParams(dimension_semantics=semantics, vmem_limit_bytes=VMEM_LIMIT_BYTES)


def _div_tile(n, cap, mult=LANES):
    best = None
    t = mult
    while t <= min(n, cap):
        if n % t == 0:
            best = t
        t += mult
    return n if best is None else best


ANY_SPEC = pl.BlockSpec(memory_space=pl.ANY)


class _Exchange:
    def __init__(self, arrays, scatter):
        self.arrays, self.scatter = list(arrays), scatter
        n = len(self.arrays)
        self.out_shapes = [jax.ShapeDtypeStruct(a.shape if scatter else (a.shape[0], N_DEV) + tuple(a.shape[1:]), a.dtype)
                           for a in self.arrays]
        self.scratch = [pltpu.SemaphoreType.DMA((n, N_DEV - 1)), pltpu.SemaphoreType.DMA((n, N_DEV - 1)),
                        pltpu.SemaphoreType.DMA((n,))]

    def _copies(self, src_refs, out_refs, sems):
        send_sems, recv_sems, local_sems = sems
        x, y, c = lax.axis_index("x"), lax.axis_index("y"), lax.axis_index("c")
        me_idx = 4 * x + 2 * y + c
        n = len(self.arrays)

        def remote(a, k, src, dst, to):
            return pltpu.make_async_remote_copy(src_ref=src, dst_ref=dst, send_sem=send_sems.at[a, k],
                                                recv_sem=recv_sems.at[a, k], device_id=to,
                                                device_id_type=pl.DeviceIdType.MESH)

        local, first, forwards, last = [], [], [], []
        if self.scatter:
            for a in range(n):
                local.append(pltpu.make_async_copy(src_refs[a].at[:, me_idx], out_refs[a].at[:, me_idx], local_sems.at[a]))
                for r in range(1, N_DEV):
                    px = 1 - x if (r >> 2) & 1 else x
                    py = 1 - y if (r >> 1) & 1 else y
                    pc = 1 - c if r & 1 else c
                    cp = remote(a, r - 1, src_refs[a].at[:, 4 * px + 2 * py + pc], out_refs[a].at[:, me_idx], (px, py, pc))
                    first.append(cp)
                    last.append(cp)
            return local, first, forwards, last
        me, sibling = (x, y, c), (x, y, 1 - c)
        chips = [(1 - x, y), (x, 1 - y), (1 - x, 1 - y)]

        def place(a, block):
            return out_refs[a].at[:, 4 * block[0] + 2 * block[1] + block[2]]

        for a in range(n):
            local.append(pltpu.make_async_copy(src_refs[a], place(a, me), local_sems.at[a]))
            first.append(remote(a, 0, src_refs[a], place(a, me), sibling))
            last.append(remote(a, 0, place(a, sibling), place(a, sibling), me))
            for j, chip in enumerate(chips):
                first.append(remote(a, 1 + j, src_refs[a], place(a, me), (*chip, c)))
                forwards.append((remote(a, 1 + j, place(a, (*chip, c)), place(a, (*chip, c)), me),
                                 remote(a, 4 + j, place(a, (*chip, c)), place(a, (*chip, c)), sibling)))
                last.append(remote(a, 4 + j, place(a, (*chip, 1 - c)), place(a, (*chip, 1 - c)), me))
        return local, first, forwards, last

    def start(self, src_refs, out_refs, sems):
        local, first, _, _ = self._copies(src_refs, out_refs, sems)
        for cp in local + first:
            cp.start()

    def finish(self, src_refs, out_refs, sems):
        local, first, forwards, last = self._copies(src_refs, out_refs, sems)
        for arrival, forward in forwards:
            arrival.wait_recv()
            forward.start()
        for cp in last:
            cp.wait_recv()
        for cp in first + [forward for _, forward in forwards]:
            cp.wait_send()
        for cp in local:
            cp.wait()


def _exchange_now(exchange, name):
    n = len(exchange.arrays)

    def body(*refs):
        exchange.start(refs[:n], refs[n:2 * n], refs[2 * n:])
        exchange.finish(refs[:n], refs[n:2 * n], refs[2 * n:])

    return pl.pallas_call(
        body, name=name, out_shape=tuple(exchange.out_shapes), in_specs=[ANY_SPEC] * n, out_specs=(ANY_SPEC,) * n,
        scratch_shapes=exchange.scratch,
    )(*exchange.arrays)


def _host_call(body, rider, name, out_shape, grid, in_specs, out_specs, operands, semantics):
    if rider is None:
        return pl.pallas_call(body, name=name, out_shape=tuple(out_shape), grid=grid, in_specs=list(in_specs),
                              out_specs=tuple(out_specs), compiler_params=_params(semantics))(*operands), None
    n_in, n_out, r = len(in_specs), len(out_shape), len(rider.arrays)

    def riding(*refs):
        ins, rider_in = refs[:n_in], refs[n_in:n_in + r]
        outs, rider_out = refs[n_in + r:n_in + r + n_out], refs[n_in + r + n_out:n_in + 2 * r + n_out]
        sems = refs[n_in + 2 * r + n_out:]
        step = pl.program_id(0)

        @pl.when(step == 0)
        def _():
            rider.start(rider_in, rider_out, sems)

        body(*ins, *outs)

        @pl.when(step == grid[0] - 1)
        def _():
            rider.finish(rider_in, rider_out, sems)

    results = pl.pallas_call(
        riding, name=name, out_shape=tuple(out_shape) + tuple(rider.out_shapes), grid=grid,
        in_specs=list(in_specs) + [ANY_SPEC] * r, out_specs=tuple(out_specs) + (ANY_SPEC,) * r,
        scratch_shapes=rider.scratch, compiler_params=_params(("arbitrary",)),
    )(*operands, *rider.arrays)
    return results[:n_out], results[n_out:]


def _matmul(a, b, mode, name, out_dtype=F32, residual=None):
    if mode == 'nn':
        (m, k), (k2, n) = a.shape, b.shape
    elif mode == 'nt':
        (m, k), (n, k2) = a.shape, b.shape
    else:
        (k, m), (k2, n) = a.shape, b.shape
    assert k == k2, (name, a.shape, b.shape, mode)
    if mode == 'tn':
        tm, tk = _div_tile(m, 1408), _div_tile(k, 512, 16)
    else:
        tm, tk = _div_tile(m, 1024, 16), _div_tile(k, 1536)
    tn = _div_tile(n, 1408)
    nk = k // tk
    dims = {'nn': (((1,), (0,)), ((), ())), 'nt': (((1,), (1,)), ((), ())), 'tn': (((0,), (0,)), ((), ()))}[mode]

    def product(a_ref, b_ref):
        return lax.dot_general(a_ref[...].astype(BF16), b_ref[...].astype(BF16), dims, preferred_element_type=F32)

    def finish(r, rest):
        if residual is not None:
            r = r + rest[0][...]
        rest[-1 if nk == 1 else -2][...] = r.astype(out_dtype)

    def body_single(a_ref, b_ref, *rest):
        finish(product(a_ref, b_ref), rest)

    def body_accumulate(a_ref, b_ref, *rest):
        acc = rest[-1]
        kk = pl.program_id(2)

        @pl.when(kk == 0)
        def _():
            acc[...] = jnp.zeros_like(acc)

        acc[...] += product(a_ref, b_ref)

        @pl.when(kk == nk - 1)
        def _():
            finish(acc[...], rest)

    a_spec = pl.BlockSpec((tk, tm), lambda i, j, kk: (kk, i)) if mode == 'tn' else pl.BlockSpec((tm, tk), lambda i, j, kk: (i, kk))
    b_spec = pl.BlockSpec((tn, tk), lambda i, j, kk: (j, kk)) if mode == 'nt' else pl.BlockSpec((tk, tn), lambda i, j, kk: (kk, j))
    o_spec = pl.BlockSpec((tm, tn), lambda i, j, kk: (i, j))
    in_specs, operands = [a_spec, b_spec], [a, b]
    if residual is not None:
        in_specs.append(o_spec)
        operands.append(residual)
    return pl.pallas_call(
        body_single if nk == 1 else body_accumulate, name=name, out_shape=jax.ShapeDtypeStruct((m, n), out_dtype),
        grid=(m // tm, n // tn, nk), in_specs=in_specs, out_specs=o_spec,
        scratch_shapes=[] if nk == 1 else [pltpu.VMEM((tm, tn), F32)],
        compiler_params=_params(("parallel", "parallel", "arbitrary")),
    )(*operands)


def _row_spec(tm, cols):
    return pl.BlockSpec((tm, cols), lambda i: (i, 0))


def _const_spec(shape):
    return pl.BlockSpec(shape, lambda i: tuple(0 for _ in shape))


def _accumulate(ref, value, step):
    @pl.when(step == 0)
    def _():
        ref[...] = value

    @pl.when(step > 0)
    def _():
        ref[...] += value


def _rstd(x):
    return lax.rsqrt(jnp.mean(x * x, axis=-1, keepdims=True) + EPS)


def _norm_bwd(x, g, dout):
    xn = x * _rstd(x)
    dg = jnp.sum(dout * xn, axis=0, keepdims=True)
    dxn = dout * g
    dx = _rstd(x) * (dxn - xn * jnp.mean(dxn * xn, axis=-1, keepdims=True))
    return dx, dg


def _rmsnorm_fwd(x, g, name):
    t, d = x.shape
    tm = _div_tile(t, 512, 16)

    def body(x_ref, g_ref, o_ref):
        xv = x_ref[...]
        o_ref[...] = (xv * _rstd(xv) * g_ref[...]).astype(BF16)

    return pl.pallas_call(
        body, name=name, out_shape=jax.ShapeDtypeStruct((t, d), BF16), grid=(t // tm,),
        in_specs=[_row_spec(tm, d), _const_spec((1, d))], out_specs=_row_spec(tm, d),
        compiler_params=_params(("parallel",)),
    )(x, g)


def _rmsnorm_bwd(x, g, dh, dres, name):
    t, d = x.shape
    tm = _div_tile(t, 512, 8)

    def body(x_ref, g_ref, dh_ref, dres_ref, dx_ref, dg_ref):
        dx, dg = _norm_bwd(x_ref[...], g_ref[...], dh_ref[...])
        dx_ref[...] = dres_ref[...] + dx
        _accumulate(dg_ref, dg, pl.program_id(0))

    return pl.pallas_call(
        body, name=name,
        out_shape=(jax.ShapeDtypeStruct((t, d), F32), jax.ShapeDtypeStruct((1, d), F32)), grid=(t // tm,),
        in_specs=[_row_spec(tm, d), _const_spec((1, d)), _row_spec(tm, d), _row_spec(tm, d)],
        out_specs=(_row_spec(tm, d), _const_spec((1, d))),
        compiler_params=_params(("arbitrary",)),
    )(x, g, dh, dres)


def _sigmoid(x):
    return 0.5 * jnp.tanh(0.5 * x) + 0.5


def _swiglu_fwd(gu, name):
    t, f2 = gu.shape
    f = f2 // 2
    tm = _div_tile(t, 512, 16)

    def body(gu_ref, o_ref):
        gate, up = gu_ref[:, :f].astype(F32), gu_ref[:, f:].astype(F32)
        o_ref[...] = (gate * _sigmoid(gate) * up).astype(BF16)

    return pl.pallas_call(
        body, name=name, out_shape=jax.ShapeDtypeStruct((t, f), BF16), grid=(t // tm,),
        in_specs=[_row_spec(tm, f2)], out_specs=_row_spec(tm, f),
        compiler_params=_params(("parallel",)),
    )(gu)


def _swiglu_bwd(gu, da, name):
    t, f2 = gu.shape
    f = f2 // 2
    tm = _div_tile(t, 512, 16)

    def body(gu_ref, da_ref, o_ref):
        gate, up, dav = gu_ref[:, :f].astype(F32), gu_ref[:, f:].astype(F32), da_ref[...].astype(F32)
        sig = _sigmoid(gate)
        o_ref[:, :f] = (dav * up * (sig * (1.0 + gate * (1.0 - sig)))).astype(BF16)
        o_ref[:, f:] = (dav * (gate * sig)).astype(BF16)

    return pl.pallas_call(
        body, name=name, out_shape=jax.ShapeDtypeStruct((t, f2), BF16), grid=(t // tm,),
        in_specs=[_row_spec(tm, f2), _row_spec(tm, f)], out_specs=_row_spec(tm, f2),
        compiler_params=_params(("parallel",)),
    )(gu, da)


def _loss_head(y, target, name):
    t, d = y.shape
    tm = _div_tile(t, 512, 8)

    def body(y_ref, t_ref, loss_ref, dy_ref):
        diff = y_ref[...] - t_ref[...]
        dy_ref[...] = diff * (1.0 / d)
        part = jnp.sum(jnp.sum(diff * diff, axis=1, keepdims=True), axis=0, keepdims=True) * (0.5 / d)
        _accumulate(loss_ref, part, pl.program_id(0))

    return pl.pallas_call(
        body, name=name,
        out_shape=(jax.ShapeDtypeStruct((1, 1), F32), jax.ShapeDtypeStruct((t, d), F32)), grid=(t // tm,),
        in_specs=[_row_spec(tm, d), _row_spec(tm, d)], out_specs=(_const_spec((1, 1)), _row_spec(tm, d)),
        compiler_params=_params(("arbitrary",)),
    )(y, target)


HALO = 8


def _shift_down(z, k, halo_rows):
    tm = z.shape[0]
    row = lax.broadcasted_iota(jnp.int32, z.shape, 0)
    out = pltpu.roll(z, k, 0)
    for j in range(k):
        out = jnp.where(row == j, halo_rows[HALO - k + j:HALO - k + j + 1, :], out)
    return out


def _shift_up(z, k, halo_rows):
    tm = z.shape[0]
    row = lax.broadcasted_iota(jnp.int32, z.shape, 0)
    out = pltpu.roll(z, tm - k, 0)
    for j in range(k):
        out = jnp.where(row == tm - k + j, halo_rows[j:j + 1, :], out)
    return out


def _sconv_specs(t, tm, cols):
    per = tm // HALO
    last = t // HALO - 1
    cur = pl.BlockSpec((tm, cols), lambda i: (i, 0))
    prev = pl.BlockSpec((HALO, cols), lambda i: (jnp.maximum(i * per - 1, 0), 0))
    nxt = pl.BlockSpec((HALO, cols), lambda i: (jnp.minimum((i + 1) * per, last), 0))
    return cur, prev, nxt


def _sconv_fwd(bcu, conv_w, name):
    t, d3 = bcu.shape
    d = d3 // 3
    tm = _div_tile(t, 256, 16)
    cur, prev, _ = _sconv_specs(t, tm, d3)

    def body(cur_ref, prev_ref, w_ref, o_ref):
        i = pl.program_id(0)
        z = cur_ref[:, d:2 * d] * cur_ref[:, 2 * d:]
        zp = prev_ref[:, d:2 * d] * prev_ref[:, 2 * d:] * (i > 0).astype(F32)
        y = w_ref[0:1, :] * _shift_down(z, 2, zp) + w_ref[1:2, :] * _shift_down(z, 1, zp) + w_ref[2:3, :] * z
        o_ref[...] = (cur_ref[:, :d] * y).astype(BF16)

    return pl.pallas_call(
        body, name=name, out_shape=jax.ShapeDtypeStruct((t, d), BF16), grid=(t // tm,),
        in_specs=[cur, prev, _const_spec((3, d))], out_specs=_row_spec(tm, d),
        compiler_params=_params(("parallel",)),
    )(bcu, bcu, conv_w)


def _sconv_bwd(bcu, dout, conv_w, name):
    t, d3 = bcu.shape
    d = d3 // 3
    tm = _div_tile(t, 256, 16)
    cur, prev, nxt = _sconv_specs(t, tm, d3)
    dcur, _, dnxt = _sconv_specs(t, tm, d)
    n_tiles = t // tm

    def body(cur_ref, prev_ref, nxt_ref, do_ref, don_ref, w_ref, o_ref, dw_ref):
        i = pl.program_id(0)
        b, cg, u = cur_ref[:, :d], cur_ref[:, d:2 * d], cur_ref[:, 2 * d:]
        z = cg * u
        zp = prev_ref[:, d:2 * d] * prev_ref[:, 2 * d:] * (i > 0).astype(F32)
        z1, z2 = _shift_down(z, 1, zp), _shift_down(z, 2, zp)
        w0, w1, w2 = w_ref[0:1, :], w_ref[1:2, :], w_ref[2:3, :]
        y = w0 * z2 + w1 * z1 + w2 * z
        dov = do_ref[...]
        dy = dov * b
        dyn = don_ref[...] * nxt_ref[:, :d] * (i < n_tiles - 1).astype(F32)
        dz = w2 * dy + w1 * _shift_up(dy, 1, dyn) + w0 * _shift_up(dy, 2, dyn)
        o_ref[:, :d] = (dov * y).astype(BF16)
        o_ref[:, d:2 * d] = (dz * u).astype(BF16)
        o_ref[:, 2 * d:] = (dz * cg).astype(BF16)
        dw = jnp.concatenate([jnp.sum(dy * z2, axis=0, keepdims=True), jnp.sum(dy * z1, axis=0, keepdims=True),
                              jnp.sum(dy * z, axis=0, keepdims=True)], axis=0)
        _accumulate(dw_ref, dw, i)

    return pl.pallas_call(
        body, name=name,
        out_shape=(jax.ShapeDtypeStruct((t, d3), BF16), jax.ShapeDtypeStruct((3, d), F32)), grid=(n_tiles,),
        in_specs=[cur, prev, nxt, dcur, dnxt, _const_spec((3, d))],
        out_specs=(_row_spec(tm, d3), _const_spec((3, d))),
        compiler_params=_params(("arbitrary",)),
    )(bcu, bcu, bcu, dout, dout, conv_w)


def _rope_tables(positions, rot, lead, trail):
    inv_freq = ROPE_THETA ** (-jnp.arange(0, rot, 2, dtype=F32) / rot)
    ang = positions.astype(F32)[:, None] * inv_freq
    cos, sin = jnp.cos(ang), jnp.sin(ang)
    t = positions.shape[0]
    cos_full = jnp.concatenate([jnp.ones((t, lead), F32), cos, cos, jnp.ones((t, trail), F32)], axis=1)
    sin_full = jnp.concatenate([jnp.zeros((t, lead), F32), -sin, sin, jnp.zeros((t, trail), F32)], axis=1)
    return cos_full, sin_full


def _swap_halves(x, lead, rot):
    half = rot // 2
    rows, d = x.shape
    parts = []
    if lead:
        parts.append(jnp.zeros((rows, lead), x.dtype))
    parts += [x[:, lead + half:lead + rot], x[:, lead:lead + half]]
    if d - lead - rot:
        parts.append(jnp.zeros((rows, d - lead - rot), x.dtype))
    return jnp.concatenate(parts, axis=1)


def _head_fwd(x, g, cos, sin, lead, rot):
    n = x * _rstd(x) * g
    return n * cos + _swap_halves(n, lead, rot) * sin


def _head_bwd(x, g, cos, sin, dout, lead, rot):
    dn = dout * cos + _swap_halves(dout * sin, lead, rot)
    return _norm_bwd(x, g, dn)


A_Q_COLS = A_HEADS * A_HEAD_DIM
A_KV_COLS = A_KV_HEADS * A_HEAD_DIM
A_COLS = A_Q_COLS + 2 * A_KV_COLS
A_SCALE = A_HEAD_DIM ** -0.5


A_NORMED = A_Q_COLS + A_KV_COLS


def _swa_prep_tables(q_norm, k_norm, cos, sin):
    lane = np.arange(A_NORMED)
    seg = (lane[:, None] // A_HEAD_DIM == np.arange(LANES)[None, :]).astype(np.float32)
    fold = (np.where(lane < A_Q_COLS, 0, A_HEAD_DIM)[:, None] + lane[:, None] % A_HEAD_DIM
            == np.arange(LANES)[None, :]).astype(np.float32)
    heads = A_HEADS + A_KV_HEADS
    gains = jnp.concatenate([jnp.tile(q_norm, (1, A_HEADS)), jnp.tile(k_norm, (1, A_KV_HEADS))], axis=1)
    return (gains, jnp.tile(cos, (1, heads)), jnp.tile(sin, (1, heads)),
            jnp.asarray(seg, BF16), jnp.asarray(seg.T, BF16), jnp.asarray(fold, BF16))


def _pieces_dot(a, b, pieces):
    total, rest = None, a
    for _ in range(pieces):
        piece = rest.astype(BF16)
        term = jnp.dot(piece, b, preferred_element_type=F32)
        total = term if total is None else total + term
        rest = rest - piece.astype(F32)
    return total


def _wide_rstd(x, seg, seg_t):
    mean_sq = _pieces_dot(x * x, seg, 1) * (1.0 / A_HEAD_DIM)
    return _pieces_dot(lax.rsqrt(mean_sq + EPS), seg_t, 2)


def _wide_partner(n):
    dim = lax.broadcasted_iota(jnp.int32, n.shape, 1) & (A_HEAD_DIM - 1)
    half = A_ROT_DIM // 2
    return jnp.where(dim < half, pltpu.roll(n, A_NORMED - half, 1),
                     jnp.where(dim < A_ROT_DIM, pltpu.roll(n, half, 1), 0.0))


def _swa_prep_fwd(qkv, q_norm, k_norm, cos, sin, name):
    t = qkv.shape[0]
    tm = _div_tile(t, 256, 16)
    gains, cos_w, sin_w, seg, seg_t, _ = _swa_prep_tables(q_norm, k_norm, cos, sin)

    def body(x_ref, g_ref, cos_ref, sin_ref, seg_ref, segt_ref, o_ref):
        x = x_ref[:, :A_NORMED]
        n = x * _wide_rstd(x, seg_ref[...], segt_ref[...]) * g_ref[...]
        o_ref[:, :A_NORMED] = (n * cos_ref[...] + _wide_partner(n) * sin_ref[...]).astype(BF16)
        o_ref[:, A_NORMED:] = x_ref[:, A_NORMED:].astype(BF16)

    return pl.pallas_call(
        body, name=name, out_shape=jax.ShapeDtypeStruct((t, A_COLS), BF16), grid=(t // tm,),
        in_specs=[_row_spec(tm, A_COLS), _const_spec((1, A_NORMED)), _row_spec(tm, A_NORMED), _row_spec(tm, A_NORMED),
                  _const_spec(seg.shape), _const_spec(seg_t.shape)],
        out_specs=_row_spec(tm, A_COLS), compiler_params=_params(("parallel",)),
    )(qkv, gains, cos_w, sin_w, seg, seg_t)


def _swa_prep_bwd(qkv, dqkv_r, q_norm, k_norm, cos, sin, name):
    t = qkv.shape[0]
    tm = _div_tile(t, 256, 16)
    hd = A_HEAD_DIM
    gains, cos_w, sin_w, seg, seg_t, fold = _swa_prep_tables(q_norm, k_norm, cos, sin)

    def body(x_ref, d_ref, g_ref, cos_ref, sin_ref, seg_ref, segt_ref, fold_ref, o_ref, dgq_ref, dgk_ref):
        x, dout = x_ref[:, :A_NORMED], d_ref[:, :A_NORMED]
        rstd = _wide_rstd(x, seg_ref[...], segt_ref[...])
        xn = x * rstd
        dn = dout * cos_ref[...] + _wide_partner(dout * sin_ref[...])
        dg = _pieces_dot(jnp.sum(dn * xn, axis=0, keepdims=True), fold_ref[...], 3)
        dxn = dn * g_ref[...]
        mean = _pieces_dot(_pieces_dot(dxn * xn, seg_ref[...], 1) * (1.0 / hd), segt_ref[...], 2)
        o_ref[:, :A_NORMED] = (rstd * (dxn - xn * mean)).astype(BF16)
        o_ref[:, A_NORMED:] = d_ref[:, A_NORMED:].astype(BF16)
        _accumulate(dgq_ref, dg[:, :hd], pl.program_id(0))
        _accumulate(dgk_ref, dg[:, hd:2 * hd], pl.program_id(0))

    return pl.pallas_call(
        body, name=name,
        out_shape=(jax.ShapeDtypeStruct((t, A_COLS), BF16), jax.ShapeDtypeStruct((1, hd), F32),
                   jax.ShapeDtypeStruct((1, hd), F32)),
        grid=(t // tm,),
        in_specs=[_row_spec(tm, A_COLS), _row_spec(tm, A_COLS), _const_spec((1, A_NORMED)), _row_spec(tm, A_NORMED),
                  _row_spec(tm, A_NORMED), _const_spec(seg.shape), _const_spec(seg_t.shape), _const_spec(fold.shape)],
        out_specs=(_row_spec(tm, A_COLS), _const_spec((1, hd)), _const_spec((1, hd))),
        compiler_params=_params(("arbitrary",)),
    )(qkv, dqkv_r, gains, cos_w, sin_w, seg, seg_t, fold)


def _group_rows(ref, k, width=A_HEAD_DIM, base=0):
    return jnp.concatenate([ref[:, base + (A_GROUP * k + g) * width:base + (A_GROUP * k + g + 1) * width]
                            for g in range(A_GROUP)], axis=0)


def _group_column(ref, k, rows):
    cols = []
    for g in range(A_GROUP):
        h = A_GROUP * k + g
        col = ref[:, h:h + 1]
        cols.append(jnp.broadcast_to(col, (rows, 1)) if col.shape[0] == 1 else col)
    return jnp.concatenate(cols, axis=0)


def _swa_fwd(qkv_r, sinks, name, rider=None):
    t = qkv_r.shape[0]
    blk = A_WINDOW
    nb = t // blk
    hd = A_HEAD_DIM
    kv_block = A_Q_COLS // (2 * A_KV_COLS)

    def body(q_ref, kvc_ref, kvp_ref, s_ref, o_ref, lse_ref):
        n = pl.program_id(0)
        shape = (A_GROUP * blk, 2 * blk)
        qpos = lax.broadcasted_iota(jnp.int32, shape, 0) & (blk - 1)
        col = lax.broadcasted_iota(jnp.int32, shape, 1)
        delta = qpos + blk - col
        valid = (delta >= 0) & (delta < A_WINDOW) & ((col >= blk) | (n > 0))
        for k in range(A_KV_HEADS):
            qg = _group_rows(q_ref, k)
            kw = jnp.concatenate([kvp_ref[:, k * hd:(k + 1) * hd], kvc_ref[:, k * hd:(k + 1) * hd]], axis=0)
            vw = jnp.concatenate([kvp_ref[:, A_KV_COLS + k * hd:A_KV_COLS + (k + 1) * hd],
                                  kvc_ref[:, A_KV_COLS + k * hd:A_KV_COLS + (k + 1) * hd]], axis=0)
            s = lax.dot_general(qg, kw, (((1,), (1,)), ((), ())), preferred_element_type=F32) * A_SCALE
            s = jnp.where(valid, s, NEG)
            sink = _group_column(s_ref, k, blk)
            m = jnp.maximum(jnp.max(s, axis=-1, keepdims=True), sink)
            p = jnp.exp(s - m)
            denom = jnp.sum(p, axis=-1, keepdims=True) + jnp.exp(sink - m)
            o = jnp.dot(p.astype(BF16), vw, preferred_element_type=F32) / denom
            lse = m + jnp.log(denom)
            for g in range(A_GROUP):
                h = A_GROUP * k + g
                o_ref[:, h * hd:(h + 1) * hd] = o[g * blk:(g + 1) * blk].astype(BF16)
                lse_ref[:, h:h + 1] = lse[g * blk:(g + 1) * blk]

    return _host_call(
        body, rider, name,
        out_shape=(jax.ShapeDtypeStruct((t, A_Q_COLS), BF16), jax.ShapeDtypeStruct((t, A_HEADS), F32)), grid=(nb,),
        in_specs=[pl.BlockSpec((blk, A_Q_COLS), lambda n: (n, 0)),
                  pl.BlockSpec((blk, 2 * A_KV_COLS), lambda n: (n, kv_block)),
                  pl.BlockSpec((blk, 2 * A_KV_COLS), lambda n: (jnp.maximum(n - 1, 0), kv_block)),
                  _const_spec((1, A_HEADS))],
        out_specs=(pl.BlockSpec((blk, A_Q_COLS), lambda n: (n, 0)), pl.BlockSpec((blk, A_HEADS), lambda n: (n, 0))),
        operands=(qkv_r, qkv_r, qkv_r, sinks), semantics=("parallel",))


def _swa_bwd(qkv_r, o, lse, do, sinks, name, rider=None):
    t = qkv_r.shape[0]
    blk = A_WINDOW
    nb = t // blk
    hd = A_HEAD_DIM
    kv_block = A_Q_COLS // (2 * A_KV_COLS)
    rows = A_GROUP * blk

    def nxt(n):
        return jnp.minimum(n + 1, nb - 1)

    def body(qc_ref, qn_ref, kvc_ref, kvp_ref, doc_ref, don_ref, oc_ref, on_ref, lc_ref, ln_ref, s_ref, dx_ref, ds_ref):
        n = pl.program_id(0)
        shape = (2 * rows, 2 * blk)
        row = lax.broadcasted_iota(jnp.int32, shape, 0)
        col = lax.broadcasted_iota(jnp.int32, shape, 1)
        is_next = row >= rows
        delta = jnp.where(is_next, blk, 0) + blk + (row & (blk - 1)) - col
        valid = ((delta >= 0) & (delta < A_WINDOW) & ((col >= blk) | (n > 0)) & (jnp.logical_not(is_next) | (n < nb - 1)))
        dsink_cols = []
        for k in range(A_KV_HEADS):
            qs = jnp.concatenate([_group_rows(qc_ref, k), _group_rows(qn_ref, k)], axis=0)
            dos = jnp.concatenate([_group_rows(doc_ref, k), _group_rows(don_ref, k)], axis=0)
            os_ = jnp.concatenate([_group_rows(oc_ref, k), _group_rows(on_ref, k)], axis=0).astype(F32)
            lses = jnp.concatenate([_group_column(lc_ref, k, blk), _group_column(ln_ref, k, blk)], axis=0)
            kw = jnp.concatenate([kvp_ref[:, k * hd:(k + 1) * hd], kvc_ref[:, k * hd:(k + 1) * hd]], axis=0)
            vw = jnp.concatenate([kvp_ref[:, A_KV_COLS + k * hd:A_KV_COLS + (k + 1) * hd],
                                  kvc_ref[:, A_KV_COLS + k * hd:A_KV_COLS + (k + 1) * hd]], axis=0)
            s = lax.dot_general(qs, kw, (((1,), (1,)), ((), ())), preferred_element_type=F32) * A_SCALE
            p = jnp.exp(jnp.where(valid, s - lses, NEG))
            dos_b = dos.astype(BF16)
            dp = lax.dot_general(dos_b, vw, (((1,), (1,)), ((), ())), preferred_element_type=F32)
            dlt = jnp.sum(dos * os_, axis=-1, keepdims=True)
            ds = p * (dp - dlt)
            dq = jnp.dot(ds[:rows].astype(BF16), kw, preferred_element_type=F32) * A_SCALE
            dk = lax.dot_general(ds[:, blk:].astype(BF16), qs, (((0,), (0,)), ((), ())), preferred_element_type=F32) * A_SCALE
            dv = lax.dot_general(p[:, blk:].astype(BF16), dos_b, (((0,), (0,)), ((), ())), preferred_element_type=F32)
            for g in range(A_GROUP):
                h = A_GROUP * k + g
                dx_ref[:, h * hd:(h + 1) * hd] = dq[g * blk:(g + 1) * blk]
            dx_ref[:, A_Q_COLS + k * hd:A_Q_COLS + (k + 1) * hd] = dk
            dx_ref[:, A_Q_COLS + A_KV_COLS + k * hd:A_Q_COLS + A_KV_COLS + (k + 1) * hd] = dv
            sink = _group_column(s_ref, k, blk)
            contrib = -jnp.exp(sink - lses[:rows]) * dlt[:rows]
            for g in range(A_GROUP):
                dsink_cols.append(jnp.sum(contrib[g * blk:(g + 1) * blk], axis=0, keepdims=True))
        _accumulate(ds_ref, jnp.concatenate(dsink_cols, axis=1), n)

    q_spec = lambda f: pl.BlockSpec((blk, A_Q_COLS), lambda n: (f(n), 0))
    l_spec = lambda f: pl.BlockSpec((blk, A_HEADS), lambda n: (f(n), 0))
    same = lambda n: n
    return _host_call(
        body, rider, name,
        out_shape=(jax.ShapeDtypeStruct((t, A_COLS), F32), jax.ShapeDtypeStruct((1, A_HEADS), F32)), grid=(nb,),
        in_specs=[q_spec(same), q_spec(nxt),
                  pl.BlockSpec((blk, 2 * A_KV_COLS), lambda n: (n, kv_block)),
                  pl.BlockSpec((blk, 2 * A_KV_COLS), lambda n: (jnp.maximum(n - 1, 0), kv_block)),
                  q_spec(same), q_spec(nxt), q_spec(same), q_spec(nxt), l_spec(same), l_spec(nxt),
                  _const_spec((1, A_HEADS))],
        out_specs=(pl.BlockSpec((blk, A_COLS), lambda n: (n, 0)), _const_spec((1, A_HEADS))),
        operands=(qkv_r, qkv_r, qkv_r, qkv_r, do, do, o, o, lse, lse, sinks), semantics=("arbitrary",))


C_DOWN_COLS = C_Q_RANK + C_KV_RANK + C_ROPE
C_Q_COLS = C_HEADS * C_QK
C_KV_COLS = C_HEADS * (C_NOPE + C_V)
C_O_COLS = C_HEADS * C_V
C_PAD = LANES
C_SCALE = C_QK ** -0.5
LOG2E = 1.4426950408889634
LN2 = 0.6931471805599453
C_Q_SCALE = C_SCALE * LOG2E
C_PAIR = 2


def _mla_latent_fwd(down, q_a_norm, kv_a_norm, name):
    t = down.shape[0]
    tm = _div_tile(t, 512, 16)

    def body(x_ref, gq_ref, gk_ref, cq_ref, ckv_ref):
        cq, ckv = x_ref[:, :C_Q_RANK], x_ref[:, C_Q_RANK:C_Q_RANK + C_KV_RANK]
        cq_ref[...] = (cq * _rstd(cq) * gq_ref[...]).astype(BF16)
        ckv_ref[...] = (ckv * _rstd(ckv) * gk_ref[...]).astype(BF16)

    return pl.pallas_call(
        body, name=name,
        out_shape=(jax.ShapeDtypeStruct((t, C_Q_RANK), BF16), jax.ShapeDtypeStruct((t, C_KV_RANK), BF16)), grid=(t // tm,),
        in_specs=[_row_spec(tm, C_DOWN_COLS), _const_spec((1, C_Q_RANK)), _const_spec((1, C_KV_RANK))],
        out_specs=(_row_spec(tm, C_Q_RANK), _row_spec(tm, C_KV_RANK)), compiler_params=_params(("parallel",)),
    )(down, q_a_norm, kv_a_norm)


def _mla_latent_bwd(down, dcq, dckv, dkrope, q_a_norm, kv_a_norm, name):
    t = down.shape[0]
    tm = _div_tile(t, 512, 16)

    def body(x_ref, dcq_ref, dckv_ref, dkr_ref, gq_ref, gk_ref, o_ref, dgq_ref, dgk_ref):
        dq, dgq = _norm_bwd(x_ref[:, :C_Q_RANK], gq_ref[...], dcq_ref[...])
        dkv, dgk = _norm_bwd(x_ref[:, C_Q_RANK:C_Q_RANK + C_KV_RANK], gk_ref[...], dckv_ref[...])
        o_ref[...] = jnp.concatenate([dq, dkv, dkr_ref[...]], axis=1).astype(BF16)
        _accumulate(dgq_ref, dgq, pl.program_id(0))
        _accumulate(dgk_ref, dgk, pl.program_id(0))

    return pl.pallas_call(
        body, name=name,
        out_shape=(jax.ShapeDtypeStruct((t, C_DOWN_COLS), BF16), jax.ShapeDtypeStruct((1, C_Q_RANK), F32),
                   jax.ShapeDtypeStruct((1, C_KV_RANK), F32)),
        grid=(t // tm,),
        in_specs=[_row_spec(tm, C_DOWN_COLS), _row_spec(tm, C_Q_RANK), _row_spec(tm, C_KV_RANK), _row_spec(tm, C_ROPE),
                  _const_spec((1, C_Q_RANK)), _const_spec((1, C_KV_RANK))],
        out_specs=(_row_spec(tm, C_DOWN_COLS), _const_spec((1, C_Q_RANK)), _const_spec((1, C_KV_RANK))),
        compiler_params=_params(("arbitrary",)),
    )(down, dcq, dckv, dkrope, q_a_norm, kv_a_norm)


def _head_major_spec(tm, width):
    return pl.BlockSpec((C_HEADS, tm, width), lambda i: (0, i, 0))


def _mla_qk_fwd(qw, kvw, down, q_norm, k_norm, cos, sin, name):
    t = qw.shape[0]
    tm = _div_tile(t, 256, 16)
    kvd = C_NOPE + C_V

    def body(q_ref, kv_ref, dn_ref, gq_ref, gk_ref, cos_ref, sin_ref, qo_ref, ko_ref, vo_ref):
        cosv, sinv = cos_ref[...], sin_ref[...]
        k_rope = dn_ref[:, C_Q_RANK + C_KV_RANK:]
        pad = jnp.zeros((tm, C_PAD - C_QK), F32)
        one_then_zeros = (lax.broadcasted_iota(jnp.int32, (tm, C_PAD - C_V), 1) == 0).astype(F32)
        for h in range(C_HEADS):
            qh = _head_fwd(q_ref[:, h * C_QK:(h + 1) * C_QK], gq_ref[...], cosv, sinv, C_NOPE, C_ROPE)
            kx = jnp.concatenate([kv_ref[:, h * kvd:h * kvd + C_NOPE], k_rope], axis=1)
            kh = _head_fwd(kx, gk_ref[...], cosv, sinv, C_NOPE, C_ROPE)
            qo_ref[h] = jnp.concatenate([qh * C_Q_SCALE, pad], axis=1).astype(BF16)
            ko_ref[h] = jnp.concatenate([kh, pad], axis=1).astype(BF16)
            vo_ref[h] = jnp.concatenate([kv_ref[:, h * kvd + C_NOPE:(h + 1) * kvd], one_then_zeros], axis=1).astype(BF16)

    return pl.pallas_call(
        body, name=name,
        out_shape=(jax.ShapeDtypeStruct((C_HEADS, t, C_PAD), BF16), jax.ShapeDtypeStruct((C_HEADS, t, C_PAD), BF16),
                   jax.ShapeDtypeStruct((C_HEADS, t, C_PAD), BF16)),
        grid=(t // tm,),
        in_specs=[_row_spec(tm, C_Q_COLS), _row_spec(tm, C_KV_COLS), _row_spec(tm, C_DOWN_COLS), _const_spec((1, C_QK)),
                  _const_spec((1, C_QK)), _row_spec(tm, C_QK), _row_spec(tm, C_QK)],
        out_specs=(_head_major_spec(tm, C_PAD), _head_major_spec(tm, C_PAD), _head_major_spec(tm, C_PAD)),
        compiler_params=_params(("parallel",)),
    )(qw, kvw, down, q_norm, k_norm, cos, sin)


def _mla_qk_bwd(qw, kvw, down, dq, dk, dv, q_norm, k_norm, cos, sin, name, rider=None):
    t = qw.shape[0]
    tm = _div_tile(t, 256, 16)
    kvd = C_NOPE + C_V

    def body(q_ref, kv_ref, dn_ref, dq_ref, dk_ref, dv_ref, gq_ref, gk_ref, cos_ref, sin_ref,
             dqw_ref, dkvw_ref, dkr_ref, dgq_ref, dgk_ref):
        cosv, sinv = cos_ref[...], sin_ref[...]
        k_rope = dn_ref[:, C_Q_RANK + C_KV_RANK:]
        dgq = jnp.zeros((1, C_QK), F32)
        dgk = jnp.zeros((1, C_QK), F32)
        dkr = jnp.zeros((tm, C_ROPE), F32)
        for h in range(C_HEADS):
            dxq, dg = _head_bwd(q_ref[:, h * C_QK:(h + 1) * C_QK], gq_ref[...], cosv, sinv, dq_ref[h][:, :C_QK], C_NOPE, C_ROPE)
            dgq = dgq + dg
            dqw_ref[:, h * C_QK:(h + 1) * C_QK] = dxq.astype(BF16)
            kx = jnp.concatenate([kv_ref[:, h * kvd:h * kvd + C_NOPE], k_rope], axis=1)
            dxk, dg = _head_bwd(kx, gk_ref[...], cosv, sinv, dk_ref[h][:, :C_QK], C_NOPE, C_ROPE)
            dgk = dgk + dg
            dkr = dkr + dxk[:, C_NOPE:]
            dkvw_ref[:, h * kvd:(h + 1) * kvd] = jnp.concatenate([dxk[:, :C_NOPE], dv_ref[h]], axis=1).astype(BF16)
        dkr_ref[...] = dkr
        _accumulate(dgq_ref, dgq, pl.program_id(0))
        _accumulate(dgk_ref, dgk, pl.program_id(0))

    return _host_call(
        body, rider, name,
        out_shape=(jax.ShapeDtypeStruct((t, C_Q_COLS), BF16), jax.ShapeDtypeStruct((t, C_KV_COLS), BF16),
                   jax.ShapeDtypeStruct((t, C_ROPE), F32), jax.ShapeDtypeStruct((1, C_QK), F32),
                   jax.ShapeDtypeStruct((1, C_QK), F32)),
        grid=(t // tm,),
        in_specs=[_row_spec(tm, C_Q_COLS), _row_spec(tm, C_KV_COLS), _row_spec(tm, C_DOWN_COLS),
                  _head_major_spec(tm, C_PAD), _head_major_spec(tm, C_PAD), _head_major_spec(tm, C_V),
                  _const_spec((1, C_QK)), _const_spec((1, C_QK)), _row_spec(tm, C_QK), _row_spec(tm, C_QK)],
        out_specs=(_row_spec(tm, C_Q_COLS), _row_spec(tm, C_KV_COLS), _row_spec(tm, C_ROPE), _const_spec((1, C_QK)),
                   _const_spec((1, C_QK))),
        operands=(qw, kvw, down, dq, dk, dv, q_norm, k_norm, cos, sin), semantics=("arbitrary",))


def _causal_keep(rows, cols, row_offset=0, transposed=False):
    row = lax.broadcasted_iota(jnp.int32, (rows, cols), 0) + row_offset
    col = lax.broadcasted_iota(jnp.int32, (rows, cols), 1)
    return (row <= col) if transposed else (col <= row)


def _mla_fwd(q, k, v, name):
    _, t, _ = q.shape
    blk = min(MLA_FWD_BLOCK, t)
    nq = t // blk

    def body(q_ref, k_ref, v_ref, o_ref, lse_ref, m_sc, acc_sc):
        qi = pl.program_id(1)
        m_sc[...] = jnp.full_like(m_sc, NEG)
        acc_sc[...] = jnp.zeros_like(acc_sc)

        def step(ki, masked):
            rows = pl.ds(pl.multiple_of(ki * blk, blk), blk)
            for hh in range(C_PAIR):
                s = lax.dot_general(q_ref[hh], k_ref[hh, rows, :], (((1,), (1,)), ((), ())), preferred_element_type=F32)
                if masked:
                    s = jnp.where(_causal_keep(blk, blk), s, NEG)
                m_prev = m_sc[hh]
                m_new = jnp.maximum(m_prev, jnp.max(s, axis=-1, keepdims=True))
                p = jnp.exp2(s - m_new)
                acc_sc[hh] = jnp.exp2(m_prev - m_new) * acc_sc[hh] + jnp.dot(p.astype(BF16), v_ref[hh, rows, :],
                                                                                preferred_element_type=F32)
                m_sc[hh] = m_new

        def below_diagonal(ki, carry):
            step(ki, False)
            return carry

        lax.fori_loop(0, qi, below_diagonal, 0)
        step(qi, True)
        outs = []
        for hh in range(C_PAIR):
            denom = acc_sc[hh, :, C_V:C_V + 1]
            outs.append(acc_sc[hh, :, :C_V] / denom)
            lse_ref[hh] = m_sc[hh] + jnp.log(denom) * LOG2E
        o_ref[...] = jnp.concatenate(outs, axis=1).astype(BF16)

    whole = lambda hp, qi: (hp, 0, 0)
    return pl.pallas_call(
        body, name=name,
        out_shape=(jax.ShapeDtypeStruct((t, C_O_COLS), BF16), jax.ShapeDtypeStruct((C_HEADS, t, 1), F32)),
        grid=(C_HEADS // C_PAIR, nq),
        in_specs=[pl.BlockSpec((C_PAIR, blk, C_PAD), lambda hp, qi: (hp, qi, 0)),
                  pl.BlockSpec((C_PAIR, t, C_PAD), whole), pl.BlockSpec((C_PAIR, t, C_PAD), whole)],
        out_specs=(pl.BlockSpec((blk, C_PAIR * C_V), lambda hp, qi: (qi, hp)),
                   pl.BlockSpec((C_PAIR, blk, 1), lambda hp, qi: (hp, qi, 0))),
        scratch_shapes=[pltpu.VMEM((C_PAIR, blk, 1), F32), pltpu.VMEM((C_PAIR, blk, C_PAD), F32)],
        compiler_params=_params(("parallel", "arbitrary")),
    )(q, k, v)


def _mla_delta(do, o, name):
    t = do.shape[0]
    blk = min(MLA_BLOCK, t)

    def body(do_ref, o_ref, dlt_ref, dob_ref):
        for hh in range(C_PAIR):
            do_h = do_ref[:, hh * C_V:(hh + 1) * C_V]
            dlt_ref[hh] = jnp.sum(do_h * o_ref[:, hh * C_V:(hh + 1) * C_V].astype(F32), axis=-1, keepdims=True)
        dob_ref[...] = do_ref[...].astype(BF16)

    wide = pl.BlockSpec((blk, C_PAIR * C_V), lambda hp, i: (i, hp))
    return pl.pallas_call(
        body, name=name,
        out_shape=(jax.ShapeDtypeStruct((C_HEADS, t, 1), F32), jax.ShapeDtypeStruct(do.shape, BF16)),
        grid=(C_HEADS // C_PAIR, t // blk), in_specs=[wide, wide],
        out_specs=(pl.BlockSpec((C_PAIR, blk, 1), lambda hp, i: (hp, i, 0)), wide),
        compiler_params=_params(("parallel", "parallel")),
    )(do, o)


def _mla_bwd(q, k, v, do_b, lse_rows, dlt_rows, name):
    _, t, _ = q.shape
    blk = min(MLA_BLOCK, t)
    nq = t // blk

    def body(q_ref, k_ref, v_ref, do_ref, lse_ref, dlt_ref, dq_hbm, dk_ref, dv_ref, dq_sc, dk_sc, dv_sc, sem):
        hp, ki = pl.program_id(0), pl.program_id(1)

        @pl.when(ki == 0)
        def _():
            dq_sc[...] = jnp.zeros_like(dq_sc)

        dk_sc[...] = jnp.zeros_like(dk_sc)
        dv_sc[...] = jnp.zeros_like(dv_sc)

        def step(qi, masked):
            rows = pl.ds(pl.multiple_of(qi * blk, blk), blk)
            for hh in range(C_PAIR):
                qb = q_ref[hh, rows, :]
                dob = do_ref[rows, hh * C_V:(hh + 1) * C_V]
                s = lax.dot_general(k_ref[hh], qb, (((1,), (1,)), ((), ())), preferred_element_type=F32)
                if masked:
                    s = jnp.where(_causal_keep(blk, blk, transposed=True), s, NEG)
                p = jnp.exp2(s - lse_ref[hh, qi])
                dp = lax.dot_general(v_ref[hh, :, :C_V], dob, (((1,), (1,)), ((), ())), preferred_element_type=F32)
                ds = (p * (dp - dlt_ref[hh, qi])).astype(BF16)
                dv_sc[hh] += jnp.dot(p.astype(BF16), dob, preferred_element_type=F32)
                dk_sc[hh] += jnp.dot(ds, qb, preferred_element_type=F32)
                dq_sc[hh, rows, :] += lax.dot_general(ds, k_ref[hh], (((0,), (0,)), ((), ())), preferred_element_type=F32)

        def above_diagonal(qi, carry):
            step(qi, False)
            return carry

        step(ki, True)
        lax.fori_loop(ki + 1, nq, above_diagonal, 0)
        dk_ref[...] = dk_sc[...] * LN2
        dv_ref[...] = dv_sc[...]

        @pl.when(ki == nq - 1)
        def _():
            dq_sc[...] = dq_sc[...] * C_SCALE
            out = pltpu.make_async_copy(dq_sc, dq_hbm.at[pl.ds(hp * C_PAIR, C_PAIR)], sem)
            out.start()
            out.wait()

    once = pl.Buffered(1)
    whole = lambda hp, ki: (hp, 0, 0)
    whole4 = lambda hp, ki: (hp, 0, 0, 0)
    kmap = lambda hp, ki: (hp, ki, 0)
    return pl.pallas_call(
        body, name=name,
        out_shape=(jax.ShapeDtypeStruct((C_HEADS, t, C_PAD), F32), jax.ShapeDtypeStruct((C_HEADS, t, C_PAD), F32),
                   jax.ShapeDtypeStruct((C_HEADS, t, C_V), F32)),
        grid=(C_HEADS // C_PAIR, nq),
        in_specs=[pl.BlockSpec((C_PAIR, t, C_PAD), whole, pipeline_mode=once), pl.BlockSpec((C_PAIR, blk, C_PAD), kmap),
                  pl.BlockSpec((C_PAIR, blk, C_PAD), kmap),
                  pl.BlockSpec((t, C_PAIR * C_V), lambda hp, ki: (0, hp), pipeline_mode=once),
                  pl.BlockSpec((C_PAIR, nq, 1, blk), whole4, pipeline_mode=once),
                  pl.BlockSpec((C_PAIR, nq, 1, blk), whole4, pipeline_mode=once)],
        out_specs=(pl.BlockSpec(memory_space=pl.ANY), pl.BlockSpec((C_PAIR, blk, C_PAD), kmap),
                   pl.BlockSpec((C_PAIR, blk, C_V), kmap)),
        scratch_shapes=[pltpu.VMEM((C_PAIR, t, C_PAD), F32), pltpu.VMEM((C_PAIR, blk, C_PAD), F32),
                        pltpu.VMEM((C_PAIR, blk, C_V), F32), pltpu.SemaphoreType.DMA(())],
        compiler_params=_params(("arbitrary", "arbitrary")),
    )(q, k, v, do_b, lse_rows, dlt_rows)


def _adamw(parts, w, m, v, name):
    layers, rows, cols = w.shape
    tm = _div_tile(rows, 256, 16)

    def body(p_ref, w_ref, m_ref, v_ref, g_ref, d_ref, nm_ref, nv_ref):
        g = p_ref[0].astype(F32)
        for j in range(1, N_DEV):
            g = g + p_ref[j].astype(F32)
        nm = ADAM_B1 * m_ref[...] + (1.0 - ADAM_B1) * g
        nv = ADAM_B2 * v_ref[...] + (1.0 - ADAM_B2) * jnp.square(g)
        m_hat = nm / (1.0 - ADAM_B1 ** ADAM_STEP)
        v_hat = nv / (1.0 - ADAM_B2 ** ADAM_STEP)
        g_ref[...] = g
        d_ref[...] = -ADAM_LR * (m_hat / (jnp.sqrt(v_hat) + ADAM_EPS) + ADAM_WD * w_ref[...])
        nm_ref[...] = nm
        nv_ref[...] = nv

    spec = pl.BlockSpec((None, tm, cols), lambda l, i: (l, i, 0))
    return pl.pallas_call(
        body, name=name, out_shape=tuple(jax.ShapeDtypeStruct(w.shape, F32) for _ in range(4)),
        grid=(layers, rows // tm),
        in_specs=[pl.BlockSpec((None, N_DEV, tm, cols), lambda l, i: (l, 0, i, 0)), spec, spec, spec],
        out_specs=(spec, spec, spec, spec), compiler_params=_params(("parallel", "parallel")),
    )(parts, w, m, v)


def _join_shards(gathered, axis):
    moved = jnp.moveaxis(gathered, 1, axis)
    shape = list(moved.shape)
    shape[axis:axis + 2] = [shape[axis] * shape[axis + 1]]
    return moved.reshape(shape)


def _split_shards(full, axis):
    shape = list(full.shape)
    shape[axis:axis + 1] = [N_DEV, shape[axis] // N_DEV]
    return jnp.moveaxis(full.reshape(shape), axis, 1)


def _as_rows(shape):
    rest = tuple(shape[1:])
    return (shape[0], 1, rest[0]) if len(rest) == 1 else (shape[0],) + rest


MIXER_WEIGHTS = {0: ['a_w_qkv', 'a_w_o'], 1: ['b_w_in', 'b_conv_w', 'b_w_out'],
                 2: ['c_w_down', 'c_q_a_norm', 'c_kv_a_norm', 'c_w_q_up', 'c_w_kv_up', 'c_w_o']}


def _layer_units(i):
    return [(n, i // N_MIXERS) for n in MIXER_WEIGHTS[i % N_MIXERS]] + [('f_w_gate_up', i), ('f_w_down', i)]


def _forward_backward(x, positions, target, local, rep):
    def gather(units):
        return _Exchange([local[n][i:i + 1].astype(BF16) if n in GATHER_BF16 else local[n][i:i + 1] for n, i in units],
                         scatter=False)

    w = {n: {} for n in SHARDED}

    def arrived(units, gathered):
        for (n, i), g in zip(units, gathered):
            full = _join_shards(g, SHARD_AXIS[n])
            w[n][i] = full if full.ndim == 2 else full[0]

    all_units = [u for i in range(DEPTH) for u in _layer_units(i)]
    first_units = _layer_units(0) + [u for u in all_units if u[0] in GATHER_F32]
    later_units = [u for u in all_units if u not in first_units]
    arrived(first_units, _exchange_now(gather(first_units), "gather_first_weights"))

    cos_a, sin_a = _rope_tables(positions, A_ROT_DIM, 0, A_HEAD_DIM - A_ROT_DIM)
    cos_c, sin_c = _rope_tables(positions, C_ROPE, C_NOPE, 0)
    saved = []
    for i in range(DEPTH):
        kind, j = i % N_MIXERS, i // N_MIXERS
        s = {'x': x}
        h1 = _rmsnorm_fwd(x, rep['mix_norm'][i:i + 1], f"mix_norm_fwd_{i}")
        s['h1'] = h1
        if kind == 0:
            s['qkv'] = _matmul(h1, w['a_w_qkv'][j], 'nn', f"a_qkv_{i}")
            s['qkv_r'] = _swa_prep_fwd(s['qkv'], rep['a_q_norm'][j:j + 1], rep['a_k_norm'][j:j + 1], cos_a, sin_a,
                                       f"a_prep_fwd_{i}")
            (s['o'], s['lse']), gathered = _swa_fwd(s['qkv_r'], rep['a_sinks'][j:j + 1], f"a_attn_fwd_{i}",
                                                    rider=gather(later_units) if i == 0 else None)
            if i == 0:
                arrived(later_units, gathered)
            x1 = _matmul(s['o'], w['a_w_o'][j], 'nn', f"a_out_{i}", residual=x)
        elif kind == 1:
            s['bcu'] = _matmul(h1, w['b_w_in'][j], 'nn', f"b_in_{i}")
            s['by'] = _sconv_fwd(s['bcu'], w['b_conv_w'][j], f"b_conv_fwd_{i}")
            x1 = _matmul(s['by'], w['b_w_out'][j], 'nn', f"b_out_{i}", residual=x)
        else:
            s['down'] = _matmul(h1, w['c_w_down'][j], 'nn', f"c_down_{i}")
            s['cq'], s['ckv'] = _mla_latent_fwd(s['down'], w['c_q_a_norm'][j], w['c_kv_a_norm'][j],
                                                f"c_latent_fwd_{i}")
            s['qw'] = _matmul(s['cq'], w['c_w_q_up'][j], 'nn', f"c_q_up_{i}")
            s['kvw'] = _matmul(s['ckv'], w['c_w_kv_up'][j], 'nn', f"c_kv_up_{i}")
            s['q'], s['k'], s['v'] = _mla_qk_fwd(s['qw'], s['kvw'], s['down'], rep['c_q_norm'][j:j + 1],
                                                 rep['c_k_norm'][j:j + 1], cos_c, sin_c, f"c_prep_fwd_{i}")
            s['o'], s['lse'] = _mla_fwd(s['q'], s['k'], s['v'], f"c_attn_fwd_{i}")
            x1 = _matmul(s['o'], w['c_w_o'][j], 'nn', f"c_out_{i}", residual=x)
        s['x1'] = x1
        s['h2'] = _rmsnorm_fwd(x1, rep['ffn_norm'][i:i + 1], f"ffn_norm_fwd_{i}")
        s['gu'] = _matmul(s['h2'], w['f_w_gate_up'][i], 'nn', f"f_gate_up_{i}", out_dtype=BF16)
        s['act'] = _swiglu_fwd(s['gu'], f"f_act_fwd_{i}")
        x = _matmul(s['act'], w['f_w_down'][i], 'nn', f"f_down_{i}", residual=x1)
        saved.append(s)

    loss, dx = _loss_head(x, target, "loss_head")

    per_layer = {n: {} for n in WEIGHTS}
    received = {}
    sent = set()

    def ready():
        units = [(n, j) for n in SHARDED for j in sorted(per_layer[n]) if (n, j) not in sent]
        if not units:
            return None, units
        sent.update(units)
        blocks = []
        for n, j in units:
            g = per_layer[n][j]
            blocks.append(_split_shards(g if n in ('c_q_a_norm', 'c_kv_a_norm') else g[None], SHARD_AXIS[n]))
        return _Exchange(blocks, scatter=True), units

    for i in reversed(range(DEPTH)):
        kind, j = i % N_MIXERS, i // N_MIXERS
        s = saved[i]
        per_layer['f_w_down'][i] = _matmul(s['act'], dx, 'tn', f"f_down_dw_{i}", out_dtype=BF16)
        dact = _matmul(dx, w['f_w_down'][i], 'nt', f"f_down_dx_{i}", out_dtype=BF16)
        dgu = _swiglu_bwd(s['gu'], dact, f"f_act_bwd_{i}")
        per_layer['f_w_gate_up'][i] = _matmul(s['h2'], dgu, 'tn', f"f_gate_up_dw_{i}", out_dtype=BF16)
        dh2 = _matmul(dgu, w['f_w_gate_up'][i], 'nt', f"f_gate_up_dx_{i}")
        dx1, per_layer['ffn_norm'][i] = _rmsnorm_bwd(s['x1'], rep['ffn_norm'][i:i + 1], dh2, dx, f"ffn_norm_bwd_{i}")
        if kind == 0:
            per_layer['a_w_o'][j] = _matmul(s['o'], dx1, 'tn', f"a_out_dw_{i}", out_dtype=BF16)
            do = _matmul(dx1, w['a_w_o'][j], 'nt', f"a_out_dx_{i}")
            rider, units = ready()
            (dqkv_r, per_layer['a_sinks'][j]), parts = _swa_bwd(s['qkv_r'], s['o'], s['lse'], do, rep['a_sinks'][j:j + 1],
                                                                f"a_attn_bwd_{i}", rider=rider)
            received.update(zip(units, parts or ()))
            dqkv, per_layer['a_q_norm'][j], per_layer['a_k_norm'][j] = _swa_prep_bwd(
                s['qkv'], dqkv_r, rep['a_q_norm'][j:j + 1], rep['a_k_norm'][j:j + 1], cos_a, sin_a, f"a_prep_bwd_{i}")
            per_layer['a_w_qkv'][j] = _matmul(s['h1'], dqkv, 'tn', f"a_qkv_dw_{i}", out_dtype=BF16)
            dh1 = _matmul(dqkv, w['a_w_qkv'][j], 'nt', f"a_qkv_dx_{i}")
        elif kind == 1:
            per_layer['b_w_out'][j] = _matmul(s['by'], dx1, 'tn', f"b_out_dw_{i}", out_dtype=BF16)
            dby = _matmul(dx1, w['b_w_out'][j], 'nt', f"b_out_dx_{i}")
            dbcu, per_layer['b_conv_w'][j] = _sconv_bwd(s['bcu'], dby, w['b_conv_w'][j], f"b_conv_bwd_{i}")
            per_layer['b_w_in'][j] = _matmul(s['h1'], dbcu, 'tn', f"b_in_dw_{i}", out_dtype=BF16)
            dh1 = _matmul(dbcu, w['b_w_in'][j], 'nt', f"b_in_dx_{i}")
        else:
            per_layer['c_w_o'][j] = _matmul(s['o'], dx1, 'tn', f"c_out_dw_{i}", out_dtype=BF16)
            do = _matmul(dx1, w['c_w_o'][j], 'nt', f"c_out_dx_{i}")
            dlt, do_b = _mla_delta(do, s['o'], f"c_attn_delta_{i}")
            blk = min(MLA_BLOCK, do.shape[0])
            as_rows = lambda col: col.reshape(C_HEADS, do.shape[0] // blk, 1, blk)
            dq, dk, dv = _mla_bwd(s['q'], s['k'], s['v'], do_b, as_rows(s['lse']), as_rows(dlt), f"c_attn_bwd_{i}")
            rider, units = ready()
            (dqw, dkvw, dkrope, per_layer['c_q_norm'][j], per_layer['c_k_norm'][j]), parts = _mla_qk_bwd(
                s['qw'], s['kvw'], s['down'], dq, dk, dv, rep['c_q_norm'][j:j + 1], rep['c_k_norm'][j:j + 1], cos_c, sin_c,
                f"c_prep_bwd_{i}", rider=rider)
            received.update(zip(units, parts or ()))
            per_layer['c_w_q_up'][j] = _matmul(s['cq'], dqw, 'tn', f"c_q_up_dw_{i}", out_dtype=BF16)
            dcq = _matmul(dqw, w['c_w_q_up'][j], 'nt', f"c_q_up_dx_{i}")
            per_layer['c_w_kv_up'][j] = _matmul(s['ckv'], dkvw, 'tn', f"c_kv_up_dw_{i}", out_dtype=BF16)
            dckv = _matmul(dkvw, w['c_w_kv_up'][j], 'nt', f"c_kv_up_dx_{i}")
            ddown, per_layer['c_q_a_norm'][j], per_layer['c_kv_a_norm'][j] = _mla_latent_bwd(
                s['down'], dcq, dckv, dkrope, w['c_q_a_norm'][j], w['c_kv_a_norm'][j], f"c_latent_bwd_{i}")
            per_layer['c_w_down'][j] = _matmul(s['h1'], ddown, 'tn', f"c_down_dw_{i}", out_dtype=BF16)
            dh1 = _matmul(ddown, w['c_w_down'][j], 'nt', f"c_down_dx_{i}")
        dx, per_layer['mix_norm'][i] = _rmsnorm_bwd(s['x'], rep['mix_norm'][i:i + 1], dh1, dx1, f"mix_norm_bwd_{i}")

    last, units = ready()
    received.update(zip(units, _exchange_now(last, "scatter_last_gradients")))
    parts = {n: jnp.concatenate([received[(n, j)] for j in sorted(per_layer[n])], axis=0) for n in SHARDED}
    small = {}
    for n in REPLICATED:
        stacked = jnp.stack([per_layer[n][j] for j in sorted(per_layer[n])])
        small[n] = stacked.reshape(stacked.shape[0], stacked.shape[-1])
    return loss, dx, parts, small


def kernel(x, positions, mix_norm, ffn_norm, a_w_qkv, a_q_norm, a_k_norm, a_sinks, a_w_o, b_w_in, b_conv_w, b_w_out, c_w_down, c_q_a_norm, c_kv_a_norm, c_w_q_up, c_w_kv_up, c_q_norm, c_k_norm, c_w_o, f_w_gate_up, f_w_down, loss_target, m_mix_norm, m_ffn_norm, m_a_w_qkv, m_a_q_norm, m_a_k_norm, m_a_sinks, m_a_w_o, m_b_w_in, m_b_conv_w, m_b_w_out, m_c_w_down, m_c_q_a_norm, m_c_kv_a_norm, m_c_w_q_up, m_c_w_kv_up, m_c_q_norm, m_c_k_norm, m_c_w_o, m_f_w_gate_up, m_f_w_down, v_mix_norm, v_ffn_norm, v_a_w_qkv, v_a_q_norm, v_a_k_norm, v_a_sinks, v_a_w_o, v_b_w_in, v_b_conv_w, v_b_w_out, v_c_w_down, v_c_q_a_norm, v_c_kv_a_norm, v_c_w_q_up, v_c_w_kv_up, v_c_q_norm, v_c_k_norm, v_c_w_o, v_f_w_gate_up, v_f_w_down):
    local = dict(mix_norm=mix_norm, ffn_norm=ffn_norm, a_w_qkv=a_w_qkv, a_q_norm=a_q_norm, a_k_norm=a_k_norm, a_sinks=a_sinks, a_w_o=a_w_o, b_w_in=b_w_in, b_conv_w=b_conv_w, b_w_out=b_w_out, c_w_down=c_w_down, c_q_a_norm=c_q_a_norm, c_kv_a_norm=c_kv_a_norm, c_w_q_up=c_w_q_up, c_w_kv_up=c_w_kv_up, c_q_norm=c_q_norm, c_k_norm=c_k_norm, c_w_o=c_w_o, f_w_gate_up=f_w_gate_up, f_w_down=f_w_down)
    mom1 = dict(mix_norm=m_mix_norm, ffn_norm=m_ffn_norm, a_w_qkv=m_a_w_qkv, a_q_norm=m_a_q_norm, a_k_norm=m_a_k_norm, a_sinks=m_a_sinks, a_w_o=m_a_w_o, b_w_in=m_b_w_in, b_conv_w=m_b_conv_w, b_w_out=m_b_w_out, c_w_down=m_c_w_down, c_q_a_norm=m_c_q_a_norm, c_kv_a_norm=m_c_kv_a_norm, c_w_q_up=m_c_w_q_up, c_w_kv_up=m_c_w_kv_up, c_q_norm=m_c_q_norm, c_k_norm=m_c_k_norm, c_w_o=m_c_w_o, f_w_gate_up=m_f_w_gate_up, f_w_down=m_f_w_down)
    mom2 = dict(mix_norm=v_mix_norm, ffn_norm=v_ffn_norm, a_w_qkv=v_a_w_qkv, a_q_norm=v_a_q_norm, a_k_norm=v_a_k_norm, a_sinks=v_a_sinks, a_w_o=v_a_w_o, b_w_in=v_b_w_in, b_conv_w=v_b_conv_w, b_w_out=v_b_w_out, c_w_down=v_c_w_down, c_q_a_norm=v_c_q_a_norm, c_kv_a_norm=v_c_kv_a_norm, c_w_q_up=v_c_w_q_up, c_w_kv_up=v_c_w_kv_up, c_q_norm=v_c_q_norm, c_k_norm=v_c_k_norm, c_w_o=v_c_w_o, f_w_gate_up=v_f_w_gate_up, f_w_down=v_f_w_down)
    t, d = x.shape[1], x.shape[2]

    rep = {n: local[n] for n in REPLICATED}
    loss, grad_x, parts, small = _forward_backward(x.reshape(t, d), positions.reshape(t), loss_target.reshape(t, d),
                                                   {n: local[n] for n in SHARDED}, rep)

    out_g, out_d, out_m, out_v = {}, {}, {}, {}

    def update(names, parts):
        for n, part in zip(names, parts):
            shape = local[n].shape if n in SHARD_AXIS else (1,) + local[n].shape
            view = _as_rows(shape)
            results = _adamw(part.reshape(view[0], N_DEV, view[1], view[2]),
                             *[src[n].reshape(view) for src in (local, mom1, mom2)], name="adamw_" + n)
            for dst, res in zip((out_g, out_d, out_m, out_v), results):
                dst[n] = res.reshape(local[n].shape)

    update(SHARDED, [parts[n] for n in SHARDED])
    update(REPLICATED, _exchange_now(_Exchange([small[n].reshape((1,) + small[n].shape) for n in REPLICATED],
                                               scatter=False), "gather_small_gradients"))

    loss = lax.psum(loss.reshape(()), MESH_AXES)
    outs = [loss, grad_x.reshape(1, t, d)]
    for res in (out_g, out_d, out_m, out_v):
        outs += [res[n] for n in WEIGHTS]
    return tuple(outs)
```

```python
import jax
import jax.numpy as jnp
import numpy as np
from jax import lax
from jax.experimental import pallas as pl
from jax.experimental.pallas import tpu as pltpu

F32 = jnp.float32
BF16 = jnp.bfloat16

N_DEV = 8
MESH_AXES = ("x", "y", "c")

DEPTH = 4
N_MIXERS = 3
ROPE_THETA = 500000.0
EPS = 1e-6
A_HEADS, A_KV_HEADS, A_HEAD_DIM, A_ROT_DIM, A_WINDOW = 16, 4, 64, 16, 128
A_GROUP = A_HEADS // A_KV_HEADS
C_HEADS, C_NOPE, C_ROPE, C_V, C_Q_RANK, C_KV_RANK = 16, 64, 32, 64, 384, 256
C_QK = C_NOPE + C_ROPE
ADAM_LR, ADAM_B1, ADAM_B2, ADAM_EPS, ADAM_WD, ADAM_STEP = 0.001, 0.9, 0.999, 1e-08, 0.01, 10

VMEM_LIMIT_BYTES = 48 * 1024 * 1024
LANES = 128
NEG = -1e30
MLA_BLOCK = 512
MLA_FWD_BLOCK = 1024

WEIGHTS = ['mix_norm', 'ffn_norm', 'a_w_qkv', 'a_q_norm', 'a_k_norm', 'a_sinks', 'a_w_o', 'b_w_in', 'b_conv_w', 'b_w_out',
           'c_w_down', 'c_q_a_norm', 'c_kv_a_norm', 'c_w_q_up', 'c_w_kv_up', 'c_q_norm', 'c_k_norm', 'c_w_o', 'f_w_gate_up',
           'f_w_down']
SHARD_AXIS = {'a_w_qkv': 2, 'a_w_o': 1, 'b_w_in': 2, 'b_conv_w': 2, 'b_w_out': 1, 'c_w_down': 1, 'c_q_a_norm': 1,
              'c_kv_a_norm': 1, 'c_w_q_up': 2, 'c_w_kv_up': 2, 'c_w_o': 1, 'f_w_gate_up': 2, 'f_w_down': 1}
SHARDED = [n for n in WEIGHTS if n in SHARD_AXIS]
REPLICATED = [n for n in WEIGHTS if n not in SHARD_AXIS]
GATHER_F32 = ['b_conv_w', 'c_q_a_norm', 'c_kv_a_norm']
GATHER_BF16 = [n for n in SHARDED if n not in GATHER_F32]

def _params(semantics=None):
    return pltpu.CompilerParams(dimension_semantics=semantics, vmem_limit_bytes=VMEM_LIMIT_BYTES)


def _div_tile(n, cap, mult=LANES):
    best = None
    t = mult
    while t <= min(n, cap):
        if n % t == 0:
            best = t
        t += mult
    return n if best is None else best


ANY_SPEC = pl.BlockSpec(memory_space=pl.ANY)


class _Exchange:
    def __init__(self, arrays, scatter):
        self.arrays, self.scatter = list(arrays), scatter
        n = len(self.arrays)
        self.out_shapes = [jax.ShapeDtypeStruct(a.shape if scatter else (a.shape[0], N_DEV) + tuple(a.shape[1:]), a.dtype)
                           for a in self.arrays]
        self.scratch = [pltpu.SemaphoreType.DMA((n, N_DEV - 1)), pltpu.SemaphoreType.DMA((n, N_DEV - 1)),
                        pltpu.SemaphoreType.DMA((n,))]

    def _copies(self, src_refs, out_refs, sems):
        send_sems, recv_sems, local_sems = sems
        x, y, c = lax.axis_index("x"), lax.axis_index("y"), lax.axis_index("c")
        me_idx = 4 * x + 2 * y + c
        n = len(self.arrays)

        def remote(a, k, src, dst, to):
            return pltpu.make_async_remote_copy(src_ref=src, dst_ref=dst, send_sem=send_sems.at[a, k],
                                                recv_sem=recv_sems.at[a, k], device_id=to,
                                                device_id_type=pl.DeviceIdType.MESH)

        local, first, forwards, last = [], [], [], []
        if self.scatter:
            for a in range(n):
                local.append(pltpu.make_async_copy(src_refs[a].at[:, me_idx], out_refs[a].at[:, me_idx], local_sems.at[a]))
                for r in range(1, N_DEV):
                    px = 1 - x if (r >> 2) & 1 else x
                    py = 1 - y if (r >> 1) & 1 else y
                    pc = 1 - c if r & 1 else c
                    cp = remote(a, r - 1, src_refs[a].at[:, 4 * px + 2 * py + pc], out_refs[a].at[:, me_idx], (px, py, pc))
                    first.append(cp)
                    last.append(cp)
            return local, first, forwards, last
        me, sibling = (x, y, c), (x, y, 1 - c)
        chips = [(1 - x, y), (x, 1 - y), (1 - x, 1 - y)]

        def place(a, block):
            return out_refs[a].at[:, 4 * block[0] + 2 * block[1] + block[2]]

        for a in range(n):
            local.append(pltpu.make_async_copy(src_refs[a], place(a, me), local_sems.at[a]))
            first.append(remote(a, 0, src_refs[a], place(a, me), sibling))
            last.append(remote(a, 0, place(a, sibling), place(a, sibling), me))
            for j, chip in enumerate(chips):
                first.append(remote(a, 1 + j, src_refs[a], place(a, me), (*chip, c)))
                forwards.append((remote(a, 1 + j, place(a, (*chip, c)), place(a, (*chip, c)), me),
                                 remote(a, 4 + j, place(a, (*chip, c)), place(a, (*chip, c)), sibling)))
                last.append(remote(a, 4 + j, place(a, (*chip, 1 - c)), place(a, (*chip, 1 - c)), me))
        return local, first, forwards, last

    def start(self, src_refs, out_refs, sems):
        local, first, _, _ = self._copies(src_refs, out_refs, sems)
        for cp in local + first:
            cp.start()

    def finish(self, src_refs, out_refs, sems):
        local, first, forwards, last = self._copies(src_refs, out_refs, sems)
        for arrival, forward in forwards:
            arrival.wait_recv()
            forward.start()
        for cp in last:
            cp.wait_recv()
        for cp in first + [forward for _, forward in forwards]:
            cp.wait_send()
        for cp in local:
            cp.wait()


def _exchange_now(exchange, name):
    n = len(exchange.arrays)

    def body(*refs):
        exchange.start(refs[:n], refs[n:2 * n], refs[2 * n:])
        exchange.finish(refs[:n], refs[n:2 * n], refs[2 * n:])

    return pl.pallas_call(
        body, name=name, out_shape=tuple(exchange.out_shapes), in_specs=[ANY_SPEC] * n, out_specs=(ANY_SPEC,) * n,
        scratch_shapes=exchange.scratch,
    )(*exchange.arrays)


def _host_call(body, rider, name, out_shape, grid, in_specs, out_specs, operands, semantics):
    if rider is None:
        return pl.pallas_call(body, name=name, out_shape=tuple(out_shape), grid=grid, in_specs=list(in_specs),
                              out_specs=tuple(out_specs), compiler_params=_params(semantics))(*operands), None
    n_in, n_out, r = len(in_specs), len(out_shape), len(rider.arrays)

    def riding(*refs):
        ins, rider_in = refs[:n_in], refs[n_in:n_in + r]
        outs, rider_out = refs[n_in + r:n_in + r + n_out], refs[n_in + r + n_out:n_in + 2 * r + n_out]
        sems = refs[n_in + 2 * r + n_out:]
        step = pl.program_id(0)

        @pl.when(step == 0)
        def _():
            rider.start(rider_in, rider_out, sems)

        body(*ins, *outs)

        @pl.when(step == grid[0] - 1)
        def _():
            rider.finish(rider_in, rider_out, sems)

    results = pl.pallas_call(
        riding, name=name, out_shape=tuple(out_shape) + tuple(rider.out_shapes), grid=grid,
        in_specs=list(in_specs) + [ANY_SPEC] * r, out_specs=tuple(out_specs) + (ANY_SPEC,) * r,
        scratch_shapes=rider.scratch, compiler_params=_params(("arbitrary",)),
    )(*operands, *rider.arrays)
    return results[:n_out], results[n_out:]


def _matmul(a, b, mode, name, out_dtype=F32, residual=None):
    if mode == 'nn':
        (m, k), (k2, n) = a.shape, b.shape
    elif mode == 'nt':
        (m, k), (n, k2) = a.shape, b.shape
    else:
        (k, m), (k2, n) = a.shape, b.shape
    assert k == k2, (name, a.shape, b.shape, mode)
    if mode == 'tn':
        tm, tk = _div_tile(m, 1408), _div_tile(k, 512, 16)
    else:
        tm, tk = _div_tile(m, 1024, 16), _div_tile(k, 1536)
    tn = _div_tile(n, 1408)
    nk = k // tk
    dims = {'nn': (((1,), (0,)), ((), ())), 'nt': (((1,), (1,)), ((), ())), 'tn': (((0,), (0,)), ((), ()))}[mode]

    def product(a_ref, b_ref):
        return lax.dot_general(a_ref[...].astype(BF16), b_ref[...].astype(BF16), dims, preferred_element_type=F32)

    def finish(r, rest):
        if residual is not None:
            r = r + rest[0][...]
        rest[-1 if nk == 1 else -2][...] = r.astype(out_dtype)

    def body_single(a_ref, b_ref, *rest):
        finish(product(a_ref, b_ref), rest)

    def body_accumulate(a_ref, b_ref, *rest):
        acc = rest[-1]
        kk = pl.program_id(2)

        @pl.when(kk == 0)
        def _():
            acc[...] = jnp.zeros_like(acc)

        acc[...] += product(a_ref, b_ref)

        @pl.when(kk == nk - 1)
        def _():
            finish(acc[...], rest)

    a_spec = pl.BlockSpec((tk, tm), lambda i, j, kk: (kk, i)) if mode == 'tn' else pl.BlockSpec((tm, tk), lambda i, j, kk: (i, kk))
    b_spec = pl.BlockSpec((tn, tk), lambda i, j, kk: (j, kk)) if mode == 'nt' else pl.BlockSpec((tk, tn), lambda i, j, kk: (kk, j))
    o_spec = pl.BlockSpec((tm, tn), lambda i, j, kk: (i, j))
    in_specs, operands = [a_spec, b_spec], [a, b]
    if residual is not None:
        in_specs.append(o_spec)
        operands.append(residual)
    return pl.pallas_call(
        body_single if nk == 1 else body_accumulate, name=name, out_shape=jax.ShapeDtypeStruct((m, n), out_dtype),
        grid=(m // tm, n // tn, nk), in_specs=in_specs, out_specs=o_spec,
        scratch_shapes=[] if nk == 1 else [pltpu.VMEM((tm, tn), F32)],
        compiler_params=_params(("parallel", "parallel", "arbitrary")),
    )(*operands)


def _row_spec(tm, cols):
    return pl.BlockSpec((tm, cols), lambda i: (i, 0))


def _const_spec(shape):
    return pl.BlockSpec(shape, lambda i: tuple(0 for _ in shape))


def _accumulate(ref, value, step):
    @pl.when(step == 0)
    def _():
        ref[...] = value

    @pl.when(step > 0)
    def _():
        ref[...] += value


def _rstd(x):
    return lax.rsqrt(jnp.mean(x * x, axis=-1, keepdims=True) + EPS)


def _norm_bwd(x, g, dout):
    xn = x * _rstd(x)
    dg = jnp.sum(dout * xn, axis=0, keepdims=True)
    dxn = dout * g
    dx = _rstd(x) * (dxn - xn * jnp.mean(dxn * xn, axis=-1, keepdims=True))
    return dx, dg


def _rmsnorm_fwd(x, g, name):
    t, d = x.shape
    tm = _div_tile(t, 512, 16)

    def body(x_ref, g_ref, o_ref):
        xv = x_ref[...]
        o_ref[...] = (xv * _rstd(xv) * g_ref[...]).astype(BF16)

    return pl.pallas_call(
        body, name=name, out_shape=jax.ShapeDtypeStruct((t, d), BF16), grid=(t // tm,),
        in_specs=[_row_spec(tm, d), _const_spec((1, d))], out_specs=_row_spec(tm, d),
        compiler_params=_params(("parallel",)),
    )(x, g)


def _rmsnorm_bwd(x, g, dh, dres, name):
    t, d = x.shape
    tm = _div_tile(t, 512, 8)

    def body(x_ref, g_ref, dh_ref, dres_ref, dx_ref, dg_ref):
        dx, dg = _norm_bwd(x_ref[...], g_ref[...], dh_ref[...])
        dx_ref[...] = dres_ref[...] + dx
        _accumulate(dg_ref, dg, pl.program_id(0))

    return pl.pallas_call(
        body, name=name,
        out_shape=(jax.ShapeDtypeStruct((t, d), F32), jax.ShapeDtypeStruct((1, d), F32)), grid=(t // tm,),
        in_specs=[_row_spec(tm, d), _const_spec((1, d)), _row_spec(tm, d), _row_spec(tm, d)],
        out_specs=(_row_spec(tm, d), _const_spec((1, d))),
        compiler_params=_params(("arbitrary",)),
    )(x, g, dh, dres)


def _sigmoid(x):
    return 0.5 * jnp.tanh(0.5 * x) + 0.5


def _swiglu_fwd(gu, name):
    t, f2 = gu.shape
    f = f2 // 2
    tm = _div_tile(t, 512, 16)

    def body(gu_ref, o_ref):
        gate, up = gu_ref[:, :f].astype(F32), gu_ref[:, f:].astype(F32)
        o_ref[...] = (gate * _sigmoid(gate) * up).astype(BF16)

    return pl.pallas_call(
        body, name=name, out_shape=jax.ShapeDtypeStruct((t, f), BF16), grid=(t // tm,),
        in_specs=[_row_spec(tm, f2)], out_specs=_row_spec(tm, f),
        compiler_params=_params(("parallel",)),
    )(gu)


def _swiglu_bwd(gu, da, name):
    t, f2 = gu.shape
    f = f2 // 2
    tm = _div_tile(t, 512, 16)

    def body(gu_ref, da_ref, o_ref):
        gate, up, dav = gu_ref[:, :f].astype(F32), gu_ref[:, f:].astype(F32), da_ref[...].astype(F32)
        sig = _sigmoid(gate)
        o_ref[:, :f] = (dav * up * (sig * (1.0 + gate * (1.0 - sig)))).astype(BF16)
        o_ref[:, f:] = (dav * (gate * sig)).astype(BF16)

    return pl.pallas_call(
        body, name=name, out_shape=jax.ShapeDtypeStruct((t, f2), BF16), grid=(t // tm,),
        in_specs=[_row_spec(tm, f2), _row_spec(tm, f)], out_specs=_row_spec(tm, f2),
        compiler_params=_params(("parallel",)),
    )(gu, da)


def _loss_head(y, target, name):
    t, d = y.shape
    tm = _div_tile(t, 512, 8)

    def body(y_ref, t_ref, loss_ref, dy_ref):
        diff = y_ref[...] - t_ref[...]
        dy_ref[...] = diff * (1.0 / d)
        part = jnp.sum(jnp.sum(diff * diff, axis=1, keepdims=True), axis=0, keepdims=True) * (0.5 / d)
        _accumulate(loss_ref, part, pl.program_id(0))

    return pl.pallas_call(
        body, name=name,
        out_shape=(jax.ShapeDtypeStruct((1, 1), F32), jax.ShapeDtypeStruct((t, d), F32)), grid=(t // tm,),
        in_specs=[_row_spec(tm, d), _row_spec(tm, d)], out_specs=(_const_spec((1, 1)), _row_spec(tm, d)),
        compiler_params=_params(("arbitrary",)),
    )(y, target)


HALO = 8


def _shift_down(z, k, halo_rows):
    tm = z.shape[0]
    row = lax.broadcasted_iota(jnp.int32, z.shape, 0)
    out = pltpu.roll(z, k, 0)
    for j in range(k):
        out = jnp.where(row == j, halo_rows[HALO - k + j:HALO - k + j + 1, :], out)
    return out


def _shift_up(z, k, halo_rows):
    tm = z.shape[0]
    row = lax.broadcasted_iota(jnp.int32, z.shape, 0)
    out = pltpu.roll(z, tm - k, 0)
    for j in range(k):
        out = jnp.where(row == tm - k + j, halo_rows[j:j + 1, :], out)
    return out


def _sconv_specs(t, tm, cols):
    per = tm // HALO
    last = t // HALO - 1
    cur = pl.BlockSpec((tm, cols), lambda i: (i, 0))
    prev = pl.BlockSpec((HALO, cols), lambda i: (jnp.maximum(i * per - 1, 0), 0))
    nxt = pl.BlockSpec((HALO, cols), lambda i: (jnp.minimum((i + 1) * per, last), 0))
    return cur, prev, nxt


def _sconv_fwd(bcu, conv_w, name):
    t, d3 = bcu.shape
    d = d3 // 3
    tm = _div_tile(t, 256, 16)
    cur, prev, _ = _sconv_specs(t, tm, d3)

    def body(cur_ref, prev_ref, w_ref, o_ref):
        i = pl.program_id(0)
        z = cur_ref[:, d:2 * d] * cur_ref[:, 2 * d:]
        zp = prev_ref[:, d:2 * d] * prev_ref[:, 2 * d:] * (i > 0).astype(F32)
        y = w_ref[0:1, :] * _shift_down(z, 2, zp) + w_ref[1:2, :] * _shift_down(z, 1, zp) + w_ref[2:3, :] * z
        o_ref[...] = (cur_ref[:, :d] * y).astype(BF16)

    return pl.pallas_call(
        body, name=name, out_shape=jax.ShapeDtypeStruct((t, d), BF16), grid=(t // tm,),
        in_specs=[cur, prev, _const_spec((3, d))], out_specs=_row_spec(tm, d),
        compiler_params=_params(("parallel",)),
    )(bcu, bcu, conv_w)


def _sconv_bwd(bcu, dout, conv_w, name):
    t, d3 = bcu.shape
    d = d3 // 3
    tm = _div_tile(t, 256, 16)
    cur, prev, nxt = _sconv_specs(t, tm, d3)
    dcur, _, dnxt = _sconv_specs(t, tm, d)
    n_tiles = t // tm

    def body(cur_ref, prev_ref, nxt_ref, do_ref, don_ref, w_ref, o_ref, dw_ref):
        i = pl.program_id(0)
        b, cg, u = cur_ref[:, :d], cur_ref[:, d:2 * d], cur_ref[:, 2 * d:]
        z = cg * u
        zp = prev_ref[:, d:2 * d] * prev_ref[:, 2 * d:] * (i > 0).astype(F32)
        z1, z2 = _shift_down(z, 1, zp), _shift_down(z, 2, zp)
        w0, w1, w2 = w_ref[0:1, :], w_ref[1:2, :], w_ref[2:3, :]
        y = w0 * z2 + w1 * z1 + w2 * z
        dov = do_ref[...]
        dy = dov * b
        dyn = don_ref[...] * nxt_ref[:, :d] * (i < n_tiles - 1).astype(F32)
        dz = w2 * dy + w1 * _shift_up(dy, 1, dyn) + w0 * _shift_up(dy, 2, dyn)
        o_ref[:, :d] = (dov * y).astype(BF16)
        o_ref[:, d:2 * d] = (dz * u).astype(BF16)
        o_ref[:, 2 * d:] = (dz * cg).astype(BF16)
        dw = jnp.concatenate([jnp.sum(dy * z2, axis=0, keepdims=True), jnp.sum(dy * z1, axis=0, keepdims=True),
                              jnp.sum(dy * z, axis=0, keepdims=True)], axis=0)
        _accumulate(dw_ref, dw, i)

    return pl.pallas_call(
        body, name=name,
        out_shape=(jax.ShapeDtypeStruct((t, d3), BF16), jax.ShapeDtypeStruct((3, d), F32)), grid=(n_tiles,),
        in_specs=[cur, prev, nxt, dcur, dnxt, _const_spec((3, d))],
        out_specs=(_row_spec(tm, d3), _const_spec((3, d))),
        compiler_params=_params(("arbitrary",)),
    )(bcu, bcu, bcu, dout, dout, conv_w)


def _rope_tables(positions, rot, lead, trail):
    inv_freq = ROPE_THETA ** (-jnp.arange(0, rot, 2, dtype=F32) / rot)
    ang = positions.astype(F32)[:, None] * inv_freq
    cos, sin = jnp.cos(ang), jnp.sin(ang)
    t = positions.shape[0]
    cos_full = jnp.concatenate([jnp.ones((t, lead), F32), cos, cos, jnp.ones((t, trail), F32)], axis=1)
    sin_full = jnp.concatenate([jnp.zeros((t, lead), F32), -sin, sin, jnp.zeros((t, trail), F32)], axis=1)
    return cos_full, sin_full


def _pieces_dot(a, b, pieces):
    total, rest = None, a
    for _ in range(pieces):
        piece = rest.astype(BF16)
        term = jnp.dot(piece, b, preferred_element_type=F32)
        total = term if total is None else total + term
        rest = rest - piece.astype(F32)
    return total


A_Q_COLS = A_HEADS * A_HEAD_DIM
A_KV_COLS = A_KV_HEADS * A_HEAD_DIM
A_COLS = A_Q_COLS + 2 * A_KV_COLS
A_SCALE = A_HEAD_DIM ** -0.5


A_NORMED = A_Q_COLS + A_KV_COLS


def _swa_prep_tables(q_norm, k_norm, cos, sin):
    lane = np.arange(A_NORMED)
    seg = (lane[:, None] // A_HEAD_DIM == np.arange(LANES)[None, :]).astype(np.float32)
    fold = (np.where(lane < A_Q_COLS, 0, A_HEAD_DIM)[:, None] + lane[:, None] % A_HEAD_DIM
            == np.arange(LANES)[None, :]).astype(np.float32)
    heads = A_HEADS + A_KV_HEADS
    gains = jnp.concatenate([jnp.tile(q_norm, (1, A_HEADS)), jnp.tile(k_norm, (1, A_KV_HEADS))], axis=1)
    return (gains, jnp.tile(cos, (1, heads)), jnp.tile(sin, (1, heads)),
            jnp.asarray(seg, BF16), jnp.asarray(seg.T, BF16), jnp.asarray(fold, BF16))


def _wide_rstd(x, seg, seg_t):
    mean_sq = _pieces_dot(x * x, seg, 1) * (1.0 / A_HEAD_DIM)
    return _pieces_dot(lax.rsqrt(mean_sq + EPS), seg_t, 2)


def _wide_partner(n):
    dim = lax.broadcasted_iota(jnp.int32, n.shape, 1) & (A_HEAD_DIM - 1)
    half = A_ROT_DIM // 2
    return jnp.where(dim < half, pltpu.roll(n, A_NORMED - half, 1),
                     jnp.where(dim < A_ROT_DIM, pltpu.roll(n, half, 1), 0.0))


def _swa_prep_fwd(qkv, q_norm, k_norm, cos, sin, name):
    t = qkv.shape[0]
    tm = _div_tile(t, 256, 16)
    gains, cos_w, sin_w, seg, seg_t, _ = _swa_prep_tables(q_norm, k_norm, cos, sin)

    def body(x_ref, g_ref, cos_ref, sin_ref, seg_ref, segt_ref, o_ref):
        x = x_ref[:, :A_NORMED]
        n = x * _wide_rstd(x, seg_ref[...], segt_ref[...]) * g_ref[...]
        o_ref[:, :A_NORMED] = (n * cos_ref[...] + _wide_partner(n) * sin_ref[...]).astype(BF16)
        o_ref[:, A_NORMED:] = x_ref[:, A_NORMED:].astype(BF16)

    return pl.pallas_call(
        body, name=name, out_shape=jax.ShapeDtypeStruct((t, A_COLS), BF16), grid=(t // tm,),
        in_specs=[_row_spec(tm, A_COLS), _const_spec((1, A_NORMED)), _row_spec(tm, A_NORMED), _row_spec(tm, A_NORMED),
                  _const_spec(seg.shape), _const_spec(seg_t.shape)],
        out_specs=_row_spec(tm, A_COLS), compiler_params=_params(("parallel",)),
    )(qkv, gains, cos_w, sin_w, seg, seg_t)


def _swa_prep_bwd(qkv, dqkv_r, q_norm, k_norm, cos, sin, name):
    t = qkv.shape[0]
    tm = _div_tile(t, 256, 16)
    hd = A_HEAD_DIM
    gains, cos_w, sin_w, seg, seg_t, fold = _swa_prep_tables(q_norm, k_norm, cos, sin)

    def body(x_ref, d_ref, g_ref, cos_ref, sin_ref, seg_ref, segt_ref, fold_ref, o_ref, dgq_ref, dgk_ref):
        x, dout = x_ref[:, :A_NORMED], d_ref[:, :A_NORMED]
        rstd = _wide_rstd(x, seg_ref[...], segt_ref[...])
        xn = x * rstd
        dn = dout * cos_ref[...] + _wide_partner(dout * sin_ref[...])
        dg = _pieces_dot(jnp.sum(dn * xn, axis=0, keepdims=True), fold_ref[...], 3)
        dxn = dn * g_ref[...]
        mean = _pieces_dot(_pieces_dot(dxn * xn, seg_ref[...], 1) * (1.0 / hd), segt_ref[...], 2)
        o_ref[:, :A_NORMED] = (rstd * (dxn - xn * mean)).astype(BF16)
        o_ref[:, A_NORMED:] = d_ref[:, A_NORMED:].astype(BF16)
        _accumulate(dgq_ref, dg[:, :hd], pl.program_id(0))
        _accumulate(dgk_ref, dg[:, hd:2 * hd], pl.program_id(0))

    return pl.pallas_call(
        body, name=name,
        out_shape=(jax.ShapeDtypeStruct((t, A_COLS), BF16), jax.ShapeDtypeStruct((1, hd), F32),
                   jax.ShapeDtypeStruct((1, hd), F32)),
        grid=(t // tm,),
        in_specs=[_row_spec(tm, A_COLS), _row_spec(tm, A_COLS), _const_spec((1, A_NORMED)), _row_spec(tm, A_NORMED),
                  _row_spec(tm, A_NORMED), _const_spec(seg.shape), _const_spec(seg_t.shape), _const_spec(fold.shape)],
        out_specs=(_row_spec(tm, A_COLS), _const_spec((1, hd)), _const_spec((1, hd))),
        compiler_params=_params(("arbitrary",)),
    )(qkv, dqkv_r, gains, cos_w, sin_w, seg, seg_t, fold)


def _group_rows(ref, k, width=A_HEAD_DIM, base=0):
    return jnp.concatenate([ref[:, base + (A_GROUP * k + g) * width:base + (A_GROUP * k + g + 1) * width]
                            for g in range(A_GROUP)], axis=0)


def _group_column(ref, k, rows):
    cols = []
    for g in range(A_GROUP):
        h = A_GROUP * k + g
        col = ref[:, h:h + 1]
        cols.append(jnp.broadcast_to(col, (rows, 1)) if col.shape[0] == 1 else col)
    return jnp.concatenate(cols, axis=0)


def _swa_fwd(qkv_r, sinks, name, rider=None):
    t = qkv_r.shape[0]
    blk = A_WINDOW
    nb = t // blk
    hd = A_HEAD_DIM
    kv_block = A_Q_COLS // (2 * A_KV_COLS)

    def body(q_ref, kvc_ref, kvp_ref, s_ref, o_ref, lse_ref):
        n = pl.program_id(0)
        shape = (A_GROUP * blk, 2 * blk)
        qpos = lax.broadcasted_iota(jnp.int32, shape, 0) & (blk - 1)
        col = lax.broadcasted_iota(jnp.int32, shape, 1)
        delta = qpos + blk - col
        valid = (delta >= 0) & (delta < A_WINDOW) & ((col >= blk) | (n > 0))
        for k in range(A_KV_HEADS):
            qg = _group_rows(q_ref, k)
            kw = jnp.concatenate([kvp_ref[:, k * hd:(k + 1) * hd], kvc_ref[:, k * hd:(k + 1) * hd]], axis=0)
            vw = jnp.concatenate([kvp_ref[:, A_KV_COLS + k * hd:A_KV_COLS + (k + 1) * hd],
                                  kvc_ref[:, A_KV_COLS + k * hd:A_KV_COLS + (k + 1) * hd]], axis=0)
            s = lax.dot_general(qg, kw, (((1,), (1,)), ((), ())), preferred_element_type=F32) * A_SCALE
            s = jnp.where(valid, s, NEG)
            sink = _group_column(s_ref, k, blk)
            m = jnp.maximum(jnp.max(s, axis=-1, keepdims=True), sink)
            p = jnp.exp(s - m)
            denom = jnp.sum(p, axis=-1, keepdims=True) + jnp.exp(sink - m)
            o = jnp.dot(p.astype(BF16), vw, preferred_element_type=F32) / denom
            lse = m + jnp.log(denom)
            for g in range(A_GROUP):
                h = A_GROUP * k + g
                o_ref[:, h * hd:(h + 1) * hd] = o[g * blk:(g + 1) * blk].astype(BF16)
                lse_ref[:, h:h + 1] = lse[g * blk:(g + 1) * blk]

    return _host_call(
        body, rider, name,
        out_shape=(jax.ShapeDtypeStruct((t, A_Q_COLS), BF16), jax.ShapeDtypeStruct((t, A_HEADS), F32)), grid=(nb,),
        in_specs=[pl.BlockSpec((blk, A_Q_COLS), lambda n: (n, 0)),
                  pl.BlockSpec((blk, 2 * A_KV_COLS), lambda n: (n, kv_block)),
                  pl.BlockSpec((blk, 2 * A_KV_COLS), lambda n: (jnp.maximum(n - 1, 0), kv_block)),
                  _const_spec((1, A_HEADS))],
        out_specs=(pl.BlockSpec((blk, A_Q_COLS), lambda n: (n, 0)), pl.BlockSpec((blk, A_HEADS), lambda n: (n, 0))),
        operands=(qkv_r, qkv_r, qkv_r, sinks), semantics=("parallel",))


def _swa_bwd(qkv_r, o, lse, do, sinks, name, rider=None):
    t = qkv_r.shape[0]
    blk = A_WINDOW
    nb = t // blk
    hd = A_HEAD_DIM
    kv_block = A_Q_COLS // (2 * A_KV_COLS)
    rows = A_GROUP * blk

    def nxt(n):
        return jnp.minimum(n + 1, nb - 1)

    def body(qc_ref, qn_ref, kvc_ref, kvp_ref, doc_ref, don_ref, oc_ref, on_ref, lc_ref, ln_ref, s_ref, dx_ref, ds_ref):
        n = pl.program_id(0)
        shape = (2 * rows, 2 * blk)
        row = lax.broadcasted_iota(jnp.int32, shape, 0)
        col = lax.broadcasted_iota(jnp.int32, shape, 1)
        is_next = row >= rows
        delta = jnp.where(is_next, blk, 0) + blk + (row & (blk - 1)) - col
        valid = ((delta >= 0) & (delta < A_WINDOW) & ((col >= blk) | (n > 0)) & (jnp.logical_not(is_next) | (n < nb - 1)))
        dsink_cols = []
        for k in range(A_KV_HEADS):
            qs = jnp.concatenate([_group_rows(qc_ref, k), _group_rows(qn_ref, k)], axis=0)
            dos = jnp.concatenate([_group_rows(doc_ref, k), _group_rows(don_ref, k)], axis=0)
            os_ = jnp.concatenate([_group_rows(oc_ref, k), _group_rows(on_ref, k)], axis=0).astype(F32)
            lses = jnp.concatenate([_group_column(lc_ref, k, blk), _group_column(ln_ref, k, blk)], axis=0)
            kw = jnp.concatenate([kvp_ref[:, k * hd:(k + 1) * hd], kvc_ref[:, k * hd:(k + 1) * hd]], axis=0)
            vw = jnp.concatenate([kvp_ref[:, A_KV_COLS + k * hd:A_KV_COLS + (k + 1) * hd],
                                  kvc_ref[:, A_KV_COLS + k * hd:A_KV_COLS + (k + 1) * hd]], axis=0)
            s = lax.dot_general(qs, kw, (((1,), (1,)), ((), ())), preferred_element_type=F32) * A_SCALE
            p = jnp.exp(jnp.where(valid, s - lses, NEG))
            dos_b = dos.astype(BF16)
            dp = lax.dot_general(dos_b, vw, (((1,), (1,)), ((), ())), preferred_element_type=F32)
            dlt = jnp.sum(dos * os_, axis=-1, keepdims=True)
            ds = p * (dp - dlt)
            dq = jnp.dot(ds[:rows].astype(BF16), kw, preferred_element_type=F32) * A_SCALE
            dk = lax.dot_general(ds[:, blk:].astype(BF16), qs, (((0,), (0,)), ((), ())), preferred_element_type=F32) * A_SCALE
            dv = lax.dot_general(p[:, blk:].astype(BF16), dos_b, (((0,), (0,)), ((), ())), preferred_element_type=F32)
            for g in range(A_GROUP):
                h = A_GROUP * k + g
                dx_ref[:, h * hd:(h + 1) * hd] = dq[g * blk:(g + 1) * blk]
            dx_ref[:, A_Q_COLS + k * hd:A_Q_COLS + (k + 1) * hd] = dk
            dx_ref[:, A_Q_COLS + A_KV_COLS + k * hd:A_Q_COLS + A_KV_COLS + (k + 1) * hd] = dv
            sink = _group_column(s_ref, k, blk)
            contrib = -jnp.exp(sink - lses[:rows]) * dlt[:rows]
            for g in range(A_GROUP):
                dsink_cols.append(jnp.sum(contrib[g * blk:(g + 1) * blk], axis=0, keepdims=True))
        _accumulate(ds_ref, jnp.concatenate(dsink_cols, axis=1), n)

    q_spec = lambda f: pl.BlockSpec((blk, A_Q_COLS), lambda n: (f(n), 0))
    l_spec = lambda f: pl.BlockSpec((blk, A_HEADS), lambda n: (f(n), 0))
    same = lambda n: n
    return _host_call(
        body, rider, name,
        out_shape=(jax.ShapeDtypeStruct((t, A_COLS), F32), jax.ShapeDtypeStruct((1, A_HEADS), F32)), grid=(nb,),
        in_specs=[q_spec(same), q_spec(nxt),
                  pl.BlockSpec((blk, 2 * A_KV_COLS), lambda n: (n, kv_block)),
                  pl.BlockSpec((blk, 2 * A_KV_COLS), lambda n: (jnp.maximum(n - 1, 0), kv_block)),
                  q_spec(same), q_spec(nxt), q_spec(same), q_spec(nxt), l_spec(same), l_spec(nxt),
                  _const_spec((1, A_HEADS))],
        out_specs=(pl.BlockSpec((blk, A_COLS), lambda n: (n, 0)), _const_spec((1, A_HEADS))),
        operands=(qkv_r, qkv_r, qkv_r, qkv_r, do, do, o, o, lse, lse, sinks), semantics=("arbitrary",))


C_DOWN_COLS = C_Q_RANK + C_KV_RANK + C_ROPE
C_Q_COLS = C_HEADS * C_QK
C_KV_COLS = C_HEADS * (C_NOPE + C_V)
C_O_COLS = C_HEADS * C_V
C_PAD = LANES
C_SCALE = C_QK ** -0.5
LOG2E = 1.4426950408889634
LN2 = 0.6931471805599453
C_Q_SCALE = C_SCALE * LOG2E
C_PAIR = 2


def _mla_latent_fwd(down, q_a_norm, kv_a_norm, name):
    t = down.shape[0]
    tm = _div_tile(t, 512, 16)

    def body(x_ref, gq_ref, gk_ref, cq_ref, ckv_ref):
        cq, ckv = x_ref[:, :C_Q_RANK], x_ref[:, C_Q_RANK:C_Q_RANK + C_KV_RANK]
        cq_ref[...] = (cq * _rstd(cq) * gq_ref[...]).astype(BF16)
        ckv_ref[...] = (ckv * _rstd(ckv) * gk_ref[...]).astype(BF16)

    return pl.pallas_call(
        body, name=name,
        out_shape=(jax.ShapeDtypeStruct((t, C_Q_RANK), BF16), jax.ShapeDtypeStruct((t, C_KV_RANK), BF16)), grid=(t // tm,),
        in_specs=[_row_spec(tm, C_DOWN_COLS), _const_spec((1, C_Q_RANK)), _const_spec((1, C_KV_RANK))],
        out_specs=(_row_spec(tm, C_Q_RANK), _row_spec(tm, C_KV_RANK)), compiler_params=_params(("parallel",)),
    )(down, q_a_norm, kv_a_norm)


def _mla_latent_bwd(down, dcq, dckv, dkrope, q_a_norm, kv_a_norm, name):
    t = down.shape[0]
    tm = _div_tile(t, 512, 16)

    def body(x_ref, dcq_ref, dckv_ref, dkr_ref, gq_ref, gk_ref, o_ref, dgq_ref, dgk_ref):
        dq, dgq = _norm_bwd(x_ref[:, :C_Q_RANK], gq_ref[...], dcq_ref[...])
        dkv, dgk = _norm_bwd(x_ref[:, C_Q_RANK:C_Q_RANK + C_KV_RANK], gk_ref[...], dckv_ref[...])
        o_ref[...] = jnp.concatenate([dq, dkv, dkr_ref[...]], axis=1).astype(BF16)
        _accumulate(dgq_ref, dgq, pl.program_id(0))
        _accumulate(dgk_ref, dgk, pl.program_id(0))

    return pl.pallas_call(
        body, name=name,
        out_shape=(jax.ShapeDtypeStruct((t, C_DOWN_COLS), BF16), jax.ShapeDtypeStruct((1, C_Q_RANK), F32),
                   jax.ShapeDtypeStruct((1, C_KV_RANK), F32)),
        grid=(t // tm,),
        in_specs=[_row_spec(tm, C_DOWN_COLS), _row_spec(tm, C_Q_RANK), _row_spec(tm, C_KV_RANK), _row_spec(tm, C_ROPE),
                  _const_spec((1, C_Q_RANK)), _const_spec((1, C_KV_RANK))],
        out_specs=(_row_spec(tm, C_DOWN_COLS), _const_spec((1, C_Q_RANK)), _const_spec((1, C_KV_RANK))),
        compiler_params=_params(("arbitrary",)),
    )(down, dcq, dckv, dkrope, q_a_norm, kv_a_norm)


def _head_major_spec(tm, width):
    return pl.BlockSpec((C_HEADS, tm, width), lambda i: (0, i, 0))


C_NOPE_V = C_NOPE + C_V
C_ROPE_COLS = C_HEADS * C_ROPE


def _mla_prep_tables(q_norm, k_norm, cos, sin):
    ql, kl, rl = np.arange(C_Q_COLS), np.arange(C_KV_COLS), np.arange(C_ROPE_COLS)
    col = np.arange(LANES)[None, :]
    is_nope = (kl % C_NOPE_V) < C_NOPE
    one_hot = lambda m: jnp.asarray(m.astype(np.float32), BF16)
    gk_nope = jnp.concatenate([k_norm[:, :C_NOPE], jnp.zeros((1, C_V), F32)], axis=1)
    return dict(
        q_dim=jnp.asarray((ql % C_QK)[None, :], jnp.int32),
        q_gain=jnp.tile(q_norm, (1, C_HEADS)), q_cos=jnp.tile(cos, (1, C_HEADS)), q_sin=jnp.tile(sin, (1, C_HEADS)),
        q_seg=one_hot(ql[:, None] // C_QK == col), q_spread=one_hot((ql[:, None] // C_QK == col).T),
        q_fold=one_hot(ql[:, None] % C_QK == col),
        k_nope=jnp.asarray(is_nope[None, :].astype(np.float32)),
        k_gain=jnp.tile(gk_nope, (1, C_HEADS)),
        k_seg=one_hot(is_nope[:, None] & (kl[:, None] // C_NOPE_V == col)),
        k_spread=one_hot((kl[:, None] // C_NOPE_V == col).T),
        k_fold=one_hot(is_nope[:, None] & (kl[:, None] % C_NOPE_V == col)),
        r_gain=jnp.tile(k_norm[:, C_NOPE:], (1, C_HEADS)),
        r_cos=jnp.tile(cos[:, C_NOPE:], (1, C_HEADS)), r_sin=jnp.tile(sin[:, C_NOPE:], (1, C_HEADS)),
        r_rep=one_hot(np.arange(C_ROPE)[:, None] == rl[None, :] % C_ROPE),
        r_seg=one_hot(rl[:, None] // C_ROPE == col), r_spread=one_hot((rl[:, None] // C_ROPE == col).T),
        r_fold=one_hot(rl[:, None] % C_ROPE == col))


def _q_partner(n, dim):
    half = C_ROPE // 2
    return jnp.where((dim >= C_NOPE) & (dim < C_NOPE + half), pltpu.roll(n, C_Q_COLS - half, 1),
                     jnp.where(dim >= C_NOPE + half, pltpu.roll(n, half, 1), 0.0))


def _rope_partner(n):
    half = C_ROPE // 2
    dim = lax.broadcasted_iota(jnp.int32, n.shape, 1) & (C_ROPE - 1)
    return jnp.where(dim < half, pltpu.roll(n, C_ROPE_COLS - half, 1), pltpu.roll(n, half, 1))


def _head_rstd(sum_sq):
    return lax.rsqrt(sum_sq * (1.0 / C_QK) + EPS)


MLA_PREP_FWD_TABLES = ['q_dim', 'q_gain', 'q_seg', 'q_spread', 'k_gain', 'k_seg', 'k_spread', 'r_gain', 'r_rep', 'r_spread']
MLA_PREP_BWD_TABLES = MLA_PREP_FWD_TABLES + ['q_fold', 'k_nope', 'k_fold', 'r_seg', 'r_fold']
MLA_PREP_ROW_TABLES = ['q_cos', 'q_sin', 'r_cos', 'r_sin']


def _mla_qk_fwd(qw, kvw, down, q_norm, k_norm, cos, sin, name):
    t = qw.shape[0]
    tm = _div_tile(t, 256, 16)
    tables = _mla_prep_tables(q_norm, k_norm, cos, sin)
    consts = [tables[n] for n in MLA_PREP_FWD_TABLES]
    rows = [tables[n] for n in MLA_PREP_ROW_TABLES]

    def body(q_ref, kv_ref, dn_ref, qcos_ref, qsin_ref, rcos_ref, rsin_ref, *rest):
        c = {n: r[...] for n, r in zip(MLA_PREP_FWD_TABLES, rest)}
        qo_ref, ko_ref, vo_ref = rest[len(MLA_PREP_FWD_TABLES):]
        q, kv, kr = q_ref[...], kv_ref[...], dn_ref[:, C_Q_RANK + C_KV_RANK:]
        nq = q * _pieces_dot(_head_rstd(_pieces_dot(q * q, c['q_seg'], 1)), c['q_spread'], 2) * c['q_gain']
        out_q = (nq * qcos_ref[...] + _q_partner(nq, c['q_dim']) * qsin_ref[...]) * C_Q_SCALE
        rstd = _head_rstd(_pieces_dot(kv * kv, c['k_seg'], 1) + jnp.sum(kr * kr, axis=-1, keepdims=True))
        nope = kv * _pieces_dot(rstd, c['k_spread'], 2) * c['k_gain']
        nr = _pieces_dot(kr, c['r_rep'], 2) * _pieces_dot(rstd, c['r_spread'], 2) * c['r_gain']
        rope = nr * rcos_ref[...] + _rope_partner(nr) * rsin_ref[...]
        pad = jnp.zeros((tm, C_PAD - C_QK), F32)
        one_then_zeros = (lax.broadcasted_iota(jnp.int32, (tm, C_PAD - C_V), 1) == 0).astype(F32)
        for h in range(C_HEADS):
            qo_ref[h] = jnp.concatenate([out_q[:, h * C_QK:(h + 1) * C_QK], pad], axis=1).astype(BF16)
            ko_ref[h] = jnp.concatenate([nope[:, h * C_NOPE_V:h * C_NOPE_V + C_NOPE], rope[:, h * C_ROPE:(h + 1) * C_ROPE],
                                         pad], axis=1).astype(BF16)
            vo_ref[h] = jnp.concatenate([kv[:, h * C_NOPE_V + C_NOPE:(h + 1) * C_NOPE_V], one_then_zeros],
                                        axis=1).astype(BF16)

    return pl.pallas_call(
        body, name=name,
        out_shape=(jax.ShapeDtypeStruct((C_HEADS, t, C_PAD), BF16), jax.ShapeDtypeStruct((C_HEADS, t, C_PAD), BF16),
                   jax.ShapeDtypeStruct((C_HEADS, t, C_PAD), BF16)),
        grid=(t // tm,),
        in_specs=[_row_spec(tm, C_Q_COLS), _row_spec(tm, C_KV_COLS), _row_spec(tm, C_DOWN_COLS)]
                 + [_row_spec(tm, r.shape[1]) for r in rows] + [_const_spec(a.shape) for a in consts],
        out_specs=(_head_major_spec(tm, C_PAD), _head_major_spec(tm, C_PAD), _head_major_spec(tm, C_PAD)),
        compiler_params=_params(("parallel",)),
    )(qw, kvw, down, *rows, *consts)


def _mla_qk_bwd(qw, kvw, down, dq, dk, dv, q_norm, k_norm, cos, sin, name, rider=None):
    t = qw.shape[0]
    tm = _div_tile(t, 256, 16)
    tables = _mla_prep_tables(q_norm, k_norm, cos, sin)
    consts = [tables[n] for n in MLA_PREP_BWD_TABLES]
    rows = [tables[n] for n in MLA_PREP_ROW_TABLES]

    def body(q_ref, kv_ref, dn_ref, dq_ref, dk_ref, dv_ref, qcos_ref, qsin_ref, rcos_ref, rsin_ref, *rest):
        c = {n: r[...] for n, r in zip(MLA_PREP_BWD_TABLES, rest)}
        dqw_ref, dkvw_ref, dkr_ref, dgq_ref, dgk_ref = rest[len(MLA_PREP_BWD_TABLES):]
        step = pl.program_id(0)
        q, kv, kr = q_ref[...], kv_ref[...], dn_ref[:, C_Q_RANK + C_KV_RANK:]
        dout_q = jnp.concatenate([dq_ref[h][:, :C_QK] for h in range(C_HEADS)], axis=1)
        d_slab = jnp.concatenate([x for h in range(C_HEADS) for x in (dk_ref[h][:, :C_NOPE], dv_ref[h])], axis=1)
        d_rope = jnp.concatenate([dk_ref[h][:, C_NOPE:C_QK] for h in range(C_HEADS)], axis=1)

        rq = _pieces_dot(_head_rstd(_pieces_dot(q * q, c['q_seg'], 1)), c['q_spread'], 2)
        xq = q * rq
        dnq = dout_q * qcos_ref[...] + _q_partner(dout_q * qsin_ref[...], c['q_dim'])
        dgq = _pieces_dot(jnp.sum(dnq * xq, axis=0, keepdims=True), c['q_fold'], 3)
        dxq = dnq * c['q_gain']
        mean_q = _pieces_dot(_pieces_dot(dxq * xq, c['q_seg'], 1) * (1.0 / C_QK), c['q_spread'], 2)
        dqw_ref[...] = (rq * (dxq - xq * mean_q)).astype(BF16)

        rstd = _head_rstd(_pieces_dot(kv * kv, c['k_seg'], 1) + jnp.sum(kr * kr, axis=-1, keepdims=True))
        r_nope, r_rope = _pieces_dot(rstd, c['k_spread'], 2), _pieces_dot(rstd, c['r_spread'], 2)
        x_nope = kv * r_nope * c['k_nope']
        x_rope = _pieces_dot(kr, c['r_rep'], 2) * r_rope
        dn_nope = d_slab * c['k_nope']
        dn_rope = d_rope * rcos_ref[...] + _rope_partner(d_rope * rsin_ref[...])
        dg_nope = _pieces_dot(jnp.sum(dn_nope * x_nope, axis=0, keepdims=True), c['k_fold'], 3)
        dg_rope = _pieces_dot(jnp.sum(dn_rope * x_rope, axis=0, keepdims=True), c['r_fold'], 3)
        dx_nope, dx_rope = dn_nope * c['k_gain'], dn_rope * c['r_gain']
        mean = (_pieces_dot(dx_nope * x_nope, c['k_seg'], 1) + _pieces_dot(dx_rope * x_rope, c['r_seg'], 1)) * (1.0 / C_QK)
        g_nope = r_nope * (dx_nope - x_nope * _pieces_dot(mean, c['k_spread'], 2))
        g_rope = r_rope * (dx_rope - x_rope * _pieces_dot(mean, c['r_spread'], 2))
        dkvw_ref[...] = jnp.where(c['k_nope'] > 0.0, g_nope, d_slab).astype(BF16)
        dkr_ref[...] = _pieces_dot(g_rope, c['r_fold'], 3)[:, :C_ROPE]
        _accumulate(dgq_ref, dgq[:, :C_QK], step)
        _accumulate(dgk_ref, jnp.concatenate([dg_nope[:, :C_NOPE], dg_rope[:, :C_ROPE]], axis=1), step)

    return _host_call(
        body, rider, name,
        out_shape=(jax.ShapeDtypeStruct((t, C_Q_COLS), BF16), jax.ShapeDtypeStruct((t, C_KV_COLS), BF16),
                   jax.ShapeDtypeStruct((t, C_ROPE), F32), jax.ShapeDtypeStruct((1, C_QK), F32),
                   jax.ShapeDtypeStruct((1, C_QK), F32)),
        grid=(t // tm,),
        in_specs=[_row_spec(tm, C_Q_COLS), _row_spec(tm, C_KV_COLS), _row_spec(tm, C_DOWN_COLS),
                  _head_major_spec(tm, C_PAD), _head_major_spec(tm, C_PAD), _head_major_spec(tm, C_V)]
                 + [_row_spec(tm, r.shape[1]) for r in rows] + [_const_spec(a.shape) for a in consts],
        out_specs=(_row_spec(tm, C_Q_COLS), _row_spec(tm, C_KV_COLS), _row_spec(tm, C_ROPE), _const_spec((1, C_QK)),
                   _const_spec((1, C_QK))),
        operands=(qw, kvw, down, dq, dk, dv, *rows, *consts), semantics=("arbitrary",))


def _causal_keep(rows, cols, row_offset=0, transposed=False):
    row = lax.broadcasted_iota(jnp.int32, (rows, cols), 0) + row_offset
    col = lax.broadcasted_iota(jnp.int32, (rows, cols), 1)
    return (row <= col) if transposed else (col <= row)


def _mla_fwd(q, k, v, name):
    _, t, _ = q.shape
    blk = min(MLA_FWD_BLOCK, t)
    nq = t // blk

    def body(q_ref, k_ref, v_ref, o_ref, lse_ref, m_sc, acc_sc):
        qi = pl.program_id(1)
        m_sc[...] = jnp.full_like(m_sc, NEG)
        acc_sc[...] = jnp.zeros_like(acc_sc)

        def step(ki, masked):
            rows = pl.ds(pl.multiple_of(ki * blk, blk), blk)
            for hh in range(C_PAIR):
                s = lax.dot_general(q_ref[hh], k_ref[hh, rows, :], (((1,), (1,)), ((), ())), preferred_element_type=F32)
                if masked:
                    s = jnp.where(_causal_keep(blk, blk), s, NEG)
                m_prev = m_sc[hh]
                m_new = jnp.maximum(m_prev, jnp.max(s, axis=-1, keepdims=True))
                p = jnp.exp2(s - m_new)
                acc_sc[hh] = jnp.exp2(m_prev - m_new) * acc_sc[hh] + jnp.dot(p.astype(BF16), v_ref[hh, rows, :],
                                                                                preferred_element_type=F32)
                m_sc[hh] = m_new

        def below_diagonal(ki, carry):
            step(ki, False)
            return carry

        lax.fori_loop(0, qi, below_diagonal, 0)
        step(qi, True)
        outs = []
        for hh in range(C_PAIR):
            denom = acc_sc[hh, :, C_V:C_V + 1]
            outs.append(acc_sc[hh, :, :C_V] / denom)
            lse_ref[hh] = m_sc[hh] + jnp.log(denom) * LOG2E
        o_ref[...] = jnp.concatenate(outs, axis=1).astype(BF16)

    whole = lambda hp, qi: (hp, 0, 0)
    return pl.pallas_call(
        body, name=name,
        out_shape=(jax.ShapeDtypeStruct((t, C_O_COLS), BF16), jax.ShapeDtypeStruct((C_HEADS, t, 1), F32)),
        grid=(C_HEADS // C_PAIR, nq),
        in_specs=[pl.BlockSpec((C_PAIR, blk, C_PAD), lambda hp, qi: (hp, qi, 0)),
                  pl.BlockSpec((C_PAIR, t, C_PAD), whole, pipeline_mode=pl.Buffered(1)),
                  pl.BlockSpec((C_PAIR, t, C_PAD), whole, pipeline_mode=pl.Buffered(1))],
        out_specs=(pl.BlockSpec((blk, C_PAIR * C_V), lambda hp, qi: (qi, hp)),
                   pl.BlockSpec((C_PAIR, blk, 1), lambda hp, qi: (hp, qi, 0))),
        scratch_shapes=[pltpu.VMEM((C_PAIR, blk, 1), F32), pltpu.VMEM((C_PAIR, blk, C_PAD), F32)],
        compiler_params=_params(("parallel", "arbitrary")),
    )(q, k, v)


def _mla_delta(do, o, name):
    t = do.shape[0]
    blk = min(MLA_BLOCK, t)

    def body(do_ref, o_ref, dlt_ref, dob_ref):
        for hh in range(C_PAIR):
            do_h = do_ref[:, hh * C_V:(hh + 1) * C_V]
            dlt_ref[hh] = jnp.sum(do_h * o_ref[:, hh * C_V:(hh + 1) * C_V].astype(F32), axis=-1, keepdims=True)
        dob_ref[...] = do_ref[...].astype(BF16)

    wide = pl.BlockSpec((blk, C_PAIR * C_V), lambda hp, i: (i, hp))
    return pl.pallas_call(
        body, name=name,
        out_shape=(jax.ShapeDtypeStruct((C_HEADS, t, 1), F32), jax.ShapeDtypeStruct(do.shape, BF16)),
        grid=(C_HEADS // C_PAIR, t // blk), in_specs=[wide, wide],
        out_specs=(pl.BlockSpec((C_PAIR, blk, 1), lambda hp, i: (hp, i, 0)), wide),
        compiler_params=_params(("parallel", "parallel")),
    )(do, o)


def _mla_bwd(q, k, v, do_b, lse_rows, dlt_rows, name):
    _, t, _ = q.shape
    blk = min(MLA_BLOCK, t)
    nq = t // blk

    def body(q_ref, k_ref, v_ref, do_ref, lse_ref, dlt_ref, dq_hbm, dk_ref, dv_ref, dq_sc, dk_sc, dv_sc, sem):
        hp, ki = pl.program_id(0), pl.program_id(1)

        @pl.when(ki == 0)
        def _():
            dq_sc[...] = jnp.zeros_like(dq_sc)

        dk_sc[...] = jnp.zeros_like(dk_sc)
        dv_sc[...] = jnp.zeros_like(dv_sc)

        def step(qi, masked):
            rows = pl.ds(pl.multiple_of(qi * blk, blk), blk)
            for hh in range(C_PAIR):
                qb = q_ref[hh, rows, :]
                dob = do_ref[rows, hh * C_V:(hh + 1) * C_V]
                s = lax.dot_general(k_ref[hh], qb, (((1,), (1,)), ((), ())), preferred_element_type=F32)
                if masked:
                    s = jnp.where(_causal_keep(blk, blk, transposed=True), s, NEG)
                p = jnp.exp2(s - lse_ref[hh, qi])
                dp = lax.dot_general(v_ref[hh, :, :C_V], dob, (((1,), (1,)), ((), ())), preferred_element_type=F32)
                ds = (p * (dp - dlt_ref[hh, qi])).astype(BF16)
                dv_sc[hh] += jnp.dot(p.astype(BF16), dob, preferred_element_type=F32)
                dk_sc[hh] += jnp.dot(ds, qb, preferred_element_type=F32)
                dq_sc[hh, rows, :] += lax.dot_general(ds, k_ref[hh], (((0,), (0,)), ((), ())), preferred_element_type=F32)

        def above_diagonal(qi, carry):
            step(qi, False)
            return carry

        step(ki, True)
        lax.fori_loop(ki + 1, nq, above_diagonal, 0)
        dk_ref[...] = dk_sc[...] * LN2
        dv_ref[...] = dv_sc[...]

        @pl.when(ki == nq - 1)
        def _():
            dq_sc[...] = dq_sc[...] * C_SCALE
            out = pltpu.make_async_copy(dq_sc, dq_hbm.at[pl.ds(hp * C_PAIR, C_PAIR)], sem)
            out.start()
            out.wait()

    once = pl.Buffered(1)
    whole = lambda hp, ki: (hp, 0, 0)
    whole4 = lambda hp, ki: (hp, 0, 0, 0)
    kmap = lambda hp, ki: (hp, ki, 0)
    return pl.pallas_call(
        body, name=name,
        out_shape=(jax.ShapeDtypeStruct((C_HEADS, t, C_PAD), F32), jax.ShapeDtypeStruct((C_HEADS, t, C_PAD), F32),
                   jax.ShapeDtypeStruct((C_HEADS, t, C_V), F32)),
        grid=(C_HEADS // C_PAIR, nq),
        in_specs=[pl.BlockSpec((C_PAIR, t, C_PAD), whole, pipeline_mode=once), pl.BlockSpec((C_PAIR, blk, C_PAD), kmap),
                  pl.BlockSpec((C_PAIR, blk, C_PAD), kmap),
                  pl.BlockSpec((t, C_PAIR * C_V), lambda hp, ki: (0, hp), pipeline_mode=once),
                  pl.BlockSpec((C_PAIR, nq, 1, blk), whole4, pipeline_mode=once),
                  pl.BlockSpec((C_PAIR, nq, 1, blk), whole4, pipeline_mode=once)],
        out_specs=(pl.BlockSpec(memory_space=pl.ANY), pl.BlockSpec((C_PAIR, blk, C_PAD), kmap),
                   pl.BlockSpec((C_PAIR, blk, C_V), kmap)),
        scratch_shapes=[pltpu.VMEM((C_PAIR, t, C_PAD), F32), pltpu.VMEM((C_PAIR, blk, C_PAD), F32),
                        pltpu.VMEM((C_PAIR, blk, C_V), F32), pltpu.SemaphoreType.DMA(())],
        compiler_params=_params(("arbitrary", "arbitrary")),
    )(q, k, v, do_b, lse_rows, dlt_rows)


def _adamw(parts, w, m, v, name):
    layers, rows, cols = w.shape
    tm = _div_tile(rows, 256, 16)

    def body(p_ref, w_ref, m_ref, v_ref, g_ref, d_ref, nm_ref, nv_ref):
        g = p_ref[0].astype(F32)
        for j in range(1, N_DEV):
            g = g + p_ref[j].astype(F32)
        nm = ADAM_B1 * m_ref[...] + (1.0 - ADAM_B1) * g
        nv = ADAM_B2 * v_ref[...] + (1.0 - ADAM_B2) * jnp.square(g)
        m_hat = nm / (1.0 - ADAM_B1 ** ADAM_STEP)
        v_hat = nv / (1.0 - ADAM_B2 ** ADAM_STEP)
        g_ref[...] = g
        d_ref[...] = -ADAM_LR * (m_hat / (jnp.sqrt(v_hat) + ADAM_EPS) + ADAM_WD * w_ref[...])
        nm_ref[...] = nm
        nv_ref[...] = nv

    spec = pl.BlockSpec((None, tm, cols), lambda l, i: (l, i, 0))
    return pl.pallas_call(
        body, name=name, out_shape=tuple(jax.ShapeDtypeStruct(w.shape, F32) for _ in range(4)),
        grid=(layers, rows // tm),
        in_specs=[pl.BlockSpec((None, N_DEV, tm, cols), lambda l, i: (l, 0, i, 0)), spec, spec, spec],
        out_specs=(spec, spec, spec, spec), compiler_params=_params(("parallel", "parallel")),
    )(parts, w, m, v)


def _join_shards(gathered, axis):
    moved = jnp.moveaxis(gathered, 1, axis)
    shape = list(moved.shape)
    shape[axis:axis + 2] = [shape[axis] * shape[axis + 1]]
    return moved.reshape(shape)


def _split_shards(full, axis):
    shape = list(full.shape)
    shape[axis:axis + 1] = [N_DEV, shape[axis] // N_DEV]
    return jnp.moveaxis(full.reshape(shape), axis, 1)


def _as_rows(shape):
    rest = tuple(shape[1:])
    return (shape[0], 1, rest[0]) if len(rest) == 1 else (shape[0],) + rest


MIXER_WEIGHTS = {0: ['a_w_qkv', 'a_w_o'], 1: ['b_w_in', 'b_conv_w', 'b_w_out'],
                 2: ['c_w_down', 'c_q_a_norm', 'c_kv_a_norm', 'c_w_q_up', 'c_w_kv_up', 'c_w_o']}


def _layer_units(i):
    return [(n, i // N_MIXERS) for n in MIXER_WEIGHTS[i % N_MIXERS]] + [('f_w_gate_up', i), ('f_w_down', i)]


def _forward_backward(x, positions, target, local, rep):
    def gather(units):
        return _Exchange([local[n][i:i + 1].astype(BF16) if n in GATHER_BF16 else local[n][i:i + 1] for n, i in units],
                         scatter=False)

    w = {n: {} for n in SHARDED}

    def arrived(units, gathered):
        for (n, i), g in zip(units, gathered):
            full = _join_shards(g, SHARD_AXIS[n])
            w[n][i] = full if full.ndim == 2 else full[0]

    all_units = [u for i in range(DEPTH) for u in _layer_units(i)]
    first_units = _layer_units(0) + [u for u in all_units if u[0] in GATHER_F32]
    later_units = [u for u in all_units if u not in first_units]
    arrived(first_units, _exchange_now(gather(first_units), "gather_first_weights"))

    cos_a, sin_a = _rope_tables(positions, A_ROT_DIM, 0, A_HEAD_DIM - A_ROT_DIM)
    cos_c, sin_c = _rope_tables(positions, C_ROPE, C_NOPE, 0)
    saved = []
    for i in range(DEPTH):
        kind, j = i % N_MIXERS, i // N_MIXERS
        s = {'x': x}
        h1 = _rmsnorm_fwd(x, rep['mix_norm'][i:i + 1], f"mix_norm_fwd_{i}")
        s['h1'] = h1
        if kind == 0:
            s['qkv'] = _matmul(h1, w['a_w_qkv'][j], 'nn', f"a_qkv_{i}")
            s['qkv_r'] = _swa_prep_fwd(s['qkv'], rep['a_q_norm'][j:j + 1], rep['a_k_norm'][j:j + 1], cos_a, sin_a,
                                       f"a_prep_fwd_{i}")
            (s['o'], s['lse']), gathered = _swa_fwd(s['qkv_r'], rep['a_sinks'][j:j + 1], f"a_attn_fwd_{i}",
                                                    rider=gather(later_units) if i == 0 else None)
            if i == 0:
                arrived(later_units, gathered)
            x1 = _matmul(s['o'], w['a_w_o'][j], 'nn', f"a_out_{i}", residual=x)
        elif kind == 1:
            s['bcu'] = _matmul(h1, w['b_w_in'][j], 'nn', f"b_in_{i}")
            s['by'] = _sconv_fwd(s['bcu'], w['b_conv_w'][j], f"b_conv_fwd_{i}")
            x1 = _matmul(s['by'], w['b_w_out'][j], 'nn', f"b_out_{i}", residual=x)
        else:
            s['down'] = _matmul(h1, w['c_w_down'][j], 'nn', f"c_down_{i}")
            s['cq'], s['ckv'] = _mla_latent_fwd(s['down'], w['c_q_a_norm'][j], w['c_kv_a_norm'][j],
                                                f"c_latent_fwd_{i}")
            s['qw'] = _matmul(s['cq'], w['c_w_q_up'][j], 'nn', f"c_q_up_{i}")
            s['kvw'] = _matmul(s['ckv'], w['c_w_kv_up'][j], 'nn', f"c_kv_up_{i}")
            s['q'], s['k'], s['v'] = _mla_qk_fwd(s['qw'], s['kvw'], s['down'], rep['c_q_norm'][j:j + 1],
                                                 rep['c_k_norm'][j:j + 1], cos_c, sin_c, f"c_prep_fwd_{i}")
            s['o'], s['lse'] = _mla_fwd(s['q'], s['k'], s['v'], f"c_attn_fwd_{i}")
            x1 = _matmul(s['o'], w['c_w_o'][j], 'nn', f"c_out_{i}", residual=x)
        s['x1'] = x1
        s['h2'] = _rmsnorm_fwd(x1, rep['ffn_norm'][i:i + 1], f"ffn_norm_fwd_{i}")
        s['gu'] = _matmul(s['h2'], w['f_w_gate_up'][i], 'nn', f"f_gate_up_{i}", out_dtype=BF16)
        s['act'] = _swiglu_fwd(s['gu'], f"f_act_fwd_{i}")
        x = _matmul(s['act'], w['f_w_down'][i], 'nn', f"f_down_{i}", residual=x1)
        saved.append(s)

    loss, dx = _loss_head(x, target, "loss_head")

    per_layer = {n: {} for n in WEIGHTS}
    received = {}
    sent = set()

    def ready():
        units = [(n, j) for n in SHARDED for j in sorted(per_layer[n]) if (n, j) not in sent]
        if not units:
            return None, units
        sent.update(units)
        blocks = []
        for n, j in units:
            g = per_layer[n][j]
            blocks.append(_split_shards(g if n in ('c_q_a_norm', 'c_kv_a_norm') else g[None], SHARD_AXIS[n]))
        return _Exchange(blocks, scatter=True), units

    for i in reversed(range(DEPTH)):
        kind, j = i % N_MIXERS, i // N_MIXERS
        s = saved[i]
        per_layer['f_w_down'][i] = _matmul(s['act'], dx, 'tn', f"f_down_dw_{i}", out_dtype=BF16)
        dact = _matmul(dx, w['f_w_down'][i], 'nt', f"f_down_dx_{i}", out_dtype=BF16)
        dgu = _swiglu_bwd(s['gu'], dact, f"f_act_bwd_{i}")
        per_layer['f_w_gate_up'][i] = _matmul(s['h2'], dgu, 'tn', f"f_gate_up_dw_{i}", out_dtype=BF16)
        dh2 = _matmul(dgu, w['f_w_gate_up'][i], 'nt', f"f_gate_up_dx_{i}")
        dx1, per_layer['ffn_norm'][i] = _rmsnorm_bwd(s['x1'], rep['ffn_norm'][i:i + 1], dh2, dx, f"ffn_norm_bwd_{i}")
        if kind == 0:
            per_layer['a_w_o'][j] = _matmul(s['o'], dx1, 'tn', f"a_out_dw_{i}", out_dtype=BF16)
            do = _matmul(dx1, w['a_w_o'][j], 'nt', f"a_out_dx_{i}")
            rider, units = ready()
            (dqkv_r, per_layer['a_sinks'][j]), parts = _swa_bwd(s['qkv_r'], s['o'], s['lse'], do, rep['a_sinks'][j:j + 1],
                                                                f"a_attn_bwd_{i}", rider=rider)
            received.update(zip(units, parts or ()))
            dqkv, per_layer['a_q_norm'][j], per_layer['a_k_norm'][j] = _swa_prep_bwd(
                s['qkv'], dqkv_r, rep['a_q_norm'][j:j + 1], rep['a_k_norm'][j:j + 1], cos_a, sin_a, f"a_prep_bwd_{i}")
            per_layer['a_w_qkv'][j] = _matmul(s['h1'], dqkv, 'tn', f"a_qkv_dw_{i}", out_dtype=BF16)
            dh1 = _matmul(dqkv, w['a_w_qkv'][j], 'nt', f"a_qkv_dx_{i}")
        elif kind == 1:
            per_layer['b_w_out'][j] = _matmul(s['by'], dx1, 'tn', f"b_out_dw_{i}", out_dtype=BF16)
            dby = _matmul(dx1, w['b_w_out'][j], 'nt', f"b_out_dx_{i}")
            dbcu, per_layer['b_conv_w'][j] = _sconv_bwd(s['bcu'], dby, w['b_conv_w'][j], f"b_conv_bwd_{i}")
            per_layer['b_w_in'][j] = _matmul(s['h1'], dbcu, 'tn', f"b_in_dw_{i}", out_dtype=BF16)
            dh1 = _matmul(dbcu, w['b_w_in'][j], 'nt', f"b_in_dx_{i}")
        else:
            per_layer['c_w_o'][j] = _matmul(s['o'], dx1, 'tn', f"c_out_dw_{i}", out_dtype=BF16)
            do = _matmul(dx1, w['c_w_o'][j], 'nt', f"c_out_dx_{i}")
            dlt, do_b = _mla_delta(do, s['o'], f"c_attn_delta_{i}")
            blk = min(MLA_BLOCK, do.shape[0])
            as_rows = lambda col: col.reshape(C_HEADS, do.shape[0] // blk, 1, blk)
            dq, dk, dv = _mla_bwd(s['q'], s['k'], s['v'], do_b, as_rows(s['lse']), as_rows(dlt), f"c_attn_bwd_{i}")
            rider, units = ready()
            (dqw, dkvw, dkrope, per_layer['c_q_norm'][j], per_layer['c_k_norm'][j]), parts = _mla_qk_bwd(
                s['qw'], s['kvw'], s['down'], dq, dk, dv, rep['c_q_norm'][j:j + 1], rep['c_k_norm'][j:j + 1], cos_c, sin_c,
                f"c_prep_bwd_{i}", rider=rider)
            received.update(zip(units, parts or ()))
            per_layer['c_w_q_up'][j] = _matmul(s['cq'], dqw, 'tn', f"c_q_up_dw_{i}", out_dtype=BF16)
            dcq = _matmul(dqw, w['c_w_q_up'][j], 'nt', f"c_q_up_dx_{i}")
            per_layer['c_w_kv_up'][j] = _matmul(s['ckv'], dkvw, 'tn', f"c_kv_up_dw_{i}", out_dtype=BF16)
            dckv = _matmul(dkvw, w['c_w_kv_up'][j], 'nt', f"c_kv_up_dx_{i}")
            ddown, per_layer['c_q_a_norm'][j], per_layer['c_kv_a_norm'][j] = _mla_latent_bwd(
                s['down'], dcq, dckv, dkrope, w['c_q_a_norm'][j], w['c_kv_a_norm'][j], f"c_latent_bwd_{i}")
            per_layer['c_w_down'][j] = _matmul(s['h1'], ddown, 'tn', f"c_down_dw_{i}", out_dtype=BF16)
            dh1 = _matmul(ddown, w['c_w_down'][j], 'nt', f"c_down_dx_{i}")
        dx, per_layer['mix_norm'][i] = _rmsnorm_bwd(s['x'], rep['mix_norm'][i:i + 1], dh1, dx1, f"mix_norm_bwd_{i}")

    last, units = ready()
    received.update(zip(units, _exchange_now(last, "scatter_last_gradients")))
    parts = {n: jnp.concatenate([received[(n, j)] for j in sorted(per_layer[n])], axis=0) for n in SHARDED}
    small = {}
    for n in REPLICATED:
        stacked = jnp.stack([per_layer[n][j] for j in sorted(per_layer[n])])
        small[n] = stacked.reshape(stacked.shape[0], stacked.shape[-1])
    return loss, dx, parts, small


def kernel(x, positions, mix_norm, ffn_norm, a_w_qkv, a_q_norm, a_k_norm, a_sinks, a_w_o, b_w_in, b_conv_w, b_w_out, c_w_down, c_q_a_norm, c_kv_a_norm, c_w_q_up, c_w_kv_up, c_q_norm, c_k_norm, c_w_o, f_w_gate_up, f_w_down, loss_target, m_mix_norm, m_ffn_norm, m_a_w_qkv, m_a_q_norm, m_a_k_norm, m_a_sinks, m_a_w_o, m_b_w_in, m_b_conv_w, m_b_w_out, m_c_w_down, m_c_q_a_norm, m_c_kv_a_norm, m_c_w_q_up, m_c_w_kv_up, m_c_q_norm, m_c_k_norm, m_c_w_o, m_f_w_gate_up, m_f_w_down, v_mix_norm, v_ffn_norm, v_a_w_qkv, v_a_q_norm, v_a_k_norm, v_a_sinks, v_a_w_o, v_b_w_in, v_b_conv_w, v_b_w_out, v_c_w_down, v_c_q_a_norm, v_c_kv_a_norm, v_c_w_q_up, v_c_w_kv_up, v_c_q_norm, v_c_k_norm, v_c_w_o, v_f_w_gate_up, v_f_w_down):
    local = dict(mix_norm=mix_norm, ffn_norm=ffn_norm, a_w_qkv=a_w_qkv, a_q_norm=a_q_norm, a_k_norm=a_k_norm, a_sinks=a_sinks, a_w_o=a_w_o, b_w_in=b_w_in, b_conv_w=b_conv_w, b_w_out=b_w_out, c_w_down=c_w_down, c_q_a_norm=c_q_a_norm, c_kv_a_norm=c_kv_a_norm, c_w_q_up=c_w_q_up, c_w_kv_up=c_w_kv_up, c_q_norm=c_q_norm, c_k_norm=c_k_norm, c_w_o=c_w_o, f_w_gate_up=f_w_gate_up, f_w_down=f_w_down)
    mom1 = dict(mix_norm=m_mix_norm, ffn_norm=m_ffn_norm, a_w_qkv=m_a_w_qkv, a_q_norm=m_a_q_norm, a_k_norm=m_a_k_norm, a_sinks=m_a_sinks, a_w_o=m_a_w_o, b_w_in=m_b_w_in, b_conv_w=m_b_conv_w, b_w_out=m_b_w_out, c_w_down=m_c_w_down, c_q_a_norm=m_c_q_a_norm, c_kv_a_norm=m_c_kv_a_norm, c_w_q_up=m_c_w_q_up, c_w_kv_up=m_c_w_kv_up, c_q_norm=m_c_q_norm, c_k_norm=m_c_k_norm, c_w_o=m_c_w_o, f_w_gate_up=m_f_w_gate_up, f_w_down=m_f_w_down)
    mom2 = dict(mix_norm=v_mix_norm, ffn_norm=v_ffn_norm, a_w_qkv=v_a_w_qkv, a_q_norm=v_a_q_norm, a_k_norm=v_a_k_norm, a_sinks=v_a_sinks, a_w_o=v_a_w_o, b_w_in=v_b_w_in, b_conv_w=v_b_conv_w, b_w_out=v_b_w_out, c_w_down=v_c_w_down, c_q_a_norm=v_c_q_a_norm, c_kv_a_norm=v_c_kv_a_norm, c_w_q_up=v_c_w_q_up, c_w_kv_up=v_c_w_kv_up, c_q_norm=v_c_q_norm, c_k_norm=v_c_k_norm, c_w_o=v_c_w_o, f_w_gate_up=v_f_w_gate_up, f_w_down=v_f_w_down)
    t, d = x.shape[1], x.shape[2]

    rep = {n: local[n] for n in REPLICATED}
    loss, grad_x, parts, small = _forward_backward(x.reshape(t, d), positions.reshape(t), loss_target.reshape(t, d),
                                                   {n: local[n] for n in SHARDED}, rep)

    out_g, out_d, out_m, out_v = {}, {}, {}, {}

    def update(names, parts):
        for n, part in zip(names, parts):
            shape = local[n].shape if n in SHARD_AXIS else (1,) + local[n].shape
            view = _as_rows(shape)
            results = _adamw(part.reshape(view[0], N_DEV, view[1], view[2]),
                             *[src[n].reshape(view) for src in (local, mom1, mom2)], name="adamw_" + n)
            for dst, res in zip((out_g, out_d, out_m, out_v), results):
                dst[n] = res.reshape(local[n].shape)

    update(SHARDED, [parts[n] for n in SHARDED])
    update(REPLICATED, _exchange_now(_Exchange([small[n].reshape((1,) + small[n].shape) for n in REPLICATED],
                                               scatter=False), "gather_small_gradients"))

    loss = lax.psum(loss.reshape(()), MESH_AXES)
    outs = [loss, grad_x.reshape(1, t, d)]
    for res in (out_g, out_d, out_m, out_v):
        outs += [res[n] for n in WEIGHTS]
    return tuple(outs)
```

```python
import jax
import jax.numpy as jnp
import numpy as np
from jax import lax
from jax.experimental import pallas as pl
from jax.experimental.pallas import tpu as pltpu

F32 = jnp.float32
BF16 = jnp.bfloat16

N_DEV = 8
MESH_AXES = ("x", "y", "c")

DEPTH = 4
N_MIXERS = 3
ROPE_THETA = 500000.0
EPS = 1e-6
A_HEADS, A_KV_HEADS, A_HEAD_DIM, A_ROT_DIM, A_WINDOW = 16, 4, 64, 16, 128
A_GROUP = A_HEADS // A_KV_HEADS
C_HEADS, C_NOPE, C_ROPE, C_V, C_Q_RANK, C_KV_RANK = 16, 64, 32, 64, 384, 256
C_QK = C_NOPE + C_ROPE
ADAM_LR, ADAM_B1, ADAM_B2, ADAM_EPS, ADAM_WD, ADAM_STEP = 0.001, 0.9, 0.999, 1e-08, 0.01, 10

VMEM_LIMIT_BYTES = 48 * 1024 * 1024
LANES = 128
NEG = -1e30
MLA_BLOCK = 512
MLA_FWD_Q_BLOCK = 512
MLA_FWD_K_BLOCK = 2048

WEIGHTS = ['mix_norm', 'ffn_norm', 'a_w_qkv', 'a_q_norm', 'a_k_norm', 'a_sinks', 'a_w_o', 'b_w_in', 'b_conv_w', 'b_w_out',
           'c_w_down', 'c_q_a_norm', 'c_kv_a_norm', 'c_w_q_up', 'c_w_kv_up', 'c_q_norm', 'c_k_norm', 'c_w_o', 'f_w_gate_up',
           'f_w_down']
SHARD_AXIS = {'a_w_qkv': 2, 'a_w_o': 1, 'b_w_in': 2, 'b_conv_w': 2, 'b_w_out': 1, 'c_w_down': 1, 'c_q_a_norm': 1,
              'c_kv_a_norm': 1, 'c_w_q_up': 2, 'c_w_kv_up': 2, 'c_w_o': 1, 'f_w_gate_up': 2, 'f_w_down': 1}
SHARDED = [n for n in WEIGHTS if n in SHARD_AXIS]
REPLICATED = [n for n in WEIGHTS if n not in SHARD_AXIS]
GATHER_F32 = ['b_conv_w', 'c_q_a_norm', 'c_kv_a_norm']
GATHER_BF16 = [n for n in SHARDED if n not in GATHER_F32]

def _params(semantics=None):
    return pltpu.CompilerParams(dimension_semantics=semantics, vmem_limit_bytes=VMEM_LIMIT_BYTES)


def _div_tile(n, cap, mult=LANES):
    best = None
    t = mult
    while t <= min(n, cap):
        if n % t == 0:
            best = t
        t += mult
    return n if best is None else best


ANY_SPEC = pl.BlockSpec(memory_space=pl.ANY)


class _Exchange:
    def __init__(self, arrays, scatter):
        self.arrays, self.scatter = list(arrays), scatter
        n = len(self.arrays)
        self.out_shapes = [jax.ShapeDtypeStruct(a.shape if scatter else (a.shape[0], N_DEV) + tuple(a.shape[1:]), a.dtype)
                           for a in self.arrays]
        self.scratch = [pltpu.SemaphoreType.DMA((n, N_DEV - 1)), pltpu.SemaphoreType.DMA((n, N_DEV - 1)),
                        pltpu.SemaphoreType.DMA((n,))]

    def _copies(self, src_refs, out_refs, sems):
        send_sems, recv_sems, local_sems = sems
        x, y, c = lax.axis_index("x"), lax.axis_index("y"), lax.axis_index("c")
        me_idx = 4 * x + 2 * y + c
        n = len(self.arrays)

        def remote(a, k, src, dst, to):
            return pltpu.make_async_remote_copy(src_ref=src, dst_ref=dst, send_sem=send_sems.at[a, k],
                                                recv_sem=recv_sems.at[a, k], device_id=to,
                                                device_id_type=pl.DeviceIdType.MESH)

        local, first, forwards, last = [], [], [], []
        if self.scatter:
            for a in range(n):
                local.append(pltpu.make_async_copy(src_refs[a].at[:, me_idx], out_refs[a].at[:, me_idx], local_sems.at[a]))
                for r in range(1, N_DEV):
                    px = 1 - x if (r >> 2) & 1 else x
                    py = 1 - y if (r >> 1) & 1 else y
                    pc = 1 - c if r & 1 else c
                    cp = remote(a, r - 1, src_refs[a].at[:, 4 * px + 2 * py + pc], out_refs[a].at[:, me_idx], (px, py, pc))
                    first.append(cp)
                    last.append(cp)
            return local, first, forwards, last
        me, sibling = (x, y, c), (x, y, 1 - c)
        chips = [(1 - x, y), (x, 1 - y), (1 - x, 1 - y)]

        def place(a, block):
            return out_refs[a].at[:, 4 * block[0] + 2 * block[1] + block[2]]

        for a in range(n):
            local.append(pltpu.make_async_copy(src_refs[a], place(a, me), local_sems.at[a]))
            first.append(remote(a, 0, src_refs[a], place(a, me), sibling))
            last.append(remote(a, 0, place(a, sibling), place(a, sibling), me))
            for j, chip in enumerate(chips):
                first.append(remote(a, 1 + j, src_refs[a], place(a, me), (*chip, c)))
                forwards.append((remote(a, 1 + j, place(a, (*chip, c)), place(a, (*chip, c)), me),
                                 remote(a, 4 + j, place(a, (*chip, c)), place(a, (*chip, c)), sibling)))
                last.append(remote(a, 4 + j, place(a, (*chip, 1 - c)), place(a, (*chip, 1 - c)), me))
        return local, first, forwards, last

    def start(self, src_refs, out_refs, sems):
        local, first, _, _ = self._copies(src_refs, out_refs, sems)
        for cp in local + first:
            cp.start()

    def finish(self, src_refs, out_refs, sems):
        local, first, forwards, last = self._copies(src_refs, out_refs, sems)
        for arrival, forward in forwards:
            arrival.wait_recv()
            forward.start()
        for cp in last:
            cp.wait_recv()
        for cp in first + [forward for _, forward in forwards]:
            cp.wait_send()
        for cp in local:
            cp.wait()


def _exchange_now(exchange, name):
    n = len(exchange.arrays)

    def body(*refs):
        exchange.start(refs[:n], refs[n:2 * n], refs[2 * n:])
        exchange.finish(refs[:n], refs[n:2 * n], refs[2 * n:])

    return pl.pallas_call(
        body, name=name, out_shape=tuple(exchange.out_shapes), in_specs=[ANY_SPEC] * n, out_specs=(ANY_SPEC,) * n,
        scratch_shapes=exchange.scratch,
    )(*exchange.arrays)


def _host_call(body, rider, name, out_shape, grid, in_specs, out_specs, operands, semantics):
    if rider is None:
        return pl.pallas_call(body, name=name, out_shape=tuple(out_shape), grid=grid, in_specs=list(in_specs),
                              out_specs=tuple(out_specs), compiler_params=_params(semantics))(*operands), None
    n_in, n_out, r = len(in_specs), len(out_shape), len(rider.arrays)

    def riding(*refs):
        ins, rider_in = refs[:n_in], refs[n_in:n_in + r]
        outs, rider_out = refs[n_in + r:n_in + r + n_out], refs[n_in + r + n_out:n_in + 2 * r + n_out]
        sems = refs[n_in + 2 * r + n_out:]
        step = pl.program_id(0)

        @pl.when(step == 0)
        def _():
            rider.start(rider_in, rider_out, sems)

        body(*ins, *outs)

        @pl.when(step == grid[0] - 1)
        def _():
            rider.finish(rider_in, rider_out, sems)

    results = pl.pallas_call(
        riding, name=name, out_shape=tuple(out_shape) + tuple(rider.out_shapes), grid=grid,
        in_specs=list(in_specs) + [ANY_SPEC] * r, out_specs=tuple(out_specs) + (ANY_SPEC,) * r,
        scratch_shapes=rider.scratch, compiler_params=_params(("arbitrary",)),
    )(*operands, *rider.arrays)
    return results[:n_out], results[n_out:]


def _matmul(a, b, mode, name, out_dtype=F32, residual=None):
    if mode == 'nn':
        (m, k), (k2, n) = a.shape, b.shape
    elif mode == 'nt':
        (m, k), (n, k2) = a.shape, b.shape
    else:
        (k, m), (k2, n) = a.shape, b.shape
    assert k == k2, (name, a.shape, b.shape, mode)
    if mode == 'tn':
        tm, tk = _div_tile(m, 1408), _div_tile(k, 512, 16)
    else:
        tm, tk = _div_tile(m, 1024, 16), _div_tile(k, 1536)
    tn = _div_tile(n, 1408)
    nk = k // tk
    dims = {'nn': (((1,), (0,)), ((), ())), 'nt': (((1,), (1,)), ((), ())), 'tn': (((0,), (0,)), ((), ()))}[mode]

    def product(a_ref, b_ref):
        return lax.dot_general(a_ref[...].astype(BF16), b_ref[...].astype(BF16), dims, preferred_element_type=F32)

    def finish(r, rest):
        if residual is not None:
            r = r + rest[0][...]
        rest[-1 if nk == 1 else -2][...] = r.astype(out_dtype)

    def body_single(a_ref, b_ref, *rest):
        finish(product(a_ref, b_ref), rest)

    def body_accumulate(a_ref, b_ref, *rest):
        acc = rest[-1]
        kk = pl.program_id(2)

        @pl.when(kk == 0)
        def _():
            acc[...] = jnp.zeros_like(acc)

        acc[...] += product(a_ref, b_ref)

        @pl.when(kk == nk - 1)
        def _():
            finish(acc[...], rest)

    a_spec = pl.BlockSpec((tk, tm), lambda i, j, kk: (kk, i)) if mode == 'tn' else pl.BlockSpec((tm, tk), lambda i, j, kk: (i, kk))
    b_spec = pl.BlockSpec((tn, tk), lambda i, j, kk: (j, kk)) if mode == 'nt' else pl.BlockSpec((tk, tn), lambda i, j, kk: (kk, j))
    o_spec = pl.BlockSpec((tm, tn), lambda i, j, kk: (i, j))
    in_specs, operands = [a_spec, b_spec], [a, b]
    if residual is not None:
        in_specs.append(o_spec)
        operands.append(residual)
    return pl.pallas_call(
        body_single if nk == 1 else body_accumulate, name=name, out_shape=jax.ShapeDtypeStruct((m, n), out_dtype),
        grid=(m // tm, n // tn, nk), in_specs=in_specs, out_specs=o_spec,
        scratch_shapes=[] if nk == 1 else [pltpu.VMEM((tm, tn), F32)],
        compiler_params=_params(("parallel", "parallel", "arbitrary")),
    )(*operands)


def _row_spec(tm, cols):
    return pl.BlockSpec((tm, cols), lambda i: (i, 0))


def _const_spec(shape):
    return pl.BlockSpec(shape, lambda i: tuple(0 for _ in shape))


def _accumulate(ref, value, step):
    @pl.when(step == 0)
    def _():
        ref[...] = value

    @pl.when(step > 0)
    def _():
        ref[...] += value


def _rstd(x):
    return lax.rsqrt(jnp.mean(x * x, axis=-1, keepdims=True) + EPS)


def _norm_bwd(x, g, dout):
    xn = x * _rstd(x)
    dg = jnp.sum(dout * xn, axis=0, keepdims=True)
    dxn = dout * g
    dx = _rstd(x) * (dxn - xn * jnp.mean(dxn * xn, axis=-1, keepdims=True))
    return dx, dg


def _rmsnorm_fwd(x, g, name):
    t, d = x.shape
    tm = _div_tile(t, 512, 16)

    def body(x_ref, g_ref, o_ref):
        xv = x_ref[...]
        o_ref[...] = (xv * _rstd(xv) * g_ref[...]).astype(BF16)

    return pl.pallas_call(
        body, name=name, out_shape=jax.ShapeDtypeStruct((t, d), BF16), grid=(t // tm,),
        in_specs=[_row_spec(tm, d), _const_spec((1, d))], out_specs=_row_spec(tm, d),
        compiler_params=_params(("parallel",)),
    )(x, g)


def _rmsnorm_bwd(x, g, dh, dres, name):
    t, d = x.shape
    tm = _div_tile(t, 512, 8)

    def body(x_ref, g_ref, dh_ref, dres_ref, dx_ref, dg_ref):
        dx, dg = _norm_bwd(x_ref[...], g_ref[...], dh_ref[...])
        dx_ref[...] = dres_ref[...] + dx
        _accumulate(dg_ref, dg, pl.program_id(0))

    return pl.pallas_call(
        body, name=name,
        out_shape=(jax.ShapeDtypeStruct((t, d), F32), jax.ShapeDtypeStruct((1, d), F32)), grid=(t // tm,),
        in_specs=[_row_spec(tm, d), _const_spec((1, d)), _row_spec(tm, d), _row_spec(tm, d)],
        out_specs=(_row_spec(tm, d), _const_spec((1, d))),
        compiler_params=_params(("arbitrary",)),
    )(x, g, dh, dres)


def _sigmoid(x):
    return 0.5 * jnp.tanh(0.5 * x) + 0.5


def _swiglu_fwd(gu, name):
    t, f2 = gu.shape
    f = f2 // 2
    tm = _div_tile(t, 512, 16)

    def body(gu_ref, o_ref):
        gate, up = gu_ref[:, :f].astype(F32), gu_ref[:, f:].astype(F32)
        o_ref[...] = (gate * _sigmoid(gate) * up).astype(BF16)

    return pl.pallas_call(
        body, name=name, out_shape=jax.ShapeDtypeStruct((t, f), BF16), grid=(t // tm,),
        in_specs=[_row_spec(tm, f2)], out_specs=_row_spec(tm, f),
        compiler_params=_params(("parallel",)),
    )(gu)


def _swiglu_bwd(gu, da, name):
    t, f2 = gu.shape
    f = f2 // 2
    tm = _div_tile(t, 512, 16)

    def body(gu_ref, da_ref, o_ref):
        gate, up, dav = gu_ref[:, :f].astype(F32), gu_ref[:, f:].astype(F32), da_ref[...].astype(F32)
        sig = _sigmoid(gate)
        o_ref[:, :f] = (dav * up * (sig * (1.0 + gate * (1.0 - sig)))).astype(BF16)
        o_ref[:, f:] = (dav * (gate * sig)).astype(BF16)

    return pl.pallas_call(
        body, name=name, out_shape=jax.ShapeDtypeStruct((t, f2), BF16), grid=(t // tm,),
        in_specs=[_row_spec(tm, f2), _row_spec(tm, f)], out_specs=_row_spec(tm, f2),
        compiler_params=_params(("parallel",)),
    )(gu, da)


def _loss_head(y, target, name):
    t, d = y.shape
    tm = _div_tile(t, 512, 8)

    def body(y_ref, t_ref, loss_ref, dy_ref):
        diff = y_ref[...] - t_ref[...]
        dy_ref[...] = diff * (1.0 / d)
        part = jnp.sum(jnp.sum(diff * diff, axis=1, keepdims=True), axis=0, keepdims=True) * (0.5 / d)
        _accumulate(loss_ref, part, pl.program_id(0))

    return pl.pallas_call(
        body, name=name,
        out_shape=(jax.ShapeDtypeStruct((1, 1), F32), jax.ShapeDtypeStruct((t, d), F32)), grid=(t // tm,),
        in_specs=[_row_spec(tm, d), _row_spec(tm, d)], out_specs=(_const_spec((1, 1)), _row_spec(tm, d)),
        compiler_params=_params(("arbitrary",)),
    )(y, target)


HALO = 8


def _shift_down(z, k, halo_rows):
    tm = z.shape[0]
    row = lax.broadcasted_iota(jnp.int32, z.shape, 0)
    out = pltpu.roll(z, k, 0)
    for j in range(k):
        out = jnp.where(row == j, halo_rows[HALO - k + j:HALO - k + j + 1, :], out)
    return out


def _shift_up(z, k, halo_rows):
    tm = z.shape[0]
    row = lax.broadcasted_iota(jnp.int32, z.shape, 0)
    out = pltpu.roll(z, tm - k, 0)
    for j in range(k):
        out = jnp.where(row == tm - k + j, halo_rows[j:j + 1, :], out)
    return out


def _sconv_specs(t, tm, cols):
    per = tm // HALO
    last = t // HALO - 1
    cur = pl.BlockSpec((tm, cols), lambda i: (i, 0))
    prev = pl.BlockSpec((HALO, cols), lambda i: (jnp.maximum(i * per - 1, 0), 0))
    nxt = pl.BlockSpec((HALO, cols), lambda i: (jnp.minimum((i + 1) * per, last), 0))
    return cur, prev, nxt


def _sconv_fwd(bcu, conv_w, name):
    t, d3 = bcu.shape
    d = d3 // 3
    tm = _div_tile(t, 256, 16)
    cur, prev, _ = _sconv_specs(t, tm, d3)

    def body(cur_ref, prev_ref, w_ref, o_ref):
        i = pl.program_id(0)
        z = cur_ref[:, d:2 * d] * cur_ref[:, 2 * d:]
        zp = prev_ref[:, d:2 * d] * prev_ref[:, 2 * d:] * (i > 0).astype(F32)
        y = w_ref[0:1, :] * _shift_down(z, 2, zp) + w_ref[1:2, :] * _shift_down(z, 1, zp) + w_ref[2:3, :] * z
        o_ref[...] = (cur_ref[:, :d] * y).astype(BF16)

    return pl.pallas_call(
        body, name=name, out_shape=jax.ShapeDtypeStruct((t, d), BF16), grid=(t // tm,),
        in_specs=[cur, prev, _const_spec((3, d))], out_specs=_row_spec(tm, d),
        compiler_params=_params(("parallel",)),
    )(bcu, bcu, conv_w)


def _sconv_bwd(bcu, dout, conv_w, name):
    t, d3 = bcu.shape
    d = d3 // 3
    tm = _div_tile(t, 256, 16)
    cur, prev, nxt = _sconv_specs(t, tm, d3)
    dcur, _, dnxt = _sconv_specs(t, tm, d)
    n_tiles = t // tm

    def body(cur_ref, prev_ref, nxt_ref, do_ref, don_ref, w_ref, o_ref, dw_ref):
        i = pl.program_id(0)
        b, cg, u = cur_ref[:, :d], cur_ref[:, d:2 * d], cur_ref[:, 2 * d:]
        z = cg * u
        zp = prev_ref[:, d:2 * d] * prev_ref[:, 2 * d:] * (i > 0).astype(F32)
        z1, z2 = _shift_down(z, 1, zp), _shift_down(z, 2, zp)
        w0, w1, w2 = w_ref[0:1, :], w_ref[1:2, :], w_ref[2:3, :]
        y = w0 * z2 + w1 * z1 + w2 * z
        dov = do_ref[...]
        dy = dov * b
        dyn = don_ref[...] * nxt_ref[:, :d] * (i < n_tiles - 1).astype(F32)
        dz = w2 * dy + w1 * _shift_up(dy, 1, dyn) + w0 * _shift_up(dy, 2, dyn)
        o_ref[:, :d] = (dov * y).astype(BF16)
        o_ref[:, d:2 * d] = (dz * u).astype(BF16)
        o_ref[:, 2 * d:] = (dz * cg).astype(BF16)
        dw = jnp.concatenate([jnp.sum(dy * z2, axis=0, keepdims=True), jnp.sum(dy * z1, axis=0, keepdims=True),
                              jnp.sum(dy * z, axis=0, keepdims=True)], axis=0)
        _accumulate(dw_ref, dw, i)

    return pl.pallas_call(
        body, name=name,
        out_shape=(jax.ShapeDtypeStruct((t, d3), BF16), jax.ShapeDtypeStruct((3, d), F32)), grid=(n_tiles,),
        in_specs=[cur, prev, nxt, dcur, dnxt, _const_spec((3, d))],
        out_specs=(_row_spec(tm, d3), _const_spec((3, d))),
        compiler_params=_params(("arbitrary",)),
    )(bcu, bcu, bcu, dout, dout, conv_w)


def _rope_tables(positions, rot, lead, trail):
    inv_freq = ROPE_THETA ** (-jnp.arange(0, rot, 2, dtype=F32) / rot)
    ang = positions.astype(F32)[:, None] * inv_freq
    cos, sin = jnp.cos(ang), jnp.sin(ang)
    t = positions.shape[0]
    cos_full = jnp.concatenate([jnp.ones((t, lead), F32), cos, cos, jnp.ones((t, trail), F32)], axis=1)
    sin_full = jnp.concatenate([jnp.zeros((t, lead), F32), -sin, sin, jnp.zeros((t, trail), F32)], axis=1)
    return cos_full, sin_full


def _repeat_lanes(x, reps, name):
    t, d = x.shape
    tm = _div_tile(t, 512, 8)

    def body(x_ref, o_ref):
        o_ref[...] = jnp.concatenate([x_ref[...]] * reps, axis=1)

    return pl.pallas_call(
        body, name=name, out_shape=jax.ShapeDtypeStruct((t, reps * d), x.dtype), grid=(t // tm,),
        in_specs=[_row_spec(tm, d)], out_specs=_row_spec(tm, reps * d), compiler_params=_params(("parallel",)),
    )(x)


def _pieces_dot(a, b, pieces):
    total, rest = None, a
    for _ in range(pieces):
        piece = rest.astype(BF16)
        term = jnp.dot(piece, b, preferred_element_type=F32)
        total = term if total is None else total + term
        rest = rest - piece.astype(F32)
    return total


A_Q_COLS = A_HEADS * A_HEAD_DIM
A_KV_COLS = A_KV_HEADS * A_HEAD_DIM
A_COLS = A_Q_COLS + 2 * A_KV_COLS
A_SCALE = A_HEAD_DIM ** -0.5


A_NORMED = A_Q_COLS + A_KV_COLS


def _swa_prep_tables(q_norm, k_norm):
    lane = np.arange(A_NORMED)
    seg = (lane[:, None] // A_HEAD_DIM == np.arange(LANES)[None, :]).astype(np.float32)
    fold = (np.where(lane < A_Q_COLS, 0, A_HEAD_DIM)[:, None] + lane[:, None] % A_HEAD_DIM
            == np.arange(LANES)[None, :]).astype(np.float32)
    gains = jnp.concatenate([jnp.tile(q_norm, (1, A_HEADS)), jnp.tile(k_norm, (1, A_KV_HEADS))], axis=1)
    return gains, jnp.asarray(seg, BF16), jnp.asarray(seg.T, BF16), jnp.asarray(fold, BF16)


def _wide_rstd(x, seg, seg_t):
    mean_sq = _pieces_dot(x * x, seg, 1) * (1.0 / A_HEAD_DIM)
    return _pieces_dot(lax.rsqrt(mean_sq + EPS), seg_t, 2)


def _wide_partner(n):
    dim = lax.broadcasted_iota(jnp.int32, n.shape, 1) & (A_HEAD_DIM - 1)
    half = A_ROT_DIM // 2
    return jnp.where(dim < half, pltpu.roll(n, A_NORMED - half, 1),
                     jnp.where(dim < A_ROT_DIM, pltpu.roll(n, half, 1), 0.0))


def _swa_prep_fwd(qkv, q_norm, k_norm, cos_w, sin_w, name):
    t = qkv.shape[0]
    tm = _div_tile(t, 256, 16)
    gains, seg, seg_t, _ = _swa_prep_tables(q_norm, k_norm)

    def body(x_ref, g_ref, cos_ref, sin_ref, seg_ref, segt_ref, o_ref):
        x = x_ref[:, :A_NORMED]
        n = x * _wide_rstd(x, seg_ref[...], segt_ref[...]) * g_ref[...]
        o_ref[:, :A_NORMED] = (n * cos_ref[...] + _wide_partner(n) * sin_ref[...]).astype(BF16)
        o_ref[:, A_NORMED:] = x_ref[:, A_NORMED:].astype(BF16)

    return pl.pallas_call(
        body, name=name, out_shape=jax.ShapeDtypeStruct((t, A_COLS), BF16), grid=(t // tm,),
        in_specs=[_row_spec(tm, A_COLS), _const_spec((1, A_NORMED)), _row_spec(tm, A_NORMED), _row_spec(tm, A_NORMED),
                  _const_spec(seg.shape), _const_spec(seg_t.shape)],
        out_specs=_row_spec(tm, A_COLS), compiler_params=_params(("parallel",)),
    )(qkv, gains, cos_w, sin_w, seg, seg_t)


def _swa_prep_bwd(qkv, dqkv_r, q_norm, k_norm, cos_w, sin_w, name):
    t = qkv.shape[0]
    tm = _div_tile(t, 256, 16)
    hd = A_HEAD_DIM
    gains, seg, seg_t, fold = _swa_prep_tables(q_norm, k_norm)

    def body(x_ref, d_ref, g_ref, cos_ref, sin_ref, seg_ref, segt_ref, fold_ref, o_ref, dgq_ref, dgk_ref):
        x, dout = x_ref[:, :A_NORMED], d_ref[:, :A_NORMED]
        rstd = _wide_rstd(x, seg_ref[...], segt_ref[...])
        xn = x * rstd
        dn = dout * cos_ref[...] + _wide_partner(dout * sin_ref[...])
        dg = _pieces_dot(jnp.sum(dn * xn, axis=0, keepdims=True), fold_ref[...], 3)
        dxn = dn * g_ref[...]
        mean = _pieces_dot(_pieces_dot(dxn * xn, seg_ref[...], 1) * (1.0 / hd), segt_ref[...], 2)
        o_ref[:, :A_NORMED] = (rstd * (dxn - xn * mean)).astype(BF16)
        o_ref[:, A_NORMED:] = d_ref[:, A_NORMED:].astype(BF16)
        _accumulate(dgq_ref, dg[:, :hd], pl.program_id(0))
        _accumulate(dgk_ref, dg[:, hd:2 * hd], pl.program_id(0))

    return pl.pallas_call(
        body, name=name,
        out_shape=(jax.ShapeDtypeStruct((t, A_COLS), BF16), jax.ShapeDtypeStruct((1, hd), F32),
                   jax.ShapeDtypeStruct((1, hd), F32)),
        grid=(t // tm,),
        in_specs=[_row_spec(tm, A_COLS), _row_spec(tm, A_COLS), _const_spec((1, A_NORMED)), _row_spec(tm, A_NORMED),
                  _row_spec(tm, A_NORMED), _const_spec(seg.shape), _const_spec(seg_t.shape), _const_spec(fold.shape)],
        out_specs=(_row_spec(tm, A_COLS), _const_spec((1, hd)), _const_spec((1, hd))),
        compiler_params=_params(("arbitrary",)),
    )(qkv, dqkv_r, gains, cos_w, sin_w, seg, seg_t, fold)


def _group_rows(ref, k, width=A_HEAD_DIM, base=0):
    return jnp.concatenate([ref[:, base + (A_GROUP * k + g) * width:base + (A_GROUP * k + g + 1) * width]
                            for g in range(A_GROUP)], axis=0)


def _group_column(ref, k, rows):
    cols = []
    for g in range(A_GROUP):
        h = A_GROUP * k + g
        col = ref[:, h:h + 1]
        cols.append(jnp.broadcast_to(col, (rows, 1)) if col.shape[0] == 1 else col)
    return jnp.concatenate(cols, axis=0)


def _swa_fwd(qkv_r, sinks, name, rider=None):
    t = qkv_r.shape[0]
    blk = A_WINDOW
    nb = t // blk
    hd = A_HEAD_DIM
    kv_block = A_Q_COLS // (2 * A_KV_COLS)

    def body(q_ref, kvc_ref, kvp_ref, s_ref, o_ref, lse_ref):
        n = pl.program_id(0)
        shape = (A_GROUP * blk, 2 * blk)
        qpos = lax.broadcasted_iota(jnp.int32, shape, 0) & (blk - 1)
        col = lax.broadcasted_iota(jnp.int32, shape, 1)
        delta = qpos + blk - col
        valid = (delta >= 0) & (delta < A_WINDOW) & ((col >= blk) | (n > 0))
        for k in range(A_KV_HEADS):
            qg = _group_rows(q_ref, k)
            kw = jnp.concatenate([kvp_ref[:, k * hd:(k + 1) * hd], kvc_ref[:, k * hd:(k + 1) * hd]], axis=0)
            vw = jnp.concatenate([kvp_ref[:, A_KV_COLS + k * hd:A_KV_COLS + (k + 1) * hd],
                                  kvc_ref[:, A_KV_COLS + k * hd:A_KV_COLS + (k + 1) * hd]], axis=0)
            s = lax.dot_general(qg, kw, (((1,), (1,)), ((), ())), preferred_element_type=F32) * A_SCALE
            s = jnp.where(valid, s, NEG)
            sink = _group_column(s_ref, k, blk)
            m = jnp.maximum(jnp.max(s, axis=-1, keepdims=True), sink)
            p = jnp.exp(s - m)
            denom = jnp.sum(p, axis=-1, keepdims=True) + jnp.exp(sink - m)
            o = jnp.dot(p.astype(BF16), vw, preferred_element_type=F32) / denom
            lse = m + jnp.log(denom)
            for g in range(A_GROUP):
                h = A_GROUP * k + g
                o_ref[:, h * hd:(h + 1) * hd] = o[g * blk:(g + 1) * blk].astype(BF16)
                lse_ref[:, h:h + 1] = lse[g * blk:(g + 1) * blk]

    return _host_call(
        body, rider, name,
        out_shape=(jax.ShapeDtypeStruct((t, A_Q_COLS), BF16), jax.ShapeDtypeStruct((t, A_HEADS), F32)), grid=(nb,),
        in_specs=[pl.BlockSpec((blk, A_Q_COLS), lambda n: (n, 0)),
                  pl.BlockSpec((blk, 2 * A_KV_COLS), lambda n: (n, kv_block)),
                  pl.BlockSpec((blk, 2 * A_KV_COLS), lambda n: (jnp.maximum(n - 1, 0), kv_block)),
                  _const_spec((1, A_HEADS))],
        out_specs=(pl.BlockSpec((blk, A_Q_COLS), lambda n: (n, 0)), pl.BlockSpec((blk, A_HEADS), lambda n: (n, 0))),
        operands=(qkv_r, qkv_r, qkv_r, sinks), semantics=("parallel",))


def _swa_bwd(qkv_r, o, lse, do, sinks, name, rider=None):
    t = qkv_r.shape[0]
    blk = A_WINDOW
    nb = t // blk
    hd = A_HEAD_DIM
    kv_block = A_Q_COLS // (2 * A_KV_COLS)
    rows = A_GROUP * blk

    def nxt(n):
        return jnp.minimum(n + 1, nb - 1)

    def body(qc_ref, qn_ref, kvc_ref, kvp_ref, doc_ref, don_ref, oc_ref, on_ref, lc_ref, ln_ref, s_ref, dx_ref, ds_ref):
        n = pl.program_id(0)
        shape = (2 * rows, 2 * blk)
        row = lax.broadcasted_iota(jnp.int32, shape, 0)
        col = lax.broadcasted_iota(jnp.int32, shape, 1)
        is_next = row >= rows
        delta = jnp.where(is_next, blk, 0) + blk + (row & (blk - 1)) - col
        valid = ((delta >= 0) & (delta < A_WINDOW) & ((col >= blk) | (n > 0)) & (jnp.logical_not(is_next) | (n < nb - 1)))
        dsink_cols = []
        for k in range(A_KV_HEADS):
            qs = jnp.concatenate([_group_rows(qc_ref, k), _group_rows(qn_ref, k)], axis=0)
            dos = jnp.concatenate([_group_rows(doc_ref, k), _group_rows(don_ref, k)], axis=0)
            os_ = jnp.concatenate([_group_rows(oc_ref, k), _group_rows(on_ref, k)], axis=0).astype(F32)
            lses = jnp.concatenate([_group_column(lc_ref, k, blk), _group_column(ln_ref, k, blk)], axis=0)
            kw = jnp.concatenate([kvp_ref[:, k * hd:(k + 1) * hd], kvc_ref[:, k * hd:(k + 1) * hd]], axis=0)
            vw = jnp.concatenate([kvp_ref[:, A_KV_COLS + k * hd:A_KV_COLS + (k + 1) * hd],
                                  kvc_ref[:, A_KV_COLS + k * hd:A_KV_COLS + (k + 1) * hd]], axis=0)
            s = lax.dot_general(qs, kw, (((1,), (1,)), ((), ())), preferred_element_type=F32) * A_SCALE
            p = jnp.exp(jnp.where(valid, s - lses, NEG))
            dos_b = dos.astype(BF16)
            dp = lax.dot_general(dos_b, vw, (((1,), (1,)), ((), ())), preferred_element_type=F32)
            dlt = jnp.sum(dos * os_, axis=-1, keepdims=True)
            ds = p * (dp - dlt)
            dq = jnp.dot(ds[:rows].astype(BF16), kw, preferred_element_type=F32) * A_SCALE
            dk = lax.dot_general(ds[:, blk:].astype(BF16), qs, (((0,), (0,)), ((), ())), preferred_element_type=F32) * A_SCALE
            dv = lax.dot_general(p[:, blk:].astype(BF16), dos_b, (((0,), (0,)), ((), ())), preferred_element_type=F32)
            for g in range(A_GROUP):
                h = A_GROUP * k + g
                dx_ref[:, h * hd:(h + 1) * hd] = dq[g * blk:(g + 1) * blk]
            dx_ref[:, A_Q_COLS + k * hd:A_Q_COLS + (k + 1) * hd] = dk
            dx_ref[:, A_Q_COLS + A_KV_COLS + k * hd:A_Q_COLS + A_KV_COLS + (k + 1) * hd] = dv
            sink = _group_column(s_ref, k, blk)
            contrib = -jnp.exp(sink - lses[:rows]) * dlt[:rows]
            for g in range(A_GROUP):
                dsink_cols.append(jnp.sum(contrib[g * blk:(g + 1) * blk], axis=0, keepdims=True))
        _accumulate(ds_ref, jnp.concatenate(dsink_cols, axis=1), n)

    q_spec = lambda f: pl.BlockSpec((blk, A_Q_COLS), lambda n: (f(n), 0))
    l_spec = lambda f: pl.BlockSpec((blk, A_HEADS), lambda n: (f(n), 0))
    same = lambda n: n
    return _host_call(
        body, rider, name,
        out_shape=(jax.ShapeDtypeStruct((t, A_COLS), F32), jax.ShapeDtypeStruct((1, A_HEADS), F32)), grid=(nb,),
        in_specs=[q_spec(same), q_spec(nxt),
                  pl.BlockSpec((blk, 2 * A_KV_COLS), lambda n: (n, kv_block)),
                  pl.BlockSpec((blk, 2 * A_KV_COLS), lambda n: (jnp.maximum(n - 1, 0), kv_block)),
                  q_spec(same), q_spec(nxt), q_spec(same), q_spec(nxt), l_spec(same), l_spec(nxt),
                  _const_spec((1, A_HEADS))],
        out_specs=(pl.BlockSpec((blk, A_COLS), lambda n: (n, 0)), _const_spec((1, A_HEADS))),
        operands=(qkv_r, qkv_r, qkv_r, qkv_r, do, do, o, o, lse, lse, sinks), semantics=("arbitrary",))


C_DOWN_COLS = C_Q_RANK + C_KV_RANK + C_ROPE
C_Q_COLS = C_HEADS * C_QK
C_KV_COLS = C_HEADS * (C_NOPE + C_V)
C_O_COLS = C_HEADS * C_V
C_PAD = LANES
C_SCALE = C_QK ** -0.5
LOG2E = 1.4426950408889634
LN2 = 0.6931471805599453
C_Q_SCALE = C_SCALE * LOG2E
C_PAIR = 2


def _mla_latent_fwd(down, q_a_norm, kv_a_norm, name):
    t = down.shape[0]
    tm = _div_tile(t, 512, 16)

    def body(x_ref, gq_ref, gk_ref, cq_ref, ckv_ref):
        cq, ckv = x_ref[:, :C_Q_RANK], x_ref[:, C_Q_RANK:C_Q_RANK + C_KV_RANK]
        cq_ref[...] = (cq * _rstd(cq) * gq_ref[...]).astype(BF16)
        ckv_ref[...] = (ckv * _rstd(ckv) * gk_ref[...]).astype(BF16)

    return pl.pallas_call(
        body, name=name,
        out_shape=(jax.ShapeDtypeStruct((t, C_Q_RANK), BF16), jax.ShapeDtypeStruct((t, C_KV_RANK), BF16)), grid=(t // tm,),
        in_specs=[_row_spec(tm, C_DOWN_COLS), _const_spec((1, C_Q_RANK)), _const_spec((1, C_KV_RANK))],
        out_specs=(_row_spec(tm, C_Q_RANK), _row_spec(tm, C_KV_RANK)), compiler_params=_params(("parallel",)),
    )(down, q_a_norm, kv_a_norm)


def _mla_latent_bwd(down, dcq, dckv, dkrope, q_a_norm, kv_a_norm, name):
    t = down.shape[0]
    tm = _div_tile(t, 512, 16)

    def body(x_ref, dcq_ref, dckv_ref, dkr_ref, gq_ref, gk_ref, o_ref, dgq_ref, dgk_ref):
        dq, dgq = _norm_bwd(x_ref[:, :C_Q_RANK], gq_ref[...], dcq_ref[...])
        dkv, dgk = _norm_bwd(x_ref[:, C_Q_RANK:C_Q_RANK + C_KV_RANK], gk_ref[...], dckv_ref[...])
        o_ref[...] = jnp.concatenate([dq, dkv, dkr_ref[...]], axis=1).astype(BF16)
        _accumulate(dgq_ref, dgq, pl.program_id(0))
        _accumulate(dgk_ref, dgk, pl.program_id(0))

    return pl.pallas_call(
        body, name=name,
        out_shape=(jax.ShapeDtypeStruct((t, C_DOWN_COLS), BF16), jax.ShapeDtypeStruct((1, C_Q_RANK), F32),
                   jax.ShapeDtypeStruct((1, C_KV_RANK), F32)),
        grid=(t // tm,),
        in_specs=[_row_spec(tm, C_DOWN_COLS), _row_spec(tm, C_Q_RANK), _row_spec(tm, C_KV_RANK), _row_spec(tm, C_ROPE),
                  _const_spec((1, C_Q_RANK)), _const_spec((1, C_KV_RANK))],
        out_specs=(_row_spec(tm, C_DOWN_COLS), _const_spec((1, C_Q_RANK)), _const_spec((1, C_KV_RANK))),
        compiler_params=_params(("arbitrary",)),
    )(down, dcq, dckv, dkrope, q_a_norm, kv_a_norm)


def _head_major_spec(tm, width):
    return pl.BlockSpec((C_HEADS, tm, width), lambda i: (0, i, 0))


C_NOPE_V = C_NOPE + C_V
C_ROPE_COLS = C_HEADS * C_ROPE


def _mla_prep_tables(q_norm, k_norm):
    ql, kl, rl = np.arange(C_Q_COLS), np.arange(C_KV_COLS), np.arange(C_ROPE_COLS)
    col = np.arange(LANES)[None, :]
    is_nope = (kl % C_NOPE_V) < C_NOPE
    one_hot = lambda m: jnp.asarray(m.astype(np.float32), BF16)
    gk_nope = jnp.concatenate([k_norm[:, :C_NOPE], jnp.zeros((1, C_V), F32)], axis=1)
    return dict(
        q_dim=jnp.asarray((ql % C_QK)[None, :], jnp.int32),
        q_gain=jnp.tile(q_norm, (1, C_HEADS)),
        q_seg=one_hot(ql[:, None] // C_QK == col), q_spread=one_hot((ql[:, None] // C_QK == col).T),
        q_fold=one_hot(ql[:, None] % C_QK == col),
        k_nope=jnp.asarray(is_nope[None, :].astype(np.float32)),
        k_gain=jnp.tile(gk_nope, (1, C_HEADS)),
        k_seg=one_hot(is_nope[:, None] & (kl[:, None] // C_NOPE_V == col)),
        k_spread=one_hot((kl[:, None] // C_NOPE_V == col).T),
        k_fold=one_hot(is_nope[:, None] & (kl[:, None] % C_NOPE_V == col)),
        r_gain=jnp.tile(k_norm[:, C_NOPE:], (1, C_HEADS)),
        r_rep=one_hot(np.arange(C_ROPE)[:, None] == rl[None, :] % C_ROPE),
        r_seg=one_hot(rl[:, None] // C_ROPE == col), r_spread=one_hot((rl[:, None] // C_ROPE == col).T),
        r_fold=one_hot(rl[:, None] % C_ROPE == col))


def _q_partner(n, dim):
    half = C_ROPE // 2
    return jnp.where((dim >= C_NOPE) & (dim < C_NOPE + half), pltpu.roll(n, C_Q_COLS - half, 1),
                     jnp.where(dim >= C_NOPE + half, pltpu.roll(n, half, 1), 0.0))


def _rope_partner(n):
    half = C_ROPE // 2
    dim = lax.broadcasted_iota(jnp.int32, n.shape, 1) & (C_ROPE - 1)
    return jnp.where(dim < half, pltpu.roll(n, C_ROPE_COLS - half, 1), pltpu.roll(n, half, 1))


def _head_rstd(sum_sq):
    return lax.rsqrt(sum_sq * (1.0 / C_QK) + EPS)


MLA_PREP_FWD_TABLES = ['q_dim', 'q_gain', 'q_seg', 'q_spread', 'k_gain', 'k_seg', 'k_spread', 'r_gain', 'r_rep', 'r_spread']
MLA_PREP_BWD_TABLES = MLA_PREP_FWD_TABLES + ['q_fold', 'k_nope', 'k_fold', 'r_seg', 'r_fold']


def _mla_qk_fwd(qw, kvw, down, q_norm, k_norm, rows, name):
    t = qw.shape[0]
    tm = _div_tile(t, 256, 16)
    tables = _mla_prep_tables(q_norm, k_norm)
    consts = [tables[n] for n in MLA_PREP_FWD_TABLES]

    def body(q_ref, kv_ref, dn_ref, qcos_ref, qsin_ref, rcos_ref, rsin_ref, *rest):
        c = {n: r[...] for n, r in zip(MLA_PREP_FWD_TABLES, rest)}
        qo_ref, ko_ref, vo_ref = rest[len(MLA_PREP_FWD_TABLES):]
        q, kv, kr = q_ref[...], kv_ref[...], dn_ref[:, C_Q_RANK + C_KV_RANK:]
        nq = q * _pieces_dot(_head_rstd(_pieces_dot(q * q, c['q_seg'], 1)), c['q_spread'], 2) * c['q_gain']
        out_q = (nq * qcos_ref[...] + _q_partner(nq, c['q_dim']) * qsin_ref[...]) * C_Q_SCALE
        rstd = _head_rstd(_pieces_dot(kv * kv, c['k_seg'], 1) + jnp.sum(kr * kr, axis=-1, keepdims=True))
        nope = kv * _pieces_dot(rstd, c['k_spread'], 2) * c['k_gain']
        nr = _pieces_dot(kr, c['r_rep'], 2) * _pieces_dot(rstd, c['r_spread'], 2) * c['r_gain']
        rope = nr * rcos_ref[...] + _rope_partner(nr) * rsin_ref[...]
        pad = jnp.zeros((tm, C_PAD - C_QK), F32)
        one_then_zeros = (lax.broadcasted_iota(jnp.int32, (tm, C_PAD - C_V), 1) == 0).astype(F32)
        for h in range(C_HEADS):
            qo_ref[h] = jnp.concatenate([out_q[:, h * C_QK:(h + 1) * C_QK], pad], axis=1).astype(BF16)
            ko_ref[h] = jnp.concatenate([nope[:, h * C_NOPE_V:h * C_NOPE_V + C_NOPE], rope[:, h * C_ROPE:(h + 1) * C_ROPE],
                                         pad], axis=1).astype(BF16)
            vo_ref[h] = jnp.concatenate([kv[:, h * C_NOPE_V + C_NOPE:(h + 1) * C_NOPE_V], one_then_zeros],
                                        axis=1).astype(BF16)

    return pl.pallas_call(
        body, name=name,
        out_shape=(jax.ShapeDtypeStruct((C_HEADS, t, C_PAD), BF16), jax.ShapeDtypeStruct((C_HEADS, t, C_PAD), BF16),
                   jax.ShapeDtypeStruct((C_HEADS, t, C_PAD), BF16)),
        grid=(t // tm,),
        in_specs=[_row_spec(tm, C_Q_COLS), _row_spec(tm, C_KV_COLS), _row_spec(tm, C_DOWN_COLS)]
                 + [_row_spec(tm, r.shape[1]) for r in rows] + [_const_spec(a.shape) for a in consts],
        out_specs=(_head_major_spec(tm, C_PAD), _head_major_spec(tm, C_PAD), _head_major_spec(tm, C_PAD)),
        compiler_params=_params(("parallel",)),
    )(qw, kvw, down, *rows, *consts)


def _mla_qk_bwd(qw, kvw, down, dq, dk, dv, q_norm, k_norm, rows, name, rider=None):
    t = qw.shape[0]
    tm = _div_tile(t, 256, 16)
    tables = _mla_prep_tables(q_norm, k_norm)
    consts = [tables[n] for n in MLA_PREP_BWD_TABLES]

    def body(q_ref, kv_ref, dn_ref, dq_ref, dk_ref, dv_ref, qcos_ref, qsin_ref, rcos_ref, rsin_ref, *rest):
        c = {n: r[...] for n, r in zip(MLA_PREP_BWD_TABLES, rest)}
        dqw_ref, dkvw_ref, dkr_ref, dgq_ref, dgk_ref = rest[len(MLA_PREP_BWD_TABLES):]
        step = pl.program_id(0)
        q, kv, kr = q_ref[...], kv_ref[...], dn_ref[:, C_Q_RANK + C_KV_RANK:]
        dout_q = jnp.concatenate([dq_ref[h][:, :C_QK] for h in range(C_HEADS)], axis=1)
        d_slab = jnp.concatenate([x for h in range(C_HEADS) for x in (dk_ref[h][:, :C_NOPE], dv_ref[h])], axis=1)
        d_rope = jnp.concatenate([dk_ref[h][:, C_NOPE:C_QK] for h in range(C_HEADS)], axis=1)

        rq = _pieces_dot(_head_rstd(_pieces_dot(q * q, c['q_seg'], 1)), c['q_spread'], 2)
        xq = q * rq
        dnq = dout_q * qcos_ref[...] + _q_partner(dout_q * qsin_ref[...], c['q_dim'])
        dgq = _pieces_dot(jnp.sum(dnq * xq, axis=0, keepdims=True), c['q_fold'], 3)
        dxq = dnq * c['q_gain']
        mean_q = _pieces_dot(_pieces_dot(dxq * xq, c['q_seg'], 1) * (1.0 / C_QK), c['q_spread'], 2)
        dqw_ref[...] = (rq * (dxq - xq * mean_q)).astype(BF16)

        rstd = _head_rstd(_pieces_dot(kv * kv, c['k_seg'], 1) + jnp.sum(kr * kr, axis=-1, keepdims=True))
        r_nope, r_rope = _pieces_dot(rstd, c['k_spread'], 2), _pieces_dot(rstd, c['r_spread'], 2)
        x_nope = kv * r_nope * c['k_nope']
        x_rope = _pieces_dot(kr, c['r_rep'], 2) * r_rope
        dn_nope = d_slab * c['k_nope']
        dn_rope = d_rope * rcos_ref[...] + _rope_partner(d_rope * rsin_ref[...])
        dg_nope = _pieces_dot(jnp.sum(dn_nope * x_nope, axis=0, keepdims=True), c['k_fold'], 3)
        dg_rope = _pieces_dot(jnp.sum(dn_rope * x_rope, axis=0, keepdims=True), c['r_fold'], 3)
        dx_nope, dx_rope = dn_nope * c['k_gain'], dn_rope * c['r_gain']
        mean = (_pieces_dot(dx_nope * x_nope, c['k_seg'], 1) + _pieces_dot(dx_rope * x_rope, c['r_seg'], 1)) * (1.0 / C_QK)
        g_nope = r_nope * (dx_nope - x_nope * _pieces_dot(mean, c['k_spread'], 2))
        g_rope = r_rope * (dx_rope - x_rope * _pieces_dot(mean, c['r_spread'], 2))
        dkvw_ref[...] = jnp.where(c['k_nope'] > 0.0, g_nope, d_slab).astype(BF16)
        dkr_ref[...] = _pieces_dot(g_rope, c['r_fold'], 3)[:, :C_ROPE]
        _accumulate(dgq_ref, dgq[:, :C_QK], step)
        _accumulate(dgk_ref, jnp.concatenate([dg_nope[:, :C_NOPE], dg_rope[:, :C_ROPE]], axis=1), step)

    return _host_call(
        body, rider, name,
        out_shape=(jax.ShapeDtypeStruct((t, C_Q_COLS), BF16), jax.ShapeDtypeStruct((t, C_KV_COLS), BF16),
                   jax.ShapeDtypeStruct((t, C_ROPE), F32), jax.ShapeDtypeStruct((1, C_QK), F32),
                   jax.ShapeDtypeStruct((1, C_QK), F32)),
        grid=(t // tm,),
        in_specs=[_row_spec(tm, C_Q_COLS), _row_spec(tm, C_KV_COLS), _row_spec(tm, C_DOWN_COLS),
                  _head_major_spec(tm, C_PAD), _head_major_spec(tm, C_PAD), _head_major_spec(tm, C_V)]
                 + [_row_spec(tm, r.shape[1]) for r in rows] + [_const_spec(a.shape) for a in consts],
        out_specs=(_row_spec(tm, C_Q_COLS), _row_spec(tm, C_KV_COLS), _row_spec(tm, C_ROPE), _const_spec((1, C_QK)),
                   _const_spec((1, C_QK))),
        operands=(qw, kvw, down, dq, dk, dv, *rows, *consts), semantics=("arbitrary",))


def _causal_keep(rows, cols, row_offset=0, transposed=False):
    row = lax.broadcasted_iota(jnp.int32, (rows, cols), 0) + row_offset
    col = lax.broadcasted_iota(jnp.int32, (rows, cols), 1)
    return (row <= col) if transposed else (col <= row)


def _mla_fwd(q, k, v, name):
    _, t, _ = q.shape
    bq, bk = min(MLA_FWD_Q_BLOCK, t), min(MLA_FWD_K_BLOCK, t)
    nq = t // bq

    def body(q_ref, k_ref, v_ref, o_ref, lse_ref, m_sc, acc_sc):
        qi = pl.program_id(1)
        m_sc[...] = jnp.full_like(m_sc, NEG)
        acc_sc[...] = jnp.zeros_like(acc_sc)
        diagonal = (qi * bq) // bk
        lead = qi * bq - diagonal * bk

        def step(ki, masked):
            rows = pl.ds(pl.multiple_of(ki * bk, bk), bk)
            for hh in range(C_PAIR):
                s = lax.dot_general(q_ref[hh], k_ref[hh, rows, :], (((1,), (1,)), ((), ())), preferred_element_type=F32)
                if masked:
                    s = jnp.where(_causal_keep(bq, bk, lead), s, NEG)
                m_prev = m_sc[hh]
                m_new = jnp.maximum(m_prev, jnp.max(s, axis=-1, keepdims=True))
                p = jnp.exp2(s - m_new)
                acc_sc[hh] = jnp.exp2(m_prev - m_new) * acc_sc[hh] + jnp.dot(p.astype(BF16), v_ref[hh, rows, :],
                                                                                preferred_element_type=F32)
                m_sc[hh] = m_new

        def below_diagonal(ki, carry):
            step(ki, False)
            return carry

        lax.fori_loop(0, diagonal, below_diagonal, 0)
        step(diagonal, True)
        outs = []
        for hh in range(C_PAIR):
            denom = acc_sc[hh, :, C_V:C_V + 1]
            outs.append(acc_sc[hh, :, :C_V] / denom)
            lse_ref[hh] = m_sc[hh] + jnp.log(denom) * LOG2E
        o_ref[...] = jnp.concatenate(outs, axis=1).astype(BF16)

    whole = lambda hp, qi: (hp, 0, 0)
    return pl.pallas_call(
        body, name=name,
        out_shape=(jax.ShapeDtypeStruct((t, C_O_COLS), BF16), jax.ShapeDtypeStruct((C_HEADS, t, 1), F32)),
        grid=(C_HEADS // C_PAIR, nq),
        in_specs=[pl.BlockSpec((C_PAIR, bq, C_PAD), lambda hp, qi: (hp, qi, 0)),
                  pl.BlockSpec((C_PAIR, t, C_PAD), whole, pipeline_mode=pl.Buffered(1)),
                  pl.BlockSpec((C_PAIR, t, C_PAD), whole, pipeline_mode=pl.Buffered(1))],
        out_specs=(pl.BlockSpec((bq, C_PAIR * C_V), lambda hp, qi: (qi, hp)),
                   pl.BlockSpec((C_PAIR, bq, 1), lambda hp, qi: (hp, qi, 0))),
        scratch_shapes=[pltpu.VMEM((C_PAIR, bq, 1), F32), pltpu.VMEM((C_PAIR, bq, C_PAD), F32)],
        compiler_params=_params(("parallel", "arbitrary")),
    )(q, k, v)


def _mla_delta(do, o, name):
    t = do.shape[0]
    blk = min(MLA_BLOCK, t)

    def body(do_ref, o_ref, dlt_ref, dob_ref):
        for hh in range(C_PAIR):
            do_h = do_ref[:, hh * C_V:(hh + 1) * C_V]
            dlt_ref[hh] = jnp.sum(do_h * o_ref[:, hh * C_V:(hh + 1) * C_V].astype(F32), axis=-1, keepdims=True)
        dob_ref[...] = do_ref[...].astype(BF16)

    wide = pl.BlockSpec((blk, C_PAIR * C_V), lambda hp, i: (i, hp))
    return pl.pallas_call(
        body, name=name,
        out_shape=(jax.ShapeDtypeStruct((C_HEADS, t, 1), F32), jax.ShapeDtypeStruct(do.shape, BF16)),
        grid=(C_HEADS // C_PAIR, t // blk), in_specs=[wide, wide],
        out_specs=(pl.BlockSpec((C_PAIR, blk, 1), lambda hp, i: (hp, i, 0)), wide),
        compiler_params=_params(("parallel", "parallel")),
    )(do, o)


def _mla_bwd(q, k, v, do_b, lse_rows, dlt_rows, name):
    _, t, _ = q.shape
    blk = min(MLA_BLOCK, t)
    nq = t // blk

    def body(q_ref, k_ref, v_ref, do_ref, lse_ref, dlt_ref, dq_hbm, dk_ref, dv_ref, dq_sc, dk_sc, dv_sc, sem):
        hp, ki = pl.program_id(0), pl.program_id(1)

        @pl.when(ki == 0)
        def _():
            dq_sc[...] = jnp.zeros_like(dq_sc)

        dk_sc[...] = jnp.zeros_like(dk_sc)
        dv_sc[...] = jnp.zeros_like(dv_sc)

        def step(qi, masked):
            rows = pl.ds(pl.multiple_of(qi * blk, blk), blk)
            for hh in range(C_PAIR):
                qb = q_ref[hh, rows, :]
                dob = do_ref[rows, hh * C_V:(hh + 1) * C_V]
                s = lax.dot_general(k_ref[hh], qb, (((1,), (1,)), ((), ())), preferred_element_type=F32)
                if masked:
                    s = jnp.where(_causal_keep(blk, blk, transposed=True), s, NEG)
                p = jnp.exp2(s - lse_ref[hh, qi])
                dp = lax.dot_general(v_ref[hh, :, :C_V], dob, (((1,), (1,)), ((), ())), preferred_element_type=F32)
                ds = (p * (dp - dlt_ref[hh, qi])).astype(BF16)
                dv_sc[hh] += jnp.dot(p.astype(BF16), dob, preferred_element_type=F32)
                dk_sc[hh] += jnp.dot(ds, qb, preferred_element_type=F32)
                dq_sc[hh, rows, :] += lax.dot_general(ds, k_ref[hh], (((0,), (0,)), ((), ())), preferred_element_type=F32)

        def above_diagonal(qi, carry):
            step(qi, False)
            return carry

        step(ki, True)
        lax.fori_loop(ki + 1, nq, above_diagonal, 0)
        dk_ref[...] = dk_sc[...] * LN2
        dv_ref[...] = dv_sc[...]

        @pl.when(ki == nq - 1)
        def _():
            dq_sc[...] = dq_sc[...] * C_SCALE
            out = pltpu.make_async_copy(dq_sc, dq_hbm.at[pl.ds(hp * C_PAIR, C_PAIR)], sem)
            out.start()
            out.wait()

    once = pl.Buffered(1)
    whole = lambda hp, ki: (hp, 0, 0)
    whole4 = lambda hp, ki: (hp, 0, 0, 0)
    kmap = lambda hp, ki: (hp, ki, 0)
    return pl.pallas_call(
        body, name=name,
        out_shape=(jax.ShapeDtypeStruct((C_HEADS, t, C_PAD), F32), jax.ShapeDtypeStruct((C_HEADS, t, C_PAD), F32),
                   jax.ShapeDtypeStruct((C_HEADS, t, C_V), F32)),
        grid=(C_HEADS // C_PAIR, nq),
        in_specs=[pl.BlockSpec((C_PAIR, t, C_PAD), whole, pipeline_mode=once), pl.BlockSpec((C_PAIR, blk, C_PAD), kmap),
                  pl.BlockSpec((C_PAIR, blk, C_PAD), kmap),
                  pl.BlockSpec((t, C_PAIR * C_V), lambda hp, ki: (0, hp), pipeline_mode=once),
                  pl.BlockSpec((C_PAIR, nq, 1, blk), whole4, pipeline_mode=once),
                  pl.BlockSpec((C_PAIR, nq, 1, blk), whole4, pipeline_mode=once)],
        out_specs=(pl.BlockSpec(memory_space=pl.ANY), pl.BlockSpec((C_PAIR, blk, C_PAD), kmap),
                   pl.BlockSpec((C_PAIR, blk, C_V), kmap)),
        scratch_shapes=[pltpu.VMEM((C_PAIR, t, C_PAD), F32), pltpu.VMEM((C_PAIR, blk, C_PAD), F32),
                        pltpu.VMEM((C_PAIR, blk, C_V), F32), pltpu.SemaphoreType.DMA(())],
        compiler_params=_params(("arbitrary", "arbitrary")),
    )(q, k, v, do_b, lse_rows, dlt_rows)


def _adamw(parts, w, m, v, name):
    layers, rows, cols = w.shape
    tm = _div_tile(rows, 256, 16)

    def body(p_ref, w_ref, m_ref, v_ref, g_ref, d_ref, nm_ref, nv_ref):
        g = p_ref[0].astype(F32)
        for j in range(1, N_DEV):
            g = g + p_ref[j].astype(F32)
        nm = ADAM_B1 * m_ref[...] + (1.0 - ADAM_B1) * g
        nv = ADAM_B2 * v_ref[...] + (1.0 - ADAM_B2) * jnp.square(g)
        m_hat = nm / (1.0 - ADAM_B1 ** ADAM_STEP)
        v_hat = nv / (1.0 - ADAM_B2 ** ADAM_STEP)
        g_ref[...] = g
        d_ref[...] = -ADAM_LR * (m_hat / (jnp.sqrt(v_hat) + ADAM_EPS) + ADAM_WD * w_ref[...])
        nm_ref[...] = nm
        nv_ref[...] = nv

    spec = pl.BlockSpec((None, tm, cols), lambda l, i: (l, i, 0))
    return pl.pallas_call(
        body, name=name, out_shape=tuple(jax.ShapeDtypeStruct(w.shape, F32) for _ in range(4)),
        grid=(layers, rows // tm),
        in_specs=[pl.BlockSpec((None, N_DEV, tm, cols), lambda l, i: (l, 0, i, 0)), spec, spec, spec],
        out_specs=(spec, spec, spec, spec), compiler_params=_params(("parallel", "parallel")),
    )(parts, w, m, v)


def _join_shards(gathered, axis):
    moved = jnp.moveaxis(gathered, 1, axis)
    shape = list(moved.shape)
    shape[axis:axis + 2] = [shape[axis] * shape[axis + 1]]
    return moved.reshape(shape)


def _split_shards(full, axis):
    shape = list(full.shape)
    shape[axis:axis + 1] = [N_DEV, shape[axis] // N_DEV]
    return jnp.moveaxis(full.reshape(shape), axis, 1)


def _as_rows(shape):
    rest = tuple(shape[1:])
    return (shape[0], 1, rest[0]) if len(rest) == 1 else (shape[0],) + rest


MIXER_WEIGHTS = {0: ['a_w_qkv', 'a_w_o'], 1: ['b_w_in', 'b_conv_w', 'b_w_out'],
                 2: ['c_w_down', 'c_q_a_norm', 'c_kv_a_norm', 'c_w_q_up', 'c_w_kv_up', 'c_w_o']}


def _layer_units(i):
    return [(n, i // N_MIXERS) for n in MIXER_WEIGHTS[i % N_MIXERS]] + [('f_w_gate_up', i), ('f_w_down', i)]


def _forward_backward(x, positions, target, local, rep):
    def gather(units):
        return _Exchange([local[n][i:i + 1].astype(BF16) if n in GATHER_BF16 else local[n][i:i + 1] for n, i in units],
                         scatter=False)

    w = {n: {} for n in SHARDED}

    def arrived(units, gathered):
        for (n, i), g in zip(units, gathered):
            full = _join_shards(g, SHARD_AXIS[n])
            w[n][i] = full if full.ndim == 2 else full[0]

    all_units = [u for i in range(DEPTH) for u in _layer_units(i)]
    first_units = _layer_units(0) + [u for u in all_units if u[0] in GATHER_F32]
    later_units = [u for u in all_units if u not in first_units]
    arrived(first_units, _exchange_now(gather(first_units), "gather_first_weights"))

    cos_a, sin_a = (_repeat_lanes(tbl, A_HEADS + A_KV_HEADS, f"a_rope_table_{i}")
                    for i, tbl in enumerate(_rope_tables(positions, A_ROT_DIM, 0, A_HEAD_DIM - A_ROT_DIM)))
    tables_c = _rope_tables(positions, C_ROPE, C_NOPE, 0)
    rows_c = ([_repeat_lanes(tbl, C_HEADS, f"c_rope_table_q{i}") for i, tbl in enumerate(tables_c)]
              + [_repeat_lanes(tbl[:, C_NOPE:], C_HEADS, f"c_rope_table_k{i}") for i, tbl in enumerate(tables_c)])
    saved = []
    for i in range(DEPTH):
        kind, j = i % N_MIXERS, i // N_MIXERS
        s = {'x': x}
        h1 = _rmsnorm_fwd(x, rep['mix_norm'][i:i + 1], f"mix_norm_fwd_{i}")
        s['h1'] = h1
        if kind == 0:
            s['qkv'] = _matmul(h1, w['a_w_qkv'][j], 'nn', f"a_qkv_{i}")
            s['qkv_r'] = _swa_prep_fwd(s['qkv'], rep['a_q_norm'][j:j + 1], rep['a_k_norm'][j:j + 1], cos_a, sin_a,
                                       f"a_prep_fwd_{i}")
            (s['o'], s['lse']), gathered = _swa_fwd(s['qkv_r'], rep['a_sinks'][j:j + 1], f"a_attn_fwd_{i}",
                                                    rider=gather(later_units) if i == 0 else None)
            if i == 0:
                arrived(later_units, gathered)
            x1 = _matmul(s['o'], w['a_w_o'][j], 'nn', f"a_out_{i}", residual=x)
        elif kind == 1:
            s['bcu'] = _matmul(h1, w['b_w_in'][j], 'nn', f"b_in_{i}")
            s['by'] = _sconv_fwd(s['bcu'], w['b_conv_w'][j], f"b_conv_fwd_{i}")
            x1 = _matmul(s['by'], w['b_w_out'][j], 'nn', f"b_out_{i}", residual=x)
        else:
            s['down'] = _matmul(h1, w['c_w_down'][j], 'nn', f"c_down_{i}")
            s['cq'], s['ckv'] = _mla_latent_fwd(s['down'], w['c_q_a_norm'][j], w['c_kv_a_norm'][j],
                                                f"c_latent_fwd_{i}")
            s['qw'] = _matmul(s['cq'], w['c_w_q_up'][j], 'nn', f"c_q_up_{i}")
            s['kvw'] = _matmul(s['ckv'], w['c_w_kv_up'][j], 'nn', f"c_kv_up_{i}")
            s['q'], s['k'], s['v'] = _mla_qk_fwd(s['qw'], s['kvw'], s['down'], rep['c_q_norm'][j:j + 1],
                                                 rep['c_k_norm'][j:j + 1], rows_c, f"c_prep_fwd_{i}")
            s['o'], s['lse'] = _mla_fwd(s['q'], s['k'], s['v'], f"c_attn_fwd_{i}")
            x1 = _matmul(s['o'], w['c_w_o'][j], 'nn', f"c_out_{i}", residual=x)
        s['x1'] = x1
        s['h2'] = _rmsnorm_fwd(x1, rep['ffn_norm'][i:i + 1], f"ffn_norm_fwd_{i}")
        s['gu'] = _matmul(s['h2'], w['f_w_gate_up'][i], 'nn', f"f_gate_up_{i}", out_dtype=BF16)
        s['act'] = _swiglu_fwd(s['gu'], f"f_act_fwd_{i}")
        x = _matmul(s['act'], w['f_w_down'][i], 'nn', f"f_down_{i}", residual=x1)
        saved.append(s)

    loss, dx = _loss_head(x, target, "loss_head")

    per_layer = {n: {} for n in WEIGHTS}
    received = {}
    sent = set()

    def ready():
        units = [(n, j) for n in SHARDED for j in sorted(per_layer[n]) if (n, j) not in sent]
        if not units:
            return None, units
        sent.update(units)
        blocks = []
        for n, j in units:
            g = per_layer[n][j]
            blocks.append(_split_shards(g if n in ('c_q_a_norm', 'c_kv_a_norm') else g[None], SHARD_AXIS[n]))
        return _Exchange(blocks, scatter=True), units

    for i in reversed(range(DEPTH)):
        kind, j = i % N_MIXERS, i // N_MIXERS
        s = saved[i]
        per_layer['f_w_down'][i] = _matmul(s['act'], dx, 'tn', f"f_down_dw_{i}", out_dtype=BF16)
        dact = _matmul(dx, w['f_w_down'][i], 'nt', f"f_down_dx_{i}", out_dtype=BF16)
        dgu = _swiglu_bwd(s['gu'], dact, f"f_act_bwd_{i}")
        per_layer['f_w_gate_up'][i] = _matmul(s['h2'], dgu, 'tn', f"f_gate_up_dw_{i}", out_dtype=BF16)
        dh2 = _matmul(dgu, w['f_w_gate_up'][i], 'nt', f"f_gate_up_dx_{i}")
        dx1, per_layer['ffn_norm'][i] = _rmsnorm_bwd(s['x1'], rep['ffn_norm'][i:i + 1], dh2, dx, f"ffn_norm_bwd_{i}")
        if kind == 0:
            per_layer['a_w_o'][j] = _matmul(s['o'], dx1, 'tn', f"a_out_dw_{i}", out_dtype=BF16)
            do = _matmul(dx1, w['a_w_o'][j], 'nt', f"a_out_dx_{i}")
            rider, units = ready()
            (dqkv_r, per_layer['a_sinks'][j]), parts = _swa_bwd(s['qkv_r'], s['o'], s['lse'], do, rep['a_sinks'][j:j + 1],
                                                                f"a_attn_bwd_{i}", rider=rider)
            received.update(zip(units, parts or ()))
            dqkv, per_layer['a_q_norm'][j], per_layer['a_k_norm'][j] = _swa_prep_bwd(
                s['qkv'], dqkv_r, rep['a_q_norm'][j:j + 1], rep['a_k_norm'][j:j + 1], cos_a, sin_a, f"a_prep_bwd_{i}")
            per_layer['a_w_qkv'][j] = _matmul(s['h1'], dqkv, 'tn', f"a_qkv_dw_{i}", out_dtype=BF16)
            dh1 = _matmul(dqkv, w['a_w_qkv'][j], 'nt', f"a_qkv_dx_{i}")
        elif kind == 1:
            per_layer['b_w_out'][j] = _matmul(s['by'], dx1, 'tn', f"b_out_dw_{i}", out_dtype=BF16)
            dby = _matmul(dx1, w['b_w_out'][j], 'nt', f"b_out_dx_{i}")
            dbcu, per_layer['b_conv_w'][j] = _sconv_bwd(s['bcu'], dby, w['b_conv_w'][j], f"b_conv_bwd_{i}")
            per_layer['b_w_in'][j] = _matmul(s['h1'], dbcu, 'tn', f"b_in_dw_{i}", out_dtype=BF16)
            dh1 = _matmul(dbcu, w['b_w_in'][j], 'nt', f"b_in_dx_{i}")
        else:
            per_layer['c_w_o'][j] = _matmul(s['o'], dx1, 'tn', f"c_out_dw_{i}", out_dtype=BF16)
            do = _matmul(dx1, w['c_w_o'][j], 'nt', f"c_out_dx_{i}")
            dlt, do_b = _mla_delta(do, s['o'], f"c_attn_delta_{i}")
            blk = min(MLA_BLOCK, do.shape[0])
            as_rows = lambda col: col.reshape(C_HEADS, do.shape[0] // blk, 1, blk)
            dq, dk, dv = _mla_bwd(s['q'], s['k'], s['v'], do_b, as_rows(s['lse']), as_rows(dlt), f"c_attn_bwd_{i}")
            rider, units = ready()
            (dqw, dkvw, dkrope, per_layer['c_q_norm'][j], per_layer['c_k_norm'][j]), parts = _mla_qk_bwd(
                s['qw'], s['kvw'], s['down'], dq, dk, dv, rep['c_q_norm'][j:j + 1], rep['c_k_norm'][j:j + 1], rows_c,
                f"c_prep_bwd_{i}", rider=rider)
            received.update(zip(units, parts or ()))
            per_layer['c_w_q_up'][j] = _matmul(s['cq'], dqw, 'tn', f"c_q_up_dw_{i}", out_dtype=BF16)
            dcq = _matmul(dqw, w['c_w_q_up'][j], 'nt', f"c_q_up_dx_{i}")
            per_layer['c_w_kv_up'][j] = _matmul(s['ckv'], dkvw, 'tn', f"c_kv_up_dw_{i}", out_dtype=BF16)
            dckv = _matmul(dkvw, w['c_w_kv_up'][j], 'nt', f"c_kv_up_dx_{i}")
            ddown, per_layer['c_q_a_norm'][j], per_layer['c_kv_a_norm'][j] = _mla_latent_bwd(
                s['down'], dcq, dckv, dkrope, w['c_q_a_norm'][j], w['c_kv_a_norm'][j], f"c_latent_bwd_{i}")
            per_layer['c_w_down'][j] = _matmul(s['h1'], ddown, 'tn', f"c_down_dw_{i}", out_dtype=BF16)
            dh1 = _matmul(ddown, w['c_w_down'][j], 'nt', f"c_down_dx_{i}")
        dx, per_layer['mix_norm'][i] = _rmsnorm_bwd(s['x'], rep['mix_norm'][i:i + 1], dh1, dx1, f"mix_norm_bwd_{i}")

    last, units = ready()
    received.update(zip(units, _exchange_now(last, "scatter_last_gradients")))
    parts = {n: jnp.concatenate([received[(n, j)] for j in sorted(per_layer[n])], axis=0) for n in SHARDED}
    small = {}
    for n in REPLICATED:
        stacked = jnp.stack([per_layer[n][j] for j in sorted(per_layer[n])])
        small[n] = stacked.reshape(stacked.shape[0], stacked.shape[-1])
    return loss, dx, parts, small


def kernel(x, positions, mix_norm, ffn_norm, a_w_qkv, a_q_norm, a_k_norm, a_sinks, a_w_o, b_w_in, b_conv_w, b_w_out, c_w_down, c_q_a_norm, c_kv_a_norm, c_w_q_up, c_w_kv_up, c_q_norm, c_k_norm, c_w_o, f_w_gate_up, f_w_down, loss_target, m_mix_norm, m_ffn_norm, m_a_w_qkv, m_a_q_norm, m_a_k_norm, m_a_sinks, m_a_w_o, m_b_w_in, m_b_conv_w, m_b_w_out, m_c_w_down, m_c_q_a_norm, m_c_kv_a_norm, m_c_w_q_up, m_c_w_kv_up, m_c_q_norm, m_c_k_norm, m_c_w_o, m_f_w_gate_up, m_f_w_down, v_mix_norm, v_ffn_norm, v_a_w_qkv, v_a_q_norm, v_a_k_norm, v_a_sinks, v_a_w_o, v_b_w_in, v_b_conv_w, v_b_w_out, v_c_w_down, v_c_q_a_norm, v_c_kv_a_norm, v_c_w_q_up, v_c_w_kv_up, v_c_q_norm, v_c_k_norm, v_c_w_o, v_f_w_gate_up, v_f_w_down):
    local = dict(mix_norm=mix_norm, ffn_norm=ffn_norm, a_w_qkv=a_w_qkv, a_q_norm=a_q_norm, a_k_norm=a_k_norm, a_sinks=a_sinks, a_w_o=a_w_o, b_w_in=b_w_in, b_conv_w=b_conv_w, b_w_out=b_w_out, c_w_down=c_w_down, c_q_a_norm=c_q_a_norm, c_kv_a_norm=c_kv_a_norm, c_w_q_up=c_w_q_up, c_w_kv_up=c_w_kv_up, c_q_norm=c_q_norm, c_k_norm=c_k_norm, c_w_o=c_w_o, f_w_gate_up=f_w_gate_up, f_w_down=f_w_down)
    mom1 = dict(mix_norm=m_mix_norm, ffn_norm=m_ffn_norm, a_w_qkv=m_a_w_qkv, a_q_norm=m_a_q_norm, a_k_norm=m_a_k_norm, a_sinks=m_a_sinks, a_w_o=m_a_w_o, b_w_in=m_b_w_in, b_conv_w=m_b_conv_w, b_w_out=m_b_w_out, c_w_down=m_c_w_down, c_q_a_norm=m_c_q_a_norm, c_kv_a_norm=m_c_kv_a_norm, c_w_q_up=m_c_w_q_up, c_w_kv_up=m_c_w_kv_up, c_q_norm=m_c_q_norm, c_k_norm=m_c_k_norm, c_w_o=m_c_w_o, f_w_gate_up=m_f_w_gate_up, f_w_down=m_f_w_down)
    mom2 = dict(mix_norm=v_mix_norm, ffn_norm=v_ffn_norm, a_w_qkv=v_a_w_qkv, a_q_norm=v_a_q_norm, a_k_norm=v_a_k_norm, a_sinks=v_a_sinks, a_w_o=v_a_w_o, b_w_in=v_b_w_in, b_conv_w=v_b_conv_w, b_w_out=v_b_w_out, c_w_down=v_c_w_down, c_q_a_norm=v_c_q_a_norm, c_kv_a_norm=v_c_kv_a_norm, c_w_q_up=v_c_w_q_up, c_w_kv_up=v_c_w_kv_up, c_q_norm=v_c_q_norm, c_k_norm=v_c_k_norm, c_w_o=v_c_w_o, f_w_gate_up=v_f_w_gate_up, f_w_down=v_f_w_down)
    t, d = x.shape[1], x.shape[2]

    rep = {n: local[n] for n in REPLICATED}
    loss, grad_x, parts, small = _forward_backward(x.reshape(t, d), positions.reshape(t), loss_target.reshape(t, d),
                                                   {n: local[n] for n in SHARDED}, rep)

    out_g, out_d, out_m, out_v = {}, {}, {}, {}

    def update(names, parts):
        for n, part in zip(names, parts):
            shape = local[n].shape if n in SHARD_AXIS else (1,) + local[n].shape
            view = _as_rows(shape)
            results = _adamw(part.reshape(view[0], N_DEV, view[1], view[2]),
                             *[src[n].reshape(view) for src in (local, mom1, mom2)], name="adamw_" + n)
            for dst, res in zip((out_g, out_d, out_m, out_v), results):
                dst[n] = res.reshape(local[n].shape)

    update(SHARDED, [parts[n] for n in SHARDED])
    update(REPLICATED, _exchange_now(_Exchange([small[n].reshape((1,) + small[n].shape) for n in REPLICATED],
                                               scatter=False), "gather_small_gradients"))

    loss = lax.psum(loss.reshape(()), MESH_AXES)
    outs = [loss, grad_x.reshape(1, t, d)]
    for res in (out_g, out_d, out_m, out_v):
        outs += [res[n] for n in WEIGHTS]
    return tuple(outs)
```

```python
import jax
import jax.numpy as jnp
import numpy as np
from jax import lax
from jax.experimental import pallas as pl
from jax.experimental.pallas import tpu as pltpu

F32 = jnp.float32
BF16 = jnp.bfloat16

N_DEV = 8
MESH_AXES = ("x", "y", "c")

DEPTH = 4
N_MIXERS = 3
ROPE_THETA = 500000.0
EPS = 1e-6
A_HEADS, A_KV_HEADS, A_HEAD_DIM, A_ROT_DIM, A_WINDOW = 16, 4, 64, 16, 128
A_GROUP = A_HEADS // A_KV_HEADS
C_HEADS, C_NOPE, C_ROPE, C_V, C_Q_RANK, C_KV_RANK = 16, 64, 32, 64, 384, 256
C_QK = C_NOPE + C_ROPE
ADAM_LR, ADAM_B1, ADAM_B2, ADAM_EPS, ADAM_WD, ADAM_STEP = 0.001, 0.9, 0.999, 1e-08, 0.01, 10

VMEM_LIMIT_BYTES = 48 * 1024 * 1024
LANES = 128
NEG = -1e30
MLA_BLOCK = 512
MLA_FWD_Q_BLOCK = 512
MLA_FWD_K_BLOCK = 2048

WEIGHTS = ['mix_norm', 'ffn_norm', 'a_w_qkv', 'a_q_norm', 'a_k_norm', 'a_sinks', 'a_w_o', 'b_w_in', 'b_conv_w', 'b_w_out',
           'c_w_down', 'c_q_a_norm', 'c_kv_a_norm', 'c_w_q_up', 'c_w_kv_up', 'c_q_norm', 'c_k_norm', 'c_w_o', 'f_w_gate_up',
           'f_w_down']
SHARD_AXIS = {'a_w_qkv': 2, 'a_w_o': 1, 'b_w_in': 2, 'b_conv_w': 2, 'b_w_out': 1, 'c_w_down': 1, 'c_q_a_norm': 1,
              'c_kv_a_norm': 1, 'c_w_q_up': 2, 'c_w_kv_up': 2, 'c_w_o': 1, 'f_w_gate_up': 2, 'f_w_down': 1}
SHARDED = [n for n in WEIGHTS if n in SHARD_AXIS]
REPLICATED = [n for n in WEIGHTS if n not in SHARD_AXIS]
GATHER_F32 = ['b_conv_w', 'c_q_a_norm', 'c_kv_a_norm']
GATHER_BF16 = [n for n in SHARDED if n not in GATHER_F32]

def _params(semantics=None):
    return pltpu.CompilerParams(dimension_semantics=semantics, vmem_limit_bytes=VMEM_LIMIT_BYTES)


def _div_tile(n, cap, mult=LANES):
    best = None
    t = mult
    while t <= min(n, cap):
        if n % t == 0:
            best = t
        t += mult
    return n if best is None else best


ANY_SPEC = pl.BlockSpec(memory_space=pl.ANY)


class _Exchange:
    def __init__(self, arrays, scatter):
        self.arrays, self.scatter = list(arrays), scatter
        n = len(self.arrays)
        self.out_shapes = [jax.ShapeDtypeStruct(a.shape if scatter else (a.shape[0], N_DEV) + tuple(a.shape[1:]), a.dtype)
                           for a in self.arrays]
        self.scratch = [pltpu.SemaphoreType.DMA((n, N_DEV - 1)), pltpu.SemaphoreType.DMA((n, N_DEV - 1)),
                        pltpu.SemaphoreType.DMA((n,))]

    def _copies(self, src_refs, out_refs, sems):
        send_sems, recv_sems, local_sems = sems
        x, y, c = lax.axis_index("x"), lax.axis_index("y"), lax.axis_index("c")
        me_idx = 4 * x + 2 * y + c
        n = len(self.arrays)

        def remote(a, k, src, dst, to):
            return pltpu.make_async_remote_copy(src_ref=src, dst_ref=dst, send_sem=send_sems.at[a, k],
                                                recv_sem=recv_sems.at[a, k], device_id=to,
                                                device_id_type=pl.DeviceIdType.MESH)

        local, first, forwards, last = [], [], [], []
        if self.scatter:
            for a in range(n):
                local.append(pltpu.make_async_copy(src_refs[a].at[:, me_idx], out_refs[a].at[:, me_idx], local_sems.at[a]))
                for r in range(1, N_DEV):
                    px = 1 - x if (r >> 2) & 1 else x
                    py = 1 - y if (r >> 1) & 1 else y
                    pc = 1 - c if r & 1 else c
                    cp = remote(a, r - 1, src_refs[a].at[:, 4 * px + 2 * py + pc], out_refs[a].at[:, me_idx], (px, py, pc))
                    first.append(cp)
                    last.append(cp)
            return local, first, forwards, last
        me, sibling = (x, y, c), (x, y, 1 - c)
        chips = [(1 - x, y), (x, 1 - y), (1 - x, 1 - y)]

        def place(a, block):
            return out_refs[a].at[:, 4 * block[0] + 2 * block[1] + block[2]]

        for a in range(n):
            local.append(pltpu.make_async_copy(src_refs[a], place(a, me), local_sems.at[a]))
            first.append(remote(a, 0, src_refs[a], place(a, me), sibling))
            last.append(remote(a, 0, place(a, sibling), place(a, sibling), me))
            for j, chip in enumerate(chips):
                first.append(remote(a, 1 + j, src_refs[a], place(a, me), (*chip, c)))
                forwards.append((remote(a, 1 + j, place(a, (*chip, c)), place(a, (*chip, c)), me),
                                 remote(a, 4 + j, place(a, (*chip, c)), place(a, (*chip, c)), sibling)))
                last.append(remote(a, 4 + j, place(a, (*chip, 1 - c)), place(a, (*chip, 1 - c)), me))
        return local, first, forwards, last

    def start(self, src_refs, out_refs, sems):
        local, first, _, _ = self._copies(src_refs, out_refs, sems)
        for cp in local + first:
            cp.start()

    def finish(self, src_refs, out_refs, sems):
        local, first, forwards, last = self._copies(src_refs, out_refs, sems)
        for arrival, forward in forwards:
            arrival.wait_recv()
            forward.start()
        for cp in last:
            cp.wait_recv()
        for cp in first + [forward for _, forward in forwards]:
            cp.wait_send()
        for cp in local:
            cp.wait()


def _exchange_now(exchange, name):
    n = len(exchange.arrays)

    def body(*refs):
        exchange.start(refs[:n], refs[n:2 * n], refs[2 * n:])
        exchange.finish(refs[:n], refs[n:2 * n], refs[2 * n:])

    return pl.pallas_call(
        body, name=name, out_shape=tuple(exchange.out_shapes), in_specs=[ANY_SPEC] * n, out_specs=(ANY_SPEC,) * n,
        scratch_shapes=exchange.scratch,
    )(*exchange.arrays)


def _host_call(body, rider, name, out_shape, grid, in_specs, out_specs, operands, semantics):
    if rider is None:
        return pl.pallas_call(body, name=name, out_shape=tuple(out_shape), grid=grid, in_specs=list(in_specs),
                              out_specs=tuple(out_specs), compiler_params=_params(semantics))(*operands), None
    n_in, n_out, r = len(in_specs), len(out_shape), len(rider.arrays)

    def riding(*refs):
        ins, rider_in = refs[:n_in], refs[n_in:n_in + r]
        outs, rider_out = refs[n_in + r:n_in + r + n_out], refs[n_in + r + n_out:n_in + 2 * r + n_out]
        sems = refs[n_in + 2 * r + n_out:]
        step = pl.program_id(0)

        @pl.when(step == 0)
        def _():
            rider.start(rider_in, rider_out, sems)

        body(*ins, *outs)

        @pl.when(step == grid[0] - 1)
        def _():
            rider.finish(rider_in, rider_out, sems)

    results = pl.pallas_call(
        riding, name=name, out_shape=tuple(out_shape) + tuple(rider.out_shapes), grid=grid,
        in_specs=list(in_specs) + [ANY_SPEC] * r, out_specs=tuple(out_specs) + (ANY_SPEC,) * r,
        scratch_shapes=rider.scratch, compiler_params=_params(("arbitrary",)),
    )(*operands, *rider.arrays)
    return results[:n_out], results[n_out:]


def _matmul(a, b, mode, name, out_dtype=F32, residual=None):
    if mode == 'nn':
        (m, k), (k2, n) = a.shape, b.shape
    elif mode == 'nt':
        (m, k), (n, k2) = a.shape, b.shape
    else:
        (k, m), (k2, n) = a.shape, b.shape
    assert k == k2, (name, a.shape, b.shape, mode)
    if mode == 'tn':
        tm, tk = _div_tile(m, 1408), _div_tile(k, 1024, 16)
    else:
        tm, tk = _div_tile(m, 1024, 16), _div_tile(k, 1536)
    tn = _div_tile(n, 1408)
    nk = k // tk
    dims = {'nn': (((1,), (0,)), ((), ())), 'nt': (((1,), (1,)), ((), ())), 'tn': (((0,), (0,)), ((), ()))}[mode]

    def product(a_ref, b_ref):
        return lax.dot_general(a_ref[...].astype(BF16), b_ref[...].astype(BF16), dims, preferred_element_type=F32)

    def finish(r, rest):
        if residual is not None:
            r = r + rest[0][...]
        rest[-1 if nk == 1 else -2][...] = r.astype(out_dtype)

    def body_single(a_ref, b_ref, *rest):
        finish(product(a_ref, b_ref), rest)

    def body_accumulate(a_ref, b_ref, *rest):
        acc = rest[-1]
        kk = pl.program_id(2)

        @pl.when(kk == 0)
        def _():
            acc[...] = jnp.zeros_like(acc)

        acc[...] += product(a_ref, b_ref)

        @pl.when(kk == nk - 1)
        def _():
            finish(acc[...], rest)

    a_spec = pl.BlockSpec((tk, tm), lambda i, j, kk: (kk, i)) if mode == 'tn' else pl.BlockSpec((tm, tk), lambda i, j, kk: (i, kk))
    b_spec = pl.BlockSpec((tn, tk), lambda i, j, kk: (j, kk)) if mode == 'nt' else pl.BlockSpec((tk, tn), lambda i, j, kk: (kk, j))
    o_spec = pl.BlockSpec((tm, tn), lambda i, j, kk: (i, j))
    in_specs, operands = [a_spec, b_spec], [a, b]
    if residual is not None:
        in_specs.append(o_spec)
        operands.append(residual)
    return pl.pallas_call(
        body_single if nk == 1 else body_accumulate, name=name, out_shape=jax.ShapeDtypeStruct((m, n), out_dtype),
        grid=(m // tm, n // tn, nk), in_specs=in_specs, out_specs=o_spec,
        scratch_shapes=[] if nk == 1 else [pltpu.VMEM((tm, tn), F32)],
        compiler_params=_params(("parallel", "parallel", "arbitrary")),
    )(*operands)


def _row_spec(tm, cols):
    return pl.BlockSpec((tm, cols), lambda i: (i, 0))


def _const_spec(shape):
    return pl.BlockSpec(shape, lambda i: tuple(0 for _ in shape))


def _accumulate(ref, value, step):
    @pl.when(step == 0)
    def _():
        ref[...] = value

    @pl.when(step > 0)
    def _():
        ref[...] += value


def _rstd(x):
    return lax.rsqrt(jnp.mean(x * x, axis=-1, keepdims=True) + EPS)


def _norm_bwd(x, g, dout):
    xn = x * _rstd(x)
    dg = jnp.sum(dout * xn, axis=0, keepdims=True)
    dxn = dout * g
    dx = _rstd(x) * (dxn - xn * jnp.mean(dxn * xn, axis=-1, keepdims=True))
    return dx, dg


def _rmsnorm_fwd(x, g, name):
    t, d = x.shape
    tm = _div_tile(t, 512, 16)

    def body(x_ref, g_ref, o_ref):
        xv = x_ref[...]
        o_ref[...] = (xv * _rstd(xv) * g_ref[...]).astype(BF16)

    return pl.pallas_call(
        body, name=name, out_shape=jax.ShapeDtypeStruct((t, d), BF16), grid=(t // tm,),
        in_specs=[_row_spec(tm, d), _const_spec((1, d))], out_specs=_row_spec(tm, d),
        compiler_params=_params(("parallel",)),
    )(x, g)


def _rmsnorm_bwd(x, g, dh, dres, name):
    t, d = x.shape
    tm = _div_tile(t, 512, 8)

    def body(x_ref, g_ref, dh_ref, dres_ref, dx_ref, dg_ref):
        dx, dg = _norm_bwd(x_ref[...], g_ref[...], dh_ref[...])
        dx_ref[...] = dres_ref[...] + dx
        _accumulate(dg_ref, dg, pl.program_id(0))

    return pl.pallas_call(
        body, name=name,
        out_shape=(jax.ShapeDtypeStruct((t, d), F32), jax.ShapeDtypeStruct((1, d), F32)), grid=(t // tm,),
        in_specs=[_row_spec(tm, d), _const_spec((1, d)), _row_spec(tm, d), _row_spec(tm, d)],
        out_specs=(_row_spec(tm, d), _const_spec((1, d))),
        compiler_params=_params(("arbitrary",)),
    )(x, g, dh, dres)


def _sigmoid(x):
    return 0.5 * jnp.tanh(0.5 * x) + 0.5


def _swiglu_fwd(gu, name):
    t, f2 = gu.shape
    f = f2 // 2
    tm = _div_tile(t, 512, 16)

    def body(gu_ref, o_ref):
        gate, up = gu_ref[:, :f].astype(F32), gu_ref[:, f:].astype(F32)
        o_ref[...] = (gate * _sigmoid(gate) * up).astype(BF16)

    return pl.pallas_call(
        body, name=name, out_shape=jax.ShapeDtypeStruct((t, f), BF16), grid=(t // tm,),
        in_specs=[_row_spec(tm, f2)], out_specs=_row_spec(tm, f),
        compiler_params=_params(("parallel",)),
    )(gu)


def _swiglu_bwd(gu, da, name):
    t, f2 = gu.shape
    f = f2 // 2
    tm = _div_tile(t, 512, 16)

    def body(gu_ref, da_ref, o_ref):
        gate, up, dav = gu_ref[:, :f].astype(F32), gu_ref[:, f:].astype(F32), da_ref[...].astype(F32)
        sig = _sigmoid(gate)
        o_ref[:, :f] = (dav * up * (sig * (1.0 + gate * (1.0 - sig)))).astype(BF16)
        o_ref[:, f:] = (dav * (gate * sig)).astype(BF16)

    return pl.pallas_call(
        body, name=name, out_shape=jax.ShapeDtypeStruct((t, f2), BF16), grid=(t // tm,),
        in_specs=[_row_spec(tm, f2), _row_spec(tm, f)], out_specs=_row_spec(tm, f2),
        compiler_params=_params(("parallel",)),
    )(gu, da)


def _loss_head(y, target, name):
    t, d = y.shape
    tm = _div_tile(t, 512, 8)

    def body(y_ref, t_ref, loss_ref, dy_ref):
        diff = y_ref[...] - t_ref[...]
        dy_ref[...] = diff * (1.0 / d)
        part = jnp.sum(jnp.sum(diff * diff, axis=1, keepdims=True), axis=0, keepdims=True) * (0.5 / d)
        _accumulate(loss_ref, part, pl.program_id(0))

    return pl.pallas_call(
        body, name=name,
        out_shape=(jax.ShapeDtypeStruct((1, 1), F32), jax.ShapeDtypeStruct((t, d), F32)), grid=(t // tm,),
        in_specs=[_row_spec(tm, d), _row_spec(tm, d)], out_specs=(_const_spec((1, 1)), _row_spec(tm, d)),
        compiler_params=_params(("arbitrary",)),
    )(y, target)


HALO = 8


def _shift_down(z, k, halo_rows):
    tm = z.shape[0]
    row = lax.broadcasted_iota(jnp.int32, z.shape, 0)
    out = pltpu.roll(z, k, 0)
    for j in range(k):
        out = jnp.where(row == j, halo_rows[HALO - k + j:HALO - k + j + 1, :], out)
    return out


def _shift_up(z, k, halo_rows):
    tm = z.shape[0]
    row = lax.broadcasted_iota(jnp.int32, z.shape, 0)
    out = pltpu.roll(z, tm - k, 0)
    for j in range(k):
        out = jnp.where(row == tm - k + j, halo_rows[j:j + 1, :], out)
    return out


def _sconv_specs(t, tm, cols):
    per = tm // HALO
    last = t // HALO - 1
    cur = pl.BlockSpec((tm, cols), lambda i: (i, 0))
    prev = pl.BlockSpec((HALO, cols), lambda i: (jnp.maximum(i * per - 1, 0), 0))
    nxt = pl.BlockSpec((HALO, cols), lambda i: (jnp.minimum((i + 1) * per, last), 0))
    return cur, prev, nxt


def _sconv_fwd(bcu, conv_w, name):
    t, d3 = bcu.shape
    d = d3 // 3
    tm = _div_tile(t, 256, 16)
    cur, prev, _ = _sconv_specs(t, tm, d3)

    def body(cur_ref, prev_ref, w_ref, o_ref):
        i = pl.program_id(0)
        z = cur_ref[:, d:2 * d] * cur_ref[:, 2 * d:]
        zp = prev_ref[:, d:2 * d] * prev_ref[:, 2 * d:] * (i > 0).astype(F32)
        y = w_ref[0:1, :] * _shift_down(z, 2, zp) + w_ref[1:2, :] * _shift_down(z, 1, zp) + w_ref[2:3, :] * z
        o_ref[...] = (cur_ref[:, :d] * y).astype(BF16)

    return pl.pallas_call(
        body, name=name, out_shape=jax.ShapeDtypeStruct((t, d), BF16), grid=(t // tm,),
        in_specs=[cur, prev, _const_spec((3, d))], out_specs=_row_spec(tm, d),
        compiler_params=_params(("parallel",)),
    )(bcu, bcu, conv_w)


def _sconv_bwd(bcu, dout, conv_w, name):
    t, d3 = bcu.shape
    d = d3 // 3
    tm = _div_tile(t, 256, 16)
    cur, prev, nxt = _sconv_specs(t, tm, d3)
    dcur, _, dnxt = _sconv_specs(t, tm, d)
    n_tiles = t // tm

    def body(cur_ref, prev_ref, nxt_ref, do_ref, don_ref, w_ref, o_ref, dw_ref):
        i = pl.program_id(0)
        b, cg, u = cur_ref[:, :d], cur_ref[:, d:2 * d], cur_ref[:, 2 * d:]
        z = cg * u
        zp = prev_ref[:, d:2 * d] * prev_ref[:, 2 * d:] * (i > 0).astype(F32)
        z1, z2 = _shift_down(z, 1, zp), _shift_down(z, 2, zp)
        w0, w1, w2 = w_ref[0:1, :], w_ref[1:2, :], w_ref[2:3, :]
        y = w0 * z2 + w1 * z1 + w2 * z
        dov = do_ref[...]
        dy = dov * b
        dyn = don_ref[...] * nxt_ref[:, :d] * (i < n_tiles - 1).astype(F32)
        dz = w2 * dy + w1 * _shift_up(dy, 1, dyn) + w0 * _shift_up(dy, 2, dyn)
        o_ref[:, :d] = (dov * y).astype(BF16)
        o_ref[:, d:2 * d] = (dz * u).astype(BF16)
        o_ref[:, 2 * d:] = (dz * cg).astype(BF16)
        dw = jnp.concatenate([jnp.sum(dy * z2, axis=0, keepdims=True), jnp.sum(dy * z1, axis=0, keepdims=True),
                              jnp.sum(dy * z, axis=0, keepdims=True)], axis=0)
        _accumulate(dw_ref, dw, i)

    return pl.pallas_call(
        body, name=name,
        out_shape=(jax.ShapeDtypeStruct((t, d3), BF16), jax.ShapeDtypeStruct((3, d), F32)), grid=(n_tiles,),
        in_specs=[cur, prev, nxt, dcur, dnxt, _const_spec((3, d))],
        out_specs=(_row_spec(tm, d3), _const_spec((3, d))),
        compiler_params=_params(("arbitrary",)),
    )(bcu, bcu, bcu, dout, dout, conv_w)


def _rope_tables(positions, rot, lead, trail):
    inv_freq = ROPE_THETA ** (-jnp.arange(0, rot, 2, dtype=F32) / rot)
    ang = positions.astype(F32)[:, None] * inv_freq
    cos, sin = jnp.cos(ang), jnp.sin(ang)
    t = positions.shape[0]
    cos_full = jnp.concatenate([jnp.ones((t, lead), F32), cos, cos, jnp.ones((t, trail), F32)], axis=1)
    sin_full = jnp.concatenate([jnp.zeros((t, lead), F32), -sin, sin, jnp.zeros((t, trail), F32)], axis=1)
    return cos_full, sin_full


def _repeat_lanes(x, reps, name):
    t, d = x.shape
    tm = _div_tile(t, 512, 8)

    def body(x_ref, o_ref):
        o_ref[...] = jnp.concatenate([x_ref[...]] * reps, axis=1)

    return pl.pallas_call(
        body, name=name, out_shape=jax.ShapeDtypeStruct((t, reps * d), x.dtype), grid=(t // tm,),
        in_specs=[_row_spec(tm, d)], out_specs=_row_spec(tm, reps * d), compiler_params=_params(("parallel",)),
    )(x)


def _pieces_dot(a, b, pieces):
    total, rest = None, a
    for _ in range(pieces):
        piece = rest.astype(BF16)
        term = jnp.dot(piece, b, preferred_element_type=F32)
        total = term if total is None else total + term
        rest = rest - piece.astype(F32)
    return total


A_Q_COLS = A_HEADS * A_HEAD_DIM
A_KV_COLS = A_KV_HEADS * A_HEAD_DIM
A_COLS = A_Q_COLS + 2 * A_KV_COLS
A_SCALE = A_HEAD_DIM ** -0.5


A_NORMED = A_Q_COLS + A_KV_COLS


def _swa_prep_tables(q_norm, k_norm):
    lane = np.arange(A_NORMED)
    seg = (lane[:, None] // A_HEAD_DIM == np.arange(LANES)[None, :]).astype(np.float32)
    fold = (np.where(lane < A_Q_COLS, 0, A_HEAD_DIM)[:, None] + lane[:, None] % A_HEAD_DIM
            == np.arange(LANES)[None, :]).astype(np.float32)
    gains = jnp.concatenate([jnp.tile(q_norm, (1, A_HEADS)), jnp.tile(k_norm, (1, A_KV_HEADS))], axis=1)
    return gains, jnp.asarray(seg, BF16), jnp.asarray(seg.T, BF16), jnp.asarray(fold, BF16)


def _wide_rstd(x, seg, seg_t):
    mean_sq = _pieces_dot(x * x, seg, 1) * (1.0 / A_HEAD_DIM)
    return _pieces_dot(lax.rsqrt(mean_sq + EPS), seg_t, 2)


def _wide_partner(n):
    dim = lax.broadcasted_iota(jnp.int32, n.shape, 1) & (A_HEAD_DIM - 1)
    half = A_ROT_DIM // 2
    return jnp.where(dim < half, pltpu.roll(n, A_NORMED - half, 1),
                     jnp.where(dim < A_ROT_DIM, pltpu.roll(n, half, 1), 0.0))


def _swa_prep_fwd(qkv, q_norm, k_norm, cos_w, sin_w, name):
    t = qkv.shape[0]
    tm = _div_tile(t, 256, 16)
    gains, seg, seg_t, _ = _swa_prep_tables(q_norm, k_norm)

    def body(x_ref, g_ref, cos_ref, sin_ref, seg_ref, segt_ref, o_ref):
        x = x_ref[:, :A_NORMED]
        n = x * _wide_rstd(x, seg_ref[...], segt_ref[...]) * g_ref[...]
        o_ref[:, :A_NORMED] = (n * cos_ref[...] + _wide_partner(n) * sin_ref[...]).astype(BF16)
        o_ref[:, A_NORMED:] = x_ref[:, A_NORMED:].astype(BF16)

    return pl.pallas_call(
        body, name=name, out_shape=jax.ShapeDtypeStruct((t, A_COLS), BF16), grid=(t // tm,),
        in_specs=[_row_spec(tm, A_COLS), _const_spec((1, A_NORMED)), _row_spec(tm, A_NORMED), _row_spec(tm, A_NORMED),
                  _const_spec(seg.shape), _const_spec(seg_t.shape)],
        out_specs=_row_spec(tm, A_COLS), compiler_params=_params(("parallel",)),
    )(qkv, gains, cos_w, sin_w, seg, seg_t)


def _swa_prep_bwd(qkv, dqkv_r, q_norm, k_norm, cos_w, sin_w, name):
    t = qkv.shape[0]
    tm = _div_tile(t, 256, 16)
    hd = A_HEAD_DIM
    gains, seg, seg_t, fold = _swa_prep_tables(q_norm, k_norm)

    def body(x_ref, d_ref, g_ref, cos_ref, sin_ref, seg_ref, segt_ref, fold_ref, o_ref, dgq_ref, dgk_ref):
        x, dout = x_ref[:, :A_NORMED], d_ref[:, :A_NORMED]
        rstd = _wide_rstd(x, seg_ref[...], segt_ref[...])
        xn = x * rstd
        dn = dout * cos_ref[...] + _wide_partner(dout * sin_ref[...])
        dg = _pieces_dot(jnp.sum(dn * xn, axis=0, keepdims=True), fold_ref[...], 3)
        dxn = dn * g_ref[...]
        mean = _pieces_dot(_pieces_dot(dxn * xn, seg_ref[...], 1) * (1.0 / hd), segt_ref[...], 2)
        o_ref[:, :A_NORMED] = (rstd * (dxn - xn * mean)).astype(BF16)
        o_ref[:, A_NORMED:] = d_ref[:, A_NORMED:].astype(BF16)
        _accumulate(dgq_ref, dg[:, :hd], pl.program_id(0))
        _accumulate(dgk_ref, dg[:, hd:2 * hd], pl.program_id(0))

    return pl.pallas_call(
        body, name=name,
        out_shape=(jax.ShapeDtypeStruct((t, A_COLS), BF16), jax.ShapeDtypeStruct((1, hd), F32),
                   jax.ShapeDtypeStruct((1, hd), F32)),
        grid=(t // tm,),
        in_specs=[_row_spec(tm, A_COLS), _row_spec(tm, A_COLS), _const_spec((1, A_NORMED)), _row_spec(tm, A_NORMED),
                  _row_spec(tm, A_NORMED), _const_spec(seg.shape), _const_spec(seg_t.shape), _const_spec(fold.shape)],
        out_specs=(_row_spec(tm, A_COLS), _const_spec((1, hd)), _const_spec((1, hd))),
        compiler_params=_params(("arbitrary",)),
    )(qkv, dqkv_r, gains, cos_w, sin_w, seg, seg_t, fold)


def _group_rows(ref, k, width=A_HEAD_DIM, base=0):
    return jnp.concatenate([ref[:, base + (A_GROUP * k + g) * width:base + (A_GROUP * k + g + 1) * width]
                            for g in range(A_GROUP)], axis=0)


def _group_column(ref, k, rows):
    cols = []
    for g in range(A_GROUP):
        h = A_GROUP * k + g
        col = ref[:, h:h + 1]
        cols.append(jnp.broadcast_to(col, (rows, 1)) if col.shape[0] == 1 else col)
    return jnp.concatenate(cols, axis=0)


def _swa_fwd(qkv_r, sinks, name, rider=None):
    t = qkv_r.shape[0]
    blk = A_WINDOW
    nb = t // blk
    hd = A_HEAD_DIM
    kv_block = A_Q_COLS // (2 * A_KV_COLS)

    def body(q_ref, kvc_ref, kvp_ref, s_ref, o_ref, lse_ref):
        n = pl.program_id(0)
        shape = (A_GROUP * blk, 2 * blk)
        qpos = lax.broadcasted_iota(jnp.int32, shape, 0) & (blk - 1)
        col = lax.broadcasted_iota(jnp.int32, shape, 1)
        delta = qpos + blk - col
        valid = (delta >= 0) & (delta < A_WINDOW) & ((col >= blk) | (n > 0))
        for k in range(A_KV_HEADS):
            qg = _group_rows(q_ref, k)
            kw = jnp.concatenate([kvp_ref[:, k * hd:(k + 1) * hd], kvc_ref[:, k * hd:(k + 1) * hd]], axis=0)
            vw = jnp.concatenate([kvp_ref[:, A_KV_COLS + k * hd:A_KV_COLS + (k + 1) * hd],
                                  kvc_ref[:, A_KV_COLS + k * hd:A_KV_COLS + (k + 1) * hd]], axis=0)
            s = lax.dot_general(qg, kw, (((1,), (1,)), ((), ())), preferred_element_type=F32) * A_SCALE
            s = jnp.where(valid, s, NEG)
            sink = _group_column(s_ref, k, blk)
            m = jnp.maximum(jnp.max(s, axis=-1, keepdims=True), sink)
            p = jnp.exp(s - m)
            denom = jnp.sum(p, axis=-1, keepdims=True) + jnp.exp(sink - m)
            o = jnp.dot(p.astype(BF16), vw, preferred_element_type=F32) / denom
            lse = m + jnp.log(denom)
            for g in range(A_GROUP):
                h = A_GROUP * k + g
                o_ref[:, h * hd:(h + 1) * hd] = o[g * blk:(g + 1) * blk].astype(BF16)
                lse_ref[:, h:h + 1] = lse[g * blk:(g + 1) * blk]

    return _host_call(
        body, rider, name,
        out_shape=(jax.ShapeDtypeStruct((t, A_Q_COLS), BF16), jax.ShapeDtypeStruct((t, A_HEADS), F32)), grid=(nb,),
        in_specs=[pl.BlockSpec((blk, A_Q_COLS), lambda n: (n, 0)),
                  pl.BlockSpec((blk, 2 * A_KV_COLS), lambda n: (n, kv_block)),
                  pl.BlockSpec((blk, 2 * A_KV_COLS), lambda n: (jnp.maximum(n - 1, 0), kv_block)),
                  _const_spec((1, A_HEADS))],
        out_specs=(pl.BlockSpec((blk, A_Q_COLS), lambda n: (n, 0)), pl.BlockSpec((blk, A_HEADS), lambda n: (n, 0))),
        operands=(qkv_r, qkv_r, qkv_r, sinks), semantics=("parallel",))


def _swa_bwd(qkv_r, o, lse, do, sinks, name, rider=None):
    t = qkv_r.shape[0]
    blk = A_WINDOW
    nb = t // blk
    hd = A_HEAD_DIM
    kv_block = A_Q_COLS // (2 * A_KV_COLS)
    rows = A_GROUP * blk

    def nxt(n):
        return jnp.minimum(n + 1, nb - 1)

    def body(qc_ref, qn_ref, kvc_ref, kvp_ref, doc_ref, don_ref, oc_ref, on_ref, lc_ref, ln_ref, s_ref, dx_ref, ds_ref):
        n = pl.program_id(0)
        shape = (2 * rows, 2 * blk)
        row = lax.broadcasted_iota(jnp.int32, shape, 0)
        col = lax.broadcasted_iota(jnp.int32, shape, 1)
        is_next = row >= rows
        delta = jnp.where(is_next, blk, 0) + blk + (row & (blk - 1)) - col
        valid = ((delta >= 0) & (delta < A_WINDOW) & ((col >= blk) | (n > 0)) & (jnp.logical_not(is_next) | (n < nb - 1)))
        dsink_cols = []
        for k in range(A_KV_HEADS):
            qs = jnp.concatenate([_group_rows(qc_ref, k), _group_rows(qn_ref, k)], axis=0)
            dos = jnp.concatenate([_group_rows(doc_ref, k), _group_rows(don_ref, k)], axis=0)
            os_ = jnp.concatenate([_group_rows(oc_ref, k), _group_rows(on_ref, k)], axis=0).astype(F32)
            lses = jnp.concatenate([_group_column(lc_ref, k, blk), _group_column(ln_ref, k, blk)], axis=0)
            kw = jnp.concatenate([kvp_ref[:, k * hd:(k + 1) * hd], kvc_ref[:, k * hd:(k + 1) * hd]], axis=0)
            vw = jnp.concatenate([kvp_ref[:, A_KV_COLS + k * hd:A_KV_COLS + (k + 1) * hd],
                                  kvc_ref[:, A_KV_COLS + k * hd:A_KV_COLS + (k + 1) * hd]], axis=0)
            s = lax.dot_general(qs, kw, (((1,), (1,)), ((), ())), preferred_element_type=F32) * A_SCALE
            p = jnp.exp(jnp.where(valid, s - lses, NEG))
            dos_b = dos.astype(BF16)
            dp = lax.dot_general(dos_b, vw, (((1,), (1,)), ((), ())), preferred_element_type=F32)
            dlt = jnp.sum(dos * os_, axis=-1, keepdims=True)
            ds = p * (dp - dlt)
            dq = jnp.dot(ds[:rows].astype(BF16), kw, preferred_element_type=F32) * A_SCALE
            dk = lax.dot_general(ds[:, blk:].astype(BF16), qs, (((0,), (0,)), ((), ())), preferred_element_type=F32) * A_SCALE
            dv = lax.dot_general(p[:, blk:].astype(BF16), dos_b, (((0,), (0,)), ((), ())), preferred_element_type=F32)
            for g in range(A_GROUP):
                h = A_GROUP * k + g
                dx_ref[:, h * hd:(h + 1) * hd] = dq[g * blk:(g + 1) * blk]
            dx_ref[:, A_Q_COLS + k * hd:A_Q_COLS + (k + 1) * hd] = dk
            dx_ref[:, A_Q_COLS + A_KV_COLS + k * hd:A_Q_COLS + A_KV_COLS + (k + 1) * hd] = dv
            sink = _group_column(s_ref, k, blk)
            contrib = -jnp.exp(sink - lses[:rows]) * dlt[:rows]
            for g in range(A_GROUP):
                dsink_cols.append(jnp.sum(contrib[g * blk:(g + 1) * blk], axis=0, keepdims=True))
        _accumulate(ds_ref, jnp.concatenate(dsink_cols, axis=1), n)

    q_spec = lambda f: pl.BlockSpec((blk, A_Q_COLS), lambda n: (f(n), 0))
    l_spec = lambda f: pl.BlockSpec((blk, A_HEADS), lambda n: (f(n), 0))
    same = lambda n: n
    return _host_call(
        body, rider, name,
        out_shape=(jax.ShapeDtypeStruct((t, A_COLS), F32), jax.ShapeDtypeStruct((1, A_HEADS), F32)), grid=(nb,),
        in_specs=[q_spec(same), q_spec(nxt),
                  pl.BlockSpec((blk, 2 * A_KV_COLS), lambda n: (n, kv_block)),
                  pl.BlockSpec((blk, 2 * A_KV_COLS), lambda n: (jnp.maximum(n - 1, 0), kv_block)),
                  q_spec(same), q_spec(nxt), q_spec(same), q_spec(nxt), l_spec(same), l_spec(nxt),
                  _const_spec((1, A_HEADS))],
        out_specs=(pl.BlockSpec((blk, A_COLS), lambda n: (n, 0)), _const_spec((1, A_HEADS))),
        operands=(qkv_r, qkv_r, qkv_r, qkv_r, do, do, o, o, lse, lse, sinks), semantics=("arbitrary",))


C_DOWN_COLS = C_Q_RANK + C_KV_RANK + C_ROPE
C_Q_COLS = C_HEADS * C_QK
C_KV_COLS = C_HEADS * (C_NOPE + C_V)
C_O_COLS = C_HEADS * C_V
C_PAD = LANES
C_SCALE = C_QK ** -0.5
LOG2E = 1.4426950408889634
LN2 = 0.6931471805599453
C_Q_SCALE = C_SCALE * LOG2E
C_PAIR = 2


def _mla_latent_fwd(down, q_a_norm, kv_a_norm, name):
    t = down.shape[0]
    tm = _div_tile(t, 512, 16)

    def body(x_ref, gq_ref, gk_ref, cq_ref, ckv_ref):
        cq, ckv = x_ref[:, :C_Q_RANK], x_ref[:, C_Q_RANK:C_Q_RANK + C_KV_RANK]
        cq_ref[...] = (cq * _rstd(cq) * gq_ref[...]).astype(BF16)
        ckv_ref[...] = (ckv * _rstd(ckv) * gk_ref[...]).astype(BF16)

    return pl.pallas_call(
        body, name=name,
        out_shape=(jax.ShapeDtypeStruct((t, C_Q_RANK), BF16), jax.ShapeDtypeStruct((t, C_KV_RANK), BF16)), grid=(t // tm,),
        in_specs=[_row_spec(tm, C_DOWN_COLS), _const_spec((1, C_Q_RANK)), _const_spec((1, C_KV_RANK))],
        out_specs=(_row_spec(tm, C_Q_RANK), _row_spec(tm, C_KV_RANK)), compiler_params=_params(("parallel",)),
    )(down, q_a_norm, kv_a_norm)


def _mla_latent_bwd(down, dcq, dckv, dkrope, q_a_norm, kv_a_norm, name):
    t = down.shape[0]
    tm = _div_tile(t, 512, 16)

    def body(x_ref, dcq_ref, dckv_ref, dkr_ref, gq_ref, gk_ref, o_ref, dgq_ref, dgk_ref):
        dq, dgq = _norm_bwd(x_ref[:, :C_Q_RANK], gq_ref[...], dcq_ref[...])
        dkv, dgk = _norm_bwd(x_ref[:, C_Q_RANK:C_Q_RANK + C_KV_RANK], gk_ref[...], dckv_ref[...])
        o_ref[...] = jnp.concatenate([dq, dkv, dkr_ref[...]], axis=1).astype(BF16)
        _accumulate(dgq_ref, dgq, pl.program_id(0))
        _accumulate(dgk_ref, dgk, pl.program_id(0))

    return pl.pallas_call(
        body, name=name,
        out_shape=(jax.ShapeDtypeStruct((t, C_DOWN_COLS), BF16), jax.ShapeDtypeStruct((1, C_Q_RANK), F32),
                   jax.ShapeDtypeStruct((1, C_KV_RANK), F32)),
        grid=(t // tm,),
        in_specs=[_row_spec(tm, C_DOWN_COLS), _row_spec(tm, C_Q_RANK), _row_spec(tm, C_KV_RANK), _row_spec(tm, C_ROPE),
                  _const_spec((1, C_Q_RANK)), _const_spec((1, C_KV_RANK))],
        out_specs=(_row_spec(tm, C_DOWN_COLS), _const_spec((1, C_Q_RANK)), _const_spec((1, C_KV_RANK))),
        compiler_params=_params(("arbitrary",)),
    )(down, dcq, dckv, dkrope, q_a_norm, kv_a_norm)


def _head_major_spec(tm, width):
    return pl.BlockSpec((C_HEADS, tm, width), lambda i: (0, i, 0))


C_NOPE_V = C_NOPE + C_V
C_ROPE_COLS = C_HEADS * C_ROPE


def _mla_prep_tables(q_norm, k_norm):
    ql, kl, rl = np.arange(C_Q_COLS), np.arange(C_KV_COLS), np.arange(C_ROPE_COLS)
    col = np.arange(LANES)[None, :]
    is_nope = (kl % C_NOPE_V) < C_NOPE
    one_hot = lambda m: jnp.asarray(m.astype(np.float32), BF16)
    gk_nope = jnp.concatenate([k_norm[:, :C_NOPE], jnp.zeros((1, C_V), F32)], axis=1)
    return dict(
        q_dim=jnp.asarray((ql % C_QK)[None, :], jnp.int32),
        q_gain=jnp.tile(q_norm, (1, C_HEADS)),
        q_seg=one_hot(ql[:, None] // C_QK == col), q_spread=one_hot((ql[:, None] // C_QK == col).T),
        q_fold=one_hot(ql[:, None] % C_QK == col),
        k_nope=jnp.asarray(is_nope[None, :].astype(np.float32)),
        k_gain=jnp.tile(gk_nope, (1, C_HEADS)),
        k_seg=one_hot(is_nope[:, None] & (kl[:, None] // C_NOPE_V == col)),
        k_spread=one_hot((kl[:, None] // C_NOPE_V == col).T),
        k_fold=one_hot(is_nope[:, None] & (kl[:, None] % C_NOPE_V == col)),
        r_gain=jnp.tile(k_norm[:, C_NOPE:], (1, C_HEADS)),
        r_rep=one_hot(np.arange(C_ROPE)[:, None] == rl[None, :] % C_ROPE),
        r_seg=one_hot(rl[:, None] // C_ROPE == col), r_spread=one_hot((rl[:, None] // C_ROPE == col).T),
        r_fold=one_hot(rl[:, None] % C_ROPE == col))


def _q_partner(n, dim):
    half = C_ROPE // 2
    return jnp.where((dim >= C_NOPE) & (dim < C_NOPE + half), pltpu.roll(n, C_Q_COLS - half, 1),
                     jnp.where(dim >= C_NOPE + half, pltpu.roll(n, half, 1), 0.0))


def _rope_partner(n):
    half = C_ROPE // 2
    dim = lax.broadcasted_iota(jnp.int32, n.shape, 1) & (C_ROPE - 1)
    return jnp.where(dim < half, pltpu.roll(n, C_ROPE_COLS - half, 1), pltpu.roll(n, half, 1))


def _head_rstd(sum_sq):
    return lax.rsqrt(sum_sq * (1.0 / C_QK) + EPS)


MLA_PREP_FWD_TABLES = ['q_dim', 'q_gain', 'q_seg', 'q_spread', 'k_gain', 'k_seg', 'k_spread', 'r_gain', 'r_rep', 'r_spread']
MLA_PREP_BWD_TABLES = MLA_PREP_FWD_TABLES + ['q_fold', 'k_nope', 'k_fold', 'r_seg', 'r_fold']


def _mla_qk_fwd(qw, kvw, down, q_norm, k_norm, rows, name):
    t = qw.shape[0]
    tm = _div_tile(t, 256, 16)
    tables = _mla_prep_tables(q_norm, k_norm)
    consts = [tables[n] for n in MLA_PREP_FWD_TABLES]

    def body(q_ref, kv_ref, dn_ref, qcos_ref, qsin_ref, rcos_ref, rsin_ref, *rest):
        c = {n: r[...] for n, r in zip(MLA_PREP_FWD_TABLES, rest)}
        qo_ref, ko_ref, vo_ref = rest[len(MLA_PREP_FWD_TABLES):]
        q, kv, kr = q_ref[...], kv_ref[...], dn_ref[:, C_Q_RANK + C_KV_RANK:]
        nq = q * _pieces_dot(_head_rstd(_pieces_dot(q * q, c['q_seg'], 1)), c['q_spread'], 2) * c['q_gain']
        out_q = (nq * qcos_ref[...] + _q_partner(nq, c['q_dim']) * qsin_ref[...]) * C_Q_SCALE
        rstd = _head_rstd(_pieces_dot(kv * kv, c['k_seg'], 1) + jnp.sum(kr * kr, axis=-1, keepdims=True))
        nope = kv * _pieces_dot(rstd, c['k_spread'], 2) * c['k_gain']
        nr = _pieces_dot(kr, c['r_rep'], 2) * _pieces_dot(rstd, c['r_spread'], 2) * c['r_gain']
        rope = nr * rcos_ref[...] + _rope_partner(nr) * rsin_ref[...]
        pad = jnp.zeros((tm, C_PAD - C_QK), F32)
        one_then_zeros = (lax.broadcasted_iota(jnp.int32, (tm, C_PAD - C_V), 1) == 0).astype(F32)
        for h in range(C_HEADS):
            qo_ref[h] = jnp.concatenate([out_q[:, h * C_QK:(h + 1) * C_QK], pad], axis=1).astype(BF16)
            ko_ref[h] = jnp.concatenate([nope[:, h * C_NOPE_V:h * C_NOPE_V + C_NOPE], rope[:, h * C_ROPE:(h + 1) * C_ROPE],
                                         pad], axis=1).astype(BF16)
            vo_ref[h] = jnp.concatenate([kv[:, h * C_NOPE_V + C_NOPE:(h + 1) * C_NOPE_V], one_then_zeros],
                                        axis=1).astype(BF16)

    return pl.pallas_call(
        body, name=name,
        out_shape=(jax.ShapeDtypeStruct((C_HEADS, t, C_PAD), BF16), jax.ShapeDtypeStruct((C_HEADS, t, C_PAD), BF16),
                   jax.ShapeDtypeStruct((C_HEADS, t, C_PAD), BF16)),
        grid=(t // tm,),
        in_specs=[_row_spec(tm, C_Q_COLS), _row_spec(tm, C_KV_COLS), _row_spec(tm, C_DOWN_COLS)]
                 + [_row_spec(tm, r.shape[1]) for r in rows] + [_const_spec(a.shape) for a in consts],
        out_specs=(_head_major_spec(tm, C_PAD), _head_major_spec(tm, C_PAD), _head_major_spec(tm, C_PAD)),
        compiler_params=_params(("parallel",)),
    )(qw, kvw, down, *rows, *consts)


def _mla_qk_bwd(qw, kvw, down, dq, dk, dv, q_norm, k_norm, rows, name, rider=None):
    t = qw.shape[0]
    tm = _div_tile(t, 256, 16)
    tables = _mla_prep_tables(q_norm, k_norm)
    consts = [tables[n] for n in MLA_PREP_BWD_TABLES]

    def body(q_ref, kv_ref, dn_ref, dq_ref, dk_ref, dv_ref, qcos_ref, qsin_ref, rcos_ref, rsin_ref, *rest):
        c = {n: r[...] for n, r in zip(MLA_PREP_BWD_TABLES, rest)}
        dqw_ref, dkvw_ref, dkr_ref, dgq_ref, dgk_ref = rest[len(MLA_PREP_BWD_TABLES):]
        step = pl.program_id(0)
        q, kv, kr = q_ref[...], kv_ref[...], dn_ref[:, C_Q_RANK + C_KV_RANK:]
        dout_q = jnp.concatenate([dq_ref[h][:, :C_QK] for h in range(C_HEADS)], axis=1)
        d_slab = jnp.concatenate([x for h in range(C_HEADS) for x in (dk_ref[h][:, :C_NOPE], dv_ref[h])], axis=1)
        d_rope = jnp.concatenate([dk_ref[h][:, C_NOPE:C_QK] for h in range(C_HEADS)], axis=1)

        rq = _pieces_dot(_head_rstd(_pieces_dot(q * q, c['q_seg'], 1)), c['q_spread'], 2)
        xq = q * rq
        dnq = dout_q * qcos_ref[...] + _q_partner(dout_q * qsin_ref[...], c['q_dim'])
        dgq = _pieces_dot(jnp.sum(dnq * xq, axis=0, keepdims=True), c['q_fold'], 3)
        dxq = dnq * c['q_gain']
        mean_q = _pieces_dot(_pieces_dot(dxq * xq, c['q_seg'], 1) * (1.0 / C_QK), c['q_spread'], 2)
        dqw_ref[...] = (rq * (dxq - xq * mean_q)).astype(BF16)

        rstd = _head_rstd(_pieces_dot(kv * kv, c['k_seg'], 1) + jnp.sum(kr * kr, axis=-1, keepdims=True))
        r_nope, r_rope = _pieces_dot(rstd, c['k_spread'], 2), _pieces_dot(rstd, c['r_spread'], 2)
        x_nope = kv * r_nope * c['k_nope']
        x_rope = _pieces_dot(kr, c['r_rep'], 2) * r_rope
        dn_nope = d_slab * c['k_nope']
        dn_rope = d_rope * rcos_ref[...] + _rope_partner(d_rope * rsin_ref[...])
        dg_nope = _pieces_dot(jnp.sum(dn_nope * x_nope, axis=0, keepdims=True), c['k_fold'], 3)
        dg_rope = _pieces_dot(jnp.sum(dn_rope * x_rope, axis=0, keepdims=True), c['r_fold'], 3)
        dx_nope, dx_rope = dn_nope * c['k_gain'], dn_rope * c['r_gain']
        mean = (_pieces_dot(dx_nope * x_nope, c['k_seg'], 1) + _pieces_dot(dx_rope * x_rope, c['r_seg'], 1)) * (1.0 / C_QK)
        g_nope = r_nope * (dx_nope - x_nope * _pieces_dot(mean, c['k_spread'], 2))
        g_rope = r_rope * (dx_rope - x_rope * _pieces_dot(mean, c['r_spread'], 2))
        dkvw_ref[...] = jnp.where(c['k_nope'] > 0.0, g_nope, d_slab).astype(BF16)
        dkr_ref[...] = _pieces_dot(g_rope, c['r_fold'], 3)[:, :C_ROPE]
        _accumulate(dgq_ref, dgq[:, :C_QK], step)
        _accumulate(dgk_ref, jnp.concatenate([dg_nope[:, :C_NOPE], dg_rope[:, :C_ROPE]], axis=1), step)

    return _host_call(
        body, rider, name,
        out_shape=(jax.ShapeDtypeStruct((t, C_Q_COLS), BF16), jax.ShapeDtypeStruct((t, C_KV_COLS), BF16),
                   jax.ShapeDtypeStruct((t, C_ROPE), F32), jax.ShapeDtypeStruct((1, C_QK), F32),
                   jax.ShapeDtypeStruct((1, C_QK), F32)),
        grid=(t // tm,),
        in_specs=[_row_spec(tm, C_Q_COLS), _row_spec(tm, C_KV_COLS), _row_spec(tm, C_DOWN_COLS),
                  _head_major_spec(tm, C_PAD), _head_major_spec(tm, C_PAD), _head_major_spec(tm, C_V)]
                 + [_row_spec(tm, r.shape[1]) for r in rows] + [_const_spec(a.shape) for a in consts],
        out_specs=(_row_spec(tm, C_Q_COLS), _row_spec(tm, C_KV_COLS), _row_spec(tm, C_ROPE), _const_spec((1, C_QK)),
                   _const_spec((1, C_QK))),
        operands=(qw, kvw, down, dq, dk, dv, *rows, *consts), semantics=("arbitrary",))


def _causal_keep(rows, cols, row_offset=0, transposed=False):
    row = lax.broadcasted_iota(jnp.int32, (rows, cols), 0) + row_offset
    col = lax.broadcasted_iota(jnp.int32, (rows, cols), 1)
    return (row <= col) if transposed else (col <= row)


def _mla_fwd(q, k, v, name):
    _, t, _ = q.shape
    bq, bk = min(MLA_FWD_Q_BLOCK, t), min(MLA_FWD_K_BLOCK, t)
    nq = t // bq

    def body(q_ref, k_ref, v_ref, o_ref, lse_ref, m_sc, acc_sc):
        qi = pl.program_id(1)
        m_sc[...] = jnp.full_like(m_sc, NEG)
        acc_sc[...] = jnp.zeros_like(acc_sc)
        diagonal = (qi * bq) // bk
        lead = qi * bq - diagonal * bk

        def step(ki, masked):
            rows = pl.ds(pl.multiple_of(ki * bk, bk), bk)
            for hh in range(C_PAIR):
                s = lax.dot_general(q_ref[hh], k_ref[hh, rows, :], (((1,), (1,)), ((), ())), preferred_element_type=F32)
                if masked:
                    s = jnp.where(_causal_keep(bq, bk, lead), s, NEG)
                m_prev = m_sc[hh]
                m_new = jnp.maximum(m_prev, jnp.max(s, axis=-1, keepdims=True))
                p = jnp.exp2(s - m_new)
                acc_sc[hh] = jnp.exp2(m_prev - m_new) * acc_sc[hh] + jnp.dot(p.astype(BF16), v_ref[hh, rows, :],
                                                                                preferred_element_type=F32)
                m_sc[hh] = m_new

        def below_diagonal(ki, carry):
            step(ki, False)
            return carry

        lax.fori_loop(0, diagonal, below_diagonal, 0)
        step(diagonal, True)
        outs = []
        for hh in range(C_PAIR):
            denom = acc_sc[hh, :, C_V:C_V + 1]
            outs.append(acc_sc[hh, :, :C_V] / denom)
            lse_ref[hh] = m_sc[hh] + jnp.log(denom) * LOG2E
        o_ref[...] = jnp.concatenate(outs, axis=1).astype(BF16)

    whole = lambda hp, qi: (hp, 0, 0)
    return pl.pallas_call(
        body, name=name,
        out_shape=(jax.ShapeDtypeStruct((t, C_O_COLS), BF16), jax.ShapeDtypeStruct((C_HEADS, t, 1), F32)),
        grid=(C_HEADS // C_PAIR, nq),
        in_specs=[pl.BlockSpec((C_PAIR, bq, C_PAD), lambda hp, qi: (hp, qi, 0)),
                  pl.BlockSpec((C_PAIR, t, C_PAD), whole, pipeline_mode=pl.Buffered(1)),
                  pl.BlockSpec((C_PAIR, t, C_PAD), whole, pipeline_mode=pl.Buffered(1))],
        out_specs=(pl.BlockSpec((bq, C_PAIR * C_V), lambda hp, qi: (qi, hp)),
                   pl.BlockSpec((C_PAIR, bq, 1), lambda hp, qi: (hp, qi, 0))),
        scratch_shapes=[pltpu.VMEM((C_PAIR, bq, 1), F32), pltpu.VMEM((C_PAIR, bq, C_PAD), F32)],
        compiler_params=_params(("parallel", "arbitrary")),
    )(q, k, v)


def _mla_delta(do, o, name):
    t = do.shape[0]
    blk = min(MLA_BLOCK, t)

    def body(do_ref, o_ref, dlt_ref, dob_ref):
        for hh in range(C_PAIR):
            do_h = do_ref[:, hh * C_V:(hh + 1) * C_V]
            dlt_ref[hh] = jnp.sum(do_h * o_ref[:, hh * C_V:(hh + 1) * C_V].astype(F32), axis=-1, keepdims=True)
        dob_ref[...] = do_ref[...].astype(BF16)

    wide = pl.BlockSpec((blk, C_PAIR * C_V), lambda hp, i: (i, hp))
    return pl.pallas_call(
        body, name=name,
        out_shape=(jax.ShapeDtypeStruct((C_HEADS, t, 1), F32), jax.ShapeDtypeStruct(do.shape, BF16)),
        grid=(C_HEADS // C_PAIR, t // blk), in_specs=[wide, wide],
        out_specs=(pl.BlockSpec((C_PAIR, blk, 1), lambda hp, i: (hp, i, 0)), wide),
        compiler_params=_params(("parallel", "parallel")),
    )(do, o)


def _mla_bwd(q, k, v, do_b, lse_rows, dlt_rows, name):
    _, t, _ = q.shape
    blk = min(MLA_BLOCK, t)
    nq = t // blk

    def body(q_ref, k_ref, v_ref, do_ref, lse_ref, dlt_ref, dq_hbm, dk_ref, dv_ref, dq_sc, dk_sc, dv_sc, sem):
        hp, ki = pl.program_id(0), pl.program_id(1)

        @pl.when(ki == 0)
        def _():
            dq_sc[...] = jnp.zeros_like(dq_sc)

        dk_sc[...] = jnp.zeros_like(dk_sc)
        dv_sc[...] = jnp.zeros_like(dv_sc)

        def step(qi, masked):
            rows = pl.ds(pl.multiple_of(qi * blk, blk), blk)
            for hh in range(C_PAIR):
                qb = q_ref[hh, rows, :]
                dob = do_ref[rows, hh * C_V:(hh + 1) * C_V]
                s = lax.dot_general(k_ref[hh], qb, (((1,), (1,)), ((), ())), preferred_element_type=F32)
                if masked:
                    s = jnp.where(_causal_keep(blk, blk, transposed=True), s, NEG)
                p = jnp.exp2(s - lse_ref[hh, qi])
                dp = lax.dot_general(v_ref[hh, :, :C_V], dob, (((1,), (1,)), ((), ())), preferred_element_type=F32)
                ds = (p * (dp - dlt_ref[hh, qi])).astype(BF16)
                dv_sc[hh] += jnp.dot(p.astype(BF16), dob, preferred_element_type=F32)
                dk_sc[hh] += jnp.dot(ds, qb, preferred_element_type=F32)
                dq_sc[hh, rows, :] += lax.dot_general(ds, k_ref[hh], (((0,), (0,)), ((), ())), preferred_element_type=F32)

        def above_diagonal(qi, carry):
            step(qi, False)
            return carry

        step(ki, True)
        lax.fori_loop(ki + 1, nq, above_diagonal, 0)
        dk_ref[...] = dk_sc[...] * LN2
        dv_ref[...] = dv_sc[...]

        @pl.when(ki == nq - 1)
        def _():
            dq_sc[...] = dq_sc[...] * C_SCALE
            out = pltpu.make_async_copy(dq_sc, dq_hbm.at[pl.ds(hp * C_PAIR, C_PAIR)], sem)
            out.start()
            out.wait()

    once = pl.Buffered(1)
    whole = lambda hp, ki: (hp, 0, 0)
    whole4 = lambda hp, ki: (hp, 0, 0, 0)
    kmap = lambda hp, ki: (hp, ki, 0)
    return pl.pallas_call(
        body, name=name,
        out_shape=(jax.ShapeDtypeStruct((C_HEADS, t, C_PAD), F32), jax.ShapeDtypeStruct((C_HEADS, t, C_PAD), F32),
                   jax.ShapeDtypeStruct((C_HEADS, t, C_V), F32)),
        grid=(C_HEADS // C_PAIR, nq),
        in_specs=[pl.BlockSpec((C_PAIR, t, C_PAD), whole, pipeline_mode=once), pl.BlockSpec((C_PAIR, blk, C_PAD), kmap),
                  pl.BlockSpec((C_PAIR, blk, C_PAD), kmap),
                  pl.BlockSpec((t, C_PAIR * C_V), lambda hp, ki: (0, hp), pipeline_mode=once),
                  pl.BlockSpec((C_PAIR, nq, 1, blk), whole4, pipeline_mode=once),
                  pl.BlockSpec((C_PAIR, nq, 1, blk), whole4, pipeline_mode=once)],
        out_specs=(pl.BlockSpec(memory_space=pl.ANY), pl.BlockSpec((C_PAIR, blk, C_PAD), kmap),
                   pl.BlockSpec((C_PAIR, blk, C_V), kmap)),
        scratch_shapes=[pltpu.VMEM((C_PAIR, t, C_PAD), F32), pltpu.VMEM((C_PAIR, blk, C_PAD), F32),
                        pltpu.VMEM((C_PAIR, blk, C_V), F32), pltpu.SemaphoreType.DMA(())],
        compiler_params=_params(("arbitrary", "arbitrary")),
    )(q, k, v, do_b, lse_rows, dlt_rows)


def _adamw(parts, w, m, v, name):
    layers, rows, cols = w.shape
    tm = _div_tile(rows, 256, 16)

    def body(p_ref, w_ref, m_ref, v_ref, g_ref, d_ref, nm_ref, nv_ref):
        g = p_ref[0].astype(F32)
        for j in range(1, N_DEV):
            g = g + p_ref[j].astype(F32)
        nm = ADAM_B1 * m_ref[...] + (1.0 - ADAM_B1) * g
        nv = ADAM_B2 * v_ref[...] + (1.0 - ADAM_B2) * jnp.square(g)
        m_hat = nm / (1.0 - ADAM_B1 ** ADAM_STEP)
        v_hat = nv / (1.0 - ADAM_B2 ** ADAM_STEP)
        g_ref[...] = g
        d_ref[...] = -ADAM_LR * (m_hat / (jnp.sqrt(v_hat) + ADAM_EPS) + ADAM_WD * w_ref[...])
        nm_ref[...] = nm
        nv_ref[...] = nv

    spec = pl.BlockSpec((None, tm, cols), lambda l, i: (l, i, 0))
    return pl.pallas_call(
        body, name=name, out_shape=tuple(jax.ShapeDtypeStruct(w.shape, F32) for _ in range(4)),
        grid=(layers, rows // tm),
        in_specs=[pl.BlockSpec((None, N_DEV, tm, cols), lambda l, i: (l, 0, i, 0)), spec, spec, spec],
        out_specs=(spec, spec, spec, spec), compiler_params=_params(("parallel", "parallel")),
    )(parts, w, m, v)


def _join_shards(gathered, axis):
    moved = jnp.moveaxis(gathered, 1, axis)
    shape = list(moved.shape)
    shape[axis:axis + 2] = [shape[axis] * shape[axis + 1]]
    return moved.reshape(shape)


def _split_shards(full, axis):
    shape = list(full.shape)
    shape[axis:axis + 1] = [N_DEV, shape[axis] // N_DEV]
    return jnp.moveaxis(full.reshape(shape), axis, 1)


def _as_rows(shape):
    rest = tuple(shape[1:])
    return (shape[0], 1, rest[0]) if len(rest) == 1 else (shape[0],) + rest


MIXER_WEIGHTS = {0: ['a_w_qkv', 'a_w_o'], 1: ['b_w_in', 'b_conv_w', 'b_w_out'],
                 2: ['c_w_down', 'c_q_a_norm', 'c_kv_a_norm', 'c_w_q_up', 'c_w_kv_up', 'c_w_o']}


def _layer_units(i):
    return [(n, i // N_MIXERS) for n in MIXER_WEIGHTS[i % N_MIXERS]] + [('f_w_gate_up', i), ('f_w_down', i)]


def _forward_backward(x, positions, target, local, rep):
    def gather(units):
        return _Exchange([local[n][i:i + 1].astype(BF16) if n in GATHER_BF16 else local[n][i:i + 1] for n, i in units],
                         scatter=False)

    w = {n: {} for n in SHARDED}

    def arrived(units, gathered):
        for (n, i), g in zip(units, gathered):
            full = _join_shards(g, SHARD_AXIS[n])
            w[n][i] = full if full.ndim == 2 else full[0]

    all_units = [u for i in range(DEPTH) for u in _layer_units(i)]
    first_units = _layer_units(0) + [u for u in all_units if u[0] in GATHER_F32]
    later_units = [u for u in all_units if u not in first_units]
    arrived(first_units, _exchange_now(gather(first_units), "gather_first_weights"))

    cos_a, sin_a = (_repeat_lanes(tbl, A_HEADS + A_KV_HEADS, f"a_rope_table_{i}")
                    for i, tbl in enumerate(_rope_tables(positions, A_ROT_DIM, 0, A_HEAD_DIM - A_ROT_DIM)))
    tables_c = _rope_tables(positions, C_ROPE, C_NOPE, 0)
    rows_c = ([_repeat_lanes(tbl, C_HEADS, f"c_rope_table_q{i}") for i, tbl in enumerate(tables_c)]
              + [_repeat_lanes(tbl[:, C_NOPE:], C_HEADS, f"c_rope_table_k{i}") for i, tbl in enumerate(tables_c)])
    saved = []
    for i in range(DEPTH):
        kind, j = i % N_MIXERS, i // N_MIXERS
        s = {'x': x}
        h1 = _rmsnorm_fwd(x, rep['mix_norm'][i:i + 1], f"mix_norm_fwd_{i}")
        s['h1'] = h1
        if kind == 0:
            s['qkv'] = _matmul(h1, w['a_w_qkv'][j], 'nn', f"a_qkv_{i}")
            s['qkv_r'] = _swa_prep_fwd(s['qkv'], rep['a_q_norm'][j:j + 1], rep['a_k_norm'][j:j + 1], cos_a, sin_a,
                                       f"a_prep_fwd_{i}")
            (s['o'], s['lse']), gathered = _swa_fwd(s['qkv_r'], rep['a_sinks'][j:j + 1], f"a_attn_fwd_{i}",
                                                    rider=gather(later_units) if i == 0 else None)
            if i == 0:
                arrived(later_units, gathered)
            x1 = _matmul(s['o'], w['a_w_o'][j], 'nn', f"a_out_{i}", residual=x)
        elif kind == 1:
            s['bcu'] = _matmul(h1, w['b_w_in'][j], 'nn', f"b_in_{i}")
            s['by'] = _sconv_fwd(s['bcu'], w['b_conv_w'][j], f"b_conv_fwd_{i}")
            x1 = _matmul(s['by'], w['b_w_out'][j], 'nn', f"b_out_{i}", residual=x)
        else:
            s['down'] = _matmul(h1, w['c_w_down'][j], 'nn', f"c_down_{i}")
            s['cq'], s['ckv'] = _mla_latent_fwd(s['down'], w['c_q_a_norm'][j], w['c_kv_a_norm'][j],
                                                f"c_latent_fwd_{i}")
            s['qw'] = _matmul(s['cq'], w['c_w_q_up'][j], 'nn', f"c_q_up_{i}")
            s['kvw'] = _matmul(s['ckv'], w['c_w_kv_up'][j], 'nn', f"c_kv_up_{i}")
            s['q'], s['k'], s['v'] = _mla_qk_fwd(s['qw'], s['kvw'], s['down'], rep['c_q_norm'][j:j + 1],
                                                 rep['c_k_norm'][j:j + 1], rows_c, f"c_prep_fwd_{i}")
            s['o'], s['lse'] = _mla_fwd(s['q'], s['k'], s['v'], f"c_attn_fwd_{i}")
            x1 = _matmul(s['o'], w['c_w_o'][j], 'nn', f"c_out_{i}", residual=x)
        s['x1'] = x1
        s['h2'] = _rmsnorm_fwd(x1, rep['ffn_norm'][i:i + 1], f"ffn_norm_fwd_{i}")
        s['gu'] = _matmul(s['h2'], w['f_w_gate_up'][i], 'nn', f"f_gate_up_{i}", out_dtype=BF16)
        s['act'] = _swiglu_fwd(s['gu'], f"f_act_fwd_{i}")
        x = _matmul(s['act'], w['f_w_down'][i], 'nn', f"f_down_{i}", residual=x1)
        saved.append(s)

    loss, dx = _loss_head(x, target, "loss_head")

    per_layer = {n: {} for n in WEIGHTS}
    received = {}
    sent = set()

    def ready():
        units = [(n, j) for n in SHARDED for j in sorted(per_layer[n]) if (n, j) not in sent]
        if not units:
            return None, units
        sent.update(units)
        blocks = []
        for n, j in units:
            g = per_layer[n][j]
            blocks.append(_split_shards(g if n in ('c_q_a_norm', 'c_kv_a_norm') else g[None], SHARD_AXIS[n]))
        return _Exchange(blocks, scatter=True), units

    for i in reversed(range(DEPTH)):
        kind, j = i % N_MIXERS, i // N_MIXERS
        s = saved[i]
        per_layer['f_w_down'][i] = _matmul(s['act'], dx, 'tn', f"f_down_dw_{i}", out_dtype=BF16)
        dact = _matmul(dx, w['f_w_down'][i], 'nt', f"f_down_dx_{i}", out_dtype=BF16)
        dgu = _swiglu_bwd(s['gu'], dact, f"f_act_bwd_{i}")
        per_layer['f_w_gate_up'][i] = _matmul(s['h2'], dgu, 'tn', f"f_gate_up_dw_{i}", out_dtype=BF16)
        dh2 = _matmul(dgu, w['f_w_gate_up'][i], 'nt', f"f_gate_up_dx_{i}")
        dx1, per_layer['ffn_norm'][i] = _rmsnorm_bwd(s['x1'], rep['ffn_norm'][i:i + 1], dh2, dx, f"ffn_norm_bwd_{i}")
        if kind == 0:
            per_layer['a_w_o'][j] = _matmul(s['o'], dx1, 'tn', f"a_out_dw_{i}", out_dtype=BF16)
            do = _matmul(dx1, w['a_w_o'][j], 'nt', f"a_out_dx_{i}")
            rider, units = ready()
            (dqkv_r, per_layer['a_sinks'][j]), parts = _swa_bwd(s['qkv_r'], s['o'], s['lse'], do, rep['a_sinks'][j:j + 1],
                                                                f"a_attn_bwd_{i}", rider=rider)
            received.update(zip(units, parts or ()))
            dqkv, per_layer['a_q_norm'][j], per_layer['a_k_norm'][j] = _swa_prep_bwd(
                s['qkv'], dqkv_r, rep['a_q_norm'][j:j + 1], rep['a_k_norm'][j:j + 1], cos_a, sin_a, f"a_prep_bwd_{i}")
            per_layer['a_w_qkv'][j] = _matmul(s['h1'], dqkv, 'tn', f"a_qkv_dw_{i}", out_dtype=BF16)
            dh1 = _matmul(dqkv, w['a_w_qkv'][j], 'nt', f"a_qkv_dx_{i}")
        elif kind == 1:
            per_layer['b_w_out'][j] = _matmul(s['by'], dx1, 'tn', f"b_out_dw_{i}", out_dtype=BF16)
            dby = _matmul(dx1, w['b_w_out'][j], 'nt', f"b_out_dx_{i}")
            dbcu, per_layer['b_conv_w'][j] = _sconv_bwd(s['bcu'], dby, w['b_conv_w'][j], f"b_conv_bwd_{i}")
            per_layer['b_w_in'][j] = _matmul(s['h1'], dbcu, 'tn', f"b_in_dw_{i}", out_dtype=BF16)
            dh1 = _matmul(dbcu, w['b_w_in'][j], 'nt', f"b_in_dx_{i}")
        else:
            per_layer['c_w_o'][j] = _matmul(s['o'], dx1, 'tn', f"c_out_dw_{i}", out_dtype=BF16)
            do = _matmul(dx1, w['c_w_o'][j], 'nt', f"c_out_dx_{i}")
            dlt, do_b = _mla_delta(do, s['o'], f"c_attn_delta_{i}")
            blk = min(MLA_BLOCK, do.shape[0])
            as_rows = lambda col: col.reshape(C_HEADS, do.shape[0] // blk, 1, blk)
            dq, dk, dv = _mla_bwd(s['q'], s['k'], s['v'], do_b, as_rows(s['lse']), as_rows(dlt), f"c_attn_bwd_{i}")
            rider, units = ready()
            (dqw, dkvw, dkrope, per_layer['c_q_norm'][j], per_layer['c_k_norm'][j]), parts = _mla_qk_bwd(
                s['qw'], s['kvw'], s['down'], dq, dk, dv, rep['c_q_norm'][j:j + 1], rep['c_k_norm'][j:j + 1], rows_c,
                f"c_prep_bwd_{i}", rider=rider)
            received.update(zip(units, parts or ()))
            per_layer['c_w_q_up'][j] = _matmul(s['cq'], dqw, 'tn', f"c_q_up_dw_{i}", out_dtype=BF16)
            dcq = _matmul(dqw, w['c_w_q_up'][j], 'nt', f"c_q_up_dx_{i}")
            per_layer['c_w_kv_up'][j] = _matmul(s['ckv'], dkvw, 'tn', f"c_kv_up_dw_{i}", out_dtype=BF16)
            dckv = _matmul(dkvw, w['c_w_kv_up'][j], 'nt', f"c_kv_up_dx_{i}")
            ddown, per_layer['c_q_a_norm'][j], per_layer['c_kv_a_norm'][j] = _mla_latent_bwd(
                s['down'], dcq, dckv, dkrope, w['c_q_a_norm'][j], w['c_kv_a_norm'][j], f"c_latent_bwd_{i}")
            per_layer['c_w_down'][j] = _matmul(s['h1'], ddown, 'tn', f"c_down_dw_{i}", out_dtype=BF16)
            dh1 = _matmul(ddown, w['c_w_down'][j], 'nt', f"c_down_dx_{i}")
        dx, per_layer['mix_norm'][i] = _rmsnorm_bwd(s['x'], rep['mix_norm'][i:i + 1], dh1, dx1, f"mix_norm_bwd_{i}")

    last, units = ready()
    received.update(zip(units, _exchange_now(last, "scatter_last_gradients")))
    parts = {n: jnp.concatenate([received[(n, j)] for j in sorted(per_layer[n])], axis=0) for n in SHARDED}
    small = {}
    for n in REPLICATED:
        stacked = jnp.stack([per_layer[n][j] for j in sorted(per_layer[n])])
        small[n] = stacked.reshape(stacked.shape[0], stacked.shape[-1])
    return loss, dx, parts, small


def kernel(x, positions, mix_norm, ffn_norm, a_w_qkv, a_q_norm, a_k_norm, a_sinks, a_w_o, b_w_in, b_conv_w, b_w_out, c_w_down, c_q_a_norm, c_kv_a_norm, c_w_q_up, c_w_kv_up, c_q_norm, c_k_norm, c_w_o, f_w_gate_up, f_w_down, loss_target, m_mix_norm, m_ffn_norm, m_a_w_qkv, m_a_q_norm, m_a_k_norm, m_a_sinks, m_a_w_o, m_b_w_in, m_b_conv_w, m_b_w_out, m_c_w_down, m_c_q_a_norm, m_c_kv_a_norm, m_c_w_q_up, m_c_w_kv_up, m_c_q_norm, m_c_k_norm, m_c_w_o, m_f_w_gate_up, m_f_w_down, v_mix_norm, v_ffn_norm, v_a_w_qkv, v_a_q_norm, v_a_k_norm, v_a_sinks, v_a_w_o, v_b_w_in, v_b_conv_w, v_b_w_out, v_c_w_down, v_c_q_a_norm, v_c_kv_a_norm, v_c_w_q_up, v_c_w_kv_up, v_c_q_norm, v_c_k_norm, v_c_w_o, v_f_w_gate_up, v_f_w_down):
    local = dict(mix_norm=mix_norm, ffn_norm=ffn_norm, a_w_qkv=a_w_qkv, a_q_norm=a_q_norm, a_k_norm=a_k_norm, a_sinks=a_sinks, a_w_o=a_w_o, b_w_in=b_w_in, b_conv_w=b_conv_w, b_w_out=b_w_out, c_w_down=c_w_down, c_q_a_norm=c_q_a_norm, c_kv_a_norm=c_kv_a_norm, c_w_q_up=c_w_q_up, c_w_kv_up=c_w_kv_up, c_q_norm=c_q_norm, c_k_norm=c_k_norm, c_w_o=c_w_o, f_w_gate_up=f_w_gate_up, f_w_down=f_w_down)
    mom1 = dict(mix_norm=m_mix_norm, ffn_norm=m_ffn_norm, a_w_qkv=m_a_w_qkv, a_q_norm=m_a_q_norm, a_k_norm=m_a_k_norm, a_sinks=m_a_sinks, a_w_o=m_a_w_o, b_w_in=m_b_w_in, b_conv_w=m_b_conv_w, b_w_out=m_b_w_out, c_w_down=m_c_w_down, c_q_a_norm=m_c_q_a_norm, c_kv_a_norm=m_c_kv_a_norm, c_w_q_up=m_c_w_q_up, c_w_kv_up=m_c_w_kv_up, c_q_norm=m_c_q_norm, c_k_norm=m_c_k_norm, c_w_o=m_c_w_o, f_w_gate_up=m_f_w_gate_up, f_w_down=m_f_w_down)
    mom2 = dict(mix_norm=v_mix_norm, ffn_norm=v_ffn_norm, a_w_qkv=v_a_w_qkv, a_q_norm=v_a_q_norm, a_k_norm=v_a_k_norm, a_sinks=v_a_sinks, a_w_o=v_a_w_o, b_w_in=v_b_w_in, b_conv_w=v_b_conv_w, b_w_out=v_b_w_out, c_w_down=v_c_w_down, c_q_a_norm=v_c_q_a_norm, c_kv_a_norm=v_c_kv_a_norm, c_w_q_up=v_c_w_q_up, c_w_kv_up=v_c_w_kv_up, c_q_norm=v_c_q_norm, c_k_norm=v_c_k_norm, c_w_o=v_c_w_o, f_w_gate_up=v_f_w_gate_up, f_w_down=v_f_w_down)
    t, d = x.shape[1], x.shape[2]

    rep = {n: local[n] for n in REPLICATED}
    loss, grad_x, parts, small = _forward_backward(x.reshape(t, d), positions.reshape(t), loss_target.reshape(t, d),
                                                   {n: local[n] for n in SHARDED}, rep)

    out_g, out_d, out_m, out_v = {}, {}, {}, {}

    def update(names, parts):
        for n, part in zip(names, parts):
            shape = local[n].shape if n in SHARD_AXIS else (1,) + local[n].shape
            view = _as_rows(shape)
            results = _adamw(part.reshape(view[0], N_DEV, view[1], view[2]),
                             *[src[n].reshape(view) for src in (local, mom1, mom2)], name="adamw_" + n)
            for dst, res in zip((out_g, out_d, out_m, out_v), results):
                dst[n] = res.reshape(local[n].shape)

    update(SHARDED, [parts[n] for n in SHARDED])
    update(REPLICATED, _exchange_now(_Exchange([small[n].reshape((1,) + small[n].shape) for n in REPLICATED],
                                               scatter=False), "gather_small_gradients"))

    loss = lax.psum(loss.reshape(()), MESH_AXES)
    outs = [loss, grad_x.reshape(1, t, d)]
    for res in (out_g, out_d, out_m, out_v):
        outs += [res[n] for n in WEIGHTS]
    return tuple(outs)
```

```python
import jax
import jax.numpy as jnp
import numpy as np
from jax import lax
from jax.experimental import pallas as pl
from jax.experimental.pallas import tpu as pltpu

F32 = jnp.float32
BF16 = jnp.bfloat16

N_DEV = 8
MESH_AXES = ("x", "y", "c")

DEPTH = 4
N_MIXERS = 3
ROPE_THETA = 500000.0
EPS = 1e-6
A_HEADS, A_KV_HEADS, A_HEAD_DIM, A_ROT_DIM, A_WINDOW = 16, 4, 64, 16, 128
A_GROUP = A_HEADS // A_KV_HEADS
C_HEADS, C_NOPE, C_ROPE, C_V, C_Q_RANK, C_KV_RANK = 16, 64, 32, 64, 384, 256
C_QK = C_NOPE + C_ROPE
ADAM_LR, ADAM_B1, ADAM_B2, ADAM_EPS, ADAM_WD, ADAM_STEP = 0.001, 0.9, 0.999, 1e-08, 0.01, 10

VMEM_LIMIT_BYTES = 48 * 1024 * 1024
LANES = 128
NEG = -1e30
MLA_BLOCK = 512
MLA_FWD_Q_BLOCK = 512
MLA_FWD_K_BLOCK = 2048

WEIGHTS = ['mix_norm', 'ffn_norm', 'a_w_qkv', 'a_q_norm', 'a_k_norm', 'a_sinks', 'a_w_o', 'b_w_in', 'b_conv_w', 'b_w_out',
           'c_w_down', 'c_q_a_norm', 'c_kv_a_norm', 'c_w_q_up', 'c_w_kv_up', 'c_q_norm', 'c_k_norm', 'c_w_o', 'f_w_gate_up',
           'f_w_down']
SHARD_AXIS = {'a_w_qkv': 2, 'a_w_o': 1, 'b_w_in': 2, 'b_conv_w': 2, 'b_w_out': 1, 'c_w_down': 1, 'c_q_a_norm': 1,
              'c_kv_a_norm': 1, 'c_w_q_up': 2, 'c_w_kv_up': 2, 'c_w_o': 1, 'f_w_gate_up': 2, 'f_w_down': 1}
SHARDED = [n for n in WEIGHTS if n in SHARD_AXIS]
REPLICATED = [n for n in WEIGHTS if n not in SHARD_AXIS]
GATHER_F32 = ['b_conv_w', 'c_q_a_norm', 'c_kv_a_norm']
GATHER_BF16 = [n for n in SHARDED if n not in GATHER_F32]

def _params(semantics=None):
    return pltpu.CompilerParams(dimension_semantics=semantics, vmem_limit_bytes=VMEM_LIMIT_BYTES)


def _div_tile(n, cap, mult=LANES):
    best = None
    t = mult
    while t <= min(n, cap):
        if n % t == 0:
            best = t
        t += mult
    return n if best is None else best


ANY_SPEC = pl.BlockSpec(memory_space=pl.ANY)


class _Exchange:
    def __init__(self, arrays, scatter):
        self.arrays, self.scatter = list(arrays), scatter
        n = len(self.arrays)
        self.out_shapes = [jax.ShapeDtypeStruct(a.shape if scatter else (a.shape[0], N_DEV) + tuple(a.shape[1:]), a.dtype)
                           for a in self.arrays]
        self.scratch = [pltpu.SemaphoreType.DMA((n, N_DEV - 1)), pltpu.SemaphoreType.DMA((n, N_DEV - 1)),
                        pltpu.SemaphoreType.DMA((n,))]

    def _copies(self, src_refs, out_refs, sems):
        send_sems, recv_sems, local_sems = sems
        x, y, c = lax.axis_index("x"), lax.axis_index("y"), lax.axis_index("c")
        me_idx = 4 * x + 2 * y + c
        n = len(self.arrays)

        def remote(a, k, src, dst, to):
            return pltpu.make_async_remote_copy(src_ref=src, dst_ref=dst, send_sem=send_sems.at[a, k],
                                                recv_sem=recv_sems.at[a, k], device_id=to,
                                                device_id_type=pl.DeviceIdType.MESH)

        local, first, forwards, last = [], [], [], []
        if self.scatter:
            for a in range(n):
                local.append(pltpu.make_async_copy(src_refs[a].at[:, me_idx], out_refs[a].at[:, me_idx], local_sems.at[a]))
                for r in range(1, N_DEV):
                    px = 1 - x if (r >> 2) & 1 else x
                    py = 1 - y if (r >> 1) & 1 else y
                    pc = 1 - c if r & 1 else c
                    cp = remote(a, r - 1, src_refs[a].at[:, 4 * px + 2 * py + pc], out_refs[a].at[:, me_idx], (px, py, pc))
                    first.append(cp)
                    last.append(cp)
            return local, first, forwards, last
        me, sibling = (x, y, c), (x, y, 1 - c)
        chips = [(1 - x, y), (x, 1 - y), (1 - x, 1 - y)]

        def place(a, block):
            return out_refs[a].at[:, 4 * block[0] + 2 * block[1] + block[2]]

        for a in range(n):
            local.append(pltpu.make_async_copy(src_refs[a], place(a, me), local_sems.at[a]))
            first.append(remote(a, 0, src_refs[a], place(a, me), sibling))
            last.append(remote(a, 0, place(a, sibling), place(a, sibling), me))
            for j, chip in enumerate(chips):
                first.append(remote(a, 1 + j, src_refs[a], place(a, me), (*chip, c)))
                forwards.append((remote(a, 1 + j, place(a, (*chip, c)), place(a, (*chip, c)), me),
                                 remote(a, 4 + j, place(a, (*chip, c)), place(a, (*chip, c)), sibling)))
                last.append(remote(a, 4 + j, place(a, (*chip, 1 - c)), place(a, (*chip, 1 - c)), me))
        return local, first, forwards, last

    def start(self, src_refs, out_refs, sems):
        local, first, _, _ = self._copies(src_refs, out_refs, sems)
        for cp in local + first:
            cp.start()

    def finish(self, src_refs, out_refs, sems):
        local, first, forwards, last = self._copies(src_refs, out_refs, sems)
        for arrival, forward in forwards:
            arrival.wait_recv()
            forward.start()
        for cp in last:
            cp.wait_recv()
        for cp in first + [forward for _, forward in forwards]:
            cp.wait_send()
        for cp in local:
            cp.wait()


def _exchange_now(exchange, name):
    n = len(exchange.arrays)

    def body(*refs):
        exchange.start(refs[:n], refs[n:2 * n], refs[2 * n:])
        exchange.finish(refs[:n], refs[n:2 * n], refs[2 * n:])

    return pl.pallas_call(
        body, name=name, out_shape=tuple(exchange.out_shapes), in_specs=[ANY_SPEC] * n, out_specs=(ANY_SPEC,) * n,
        scratch_shapes=exchange.scratch,
    )(*exchange.arrays)


def _host_call(body, rider, name, out_shape, grid, in_specs, out_specs, operands, semantics):
    if rider is None:
        return pl.pallas_call(body, name=name, out_shape=tuple(out_shape), grid=grid, in_specs=list(in_specs),
                              out_specs=tuple(out_specs), compiler_params=_params(semantics))(*operands), None
    n_in, n_out, r = len(in_specs), len(out_shape), len(rider.arrays)

    def riding(*refs):
        ins, rider_in = refs[:n_in], refs[n_in:n_in + r]
        outs, rider_out = refs[n_in + r:n_in + r + n_out], refs[n_in + r + n_out:n_in + 2 * r + n_out]
        sems = refs[n_in + 2 * r + n_out:]
        step = pl.program_id(0)

        @pl.when(step == 0)
        def _():
            rider.start(rider_in, rider_out, sems)

        body(*ins, *outs)

        @pl.when(step == grid[0] - 1)
        def _():
            rider.finish(rider_in, rider_out, sems)

    results = pl.pallas_call(
        riding, name=name, out_shape=tuple(out_shape) + tuple(rider.out_shapes), grid=grid,
        in_specs=list(in_specs) + [ANY_SPEC] * r, out_specs=tuple(out_specs) + (ANY_SPEC,) * r,
        scratch_shapes=rider.scratch, compiler_params=_params(("arbitrary",)),
    )(*operands, *rider.arrays)
    return results[:n_out], results[n_out:]


def _matmul(a, b, mode, name, out_dtype=F32, residual=None):
    if mode == 'nn':
        (m, k), (k2, n) = a.shape, b.shape
    elif mode == 'nt':
        (m, k), (n, k2) = a.shape, b.shape
    else:
        (k, m), (k2, n) = a.shape, b.shape
    assert k == k2, (name, a.shape, b.shape, mode)
    if mode == 'tn':
        tm, tk = _div_tile(m, 1408), _div_tile(k, 1024, 16)
    else:
        tm, tk = _div_tile(m, 1024, 16), _div_tile(k, 1536)
    tn = _div_tile(n, 1408)
    nk = k // tk
    dims = {'nn': (((1,), (0,)), ((), ())), 'nt': (((1,), (1,)), ((), ())), 'tn': (((0,), (0,)), ((), ()))}[mode]

    def product(a_ref, b_ref):
        return lax.dot_general(a_ref[...].astype(BF16), b_ref[...].astype(BF16), dims, preferred_element_type=F32)

    def finish(r, rest):
        if residual is not None:
            r = r + rest[0][...]
        rest[-1 if nk == 1 else -2][...] = r.astype(out_dtype)

    def body_single(a_ref, b_ref, *rest):
        finish(product(a_ref, b_ref), rest)

    def body_accumulate(a_ref, b_ref, *rest):
        acc = rest[-1]
        kk = pl.program_id(2)

        @pl.when(kk == 0)
        def _():
            acc[...] = jnp.zeros_like(acc)

        acc[...] += product(a_ref, b_ref)

        @pl.when(kk == nk - 1)
        def _():
            finish(acc[...], rest)

    a_spec = pl.BlockSpec((tk, tm), lambda i, j, kk: (kk, i)) if mode == 'tn' else pl.BlockSpec((tm, tk), lambda i, j, kk: (i, kk))
    b_spec = pl.BlockSpec((tn, tk), lambda i, j, kk: (j, kk)) if mode == 'nt' else pl.BlockSpec((tk, tn), lambda i, j, kk: (kk, j))
    o_spec = pl.BlockSpec((tm, tn), lambda i, j, kk: (i, j))
    in_specs, operands = [a_spec, b_spec], [a, b]
    if residual is not None:
        in_specs.append(o_spec)
        operands.append(residual)
    return pl.pallas_call(
        body_single if nk == 1 else body_accumulate, name=name, out_shape=jax.ShapeDtypeStruct((m, n), out_dtype),
        grid=(m // tm, n // tn, nk), in_specs=in_specs, out_specs=o_spec,
        scratch_shapes=[] if nk == 1 else [pltpu.VMEM((tm, tn), F32)],
        compiler_params=_params(("parallel", "parallel", "arbitrary")),
    )(*operands)


def _row_spec(tm, cols):
    return pl.BlockSpec((tm, cols), lambda i: (i, 0))


def _const_spec(shape):
    return pl.BlockSpec(shape, lambda i: tuple(0 for _ in shape))


def _accumulate(ref, value, step):
    @pl.when(step == 0)
    def _():
        ref[...] = value

    @pl.when(step > 0)
    def _():
        ref[...] += value


def _rstd(x):
    return lax.rsqrt(jnp.mean(x * x, axis=-1, keepdims=True) + EPS)


def _norm_bwd(x, g, dout):
    xn = x * _rstd(x)
    dg = jnp.sum(dout * xn, axis=0, keepdims=True)
    dxn = dout * g
    dx = _rstd(x) * (dxn - xn * jnp.mean(dxn * xn, axis=-1, keepdims=True))
    return dx, dg


def _rmsnorm_fwd(x, g, name):
    t, d = x.shape
    tm = _div_tile(t, 512, 16)

    def body(x_ref, g_ref, o_ref):
        xv = x_ref[...]
        o_ref[...] = (xv * _rstd(xv) * g_ref[...]).astype(BF16)

    return pl.pallas_call(
        body, name=name, out_shape=jax.ShapeDtypeStruct((t, d), BF16), grid=(t // tm,),
        in_specs=[_row_spec(tm, d), _const_spec((1, d))], out_specs=_row_spec(tm, d),
        compiler_params=_params(("parallel",)),
    )(x, g)


def _rmsnorm_bwd(x, g, dh, dres, name):
    t, d = x.shape
    tm = _div_tile(t, 512, 8)

    def body(x_ref, g_ref, dh_ref, dres_ref, dx_ref, dg_ref):
        dx, dg = _norm_bwd(x_ref[...], g_ref[...], dh_ref[...])
        dx_ref[...] = dres_ref[...] + dx
        _accumulate(dg_ref, dg, pl.program_id(0))

    return pl.pallas_call(
        body, name=name,
        out_shape=(jax.ShapeDtypeStruct((t, d), F32), jax.ShapeDtypeStruct((1, d), F32)), grid=(t // tm,),
        in_specs=[_row_spec(tm, d), _const_spec((1, d)), _row_spec(tm, d), _row_spec(tm, d)],
        out_specs=(_row_spec(tm, d), _const_spec((1, d))),
        compiler_params=_params(("arbitrary",)),
    )(x, g, dh, dres)


def _sigmoid(x):
    return 0.5 * jnp.tanh(0.5 * x) + 0.5


def _gate_up_act(h, w, name):
    t, d = h.shape
    f = w.shape[1] // 2
    tm, tn = _div_tile(t, 512, 16), _div_tile(f, 1408)
    nj = f // tn

    def body(h_ref, wg_ref, wu_ref, g_ref, u_ref, a_ref):
        hv = h_ref[...]
        gate = jnp.dot(hv, wg_ref[...], preferred_element_type=F32)
        up = jnp.dot(hv, wu_ref[...], preferred_element_type=F32)
        g_ref[...] = gate.astype(BF16)
        u_ref[...] = up.astype(BF16)
        a_ref[...] = (gate * _sigmoid(gate) * up).astype(BF16)

    tile = pl.BlockSpec((tm, tn), lambda i, j: (i, j))
    return pl.pallas_call(
        body, name=name, out_shape=tuple(jax.ShapeDtypeStruct((t, f), BF16) for _ in range(3)), grid=(t // tm, nj),
        in_specs=[pl.BlockSpec((tm, d), lambda i, j: (i, 0)), pl.BlockSpec((d, tn), lambda i, j: (0, j)),
                  pl.BlockSpec((d, tn), lambda i, j: (0, j + nj))],
        out_specs=(tile, tile, tile), compiler_params=_params(("parallel", "parallel")),
    )(h, w, w)


def _swiglu_bwd(gate_pre, up_pre, da, name):
    t, f = gate_pre.shape
    tm = _div_tile(t, 512, 16)

    def body(g_ref, u_ref, da_ref, o_ref):
        gate, up, dav = g_ref[...].astype(F32), u_ref[...].astype(F32), da_ref[...].astype(F32)
        sig = _sigmoid(gate)
        o_ref[:, :f] = (dav * up * (sig * (1.0 + gate * (1.0 - sig)))).astype(BF16)
        o_ref[:, f:] = (dav * (gate * sig)).astype(BF16)

    return pl.pallas_call(
        body, name=name, out_shape=jax.ShapeDtypeStruct((t, 2 * f), BF16), grid=(t // tm,),
        in_specs=[_row_spec(tm, f), _row_spec(tm, f), _row_spec(tm, f)], out_specs=_row_spec(tm, 2 * f),
        compiler_params=_params(("parallel",)),
    )(gate_pre, up_pre, da)


def _loss_head(y, target, name):
    t, d = y.shape
    tm = _div_tile(t, 512, 8)

    def body(y_ref, t_ref, loss_ref, dy_ref):
        diff = y_ref[...] - t_ref[...]
        dy_ref[...] = diff * (1.0 / d)
        part = jnp.sum(jnp.sum(diff * diff, axis=1, keepdims=True), axis=0, keepdims=True) * (0.5 / d)
        _accumulate(loss_ref, part, pl.program_id(0))

    return pl.pallas_call(
        body, name=name,
        out_shape=(jax.ShapeDtypeStruct((1, 1), F32), jax.ShapeDtypeStruct((t, d), F32)), grid=(t // tm,),
        in_specs=[_row_spec(tm, d), _row_spec(tm, d)], out_specs=(_const_spec((1, 1)), _row_spec(tm, d)),
        compiler_params=_params(("arbitrary",)),
    )(y, target)


HALO = 8


def _shift_down(z, k, halo_rows):
    tm = z.shape[0]
    row = lax.broadcasted_iota(jnp.int32, z.shape, 0)
    out = pltpu.roll(z, k, 0)
    for j in range(k):
        out = jnp.where(row == j, halo_rows[HALO - k + j:HALO - k + j + 1, :], out)
    return out


def _shift_up(z, k, halo_rows):
    tm = z.shape[0]
    row = lax.broadcasted_iota(jnp.int32, z.shape, 0)
    out = pltpu.roll(z, tm - k, 0)
    for j in range(k):
        out = jnp.where(row == tm - k + j, halo_rows[j:j + 1, :], out)
    return out


def _sconv_specs(t, tm, cols):
    per = tm // HALO
    last = t // HALO - 1
    cur = pl.BlockSpec((tm, cols), lambda i: (i, 0))
    prev = pl.BlockSpec((HALO, cols), lambda i: (jnp.maximum(i * per - 1, 0), 0))
    nxt = pl.BlockSpec((HALO, cols), lambda i: (jnp.minimum((i + 1) * per, last), 0))
    return cur, prev, nxt


def _sconv_fwd(bcu, conv_w, name):
    t, d3 = bcu.shape
    d = d3 // 3
    tm = _div_tile(t, 256, 16)
    cur, prev, _ = _sconv_specs(t, tm, d3)

    def body(cur_ref, prev_ref, w_ref, o_ref):
        i = pl.program_id(0)
        z = cur_ref[:, d:2 * d] * cur_ref[:, 2 * d:]
        zp = prev_ref[:, d:2 * d] * prev_ref[:, 2 * d:] * (i > 0).astype(F32)
        y = w_ref[0:1, :] * _shift_down(z, 2, zp) + w_ref[1:2, :] * _shift_down(z, 1, zp) + w_ref[2:3, :] * z
        o_ref[...] = (cur_ref[:, :d] * y).astype(BF16)

    return pl.pallas_call(
        body, name=name, out_shape=jax.ShapeDtypeStruct((t, d), BF16), grid=(t // tm,),
        in_specs=[cur, prev, _const_spec((3, d))], out_specs=_row_spec(tm, d),
        compiler_params=_params(("parallel",)),
    )(bcu, bcu, conv_w)


def _sconv_bwd(bcu, dout, conv_w, name):
    t, d3 = bcu.shape
    d = d3 // 3
    tm = _div_tile(t, 256, 16)
    cur, prev, nxt = _sconv_specs(t, tm, d3)
    dcur, _, dnxt = _sconv_specs(t, tm, d)
    n_tiles = t // tm

    def body(cur_ref, prev_ref, nxt_ref, do_ref, don_ref, w_ref, o_ref, dw_ref):
        i = pl.program_id(0)
        b, cg, u = cur_ref[:, :d], cur_ref[:, d:2 * d], cur_ref[:, 2 * d:]
        z = cg * u
        zp = prev_ref[:, d:2 * d] * prev_ref[:, 2 * d:] * (i > 0).astype(F32)
        z1, z2 = _shift_down(z, 1, zp), _shift_down(z, 2, zp)
        w0, w1, w2 = w_ref[0:1, :], w_ref[1:2, :], w_ref[2:3, :]
        y = w0 * z2 + w1 * z1 + w2 * z
        dov = do_ref[...]
        dy = dov * b
        dyn = don_ref[...] * nxt_ref[:, :d] * (i < n_tiles - 1).astype(F32)
        dz = w2 * dy + w1 * _shift_up(dy, 1, dyn) + w0 * _shift_up(dy, 2, dyn)
        o_ref[:, :d] = (dov * y).astype(BF16)
        o_ref[:, d:2 * d] = (dz * u).astype(BF16)
        o_ref[:, 2 * d:] = (dz * cg).astype(BF16)
        dw = jnp.concatenate([jnp.sum(dy * z2, axis=0, keepdims=True), jnp.sum(dy * z1, axis=0, keepdims=True),
                              jnp.sum(dy * z, axis=0, keepdims=True)], axis=0)
        _accumulate(dw_ref, dw, i)

    return pl.pallas_call(
        body, name=name,
        out_shape=(jax.ShapeDtypeStruct((t, d3), BF16), jax.ShapeDtypeStruct((3, d), F32)), grid=(n_tiles,),
        in_specs=[cur, prev, nxt, dcur, dnxt, _const_spec((3, d))],
        out_specs=(_row_spec(tm, d3), _const_spec((3, d))),
        compiler_params=_params(("arbitrary",)),
    )(bcu, bcu, bcu, dout, dout, conv_w)


def _rope_tables(positions, rot, lead, trail):
    inv_freq = ROPE_THETA ** (-jnp.arange(0, rot, 2, dtype=F32) / rot)
    ang = positions.astype(F32)[:, None] * inv_freq
    cos, sin = jnp.cos(ang), jnp.sin(ang)
    t = positions.shape[0]
    cos_full = jnp.concatenate([jnp.ones((t, lead), F32), cos, cos, jnp.ones((t, trail), F32)], axis=1)
    sin_full = jnp.concatenate([jnp.zeros((t, lead), F32), -sin, sin, jnp.zeros((t, trail), F32)], axis=1)
    return cos_full, sin_full


def _repeat_lanes(x, reps, name):
    t, d = x.shape
    tm = _div_tile(t, 512, 8)

    def body(x_ref, o_ref):
        o_ref[...] = jnp.concatenate([x_ref[...]] * reps, axis=1)

    return pl.pallas_call(
        body, name=name, out_shape=jax.ShapeDtypeStruct((t, reps * d), x.dtype), grid=(t // tm,),
        in_specs=[_row_spec(tm, d)], out_specs=_row_spec(tm, reps * d), compiler_params=_params(("parallel",)),
    )(x)


def _pieces_dot(a, b, pieces):
    total, rest = None, a
    for _ in range(pieces):
        piece = rest.astype(BF16)
        term = jnp.dot(piece, b, preferred_element_type=F32)
        total = term if total is None else total + term
        rest = rest - piece.astype(F32)
    return total


A_Q_COLS = A_HEADS * A_HEAD_DIM
A_KV_COLS = A_KV_HEADS * A_HEAD_DIM
A_COLS = A_Q_COLS + 2 * A_KV_COLS
A_SCALE = A_HEAD_DIM ** -0.5


A_NORMED = A_Q_COLS + A_KV_COLS


def _swa_prep_tables(q_norm, k_norm):
    lane = np.arange(A_NORMED)
    seg = (lane[:, None] // A_HEAD_DIM == np.arange(LANES)[None, :]).astype(np.float32)
    fold = (np.where(lane < A_Q_COLS, 0, A_HEAD_DIM)[:, None] + lane[:, None] % A_HEAD_DIM
            == np.arange(LANES)[None, :]).astype(np.float32)
    gains = jnp.concatenate([jnp.tile(q_norm, (1, A_HEADS)), jnp.tile(k_norm, (1, A_KV_HEADS))], axis=1)
    return gains, jnp.asarray(seg, BF16), jnp.asarray(seg.T, BF16), jnp.asarray(fold, BF16)


def _wide_rstd(x, seg, seg_t):
    mean_sq = _pieces_dot(x * x, seg, 1) * (1.0 / A_HEAD_DIM)
    return _pieces_dot(lax.rsqrt(mean_sq + EPS), seg_t, 2)


def _wide_partner(n):
    dim = lax.broadcasted_iota(jnp.int32, n.shape, 1) & (A_HEAD_DIM - 1)
    half = A_ROT_DIM // 2
    return jnp.where(dim < half, pltpu.roll(n, A_NORMED - half, 1),
                     jnp.where(dim < A_ROT_DIM, pltpu.roll(n, half, 1), 0.0))


def _swa_prep_fwd(qkv, q_norm, k_norm, cos_w, sin_w, name):
    t = qkv.shape[0]
    tm = _div_tile(t, 256, 16)
    gains, seg, seg_t, _ = _swa_prep_tables(q_norm, k_norm)

    def body(x_ref, g_ref, cos_ref, sin_ref, seg_ref, segt_ref, o_ref):
        x = x_ref[:, :A_NORMED]
        n = x * _wide_rstd(x, seg_ref[...], segt_ref[...]) * g_ref[...]
        o_ref[:, :A_NORMED] = (n * cos_ref[...] + _wide_partner(n) * sin_ref[...]).astype(BF16)
        o_ref[:, A_NORMED:] = x_ref[:, A_NORMED:].astype(BF16)

    return pl.pallas_call(
        body, name=name, out_shape=jax.ShapeDtypeStruct((t, A_COLS), BF16), grid=(t // tm,),
        in_specs=[_row_spec(tm, A_COLS), _const_spec((1, A_NORMED)), _row_spec(tm, A_NORMED), _row_spec(tm, A_NORMED),
                  _const_spec(seg.shape), _const_spec(seg_t.shape)],
        out_specs=_row_spec(tm, A_COLS), compiler_params=_params(("parallel",)),
    )(qkv, gains, cos_w, sin_w, seg, seg_t)


def _swa_prep_bwd(qkv, dqkv_r, q_norm, k_norm, cos_w, sin_w, name):
    t = qkv.shape[0]
    tm = _div_tile(t, 256, 16)
    hd = A_HEAD_DIM
    gains, seg, seg_t, fold = _swa_prep_tables(q_norm, k_norm)

    def body(x_ref, d_ref, g_ref, cos_ref, sin_ref, seg_ref, segt_ref, fold_ref, o_ref, dgq_ref, dgk_ref):
        x, dout = x_ref[:, :A_NORMED], d_ref[:, :A_NORMED]
        rstd = _wide_rstd(x, seg_ref[...], segt_ref[...])
        xn = x * rstd
        dn = dout * cos_ref[...] + _wide_partner(dout * sin_ref[...])
        dg = _pieces_dot(jnp.sum(dn * xn, axis=0, keepdims=True), fold_ref[...], 3)
        dxn = dn * g_ref[...]
        mean = _pieces_dot(_pieces_dot(dxn * xn, seg_ref[...], 1) * (1.0 / hd), segt_ref[...], 2)
        o_ref[:, :A_NORMED] = (rstd * (dxn - xn * mean)).astype(BF16)
        o_ref[:, A_NORMED:] = d_ref[:, A_NORMED:].astype(BF16)
        _accumulate(dgq_ref, dg[:, :hd], pl.program_id(0))
        _accumulate(dgk_ref, dg[:, hd:2 * hd], pl.program_id(0))

    return pl.pallas_call(
        body, name=name,
        out_shape=(jax.ShapeDtypeStruct((t, A_COLS), BF16), jax.ShapeDtypeStruct((1, hd), F32),
                   jax.ShapeDtypeStruct((1, hd), F32)),
        grid=(t // tm,),
        in_specs=[_row_spec(tm, A_COLS), _row_spec(tm, A_COLS), _const_spec((1, A_NORMED)), _row_spec(tm, A_NORMED),
                  _row_spec(tm, A_NORMED), _const_spec(seg.shape), _const_spec(seg_t.shape), _const_spec(fold.shape)],
        out_specs=(_row_spec(tm, A_COLS), _const_spec((1, hd)), _const_spec((1, hd))),
        compiler_params=_params(("arbitrary",)),
    )(qkv, dqkv_r, gains, cos_w, sin_w, seg, seg_t, fold)


def _group_rows(ref, k, width=A_HEAD_DIM, base=0):
    return jnp.concatenate([ref[:, base + (A_GROUP * k + g) * width:base + (A_GROUP * k + g + 1) * width]
                            for g in range(A_GROUP)], axis=0)


def _group_column(ref, k, rows):
    cols = []
    for g in range(A_GROUP):
        h = A_GROUP * k + g
        col = ref[:, h:h + 1]
        cols.append(jnp.broadcast_to(col, (rows, 1)) if col.shape[0] == 1 else col)
    return jnp.concatenate(cols, axis=0)


def _swa_fwd(qkv_r, sinks, name, rider=None):
    t = qkv_r.shape[0]
    blk = A_WINDOW
    nb = t // blk
    hd = A_HEAD_DIM
    kv_block = A_Q_COLS // (2 * A_KV_COLS)

    def body(q_ref, kvc_ref, kvp_ref, s_ref, o_ref, lse_ref):
        n = pl.program_id(0)
        shape = (A_GROUP * blk, 2 * blk)
        qpos = lax.broadcasted_iota(jnp.int32, shape, 0) & (blk - 1)
        col = lax.broadcasted_iota(jnp.int32, shape, 1)
        delta = qpos + blk - col
        valid = (delta >= 0) & (delta < A_WINDOW) & ((col >= blk) | (n > 0))
        for k in range(A_KV_HEADS):
            qg = _group_rows(q_ref, k)
            kw = jnp.concatenate([kvp_ref[:, k * hd:(k + 1) * hd], kvc_ref[:, k * hd:(k + 1) * hd]], axis=0)
            vw = jnp.concatenate([kvp_ref[:, A_KV_COLS + k * hd:A_KV_COLS + (k + 1) * hd],
                                  kvc_ref[:, A_KV_COLS + k * hd:A_KV_COLS + (k + 1) * hd]], axis=0)
            s = lax.dot_general(qg, kw, (((1,), (1,)), ((), ())), preferred_element_type=F32) * A_SCALE
            s = jnp.where(valid, s, NEG)
            sink = _group_column(s_ref, k, blk)
            m = jnp.maximum(jnp.max(s, axis=-1, keepdims=True), sink)
            p = jnp.exp(s - m)
            denom = jnp.sum(p, axis=-1, keepdims=True) + jnp.exp(sink - m)
            o = jnp.dot(p.astype(BF16), vw, preferred_element_type=F32) / denom
            lse = m + jnp.log(denom)
            for g in range(A_GROUP):
                h = A_GROUP * k + g
                o_ref[:, h * hd:(h + 1) * hd] = o[g * blk:(g + 1) * blk].astype(BF16)
                lse_ref[:, h:h + 1] = lse[g * blk:(g + 1) * blk]

    return _host_call(
        body, rider, name,
        out_shape=(jax.ShapeDtypeStruct((t, A_Q_COLS), BF16), jax.ShapeDtypeStruct((t, A_HEADS), F32)), grid=(nb,),
        in_specs=[pl.BlockSpec((blk, A_Q_COLS), lambda n: (n, 0)),
                  pl.BlockSpec((blk, 2 * A_KV_COLS), lambda n: (n, kv_block)),
                  pl.BlockSpec((blk, 2 * A_KV_COLS), lambda n: (jnp.maximum(n - 1, 0), kv_block)),
                  _const_spec((1, A_HEADS))],
        out_specs=(pl.BlockSpec((blk, A_Q_COLS), lambda n: (n, 0)), pl.BlockSpec((blk, A_HEADS), lambda n: (n, 0))),
        operands=(qkv_r, qkv_r, qkv_r, sinks), semantics=("parallel",))


def _swa_bwd(qkv_r, o, lse, do, sinks, name, rider=None):
    t = qkv_r.shape[0]
    blk = A_WINDOW
    nb = t // blk
    hd = A_HEAD_DIM
    kv_block = A_Q_COLS // (2 * A_KV_COLS)
    rows = A_GROUP * blk

    def nxt(n):
        return jnp.minimum(n + 1, nb - 1)

    def body(qc_ref, qn_ref, kvc_ref, kvp_ref, doc_ref, don_ref, oc_ref, on_ref, lc_ref, ln_ref, s_ref, dx_ref, ds_ref):
        n = pl.program_id(0)
        shape = (2 * rows, 2 * blk)
        row = lax.broadcasted_iota(jnp.int32, shape, 0)
        col = lax.broadcasted_iota(jnp.int32, shape, 1)
        is_next = row >= rows
        delta = jnp.where(is_next, blk, 0) + blk + (row & (blk - 1)) - col
        valid = ((delta >= 0) & (delta < A_WINDOW) & ((col >= blk) | (n > 0)) & (jnp.logical_not(is_next) | (n < nb - 1)))
        dsink_cols = []
        for k in range(A_KV_HEADS):
            qs = jnp.concatenate([_group_rows(qc_ref, k), _group_rows(qn_ref, k)], axis=0)
            dos = jnp.concatenate([_group_rows(doc_ref, k), _group_rows(don_ref, k)], axis=0)
            os_ = jnp.concatenate([_group_rows(oc_ref, k), _group_rows(on_ref, k)], axis=0).astype(F32)
            lses = jnp.concatenate([_group_column(lc_ref, k, blk), _group_column(ln_ref, k, blk)], axis=0)
            kw = jnp.concatenate([kvp_ref[:, k * hd:(k + 1) * hd], kvc_ref[:, k * hd:(k + 1) * hd]], axis=0)
            vw = jnp.concatenate([kvp_ref[:, A_KV_COLS + k * hd:A_KV_COLS + (k + 1) * hd],
                                  kvc_ref[:, A_KV_COLS + k * hd:A_KV_COLS + (k + 1) * hd]], axis=0)
            s = lax.dot_general(qs, kw, (((1,), (1,)), ((), ())), preferred_element_type=F32) * A_SCALE
            p = jnp.exp(jnp.where(valid, s - lses, NEG))
            dos_b = dos.astype(BF16)
            dp = lax.dot_general(dos_b, vw, (((1,), (1,)), ((), ())), preferred_element_type=F32)
            dlt = jnp.sum(dos * os_, axis=-1, keepdims=True)
            ds = p * (dp - dlt)
            dq = jnp.dot(ds[:rows].astype(BF16), kw, preferred_element_type=F32) * A_SCALE
            dk = lax.dot_general(ds[:, blk:].astype(BF16), qs, (((0,), (0,)), ((), ())), preferred_element_type=F32) * A_SCALE
            dv = lax.dot_general(p[:, blk:].astype(BF16), dos_b, (((0,), (0,)), ((), ())), preferred_element_type=F32)
            for g in range(A_GROUP):
                h = A_GROUP * k + g
                dx_ref[:, h * hd:(h + 1) * hd] = dq[g * blk:(g + 1) * blk]
            dx_ref[:, A_Q_COLS + k * hd:A_Q_COLS + (k + 1) * hd] = dk
            dx_ref[:, A_Q_COLS + A_KV_COLS + k * hd:A_Q_COLS + A_KV_COLS + (k + 1) * hd] = dv
            sink = _group_column(s_ref, k, blk)
            contrib = -jnp.exp(sink - lses[:rows]) * dlt[:rows]
            for g in range(A_GROUP):
                dsink_cols.append(jnp.sum(contrib[g * blk:(g + 1) * blk], axis=0, keepdims=True))
        _accumulate(ds_ref, jnp.concatenate(dsink_cols, axis=1), n)

    q_spec = lambda f: pl.BlockSpec((blk, A_Q_COLS), lambda n: (f(n), 0))
    l_spec = lambda f: pl.BlockSpec((blk, A_HEADS), lambda n: (f(n), 0))
    same = lambda n: n
    return _host_call(
        body, rider, name,
        out_shape=(jax.ShapeDtypeStruct((t, A_COLS), F32), jax.ShapeDtypeStruct((1, A_HEADS), F32)), grid=(nb,),
        in_specs=[q_spec(same), q_spec(nxt),
                  pl.BlockSpec((blk, 2 * A_KV_COLS), lambda n: (n, kv_block)),
                  pl.BlockSpec((blk, 2 * A_KV_COLS), lambda n: (jnp.maximum(n - 1, 0), kv_block)),
                  q_spec(same), q_spec(nxt), q_spec(same), q_spec(nxt), l_spec(same), l_spec(nxt),
                  _const_spec((1, A_HEADS))],
        out_specs=(pl.BlockSpec((blk, A_COLS), lambda n: (n, 0)), _const_spec((1, A_HEADS))),
        operands=(qkv_r, qkv_r, qkv_r, qkv_r, do, do, o, o, lse, lse, sinks), semantics=("arbitrary",))


C_DOWN_COLS = C_Q_RANK + C_KV_RANK + C_ROPE
C_Q_COLS = C_HEADS * C_QK
C_KV_COLS = C_HEADS * (C_NOPE + C_V)
C_O_COLS = C_HEADS * C_V
C_PAD = LANES
C_SCALE = C_QK ** -0.5
LOG2E = 1.4426950408889634
LN2 = 0.6931471805599453
C_Q_SCALE = C_SCALE * LOG2E
C_PAIR = 2


def _mla_latent_fwd(down, q_a_norm, kv_a_norm, name):
    t = down.shape[0]
    tm = _div_tile(t, 512, 16)

    def body(x_ref, gq_ref, gk_ref, cq_ref, ckv_ref):
        cq, ckv = x_ref[:, :C_Q_RANK], x_ref[:, C_Q_RANK:C_Q_RANK + C_KV_RANK]
        cq_ref[...] = (cq * _rstd(cq) * gq_ref[...]).astype(BF16)
        ckv_ref[...] = (ckv * _rstd(ckv) * gk_ref[...]).astype(BF16)

    return pl.pallas_call(
        body, name=name,
        out_shape=(jax.ShapeDtypeStruct((t, C_Q_RANK), BF16), jax.ShapeDtypeStruct((t, C_KV_RANK), BF16)), grid=(t // tm,),
        in_specs=[_row_spec(tm, C_DOWN_COLS), _const_spec((1, C_Q_RANK)), _const_spec((1, C_KV_RANK))],
        out_specs=(_row_spec(tm, C_Q_RANK), _row_spec(tm, C_KV_RANK)), compiler_params=_params(("parallel",)),
    )(down, q_a_norm, kv_a_norm)


def _mla_latent_bwd(down, dcq, dckv, dkrope, q_a_norm, kv_a_norm, name):
    t = down.shape[0]
    tm = _div_tile(t, 512, 16)

    def body(x_ref, dcq_ref, dckv_ref, dkr_ref, gq_ref, gk_ref, o_ref, dgq_ref, dgk_ref):
        dq, dgq = _norm_bwd(x_ref[:, :C_Q_RANK], gq_ref[...], dcq_ref[...])
        dkv, dgk = _norm_bwd(x_ref[:, C_Q_RANK:C_Q_RANK + C_KV_RANK], gk_ref[...], dckv_ref[...])
        o_ref[...] = jnp.concatenate([dq, dkv, dkr_ref[...]], axis=1).astype(BF16)
        _accumulate(dgq_ref, dgq, pl.program_id(0))
        _accumulate(dgk_ref, dgk, pl.program_id(0))

    return pl.pallas_call(
        body, name=name,
        out_shape=(jax.ShapeDtypeStruct((t, C_DOWN_COLS), BF16), jax.ShapeDtypeStruct((1, C_Q_RANK), F32),
                   jax.ShapeDtypeStruct((1, C_KV_RANK), F32)),
        grid=(t // tm,),
        in_specs=[_row_spec(tm, C_DOWN_COLS), _row_spec(tm, C_Q_RANK), _row_spec(tm, C_KV_RANK), _row_spec(tm, C_ROPE),
                  _const_spec((1, C_Q_RANK)), _const_spec((1, C_KV_RANK))],
        out_specs=(_row_spec(tm, C_DOWN_COLS), _const_spec((1, C_Q_RANK)), _const_spec((1, C_KV_RANK))),
        compiler_params=_params(("arbitrary",)),
    )(down, dcq, dckv, dkrope, q_a_norm, kv_a_norm)


def _head_major_spec(tm, width):
    return pl.BlockSpec((C_HEADS, tm, width), lambda i: (0, i, 0))


C_NOPE_V = C_NOPE + C_V
C_ROPE_COLS = C_HEADS * C_ROPE


def _mla_prep_tables(q_norm, k_norm):
    ql, kl, rl = np.arange(C_Q_COLS), np.arange(C_KV_COLS), np.arange(C_ROPE_COLS)
    col = np.arange(LANES)[None, :]
    is_nope = (kl % C_NOPE_V) < C_NOPE
    one_hot = lambda m: jnp.asarray(m.astype(np.float32), BF16)
    gk_nope = jnp.concatenate([k_norm[:, :C_NOPE], jnp.zeros((1, C_V), F32)], axis=1)
    return dict(
        q_dim=jnp.asarray((ql % C_QK)[None, :], jnp.int32),
        q_gain=jnp.tile(q_norm, (1, C_HEADS)),
        q_seg=one_hot(ql[:, None] // C_QK == col), q_spread=one_hot((ql[:, None] // C_QK == col).T),
        q_fold=one_hot(ql[:, None] % C_QK == col),
        k_nope=jnp.asarray(is_nope[None, :].astype(np.float32)),
        k_gain=jnp.tile(gk_nope, (1, C_HEADS)),
        k_seg=one_hot(is_nope[:, None] & (kl[:, None] // C_NOPE_V == col)),
        k_spread=one_hot((kl[:, None] // C_NOPE_V == col).T),
        k_fold=one_hot(is_nope[:, None] & (kl[:, None] % C_NOPE_V == col)),
        r_gain=jnp.tile(k_norm[:, C_NOPE:], (1, C_HEADS)),
        r_rep=one_hot(np.arange(C_ROPE)[:, None] == rl[None, :] % C_ROPE),
        r_seg=one_hot(rl[:, None] // C_ROPE == col), r_spread=one_hot((rl[:, None] // C_ROPE == col).T),
        r_fold=one_hot(rl[:, None] % C_ROPE == col))


def _q_partner(n, dim):
    half = C_ROPE // 2
    return jnp.where((dim >= C_NOPE) & (dim < C_NOPE + half), pltpu.roll(n, C_Q_COLS - half, 1),
                     jnp.where(dim >= C_NOPE + half, pltpu.roll(n, half, 1), 0.0))


def _rope_partner(n):
    half = C_ROPE // 2
    dim = lax.broadcasted_iota(jnp.int32, n.shape, 1) & (C_ROPE - 1)
    return jnp.where(dim < half, pltpu.roll(n, C_ROPE_COLS - half, 1), pltpu.roll(n, half, 1))


def _head_rstd(sum_sq):
    return lax.rsqrt(sum_sq * (1.0 / C_QK) + EPS)


MLA_PREP_FWD_TABLES = ['q_dim', 'q_gain', 'q_seg', 'q_spread', 'k_gain', 'k_seg', 'k_spread', 'r_gain', 'r_rep', 'r_spread']
MLA_PREP_BWD_TABLES = MLA_PREP_FWD_TABLES + ['q_fold', 'k_nope', 'k_fold', 'r_seg', 'r_fold']


def _mla_qk_fwd(qw, kvw, down, q_norm, k_norm, rows, name):
    t = qw.shape[0]
    tm = _div_tile(t, 256, 16)
    tables = _mla_prep_tables(q_norm, k_norm)
    consts = [tables[n] for n in MLA_PREP_FWD_TABLES]

    def body(q_ref, kv_ref, dn_ref, qcos_ref, qsin_ref, rcos_ref, rsin_ref, *rest):
        c = {n: r[...] for n, r in zip(MLA_PREP_FWD_TABLES, rest)}
        qo_ref, ko_ref, vo_ref = rest[len(MLA_PREP_FWD_TABLES):]
        q, kv, kr = q_ref[...], kv_ref[...], dn_ref[:, C_Q_RANK + C_KV_RANK:]
        nq = q * _pieces_dot(_head_rstd(_pieces_dot(q * q, c['q_seg'], 1)), c['q_spread'], 2) * c['q_gain']
        out_q = (nq * qcos_ref[...] + _q_partner(nq, c['q_dim']) * qsin_ref[...]) * C_Q_SCALE
        rstd = _head_rstd(_pieces_dot(kv * kv, c['k_seg'], 1) + jnp.sum(kr * kr, axis=-1, keepdims=True))
        nope = kv * _pieces_dot(rstd, c['k_spread'], 2) * c['k_gain']
        nr = _pieces_dot(kr, c['r_rep'], 2) * _pieces_dot(rstd, c['r_spread'], 2) * c['r_gain']
        rope = nr * rcos_ref[...] + _rope_partner(nr) * rsin_ref[...]
        pad = jnp.zeros((tm, C_PAD - C_QK), F32)
        one_then_zeros = (lax.broadcasted_iota(jnp.int32, (tm, C_PAD - C_V), 1) == 0).astype(F32)
        for h in range(C_HEADS):
            qo_ref[h] = jnp.concatenate([out_q[:, h * C_QK:(h + 1) * C_QK], pad], axis=1).astype(BF16)
            ko_ref[h] = jnp.concatenate([nope[:, h * C_NOPE_V:h * C_NOPE_V + C_NOPE], rope[:, h * C_ROPE:(h + 1) * C_ROPE],
                                         pad], axis=1).astype(BF16)
            vo_ref[h] = jnp.concatenate([kv[:, h * C_NOPE_V + C_NOPE:(h + 1) * C_NOPE_V], one_then_zeros],
                                        axis=1).astype(BF16)

    return pl.pallas_call(
        body, name=name,
        out_shape=(jax.ShapeDtypeStruct((C_HEADS, t, C_PAD), BF16), jax.ShapeDtypeStruct((C_HEADS, t, C_PAD), BF16),
                   jax.ShapeDtypeStruct((C_HEADS, t, C_PAD), BF16)),
        grid=(t // tm,),
        in_specs=[_row_spec(tm, C_Q_COLS), _row_spec(tm, C_KV_COLS), _row_spec(tm, C_DOWN_COLS)]
                 + [_row_spec(tm, r.shape[1]) for r in rows] + [_const_spec(a.shape) for a in consts],
        out_specs=(_head_major_spec(tm, C_PAD), _head_major_spec(tm, C_PAD), _head_major_spec(tm, C_PAD)),
        compiler_params=_params(("parallel",)),
    )(qw, kvw, down, *rows, *consts)


def _mla_qk_bwd(qw, kvw, down, dq, dk, dv, q_norm, k_norm, rows, name, rider=None):
    t = qw.shape[0]
    tm = _div_tile(t, 256, 16)
    tables = _mla_prep_tables(q_norm, k_norm)
    consts = [tables[n] for n in MLA_PREP_BWD_TABLES]

    def body(q_ref, kv_ref, dn_ref, dq_ref, dk_ref, dv_ref, qcos_ref, qsin_ref, rcos_ref, rsin_ref, *rest):
        c = {n: r[...] for n, r in zip(MLA_PREP_BWD_TABLES, rest)}
        dqw_ref, dkvw_ref, dkr_ref, dgq_ref, dgk_ref = rest[len(MLA_PREP_BWD_TABLES):]
        step = pl.program_id(0)
        q, kv, kr = q_ref[...], kv_ref[...], dn_ref[:, C_Q_RANK + C_KV_RANK:]
        dout_q = jnp.concatenate([dq_ref[h][:, :C_QK] for h in range(C_HEADS)], axis=1)
        d_slab = jnp.concatenate([x for h in range(C_HEADS) for x in (dk_ref[h][:, :C_NOPE], dv_ref[h])], axis=1)
        d_rope = jnp.concatenate([dk_ref[h][:, C_NOPE:C_QK] for h in range(C_HEADS)], axis=1)

        rq = _pieces_dot(_head_rstd(_pieces_dot(q * q, c['q_seg'], 1)), c['q_spread'], 2)
        xq = q * rq
        dnq = dout_q * qcos_ref[...] + _q_partner(dout_q * qsin_ref[...], c['q_dim'])
        dgq = _pieces_dot(jnp.sum(dnq * xq, axis=0, keepdims=True), c['q_fold'], 3)
        dxq = dnq * c['q_gain']
        mean_q = _pieces_dot(_pieces_dot(dxq * xq, c['q_seg'], 1) * (1.0 / C_QK), c['q_spread'], 2)
        dqw_ref[...] = (rq * (dxq - xq * mean_q)).astype(BF16)

        rstd = _head_rstd(_pieces_dot(kv * kv, c['k_seg'], 1) + jnp.sum(kr * kr, axis=-1, keepdims=True))
        r_nope, r_rope = _pieces_dot(rstd, c['k_spread'], 2), _pieces_dot(rstd, c['r_spread'], 2)
        x_nope = kv * r_nope * c['k_nope']
        x_rope = _pieces_dot(kr, c['r_rep'], 2) * r_rope
        dn_nope = d_slab * c['k_nope']
        dn_rope = d_rope * rcos_ref[...] + _rope_partner(d_rope * rsin_ref[...])
        dg_nope = _pieces_dot(jnp.sum(dn_nope * x_nope, axis=0, keepdims=True), c['k_fold'], 3)
        dg_rope = _pieces_dot(jnp.sum(dn_rope * x_rope, axis=0, keepdims=True), c['r_fold'], 3)
        dx_nope, dx_rope = dn_nope * c['k_gain'], dn_rope * c['r_gain']
        mean = (_pieces_dot(dx_nope * x_nope, c['k_seg'], 1) + _pieces_dot(dx_rope * x_rope, c['r_seg'], 1)) * (1.0 / C_QK)
        g_nope = r_nope * (dx_nope - x_nope * _pieces_dot(mean, c['k_spread'], 2))
        g_rope = r_rope * (dx_rope - x_rope * _pieces_dot(mean, c['r_spread'], 2))
        dkvw_ref[...] = jnp.where(c['k_nope'] > 0.0, g_nope, d_slab).astype(BF16)
        dkr_ref[...] = _pieces_dot(g_rope, c['r_fold'], 3)[:, :C_ROPE]
        _accumulate(dgq_ref, dgq[:, :C_QK], step)
        _accumulate(dgk_ref, jnp.concatenate([dg_nope[:, :C_NOPE], dg_rope[:, :C_ROPE]], axis=1), step)

    return _host_call(
        body, rider, name,
        out_shape=(jax.ShapeDtypeStruct((t, C_Q_COLS), BF16), jax.ShapeDtypeStruct((t, C_KV_COLS), BF16),
                   jax.ShapeDtypeStruct((t, C_ROPE), F32), jax.ShapeDtypeStruct((1, C_QK), F32),
                   jax.ShapeDtypeStruct((1, C_QK), F32)),
        grid=(t // tm,),
        in_specs=[_row_spec(tm, C_Q_COLS), _row_spec(tm, C_KV_COLS), _row_spec(tm, C_DOWN_COLS),
                  _head_major_spec(tm, C_PAD), _head_major_spec(tm, C_PAD), _head_major_spec(tm, C_V)]
                 + [_row_spec(tm, r.shape[1]) for r in rows] + [_const_spec(a.shape) for a in consts],
        out_specs=(_row_spec(tm, C_Q_COLS), _row_spec(tm, C_KV_COLS), _row_spec(tm, C_ROPE), _const_spec((1, C_QK)),
                   _const_spec((1, C_QK))),
        operands=(qw, kvw, down, dq, dk, dv, *rows, *consts), semantics=("arbitrary",))


def _causal_keep(rows, cols, row_offset=0, transposed=False):
    row = lax.broadcasted_iota(jnp.int32, (rows, cols), 0) + row_offset
    col = lax.broadcasted_iota(jnp.int32, (rows, cols), 1)
    return (row <= col) if transposed else (col <= row)


def _mla_fwd(q, k, v, name):
    _, t, _ = q.shape
    bq, bk = min(MLA_FWD_Q_BLOCK, t), min(MLA_FWD_K_BLOCK, t)
    nq = t // bq

    def body(q_ref, k_ref, v_ref, o_ref, lse_ref, m_sc, acc_sc):
        qi = pl.program_id(1)
        m_sc[...] = jnp.full_like(m_sc, NEG)
        acc_sc[...] = jnp.zeros_like(acc_sc)
        diagonal = (qi * bq) // bk
        lead = qi * bq - diagonal * bk

        def step(ki, masked):
            rows = pl.ds(pl.multiple_of(ki * bk, bk), bk)
            for hh in range(C_PAIR):
                s = lax.dot_general(q_ref[hh], k_ref[hh, rows, :], (((1,), (1,)), ((), ())), preferred_element_type=F32)
                if masked:
                    s = jnp.where(_causal_keep(bq, bk, lead), s, NEG)
                m_prev = m_sc[hh]
                m_new = jnp.maximum(m_prev, jnp.max(s, axis=-1, keepdims=True))
                p = jnp.exp2(s - m_new)
                acc_sc[hh] = jnp.exp2(m_prev - m_new) * acc_sc[hh] + jnp.dot(p.astype(BF16), v_ref[hh, rows, :],
                                                                                preferred_element_type=F32)
                m_sc[hh] = m_new

        def below_diagonal(ki, carry):
            step(ki, False)
            return carry

        lax.fori_loop(0, diagonal, below_diagonal, 0)
        step(diagonal, True)
        outs = []
        for hh in range(C_PAIR):
            denom = acc_sc[hh, :, C_V:C_V + 1]
            outs.append(acc_sc[hh, :, :C_V] / denom)
            lse_ref[hh] = m_sc[hh] + jnp.log(denom) * LOG2E
        o_ref[...] = jnp.concatenate(outs, axis=1).astype(BF16)

    whole = lambda hp, qi: (hp, 0, 0)
    return pl.pallas_call(
        body, name=name,
        out_shape=(jax.ShapeDtypeStruct((t, C_O_COLS), BF16), jax.ShapeDtypeStruct((C_HEADS, t, 1), F32)),
        grid=(C_HEADS // C_PAIR, nq),
        in_specs=[pl.BlockSpec((C_PAIR, bq, C_PAD), lambda hp, qi: (hp, qi, 0)),
                  pl.BlockSpec((C_PAIR, t, C_PAD), whole, pipeline_mode=pl.Buffered(1)),
                  pl.BlockSpec((C_PAIR, t, C_PAD), whole, pipeline_mode=pl.Buffered(1))],
        out_specs=(pl.BlockSpec((bq, C_PAIR * C_V), lambda hp, qi: (qi, hp)),
                   pl.BlockSpec((C_PAIR, bq, 1), lambda hp, qi: (hp, qi, 0))),
        scratch_shapes=[pltpu.VMEM((C_PAIR, bq, 1), F32), pltpu.VMEM((C_PAIR, bq, C_PAD), F32)],
        compiler_params=_params(("parallel", "arbitrary")),
    )(q, k, v)


def _mla_delta(do, o, name):
    t = do.shape[0]
    blk = min(MLA_BLOCK, t)

    def body(do_ref, o_ref, dlt_ref, dob_ref):
        for hh in range(C_PAIR):
            do_h = do_ref[:, hh * C_V:(hh + 1) * C_V]
            dlt_ref[hh] = jnp.sum(do_h * o_ref[:, hh * C_V:(hh + 1) * C_V].astype(F32), axis=-1, keepdims=True)
        dob_ref[...] = do_ref[...].astype(BF16)

    wide = pl.BlockSpec((blk, C_PAIR * C_V), lambda hp, i: (i, hp))
    return pl.pallas_call(
        body, name=name,
        out_shape=(jax.ShapeDtypeStruct((C_HEADS, t, 1), F32), jax.ShapeDtypeStruct(do.shape, BF16)),
        grid=(C_HEADS // C_PAIR, t // blk), in_specs=[wide, wide],
        out_specs=(pl.BlockSpec((C_PAIR, blk, 1), lambda hp, i: (hp, i, 0)), wide),
        compiler_params=_params(("parallel", "parallel")),
    )(do, o)


def _mla_bwd(q, k, v, do_b, lse_rows, dlt_rows, name):
    _, t, _ = q.shape
    blk = min(MLA_BLOCK, t)
    nq = t // blk

    def body(q_ref, k_ref, v_ref, do_ref, lse_ref, dlt_ref, dq_hbm, dk_ref, dv_ref, dq_sc, dk_sc, dv_sc, sem):
        hp, ki = pl.program_id(0), pl.program_id(1)

        @pl.when(ki == 0)
        def _():
            dq_sc[...] = jnp.zeros_like(dq_sc)

        dk_sc[...] = jnp.zeros_like(dk_sc)
        dv_sc[...] = jnp.zeros_like(dv_sc)

        def step(qi, masked):
            rows = pl.ds(pl.multiple_of(qi * blk, blk), blk)
            for hh in range(C_PAIR):
                qb = q_ref[hh, rows, :]
                dob = do_ref[rows, hh * C_V:(hh + 1) * C_V]
                s = lax.dot_general(k_ref[hh], qb, (((1,), (1,)), ((), ())), preferred_element_type=F32)
                if masked:
                    s = jnp.where(_causal_keep(blk, blk, transposed=True), s, NEG)
                p = jnp.exp2(s - lse_ref[hh, qi])
                dp = lax.dot_general(v_ref[hh, :, :C_V], dob, (((1,), (1,)), ((), ())), preferred_element_type=F32)
                ds = (p * (dp - dlt_ref[hh, qi])).astype(BF16)
                dv_sc[hh] += jnp.dot(p.astype(BF16), dob, preferred_element_type=F32)
                dk_sc[hh] += jnp.dot(ds, qb, preferred_element_type=F32)
                dq_sc[hh, rows, :] += lax.dot_general(ds, k_ref[hh], (((0,), (0,)), ((), ())), preferred_element_type=F32)

        def above_diagonal(qi, carry):
            step(qi, False)
            return carry

        step(ki, True)
        lax.fori_loop(ki + 1, nq, above_diagonal, 0)
        dk_ref[...] = dk_sc[...] * LN2
        dv_ref[...] = dv_sc[...]

        @pl.when(ki == nq - 1)
        def _():
            dq_sc[...] = dq_sc[...] * C_SCALE
            out = pltpu.make_async_copy(dq_sc, dq_hbm.at[pl.ds(hp * C_PAIR, C_PAIR)], sem)
            out.start()
            out.wait()

    once = pl.Buffered(1)
    whole = lambda hp, ki: (hp, 0, 0)
    whole4 = lambda hp, ki: (hp, 0, 0, 0)
    kmap = lambda hp, ki: (hp, ki, 0)
    return pl.pallas_call(
        body, name=name,
        out_shape=(jax.ShapeDtypeStruct((C_HEADS, t, C_PAD), F32), jax.ShapeDtypeStruct((C_HEADS, t, C_PAD), F32),
                   jax.ShapeDtypeStruct((C_HEADS, t, C_V), F32)),
        grid=(C_HEADS // C_PAIR, nq),
        in_specs=[pl.BlockSpec((C_PAIR, t, C_PAD), whole, pipeline_mode=once), pl.BlockSpec((C_PAIR, blk, C_PAD), kmap),
                  pl.BlockSpec((C_PAIR, blk, C_PAD), kmap),
                  pl.BlockSpec((t, C_PAIR * C_V), lambda hp, ki: (0, hp), pipeline_mode=once),
                  pl.BlockSpec((C_PAIR, nq, 1, blk), whole4, pipeline_mode=once),
                  pl.BlockSpec((C_PAIR, nq, 1, blk), whole4, pipeline_mode=once)],
        out_specs=(pl.BlockSpec(memory_space=pl.ANY), pl.BlockSpec((C_PAIR, blk, C_PAD), kmap),
                   pl.BlockSpec((C_PAIR, blk, C_V), kmap)),
        scratch_shapes=[pltpu.VMEM((C_PAIR, t, C_PAD), F32), pltpu.VMEM((C_PAIR, blk, C_PAD), F32),
                        pltpu.VMEM((C_PAIR, blk, C_V), F32), pltpu.SemaphoreType.DMA(())],
        compiler_params=_params(("arbitrary", "arbitrary")),
    )(q, k, v, do_b, lse_rows, dlt_rows)


def _adamw(parts, w, m, v, name):
    layers, rows, cols = w.shape
    tm = _div_tile(rows, 256, 16)

    def body(p_ref, w_ref, m_ref, v_ref, g_ref, d_ref, nm_ref, nv_ref):
        g = p_ref[0].astype(F32)
        for j in range(1, N_DEV):
            g = g + p_ref[j].astype(F32)
        nm = ADAM_B1 * m_ref[...] + (1.0 - ADAM_B1) * g
        nv = ADAM_B2 * v_ref[...] + (1.0 - ADAM_B2) * jnp.square(g)
        m_hat = nm / (1.0 - ADAM_B1 ** ADAM_STEP)
        v_hat = nv / (1.0 - ADAM_B2 ** ADAM_STEP)
        g_ref[...] = g
        d_ref[...] = -ADAM_LR * (m_hat / (jnp.sqrt(v_hat) + ADAM_EPS) + ADAM_WD * w_ref[...])
        nm_ref[...] = nm
        nv_ref[...] = nv

    spec = pl.BlockSpec((None, tm, cols), lambda l, i: (l, i, 0))
    return pl.pallas_call(
        body, name=name, out_shape=tuple(jax.ShapeDtypeStruct(w.shape, F32) for _ in range(4)),
        grid=(layers, rows // tm),
        in_specs=[pl.BlockSpec((None, N_DEV, tm, cols), lambda l, i: (l, 0, i, 0)), spec, spec, spec],
        out_specs=(spec, spec, spec, spec), compiler_params=_params(("parallel", "parallel")),
    )(parts, w, m, v)


def _join_shards(gathered, axis):
    moved = jnp.moveaxis(gathered, 1, axis)
    shape = list(moved.shape)
    shape[axis:axis + 2] = [shape[axis] * shape[axis + 1]]
    return moved.reshape(shape)


def _split_shards(full, axis):
    shape = list(full.shape)
    shape[axis:axis + 1] = [N_DEV, shape[axis] // N_DEV]
    return jnp.moveaxis(full.reshape(shape), axis, 1)


def _as_rows(shape):
    rest = tuple(shape[1:])
    return (shape[0], 1, rest[0]) if len(rest) == 1 else (shape[0],) + rest


MIXER_WEIGHTS = {0: ['a_w_qkv', 'a_w_o'], 1: ['b_w_in', 'b_conv_w', 'b_w_out'],
                 2: ['c_w_down', 'c_q_a_norm', 'c_kv_a_norm', 'c_w_q_up', 'c_w_kv_up', 'c_w_o']}


def _layer_units(i):
    return [(n, i // N_MIXERS) for n in MIXER_WEIGHTS[i % N_MIXERS]] + [('f_w_gate_up', i), ('f_w_down', i)]


def _forward_backward(x, positions, target, local, rep):
    def gather(units):
        return _Exchange([local[n][i:i + 1].astype(BF16) if n in GATHER_BF16 else local[n][i:i + 1] for n, i in units],
                         scatter=False)

    w = {n: {} for n in SHARDED}

    def arrived(units, gathered):
        for (n, i), g in zip(units, gathered):
            full = _join_shards(g, SHARD_AXIS[n])
            w[n][i] = full if full.ndim == 2 else full[0]

    all_units = [u for i in range(DEPTH) for u in _layer_units(i)]
    first_units = _layer_units(0) + [u for u in all_units if u[0] in GATHER_F32]
    later_units = [u for u in all_units if u not in first_units]
    arrived(first_units, _exchange_now(gather(first_units), "gather_first_weights"))

    cos_a, sin_a = (_repeat_lanes(tbl, A_HEADS + A_KV_HEADS, f"a_rope_table_{i}")
                    for i, tbl in enumerate(_rope_tables(positions, A_ROT_DIM, 0, A_HEAD_DIM - A_ROT_DIM)))
    tables_c = _rope_tables(positions, C_ROPE, C_NOPE, 0)
    rows_c = ([_repeat_lanes(tbl, C_HEADS, f"c_rope_table_q{i}") for i, tbl in enumerate(tables_c)]
              + [_repeat_lanes(tbl[:, C_NOPE:], C_HEADS, f"c_rope_table_k{i}") for i, tbl in enumerate(tables_c)])
    saved = []
    for i in range(DEPTH):
        kind, j = i % N_MIXERS, i // N_MIXERS
        s = {'x': x}
        h1 = _rmsnorm_fwd(x, rep['mix_norm'][i:i + 1], f"mix_norm_fwd_{i}")
        s['h1'] = h1
        if kind == 0:
            s['qkv'] = _matmul(h1, w['a_w_qkv'][j], 'nn', f"a_qkv_{i}")
            s['qkv_r'] = _swa_prep_fwd(s['qkv'], rep['a_q_norm'][j:j + 1], rep['a_k_norm'][j:j + 1], cos_a, sin_a,
                                       f"a_prep_fwd_{i}")
            (s['o'], s['lse']), gathered = _swa_fwd(s['qkv_r'], rep['a_sinks'][j:j + 1], f"a_attn_fwd_{i}",
                                                    rider=gather(later_units) if i == 0 else None)
            if i == 0:
                arrived(later_units, gathered)
            x1 = _matmul(s['o'], w['a_w_o'][j], 'nn', f"a_out_{i}", residual=x)
        elif kind == 1:
            s['bcu'] = _matmul(h1, w['b_w_in'][j], 'nn', f"b_in_{i}")
            s['by'] = _sconv_fwd(s['bcu'], w['b_conv_w'][j], f"b_conv_fwd_{i}")
            x1 = _matmul(s['by'], w['b_w_out'][j], 'nn', f"b_out_{i}", residual=x)
        else:
            s['down'] = _matmul(h1, w['c_w_down'][j], 'nn', f"c_down_{i}")
            s['cq'], s['ckv'] = _mla_latent_fwd(s['down'], w['c_q_a_norm'][j], w['c_kv_a_norm'][j],
                                                f"c_latent_fwd_{i}")
            s['qw'] = _matmul(s['cq'], w['c_w_q_up'][j], 'nn', f"c_q_up_{i}")
            s['kvw'] = _matmul(s['ckv'], w['c_w_kv_up'][j], 'nn', f"c_kv_up_{i}")
            s['q'], s['k'], s['v'] = _mla_qk_fwd(s['qw'], s['kvw'], s['down'], rep['c_q_norm'][j:j + 1],
                                                 rep['c_k_norm'][j:j + 1], rows_c, f"c_prep_fwd_{i}")
            s['o'], s['lse'] = _mla_fwd(s['q'], s['k'], s['v'], f"c_attn_fwd_{i}")
            x1 = _matmul(s['o'], w['c_w_o'][j], 'nn', f"c_out_{i}", residual=x)
        s['x1'] = x1
        s['h2'] = _rmsnorm_fwd(x1, rep['ffn_norm'][i:i + 1], f"ffn_norm_fwd_{i}")
        s['gate'], s['up'], s['act'] = _gate_up_act(s['h2'], w['f_w_gate_up'][i], f"f_gate_up_{i}")
        x = _matmul(s['act'], w['f_w_down'][i], 'nn', f"f_down_{i}", residual=x1)
        saved.append(s)

    loss, dx = _loss_head(x, target, "loss_head")

    per_layer = {n: {} for n in WEIGHTS}
    received = {}
    sent = set()

    def ready():
        units = [(n, j) for n in SHARDED for j in sorted(per_layer[n]) if (n, j) not in sent]
        if not units:
            return None, units
        sent.update(units)
        blocks = []
        for n, j in units:
            g = per_layer[n][j]
            blocks.append(_split_shards(g if n in ('c_q_a_norm', 'c_kv_a_norm') else g[None], SHARD_AXIS[n]))
        return _Exchange(blocks, scatter=True), units

    for i in reversed(range(DEPTH)):
        kind, j = i % N_MIXERS, i // N_MIXERS
        s = saved[i]
        per_layer['f_w_down'][i] = _matmul(s['act'], dx, 'tn', f"f_down_dw_{i}", out_dtype=BF16)
        dact = _matmul(dx, w['f_w_down'][i], 'nt', f"f_down_dx_{i}", out_dtype=BF16)
        dgu = _swiglu_bwd(s['gate'], s['up'], dact, f"f_act_bwd_{i}")
        per_layer['f_w_gate_up'][i] = _matmul(s['h2'], dgu, 'tn', f"f_gate_up_dw_{i}", out_dtype=BF16)
        dh2 = _matmul(dgu, w['f_w_gate_up'][i], 'nt', f"f_gate_up_dx_{i}")
        dx1, per_layer['ffn_norm'][i] = _rmsnorm_bwd(s['x1'], rep['ffn_norm'][i:i + 1], dh2, dx, f"ffn_norm_bwd_{i}")
        if kind == 0:
            per_layer['a_w_o'][j] = _matmul(s['o'], dx1, 'tn', f"a_out_dw_{i}", out_dtype=BF16)
            do = _matmul(dx1, w['a_w_o'][j], 'nt', f"a_out_dx_{i}")
            rider, units = ready()
            (dqkv_r, per_layer['a_sinks'][j]), parts = _swa_bwd(s['qkv_r'], s['o'], s['lse'], do, rep['a_sinks'][j:j + 1],
                                                                f"a_attn_bwd_{i}", rider=rider)
            received.update(zip(units, parts or ()))
            dqkv, per_layer['a_q_norm'][j], per_layer['a_k_norm'][j] = _swa_prep_bwd(
                s['qkv'], dqkv_r, rep['a_q_norm'][j:j + 1], rep['a_k_norm'][j:j + 1], cos_a, sin_a, f"a_prep_bwd_{i}")
            per_layer['a_w_qkv'][j] = _matmul(s['h1'], dqkv, 'tn', f"a_qkv_dw_{i}", out_dtype=BF16)
            dh1 = _matmul(dqkv, w['a_w_qkv'][j], 'nt', f"a_qkv_dx_{i}")
        elif kind == 1:
            per_layer['b_w_out'][j] = _matmul(s['by'], dx1, 'tn', f"b_out_dw_{i}", out_dtype=BF16)
            dby = _matmul(dx1, w['b_w_out'][j], 'nt', f"b_out_dx_{i}")
            dbcu, per_layer['b_conv_w'][j] = _sconv_bwd(s['bcu'], dby, w['b_conv_w'][j], f"b_conv_bwd_{i}")
            per_layer['b_w_in'][j] = _matmul(s['h1'], dbcu, 'tn', f"b_in_dw_{i}", out_dtype=BF16)
            dh1 = _matmul(dbcu, w['b_w_in'][j], 'nt', f"b_in_dx_{i}")
        else:
            per_layer['c_w_o'][j] = _matmul(s['o'], dx1, 'tn', f"c_out_dw_{i}", out_dtype=BF16)
            do = _matmul(dx1, w['c_w_o'][j], 'nt', f"c_out_dx_{i}")
            dlt, do_b = _mla_delta(do, s['o'], f"c_attn_delta_{i}")
            blk = min(MLA_BLOCK, do.shape[0])
            as_rows = lambda col: col.reshape(C_HEADS, do.shape[0] // blk, 1, blk)
            dq, dk, dv = _mla_bwd(s['q'], s['k'], s['v'], do_b, as_rows(s['lse']), as_rows(dlt), f"c_attn_bwd_{i}")
            rider, units = ready()
            (dqw, dkvw, dkrope, per_layer['c_q_norm'][j], per_layer['c_k_norm'][j]), parts = _mla_qk_bwd(
                s['qw'], s['kvw'], s['down'], dq, dk, dv, rep['c_q_norm'][j:j + 1], rep['c_k_norm'][j:j + 1], rows_c,
                f"c_prep_bwd_{i}", rider=rider)
            received.update(zip(units, parts or ()))
            per_layer['c_w_q_up'][j] = _matmul(s['cq'], dqw, 'tn', f"c_q_up_dw_{i}", out_dtype=BF16)
            dcq = _matmul(dqw, w['c_w_q_up'][j], 'nt', f"c_q_up_dx_{i}")
            per_layer['c_w_kv_up'][j] = _matmul(s['ckv'], dkvw, 'tn', f"c_kv_up_dw_{i}", out_dtype=BF16)
            dckv = _matmul(dkvw, w['c_w_kv_up'][j], 'nt', f"c_kv_up_dx_{i}")
            ddown, per_layer['c_q_a_norm'][j], per_layer['c_kv_a_norm'][j] = _mla_latent_bwd(
                s['down'], dcq, dckv, dkrope, w['c_q_a_norm'][j], w['c_kv_a_norm'][j], f"c_latent_bwd_{i}")
            per_layer['c_w_down'][j] = _matmul(s['h1'], ddown, 'tn', f"c_down_dw_{i}", out_dtype=BF16)
            dh1 = _matmul(ddown, w['c_w_down'][j], 'nt', f"c_down_dx_{i}")
        dx, per_layer['mix_norm'][i] = _rmsnorm_bwd(s['x'], rep['mix_norm'][i:i + 1], dh1, dx1, f"mix_norm_bwd_{i}")

    last, units = ready()
    received.update(zip(units, _exchange_now(last, "scatter_last_gradients")))
    parts = {n: jnp.concatenate([received[(n, j)] for j in sorted(per_layer[n])], axis=0) for n in SHARDED}
    small = {}
    for n in REPLICATED:
        stacked = jnp.stack([per_layer[n][j] for j in sorted(per_layer[n])])
        small[n] = stacked.reshape(stacked.shape[0], stacked.shape[-1])
    return loss, dx, parts, small


def kernel(x, positions, mix_norm, ffn_norm, a_w_qkv, a_q_norm, a_k_norm, a_sinks, a_w_o, b_w_in, b_conv_w, b_w_out, c_w_down, c_q_a_norm, c_kv_a_norm, c_w_q_up, c_w_kv_up, c_q_norm, c_k_norm, c_w_o, f_w_gate_up, f_w_down, loss_target, m_mix_norm, m_ffn_norm, m_a_w_qkv, m_a_q_norm, m_a_k_norm, m_a_sinks, m_a_w_o, m_b_w_in, m_b_conv_w, m_b_w_out, m_c_w_down, m_c_q_a_norm, m_c_kv_a_norm, m_c_w_q_up, m_c_w_kv_up, m_c_q_norm, m_c_k_norm, m_c_w_o, m_f_w_gate_up, m_f_w_down, v_mix_norm, v_ffn_norm, v_a_w_qkv, v_a_q_norm, v_a_k_norm, v_a_sinks, v_a_w_o, v_b_w_in, v_b_conv_w, v_b_w_out, v_c_w_down, v_c_q_a_norm, v_c_kv_a_norm, v_c_w_q_up, v_c_w_kv_up, v_c_q_norm, v_c_k_norm, v_c_w_o, v_f_w_gate_up, v_f_w_down):
    local = dict(mix_norm=mix_norm, ffn_norm=ffn_norm, a_w_qkv=a_w_qkv, a_q_norm=a_q_norm, a_k_norm=a_k_norm, a_sinks=a_sinks, a_w_o=a_w_o, b_w_in=b_w_in, b_conv_w=b_conv_w, b_w_out=b_w_out, c_w_down=c_w_down, c_q_a_norm=c_q_a_norm, c_kv_a_norm=c_kv_a_norm, c_w_q_up=c_w_q_up, c_w_kv_up=c_w_kv_up, c_q_norm=c_q_norm, c_k_norm=c_k_norm, c_w_o=c_w_o, f_w_gate_up=f_w_gate_up, f_w_down=f_w_down)
    mom1 = dict(mix_norm=m_mix_norm, ffn_norm=m_ffn_norm, a_w_qkv=m_a_w_qkv, a_q_norm=m_a_q_norm, a_k_norm=m_a_k_norm, a_sinks=m_a_sinks, a_w_o=m_a_w_o, b_w_in=m_b_w_in, b_conv_w=m_b_conv_w, b_w_out=m_b_w_out, c_w_down=m_c_w_down, c_q_a_norm=m_c_q_a_norm, c_kv_a_norm=m_c_kv_a_norm, c_w_q_up=m_c_w_q_up, c_w_kv_up=m_c_w_kv_up, c_q_norm=m_c_q_norm, c_k_norm=m_c_k_norm, c_w_o=m_c_w_o, f_w_gate_up=m_f_w_gate_up, f_w_down=m_f_w_down)
    mom2 = dict(mix_norm=v_mix_norm, ffn_norm=v_ffn_norm, a_w_qkv=v_a_w_qkv, a_q_norm=v_a_q_norm, a_k_norm=v_a_k_norm, a_sinks=v_a_sinks, a_w_o=v_a_w_o, b_w_in=v_b_w_in, b_conv_w=v_b_conv_w, b_w_out=v_b_w_out, c_w_down=v_c_w_down, c_q_a_norm=v_c_q_a_norm, c_kv_a_norm=v_c_kv_a_norm, c_w_q_up=v_c_w_q_up, c_w_kv_up=v_c_w_kv_up, c_q_norm=v_c_q_norm, c_k_norm=v_c_k_norm, c_w_o=v_c_w_o, f_w_gate_up=v_f_w_gate_up, f_w_down=v_f_w_down)
    t, d = x.shape[1], x.shape[2]

    rep = {n: local[n] for n in REPLICATED}
    loss, grad_x, parts, small = _forward_backward(x.reshape(t, d), positions.reshape(t), loss_target.reshape(t, d),
                                                   {n: local[n] for n in SHARDED}, rep)

    out_g, out_d, out_m, out_v = {}, {}, {}, {}

    def update(names, parts):
        for n, part in zip(names, parts):
            shape = local[n].shape if n in SHARD_AXIS else (1,) + local[n].shape
            view = _as_rows(shape)
            results = _adamw(part.reshape(view[0], N_DEV, view[1], view[2]),
                             *[src[n].reshape(view) for src in (local, mom1, mom2)], name="adamw_" + n)
            for dst, res in zip((out_g, out_d, out_m, out_v), results):
                dst[n] = res.reshape(local[n].shape)

    update(SHARDED, [parts[n] for n in SHARDED])
    update(REPLICATED, _exchange_now(_Exchange([small[n].reshape((1,) + small[n].shape) for n in REPLICATED],
                                               scatter=False), "gather_small_gradients"))

    loss = lax.psum(loss.reshape(()), MESH_AXES)
    outs = [loss, grad_x.reshape(1, t, d)]
    for res in (out_g, out_d, out_m, out_v):
        outs += [res[n] for n in WEIGHTS]
    return tuple(outs)
```

```python
import jax
import jax.numpy as jnp
import numpy as np
from jax import lax
from jax.experimental import pallas as pl
from jax.experimental.pallas import tpu as pltpu

F32 = jnp.float32
BF16 = jnp.bfloat16

N_DEV = 8
MESH_AXES = ("x", "y", "c")

DEPTH = 4
N_MIXERS = 3
ROPE_THETA = 500000.0
EPS = 1e-6
A_HEADS, A_KV_HEADS, A_HEAD_DIM, A_ROT_DIM, A_WINDOW = 16, 4, 64, 16, 128
A_GROUP = A_HEADS // A_KV_HEADS
C_HEADS, C_NOPE, C_ROPE, C_V, C_Q_RANK, C_KV_RANK = 16, 64, 32, 64, 384, 256
C_QK = C_NOPE + C_ROPE
ADAM_LR, ADAM_B1, ADAM_B2, ADAM_EPS, ADAM_WD, ADAM_STEP = 0.001, 0.9, 0.999, 1e-08, 0.01, 10

VMEM_LIMIT_BYTES = 48 * 1024 * 1024
LANES = 128
NEG = -1e30
MLA_BLOCK = 512
MLA_FWD_Q_BLOCK = 512
MLA_FWD_K_BLOCK = 2048

WEIGHTS = ['mix_norm', 'ffn_norm', 'a_w_qkv', 'a_q_norm', 'a_k_norm', 'a_sinks', 'a_w_o', 'b_w_in', 'b_conv_w', 'b_w_out',
           'c_w_down', 'c_q_a_norm', 'c_kv_a_norm', 'c_w_q_up', 'c_w_kv_up', 'c_q_norm', 'c_k_norm', 'c_w_o', 'f_w_gate_up',
           'f_w_down']
SHARD_AXIS = {'a_w_qkv': 2, 'a_w_o': 1, 'b_w_in': 2, 'b_conv_w': 2, 'b_w_out': 1, 'c_w_down': 1, 'c_q_a_norm': 1,
              'c_kv_a_norm': 1, 'c_w_q_up': 2, 'c_w_kv_up': 2, 'c_w_o': 1, 'f_w_gate_up': 2, 'f_w_down': 1}
SHARDED = [n for n in WEIGHTS if n in SHARD_AXIS]
REPLICATED = [n for n in WEIGHTS if n not in SHARD_AXIS]
GATHER_F32 = ['b_conv_w', 'c_q_a_norm', 'c_kv_a_norm']
GATHER_BF16 = [n for n in SHARDED if n not in GATHER_F32]

def _params(semantics=None):
    return pltpu.CompilerParams(dimension_semantics=semantics, vmem_limit_bytes=VMEM_LIMIT_BYTES)


def _div_tile(n, cap, mult=LANES):
    best = None
    t = mult
    while t <= min(n, cap):
        if n % t == 0:
            best = t
        t += mult
    return n if best is None else best


ANY_SPEC = pl.BlockSpec(memory_space=pl.ANY)


class _Exchange:
    def __init__(self, arrays, scatter):
        self.arrays, self.scatter = list(arrays), scatter
        n = len(self.arrays)
        self.out_shapes = [jax.ShapeDtypeStruct(a.shape if scatter else (a.shape[0], N_DEV) + tuple(a.shape[1:]), a.dtype)
                           for a in self.arrays]
        self.scratch = [pltpu.SemaphoreType.DMA((n, N_DEV - 1)), pltpu.SemaphoreType.DMA((n, N_DEV - 1)),
                        pltpu.SemaphoreType.DMA((n,))]

    def _copies(self, src_refs, out_refs, sems):
        send_sems, recv_sems, local_sems = sems
        x, y, c = lax.axis_index("x"), lax.axis_index("y"), lax.axis_index("c")
        me_idx = 4 * x + 2 * y + c
        n = len(self.arrays)

        def remote(a, k, src, dst, to):
            return pltpu.make_async_remote_copy(src_ref=src, dst_ref=dst, send_sem=send_sems.at[a, k],
                                                recv_sem=recv_sems.at[a, k], device_id=to,
                                                device_id_type=pl.DeviceIdType.MESH)

        local, first, forwards, last = [], [], [], []
        if self.scatter:
            for a in range(n):
                local.append(pltpu.make_async_copy(src_refs[a].at[:, me_idx], out_refs[a].at[:, me_idx], local_sems.at[a]))
                for r in range(1, N_DEV):
                    px = 1 - x if (r >> 2) & 1 else x
                    py = 1 - y if (r >> 1) & 1 else y
                    pc = 1 - c if r & 1 else c
                    cp = remote(a, r - 1, src_refs[a].at[:, 4 * px + 2 * py + pc], out_refs[a].at[:, me_idx], (px, py, pc))
                    first.append(cp)
                    last.append(cp)
            return local, first, forwards, last
        me, sibling = (x, y, c), (x, y, 1 - c)
        chips = [(1 - x, y), (x, 1 - y), (1 - x, 1 - y)]

        def place(a, block):
            return out_refs[a].at[:, 4 * block[0] + 2 * block[1] + block[2]]

        for a in range(n):
            local.append(pltpu.make_async_copy(src_refs[a], place(a, me), local_sems.at[a]))
            first.append(remote(a, 0, src_refs[a], place(a, me), sibling))
            last.append(remote(a, 0, place(a, sibling), place(a, sibling), me))
            for j, chip in enumerate(chips):
                first.append(remote(a, 1 + j, src_refs[a], place(a, me), (*chip, c)))
                forwards.append((remote(a, 1 + j, place(a, (*chip, c)), place(a, (*chip, c)), me),
                                 remote(a, 4 + j, place(a, (*chip, c)), place(a, (*chip, c)), sibling)))
                last.append(remote(a, 4 + j, place(a, (*chip, 1 - c)), place(a, (*chip, 1 - c)), me))
        return local, first, forwards, last

    def start(self, src_refs, out_refs, sems):
        local, first, _, _ = self._copies(src_refs, out_refs, sems)
        for cp in local + first:
            cp.start()

    def finish(self, src_refs, out_refs, sems):
        local, first, forwards, last = self._copies(src_refs, out_refs, sems)
        for arrival, forward in forwards:
            arrival.wait_recv()
            forward.start()
        for cp in last:
            cp.wait_recv()
        for cp in first + [forward for _, forward in forwards]:
            cp.wait_send()
        for cp in local:
            cp.wait()


def _exchange_now(exchange, name):
    n = len(exchange.arrays)

    def body(*refs):
        exchange.start(refs[:n], refs[n:2 * n], refs[2 * n:])
        exchange.finish(refs[:n], refs[n:2 * n], refs[2 * n:])

    return pl.pallas_call(
        body, name=name, out_shape=tuple(exchange.out_shapes), in_specs=[ANY_SPEC] * n, out_specs=(ANY_SPEC,) * n,
        scratch_shapes=exchange.scratch,
    )(*exchange.arrays)


def _host_call(body, rider, name, out_shape, grid, in_specs, out_specs, operands, semantics):
    if rider is None:
        return pl.pallas_call(body, name=name, out_shape=tuple(out_shape), grid=grid, in_specs=list(in_specs),
                              out_specs=tuple(out_specs), compiler_params=_params(semantics))(*operands), None
    n_in, n_out, r = len(in_specs), len(out_shape), len(rider.arrays)

    def riding(*refs):
        ins, rider_in = refs[:n_in], refs[n_in:n_in + r]
        outs, rider_out = refs[n_in + r:n_in + r + n_out], refs[n_in + r + n_out:n_in + 2 * r + n_out]
        sems = refs[n_in + 2 * r + n_out:]
        step = pl.program_id(0)

        @pl.when(step == 0)
        def _():
            rider.start(rider_in, rider_out, sems)

        body(*ins, *outs)

        @pl.when(step == grid[0] - 1)
        def _():
            rider.finish(rider_in, rider_out, sems)

    results = pl.pallas_call(
        riding, name=name, out_shape=tuple(out_shape) + tuple(rider.out_shapes), grid=grid,
        in_specs=list(in_specs) + [ANY_SPEC] * r, out_specs=tuple(out_specs) + (ANY_SPEC,) * r,
        scratch_shapes=rider.scratch, compiler_params=_params(("arbitrary",)),
    )(*operands, *rider.arrays)
    return results[:n_out], results[n_out:]


def _matmul(a, b, mode, name, out_dtype=F32, residual=None):
    if mode == 'nn':
        (m, k), (k2, n) = a.shape, b.shape
    elif mode == 'nt':
        (m, k), (n, k2) = a.shape, b.shape
    else:
        (k, m), (k2, n) = a.shape, b.shape
    assert k == k2, (name, a.shape, b.shape, mode)
    if mode == 'tn':
        tm, tk = _div_tile(m, 1408), _div_tile(k, 1024, 16)
    else:
        tm, tk = _div_tile(m, 1024, 16), _div_tile(k, 1536)
    tn = _div_tile(n, 1408)
    nk = k // tk
    dims = {'nn': (((1,), (0,)), ((), ())), 'nt': (((1,), (1,)), ((), ())), 'tn': (((0,), (0,)), ((), ()))}[mode]

    def product(a_ref, b_ref):
        return lax.dot_general(a_ref[...].astype(BF16), b_ref[...].astype(BF16), dims, preferred_element_type=F32)

    def finish(r, rest):
        if residual is not None:
            r = r + rest[0][...]
        rest[-1 if nk == 1 else -2][...] = r.astype(out_dtype)

    def body_single(a_ref, b_ref, *rest):
        finish(product(a_ref, b_ref), rest)

    def body_accumulate(a_ref, b_ref, *rest):
        acc = rest[-1]
        kk = pl.program_id(2)

        @pl.when(kk == 0)
        def _():
            acc[...] = jnp.zeros_like(acc)

        acc[...] += product(a_ref, b_ref)

        @pl.when(kk == nk - 1)
        def _():
            finish(acc[...], rest)

    a_spec = pl.BlockSpec((tk, tm), lambda i, j, kk: (kk, i)) if mode == 'tn' else pl.BlockSpec((tm, tk), lambda i, j, kk: (i, kk))
    b_spec = pl.BlockSpec((tn, tk), lambda i, j, kk: (j, kk)) if mode == 'nt' else pl.BlockSpec((tk, tn), lambda i, j, kk: (kk, j))
    o_spec = pl.BlockSpec((tm, tn), lambda i, j, kk: (i, j))
    in_specs, operands = [a_spec, b_spec], [a, b]
    if residual is not None:
        in_specs.append(o_spec)
        operands.append(residual)
    return pl.pallas_call(
        body_single if nk == 1 else body_accumulate, name=name, out_shape=jax.ShapeDtypeStruct((m, n), out_dtype),
        grid=(m // tm, n // tn, nk), in_specs=in_specs, out_specs=o_spec,
        scratch_shapes=[] if nk == 1 else [pltpu.VMEM((tm, tn), F32)],
        compiler_params=_params(("parallel", "parallel", "arbitrary")),
    )(*operands)


def _row_spec(tm, cols):
    return pl.BlockSpec((tm, cols), lambda i: (i, 0))


def _const_spec(shape):
    return pl.BlockSpec(shape, lambda i: tuple(0 for _ in shape))


def _accumulate(ref, value, step):
    @pl.when(step == 0)
    def _():
        ref[...] = value

    @pl.when(step > 0)
    def _():
        ref[...] += value


def _rstd(x):
    return lax.rsqrt(jnp.mean(x * x, axis=-1, keepdims=True) + EPS)


def _norm_bwd(x, g, dout):
    xn = x * _rstd(x)
    dg = jnp.sum(dout * xn, axis=0, keepdims=True)
    dxn = dout * g
    dx = _rstd(x) * (dxn - xn * jnp.mean(dxn * xn, axis=-1, keepdims=True))
    return dx, dg


def _rmsnorm_fwd(x, g, name):
    t, d = x.shape
    tm = _div_tile(t, 512, 16)

    def body(x_ref, g_ref, o_ref):
        xv = x_ref[...]
        o_ref[...] = (xv * _rstd(xv) * g_ref[...]).astype(BF16)

    return pl.pallas_call(
        body, name=name, out_shape=jax.ShapeDtypeStruct((t, d), BF16), grid=(t // tm,),
        in_specs=[_row_spec(tm, d), _const_spec((1, d))], out_specs=_row_spec(tm, d),
        compiler_params=_params(("parallel",)),
    )(x, g)


def _rmsnorm_bwd(x, g, dh, dres, name):
    t, d = x.shape
    tm = _div_tile(t, 512, 8)

    def body(x_ref, g_ref, dh_ref, dres_ref, dx_ref, dg_ref):
        dx, dg = _norm_bwd(x_ref[...], g_ref[...], dh_ref[...])
        dx_ref[...] = dres_ref[...] + dx
        _accumulate(dg_ref, dg, pl.program_id(0))

    return pl.pallas_call(
        body, name=name,
        out_shape=(jax.ShapeDtypeStruct((t, d), F32), jax.ShapeDtypeStruct((1, d), F32)), grid=(t // tm,),
        in_specs=[_row_spec(tm, d), _const_spec((1, d)), _row_spec(tm, d), _row_spec(tm, d)],
        out_specs=(_row_spec(tm, d), _const_spec((1, d))),
        compiler_params=_params(("arbitrary",)),
    )(x, g, dh, dres)


def _sigmoid(x):
    return 0.5 * jnp.tanh(0.5 * x) + 0.5


def _gate_up_act(h, w, name):
    t, d = h.shape
    f = w.shape[1] // 2
    tm, tn = _div_tile(t, 512, 16), _div_tile(f, 1408)
    nj = f // tn

    def body(h_ref, wg_ref, wu_ref, g_ref, u_ref, a_ref):
        hv = h_ref[...]
        gate = jnp.dot(hv, wg_ref[...], preferred_element_type=F32)
        up = jnp.dot(hv, wu_ref[...], preferred_element_type=F32)
        g_ref[...] = gate.astype(BF16)
        u_ref[...] = up.astype(BF16)
        a_ref[...] = (gate * _sigmoid(gate) * up).astype(BF16)

    tile = pl.BlockSpec((tm, tn), lambda i, j: (i, j))
    return pl.pallas_call(
        body, name=name, out_shape=tuple(jax.ShapeDtypeStruct((t, f), BF16) for _ in range(3)), grid=(t // tm, nj),
        in_specs=[pl.BlockSpec((tm, d), lambda i, j: (i, 0)), pl.BlockSpec((d, tn), lambda i, j: (0, j)),
                  pl.BlockSpec((d, tn), lambda i, j: (0, j + nj))],
        out_specs=(tile, tile, tile), compiler_params=_params(("parallel", "parallel")),
    )(h, w, w)


def _swiglu_bwd(gate_pre, up_pre, da, name):
    t, f = gate_pre.shape
    tm = _div_tile(t, 512, 16)

    def body(g_ref, u_ref, da_ref, o_ref):
        gate, up, dav = g_ref[...].astype(F32), u_ref[...].astype(F32), da_ref[...].astype(F32)
        sig = _sigmoid(gate)
        o_ref[:, :f] = (dav * up * (sig * (1.0 + gate * (1.0 - sig)))).astype(BF16)
        o_ref[:, f:] = (dav * (gate * sig)).astype(BF16)

    return pl.pallas_call(
        body, name=name, out_shape=jax.ShapeDtypeStruct((t, 2 * f), BF16), grid=(t // tm,),
        in_specs=[_row_spec(tm, f), _row_spec(tm, f), _row_spec(tm, f)], out_specs=_row_spec(tm, 2 * f),
        compiler_params=_params(("parallel",)),
    )(gate_pre, up_pre, da)


def _loss_head(y, target, name):
    t, d = y.shape
    tm = _div_tile(t, 512, 8)

    def body(y_ref, t_ref, loss_ref, dy_ref):
        diff = y_ref[...] - t_ref[...]
        dy_ref[...] = diff * (1.0 / d)
        part = jnp.sum(jnp.sum(diff * diff, axis=1, keepdims=True), axis=0, keepdims=True) * (0.5 / d)
        _accumulate(loss_ref, part, pl.program_id(0))

    return pl.pallas_call(
        body, name=name,
        out_shape=(jax.ShapeDtypeStruct((1, 1), F32), jax.ShapeDtypeStruct((t, d), F32)), grid=(t // tm,),
        in_specs=[_row_spec(tm, d), _row_spec(tm, d)], out_specs=(_const_spec((1, 1)), _row_spec(tm, d)),
        compiler_params=_params(("arbitrary",)),
    )(y, target)


HALO = 8


def _shift_down(z, k, halo_rows):
    tm = z.shape[0]
    row = lax.broadcasted_iota(jnp.int32, z.shape, 0)
    out = pltpu.roll(z, k, 0)
    for j in range(k):
        out = jnp.where(row == j, halo_rows[HALO - k + j:HALO - k + j + 1, :], out)
    return out


def _shift_up(z, k, halo_rows):
    tm = z.shape[0]
    row = lax.broadcasted_iota(jnp.int32, z.shape, 0)
    out = pltpu.roll(z, tm - k, 0)
    for j in range(k):
        out = jnp.where(row == tm - k + j, halo_rows[j:j + 1, :], out)
    return out


def _sconv_specs(t, tm, cols):
    per = tm // HALO
    last = t // HALO - 1
    cur = pl.BlockSpec((tm, cols), lambda i: (i, 0))
    prev = pl.BlockSpec((HALO, cols), lambda i: (jnp.maximum(i * per - 1, 0), 0))
    nxt = pl.BlockSpec((HALO, cols), lambda i: (jnp.minimum((i + 1) * per, last), 0))
    return cur, prev, nxt


def _sconv_fwd(bcu, conv_w, name):
    t, d3 = bcu.shape
    d = d3 // 3
    tm = _div_tile(t, 256, 16)
    cur, prev, _ = _sconv_specs(t, tm, d3)

    def body(cur_ref, prev_ref, w_ref, o_ref):
        i = pl.program_id(0)
        z = cur_ref[:, d:2 * d] * cur_ref[:, 2 * d:]
        zp = prev_ref[:, d:2 * d] * prev_ref[:, 2 * d:] * (i > 0).astype(F32)
        y = w_ref[0:1, :] * _shift_down(z, 2, zp) + w_ref[1:2, :] * _shift_down(z, 1, zp) + w_ref[2:3, :] * z
        o_ref[...] = (cur_ref[:, :d] * y).astype(BF16)

    return pl.pallas_call(
        body, name=name, out_shape=jax.ShapeDtypeStruct((t, d), BF16), grid=(t // tm,),
        in_specs=[cur, prev, _const_spec((3, d))], out_specs=_row_spec(tm, d),
        compiler_params=_params(("parallel",)),
    )(bcu, bcu, conv_w)


def _sconv_bwd(bcu, dout, conv_w, name):
    t, d3 = bcu.shape
    d = d3 // 3
    tm = _div_tile(t, 256, 16)
    cur, prev, nxt = _sconv_specs(t, tm, d3)
    dcur, _, dnxt = _sconv_specs(t, tm, d)
    n_tiles = t // tm

    def body(cur_ref, prev_ref, nxt_ref, do_ref, don_ref, w_ref, o_ref, dw_ref):
        i = pl.program_id(0)
        b, cg, u = cur_ref[:, :d], cur_ref[:, d:2 * d], cur_ref[:, 2 * d:]
        z = cg * u
        zp = prev_ref[:, d:2 * d] * prev_ref[:, 2 * d:] * (i > 0).astype(F32)
        z1, z2 = _shift_down(z, 1, zp), _shift_down(z, 2, zp)
        w0, w1, w2 = w_ref[0:1, :], w_ref[1:2, :], w_ref[2:3, :]
        y = w0 * z2 + w1 * z1 + w2 * z
        dov = do_ref[...]
        dy = dov * b
        dyn = don_ref[...] * nxt_ref[:, :d] * (i < n_tiles - 1).astype(F32)
        dz = w2 * dy + w1 * _shift_up(dy, 1, dyn) + w0 * _shift_up(dy, 2, dyn)
        o_ref[:, :d] = (dov * y).astype(BF16)
        o_ref[:, d:2 * d] = (dz * u).astype(BF16)
        o_ref[:, 2 * d:] = (dz * cg).astype(BF16)
        dw = jnp.concatenate([jnp.sum(dy * z2, axis=0, keepdims=True), jnp.sum(dy * z1, axis=0, keepdims=True),
                              jnp.sum(dy * z, axis=0, keepdims=True)], axis=0)
        _accumulate(dw_ref, dw, i)

    return pl.pallas_call(
        body, name=name,
        out_shape=(jax.ShapeDtypeStruct((t, d3), BF16), jax.ShapeDtypeStruct((3, d), F32)), grid=(n_tiles,),
        in_specs=[cur, prev, nxt, dcur, dnxt, _const_spec((3, d))],
        out_specs=(_row_spec(tm, d3), _const_spec((3, d))),
        compiler_params=_params(("arbitrary",)),
    )(bcu, bcu, bcu, dout, dout, conv_w)


def _rope_tables(positions, rot, lead, trail):
    inv_freq = ROPE_THETA ** (-jnp.arange(0, rot, 2, dtype=F32) / rot)
    ang = positions.astype(F32)[:, None] * inv_freq
    cos, sin = jnp.cos(ang), jnp.sin(ang)
    t = positions.shape[0]
    cos_full = jnp.concatenate([jnp.ones((t, lead), F32), cos, cos, jnp.ones((t, trail), F32)], axis=1)
    sin_full = jnp.concatenate([jnp.zeros((t, lead), F32), -sin, sin, jnp.zeros((t, trail), F32)], axis=1)
    return cos_full, sin_full


def _repeat_lanes(x, reps, name):
    t, d = x.shape
    tm = _div_tile(t, 512, 8)

    def body(x_ref, o_ref):
        o_ref[...] = jnp.concatenate([x_ref[...]] * reps, axis=1)

    return pl.pallas_call(
        body, name=name, out_shape=jax.ShapeDtypeStruct((t, reps * d), x.dtype), grid=(t // tm,),
        in_specs=[_row_spec(tm, d)], out_specs=_row_spec(tm, reps * d), compiler_params=_params(("parallel",)),
    )(x)


def _pieces_dot(a, b, pieces):
    total, rest = None, a
    for _ in range(pieces):
        piece = rest.astype(BF16)
        term = jnp.dot(piece, b, preferred_element_type=F32)
        total = term if total is None else total + term
        rest = rest - piece.astype(F32)
    return total


A_Q_COLS = A_HEADS * A_HEAD_DIM
A_KV_COLS = A_KV_HEADS * A_HEAD_DIM
A_COLS = A_Q_COLS + 2 * A_KV_COLS
A_SCALE = A_HEAD_DIM ** -0.5


A_NORMED = A_Q_COLS + A_KV_COLS


def _swa_prep_tables(q_norm, k_norm):
    lane = np.arange(A_NORMED)
    seg = (lane[:, None] // A_HEAD_DIM == np.arange(LANES)[None, :]).astype(np.float32)
    fold = (np.where(lane < A_Q_COLS, 0, A_HEAD_DIM)[:, None] + lane[:, None] % A_HEAD_DIM
            == np.arange(LANES)[None, :]).astype(np.float32)
    gains = jnp.concatenate([jnp.tile(q_norm, (1, A_HEADS)), jnp.tile(k_norm, (1, A_KV_HEADS))], axis=1)
    return gains, jnp.asarray(seg, BF16), jnp.asarray(seg.T, BF16), jnp.asarray(fold, BF16)


def _wide_rstd(x, seg, seg_t):
    mean_sq = _pieces_dot(x * x, seg, 1) * (1.0 / A_HEAD_DIM)
    return _pieces_dot(lax.rsqrt(mean_sq + EPS), seg_t, 2)


def _wide_partner(n):
    dim = lax.broadcasted_iota(jnp.int32, n.shape, 1) & (A_HEAD_DIM - 1)
    half = A_ROT_DIM // 2
    return jnp.where(dim < half, pltpu.roll(n, A_NORMED - half, 1),
                     jnp.where(dim < A_ROT_DIM, pltpu.roll(n, half, 1), 0.0))


def _swa_prep_fwd(qkv, q_norm, k_norm, cos_w, sin_w, name):
    t = qkv.shape[0]
    tm = _div_tile(t, 256, 16)
    gains, seg, seg_t, _ = _swa_prep_tables(q_norm, k_norm)

    def body(x_ref, g_ref, cos_ref, sin_ref, seg_ref, segt_ref, o_ref):
        x = x_ref[:, :A_NORMED]
        n = x * _wide_rstd(x, seg_ref[...], segt_ref[...]) * g_ref[...]
        o_ref[:, :A_NORMED] = (n * cos_ref[...] + _wide_partner(n) * sin_ref[...]).astype(BF16)
        o_ref[:, A_NORMED:] = x_ref[:, A_NORMED:].astype(BF16)

    return pl.pallas_call(
        body, name=name, out_shape=jax.ShapeDtypeStruct((t, A_COLS), BF16), grid=(t // tm,),
        in_specs=[_row_spec(tm, A_COLS), _const_spec((1, A_NORMED)), _row_spec(tm, A_NORMED), _row_spec(tm, A_NORMED),
                  _const_spec(seg.shape), _const_spec(seg_t.shape)],
        out_specs=_row_spec(tm, A_COLS), compiler_params=_params(("parallel",)),
    )(qkv, gains, cos_w, sin_w, seg, seg_t)


def _swa_prep_bwd(qkv, dqkv_r, q_norm, k_norm, cos_w, sin_w, name):
    t = qkv.shape[0]
    tm = _div_tile(t, 256, 16)
    hd = A_HEAD_DIM
    gains, seg, seg_t, fold = _swa_prep_tables(q_norm, k_norm)

    def body(x_ref, d_ref, g_ref, cos_ref, sin_ref, seg_ref, segt_ref, fold_ref, o_ref, dgq_ref, dgk_ref):
        x, dout = x_ref[:, :A_NORMED], d_ref[:, :A_NORMED]
        rstd = _wide_rstd(x, seg_ref[...], segt_ref[...])
        xn = x * rstd
        dn = dout * cos_ref[...] + _wide_partner(dout * sin_ref[...])
        dg = _pieces_dot(jnp.sum(dn * xn, axis=0, keepdims=True), fold_ref[...], 3)
        dxn = dn * g_ref[...]
        mean = _pieces_dot(_pieces_dot(dxn * xn, seg_ref[...], 1) * (1.0 / hd), segt_ref[...], 2)
        o_ref[:, :A_NORMED] = (rstd * (dxn - xn * mean)).astype(BF16)
        o_ref[:, A_NORMED:] = d_ref[:, A_NORMED:].astype(BF16)
        _accumulate(dgq_ref, dg[:, :hd], pl.program_id(0))
        _accumulate(dgk_ref, dg[:, hd:2 * hd], pl.program_id(0))

    return pl.pallas_call(
        body, name=name,
        out_shape=(jax.ShapeDtypeStruct((t, A_COLS), BF16), jax.ShapeDtypeStruct((1, hd), F32),
                   jax.ShapeDtypeStruct((1, hd), F32)),
        grid=(t // tm,),
        in_specs=[_row_spec(tm, A_COLS), _row_spec(tm, A_COLS), _const_spec((1, A_NORMED)), _row_spec(tm, A_NORMED),
                  _row_spec(tm, A_NORMED), _const_spec(seg.shape), _const_spec(seg_t.shape), _const_spec(fold.shape)],
        out_specs=(_row_spec(tm, A_COLS), _const_spec((1, hd)), _const_spec((1, hd))),
        compiler_params=_params(("arbitrary",)),
    )(qkv, dqkv_r, gains, cos_w, sin_w, seg, seg_t, fold)


def _group_rows(ref, k, width=A_HEAD_DIM, base=0):
    return jnp.concatenate([ref[:, base + (A_GROUP * k + g) * width:base + (A_GROUP * k + g + 1) * width]
                            for g in range(A_GROUP)], axis=0)


def _group_column(ref, k, rows):
    cols = []
    for g in range(A_GROUP):
        h = A_GROUP * k + g
        col = ref[:, h:h + 1]
        cols.append(jnp.broadcast_to(col, (rows, 1)) if col.shape[0] == 1 else col)
    return jnp.concatenate(cols, axis=0)


def _swa_fwd(qkv_r, sinks, name, rider=None):
    t = qkv_r.shape[0]
    blk = A_WINDOW
    nb = t // blk
    hd = A_HEAD_DIM
    kv_block = A_Q_COLS // (2 * A_KV_COLS)

    def body(q_ref, kvc_ref, kvp_ref, s_ref, o_ref, lse_ref):
        n = pl.program_id(0)
        shape = (A_GROUP * blk, 2 * blk)
        qpos = lax.broadcasted_iota(jnp.int32, shape, 0) & (blk - 1)
        col = lax.broadcasted_iota(jnp.int32, shape, 1)
        delta = qpos + blk - col
        valid = (delta >= 0) & (delta < A_WINDOW) & ((col >= blk) | (n > 0))
        for k in range(A_KV_HEADS):
            qg = _group_rows(q_ref, k)
            kw = jnp.concatenate([kvp_ref[:, k * hd:(k + 1) * hd], kvc_ref[:, k * hd:(k + 1) * hd]], axis=0)
            vw = jnp.concatenate([kvp_ref[:, A_KV_COLS + k * hd:A_KV_COLS + (k + 1) * hd],
                                  kvc_ref[:, A_KV_COLS + k * hd:A_KV_COLS + (k + 1) * hd]], axis=0)
            s = lax.dot_general(qg, kw, (((1,), (1,)), ((), ())), preferred_element_type=F32) * A_SCALE
            s = jnp.where(valid, s, NEG)
            sink = _group_column(s_ref, k, blk)
            m = jnp.maximum(jnp.max(s, axis=-1, keepdims=True), sink)
            p = jnp.exp(s - m)
            denom = jnp.sum(p, axis=-1, keepdims=True) + jnp.exp(sink - m)
            o = jnp.dot(p.astype(BF16), vw, preferred_element_type=F32) / denom
            lse = m + jnp.log(denom)
            for g in range(A_GROUP):
                h = A_GROUP * k + g
                o_ref[:, h * hd:(h + 1) * hd] = o[g * blk:(g + 1) * blk].astype(BF16)
                lse_ref[:, h:h + 1] = lse[g * blk:(g + 1) * blk]

    return _host_call(
        body, rider, name,
        out_shape=(jax.ShapeDtypeStruct((t, A_Q_COLS), BF16), jax.ShapeDtypeStruct((t, A_HEADS), F32)), grid=(nb,),
        in_specs=[pl.BlockSpec((blk, A_Q_COLS), lambda n: (n, 0)),
                  pl.BlockSpec((blk, 2 * A_KV_COLS), lambda n: (n, kv_block)),
                  pl.BlockSpec((blk, 2 * A_KV_COLS), lambda n: (jnp.maximum(n - 1, 0), kv_block)),
                  _const_spec((1, A_HEADS))],
        out_specs=(pl.BlockSpec((blk, A_Q_COLS), lambda n: (n, 0)), pl.BlockSpec((blk, A_HEADS), lambda n: (n, 0))),
        operands=(qkv_r, qkv_r, qkv_r, sinks), semantics=("parallel",))


def _swa_bwd(qkv_r, o, lse, do, sinks, name, rider=None):
    t = qkv_r.shape[0]
    blk = A_WINDOW
    nb = t // blk
    hd = A_HEAD_DIM
    kv_block = A_Q_COLS // (2 * A_KV_COLS)
    rows = A_GROUP * blk

    def nxt(n):
        return jnp.minimum(n + 1, nb - 1)

    def body(qc_ref, qn_ref, kvc_ref, kvp_ref, doc_ref, don_ref, oc_ref, on_ref, lc_ref, ln_ref, s_ref, dx_ref, ds_ref):
        n = pl.program_id(0)
        shape = (2 * rows, 2 * blk)
        row = lax.broadcasted_iota(jnp.int32, shape, 0)
        col = lax.broadcasted_iota(jnp.int32, shape, 1)
        is_next = row >= rows
        delta = jnp.where(is_next, blk, 0) + blk + (row & (blk - 1)) - col
        valid = ((delta >= 0) & (delta < A_WINDOW) & ((col >= blk) | (n > 0)) & (jnp.logical_not(is_next) | (n < nb - 1)))
        dsink_cols = []
        for k in range(A_KV_HEADS):
            qs = jnp.concatenate([_group_rows(qc_ref, k), _group_rows(qn_ref, k)], axis=0)
            dos = jnp.concatenate([_group_rows(doc_ref, k), _group_rows(don_ref, k)], axis=0)
            os_ = jnp.concatenate([_group_rows(oc_ref, k), _group_rows(on_ref, k)], axis=0).astype(F32)
            lses = jnp.concatenate([_group_column(lc_ref, k, blk), _group_column(ln_ref, k, blk)], axis=0)
            kw = jnp.concatenate([kvp_ref[:, k * hd:(k + 1) * hd], kvc_ref[:, k * hd:(k + 1) * hd]], axis=0)
            vw = jnp.concatenate([kvp_ref[:, A_KV_COLS + k * hd:A_KV_COLS + (k + 1) * hd],
                                  kvc_ref[:, A_KV_COLS + k * hd:A_KV_COLS + (k + 1) * hd]], axis=0)
            s = lax.dot_general(qs, kw, (((1,), (1,)), ((), ())), preferred_element_type=F32) * A_SCALE
            p = jnp.exp(jnp.where(valid, s - lses, NEG))
            dos_b = dos.astype(BF16)
            dp = lax.dot_general(dos_b, vw, (((1,), (1,)), ((), ())), preferred_element_type=F32)
            dlt = jnp.sum(dos * os_, axis=-1, keepdims=True)
            ds = p * (dp - dlt)
            dq = jnp.dot(ds[:rows].astype(BF16), kw, preferred_element_type=F32) * A_SCALE
            dk = lax.dot_general(ds[:, blk:].astype(BF16), qs, (((0,), (0,)), ((), ())), preferred_element_type=F32) * A_SCALE
            dv = lax.dot_general(p[:, blk:].astype(BF16), dos_b, (((0,), (0,)), ((), ())), preferred_element_type=F32)
            for g in range(A_GROUP):
                h = A_GROUP * k + g
                dx_ref[:, h * hd:(h + 1) * hd] = dq[g * blk:(g + 1) * blk]
            dx_ref[:, A_Q_COLS + k * hd:A_Q_COLS + (k + 1) * hd] = dk
            dx_ref[:, A_Q_COLS + A_KV_COLS + k * hd:A_Q_COLS + A_KV_COLS + (k + 1) * hd] = dv
            sink = _group_column(s_ref, k, blk)
            contrib = -jnp.exp(sink - lses[:rows]) * dlt[:rows]
            for g in range(A_GROUP):
                dsink_cols.append(jnp.sum(contrib[g * blk:(g + 1) * blk], axis=0, keepdims=True))
        _accumulate(ds_ref, jnp.concatenate(dsink_cols, axis=1), n)

    q_spec = lambda f: pl.BlockSpec((blk, A_Q_COLS), lambda n: (f(n), 0))
    l_spec = lambda f: pl.BlockSpec((blk, A_HEADS), lambda n: (f(n), 0))
    same = lambda n: n
    return _host_call(
        body, rider, name,
        out_shape=(jax.ShapeDtypeStruct((t, A_COLS), F32), jax.ShapeDtypeStruct((1, A_HEADS), F32)), grid=(nb,),
        in_specs=[q_spec(same), q_spec(nxt),
                  pl.BlockSpec((blk, 2 * A_KV_COLS), lambda n: (n, kv_block)),
                  pl.BlockSpec((blk, 2 * A_KV_COLS), lambda n: (jnp.maximum(n - 1, 0), kv_block)),
                  q_spec(same), q_spec(nxt), q_spec(same), q_spec(nxt), l_spec(same), l_spec(nxt),
                  _const_spec((1, A_HEADS))],
        out_specs=(pl.BlockSpec((blk, A_COLS), lambda n: (n, 0)), _const_spec((1, A_HEADS))),
        operands=(qkv_r, qkv_r, qkv_r, qkv_r, do, do, o, o, lse, lse, sinks), semantics=("arbitrary",))


C_DOWN_COLS = C_Q_RANK + C_KV_RANK + C_ROPE
C_Q_COLS = C_HEADS * C_QK
C_KV_COLS = C_HEADS * (C_NOPE + C_V)
C_O_COLS = C_HEADS * C_V
C_PAD = LANES
C_SCALE = C_QK ** -0.5
LOG2E = 1.4426950408889634
LN2 = 0.6931471805599453
C_Q_SCALE = C_SCALE * LOG2E
C_PAIR = 2


def _mla_latent_fwd(down, q_a_norm, kv_a_norm, name):
    t = down.shape[0]
    tm = _div_tile(t, 512, 16)

    def body(x_ref, gq_ref, gk_ref, cq_ref, ckv_ref):
        cq, ckv = x_ref[:, :C_Q_RANK], x_ref[:, C_Q_RANK:C_Q_RANK + C_KV_RANK]
        cq_ref[...] = (cq * _rstd(cq) * gq_ref[...]).astype(BF16)
        ckv_ref[...] = (ckv * _rstd(ckv) * gk_ref[...]).astype(BF16)

    return pl.pallas_call(
        body, name=name,
        out_shape=(jax.ShapeDtypeStruct((t, C_Q_RANK), BF16), jax.ShapeDtypeStruct((t, C_KV_RANK), BF16)), grid=(t // tm,),
        in_specs=[_row_spec(tm, C_DOWN_COLS), _const_spec((1, C_Q_RANK)), _const_spec((1, C_KV_RANK))],
        out_specs=(_row_spec(tm, C_Q_RANK), _row_spec(tm, C_KV_RANK)), compiler_params=_params(("parallel",)),
    )(down, q_a_norm, kv_a_norm)


def _mla_latent_bwd(down, dcq, dckv, dkrope, q_a_norm, kv_a_norm, name):
    t = down.shape[0]
    tm = _div_tile(t, 512, 16)

    def body(x_ref, dcq_ref, dckv_ref, dkr_ref, gq_ref, gk_ref, o_ref, dgq_ref, dgk_ref):
        dq, dgq = _norm_bwd(x_ref[:, :C_Q_RANK], gq_ref[...], dcq_ref[...])
        dkv, dgk = _norm_bwd(x_ref[:, C_Q_RANK:C_Q_RANK + C_KV_RANK], gk_ref[...], dckv_ref[...])
        o_ref[...] = jnp.concatenate([dq, dkv, dkr_ref[...]], axis=1).astype(BF16)
        _accumulate(dgq_ref, dgq, pl.program_id(0))
        _accumulate(dgk_ref, dgk, pl.program_id(0))

    return pl.pallas_call(
        body, name=name,
        out_shape=(jax.ShapeDtypeStruct((t, C_DOWN_COLS), BF16), jax.ShapeDtypeStruct((1, C_Q_RANK), F32),
                   jax.ShapeDtypeStruct((1, C_KV_RANK), F32)),
        grid=(t // tm,),
        in_specs=[_row_spec(tm, C_DOWN_COLS), _row_spec(tm, C_Q_RANK), _row_spec(tm, C_KV_RANK), _row_spec(tm, C_ROPE),
                  _const_spec((1, C_Q_RANK)), _const_spec((1, C_KV_RANK))],
        out_specs=(_row_spec(tm, C_DOWN_COLS), _const_spec((1, C_Q_RANK)), _const_spec((1, C_KV_RANK))),
        compiler_params=_params(("arbitrary",)),
    )(down, dcq, dckv, dkrope, q_a_norm, kv_a_norm)


def _head_major_spec(tm, width):
    return pl.BlockSpec((C_HEADS, tm, width), lambda i: (0, i, 0))


C_NOPE_V = C_NOPE + C_V
C_ROPE_COLS = C_HEADS * C_ROPE


def _mla_prep_tables(q_norm, k_norm):
    ql, kl, rl = np.arange(C_Q_COLS), np.arange(C_KV_COLS), np.arange(C_ROPE_COLS)
    col = np.arange(LANES)[None, :]
    is_nope = (kl % C_NOPE_V) < C_NOPE
    one_hot = lambda m: jnp.asarray(m.astype(np.float32), BF16)
    gk_nope = jnp.concatenate([k_norm[:, :C_NOPE], jnp.zeros((1, C_V), F32)], axis=1)
    return dict(
        q_dim=jnp.asarray((ql % C_QK)[None, :], jnp.int32),
        q_gain=jnp.tile(q_norm, (1, C_HEADS)),
        q_seg=one_hot(ql[:, None] // C_QK == col), q_spread=one_hot((ql[:, None] // C_QK == col).T),
        q_fold=one_hot(ql[:, None] % C_QK == col),
        k_nope=jnp.asarray(is_nope[None, :].astype(np.float32)),
        k_gain=jnp.tile(gk_nope, (1, C_HEADS)),
        k_seg=one_hot(is_nope[:, None] & (kl[:, None] // C_NOPE_V == col)),
        k_spread=one_hot((kl[:, None] // C_NOPE_V == col).T),
        k_fold=one_hot(is_nope[:, None] & (kl[:, None] % C_NOPE_V == col)),
        r_gain=jnp.tile(k_norm[:, C_NOPE:], (1, C_HEADS)),
        r_rep=one_hot(np.arange(C_ROPE)[:, None] == rl[None, :] % C_ROPE),
        r_seg=one_hot(rl[:, None] // C_ROPE == col), r_spread=one_hot((rl[:, None] // C_ROPE == col).T),
        r_fold=one_hot(rl[:, None] % C_ROPE == col))


def _q_partner(n, dim):
    half = C_ROPE // 2
    return jnp.where((dim >= C_NOPE) & (dim < C_NOPE + half), pltpu.roll(n, C_Q_COLS - half, 1),
                     jnp.where(dim >= C_NOPE + half, pltpu.roll(n, half, 1), 0.0))


def _rope_partner(n):
    half = C_ROPE // 2
    dim = lax.broadcasted_iota(jnp.int32, n.shape, 1) & (C_ROPE - 1)
    return jnp.where(dim < half, pltpu.roll(n, C_ROPE_COLS - half, 1), pltpu.roll(n, half, 1))


def _head_rstd(sum_sq):
    return lax.rsqrt(sum_sq * (1.0 / C_QK) + EPS)


MLA_PREP_FWD_TABLES = ['q_dim', 'q_gain', 'q_seg', 'q_spread', 'k_gain', 'k_seg', 'k_spread', 'r_gain', 'r_rep', 'r_spread']
MLA_PREP_BWD_TABLES = MLA_PREP_FWD_TABLES + ['q_fold', 'k_nope', 'k_fold', 'r_seg', 'r_fold']


def _mla_qk_fwd(qw, kvw, down, q_norm, k_norm, rows, name):
    t = qw.shape[0]
    tm = _div_tile(t, 256, 16)
    tables = _mla_prep_tables(q_norm, k_norm)
    consts = [tables[n] for n in MLA_PREP_FWD_TABLES]

    def body(q_ref, kv_ref, dn_ref, qcos_ref, qsin_ref, rcos_ref, rsin_ref, *rest):
        c = {n: r[...] for n, r in zip(MLA_PREP_FWD_TABLES, rest)}
        qo_ref, ko_ref, vo_ref = rest[len(MLA_PREP_FWD_TABLES):]
        q, kv, kr = q_ref[...], kv_ref[...], dn_ref[:, C_Q_RANK + C_KV_RANK:]
        nq = q * _pieces_dot(_head_rstd(_pieces_dot(q * q, c['q_seg'], 1)), c['q_spread'], 2) * c['q_gain']
        out_q = (nq * qcos_ref[...] + _q_partner(nq, c['q_dim']) * qsin_ref[...]) * C_Q_SCALE
        rstd = _head_rstd(_pieces_dot(kv * kv, c['k_seg'], 1) + jnp.sum(kr * kr, axis=-1, keepdims=True))
        nope = kv * _pieces_dot(rstd, c['k_spread'], 2) * c['k_gain']
        nr = _pieces_dot(kr, c['r_rep'], 2) * _pieces_dot(rstd, c['r_spread'], 2) * c['r_gain']
        rope = nr * rcos_ref[...] + _rope_partner(nr) * rsin_ref[...]
        pad = jnp.zeros((tm, C_PAD - C_QK), F32)
        one_then_zeros = (lax.broadcasted_iota(jnp.int32, (tm, C_PAD - C_V), 1) == 0).astype(F32)
        for h in range(C_HEADS):
            qo_ref[h] = jnp.concatenate([out_q[:, h * C_QK:(h + 1) * C_QK], pad], axis=1).astype(BF16)
            ko_ref[h] = jnp.concatenate([nope[:, h * C_NOPE_V:h * C_NOPE_V + C_NOPE], rope[:, h * C_ROPE:(h + 1) * C_ROPE],
                                         pad], axis=1).astype(BF16)
            vo_ref[h] = jnp.concatenate([kv[:, h * C_NOPE_V + C_NOPE:(h + 1) * C_NOPE_V], one_then_zeros],
                                        axis=1).astype(BF16)

    return pl.pallas_call(
        body, name=name,
        out_shape=(jax.ShapeDtypeStruct((C_HEADS, t, C_PAD), BF16), jax.ShapeDtypeStruct((C_HEADS, t, C_PAD), BF16),
                   jax.ShapeDtypeStruct((C_HEADS, t, C_PAD), BF16)),
        grid=(t // tm,),
        in_specs=[_row_spec(tm, C_Q_COLS), _row_spec(tm, C_KV_COLS), _row_spec(tm, C_DOWN_COLS)]
                 + [_row_spec(tm, r.shape[1]) for r in rows] + [_const_spec(a.shape) for a in consts],
        out_specs=(_head_major_spec(tm, C_PAD), _head_major_spec(tm, C_PAD), _head_major_spec(tm, C_PAD)),
        compiler_params=_params(("parallel",)),
    )(qw, kvw, down, *rows, *consts)


def _mla_qk_bwd(qw, kvw, down, dq, dk, dv, q_norm, k_norm, rows, name, rider=None):
    t = qw.shape[0]
    tm = _div_tile(t, 256, 16)
    tables = _mla_prep_tables(q_norm, k_norm)
    consts = [tables[n] for n in MLA_PREP_BWD_TABLES]

    def body(q_ref, kv_ref, dn_ref, dq_ref, dk_ref, dv_ref, qcos_ref, qsin_ref, rcos_ref, rsin_ref, *rest):
        c = {n: r[...] for n, r in zip(MLA_PREP_BWD_TABLES, rest)}
        dqw_ref, dkvw_ref, dkr_ref, dgq_ref, dgk_ref = rest[len(MLA_PREP_BWD_TABLES):]
        step = pl.program_id(0)
        q, kv, kr = q_ref[...], kv_ref[...], dn_ref[:, C_Q_RANK + C_KV_RANK:]
        dout_q = jnp.concatenate([dq_ref[h][:, :C_QK] for h in range(C_HEADS)], axis=1)
        d_slab = jnp.concatenate([x for h in range(C_HEADS) for x in (dk_ref[h][:, :C_NOPE], dv_ref[h])], axis=1)
        d_rope = jnp.concatenate([dk_ref[h][:, C_NOPE:C_QK] for h in range(C_HEADS)], axis=1)

        rq = _pieces_dot(_head_rstd(_pieces_dot(q * q, c['q_seg'], 1)), c['q_spread'], 2)
        xq = q * rq
        dnq = dout_q * qcos_ref[...] + _q_partner(dout_q * qsin_ref[...], c['q_dim'])
        dgq = _pieces_dot(jnp.sum(dnq * xq, axis=0, keepdims=True), c['q_fold'], 3)
        dxq = dnq * c['q_gain']
        mean_q = _pieces_dot(_pieces_dot(dxq * xq, c['q_seg'], 1) * (1.0 / C_QK), c['q_spread'], 2)
        dqw_ref[...] = (rq * (dxq - xq * mean_q)).astype(BF16)

        rstd = _head_rstd(_pieces_dot(kv * kv, c['k_seg'], 1) + jnp.sum(kr * kr, axis=-1, keepdims=True))
        r_nope, r_rope = _pieces_dot(rstd, c['k_spread'], 2), _pieces_dot(rstd, c['r_spread'], 2)
        x_nope = kv * r_nope * c['k_nope']
        x_rope = _pieces_dot(kr, c['r_rep'], 2) * r_rope
        dn_nope = d_slab * c['k_nope']
        dn_rope = d_rope * rcos_ref[...] + _rope_partner(d_rope * rsin_ref[...])
        dg_nope = _pieces_dot(jnp.sum(dn_nope * x_nope, axis=0, keepdims=True), c['k_fold'], 3)
        dg_rope = _pieces_dot(jnp.sum(dn_rope * x_rope, axis=0, keepdims=True), c['r_fold'], 3)
        dx_nope, dx_rope = dn_nope * c['k_gain'], dn_rope * c['r_gain']
        mean = (_pieces_dot(dx_nope * x_nope, c['k_seg'], 1) + _pieces_dot(dx_rope * x_rope, c['r_seg'], 1)) * (1.0 / C_QK)
        g_nope = r_nope * (dx_nope - x_nope * _pieces_dot(mean, c['k_spread'], 2))
        g_rope = r_rope * (dx_rope - x_rope * _pieces_dot(mean, c['r_spread'], 2))
        dkvw_ref[...] = jnp.where(c['k_nope'] > 0.0, g_nope, d_slab).astype(BF16)
        dkr_ref[...] = _pieces_dot(g_rope, c['r_fold'], 3)[:, :C_ROPE]
        _accumulate(dgq_ref, dgq[:, :C_QK], step)
        _accumulate(dgk_ref, jnp.concatenate([dg_nope[:, :C_NOPE], dg_rope[:, :C_ROPE]], axis=1), step)

    return _host_call(
        body, rider, name,
        out_shape=(jax.ShapeDtypeStruct((t, C_Q_COLS), BF16), jax.ShapeDtypeStruct((t, C_KV_COLS), BF16),
                   jax.ShapeDtypeStruct((t, C_ROPE), F32), jax.ShapeDtypeStruct((1, C_QK), F32),
                   jax.ShapeDtypeStruct((1, C_QK), F32)),
        grid=(t // tm,),
        in_specs=[_row_spec(tm, C_Q_COLS), _row_spec(tm, C_KV_COLS), _row_spec(tm, C_DOWN_COLS),
                  _head_major_spec(tm, C_PAD), _head_major_spec(tm, C_PAD), _head_major_spec(tm, C_V)]
                 + [_row_spec(tm, r.shape[1]) for r in rows] + [_const_spec(a.shape) for a in consts],
        out_specs=(_row_spec(tm, C_Q_COLS), _row_spec(tm, C_KV_COLS), _row_spec(tm, C_ROPE), _const_spec((1, C_QK)),
                   _const_spec((1, C_QK))),
        operands=(qw, kvw, down, dq, dk, dv, *rows, *consts), semantics=("arbitrary",))


def _causal_keep(rows, cols, row_offset=0, transposed=False):
    row = lax.broadcasted_iota(jnp.int32, (rows, cols), 0) + row_offset
    col = lax.broadcasted_iota(jnp.int32, (rows, cols), 1)
    return (row <= col) if transposed else (col <= row)


def _mla_fwd(q, k, v, name):
    _, t, _ = q.shape
    bq, bk = min(MLA_FWD_Q_BLOCK, t), min(MLA_FWD_K_BLOCK, t)
    nq = t // bq

    def body(q_ref, k_ref, v_ref, o_ref, lse_ref, m_sc, acc_sc):
        qi = pl.program_id(1)
        m_sc[...] = jnp.full_like(m_sc, NEG)
        acc_sc[...] = jnp.zeros_like(acc_sc)
        diagonal = (qi * bq) // bk
        lead = qi * bq - diagonal * bk

        def step(ki, masked):
            rows = pl.ds(pl.multiple_of(ki * bk, bk), bk)
            for hh in range(C_PAIR):
                s = lax.dot_general(q_ref[hh], k_ref[hh, rows, :], (((1,), (1,)), ((), ())), preferred_element_type=F32)
                if masked:
                    s = jnp.where(_causal_keep(bq, bk, lead), s, NEG)
                m_prev = m_sc[hh]
                m_new = jnp.maximum(m_prev, jnp.max(s, axis=-1, keepdims=True))
                p = jnp.exp2(s - m_new)
                acc_sc[hh] = jnp.exp2(m_prev - m_new) * acc_sc[hh] + jnp.dot(p.astype(BF16), v_ref[hh, rows, :],
                                                                                preferred_element_type=F32)
                m_sc[hh] = m_new

        def below_diagonal(ki, carry):
            step(ki, False)
            return carry

        lax.fori_loop(0, diagonal, below_diagonal, 0)
        step(diagonal, True)
        outs = []
        for hh in range(C_PAIR):
            denom = acc_sc[hh, :, C_V:C_V + 1]
            outs.append(acc_sc[hh, :, :C_V] / denom)
            lse_ref[hh] = m_sc[hh] + jnp.log(denom) * LOG2E
        o_ref[...] = jnp.concatenate(outs, axis=1).astype(BF16)

    whole = lambda hp, qi: (hp, 0, 0)
    return pl.pallas_call(
        body, name=name,
        out_shape=(jax.ShapeDtypeStruct((t, C_O_COLS), BF16), jax.ShapeDtypeStruct((C_HEADS, t, 1), F32)),
        grid=(C_HEADS // C_PAIR, nq),
        in_specs=[pl.BlockSpec((C_PAIR, bq, C_PAD), lambda hp, qi: (hp, qi, 0)),
                  pl.BlockSpec((C_PAIR, t, C_PAD), whole, pipeline_mode=pl.Buffered(1)),
                  pl.BlockSpec((C_PAIR, t, C_PAD), whole, pipeline_mode=pl.Buffered(1))],
        out_specs=(pl.BlockSpec((bq, C_PAIR * C_V), lambda hp, qi: (qi, hp)),
                   pl.BlockSpec((C_PAIR, bq, 1), lambda hp, qi: (hp, qi, 0))),
        scratch_shapes=[pltpu.VMEM((C_PAIR, bq, 1), F32), pltpu.VMEM((C_PAIR, bq, C_PAD), F32)],
        compiler_params=_params(("parallel", "arbitrary")),
    )(q, k, v)


def _mla_delta(do, o, name):
    t = do.shape[0]
    blk = _div_tile(t, 4 * MLA_BLOCK, 16)

    def body(do_ref, o_ref, dlt_ref, dob_ref):
        for hh in range(C_PAIR):
            do_h = do_ref[:, hh * C_V:(hh + 1) * C_V]
            dlt_ref[hh] = jnp.sum(do_h * o_ref[:, hh * C_V:(hh + 1) * C_V].astype(F32), axis=-1, keepdims=True)
        dob_ref[...] = do_ref[...].astype(BF16)

    wide = pl.BlockSpec((blk, C_PAIR * C_V), lambda hp, i: (i, hp))
    return pl.pallas_call(
        body, name=name,
        out_shape=(jax.ShapeDtypeStruct((C_HEADS, t, 1), F32), jax.ShapeDtypeStruct(do.shape, BF16)),
        grid=(C_HEADS // C_PAIR, t // blk), in_specs=[wide, wide],
        out_specs=(pl.BlockSpec((C_PAIR, blk, 1), lambda hp, i: (hp, i, 0)), wide),
        compiler_params=_params(("parallel", "parallel")),
    )(do, o)


def _mla_bwd(q, k, v, do_b, lse_rows, dlt_rows, name):
    _, t, _ = q.shape
    blk = min(MLA_BLOCK, t)
    nq = t // blk

    def body(q_ref, k_ref, v_ref, do_ref, lse_ref, dlt_ref, dq_hbm, dk_ref, dv_ref, dq_sc, dk_sc, dv_sc, sem):
        hp, ki = pl.program_id(0), pl.program_id(1)

        @pl.when(ki == 0)
        def _():
            dq_sc[...] = jnp.zeros_like(dq_sc)

        dk_sc[...] = jnp.zeros_like(dk_sc)
        dv_sc[...] = jnp.zeros_like(dv_sc)

        def step(qi, masked):
            rows = pl.ds(pl.multiple_of(qi * blk, blk), blk)
            for hh in range(C_PAIR):
                qb = q_ref[hh, rows, :]
                dob = do_ref[rows, hh * C_V:(hh + 1) * C_V]
                s = lax.dot_general(k_ref[hh], qb, (((1,), (1,)), ((), ())), preferred_element_type=F32)
                if masked:
                    s = jnp.where(_causal_keep(blk, blk, transposed=True), s, NEG)
                p = jnp.exp2(s - lse_ref[hh, qi])
                dp = lax.dot_general(v_ref[hh, :, :C_V], dob, (((1,), (1,)), ((), ())), preferred_element_type=F32)
                ds = (p * (dp - dlt_ref[hh, qi])).astype(BF16)
                dv_sc[hh] += jnp.dot(p.astype(BF16), dob, preferred_element_type=F32)
                dk_sc[hh] += jnp.dot(ds, qb, preferred_element_type=F32)
                dq_sc[hh, rows, :] += lax.dot_general(ds, k_ref[hh], (((0,), (0,)), ((), ())), preferred_element_type=F32)

        def above_diagonal(qi, carry):
            step(qi, False)
            return carry

        step(ki, True)
        lax.fori_loop(ki + 1, nq, above_diagonal, 0)
        dk_ref[...] = dk_sc[...] * LN2
        dv_ref[...] = dv_sc[...]

        @pl.when(ki == nq - 1)
        def _():
            dq_sc[...] = dq_sc[...] * C_SCALE
            out = pltpu.make_async_copy(dq_sc, dq_hbm.at[pl.ds(hp * C_PAIR, C_PAIR)], sem)
            out.start()
            out.wait()

    once = pl.Buffered(1)
    whole = lambda hp, ki: (hp, 0, 0)
    whole4 = lambda hp, ki: (hp, 0, 0, 0)
    kmap = lambda hp, ki: (hp, ki, 0)
    return pl.pallas_call(
        body, name=name,
        out_shape=(jax.ShapeDtypeStruct((C_HEADS, t, C_PAD), F32), jax.ShapeDtypeStruct((C_HEADS, t, C_PAD), F32),
                   jax.ShapeDtypeStruct((C_HEADS, t, C_V), F32)),
        grid=(C_HEADS // C_PAIR, nq),
        in_specs=[pl.BlockSpec((C_PAIR, t, C_PAD), whole, pipeline_mode=once), pl.BlockSpec((C_PAIR, blk, C_PAD), kmap),
                  pl.BlockSpec((C_PAIR, blk, C_PAD), kmap),
                  pl.BlockSpec((t, C_PAIR * C_V), lambda hp, ki: (0, hp), pipeline_mode=once),
                  pl.BlockSpec((C_PAIR, nq, 1, blk), whole4, pipeline_mode=once),
                  pl.BlockSpec((C_PAIR, nq, 1, blk), whole4, pipeline_mode=once)],
        out_specs=(pl.BlockSpec(memory_space=pl.ANY), pl.BlockSpec((C_PAIR, blk, C_PAD), kmap),
                   pl.BlockSpec((C_PAIR, blk, C_V), kmap)),
        scratch_shapes=[pltpu.VMEM((C_PAIR, t, C_PAD), F32), pltpu.VMEM((C_PAIR, blk, C_PAD), F32),
                        pltpu.VMEM((C_PAIR, blk, C_V), F32), pltpu.SemaphoreType.DMA(())],
        compiler_params=_params(("arbitrary", "arbitrary")),
    )(q, k, v, do_b, lse_rows, dlt_rows)


def _adamw(parts, w, m, v, name):
    layers, rows, cols = w.shape
    tm = _div_tile(rows, 256, 16)

    def body(p_ref, w_ref, m_ref, v_ref, g_ref, d_ref, nm_ref, nv_ref):
        g = p_ref[0].astype(F32)
        for j in range(1, N_DEV):
            g = g + p_ref[j].astype(F32)
        nm = ADAM_B1 * m_ref[...] + (1.0 - ADAM_B1) * g
        nv = ADAM_B2 * v_ref[...] + (1.0 - ADAM_B2) * jnp.square(g)
        m_hat = nm / (1.0 - ADAM_B1 ** ADAM_STEP)
        v_hat = nv / (1.0 - ADAM_B2 ** ADAM_STEP)
        g_ref[...] = g
        d_ref[...] = -ADAM_LR * (m_hat / (jnp.sqrt(v_hat) + ADAM_EPS) + ADAM_WD * w_ref[...])
        nm_ref[...] = nm
        nv_ref[...] = nv

    spec = pl.BlockSpec((None, tm, cols), lambda l, i: (l, i, 0))
    return pl.pallas_call(
        body, name=name, out_shape=tuple(jax.ShapeDtypeStruct(w.shape, F32) for _ in range(4)),
        grid=(layers, rows // tm),
        in_specs=[pl.BlockSpec((None, N_DEV, tm, cols), lambda l, i: (l, 0, i, 0)), spec, spec, spec],
        out_specs=(spec, spec, spec, spec), compiler_params=_params(("parallel", "parallel")),
    )(parts, w, m, v)


def _join_shards(gathered, axis):
    moved = jnp.moveaxis(gathered, 1, axis)
    shape = list(moved.shape)
    shape[axis:axis + 2] = [shape[axis] * shape[axis + 1]]
    return moved.reshape(shape)


def _split_shards(full, axis):
    shape = list(full.shape)
    shape[axis:axis + 1] = [N_DEV, shape[axis] // N_DEV]
    return jnp.moveaxis(full.reshape(shape), axis, 1)


def _as_rows(shape):
    rest = tuple(shape[1:])
    return (shape[0], 1, rest[0]) if len(rest) == 1 else (shape[0],) + rest


MIXER_WEIGHTS = {0: ['a_w_qkv', 'a_w_o'], 1: ['b_w_in', 'b_conv_w', 'b_w_out'],
                 2: ['c_w_down', 'c_q_a_norm', 'c_kv_a_norm', 'c_w_q_up', 'c_w_kv_up', 'c_w_o']}


def _layer_units(i):
    return [(n, i // N_MIXERS) for n in MIXER_WEIGHTS[i % N_MIXERS]] + [('f_w_gate_up', i), ('f_w_down', i)]


def _forward_backward(x, positions, target, local, rep):
    def gather(units):
        return _Exchange([local[n][i:i + 1].astype(BF16) if n in GATHER_BF16 else local[n][i:i + 1] for n, i in units],
                         scatter=False)

    w = {n: {} for n in SHARDED}

    def arrived(units, gathered):
        for (n, i), g in zip(units, gathered):
            full = _join_shards(g, SHARD_AXIS[n])
            w[n][i] = full if full.ndim == 2 else full[0]

    all_units = [u for i in range(DEPTH) for u in _layer_units(i)]
    first_units = _layer_units(0) + [u for u in all_units if u[0] in GATHER_F32]
    later_units = [u for u in all_units if u not in first_units]
    arrived(first_units, _exchange_now(gather(first_units), "gather_first_weights"))

    cos_a, sin_a = (_repeat_lanes(tbl, A_HEADS + A_KV_HEADS, f"a_rope_table_{i}")
                    for i, tbl in enumerate(_rope_tables(positions, A_ROT_DIM, 0, A_HEAD_DIM - A_ROT_DIM)))
    tables_c = _rope_tables(positions, C_ROPE, C_NOPE, 0)
    rows_c = ([_repeat_lanes(tbl, C_HEADS, f"c_rope_table_q{i}") for i, tbl in enumerate(tables_c)]
              + [_repeat_lanes(tbl[:, C_NOPE:], C_HEADS, f"c_rope_table_k{i}") for i, tbl in enumerate(tables_c)])
    saved = []
    for i in range(DEPTH):
        kind, j = i % N_MIXERS, i // N_MIXERS
        s = {'x': x}
        h1 = _rmsnorm_fwd(x, rep['mix_norm'][i:i + 1], f"mix_norm_fwd_{i}")
        s['h1'] = h1
        if kind == 0:
            s['qkv'] = _matmul(h1, w['a_w_qkv'][j], 'nn', f"a_qkv_{i}")
            s['qkv_r'] = _swa_prep_fwd(s['qkv'], rep['a_q_norm'][j:j + 1], rep['a_k_norm'][j:j + 1], cos_a, sin_a,
                                       f"a_prep_fwd_{i}")
            (s['o'], s['lse']), gathered = _swa_fwd(s['qkv_r'], rep['a_sinks'][j:j + 1], f"a_attn_fwd_{i}",
                                                    rider=gather(later_units) if i == 0 else None)
            if i == 0:
                arrived(later_units, gathered)
            x1 = _matmul(s['o'], w['a_w_o'][j], 'nn', f"a_out_{i}", residual=x)
        elif kind == 1:
            s['bcu'] = _matmul(h1, w['b_w_in'][j], 'nn', f"b_in_{i}")
            s['by'] = _sconv_fwd(s['bcu'], w['b_conv_w'][j], f"b_conv_fwd_{i}")
            x1 = _matmul(s['by'], w['b_w_out'][j], 'nn', f"b_out_{i}", residual=x)
        else:
            s['down'] = _matmul(h1, w['c_w_down'][j], 'nn', f"c_down_{i}")
            s['cq'], s['ckv'] = _mla_latent_fwd(s['down'], w['c_q_a_norm'][j], w['c_kv_a_norm'][j],
                                                f"c_latent_fwd_{i}")
            s['qw'] = _matmul(s['cq'], w['c_w_q_up'][j], 'nn', f"c_q_up_{i}")
            s['kvw'] = _matmul(s['ckv'], w['c_w_kv_up'][j], 'nn', f"c_kv_up_{i}")
            s['q'], s['k'], s['v'] = _mla_qk_fwd(s['qw'], s['kvw'], s['down'], rep['c_q_norm'][j:j + 1],
                                                 rep['c_k_norm'][j:j + 1], rows_c, f"c_prep_fwd_{i}")
            s['o'], s['lse'] = _mla_fwd(s['q'], s['k'], s['v'], f"c_attn_fwd_{i}")
            x1 = _matmul(s['o'], w['c_w_o'][j], 'nn', f"c_out_{i}", residual=x)
        s['x1'] = x1
        s['h2'] = _rmsnorm_fwd(x1, rep['ffn_norm'][i:i + 1], f"ffn_norm_fwd_{i}")
        s['gate'], s['up'], s['act'] = _gate_up_act(s['h2'], w['f_w_gate_up'][i], f"f_gate_up_{i}")
        x = _matmul(s['act'], w['f_w_down'][i], 'nn', f"f_down_{i}", residual=x1)
        saved.append(s)

    loss, dx = _loss_head(x, target, "loss_head")

    per_layer = {n: {} for n in WEIGHTS}
    received = {}
    sent = set()

    def ready():
        units = [(n, j) for n in SHARDED for j in sorted(per_layer[n]) if (n, j) not in sent]
        if not units:
            return None, units
        sent.update(units)
        blocks = []
        for n, j in units:
            g = per_layer[n][j]
            blocks.append(_split_shards(g if n in ('c_q_a_norm', 'c_kv_a_norm') else g[None], SHARD_AXIS[n]))
        return _Exchange(blocks, scatter=True), units

    for i in reversed(range(DEPTH)):
        kind, j = i % N_MIXERS, i // N_MIXERS
        s = saved[i]
        per_layer['f_w_down'][i] = _matmul(s['act'], dx, 'tn', f"f_down_dw_{i}", out_dtype=BF16)
        dact = _matmul(dx, w['f_w_down'][i], 'nt', f"f_down_dx_{i}", out_dtype=BF16)
        dgu = _swiglu_bwd(s['gate'], s['up'], dact, f"f_act_bwd_{i}")
        per_layer['f_w_gate_up'][i] = _matmul(s['h2'], dgu, 'tn', f"f_gate_up_dw_{i}", out_dtype=BF16)
        dh2 = _matmul(dgu, w['f_w_gate_up'][i], 'nt', f"f_gate_up_dx_{i}")
        dx1, per_layer['ffn_norm'][i] = _rmsnorm_bwd(s['x1'], rep['ffn_norm'][i:i + 1], dh2, dx, f"ffn_norm_bwd_{i}")
        if kind == 0:
            per_layer['a_w_o'][j] = _matmul(s['o'], dx1, 'tn', f"a_out_dw_{i}", out_dtype=BF16)
            do = _matmul(dx1, w['a_w_o'][j], 'nt', f"a_out_dx_{i}")
            rider, units = ready()
            (dqkv_r, per_layer['a_sinks'][j]), parts = _swa_bwd(s['qkv_r'], s['o'], s['lse'], do, rep['a_sinks'][j:j + 1],
                                                                f"a_attn_bwd_{i}", rider=rider)
            received.update(zip(units, parts or ()))
            dqkv, per_layer['a_q_norm'][j], per_layer['a_k_norm'][j] = _swa_prep_bwd(
                s['qkv'], dqkv_r, rep['a_q_norm'][j:j + 1], rep['a_k_norm'][j:j + 1], cos_a, sin_a, f"a_prep_bwd_{i}")
            per_layer['a_w_qkv'][j] = _matmul(s['h1'], dqkv, 'tn', f"a_qkv_dw_{i}", out_dtype=BF16)
            dh1 = _matmul(dqkv, w['a_w_qkv'][j], 'nt', f"a_qkv_dx_{i}")
        elif kind == 1:
            per_layer['b_w_out'][j] = _matmul(s['by'], dx1, 'tn', f"b_out_dw_{i}", out_dtype=BF16)
            dby = _matmul(dx1, w['b_w_out'][j], 'nt', f"b_out_dx_{i}")
            dbcu, per_layer['b_conv_w'][j] = _sconv_bwd(s['bcu'], dby, w['b_conv_w'][j], f"b_conv_bwd_{i}")
            per_layer['b_w_in'][j] = _matmul(s['h1'], dbcu, 'tn', f"b_in_dw_{i}", out_dtype=BF16)
            dh1 = _matmul(dbcu, w['b_w_in'][j], 'nt', f"b_in_dx_{i}")
        else:
            per_layer['c_w_o'][j] = _matmul(s['o'], dx1, 'tn', f"c_out_dw_{i}", out_dtype=BF16)
            do = _matmul(dx1, w['c_w_o'][j], 'nt', f"c_out_dx_{i}")
            dlt, do_b = _mla_delta(do, s['o'], f"c_attn_delta_{i}")
            blk = min(MLA_BLOCK, do.shape[0])
            as_rows = lambda col: col.reshape(C_HEADS, do.shape[0] // blk, 1, blk)
            dq, dk, dv = _mla_bwd(s['q'], s['k'], s['v'], do_b, as_rows(s['lse']), as_rows(dlt), f"c_attn_bwd_{i}")
            rider, units = ready()
            (dqw, dkvw, dkrope, per_layer['c_q_norm'][j], per_layer['c_k_norm'][j]), parts = _mla_qk_bwd(
                s['qw'], s['kvw'], s['down'], dq, dk, dv, rep['c_q_norm'][j:j + 1], rep['c_k_norm'][j:j + 1], rows_c,
                f"c_prep_bwd_{i}", rider=rider)
            received.update(zip(units, parts or ()))
            per_layer['c_w_q_up'][j] = _matmul(s['cq'], dqw, 'tn', f"c_q_up_dw_{i}", out_dtype=BF16)
            dcq = _matmul(dqw, w['c_w_q_up'][j], 'nt', f"c_q_up_dx_{i}")
            per_layer['c_w_kv_up'][j] = _matmul(s['ckv'], dkvw, 'tn', f"c_kv_up_dw_{i}", out_dtype=BF16)
            dckv = _matmul(dkvw, w['c_w_kv_up'][j], 'nt', f"c_kv_up_dx_{i}")
            ddown, per_layer['c_q_a_norm'][j], per_layer['c_kv_a_norm'][j] = _mla_latent_bwd(
                s['down'], dcq, dckv, dkrope, w['c_q_a_norm'][j], w['c_kv_a_norm'][j], f"c_latent_bwd_{i}")
            per_layer['c_w_down'][j] = _matmul(s['h1'], ddown, 'tn', f"c_down_dw_{i}", out_dtype=BF16)
            dh1 = _matmul(ddown, w['c_w_down'][j], 'nt', f"c_down_dx_{i}")
        dx, per_layer['mix_norm'][i] = _rmsnorm_bwd(s['x'], rep['mix_norm'][i:i + 1], dh1, dx1, f"mix_norm_bwd_{i}")

    last, units = ready()
    received.update(zip(units, _exchange_now(last, "scatter_last_gradients")))
    parts = {n: jnp.concatenate([received[(n, j)] for j in sorted(per_layer[n])], axis=0) for n in SHARDED}
    small = {}
    for n in REPLICATED:
        stacked = jnp.stack([per_layer[n][j] for j in sorted(per_layer[n])])
        small[n] = stacked.reshape(stacked.shape[0], stacked.shape[-1])
    return loss, dx, parts, small


def kernel(x, positions, mix_norm, ffn_norm, a_w_qkv, a_q_norm, a_k_norm, a_sinks, a_w_o, b_w_in, b_conv_w, b_w_out, c_w_down, c_q_a_norm, c_kv_a_norm, c_w_q_up, c_w_kv_up, c_q_norm, c_k_norm, c_w_o, f_w_gate_up, f_w_down, loss_target, m_mix_norm, m_ffn_norm, m_a_w_qkv, m_a_q_norm, m_a_k_norm, m_a_sinks, m_a_w_o, m_b_w_in, m_b_conv_w, m_b_w_out, m_c_w_down, m_c_q_a_norm, m_c_kv_a_norm, m_c_w_q_up, m_c_w_kv_up, m_c_q_norm, m_c_k_norm, m_c_w_o, m_f_w_gate_up, m_f_w_down, v_mix_norm, v_ffn_norm, v_a_w_qkv, v_a_q_norm, v_a_k_norm, v_a_sinks, v_a_w_o, v_b_w_in, v_b_conv_w, v_b_w_out, v_c_w_down, v_c_q_a_norm, v_c_kv_a_norm, v_c_w_q_up, v_c_w_kv_up, v_c_q_norm, v_c_k_norm, v_c_w_o, v_f_w_gate_up, v_f_w_down):
    local = dict(mix_norm=mix_norm, ffn_norm=ffn_norm, a_w_qkv=a_w_qkv, a_q_norm=a_q_norm, a_k_norm=a_k_norm, a_sinks=a_sinks, a_w_o=a_w_o, b_w_in=b_w_in, b_conv_w=b_conv_w, b_w_out=b_w_out, c_w_down=c_w_down, c_q_a_norm=c_q_a_norm, c_kv_a_norm=c_kv_a_norm, c_w_q_up=c_w_q_up, c_w_kv_up=c_w_kv_up, c_q_norm=c_q_norm, c_k_norm=c_k_norm, c_w_o=c_w_o, f_w_gate_up=f_w_gate_up, f_w_down=f_w_down)
    mom1 = dict(mix_norm=m_mix_norm, ffn_norm=m_ffn_norm, a_w_qkv=m_a_w_qkv, a_q_norm=m_a_q_norm, a_k_norm=m_a_k_norm, a_sinks=m_a_sinks, a_w_o=m_a_w_o, b_w_in=m_b_w_in, b_conv_w=m_b_conv_w, b_w_out=m_b_w_out, c_w_down=m_c_w_down, c_q_a_norm=m_c_q_a_norm, c_kv_a_norm=m_c_kv_a_norm, c_w_q_up=m_c_w_q_up, c_w_kv_up=m_c_w_kv_up, c_q_norm=m_c_q_norm, c_k_norm=m_c_k_norm, c_w_o=m_c_w_o, f_w_gate_up=m_f_w_gate_up, f_w_down=m_f_w_down)
    mom2 = dict(mix_norm=v_mix_norm, ffn_norm=v_ffn_norm, a_w_qkv=v_a_w_qkv, a_q_norm=v_a_q_norm, a_k_norm=v_a_k_norm, a_sinks=v_a_sinks, a_w_o=v_a_w_o, b_w_in=v_b_w_in, b_conv_w=v_b_conv_w, b_w_out=v_b_w_out, c_w_down=v_c_w_down, c_q_a_norm=v_c_q_a_norm, c_kv_a_norm=v_c_kv_a_norm, c_w_q_up=v_c_w_q_up, c_w_kv_up=v_c_w_kv_up, c_q_norm=v_c_q_norm, c_k_norm=v_c_k_norm, c_w_o=v_c_w_o, f_w_gate_up=v_f_w_gate_up, f_w_down=v_f_w_down)
    t, d = x.shape[1], x.shape[2]

    rep = {n: local[n] for n in REPLICATED}
    loss, grad_x, parts, small = _forward_backward(x.reshape(t, d), positions.reshape(t), loss_target.reshape(t, d),
                                                   {n: local[n] for n in SHARDED}, rep)

    out_g, out_d, out_m, out_v = {}, {}, {}, {}

    def update(names, parts):
        for n, part in zip(names, parts):
            shape = local[n].shape if n in SHARD_AXIS else (1,) + local[n].shape
            view = _as_rows(shape)
            results = _adamw(part.reshape(view[0], N_DEV, view[1], view[2]),
                             *[src[n].reshape(view) for src in (local, mom1, mom2)], name="adamw_" + n)
            for dst, res in zip((out_g, out_d, out_m, out_v), results):
                dst[n] = res.reshape(local[n].shape)

    update(SHARDED, [parts[n] for n in SHARDED])
    update(REPLICATED, _exchange_now(_Exchange([small[n].reshape((1,) + small[n].shape) for n in REPLICATED],
                                               scatter=False), "gather_small_gradients"))

    loss = lax.psum(loss.reshape(()), MESH_AXES)
    outs = [loss, grad_x.reshape(1, t, d)]
    for res in (out_g, out_d, out_m, out_v):
        outs += [res[n] for n in WEIGHTS]
    return tuple(outs)
```
